```python
import math
import jax, jax.numpy as jnp
from jax import lax
import numpy as np

D_MODEL = 2048
BATCH = 4
SEQ = 2048
DEPTH = 1
DEC_BATCH = 128
DEC_SEQ = 1
PAST_LEN = 16384
PAGE_SIZE = 128

D_MIX = D_MODEL
D_S5 = D_MIX // 2
D_M2 = D_MIX - D_S5
S5_CH = 16
S5_GROUPS = D_S5 // S5_CH
S5_N = 64
M2_HEADDIM = 64
M2_HEADS = D_M2 // M2_HEADDIM
M2_NGROUPS = 2
M2_DSTATE = 128
M2_CONV = 4
M2_CHUNK = 128
M2_CONV_DIM = D_M2 + 2 * M2_NGROUPS * M2_DSTATE
D_IN_PROJ = D_S5 + D_M2 + M2_CONV_DIM + M2_HEADS
MEM_LEN = 256
XA_HEADS = 4
XA_HEADDIM = D_MODEL // XA_HEADS
MOE_GROUPS = 4
MOE_PER_GROUP = 8
MOE_EXPERTS = MOE_GROUPS * MOE_PER_GROUP
MOE_TOPK = 2
MOE_FF = D_MODEL // 8
RMS_EPS = 1e-6

kernel_name = 'hymba_s5_ssd_hmoe_decode_step'


def rmsnorm(x, w):
    xf = x.astype(jnp.float32)
    y = xf * lax.rsqrt(jnp.mean(xf * xf, axis=-1, keepdims=True) + RMS_EPS)
    return (y * w.astype(jnp.float32)).astype(x.dtype)


def cplx_affine_combine(e1, e2):
    a1r, a1i, b1r, b1i = e1
    a2r, a2i, b2r, b2i = e2
    return (a2r * a1r - a2i * a1i,
            a2r * a1i + a2i * a1r,
            a2r * b1r - a2i * b1i + b2r,
            a2r * b1i + a2i * b1r + b2i)


def s5_branch(u, h0_re, h0_im, p):
    f32 = jnp.float32
    bsz, seq, _ = u.shape
    uf = u.astype(f32).reshape(bsz, seq, S5_GROUPS, S5_CH)
    lam_re = p['s5_a_re'].astype(f32)
    lam_im = p['s5_a_im'].astype(f32)
    step = jnp.exp(p['s5_log_dt'].astype(f32))[:, None]
    mag = jnp.exp(lam_re * step)
    ab_re = mag * jnp.cos(lam_im * step)
    ab_im = mag * jnp.sin(lam_im * step)
    den = lam_re * lam_re + lam_im * lam_im
    num_re = ab_re - 1.0
    coef_re = ((num_re * lam_re + ab_im * lam_im) / den)[..., None]
    coef_im = ((ab_im * lam_re - num_re * lam_im) / den)[..., None]
    b_re = p['s5_b_re'].astype(f32)
    b_im = p['s5_b_im'].astype(f32)
    bb_re = coef_re * b_re - coef_im * b_im
    bb_im = coef_re * b_im + coef_im * b_re
    bu_re = jnp.einsum('blgc,gnc->blgn', uf, bb_re)
    bu_im = jnp.einsum('blgc,gnc->blgn', uf, bb_im)
    a_re = jnp.broadcast_to(ab_re, bu_re.shape)
    a_im = jnp.broadcast_to(ab_im, bu_re.shape)
    pw_re, pw_im, h_re, h_im = lax.associative_scan(
        cplx_affine_combine, (a_re, a_im, bu_re, bu_im), axis=1)
    if h0_re is not None:
        s_re = h0_re.astype(f32)[:, None]
        s_im = h0_im.astype(f32)[:, None]
        h_re, h_im = (h_re + pw_re * s_re - pw_im * s_im,
                      h_im + pw_re * s_im + pw_im * s_re)
    c_re = p['s5_c_re'].astype(f32)
    c_im = p['s5_c_im'].astype(f32)
    y = jnp.einsum('gcn,blgn->blgc', c_re, h_re) - jnp.einsum('gcn,blgn->blgc', c_im, h_im)
    y = y.reshape(bsz, seq, D_S5) + p['s5_d'].astype(f32) * uf.reshape(bsz, seq, D_S5)
    g = jax.nn.gelu(y)
    out = g * jax.nn.sigmoid(g @ p['s5_glu_w'].astype(f32) + p['s5_glu_b'].astype(f32))
    out = rmsnorm(out, p['s5_norm_w'])
    return out.astype(u.dtype), h_re[:, -1], h_im[:, -1]


def causal_dwconv(xpad, w, b):
    y = lax.conv_general_dilated(
        xpad, w[:, None, :].astype(xpad.dtype), window_strides=(1,), padding='VALID',
        dimension_numbers=('NWC', 'WIO', 'NWC'), feature_group_count=xpad.shape[-1])
    return y + b.astype(y.dtype)


def ssd_chunked(x, dt, a, bmat, cmat):
    bsz, seq, nh, hp = x.shape
    q = M2_CHUNK
    nc = seq // q
    xdt = (x * dt[..., None]).reshape(bsz, nc, q, nh, hp)
    bc = bmat.reshape(bsz, nc, q, nh, M2_DSTATE)
    cc = cmat.reshape(bsz, nc, q, nh, M2_DSTATE)
    acum = jnp.cumsum((dt * a).reshape(bsz, nc, q, nh), axis=2)
    causal = jnp.tril(jnp.ones((q, q), dtype=bool))[None, None, :, :, None]
    seg = acum[:, :, :, None, :] - acum[:, :, None, :, :]
    lmat = jnp.exp(jnp.where(causal, seg, -jnp.inf))
    scores = jnp.einsum('bclhn,bcshn->bclsh', cc, bc) * lmat
    y_diag = jnp.einsum('bclsh,bcshp->bclhp', scores, xdt)
    decay_to_end = jnp.exp(acum[:, :, -1:, :] - acum)
    chunk_states = jnp.einsum('bcshn,bcsh,bcshp->bchpn', bc, decay_to_end, xdt)
    chunk_decay = jnp.exp(acum[:, :, -1, :])

    def pass_state(carry, inp):
        st, dec = inp
        return carry * dec[..., None, None] + st, carry

    init = jnp.zeros((bsz, nh, hp, M2_DSTATE), jnp.float32)
    final, prev = lax.scan(pass_state, init,
                           (jnp.swapaxes(chunk_states, 0, 1), jnp.swapaxes(chunk_decay, 0, 1)))
    prev = jnp.swapaxes(prev, 0, 1)
    y_off = jnp.einsum('bclhn,bchpn,bclh->bclhp', cc, prev, jnp.exp(acum))
    return (y_diag + y_off).reshape(bsz, seq, nh, hp), final


def ssd_recurrent(x, dt, a, bmat, cmat, h0):
    def step(h, inp):
        xt, dtt, bt, ct = inp
        h = h * jnp.exp(dtt * a)[..., None, None] + jnp.einsum('bhp,bhn->bhpn', xt * dtt[..., None], bt)
        return h, jnp.einsum('bhpn,bhn->bhp', h, ct)

    hT, ys = lax.scan(step, h0, (jnp.swapaxes(x, 0, 1), jnp.swapaxes(dt, 0, 1),
                                 jnp.swapaxes(bmat, 0, 1), jnp.swapaxes(cmat, 0, 1)))
    return jnp.swapaxes(ys, 0, 1), hT


def hybrid_mixer(xn, p, s5_state, ssm_state, conv_buf):
    f32 = jnp.float32
    bsz, seq, _ = xn.shape
    h = xn @ p['w_in']
    o1 = D_S5
    o2 = o1 + D_M2
    o3 = o2 + M2_CONV_DIM
    u, z, xbc, dt = h[..., :o1], h[..., o1:o2], h[..., o2:o3], h[..., o3:]
    if s5_state is None:
        s5_out, s5_re, s5_im = s5_branch(u, None, None, p)
    else:
        s5_out, s5_re, s5_im = s5_branch(u, s5_state[0], s5_state[1], p)
    if conv_buf is None:
        conv_buf = jnp.zeros((bsz, M2_CONV - 1, M2_CONV_DIM), xbc.dtype)
    xpad = jnp.concatenate([conv_buf.astype(xbc.dtype), xbc], axis=1)
    new_conv = xpad[:, -(M2_CONV - 1):]
    xbc = jax.nn.silu(causal_dwconv(xpad, p['m2_conv_w'], p['m2_conv_b']).astype(f32))
    nb = M2_NGROUPS * M2_DSTATE
    rep = M2_HEADS // M2_NGROUPS
    xs = xbc[..., :D_M2].reshape(bsz, seq, M2_HEADS, M2_HEADDIM)
    bmat = jnp.repeat(xbc[..., D_M2:D_M2 + nb].reshape(bsz, seq, M2_NGROUPS, M2_DSTATE), rep, axis=2)
    cmat = jnp.repeat(xbc[..., D_M2 + nb:].reshape(bsz, seq, M2_NGROUPS, M2_DSTATE), rep, axis=2)
    dtf = jax.nn.softplus(dt.astype(f32) + p['m2_dt_bias'].astype(f32))
    a = -jnp.exp(p['m2_a_log'].astype(f32))
    if ssm_state is None:
        y, ssm = ssd_chunked(xs, dtf, a, bmat, cmat)
    else:
        y, ssm = ssd_recurrent(xs, dtf, a, bmat, cmat, ssm_state.astype(f32))
    y = (y + p['m2_d'].astype(f32)[:, None] * xs).reshape(bsz, seq, D_M2)
    m2_out = rmsnorm(y * jax.nn.silu(z.astype(f32)), p['m2_norm_w']).astype(xn.dtype)
    out = jnp.concatenate([s5_out, m2_out], axis=-1) @ p['w_out']
    return out, s5_re, s5_im, ssm, new_conv


def memory_kv(mem, p):
    bsz = mem.shape[0]
    mn = rmsnorm(mem, p['norm_mem_w'])
    k = (mn @ p['xa_wk']).reshape(bsz, MEM_LEN, XA_HEADS, XA_HEADDIM)
    v = (mn @ p['xa_wv']).reshape(bsz, MEM_LEN, XA_HEADS, XA_HEADDIM)
    return k, v


def memory_cross_attention(xn, k, v, p):
    bsz, seq, _ = xn.shape
    q = (xn @ p['xa_wq']).reshape(bsz, seq, XA_HEADS, XA_HEADDIM)
    s = jnp.einsum('blhd,bmhd->bhlm', q.astype(jnp.float32), k.astype(jnp.float32)) * (XA_HEADDIM ** -0.5)
    att = jax.nn.softmax(s, axis=-1)
    o = jnp.einsum('bhlm,bmhd->blhd', att, v.astype(jnp.float32)).astype(xn.dtype)
    return o.reshape(bsz, seq, D_MODEL) @ p['xa_wo']


def hier_moe(xn, p):
    f32 = jnp.float32
    shp = xn.shape
    t = xn.reshape(-1, D_MODEL)
    tf = t.astype(f32)
    pc = jax.nn.softmax(tf @ p['router_coarse_w'].astype(f32) + p['router_coarse_b'].astype(f32), axis=-1)
    gsel = jnp.argmax(pc, axis=-1)
    gate_c = jnp.take_along_axis(pc, gsel[:, None], axis=-1)
    lf = (tf @ p['router_fine_w'].astype(f32) + p['router_fine_b'].astype(f32)).reshape(-1, MOE_GROUPS, MOE_PER_GROUP)
    lf_g = jnp.take_along_axis(lf, gsel[:, None, None], axis=1)[:, 0]
    top_l, top_i = lax.top_k(lf_g, MOE_TOPK)
    gate = jax.nn.softmax(top_l, axis=-1) * gate_c
    eidx = gsel[:, None] * MOE_PER_GROUP + top_i
    gates = jnp.sum(jax.nn.one_hot(eidx, MOE_EXPERTS, dtype=f32) * gate[..., None], axis=1)
    hg = jnp.einsum('td,edf->tef', t, p['moe_w_gate'])
    hu = jnp.einsum('td,edf->tef', t, p['moe_w_up'])
    hmid = jax.nn.silu(hg) * hu * gates[..., None].astype(hg.dtype)
    out = jnp.einsum('tef,efd->td', hmid, p['moe_w_down'])
    return out.reshape(shp).astype(xn.dtype)


def decoder_layer(x, mem_k, mem_v, p, s5_state, ssm_state, conv_buf):
    mix, s5_re, s5_im, ssm, conv = hybrid_mixer(rmsnorm(x, p['norm_mix_w']), p, s5_state, ssm_state, conv_buf)
    x = x + mix
    x = x + memory_cross_attention(rmsnorm(x, p['norm_xa_w']), mem_k, mem_v, p)
    x = x + hier_moe(rmsnorm(x, p['norm_ffn_w']), p)
    return x, s5_re, s5_im, ssm, conv


def setup_inputs(seed: int = 0) -> dict:
    key = jax.random.key(seed)
    ks = iter(jax.random.split(key, 64))
    f32 = jnp.float32

    def nrm(shape, scale=1.0):
        return scale * jax.random.normal(next(ks), shape, f32)

    def gain(shape):
        return 1.0 + 0.01 * nrm(shape)

    def unif(shape, lo, hi):
        return jax.random.uniform(next(ks), shape, f32, lo, hi)

    L = (DEPTH,)
    dt0 = jnp.exp(unif(L + (M2_HEADS,), math.log(1e-3), math.log(1e-1)))
    return {
        'x_prompt': nrm((BATCH, SEQ, D_MODEL)),
        'x_sample': nrm((DEC_BATCH, DEC_SEQ, D_MODEL)),
        'mem_prompt': nrm((BATCH, MEM_LEN, D_MODEL)),
        'state_s5_re': nrm(L + (DEC_BATCH, S5_GROUPS, S5_N), 0.1),
        'state_s5_im': nrm(L + (DEC_BATCH, S5_GROUPS, S5_N), 0.1),
        'state_ssm': nrm(L + (DEC_BATCH, M2_HEADS, M2_HEADDIM, M2_DSTATE), 0.1),
        'state_conv': nrm(L + (DEC_BATCH, M2_CONV - 1, M2_CONV_DIM)),
        'cache_mem_k': nrm(L + (DEC_BATCH, MEM_LEN, XA_HEADS, XA_HEADDIM)),
        'cache_mem_v': nrm(L + (DEC_BATCH, MEM_LEN, XA_HEADS, XA_HEADDIM)),
        'norm_mix_w': gain(L + (D_MODEL,)),
        'w_in': nrm(L + (D_MODEL, D_IN_PROJ), D_MODEL ** -0.5),
        's5_a_re': -0.5 + 0.01 * nrm(L + (S5_GROUPS, S5_N)),
        's5_a_im': jnp.pi * jnp.arange(S5_N, dtype=f32) + 0.01 * nrm(L + (S5_GROUPS, S5_N)),
        's5_log_dt': unif(L + (S5_GROUPS,), math.log(1e-3), math.log(1e-1)),
        's5_b_re': nrm(L + (S5_GROUPS, S5_N, S5_CH), (2 * S5_CH) ** -0.5),
        's5_b_im': nrm(L + (S5_GROUPS, S5_N, S5_CH), (2 * S5_CH) ** -0.5),
        's5_c_re': nrm(L + (S5_GROUPS, S5_CH, S5_N), S5_N ** -0.5),
        's5_c_im': nrm(L + (S5_GROUPS, S5_CH, S5_N), S5_N ** -0.5),
        's5_d': nrm(L + (D_S5,)),
        's5_glu_w': nrm(L + (D_S5, D_S5), D_S5 ** -0.5),
        's5_glu_b': nrm(L + (D_S5,), 0.01),
        's5_norm_w': gain(L + (D_S5,)),
        'm2_conv_w': nrm(L + (M2_CONV, M2_CONV_DIM), M2_CONV ** -0.5),
        'm2_conv_b': nrm(L + (M2_CONV_DIM,), 0.01),
        'm2_dt_bias': dt0 + jnp.log(-jnp.expm1(-dt0)),
        'm2_a_log': jnp.log(unif(L + (M2_HEADS,), 1.0, 16.0)),
        'm2_d': gain(L + (M2_HEADS,)),
        'm2_norm_w': gain(L + (D_M2,)),
        'w_out': nrm(L + (D_MIX, D_MODEL), D_MIX ** -0.5),
        'norm_xa_w': gain(L + (D_MODEL,)),
        'norm_mem_w': gain(L + (D_MODEL,)),
        'xa_wq': nrm(L + (D_MODEL, D_MODEL), D_MODEL ** -0.5),
        'xa_wk': nrm(L + (D_MODEL, D_MODEL), D_MODEL ** -0.5),
        'xa_wv': nrm(L + (D_MODEL, D_MODEL), D_MODEL ** -0.5),
        'xa_wo': nrm(L + (D_MODEL, D_MODEL), D_MODEL ** -0.5),
        'norm_ffn_w': gain(L + (D_MODEL,)),
        'router_coarse_w': nrm(L + (D_MODEL, MOE_GROUPS), D_MODEL ** -0.5),
        'router_coarse_b': nrm(L + (MOE_GROUPS,), 0.01),
        'router_fine_w': nrm(L + (D_MODEL, MOE_EXPERTS), D_MODEL ** -0.5),
        'router_fine_b': nrm(L + (MOE_EXPERTS,), 0.01),
        'moe_w_gate': nrm(L + (MOE_EXPERTS, D_MODEL, MOE_FF), D_MODEL ** -0.5),
        'moe_w_up': nrm(L + (MOE_EXPERTS, D_MODEL, MOE_FF), D_MODEL ** -0.5),
        'moe_w_down': nrm(L + (MOE_EXPERTS, MOE_FF, D_MODEL), MOE_FF ** -0.5),
        'norm_final_w': gain((D_MODEL,)),
    }


def reference(x_prompt, x_sample, mem_prompt, state_s5_re, state_s5_im, state_ssm, state_conv,
              cache_mem_k, cache_mem_v, norm_mix_w, w_in, s5_a_re, s5_a_im, s5_log_dt, s5_b_re, s5_b_im,
              s5_c_re, s5_c_im, s5_d, s5_glu_w, s5_glu_b, s5_norm_w, m2_conv_w, m2_conv_b, m2_dt_bias,
              m2_a_log, m2_d, m2_norm_w, w_out, norm_xa_w, norm_mem_w, xa_wq, xa_wk, xa_wv, xa_wo,
              norm_ffn_w, router_coarse_w, router_coarse_b, router_fine_w, router_fine_b,
              moe_w_gate, moe_w_up, moe_w_down, norm_final_w):
    yp, ys = x_prompt, x_sample
    lp = ([], [], [], [], [], [])
    ls = ([], [], [], [])
    for l in range(DEPTH):
        p = {
            'norm_mix_w': norm_mix_w[l], 'w_in': w_in[l],
            's5_a_re': s5_a_re[l], 's5_a_im': s5_a_im[l], 's5_log_dt': s5_log_dt[l],
            's5_b_re': s5_b_re[l], 's5_b_im': s5_b_im[l], 's5_c_re': s5_c_re[l], 's5_c_im': s5_c_im[l],
            's5_d': s5_d[l], 's5_glu_w': s5_glu_w[l], 's5_glu_b': s5_glu_b[l], 's5_norm_w': s5_norm_w[l],
            'm2_conv_w': m2_conv_w[l], 'm2_conv_b': m2_conv_b[l], 'm2_dt_bias': m2_dt_bias[l],
            'm2_a_log': m2_a_log[l], 'm2_d': m2_d[l], 'm2_norm_w': m2_norm_w[l], 'w_out': w_out[l],
            'norm_xa_w': norm_xa_w[l], 'norm_mem_w': norm_mem_w[l], 'xa_wq': xa_wq[l], 'xa_wk': xa_wk[l],
            'xa_wv': xa_wv[l], 'xa_wo': xa_wo[l], 'norm_ffn_w': norm_ffn_w[l],
            'router_coarse_w': router_coarse_w[l], 'router_coarse_b': router_coarse_b[l],
            'router_fine_w': router_fine_w[l], 'router_fine_b': router_fine_b[l],
            'moe_w_gate': moe_w_gate[l], 'moe_w_up': moe_w_up[l], 'moe_w_down': moe_w_down[l],
        }
        mk, mv = memory_kv(mem_prompt, p)
        yp, a1, a2, a3, a4 = decoder_layer(yp, mk, mv, p, None, None, None)
        for lst, arr in zip(lp, (a1, a2, a3, a4, mk, mv)):
            lst.append(arr)
        ys, b1, b2, b3, b4 = decoder_layer(ys, cache_mem_k[l], cache_mem_v[l], p,
                                           (state_s5_re[l], state_s5_im[l]), state_ssm[l], state_conv[l])
        for lst, arr in zip(ls, (b1, b2, b3, b4)):
            lst.append(arr)
    y_prompt = rmsnorm(yp, norm_final_w)
    y_sample = rmsnorm(ys, norm_final_w)
    p_s5_re = jnp.stack(lp[0])
    p_s5_im = jnp.stack(lp[1])
    p_ssm = jnp.stack(lp[2])
    p_conv = jnp.stack(lp[3])
    p_mem_k = jnp.stack(lp[4])
    p_mem_v = jnp.stack(lp[5])
    s_s5_re = jnp.stack(ls[0])
    s_s5_im = jnp.stack(ls[1])
    s_ssm = jnp.stack(ls[2])
    s_conv = jnp.stack(ls[3])
    return (y_prompt, y_sample, p_s5_re, p_s5_im, p_ssm, p_conv, p_mem_k, p_mem_v, s_s5_re, s_s5_im, s_ssm, s_conv)
```

```python
import functools
import math

import jax
import jax.numpy as jnp
from jax import lax
from jax.experimental import pallas as pl
from jax.experimental.pallas import tpu as pltpu

F32 = jnp.float32
BF16 = jnp.bfloat16
RMS_EPS = 1e-6

V7X_VMEM_BYTES = 64 * 1024 * 1024
VMEM_LIMIT = V7X_VMEM_BYTES - 8 * 1024 * 1024
LANES = 128

S5_CH = 16
S5_N = 64
S5_Q = 16
S5_GB = 8
M2_HEADDIM = 64
M2_DSTATE = 128
M2_NGROUPS = 2
M2_CONV = 4
M2_CHUNK = 128
MOE_GROUPS = 4
MOE_PER_GROUP = 8


def _cparams(sem):
    return pltpu.CompilerParams(dimension_semantics=sem, vmem_limit_bytes=VMEM_LIMIT)


def _rms(x, w):
    return x * lax.rsqrt(jnp.mean(x * x, axis=-1, keepdims=True) + RMS_EPS) * w


def _sigmoid(x):
    return 1.0 / (1.0 + jnp.exp(-x))


def _silu(x):
    return x * _sigmoid(x)


def _softplus(x):
    return jnp.maximum(x, 0.0) + jnp.log1p(jnp.exp(-jnp.abs(x)))


def _gelu_tanh(x):
    return 0.5 * x * (1.0 + jnp.tanh(math.sqrt(2.0 / math.pi) * (x + 0.044715 * (x * x * x))))


def _dot(a, b):
    return jnp.dot(a, b, preferred_element_type=F32)


def _dot_nt(a, b):
    return lax.dot_general(a, b, (((1,), (1,)), ((), ())), preferred_element_type=F32)


def _split3(x):
    hi = x.astype(BF16)
    r = x - hi.astype(F32)
    mid = r.astype(BF16)
    lo = (r - mid.astype(F32)).astype(BF16)
    return hi, mid, lo


def _split2(x):
    hi = x.astype(BF16)
    lo = (x - hi.astype(F32)).astype(BF16)
    return hi, lo


def _mm_kernel(*refs, n_lhs, has_gain, has_res, has_side):
    it = iter(refs)
    lhs = [next(it) for _ in range(n_lhs)]
    gain = next(it) if has_gain else None
    ws = [next(it) for _ in range(n_lhs)]
    side_w = next(it) if has_side else None
    res = next(it) if has_res else None
    out = next(it)
    side_out = next(it) if has_side else None
    lhs_bf = next(it)

    @pl.when(pl.program_id(1) == 0)
    def _():
        for i in range(n_lhs):
            x = lhs[i][...]
            if has_gain:
                x = _rms(x, gain[...])
            lhs_bf[i] = x.astype(BF16)
        if has_side:
            side_out[...] = _dot(lhs_bf[0], side_w[...].astype(BF16))

    acc = None
    for i in range(n_lhs):
        p = _dot(lhs_bf[i], ws[i][...].astype(BF16))
        acc = p if acc is None else acc + p
    if has_res:
        acc = acc + res[...]
    out[...] = acc


def fused_matmul(lhs_list, w, *, n_out, gain=None, res=None, side_w=None, tm, tn):
    n_lhs = len(lhs_list)
    m, kp = lhs_list[0].shape
    assert all(a.shape == (m, kp) for a in lhs_list)
    assert w.shape[0] == n_lhs * kp and m % tm == 0 and n_out % tn == 0
    assert gain is None or n_lhs == 1
    grid = (m // tm, n_out // tn)
    in_specs = [pl.BlockSpec((tm, kp), lambda i, j: (i, 0)) for _ in range(n_lhs)]
    args = list(lhs_list)
    if gain is not None:
        in_specs.append(pl.BlockSpec((1, kp), lambda i, j: (0, 0)))
        args.append(gain.reshape(1, kp))
    for p in range(n_lhs):
        in_specs.append(pl.BlockSpec((kp, tn), lambda i, j, p=p: (p, j)))
        args.append(w)
    if side_w is not None:
        in_specs.append(pl.BlockSpec((kp, LANES), lambda i, j: (0, 0)))
        args.append(side_w)
    if res is not None:
        in_specs.append(pl.BlockSpec((tm, tn), lambda i, j: (i, j)))
        args.append(res)
    out_shape = [jax.ShapeDtypeStruct((m, n_out), F32)]
    out_specs = [pl.BlockSpec((tm, tn), lambda i, j: (i, j))]
    if side_w is not None:
        out_shape.append(jax.ShapeDtypeStruct((m, LANES), F32))
        out_specs.append(pl.BlockSpec((tm, LANES), lambda i, j: (i, 0)))
    outs = pl.pallas_call(
        functools.partial(_mm_kernel, n_lhs=n_lhs, has_gain=gain is not None,
                          has_res=res is not None, has_side=side_w is not None),
        grid=grid, in_specs=in_specs, out_specs=out_specs, out_shape=out_shape,
        scratch_shapes=[pltpu.VMEM((n_lhs, tm, kp), BF16)],
        compiler_params=_cparams(("parallel", "arbitrary")),
        name="fused_matmul",
    )(*args)
    return outs if side_w is not None else outs[0]


def _s5_prep_kernel(lre_ref, lim_ref, ldt_ref, btre_ref, btim_ref, cre_ref, cim_ref,
                    tz_ref, wsre_ref, wsim_ref, wcre_ref, wcim_ref, aq_ref, ab_ref, bbt_ref):
    q, ch = S5_Q, S5_CH
    lr = lre_ref[...]
    li = lim_ref[...]
    step = jnp.exp(ldt_ref[...])
    mag = jnp.exp(lr * step)
    ab_re = mag * jnp.cos(li * step)
    ab_im = mag * jnp.sin(li * step)
    den = lr * lr + li * li
    num_re = ab_re - 1.0
    coef_re = (num_re * lr + ab_im * li) / den
    coef_im = (ab_im * lr - num_re * li) / den
    bt_re = btre_ref[...]
    bt_im = btim_ref[...]
    bb_re = coef_re * bt_re - coef_im * bt_im
    bb_im = coef_re * bt_im + coef_im * bt_re
    c_re = cre_ref[...]
    c_im = cim_ref[...]

    pw = [(jnp.ones_like(ab_re), jnp.zeros_like(ab_re))]
    for _ in range(q):
        pr, pi = pw[-1]
        pw.append((pr * ab_re - pi * ab_im, pr * ab_im + pi * ab_re))

    ca_re = [c_re * pr - c_im * pi for pr, pi in pw]
    ca_im = [c_re * pi + c_im * pr for pr, pi in pw]
    wcre_ref[...] = jnp.concatenate(ca_re[1:], axis=0).astype(BF16)
    wcim_ref[...] = jnp.concatenate([-x for x in ca_im[1:]], axis=0).astype(BF16)

    pr_stack = jnp.concatenate(ca_re[:q], axis=0)
    pi_stack = jnp.concatenate(ca_im[:q], axis=0)
    krow = None
    for a, b, sign in ((bb_re, pr_stack, 1.0), (bb_im, pi_stack, -1.0)):
        a_hi, a_lo = _split2(a)
        b_hi, b_lo = _split2(b)
        t = _dot_nt(a_hi, b_hi) + (_dot_nt(a_hi, b_lo) + _dot_nt(a_lo, b_hi))
        krow = sign * t if krow is None else krow + sign * t
    lane = lax.broadcasted_iota(jnp.int32, krow.shape, 1)
    blocks = [krow]
    for s in range(1, q):
        blocks.append(jnp.where(lane >= s * ch, pltpu.roll(krow, s * ch, 1), 0.0))
    tz_ref[...] = jnp.concatenate(blocks, axis=0).astype(BF16)

    ws_re, ws_im = [], []
    for s in range(q):
        pr, pi = pw[q - 1 - s]
        ws_re.append(bb_re * pr - bb_im * pi)
        ws_im.append(bb_re * pi + bb_im * pr)
    wsre_ref[...] = jnp.concatenate(ws_re, axis=0).astype(BF16)
    wsim_ref[...] = jnp.concatenate(ws_im, axis=0).astype(BF16)

    aq_ref[0:1, :] = pw[q][0]
    aq_ref[1:2, :] = pw[q][1]
    ab_ref[0:1, :] = ab_re
    ab_ref[1:2, :] = ab_im
    bbt_ref[0:ch, :] = bb_re
    bbt_ref[ch:2 * ch, :] = bb_im


def s5_prepare(a_re, a_im, log_dt, b_re, b_im, c_re, c_im):
    g, n = a_re.shape
    ch, q = S5_CH, S5_Q
    qc = q * ch
    bt_re = jnp.swapaxes(b_re, 1, 2)
    bt_im = jnp.swapaxes(b_im, 1, 2)

    def per_g(*dims):
        return pl.BlockSpec((None,) + dims, lambda i: (i,) + (0,) * len(dims))

    return pl.pallas_call(
        _s5_prep_kernel,
        grid=(g,),
        in_specs=[per_g(1, n), per_g(1, n), per_g(1, 1), per_g(ch, n), per_g(ch, n), per_g(ch, n), per_g(ch, n)],
        out_specs=[per_g(qc, qc), per_g(qc, n), per_g(qc, n), per_g(qc, n), per_g(qc, n),
                   per_g(2, n), per_g(2, n), per_g(2 * ch, n)],
        out_shape=[jax.ShapeDtypeStruct((g, qc, qc), BF16),
                   jax.ShapeDtypeStruct((g, qc, n), BF16), jax.ShapeDtypeStruct((g, qc, n), BF16),
                   jax.ShapeDtypeStruct((g, qc, n), BF16), jax.ShapeDtypeStruct((g, qc, n), BF16),
                   jax.ShapeDtypeStruct((g, 2, n), F32), jax.ShapeDtypeStruct((g, 2, n), F32),
                   jax.ShapeDtypeStruct((g, 2 * ch, n), F32)],
        compiler_params=_cparams(("parallel",)),
        name="s5_prepare",
    )(a_re.reshape(g, 1, n), a_im.reshape(g, 1, n), log_dt.reshape(g, 1, 1), bt_re, bt_im, c_re, c_im)


def _s5_scan_kernel(u_ref, tz_ref, wsre_ref, wsim_ref, wcre_ref, wcim_ref, aq_ref,
                    y_ref, hfin_ref, sre_sc, sim_sc, *, nb, nchunk):
    gb = u_ref.shape[0]
    for g in range(gb):
        u = u_ref[g]
        y_ref[g] = _dot(u, tz_ref[g])
        sre_sc[g] = _dot(u, wsre_ref[g])
        sim_sc[g] = _dot(u, wsim_ref[g])

    ar = [jnp.broadcast_to(aq_ref[g, 0:1, :], (nb, S5_N)) for g in range(gb)]
    ai = [jnp.broadcast_to(aq_ref[g, 1:2, :], (nb, S5_N)) for g in range(gb)]

    def step(c, carry):
        rows = pl.ds(pl.multiple_of(c * nb, nb), nb)
        new = []
        for g in range(gb):
            hr, hi = carry[g]
            sr = sre_sc[g, rows, :]
            si = sim_sc[g, rows, :]
            sre_sc[g, rows, :] = hr
            sim_sc[g, rows, :] = hi
            new.append((ar[g] * hr - ai[g] * hi + sr, ar[g] * hi + ai[g] * hr + si))
        return tuple(new)

    zero = jnp.zeros((nb, S5_N), F32)
    fin = lax.fori_loop(0, nchunk, step, tuple((zero, zero) for _ in range(gb)))
    for g in range(gb):
        hfin_ref[g, 0] = fin[g][0]
        hfin_ref[g, 1] = fin[g][1]
        y_ref[g] += (_dot_nt(sre_sc[g].astype(BF16), wcre_ref[g])
                     + _dot_nt(sim_sc[g].astype(BF16), wcim_ref[g]))


def s5_scan(u_t, prep, nb):
    tz, ws_re, ws_im, wc_re, wc_im, aq = prep[:6]
    g, rows, qc = u_t.shape
    n = S5_N
    gb = S5_GB
    nchunk = rows // nb

    def blk(*dims):
        return pl.BlockSpec((gb,) + dims, lambda i: (i,) + (0,) * len(dims))

    return pl.pallas_call(
        functools.partial(_s5_scan_kernel, nb=nb, nchunk=nchunk),
        grid=(g // gb,),
        in_specs=[blk(rows, qc), blk(qc, qc), blk(qc, n), blk(qc, n), blk(qc, n), blk(qc, n), blk(2, n)],
        out_specs=[blk(rows, qc), blk(2, nb, n)],
        out_shape=[jax.ShapeDtypeStruct((g, rows, qc), F32), jax.ShapeDtypeStruct((g, 2, nb, n), F32)],
        scratch_shapes=[pltpu.VMEM((gb, rows, n), F32), pltpu.VMEM((gb, rows, n), F32)],
        compiler_params=_cparams(("parallel",)),
        name="s5_scan",
    )(u_t, tz, ws_re, ws_im, wc_re, wc_im, aq)


def _s5_step_kernel(u_ref, hre_ref, him_ref, ab_ref, bbt_ref, cre_ref, cim_ref, y_ref, ore_ref, oim_ref):
    gb = u_ref.shape[0]
    ch = S5_CH
    for g in range(gb):
        u = u_ref[g].astype(BF16)
        bb_re = bbt_ref[g, 0:ch, :].astype(BF16)
        bb_im = bbt_ref[g, ch:2 * ch, :].astype(BF16)
        ar = ab_ref[g, 0:1, :]
        ai = ab_ref[g, 1:2, :]
        hr0 = hre_ref[g]
        hi0 = him_ref[g]
        hr = _dot(u, bb_re) + (ar * hr0 - ai * hi0)
        hi = _dot(u, bb_im) + (ar * hi0 + ai * hr0)
        ore_ref[g] = hr
        oim_ref[g] = hi
        y_ref[g] = (_dot_nt(hr.astype(BF16), cre_ref[g].astype(BF16))
                    - _dot_nt(hi.astype(BF16), cim_ref[g].astype(BF16)))


def s5_step(u_t, h_re, h_im, prep, c_re, c_im):
    ab, bbt = prep[6], prep[7]
    g, b, ch = u_t.shape
    n = S5_N
    gb = S5_GB

    def blk(*dims):
        return pl.BlockSpec((gb,) + dims, lambda i: (i,) + (0,) * len(dims))

    return pl.pallas_call(
        _s5_step_kernel,
        grid=(g // gb,),
        in_specs=[blk(b, ch), blk(b, n), blk(b, n), blk(2, n), blk(2 * ch, n), blk(ch, n), blk(ch, n)],
        out_specs=[blk(b, ch), blk(b, n), blk(b, n)],
        out_shape=[jax.ShapeDtypeStruct((g, b, ch), F32), jax.ShapeDtypeStruct((g, b, n), F32),
                   jax.ShapeDtypeStruct((g, b, n), F32)],
        compiler_params=_cparams(("parallel",)),
        name="s5_step",
    )(u_t, h_re, h_im, ab, bbt, c_re, c_im)


def _s5_head_kernel(y_ref, u_ref, d_ref, w_ref, b_ref, nw_ref, o_ref):
    y = y_ref[...] + d_ref[...] * u_ref[...]
    g = _gelu_tanh(y)
    gate = _sigmoid(_dot(g.astype(BF16), w_ref[...].astype(BF16)) + b_ref[...])
    o_ref[...] = _rms(g * gate, nw_ref[...])


def s5_head(y, h, d, glu_w, glu_b, norm_w, *, tm):
    m, ds = y.shape
    row = lambda a: a.reshape(1, ds)
    vec = pl.BlockSpec((1, ds), lambda i: (0, 0))
    return pl.pallas_call(
        _s5_head_kernel,
        grid=(m // tm,),
        in_specs=[pl.BlockSpec((tm, ds), lambda i: (i, 0)), pl.BlockSpec((tm, ds), lambda i: (i, 0)), vec,
                  pl.BlockSpec((ds, ds), lambda i: (0, 0)), vec, vec],
        out_specs=pl.BlockSpec((tm, ds), lambda i: (i, 0)),
        out_shape=jax.ShapeDtypeStruct((m, ds), F32),
        compiler_params=_cparams(("parallel",)),
        name="s5_head",
    )(y, h, row(d), glu_w, row(glu_b), row(norm_w))


def _pair_select(first, col0, col1, shape):
    return jnp.where(first, jnp.broadcast_to(col0, shape), jnp.broadcast_to(col1, shape))


def _ssd_chunk_kernel(*refs, d_inner, nheads, n_xparts):
    xparts = refs[:n_xparts]
    (z_ref, dt_ref, cw_ref, cb_ref, dtb_ref, alog_ref, dvec_ref, nw_ref,
     out_ref, ssm_ref, conv_ref, state_sc, xpad_sc, y_sc) = refs[n_xparts:]
    c = pl.program_id(1)
    q = M2_CHUNK
    hp = M2_HEADDIM
    ns = M2_DSTATE
    heads_per_group = nheads // M2_NGROUPS
    halo = 8

    @pl.when(c == 0)
    def _():
        state_sc[...] = jnp.zeros_like(state_sc)
        xpad_sc[0:halo, :] = jnp.zeros((halo, xpad_sc.shape[1]), F32)

    wpart = xparts[0].shape[1]
    for i, xr in enumerate(xparts):
        xpad_sc[halo:halo + q, i * wpart:(i + 1) * wpart] = xr[...]
    cw = cw_ref[...]
    conv = cb_ref[...] + cw[M2_CONV - 1:M2_CONV, :] * xpad_sc[halo:halo + q, :]
    for k in range(1, M2_CONV):
        conv = conv + cw[M2_CONV - 1 - k:M2_CONV - k, :] * xpad_sc[halo - k:halo - k + q, :]
    xpad_sc[0:halo, :] = xpad_sc[q:q + halo, :]
    xc = _silu(conv)

    dt = _softplus(dt_ref[...] + dtb_ref[...])
    a = -jnp.exp(alog_ref[...])
    da = dt * a
    row = lax.broadcasted_iota(jnp.int32, (q, q), 0)
    col = lax.broadcasted_iota(jnp.int32, (q, q), 1)
    causal = row >= col
    tri = jnp.where(causal, 1.0, 0.0).astype(BF16)
    d_hi, d_mid, d_lo = _split3(da)
    acum = _dot(tri, d_hi) + (_dot(tri, d_mid) + _dot(tri, d_lo))
    acum_t = acum.T
    alast = acum[q - 1:q, :]
    first = col < hp
    first_rows = row < hp

    for pr in range(nheads // 2):
        grp = (2 * pr) // heads_per_group
        b_bf = xc[:, d_inner + grp * ns:d_inner + (grp + 1) * ns].astype(BF16)
        c_bf = xc[:, d_inner + (M2_NGROUPS + grp) * ns:d_inner + (M2_NGROUPS + grp + 1) * ns].astype(BF16)
        cb = _dot_nt(c_bf, b_bf)
        xpair = xc[:, pr * 2 * hp:(pr + 1) * 2 * hp]
        h0, h1 = 2 * pr, 2 * pr + 1
        acol = [acum[:, h:h + 1] for h in (h0, h1)]
        m = []
        for k, h in enumerate((h0, h1)):
            seg = jnp.broadcast_to(acol[k], (q, q)) - jnp.broadcast_to(acum_t[h:h + 1, :], (q, q))
            lmat = jnp.exp(jnp.where(causal, seg, -1e30))
            m.append((cb * lmat).astype(BF16))
        dtp = _pair_select(first, dt[:, h0:h0 + 1], dt[:, h1:h1 + 1], (q, q))
        xdt = xpair * dtp
        xdt_bf = xdt.astype(BF16)
        y_diag = jnp.where(first, _dot(m[0], xdt_bf), _dot(m[1], xdt_bf))
        dec_end = _pair_select(first, jnp.exp(alast[:, h0:h0 + 1] - acol[0]),
                               jnp.exp(alast[:, h1:h1 + 1] - acol[1]), (q, q))
        xw_t = (xdt * dec_end).T.astype(BF16)
        chunk_state = _dot(xw_t, b_bf)
        rows = pl.ds(pr * 2 * hp, 2 * hp)
        prev = state_sc[rows, :]
        y_off = _dot_nt(c_bf, prev.astype(BF16)) * _pair_select(first, jnp.exp(acol[0]), jnp.exp(acol[1]), (q, q))
        sdec = jnp.where(first_rows, jnp.broadcast_to(jnp.exp(alast[:, h0:h0 + 1]), (q, q)),
                         jnp.broadcast_to(jnp.exp(alast[:, h1:h1 + 1]), (q, q)))
        state_sc[rows, :] = prev * sdec + chunk_state
        y_sc[:, pr * 2 * hp:(pr + 1) * 2 * hp] = y_diag + y_off + dvec_ref[:, pr * 2 * hp:(pr + 1) * 2 * hp] * xpair

    out_ref[...] = _rms(y_sc[...] * _silu(z_ref[...]), nw_ref[...])

    @pl.when(c == pl.num_programs(1) - 1)
    def _():
        ssm_ref[...] = state_sc[...]
        conv_ref[...] = xpad_sc[halo + q - (M2_CONV - 1):halo + q, :]


def ssd_prompt(h, dt_raw, nb, seq, p, *, d_s5, d_inner, nheads):
    q = M2_CHUNK
    nc = seq // q
    conv_dim = d_inner + 2 * M2_NGROUPS * M2_DSTATE
    xw = 512
    xoff = d_s5 + d_inner
    assert d_s5 % d_inner == 0 and xoff % xw == 0 and conv_dim % xw == 0
    assert M2_CHUNK == 2 * M2_HEADDIM == M2_DSTATE == LANES
    zblk = d_s5 // d_inner
    n_xparts = conv_dim // xw
    m = nb * seq
    pad = lambda v: jnp.pad(v, (0, LANES - v.shape[0])).reshape(1, LANES)
    dvec = jnp.repeat(p['m2_d'], M2_HEADDIM).reshape(1, d_inner)
    vec = lambda n: pl.BlockSpec((1, n), lambda b, c: (0, 0))
    tok = lambda w, j: pl.BlockSpec((q, w), lambda b, c, j=j: (b * nc + c, j))
    out, ssm, conv = pl.pallas_call(
        functools.partial(_ssd_chunk_kernel, d_inner=d_inner, nheads=nheads, n_xparts=n_xparts),
        grid=(nb, nc),
        in_specs=[tok(xw, xoff // xw + i) for i in range(n_xparts)] + [tok(d_inner, zblk), tok(LANES, 0),
                  pl.BlockSpec((M2_CONV, conv_dim), lambda b, c: (0, 0)), vec(conv_dim), vec(LANES), vec(LANES),
                  vec(d_inner), vec(d_inner)],
        out_specs=[tok(d_inner, 0),
                   pl.BlockSpec((None, nheads * M2_HEADDIM, M2_DSTATE), lambda b, c: (b, 0, 0)),
                   pl.BlockSpec((None, M2_CONV - 1, conv_dim), lambda b, c: (b, 0, 0))],
        out_shape=[jax.ShapeDtypeStruct((m, d_inner), F32),
                   jax.ShapeDtypeStruct((nb, nheads * M2_HEADDIM, M2_DSTATE), F32),
                   jax.ShapeDtypeStruct((nb, M2_CONV - 1, conv_dim), F32)],
        scratch_shapes=[pltpu.VMEM((nheads * M2_HEADDIM, M2_DSTATE), F32),
                        pltpu.VMEM((q + 8, conv_dim), F32),
                        pltpu.VMEM((q, d_inner), F32)],
        compiler_params=_cparams(("parallel", "arbitrary")),
        name="ssd_chunk",
    )(*([h] * n_xparts), h, dt_raw, p['m2_conv_w'], p['m2_conv_b'].reshape(1, conv_dim), pad(p['m2_dt_bias']),
      pad(p['m2_a_log']), dvec, p['m2_norm_w'].reshape(1, d_inner))
    return out, ssm.reshape(nb, nheads, M2_HEADDIM, M2_DSTATE), conv


SSD_STEP_SEQS = 4


def _ssd_step_kernel(*refs, d_inner, nheads, n_xparts):
    xparts = refs[:n_xparts]
    (z_ref, dt_ref, cs0_ref, cs1_ref, cs2_ref, cw_ref, cb_ref, dtb_ref, alog_ref, dvec_ref, nw_ref, st_ref,
     out_ref, so_ref, lhs_sc, bfull_sc, ct_sc, yt_sc, xs_sc) = refs[n_xparts:]
    i = pl.program_id(0)
    nb = z_ref.shape[0]
    ns = M2_DSTATE
    rows_g = (nheads // M2_NGROUPS) * M2_HEADDIM

    @pl.when(i == 0)
    def _():
        cw = cw_ref[...]
        xbc = jnp.concatenate([xr[...] for xr in xparts], axis=1)
        conv = (cb_ref[...] + cw[3:4, :] * xbc + cw[2:3, :] * cs2_ref[...]
                + cw[1:2, :] * cs1_ref[...] + cw[0:1, :] * cs0_ref[...])
        xc = _silu(conv)
        dt = _softplus(dt_ref[...] + dtb_ref[...])
        dec = jnp.exp(dt * (-jnp.exp(alog_ref[...])))
        hrow = lax.broadcasted_iota(jnp.int32, (LANES, d_inner), 0)
        hcol = lax.broadcasted_iota(jnp.int32, (LANES, d_inner), 1)
        expand = jnp.where(hcol // M2_HEADDIM == hrow, 1.0, 0.0).astype(BF16)

        def expand_heads(v):
            a, b_, c = _split3(v)
            return _dot(a, expand) + (_dot(b_, expand) + _dot(c, expand))

        xs = xc[:, :d_inner]
        xs_sc[...] = xs
        xdt_t = (xs * expand_heads(dt)).T
        d_hi, d_mid, d_lo = _split3(expand_heads(dec).T)
        for g in range(M2_NGROUPS):
            r = slice(g * rows_g, (g + 1) * rows_g)
            lhs_sc[g] = jnp.concatenate([xdt_t[r].astype(BF16), d_hi[r], d_mid[r], d_lo[r]], axis=1)
            b_g = xc[:, d_inner + g * ns:d_inner + (g + 1) * ns]
            bfull_sc[g] = jnp.concatenate([b_g, jnp.zeros_like(b_g)], axis=1)
            c_g = xc[:, d_inner + (M2_NGROUPS + g) * ns:d_inner + (M2_NGROUPS + g + 1) * ns]
            ct_sc[g] = c_g.T
        yt_sc[...] = jnp.zeros_like(yt_sc)

    row_id = lax.broadcasted_iota(jnp.int32, (nb, 2 * ns), 0)
    lane_id = lax.broadcasted_iota(jnp.int32, (nb, 2 * ns), 1)
    col_id = lax.broadcasted_iota(jnp.int32, (ns, nb), 1)
    for j in range(st_ref.shape[0]):
        b = i * st_ref.shape[0] + j
        r_bot = jnp.where((row_id == b) & (lane_id >= ns), 1.0, 0.0).astype(BF16)
        for g in range(M2_NGROUPS):
            r = pl.ds(g * rows_g, rows_g)
            r_top = jnp.where(row_id == b, bfull_sc[g], 0.0).astype(BF16)
            rhs = jnp.concatenate([r_top, r_bot, r_bot, r_bot], axis=0)
            o = _dot(lhs_sc[g], rhs)
            hnew = st_ref[j, r, :] * o[:, ns:] + o[:, :ns]
            so_ref[j, r, :] = hnew
            cm = jnp.where(col_id == b, ct_sc[g], 0.0).astype(BF16)
            yt_sc[r, :] += _dot(hnew.astype(BF16), cm)

    @pl.when(i == pl.num_programs(0) - 1)
    def _():
        y = yt_sc[...].T + dvec_ref[...] * xs_sc[...]
        out_ref[...] = _rms(y * _silu(z_ref[...]), nw_ref[...])


def ssd_sample(h, dt_raw, state, conv_state, p, *, d_s5, d_inner, nheads):
    nb = h.shape[0]
    conv_dim = d_inner + 2 * M2_NGROUPS * M2_DSTATE
    xw = 512
    xoff = d_s5 + d_inner
    assert nb == LANES and M2_DSTATE == LANES and M2_CONV == 4
    assert xoff % xw == 0 and conv_dim % xw == 0 and d_s5 % d_inner == 0 and nb % SSD_STEP_SEQS == 0
    n_xparts = conv_dim // xw
    rows = nheads * M2_HEADDIM
    rows_g = rows // M2_NGROUPS
    pad = lambda v: jnp.pad(v, (0, LANES - v.shape[0])).reshape(1, LANES)
    dvec = jnp.repeat(p['m2_d'], M2_HEADDIM).reshape(1, d_inner)
    full = lambda a, b, j=0: pl.BlockSpec((a, b), lambda i, j=j: (0, j))
    st_spec = pl.BlockSpec((SSD_STEP_SEQS, rows, M2_DSTATE), lambda i: (i, 0, 0))
    out, new_state = pl.pallas_call(
        functools.partial(_ssd_step_kernel, d_inner=d_inner, nheads=nheads, n_xparts=n_xparts),
        grid=(nb // SSD_STEP_SEQS,),
        in_specs=[full(nb, xw, xoff // xw + k) for k in range(n_xparts)]
        + [full(nb, d_inner, d_s5 // d_inner), full(nb, LANES)]
        + [full(nb, conv_dim)] * 3
        + [full(M2_CONV, conv_dim), full(1, conv_dim), full(1, LANES), full(1, LANES), full(1, d_inner),
           full(1, d_inner), st_spec],
        out_specs=[full(nb, d_inner), st_spec],
        out_shape=[jax.ShapeDtypeStruct((nb, d_inner), F32), jax.ShapeDtypeStruct((nb, rows, M2_DSTATE), F32)],
        scratch_shapes=[pltpu.VMEM((M2_NGROUPS, rows_g, 4 * nb), BF16),
                        pltpu.VMEM((M2_NGROUPS, nb, 2 * M2_DSTATE), F32),
                        pltpu.VMEM((M2_NGROUPS, M2_DSTATE, nb), F32),
                        pltpu.VMEM((rows, nb), F32),
                        pltpu.VMEM((nb, d_inner), F32)],
        compiler_params=_cparams(("arbitrary",)),
        name="ssd_step",
    )(*([h] * n_xparts), h, dt_raw, conv_state[:, 0], conv_state[:, 1], conv_state[:, 2],
      p['m2_conv_w'], p['m2_conv_b'].reshape(1, conv_dim), pad(p['m2_dt_bias']), pad(p['m2_a_log']),
      dvec, p['m2_norm_w'].reshape(1, d_inner), state.reshape(nb, rows, M2_DSTATE))
    xbc = lax.slice_in_dim(h, xoff, xoff + conv_dim, axis=1)
    new_conv = jnp.concatenate([conv_state[:, 1:], xbc[:, None, :]], axis=1)
    return out, new_state.reshape(state.shape), new_conv


def _softmax_rows(s):
    e = jnp.exp(s - jnp.max(s, axis=-1, keepdims=True))
    return e / jnp.sum(e, axis=-1, keepdims=True)


def _attn_kernel(q_ref, k_ref, v_ref, o_ref, *, scale):
    s = _dot_nt(q_ref[...].astype(BF16), k_ref[...].astype(BF16)) * scale
    o_ref[...] = _dot(_softmax_rows(s).astype(BF16), v_ref[...].astype(BF16))


def attention_prompt(q, k, v, nb, seq, mem, heads, *, tq):
    d = q.shape[1]
    hd = d // heads
    nq = seq // tq
    kv_spec = pl.BlockSpec((mem, hd), lambda b, h, i: (b, h))
    q_spec = pl.BlockSpec((tq, hd), lambda b, h, i: (b * nq + i, h))
    return pl.pallas_call(
        functools.partial(_attn_kernel, scale=hd ** -0.5),
        grid=(nb, heads, nq),
        in_specs=[q_spec, kv_spec, kv_spec],
        out_specs=q_spec,
        out_shape=jax.ShapeDtypeStruct(q.shape, F32),
        compiler_params=_cparams(("parallel", "parallel", "parallel")),
        name="attention_prompt",
    )(q, k, v)


ATTN_STEP_SEQS = 2


def _attn_step_kernel(q_ref, k_ref, v_ref, o_ref, *, heads, scale):
    d = q_ref.shape[-1]
    hd = d // heads
    for j in range(q_ref.shape[0]):
        prod = k_ref[j] * q_ref[j]
        v = v_ref[j]
        for h in range(heads):
            cols = slice(h * hd, (h + 1) * hd)
            s = jnp.sum(prod[:, cols], axis=-1, keepdims=True) * scale
            e = jnp.exp(s - jnp.max(s, axis=0, keepdims=True))
            att = e / jnp.sum(e, axis=0, keepdims=True)
            o_ref[j, :, cols] = jnp.sum(att * v[:, cols], axis=0, keepdims=True)


def attention_sample(q, k_cache, v_cache, heads):
    b, mem, d = k_cache.shape
    nseq = ATTN_STEP_SEQS
    q_spec = pl.BlockSpec((nseq, 1, d), lambda i: (i, 0, 0))
    kv_spec = pl.BlockSpec((nseq, mem, d), lambda i: (i, 0, 0))
    out = pl.pallas_call(
        functools.partial(_attn_step_kernel, heads=heads, scale=(d // heads) ** -0.5),
        grid=(b // nseq,),
        in_specs=[q_spec, kv_spec, kv_spec],
        out_specs=q_spec,
        out_shape=jax.ShapeDtypeStruct((b, 1, d), F32),
        compiler_params=_cparams(("parallel",)),
        name="attention_step",
    )(q.reshape(b, 1, d), k_cache, v_cache)
    return out.reshape(b, d)


NEG = -1e30


def _router_kernel(x_ref, nw_ref, wr_ref, br_ref, xn_ref, gates_ref, *, n_experts):
    xn = _rms(x_ref[...], nw_ref[...])
    xn_ref[...] = xn.astype(BF16)
    x_hi, x_lo = _split2(xn)
    w_hi, w_lo = _split2(wr_ref[...])
    logits = _dot(x_hi, w_hi) + (_dot(x_hi, w_lo) + _dot(x_lo, w_hi)) + br_ref[...]
    lane = lax.broadcasted_iota(jnp.int32, logits.shape, 1)
    big = jnp.int32(2 ** 30)
    is_c = (lane >= n_experts) & (lane < n_experts + MOE_GROUPS)
    lc = jnp.where(is_c, logits, NEG)
    cmax = jnp.max(lc, axis=-1, keepdims=True)
    gsel = jnp.min(jnp.where(lc == cmax, lane, big), axis=-1, keepdims=True) - n_experts
    gate_c = 1.0 / jnp.sum(jnp.where(is_c, jnp.exp(lc - cmax), 0.0), axis=-1, keepdims=True)
    in_group = (lane < n_experts) & (lane // MOE_PER_GROUP == gsel)
    lf = jnp.where(in_group, logits, NEG)
    t1 = jnp.max(lf, axis=-1, keepdims=True)
    i1 = jnp.min(jnp.where(lf == t1, lane, big), axis=-1, keepdims=True)
    lf2 = jnp.where(lane == i1, NEG, lf)
    t2 = jnp.max(lf2, axis=-1, keepdims=True)
    i2 = jnp.min(jnp.where(lf2 == t2, lane, big), axis=-1, keepdims=True)
    r = jnp.exp(t2 - t1)
    g1 = gate_c / (1.0 + r)
    g2 = gate_c * r / (1.0 + r)
    gates_ref[...] = jnp.where(lane == i1, g1, jnp.where(lane == i2, g2, 0.0))


def moe_router(x, norm_w, w_coarse, b_coarse, w_fine, b_fine, *, tm):
    m, d = x.shape
    e = w_fine.shape[1]
    padw = LANES - e - MOE_GROUPS
    wr = jnp.concatenate([w_fine, w_coarse, jnp.zeros((d, padw), F32)], axis=1)
    br = jnp.concatenate([b_fine, b_coarse, jnp.zeros((padw,), F32)]).reshape(1, LANES)
    return pl.pallas_call(
        functools.partial(_router_kernel, n_experts=e),
        grid=(m // tm,),
        in_specs=[pl.BlockSpec((tm, d), lambda i: (i, 0)), pl.BlockSpec((1, d), lambda i: (0, 0)),
                  pl.BlockSpec((d, LANES), lambda i: (0, 0)), pl.BlockSpec((1, LANES), lambda i: (0, 0))],
        out_specs=[pl.BlockSpec((tm, d), lambda i: (i, 0)), pl.BlockSpec((tm, LANES), lambda i: (i, 0))],
        out_shape=[jax.ShapeDtypeStruct((m, d), BF16), jax.ShapeDtypeStruct((m, LANES), F32)],
        compiler_params=_cparams(("parallel",)),
        name="moe_router",
    )(x, norm_w.reshape(1, d), wr, br)


def _moe_dense_kernel(xn_ref, gates_ref, wg_ref, wu_ref, wd_ref, x_ref, fw_ref, y_ref, acc_sc):
    e = pl.program_id(1)

    @pl.when(e == 0)
    def _():
        acc_sc[...] = x_ref[...]

    gates = gates_ref[...]
    lane = lax.broadcasted_iota(jnp.int32, gates.shape, 1)
    g_col = jnp.sum(jnp.where(lane == e, gates, 0.0), axis=-1, keepdims=True)
    xn = xn_ref[...]
    hg = _dot(xn, wg_ref[...].astype(BF16))
    hu = _dot(xn, wu_ref[...].astype(BF16))
    hmid = _silu(hg) * hu * g_col
    acc_sc[...] += _dot(hmid.astype(BF16), wd_ref[...].astype(BF16))

    @pl.when(e == pl.num_programs(1) - 1)
    def _():
        y_ref[...] = _rms(acc_sc[...], fw_ref[...])


def moe_dense_final(xn, gates, w_gate, w_up, w_down, x, final_w, *, tm):
    m, d = x.shape
    ne, _, f = w_gate.shape
    return pl.pallas_call(
        _moe_dense_kernel,
        grid=(m // tm, ne),
        in_specs=[pl.BlockSpec((tm, d), lambda i, e: (i, 0)), pl.BlockSpec((tm, LANES), lambda i, e: (i, 0)),
                  pl.BlockSpec((None, d, f), lambda i, e: (e, 0, 0)),
                  pl.BlockSpec((None, d, f), lambda i, e: (e, 0, 0)),
                  pl.BlockSpec((None, f, d), lambda i, e: (e, 0, 0)),
                  pl.BlockSpec((tm, d), lambda i, e: (i, 0)), pl.BlockSpec((1, d), lambda i, e: (0, 0))],
        out_specs=pl.BlockSpec((tm, d), lambda i, e: (i, 0)),
        out_shape=jax.ShapeDtypeStruct((m, d), F32),
        scratch_shapes=[pltpu.VMEM((tm, d), F32)],
        compiler_params=_cparams(("parallel", "arbitrary")),
        name="moe_dense",
    )(xn, gates, w_gate, w_up, w_down, x, final_w.reshape(1, d))


def s5_prompt(h, nb, seq, prep, p):
    g = prep[0].shape[0]
    q, ch = S5_Q, S5_CH
    ds = g * ch
    nchunk = seq // q
    u = h[:, :ds].astype(BF16).reshape(nb, nchunk, q, g, ch)
    u_t = u.transpose(3, 1, 0, 2, 4).reshape(g, nchunk * nb, q * ch)
    y_t, hfin = s5_scan(u_t, prep, nb)
    y = y_t.reshape(g, nchunk, nb, q, ch).transpose(2, 1, 3, 0, 4).reshape(nb * seq, ds)
    out = s5_head(y, h, p['s5_d'], p['s5_glu_w'], p['s5_glu_b'], p['s5_norm_w'], tm=min(512, nb * seq))
    return out, hfin[:, 0].transpose(1, 0, 2), hfin[:, 1].transpose(1, 0, 2)


def s5_sample(h, st_re, st_im, prep, p):
    g = prep[0].shape[0]
    ch = S5_CH
    ds = g * ch
    b = h.shape[0]
    u_t = h[:, :ds].reshape(b, g, ch).transpose(1, 0, 2)
    y_t, n_re, n_im = s5_step(u_t, st_re.transpose(1, 0, 2), st_im.transpose(1, 0, 2), prep,
                              p['s5_c_re'], p['s5_c_im'])
    y = y_t.transpose(1, 0, 2).reshape(b, ds)
    out = s5_head(y, h, p['s5_d'], p['s5_glu_w'], p['s5_glu_b'], p['s5_norm_w'], tm=b)
    return out, n_re.transpose(1, 0, 2), n_im.transpose(1, 0, 2)


def _row_tile(m):
    return min(m, 1024)


def _decoder_layer(x, p, s5_prep, final_w, *, nb, seq, mem_kv, xa_heads, states):
    m, d = x.shape
    g, n = p['s5_a_re'].shape
    d_s5 = g * S5_CH
    nheads = p['m2_a_log'].shape[0]
    d_inner = nheads * M2_HEADDIM
    conv_dim = d_inner + 2 * M2_NGROUPS * M2_DSTATE
    n_main = d_s5 + d_inner + conv_dim
    tm = _row_tile(m)

    w_in = p['w_in']
    w_dt = jnp.pad(w_in[:, n_main:], ((0, 0), (0, LANES - nheads)))
    h, dt_raw = fused_matmul([x], w_in, n_out=n_main, gain=p['norm_mix_w'], side_w=w_dt, tm=tm, tn=512)

    if states is None:
        s5_out, s5_re, s5_im = s5_prompt(h, nb, seq, s5_prep, p)
        m2_out, ssm, conv = ssd_prompt(h, dt_raw, nb, seq, p, d_s5=d_s5, d_inner=d_inner, nheads=nheads)
    else:
        s5_out, s5_re, s5_im = s5_sample(h, states[0], states[1], s5_prep, p)
        m2_out, ssm, conv = ssd_sample(h, dt_raw, states[2], states[3], p, d_s5=d_s5, d_inner=d_inner,
                                       nheads=nheads)
    x1 = fused_matmul([s5_out, m2_out], p['w_out'], n_out=d, res=x, tm=tm, tn=512)

    q = fused_matmul([x1], p['xa_wq'], n_out=d, gain=p['norm_xa_w'], tm=tm, tn=512)
    if states is None:
        mem = mem_kv[0].shape[0] // nb
        o = attention_prompt(q, mem_kv[0], mem_kv[1], nb, seq, mem, xa_heads, tq=min(seq, 512))
    else:
        o = attention_sample(q, mem_kv[0], mem_kv[1], xa_heads)
    x2 = fused_matmul([o], p['xa_wo'], n_out=d, res=x1, tm=tm, tn=512)

    xn, gates = moe_router(x2, p['norm_ffn_w'], p['router_coarse_w'], p['router_coarse_b'],
                           p['router_fine_w'], p['router_fine_b'], tm=min(m, 512))
    y = moe_dense_final(xn, gates, p['moe_w_gate'], p['moe_w_up'], p['moe_w_down'], x2, final_w, tm=min(m, 512))
    return y, s5_re, s5_im, ssm, conv


def kernel(x_prompt, x_sample, mem_prompt, state_s5_re, state_s5_im, state_ssm, state_conv, cache_mem_k, cache_mem_v, norm_mix_w, w_in, s5_a_re, s5_a_im, s5_log_dt, s5_b_re, s5_b_im, s5_c_re, s5_c_im, s5_d, s5_glu_w, s5_glu_b, s5_norm_w, m2_conv_w, m2_conv_b, m2_dt_bias, m2_a_log, m2_d, m2_norm_w, w_out, norm_xa_w, norm_mem_w, xa_wq, xa_wk, xa_wv, xa_wo, norm_ffn_w, router_coarse_w, router_coarse_b, router_fine_w, router_fine_b, moe_w_gate, moe_w_up, moe_w_down, norm_final_w):
    depth = w_in.shape[0]
    assert depth == 1, "the final norm is fused into the (only) layer"
    per_layer = dict(
        norm_mix_w=norm_mix_w, w_in=w_in, s5_a_re=s5_a_re, s5_a_im=s5_a_im, s5_log_dt=s5_log_dt,
        s5_b_re=s5_b_re, s5_b_im=s5_b_im, s5_c_re=s5_c_re, s5_c_im=s5_c_im, s5_d=s5_d, s5_glu_w=s5_glu_w,
        s5_glu_b=s5_glu_b, s5_norm_w=s5_norm_w, m2_conv_w=m2_conv_w, m2_conv_b=m2_conv_b, m2_dt_bias=m2_dt_bias,
        m2_a_log=m2_a_log, m2_d=m2_d, m2_norm_w=m2_norm_w, w_out=w_out, norm_xa_w=norm_xa_w,
        norm_mem_w=norm_mem_w, xa_wq=xa_wq, xa_wk=xa_wk, xa_wv=xa_wv, xa_wo=xa_wo, norm_ffn_w=norm_ffn_w,
        router_coarse_w=router_coarse_w, router_coarse_b=router_coarse_b, router_fine_w=router_fine_w,
        router_fine_b=router_fine_b, moe_w_gate=moe_w_gate, moe_w_up=moe_w_up, moe_w_down=moe_w_down)
    p = {k: v[0] for k, v in per_layer.items()}
    nb, seq, d = x_prompt.shape
    db, dseq, _ = x_sample.shape
    assert dseq == 1
    mem = mem_prompt.shape[1]
    xa_heads = cache_mem_k.shape[3]

    s5_prep = s5_prepare(p['s5_a_re'], p['s5_a_im'], p['s5_log_dt'], p['s5_b_re'], p['s5_b_im'],
                         p['s5_c_re'], p['s5_c_im'])

    memx = mem_prompt.reshape(nb * mem, d)
    mk = fused_matmul([memx], p['xa_wk'], n_out=d, gain=p['norm_mem_w'], tm=_row_tile(nb * mem), tn=512)
    mv = fused_matmul([memx], p['xa_wv'], n_out=d, gain=p['norm_mem_w'], tm=_row_tile(nb * mem), tn=512)
    yp, p_re, p_im, p_ssm, p_conv = _decoder_layer(
        x_prompt.reshape(nb * seq, d), p, s5_prep, norm_final_w, nb=nb, seq=seq, mem_kv=(mk, mv), xa_heads=xa_heads, states=None)

    ys, s_re, s_im, s_ssm, s_conv = _decoder_layer(
        x_sample.reshape(db, d), p, s5_prep, norm_final_w, nb=db, seq=1,
        mem_kv=(cache_mem_k[0].reshape(db, mem, d), cache_mem_v[0].reshape(db, mem, d)), xa_heads=xa_heads,
        states=(state_s5_re[0], state_s5_im[0], state_ssm[0], state_conv[0]))

    kv_shape = (1, nb, mem) + cache_mem_k.shape[3:]
    return (yp.reshape(nb, seq, d), ys.reshape(db, 1, d), p_re[None], p_im[None], p_ssm[None], p_conv[None],
            mk.reshape(kv_shape), mv.reshape(kv_shape), s_re[None], s_im[None], s_ssm[None], s_conv[None])
```

```python
import functools
import math

import jax
import jax.numpy as jnp
from jax import lax
from jax.experimental import pallas as pl
from jax.experimental.pallas import tpu as pltpu

F32 = jnp.float32
BF16 = jnp.bfloat16
RMS_EPS = 1e-6

V7X_VMEM_BYTES = 64 * 1024 * 1024
VMEM_LIMIT = V7X_VMEM_BYTES - 8 * 1024 * 1024
LANES = 128

S5_CH = 16
S5_N = 64
S5_Q = 16
S5_GB = 8
M2_HEADDIM = 64
M2_DSTATE = 128
M2_NGROUPS = 2
M2_CONV = 4
M2_CHUNK = 128
MOE_GROUPS = 4
MOE_PER_GROUP = 8


def _cparams(sem):
    return pltpu.CompilerParams(dimension_semantics=sem, vmem_limit_bytes=VMEM_LIMIT)


def _rms(x, w):
    return x * lax.rsqrt(jnp.mean(x * x, axis=-1, keepdims=True) + RMS_EPS) * w


def _sigmoid(x):
    return 1.0 / (1.0 + jnp.exp(-x))


def _silu(x):
    return x * _sigmoid(x)


def _softplus(x):
    return jnp.maximum(x, 0.0) + jnp.log1p(jnp.exp(-jnp.abs(x)))


def _gelu_tanh(x):
    return 0.5 * x * (1.0 + jnp.tanh(math.sqrt(2.0 / math.pi) * (x + 0.044715 * (x * x * x))))


def _dot(a, b):
    return jnp.dot(a, b, preferred_element_type=F32)


def _dot_nt(a, b):
    return lax.dot_general(a, b, (((1,), (1,)), ((), ())), preferred_element_type=F32)


def _split3(x):
    hi = x.astype(BF16)
    r = x - hi.astype(F32)
    mid = r.astype(BF16)
    lo = (r - mid.astype(F32)).astype(BF16)
    return hi, mid, lo


def _split2(x):
    hi = x.astype(BF16)
    lo = (x - hi.astype(F32)).astype(BF16)
    return hi, lo


def _mm_kernel(*refs, n_lhs, has_gain, has_res, has_side):
    it = iter(refs)
    lhs = [next(it) for _ in range(n_lhs)]
    gain = next(it) if has_gain else None
    ws = [next(it) for _ in range(n_lhs)]
    side_w = next(it) if has_side else None
    res = next(it) if has_res else None
    out = next(it)
    side_out = next(it) if has_side else None
    lhs_bf = next(it)

    @pl.when(pl.program_id(1) == 0)
    def _():
        for i in range(n_lhs):
            x = lhs[i][...]
            if has_gain:
                x = _rms(x, gain[...])
            lhs_bf[i] = x.astype(BF16)
        if has_side:
            side_out[...] = _dot(lhs_bf[0], side_w[...].astype(BF16))

    acc = None
    for i in range(n_lhs):
        p = _dot(lhs_bf[i], ws[i][...].astype(BF16))
        acc = p if acc is None else acc + p
    if has_res:
        acc = acc + res[...]
    out[...] = acc


def fused_matmul(lhs_list, w, *, n_out, gain=None, res=None, side_w=None, tm, tn):
    n_lhs = len(lhs_list)
    m, kp = lhs_list[0].shape
    assert all(a.shape == (m, kp) for a in lhs_list)
    assert w.shape[0] == n_lhs * kp and m % tm == 0 and n_out % tn == 0
    assert gain is None or n_lhs == 1
    grid = (m // tm, n_out // tn)
    in_specs = [pl.BlockSpec((tm, kp), lambda i, j: (i, 0)) for _ in range(n_lhs)]
    args = list(lhs_list)
    if gain is not None:
        in_specs.append(pl.BlockSpec((1, kp), lambda i, j: (0, 0)))
        args.append(gain.reshape(1, kp))
    for p in range(n_lhs):
        in_specs.append(pl.BlockSpec((kp, tn), lambda i, j, p=p: (p, j)))
        args.append(w)
    if side_w is not None:
        in_specs.append(pl.BlockSpec((kp, LANES), lambda i, j: (0, 0)))
        args.append(side_w)
    if res is not None:
        in_specs.append(pl.BlockSpec((tm, tn), lambda i, j: (i, j)))
        args.append(res)
    out_shape = [jax.ShapeDtypeStruct((m, n_out), F32)]
    out_specs = [pl.BlockSpec((tm, tn), lambda i, j: (i, j))]
    if side_w is not None:
        out_shape.append(jax.ShapeDtypeStruct((m, LANES), F32))
        out_specs.append(pl.BlockSpec((tm, LANES), lambda i, j: (i, 0)))
    outs = pl.pallas_call(
        functools.partial(_mm_kernel, n_lhs=n_lhs, has_gain=gain is not None,
                          has_res=res is not None, has_side=side_w is not None),
        grid=grid, in_specs=in_specs, out_specs=out_specs, out_shape=out_shape,
        scratch_shapes=[pltpu.VMEM((n_lhs, tm, kp), BF16)],
        compiler_params=_cparams(("parallel", "arbitrary")),
        name="fused_matmul",
    )(*args)
    return outs if side_w is not None else outs[0]


def _s5_prep_kernel(lre_ref, lim_ref, ldt_ref, btre_ref, btim_ref, cre_ref, cim_ref,
                    tz_ref, wsre_ref, wsim_ref, wcre_ref, wcim_ref, aq_ref, ab_ref, bbt_ref):
    q, ch = S5_Q, S5_CH
    lr = lre_ref[...]
    li = lim_ref[...]
    step = jnp.exp(ldt_ref[...])
    mag = jnp.exp(lr * step)
    ab_re = mag * jnp.cos(li * step)
    ab_im = mag * jnp.sin(li * step)
    den = lr * lr + li * li
    num_re = ab_re - 1.0
    coef_re = (num_re * lr + ab_im * li) / den
    coef_im = (ab_im * lr - num_re * li) / den
    bt_re = btre_ref[...]
    bt_im = btim_ref[...]
    bb_re = coef_re * bt_re - coef_im * bt_im
    bb_im = coef_re * bt_im + coef_im * bt_re
    c_re = cre_ref[...]
    c_im = cim_ref[...]

    pw = [(jnp.ones_like(ab_re), jnp.zeros_like(ab_re))]
    for _ in range(q):
        pr, pi = pw[-1]
        pw.append((pr * ab_re - pi * ab_im, pr * ab_im + pi * ab_re))

    ca_re = [c_re * pr - c_im * pi for pr, pi in pw]
    ca_im = [c_re * pi + c_im * pr for pr, pi in pw]
    wcre_ref[...] = jnp.concatenate(ca_re[1:], axis=0).astype(BF16)
    wcim_ref[...] = jnp.concatenate([-x for x in ca_im[1:]], axis=0).astype(BF16)

    pr_stack = jnp.concatenate(ca_re[:q], axis=0)
    pi_stack = jnp.concatenate(ca_im[:q], axis=0)
    krow = None
    for a, b, sign in ((bb_re, pr_stack, 1.0), (bb_im, pi_stack, -1.0)):
        a_hi, a_lo = _split2(a)
        b_hi, b_lo = _split2(b)
        t = _dot_nt(a_hi, b_hi) + (_dot_nt(a_hi, b_lo) + _dot_nt(a_lo, b_hi))
        krow = sign * t if krow is None else krow + sign * t
    lane = lax.broadcasted_iota(jnp.int32, krow.shape, 1)
    blocks = [krow]
    for s in range(1, q):
        blocks.append(jnp.where(lane >= s * ch, pltpu.roll(krow, s * ch, 1), 0.0))
    tz_ref[...] = jnp.concatenate(blocks, axis=0).astype(BF16)

    ws_re, ws_im = [], []
    for s in range(q):
        pr, pi = pw[q - 1 - s]
        ws_re.append(bb_re * pr - bb_im * pi)
        ws_im.append(bb_re * pi + bb_im * pr)
    wsre_ref[...] = jnp.concatenate(ws_re, axis=0).astype(BF16)
    wsim_ref[...] = jnp.concatenate(ws_im, axis=0).astype(BF16)

    aq_ref[0:1, :] = pw[q][0]
    aq_ref[1:2, :] = pw[q][1]
    ab_ref[0:1, :] = ab_re
    ab_ref[1:2, :] = ab_im
    bbt_ref[0:ch, :] = bb_re
    bbt_ref[ch:2 * ch, :] = bb_im


def s5_prepare(a_re, a_im, log_dt, b_re, b_im, c_re, c_im):
    g, n = a_re.shape
    ch, q = S5_CH, S5_Q
    qc = q * ch
    bt_re = jnp.swapaxes(b_re, 1, 2)
    bt_im = jnp.swapaxes(b_im, 1, 2)

    def per_g(*dims):
        return pl.BlockSpec((None,) + dims, lambda i: (i,) + (0,) * len(dims))

    return pl.pallas_call(
        _s5_prep_kernel,
        grid=(g,),
        in_specs=[per_g(1, n), per_g(1, n), per_g(1, 1), per_g(ch, n), per_g(ch, n), per_g(ch, n), per_g(ch, n)],
        out_specs=[per_g(qc, qc), per_g(qc, n), per_g(qc, n), per_g(qc, n), per_g(qc, n),
                   per_g(2, n), per_g(2, n), per_g(2 * ch, n)],
        out_shape=[jax.ShapeDtypeStruct((g, qc, qc), BF16),
                   jax.ShapeDtypeStruct((g, qc, n), BF16), jax.ShapeDtypeStruct((g, qc, n), BF16),
                   jax.ShapeDtypeStruct((g, qc, n), BF16), jax.ShapeDtypeStruct((g, qc, n), BF16),
                   jax.ShapeDtypeStruct((g, 2, n), F32), jax.ShapeDtypeStruct((g, 2, n), F32),
                   jax.ShapeDtypeStruct((g, 2 * ch, n), F32)],
        compiler_params=_cparams(("parallel",)),
        name="s5_prepare",
    )(a_re.reshape(g, 1, n), a_im.reshape(g, 1, n), log_dt.reshape(g, 1, 1), bt_re, bt_im, c_re, c_im)


def _s5_scan_kernel(u_ref, tz_ref, wsre_ref, wsim_ref, wcre_ref, wcim_ref, aq_ref,
                    y_ref, hfin_ref, sre_sc, sim_sc, *, nb, nchunk):
    gb = u_ref.shape[0]
    for g in range(gb):
        u = u_ref[g]
        y_ref[g] = _dot(u, tz_ref[g])
        sre_sc[g] = _dot(u, wsre_ref[g])
        sim_sc[g] = _dot(u, wsim_ref[g])

    ar = [jnp.broadcast_to(aq_ref[g, 0:1, :], (nb, S5_N)) for g in range(gb)]
    ai = [jnp.broadcast_to(aq_ref[g, 1:2, :], (nb, S5_N)) for g in range(gb)]

    def step(c, carry):
        rows = pl.ds(pl.multiple_of(c * nb, nb), nb)
        new = []
        for g in range(gb):
            hr, hi = carry[g]
            sr = sre_sc[g, rows, :]
            si = sim_sc[g, rows, :]
            sre_sc[g, rows, :] = hr
            sim_sc[g, rows, :] = hi
            new.append((ar[g] * hr - ai[g] * hi + sr, ar[g] * hi + ai[g] * hr + si))
        return tuple(new)

    zero = jnp.zeros((nb, S5_N), F32)
    fin = lax.fori_loop(0, nchunk, step, tuple((zero, zero) for _ in range(gb)))
    for g in range(gb):
        hfin_ref[g, 0] = fin[g][0]
        hfin_ref[g, 1] = fin[g][1]
        y_ref[g] += (_dot_nt(sre_sc[g].astype(BF16), wcre_ref[g])
                     + _dot_nt(sim_sc[g].astype(BF16), wcim_ref[g]))


def s5_scan(u_t, prep, nb):
    tz, ws_re, ws_im, wc_re, wc_im, aq = prep[:6]
    g, rows, qc = u_t.shape
    n = S5_N
    gb = S5_GB
    nchunk = rows // nb

    def blk(*dims):
        return pl.BlockSpec((gb,) + dims, lambda i: (i,) + (0,) * len(dims))

    return pl.pallas_call(
        functools.partial(_s5_scan_kernel, nb=nb, nchunk=nchunk),
        grid=(g // gb,),
        in_specs=[blk(rows, qc), blk(qc, qc), blk(qc, n), blk(qc, n), blk(qc, n), blk(qc, n), blk(2, n)],
        out_specs=[blk(rows, qc), blk(2, nb, n)],
        out_shape=[jax.ShapeDtypeStruct((g, rows, qc), F32), jax.ShapeDtypeStruct((g, 2, nb, n), F32)],
        scratch_shapes=[pltpu.VMEM((gb, rows, n), F32), pltpu.VMEM((gb, rows, n), F32)],
        compiler_params=_cparams(("parallel",)),
        name="s5_scan",
    )(u_t, tz, ws_re, ws_im, wc_re, wc_im, aq)


def _s5_step_kernel(u_ref, hre_ref, him_ref, ab_ref, bbt_ref, cre_ref, cim_ref, y_ref, ore_ref, oim_ref):
    gb = u_ref.shape[0]
    ch = S5_CH
    for g in range(gb):
        u = u_ref[g].astype(BF16)
        bb_re = bbt_ref[g, 0:ch, :].astype(BF16)
        bb_im = bbt_ref[g, ch:2 * ch, :].astype(BF16)
        ar = ab_ref[g, 0:1, :]
        ai = ab_ref[g, 1:2, :]
        hr0 = hre_ref[g]
        hi0 = him_ref[g]
        hr = _dot(u, bb_re) + (ar * hr0 - ai * hi0)
        hi = _dot(u, bb_im) + (ar * hi0 + ai * hr0)
        ore_ref[g] = hr
        oim_ref[g] = hi
        y_ref[g] = (_dot_nt(hr.astype(BF16), cre_ref[g].astype(BF16))
                    - _dot_nt(hi.astype(BF16), cim_ref[g].astype(BF16)))


def s5_step(u_t, h_re, h_im, prep, c_re, c_im):
    ab, bbt = prep[6], prep[7]
    g, b, ch = u_t.shape
    n = S5_N
    gb = S5_GB

    def blk(*dims):
        return pl.BlockSpec((gb,) + dims, lambda i: (i,) + (0,) * len(dims))

    return pl.pallas_call(
        _s5_step_kernel,
        grid=(g // gb,),
        in_specs=[blk(b, ch), blk(b, n), blk(b, n), blk(2, n), blk(2 * ch, n), blk(ch, n), blk(ch, n)],
        out_specs=[blk(b, ch), blk(b, n), blk(b, n)],
        out_shape=[jax.ShapeDtypeStruct((g, b, ch), F32), jax.ShapeDtypeStruct((g, b, n), F32),
                   jax.ShapeDtypeStruct((g, b, n), F32)],
        compiler_params=_cparams(("parallel",)),
        name="s5_step",
    )(u_t, h_re, h_im, ab, bbt, c_re, c_im)


def _s5_head_kernel(y_ref, u_ref, d_ref, w_ref, b_ref, nw_ref, o_ref):
    y = y_ref[...] + d_ref[...] * u_ref[...]
    g = _gelu_tanh(y)
    gate = _sigmoid(_dot(g.astype(BF16), w_ref[...].astype(BF16)) + b_ref[...])
    o_ref[...] = _rms(g * gate, nw_ref[...])


def s5_head(y, h, d, glu_w, glu_b, norm_w, *, tm):
    m, ds = y.shape
    row = lambda a: a.reshape(1, ds)
    vec = pl.BlockSpec((1, ds), lambda i: (0, 0))
    return pl.pallas_call(
        _s5_head_kernel,
        grid=(m // tm,),
        in_specs=[pl.BlockSpec((tm, ds), lambda i: (i, 0)), pl.BlockSpec((tm, ds), lambda i: (i, 0)), vec,
                  pl.BlockSpec((ds, ds), lambda i: (0, 0)), vec, vec],
        out_specs=pl.BlockSpec((tm, ds), lambda i: (i, 0)),
        out_shape=jax.ShapeDtypeStruct((m, ds), F32),
        compiler_params=_cparams(("parallel",)),
        name="s5_head",
    )(y, h, row(d), glu_w, row(glu_b), row(norm_w))


def _pair_select(first, col0, col1, shape):
    return jnp.where(first, jnp.broadcast_to(col0, shape), jnp.broadcast_to(col1, shape))


def _ssd_chunk_kernel(*refs, d_inner, nheads, n_xparts):
    xparts = refs[:n_xparts]
    (z_ref, dt_ref, cw_ref, cb_ref, dtb_ref, alog_ref, dvec_ref, nw_ref,
     out_ref, ssm_ref, conv_ref, state_sc, xpad_sc, y_sc) = refs[n_xparts:]
    c = pl.program_id(1)
    q = M2_CHUNK
    hp = M2_HEADDIM
    ns = M2_DSTATE
    heads_per_group = nheads // M2_NGROUPS
    halo = 8

    @pl.when(c == 0)
    def _():
        state_sc[...] = jnp.zeros_like(state_sc)
        xpad_sc[0:halo, :] = jnp.zeros((halo, xpad_sc.shape[1]), F32)

    wpart = xparts[0].shape[1]
    for i, xr in enumerate(xparts):
        xpad_sc[halo:halo + q, i * wpart:(i + 1) * wpart] = xr[...]
    cw = cw_ref[...]
    conv = cb_ref[...] + cw[M2_CONV - 1:M2_CONV, :] * xpad_sc[halo:halo + q, :]
    for k in range(1, M2_CONV):
        conv = conv + cw[M2_CONV - 1 - k:M2_CONV - k, :] * xpad_sc[halo - k:halo - k + q, :]
    xpad_sc[0:halo, :] = xpad_sc[q:q + halo, :]
    xc = _silu(conv)

    dt = _softplus(dt_ref[...] + dtb_ref[...])
    a = -jnp.exp(alog_ref[...])
    da = dt * a
    row = lax.broadcasted_iota(jnp.int32, (q, q), 0)
    col = lax.broadcasted_iota(jnp.int32, (q, q), 1)
    causal = row >= col
    tri = jnp.where(causal, 1.0, 0.0).astype(BF16)
    d_hi, d_mid, d_lo = _split3(da)
    acum = _dot(tri, d_hi) + (_dot(tri, d_mid) + _dot(tri, d_lo))
    acum_t = acum.T
    alast = acum[q - 1:q, :]
    first = col < hp
    first_rows = row < hp

    for pr in range(nheads // 2):
        grp = (2 * pr) // heads_per_group
        b_bf = xc[:, d_inner + grp * ns:d_inner + (grp + 1) * ns].astype(BF16)
        c_bf = xc[:, d_inner + (M2_NGROUPS + grp) * ns:d_inner + (M2_NGROUPS + grp + 1) * ns].astype(BF16)
        cb = _dot_nt(c_bf, b_bf)
        xpair = xc[:, pr * 2 * hp:(pr + 1) * 2 * hp]
        h0, h1 = 2 * pr, 2 * pr + 1
        acol = [acum[:, h:h + 1] for h in (h0, h1)]
        m = []
        for k, h in enumerate((h0, h1)):
            seg = jnp.broadcast_to(acol[k], (q, q)) - jnp.broadcast_to(acum_t[h:h + 1, :], (q, q))
            lmat = jnp.exp(jnp.where(causal, seg, -1e30))
            m.append((cb * lmat).astype(BF16))
        dtp = _pair_select(first, dt[:, h0:h0 + 1], dt[:, h1:h1 + 1], (q, q))
        xdt = xpair * dtp
        xdt_bf = xdt.astype(BF16)
        y_diag = jnp.where(first, _dot(m[0], xdt_bf), _dot(m[1], xdt_bf))
        dec_end = _pair_select(first, jnp.exp(alast[:, h0:h0 + 1] - acol[0]),
                               jnp.exp(alast[:, h1:h1 + 1] - acol[1]), (q, q))
        xw_t = (xdt * dec_end).T.astype(BF16)
        chunk_state = _dot(xw_t, b_bf)
        rows = pl.ds(pr * 2 * hp, 2 * hp)
        prev = state_sc[rows, :]
        y_off = _dot_nt(c_bf, prev.astype(BF16)) * _pair_select(first, jnp.exp(acol[0]), jnp.exp(acol[1]), (q, q))
        sdec = jnp.where(first_rows, jnp.broadcast_to(jnp.exp(alast[:, h0:h0 + 1]), (q, q)),
                         jnp.broadcast_to(jnp.exp(alast[:, h1:h1 + 1]), (q, q)))
        state_sc[rows, :] = prev * sdec + chunk_state
        y_sc[:, pr * 2 * hp:(pr + 1) * 2 * hp] = y_diag + y_off + dvec_ref[:, pr * 2 * hp:(pr + 1) * 2 * hp] * xpair

    out_ref[...] = _rms(y_sc[...] * _silu(z_ref[...]), nw_ref[...])

    @pl.when(c == pl.num_programs(1) - 1)
    def _():
        ssm_ref[...] = state_sc[...]
        conv_ref[...] = xpad_sc[halo + q - (M2_CONV - 1):halo + q, :]


def ssd_prompt(h, dt_raw, nb, seq, p, *, d_s5, d_inner, nheads):
    q = M2_CHUNK
    nc = seq // q
    conv_dim = d_inner + 2 * M2_NGROUPS * M2_DSTATE
    xw = 512
    xoff = d_s5 + d_inner
    assert d_s5 % d_inner == 0 and xoff % xw == 0 and conv_dim % xw == 0
    assert M2_CHUNK == 2 * M2_HEADDIM == M2_DSTATE == LANES
    zblk = d_s5 // d_inner
    n_xparts = conv_dim // xw
    m = nb * seq
    pad = lambda v: jnp.pad(v, (0, LANES - v.shape[0])).reshape(1, LANES)
    dvec = jnp.repeat(p['m2_d'], M2_HEADDIM).reshape(1, d_inner)
    vec = lambda n: pl.BlockSpec((1, n), lambda b, c: (0, 0))
    tok = lambda w, j: pl.BlockSpec((q, w), lambda b, c, j=j: (b * nc + c, j))
    out, ssm, conv = pl.pallas_call(
        functools.partial(_ssd_chunk_kernel, d_inner=d_inner, nheads=nheads, n_xparts=n_xparts),
        grid=(nb, nc),
        in_specs=[tok(xw, xoff // xw + i) for i in range(n_xparts)] + [tok(d_inner, zblk), tok(LANES, 0),
                  pl.BlockSpec((M2_CONV, conv_dim), lambda b, c: (0, 0)), vec(conv_dim), vec(LANES), vec(LANES),
                  vec(d_inner), vec(d_inner)],
        out_specs=[tok(d_inner, 0),
                   pl.BlockSpec((None, nheads * M2_HEADDIM, M2_DSTATE), lambda b, c: (b, 0, 0)),
                   pl.BlockSpec((None, M2_CONV - 1, conv_dim), lambda b, c: (b, 0, 0))],
        out_shape=[jax.ShapeDtypeStruct((m, d_inner), F32),
                   jax.ShapeDtypeStruct((nb, nheads * M2_HEADDIM, M2_DSTATE), F32),
                   jax.ShapeDtypeStruct((nb, M2_CONV - 1, conv_dim), F32)],
        scratch_shapes=[pltpu.VMEM((nheads * M2_HEADDIM, M2_DSTATE), F32),
                        pltpu.VMEM((q + 8, conv_dim), F32),
                        pltpu.VMEM((q, d_inner), F32)],
        compiler_params=_cparams(("parallel", "arbitrary")),
        name="ssd_chunk",
    )(*([h] * n_xparts), h, dt_raw, p['m2_conv_w'], p['m2_conv_b'].reshape(1, conv_dim), pad(p['m2_dt_bias']),
      pad(p['m2_a_log']), dvec, p['m2_norm_w'].reshape(1, d_inner))
    return out, ssm.reshape(nb, nheads, M2_HEADDIM, M2_DSTATE), conv


SSD_STEP_SEQS = 4


def _ssd_step_kernel(*refs, d_inner, nheads, n_xparts):
    xparts = refs[:n_xparts]
    (z_ref, dt_ref, cs0_ref, cs1_ref, cs2_ref, cw_ref, cb_ref, dtb_ref, alog_ref, dvec_ref, nw_ref, st_ref,
     out_ref, so_ref, lhs_sc, bfull_sc, ct_sc, yt_sc, xs_sc) = refs[n_xparts:]
    i = pl.program_id(0)
    nb = z_ref.shape[0]
    ns = M2_DSTATE
    rows_g = (nheads // M2_NGROUPS) * M2_HEADDIM

    @pl.when(i == 0)
    def _():
        cw = cw_ref[...]
        xbc = jnp.concatenate([xr[...] for xr in xparts], axis=1)
        conv = (cb_ref[...] + cw[3:4, :] * xbc + cw[2:3, :] * cs2_ref[...]
                + cw[1:2, :] * cs1_ref[...] + cw[0:1, :] * cs0_ref[...])
        xc = _silu(conv)
        dt = _softplus(dt_ref[...] + dtb_ref[...])
        dec = jnp.exp(dt * (-jnp.exp(alog_ref[...])))
        hrow = lax.broadcasted_iota(jnp.int32, (LANES, d_inner), 0)
        hcol = lax.broadcasted_iota(jnp.int32, (LANES, d_inner), 1)
        expand = jnp.where(hcol // M2_HEADDIM == hrow, 1.0, 0.0).astype(BF16)

        def expand_heads(v):
            a, b_, c = _split3(v)
            return _dot(a, expand) + (_dot(b_, expand) + _dot(c, expand))

        xs = xc[:, :d_inner]
        xs_sc[...] = xs
        xdt_t = (xs * expand_heads(dt)).T
        d_hi, d_mid, d_lo = _split3(expand_heads(dec).T)
        for g in range(M2_NGROUPS):
            r = slice(g * rows_g, (g + 1) * rows_g)
            lhs_sc[g] = jnp.concatenate([xdt_t[r].astype(BF16), d_hi[r], d_mid[r], d_lo[r]], axis=1)
            b_g = xc[:, d_inner + g * ns:d_inner + (g + 1) * ns]
            bfull_sc[g] = jnp.concatenate([b_g, jnp.zeros_like(b_g)], axis=1)
            c_g = xc[:, d_inner + (M2_NGROUPS + g) * ns:d_inner + (M2_NGROUPS + g + 1) * ns]
            ct_sc[g] = c_g.T
        yt_sc[...] = jnp.zeros_like(yt_sc)

    row_id = lax.broadcasted_iota(jnp.int32, (nb, 2 * ns), 0)
    lane_id = lax.broadcasted_iota(jnp.int32, (nb, 2 * ns), 1)
    col_id = lax.broadcasted_iota(jnp.int32, (ns, nb), 1)
    for j in range(st_ref.shape[0]):
        b = i * st_ref.shape[0] + j
        r_bot = jnp.where((row_id == b) & (lane_id >= ns), 1.0, 0.0).astype(BF16)
        for g in range(M2_NGROUPS):
            r = pl.ds(g * rows_g, rows_g)
            r_top = jnp.where(row_id == b, bfull_sc[g], 0.0).astype(BF16)
            rhs = jnp.concatenate([r_top, r_bot, r_bot, r_bot], axis=0)
            o = _dot(lhs_sc[g], rhs)
            hnew = st_ref[j, r, :] * o[:, ns:] + o[:, :ns]
            so_ref[j, r, :] = hnew
            cm = jnp.where(col_id == b, ct_sc[g], 0.0).astype(BF16)
            yt_sc[r, :] += _dot(hnew.astype(BF16), cm)

    @pl.when(i == pl.num_programs(0) - 1)
    def _():
        y = yt_sc[...].T + dvec_ref[...] * xs_sc[...]
        out_ref[...] = _rms(y * _silu(z_ref[...]), nw_ref[...])


def ssd_sample(h, dt_raw, state, conv_state, p, *, d_s5, d_inner, nheads):
    nb = h.shape[0]
    conv_dim = d_inner + 2 * M2_NGROUPS * M2_DSTATE
    xw = 512
    xoff = d_s5 + d_inner
    assert nb == LANES and M2_DSTATE == LANES and M2_CONV == 4
    assert xoff % xw == 0 and conv_dim % xw == 0 and d_s5 % d_inner == 0 and nb % SSD_STEP_SEQS == 0
    n_xparts = conv_dim // xw
    rows = nheads * M2_HEADDIM
    rows_g = rows // M2_NGROUPS
    pad = lambda v: jnp.pad(v, (0, LANES - v.shape[0])).reshape(1, LANES)
    dvec = jnp.repeat(p['m2_d'], M2_HEADDIM).reshape(1, d_inner)
    full = lambda a, b, j=0: pl.BlockSpec((a, b), lambda i, j=j: (0, j))
    st_spec = pl.BlockSpec((SSD_STEP_SEQS, rows, M2_DSTATE), lambda i: (i, 0, 0))
    out, new_state = pl.pallas_call(
        functools.partial(_ssd_step_kernel, d_inner=d_inner, nheads=nheads, n_xparts=n_xparts),
        grid=(nb // SSD_STEP_SEQS,),
        in_specs=[full(nb, xw, xoff // xw + k) for k in range(n_xparts)]
        + [full(nb, d_inner, d_s5 // d_inner), full(nb, LANES)]
        + [full(nb, conv_dim)] * 3
        + [full(M2_CONV, conv_dim), full(1, conv_dim), full(1, LANES), full(1, LANES), full(1, d_inner),
           full(1, d_inner), st_spec],
        out_specs=[full(nb, d_inner), st_spec],
        out_shape=[jax.ShapeDtypeStruct((nb, d_inner), F32), jax.ShapeDtypeStruct((nb, rows, M2_DSTATE), F32)],
        scratch_shapes=[pltpu.VMEM((M2_NGROUPS, rows_g, 4 * nb), BF16),
                        pltpu.VMEM((M2_NGROUPS, nb, 2 * M2_DSTATE), F32),
                        pltpu.VMEM((M2_NGROUPS, M2_DSTATE, nb), F32),
                        pltpu.VMEM((rows, nb), F32),
                        pltpu.VMEM((nb, d_inner), F32)],
        compiler_params=_cparams(("arbitrary",)),
        name="ssd_step",
    )(*([h] * n_xparts), h, dt_raw, conv_state[:, 0], conv_state[:, 1], conv_state[:, 2],
      p['m2_conv_w'], p['m2_conv_b'].reshape(1, conv_dim), pad(p['m2_dt_bias']), pad(p['m2_a_log']),
      dvec, p['m2_norm_w'].reshape(1, d_inner), state.reshape(nb, rows, M2_DSTATE))
    xbc = lax.slice_in_dim(h, xoff, xoff + conv_dim, axis=1)
    new_conv = jnp.concatenate([conv_state[:, 1:], xbc[:, None, :]], axis=1)
    return out, new_state.reshape(state.shape), new_conv


def _softmax_rows(s):
    e = jnp.exp(s - jnp.max(s, axis=-1, keepdims=True))
    return e / jnp.sum(e, axis=-1, keepdims=True)


def _attn_kernel(q_ref, k_ref, v_ref, o_ref, *, scale):
    s = _dot_nt(q_ref[...].astype(BF16), k_ref[...].astype(BF16)) * scale
    o_ref[...] = _dot(_softmax_rows(s).astype(BF16), v_ref[...].astype(BF16))


def attention_prompt(q, k, v, nb, seq, mem, heads, *, tq):
    d = q.shape[1]
    hd = d // heads
    nq = seq // tq
    kv_spec = pl.BlockSpec((mem, hd), lambda b, h, i: (b, h))
    q_spec = pl.BlockSpec((tq, hd), lambda b, h, i: (b * nq + i, h))
    return pl.pallas_call(
        functools.partial(_attn_kernel, scale=hd ** -0.5),
        grid=(nb, heads, nq),
        in_specs=[q_spec, kv_spec, kv_spec],
        out_specs=q_spec,
        out_shape=jax.ShapeDtypeStruct(q.shape, F32),
        compiler_params=_cparams(("parallel", "parallel", "parallel")),
        name="attention_prompt",
    )(q, k, v)


ATTN_STEP_SEQS = 2


def _attn_step_kernel(q_ref, k_ref, v_ref, o_ref, *, scale):
    for j in range(q_ref.shape[0]):
        s = jnp.sum(k_ref[j] * q_ref[j], axis=-1, keepdims=True) * scale
        e = jnp.exp(s - jnp.max(s, axis=0, keepdims=True))
        att = e / jnp.sum(e, axis=0, keepdims=True)
        o_ref[j] = jnp.sum(att * v_ref[j], axis=0)


def attention_sample(q, k_cache, v_cache):
    b, mem, heads, hd = k_cache.shape
    nseq = ATTN_STEP_SEQS
    q_spec = pl.BlockSpec((nseq, heads, hd), lambda i: (i, 0, 0))
    kv_spec = pl.BlockSpec((nseq, mem, heads, hd), lambda i: (i, 0, 0, 0))
    out = pl.pallas_call(
        functools.partial(_attn_step_kernel, scale=hd ** -0.5),
        grid=(b // nseq,),
        in_specs=[q_spec, kv_spec, kv_spec],
        out_specs=q_spec,
        out_shape=jax.ShapeDtypeStruct((b, heads, hd), F32),
        compiler_params=_cparams(("parallel",)),
        name="attention_step",
    )(q.reshape(b, heads, hd), k_cache, v_cache)
    return out.reshape(b, heads * hd)


NEG = -1e30


INFO_G1, INFO_G2, INFO_E1, INFO_E2 = 0, 1, 2, 3


def _lane_pack(lane, values):
    out = 0.0
    for k, v in values:
        out = jnp.where(lane == k, v, out)
    return out


def _router_kernel(x_ref, nw_ref, wr_ref, br_ref, xn_ref, sel_ref, info_ref, *, n_experts):
    xn = _rms(x_ref[...], nw_ref[...])
    xn_ref[...] = xn
    x_hi, x_lo = _split2(xn)
    w_hi, w_lo = _split2(wr_ref[...])
    logits = _dot(x_hi, w_hi) + (_dot(x_hi, w_lo) + _dot(x_lo, w_hi)) + br_ref[...]
    lane = lax.broadcasted_iota(jnp.int32, logits.shape, 1)
    big = jnp.int32(2 ** 30)
    is_c = (lane >= n_experts) & (lane < n_experts + MOE_GROUPS)
    lc = jnp.where(is_c, logits, NEG)
    cmax = jnp.max(lc, axis=-1, keepdims=True)
    gsel = jnp.min(jnp.where(lc == cmax, lane, big), axis=-1, keepdims=True) - n_experts
    gate_c = 1.0 / jnp.sum(jnp.where(is_c, jnp.exp(lc - cmax), 0.0), axis=-1, keepdims=True)
    in_group = (lane < n_experts) & (lane // MOE_PER_GROUP == gsel)
    lf = jnp.where(in_group, logits, NEG)
    t1 = jnp.max(lf, axis=-1, keepdims=True)
    i1 = jnp.min(jnp.where(lf == t1, lane, big), axis=-1, keepdims=True)
    lf2 = jnp.where(lane == i1, NEG, lf)
    t2 = jnp.max(lf2, axis=-1, keepdims=True)
    i2 = jnp.min(jnp.where(lf2 == t2, lane, big), axis=-1, keepdims=True)
    r = jnp.exp(t2 - t1)
    g1 = gate_c / (1.0 + r)
    g2 = gate_c * r / (1.0 + r)
    sel_ref[...] = jnp.where((lane == i1) | (lane == i2), 1.0, 0.0)
    info_ref[...] = _lane_pack(lane, ((INFO_G1, g1), (INFO_G2, g2), (INFO_E1, i1.astype(F32)),
                                      (INFO_E2, i2.astype(F32))))


def moe_router(x, norm_w, w_coarse, b_coarse, w_fine, b_fine, *, tm):
    m, d = x.shape
    e = w_fine.shape[1]
    padw = LANES - e - MOE_GROUPS
    wr = jnp.concatenate([w_fine, w_coarse, jnp.zeros((d, padw), F32)], axis=1)
    br = jnp.concatenate([b_fine, b_coarse, jnp.zeros((padw,), F32)]).reshape(1, LANES)
    tok = pl.BlockSpec((tm, LANES), lambda i: (i, 0))
    return pl.pallas_call(
        functools.partial(_router_kernel, n_experts=e),
        grid=(m // tm,),
        in_specs=[pl.BlockSpec((tm, d), lambda i: (i, 0)), pl.BlockSpec((1, d), lambda i: (0, 0)),
                  pl.BlockSpec((d, LANES), lambda i: (0, 0)), pl.BlockSpec((1, LANES), lambda i: (0, 0))],
        out_specs=[pl.BlockSpec((tm, d), lambda i: (i, 0)), tok, tok],
        out_shape=[jax.ShapeDtypeStruct((m, d), F32), jax.ShapeDtypeStruct((m, LANES), F32),
                   jax.ShapeDtypeStruct((m, LANES), F32)],
        compiler_params=_cparams(("parallel",)),
        name="moe_router",
    )(x, norm_w.reshape(1, d), wr, br)


def _moe_rank_kernel(sel_ref, info_ref, rank_ref, counts_ref, carry_sc):
    i = pl.program_id(0)

    @pl.when(i == 0)
    def _():
        carry_sc[...] = jnp.zeros_like(carry_sc)

    sel = sel_ref[...]
    tm = sel.shape[0]
    row = lax.broadcasted_iota(jnp.int32, (tm, tm), 0)
    col = lax.broadcasted_iota(jnp.int32, (tm, tm), 1)
    before = jnp.where(row > col, 1.0, 0.0).astype(BF16)
    rank = _dot(before, sel.astype(BF16)) + carry_sc[...]
    info = info_ref[...]
    lane = lax.broadcasted_iota(jnp.int32, sel.shape, 1)
    e1 = info[:, INFO_E1:INFO_E1 + 1].astype(jnp.int32)
    e2 = info[:, INFO_E2:INFO_E2 + 1].astype(jnp.int32)
    r1 = jnp.sum(jnp.where(lane == e1, rank, 0.0), axis=-1, keepdims=True)
    r2 = jnp.sum(jnp.where(lane == e2, rank, 0.0), axis=-1, keepdims=True)
    rank_ref[...] = _lane_pack(lane, ((0, r1), (1, r2)))
    carry_sc[...] += jnp.sum(sel, axis=0, keepdims=True)

    @pl.when(i == pl.num_programs(0) - 1)
    def _():
        counts_ref[...] = carry_sc[...]


def _moe_plan_kernel(rank_ref, info_ref, counts_ref, dest_ref, tiles_ref, *, tile, n_experts):
    counts = counts_ref[...]
    ntile_e = jnp.floor((counts + (tile - 1)) * (1.0 / tile))
    padded = jnp.broadcast_to(ntile_e * tile, (8, LANES))
    r = lax.broadcasted_iota(jnp.int32, (LANES, LANES), 0)
    c = lax.broadcasted_iota(jnp.int32, (LANES, LANES), 1)
    lower = jnp.where(r < c, 1.0, 0.0).astype(BF16)
    p_hi, p_mid, p_lo = _split3(padded)
    offs = (_dot(p_hi, lower) + (_dot(p_mid, lower) + _dot(p_lo, lower)))[0:1, :]
    ends = offs + padded[0:1, :]

    info = info_ref[...]
    rank = rank_ref[...]
    lane = lax.broadcasted_iota(jnp.int32, info.shape, 1)
    e1 = info[:, INFO_E1:INFO_E1 + 1].astype(jnp.int32)
    e2 = info[:, INFO_E2:INFO_E2 + 1].astype(jnp.int32)
    d1 = jnp.sum(jnp.where(lane == e1, offs, 0.0), axis=-1, keepdims=True) + rank[:, 0:1]
    d2 = jnp.sum(jnp.where(lane == e2, offs, 0.0), axis=-1, keepdims=True) + rank[:, 1:2]
    dest_ref[...] = _lane_pack(lane, ((0, d1), (1, d2))).astype(jnp.int32)

    ends_col = jnp.broadcast_to(ends, (LANES, LANES)).T
    start = (c * tile).astype(F32)
    n_before = jnp.sum(jnp.where((ends_col <= start) & (r < n_experts), 1.0, 0.0), axis=0, keepdims=True)
    lane1 = lax.broadcasted_iota(jnp.int32, (1, LANES), 1)
    total = jnp.sum(jnp.where(lane1 == n_experts - 1, ends, 0.0), axis=-1, keepdims=True)
    tail = jnp.where(counts > 0.0, ends - tile, -1.0)
    tiles_ref[...] = jnp.zeros_like(tiles_ref)
    tiles_ref[0:1, :] = jnp.minimum(n_before, n_experts - 1.0).astype(jnp.int32)
    tiles_ref[1:2, :] = jnp.broadcast_to(total * (1.0 / tile), (1, LANES)).astype(jnp.int32)
    tiles_ref[2:3, :] = tail.astype(jnp.int32)


def _moe_dispatch_kernel(dest_ref, tails_ref, ntiles_ref, xn_ref, xs_ref, zero_sc, sems, *, tile, n_experts, burst):
    m = xn_ref.shape[0]
    n_tiles_max = xs_ref.shape[0] // tile
    zero_sc[...] = jnp.zeros_like(zero_sc)

    def fill_copy(i):
        return pltpu.make_async_copy(zero_sc, xs_ref.at[pl.ds(pl.multiple_of(i * tile, tile), tile)], sems.at[2])

    def fill(i, _):
        fill_copy(i).start()
        return 0

    def fill_wait(i, _):
        fill_copy(i).wait()
        return 0

    lax.fori_loop(ntiles_ref[0], n_tiles_max, fill, 0)

    def tail_copy(e):
        return pltpu.make_async_copy(zero_sc, xs_ref.at[pl.ds(pl.multiple_of(tails_ref[e], tile), tile)],
                                     sems.at[1])

    def clear(e, _):
        @pl.when(tails_ref[e] >= 0)
        def _():
            tail_copy(e).start()
        return 0

    def clear_wait(e, _):
        @pl.when(tails_ref[e] >= 0)
        def _():
            tail_copy(e).wait()
        return 0

    lax.fori_loop(0, n_experts, clear, 0)
    lax.fori_loop(0, n_experts, clear_wait, 0)

    def row_copy(t, k):
        return pltpu.make_async_copy(xn_ref.at[pl.ds(t, 1)], xs_ref.at[pl.ds(dest_ref[2 * t + k], 1)], sems.at[0])

    def chunk(ci, _):
        def issue(j, _):
            t = ci * burst + j
            row_copy(t, 0).start()
            row_copy(t, 1).start()
            return 0

        def drain(j, _):
            t = ci * burst + j
            row_copy(t, 0).wait()
            row_copy(t, 1).wait()
            return 0

        lax.fori_loop(0, burst, issue, 0)
        lax.fori_loop(0, burst, drain, 0)
        return 0

    lax.fori_loop(0, m // burst, chunk, 0)
    lax.fori_loop(ntiles_ref[0], n_tiles_max, fill_wait, 0)


def _moe_expert_kernel(texp_ref, ntiles_ref, xs_ref, wg_ref, wu_ref, wd_ref, ys_ref):
    @pl.when(pl.program_id(0) < ntiles_ref[0])
    def _():
        x = xs_ref[...].astype(BF16)
        hg = _dot(x, wg_ref[...].astype(BF16))
        hu = _dot(x, wu_ref[...].astype(BF16))
        ys_ref[...] = _dot((_silu(hg) * hu).astype(BF16), wd_ref[...].astype(BF16))


def _moe_combine_kernel(dest_ref, ys_ref, x_ref, info_ref, fw_ref, y_ref, buf_sc, sem):
    i = pl.program_id(0)
    tm = x_ref.shape[0]

    def row_copy(r, k):
        t = i * tm + r
        return pltpu.make_async_copy(ys_ref.at[pl.ds(dest_ref[2 * t + k], 1)], buf_sc.at[k, pl.ds(r, 1)], sem)

    def issue(r, _):
        row_copy(r, 0).start()
        row_copy(r, 1).start()
        return 0

    def drain(r, _):
        row_copy(r, 0).wait()
        row_copy(r, 1).wait()
        return 0

    lax.fori_loop(0, tm, issue, 0)
    lax.fori_loop(0, tm, drain, 0)
    info = info_ref[...]
    y = x_ref[...] + info[:, INFO_G1:INFO_G1 + 1] * buf_sc[0] + info[:, INFO_G2:INFO_G2 + 1] * buf_sc[1]
    y_ref[...] = _rms(y, fw_ref[...])


MOE_TILE = 256
MOE_BURST = 128


def moe_routed_final(x, p, final_w):
    m, d = x.shape
    ne, _, f = p['moe_w_gate'].shape
    tile = min(MOE_TILE, m)
    tm = min(m, 512)
    xn, sel, info = moe_router(x, p['norm_ffn_w'], p['router_coarse_w'], p['router_coarse_b'],
                               p['router_fine_w'], p['router_fine_b'], tm=tm)
    tok = pl.BlockSpec((tm, LANES), lambda i: (i, 0))
    rank, counts = pl.pallas_call(
        _moe_rank_kernel,
        grid=(m // tm,),
        in_specs=[tok, tok],
        out_specs=[tok, pl.BlockSpec((1, LANES), lambda i: (0, 0))],
        out_shape=[jax.ShapeDtypeStruct((m, LANES), F32), jax.ShapeDtypeStruct((1, LANES), F32)],
        scratch_shapes=[pltpu.VMEM((1, LANES), F32)],
        compiler_params=_cparams(("arbitrary",)),
        name="moe_rank",
    )(sel, info)
    n_tiles_max = (2 * m) // tile + ne
    assert n_tiles_max <= LANES
    dest, tiles = pl.pallas_call(
        functools.partial(_moe_plan_kernel, tile=tile, n_experts=ne),
        out_shape=[jax.ShapeDtypeStruct((m, LANES), jnp.int32), jax.ShapeDtypeStruct((8, LANES), jnp.int32)],
        compiler_params=pltpu.CompilerParams(vmem_limit_bytes=VMEM_LIMIT),
        name="moe_plan",
    )(rank, info, counts)
    dest_flat = dest[:, :2].reshape(2 * m)
    tile_expert = tiles[0, :n_tiles_max]
    n_tiles = tiles[1, :1]
    tails = tiles[2, :ne]

    rows = n_tiles_max * tile
    burst = min(MOE_BURST, m)
    xs = pl.pallas_call(
        functools.partial(_moe_dispatch_kernel, tile=tile, n_experts=ne, burst=burst),
        grid_spec=pltpu.PrefetchScalarGridSpec(
            num_scalar_prefetch=3, grid=(1,),
            in_specs=[pl.BlockSpec(memory_space=pl.ANY)],
            out_specs=pl.BlockSpec(memory_space=pl.ANY),
            scratch_shapes=[pltpu.VMEM((tile, d), F32), pltpu.SemaphoreType.DMA((3,))]),
        out_shape=jax.ShapeDtypeStruct((rows, d), F32),
        compiler_params=_cparams(("arbitrary",)),
        name="moe_dispatch",
    )(dest_flat, tails, n_tiles, xn)

    def tile_idx(i, te, nt):
        return jnp.minimum(i, nt[0] - 1)

    ys = pl.pallas_call(
        _moe_expert_kernel,
        grid_spec=pltpu.PrefetchScalarGridSpec(
            num_scalar_prefetch=2, grid=(n_tiles_max,),
            in_specs=[pl.BlockSpec((tile, d), lambda i, te, nt: (tile_idx(i, te, nt), 0)),
                      pl.BlockSpec((None, d, f), lambda i, te, nt: (te[tile_idx(i, te, nt)], 0, 0)),
                      pl.BlockSpec((None, d, f), lambda i, te, nt: (te[tile_idx(i, te, nt)], 0, 0)),
                      pl.BlockSpec((None, f, d), lambda i, te, nt: (te[tile_idx(i, te, nt)], 0, 0))],
            out_specs=pl.BlockSpec((tile, d), lambda i, te, nt: (tile_idx(i, te, nt), 0))),
        out_shape=jax.ShapeDtypeStruct((rows, d), F32),
        input_output_aliases={2: 0},
        compiler_params=_cparams(("arbitrary",)),
        name="moe_experts",
    )(tile_expert, n_tiles, xs, p['moe_w_gate'], p['moe_w_up'], p['moe_w_down'])

    tc = min(m, 256)
    return pl.pallas_call(
        _moe_combine_kernel,
        grid_spec=pltpu.PrefetchScalarGridSpec(
            num_scalar_prefetch=1, grid=(m // tc,),
            in_specs=[pl.BlockSpec(memory_space=pl.ANY),
                      pl.BlockSpec((tc, d), lambda i, dref: (i, 0)),
                      pl.BlockSpec((tc, LANES), lambda i, dref: (i, 0)),
                      pl.BlockSpec((1, d), lambda i, dref: (0, 0))],
            out_specs=pl.BlockSpec((tc, d), lambda i, dref: (i, 0)),
            scratch_shapes=[pltpu.VMEM((2, tc, d), F32), pltpu.SemaphoreType.DMA(())]),
        out_shape=jax.ShapeDtypeStruct((m, d), F32),
        compiler_params=_cparams(("arbitrary",)),
        name="moe_combine",
    )(dest_flat, ys, x, info, final_w.reshape(1, d))


def s5_prompt(h, nb, seq, prep, p):
    g = prep[0].shape[0]
    q, ch = S5_Q, S5_CH
    ds = g * ch
    nchunk = seq // q
    u = h[:, :ds].astype(BF16).reshape(nb, nchunk, q, g, ch)
    u_t = u.transpose(3, 1, 0, 2, 4).reshape(g, nchunk * nb, q * ch)
    y_t, hfin = s5_scan(u_t, prep, nb)
    y = y_t.reshape(g, nchunk, nb, q, ch).transpose(2, 1, 3, 0, 4).reshape(nb * seq, ds)
    out = s5_head(y, h, p['s5_d'], p['s5_glu_w'], p['s5_glu_b'], p['s5_norm_w'], tm=min(512, nb * seq))
    return out, hfin[:, 0].transpose(1, 0, 2), hfin[:, 1].transpose(1, 0, 2)


def s5_sample(h, st_re, st_im, prep, p):
    g = prep[0].shape[0]
    ch = S5_CH
    ds = g * ch
    b = h.shape[0]
    u_t = h[:, :ds].reshape(b, g, ch).transpose(1, 0, 2)
    y_t, n_re, n_im = s5_step(u_t, st_re.transpose(1, 0, 2), st_im.transpose(1, 0, 2), prep,
                              p['s5_c_re'], p['s5_c_im'])
    y = y_t.transpose(1, 0, 2).reshape(b, ds)
    out = s5_head(y, h, p['s5_d'], p['s5_glu_w'], p['s5_glu_b'], p['s5_norm_w'], tm=b)
    return out, n_re.transpose(1, 0, 2), n_im.transpose(1, 0, 2)


def _row_tile(m):
    return min(m, 1024)


def _decoder_layer(x, p, s5_prep, final_w, *, nb, seq, mem_kv, xa_heads, states):
    m, d = x.shape
    g, n = p['s5_a_re'].shape
    d_s5 = g * S5_CH
    nheads = p['m2_a_log'].shape[0]
    d_inner = nheads * M2_HEADDIM
    conv_dim = d_inner + 2 * M2_NGROUPS * M2_DSTATE
    n_main = d_s5 + d_inner + conv_dim
    tm = _row_tile(m)

    w_in = p['w_in']
    w_dt = jnp.pad(w_in[:, n_main:], ((0, 0), (0, LANES - nheads)))
    h, dt_raw = fused_matmul([x], w_in, n_out=n_main, gain=p['norm_mix_w'], side_w=w_dt, tm=tm, tn=512)

    if states is None:
        s5_out, s5_re, s5_im = s5_prompt(h, nb, seq, s5_prep, p)
        m2_out, ssm, conv = ssd_prompt(h, dt_raw, nb, seq, p, d_s5=d_s5, d_inner=d_inner, nheads=nheads)
    else:
        s5_out, s5_re, s5_im = s5_sample(h, states[0], states[1], s5_prep, p)
        m2_out, ssm, conv = ssd_sample(h, dt_raw, states[2], states[3], p, d_s5=d_s5, d_inner=d_inner,
                                       nheads=nheads)
    x1 = fused_matmul([s5_out, m2_out], p['w_out'], n_out=d, res=x, tm=tm, tn=512)

    q = fused_matmul([x1], p['xa_wq'], n_out=d, gain=p['norm_xa_w'], tm=tm, tn=512)
    if states is None:
        mem = mem_kv[0].shape[0] // nb
        o = attention_prompt(q, mem_kv[0], mem_kv[1], nb, seq, mem, xa_heads, tq=min(seq, 512))
    else:
        o = attention_sample(q, mem_kv[0], mem_kv[1])
    x2 = fused_matmul([o], p['xa_wo'], n_out=d, res=x1, tm=tm, tn=512)

    y = moe_routed_final(x2, p, final_w)
    return y, s5_re, s5_im, ssm, conv


def kernel(x_prompt, x_sample, mem_prompt, state_s5_re, state_s5_im, state_ssm, state_conv, cache_mem_k, cache_mem_v, norm_mix_w, w_in, s5_a_re, s5_a_im, s5_log_dt, s5_b_re, s5_b_im, s5_c_re, s5_c_im, s5_d, s5_glu_w, s5_glu_b, s5_norm_w, m2_conv_w, m2_conv_b, m2_dt_bias, m2_a_log, m2_d, m2_norm_w, w_out, norm_xa_w, norm_mem_w, xa_wq, xa_wk, xa_wv, xa_wo, norm_ffn_w, router_coarse_w, router_coarse_b, router_fine_w, router_fine_b, moe_w_gate, moe_w_up, moe_w_down, norm_final_w):
    depth = w_in.shape[0]
    assert depth == 1, "the final norm is fused into the (only) layer"
    per_layer = dict(
        norm_mix_w=norm_mix_w, w_in=w_in, s5_a_re=s5_a_re, s5_a_im=s5_a_im, s5_log_dt=s5_log_dt,
        s5_b_re=s5_b_re, s5_b_im=s5_b_im, s5_c_re=s5_c_re, s5_c_im=s5_c_im, s5_d=s5_d, s5_glu_w=s5_glu_w,
        s5_glu_b=s5_glu_b, s5_norm_w=s5_norm_w, m2_conv_w=m2_conv_w, m2_conv_b=m2_conv_b, m2_dt_bias=m2_dt_bias,
        m2_a_log=m2_a_log, m2_d=m2_d, m2_norm_w=m2_norm_w, w_out=w_out, norm_xa_w=norm_xa_w,
        norm_mem_w=norm_mem_w, xa_wq=xa_wq, xa_wk=xa_wk, xa_wv=xa_wv, xa_wo=xa_wo, norm_ffn_w=norm_ffn_w,
        router_coarse_w=router_coarse_w, router_coarse_b=router_coarse_b, router_fine_w=router_fine_w,
        router_fine_b=router_fine_b, moe_w_gate=moe_w_gate, moe_w_up=moe_w_up, moe_w_down=moe_w_down)
    p = {k: v[0] for k, v in per_layer.items()}
    nb, seq, d = x_prompt.shape
    db, dseq, _ = x_sample.shape
    assert dseq == 1
    mem = mem_prompt.shape[1]
    xa_heads = cache_mem_k.shape[3]

    s5_prep = s5_prepare(p['s5_a_re'], p['s5_a_im'], p['s5_log_dt'], p['s5_b_re'], p['s5_b_im'],
                         p['s5_c_re'], p['s5_c_im'])

    memx = mem_prompt.reshape(nb * mem, d)
    mk = fused_matmul([memx], p['xa_wk'], n_out=d, gain=p['norm_mem_w'], tm=_row_tile(nb * mem), tn=512)
    mv = fused_matmul([memx], p['xa_wv'], n_out=d, gain=p['norm_mem_w'], tm=_row_tile(nb * mem), tn=512)
    yp, p_re, p_im, p_ssm, p_conv = _decoder_layer(
        x_prompt.reshape(nb * seq, d), p, s5_prep, norm_final_w, nb=nb, seq=seq, mem_kv=(mk, mv), xa_heads=xa_heads, states=None)

    ys, s_re, s_im, s_ssm, s_conv = _decoder_layer(
        x_sample.reshape(db, d), p, s5_prep, norm_final_w, nb=db, seq=1,
        mem_kv=(cache_mem_k[0], cache_mem_v[0]), xa_heads=xa_heads,
        states=(state_s5_re[0], state_s5_im[0], state_ssm[0], state_conv[0]))

    kv_shape = (1, nb, mem) + cache_mem_k.shape[3:]
    return (yp.reshape(nb, seq, d), ys.reshape(db, 1, d), p_re[None], p_im[None], p_ssm[None], p_conv[None],
            mk.reshape(kv_shape), mv.reshape(kv_shape), s_re[None], s_im[None], s_ssm[None], s_conv[None])
```

```python
import functools
import math

import jax
import jax.numpy as jnp
from jax import lax
from jax.experimental import pallas as pl
from jax.experimental.pallas import tpu as pltpu

F32 = jnp.float32
BF16 = jnp.bfloat16
RMS_EPS = 1e-6

V7X_VMEM_BYTES = 64 * 1024 * 1024
VMEM_LIMIT = V7X_VMEM_BYTES - 8 * 1024 * 1024
LANES = 128

S5_CH = 16
S5_N = 64
S5_Q = 16
S5_GB = 8
M2_HEADDIM = 64
M2_DSTATE = 128
M2_NGROUPS = 2
M2_CONV = 4
M2_CHUNK = 128
MOE_GROUPS = 4
MOE_PER_GROUP = 8


def _cparams(sem):
    return pltpu.CompilerParams(dimension_semantics=sem, vmem_limit_bytes=VMEM_LIMIT)


def _rms(x, w):
    return x * lax.rsqrt(jnp.mean(x * x, axis=-1, keepdims=True) + RMS_EPS) * w


def _sigmoid(x):
    return 1.0 / (1.0 + jnp.exp(-x))


def _silu(x):
    return x * _sigmoid(x)


def _softplus(x):
    return jnp.maximum(x, 0.0) + jnp.log1p(jnp.exp(-jnp.abs(x)))


def _gelu_tanh(x):
    return 0.5 * x * (1.0 + jnp.tanh(math.sqrt(2.0 / math.pi) * (x + 0.044715 * (x * x * x))))


def _dot(a, b):
    return jnp.dot(a, b, preferred_element_type=F32)


def _dot_nt(a, b):
    return lax.dot_general(a, b, (((1,), (1,)), ((), ())), preferred_element_type=F32)


def _split3(x):
    hi = x.astype(BF16)
    r = x - hi.astype(F32)
    mid = r.astype(BF16)
    lo = (r - mid.astype(F32)).astype(BF16)
    return hi, mid, lo


def _split2(x):
    hi = x.astype(BF16)
    lo = (x - hi.astype(F32)).astype(BF16)
    return hi, lo


def _mm_kernel(*refs, n_lhs, has_gain, has_res, has_side):
    it = iter(refs)
    lhs = [next(it) for _ in range(n_lhs)]
    gain = next(it) if has_gain else None
    ws = [next(it) for _ in range(n_lhs)]
    side_w = next(it) if has_side else None
    res = next(it) if has_res else None
    out = next(it)
    side_out = next(it) if has_side else None
    lhs_bf = next(it)

    @pl.when(pl.program_id(1) == 0)
    def _():
        for i in range(n_lhs):
            x = lhs[i][...]
            if has_gain:
                x = _rms(x, gain[...])
            lhs_bf[i] = x.astype(BF16)
        if has_side:
            side_out[...] = _dot(lhs_bf[0], side_w[...].astype(BF16))

    acc = None
    for i in range(n_lhs):
        p = _dot(lhs_bf[i], ws[i][...].astype(BF16))
        acc = p if acc is None else acc + p
    if has_res:
        acc = acc + res[...]
    out[...] = acc


def fused_matmul(lhs_list, w, *, n_out, gain=None, res=None, side_w=None, tm, tn):
    n_lhs = len(lhs_list)
    m, kp = lhs_list[0].shape
    assert all(a.shape == (m, kp) for a in lhs_list)
    assert w.shape[0] == n_lhs * kp and m % tm == 0 and n_out % tn == 0
    assert gain is None or n_lhs == 1
    grid = (m // tm, n_out // tn)
    in_specs = [pl.BlockSpec((tm, kp), lambda i, j: (i, 0)) for _ in range(n_lhs)]
    args = list(lhs_list)
    if gain is not None:
        in_specs.append(pl.BlockSpec((1, kp), lambda i, j: (0, 0)))
        args.append(gain.reshape(1, kp))
    for p in range(n_lhs):
        in_specs.append(pl.BlockSpec((kp, tn), lambda i, j, p=p: (p, j)))
        args.append(w)
    if side_w is not None:
        in_specs.append(pl.BlockSpec((kp, LANES), lambda i, j: (0, 0)))
        args.append(side_w)
    if res is not None:
        in_specs.append(pl.BlockSpec((tm, tn), lambda i, j: (i, j)))
        args.append(res)
    out_shape = [jax.ShapeDtypeStruct((m, n_out), F32)]
    out_specs = [pl.BlockSpec((tm, tn), lambda i, j: (i, j))]
    if side_w is not None:
        out_shape.append(jax.ShapeDtypeStruct((m, LANES), F32))
        out_specs.append(pl.BlockSpec((tm, LANES), lambda i, j: (i, 0)))
    outs = pl.pallas_call(
        functools.partial(_mm_kernel, n_lhs=n_lhs, has_gain=gain is not None,
                          has_res=res is not None, has_side=side_w is not None),
        grid=grid, in_specs=in_specs, out_specs=out_specs, out_shape=out_shape,
        scratch_shapes=[pltpu.VMEM((n_lhs, tm, kp), BF16)],
        compiler_params=_cparams(("parallel", "arbitrary")),
        name="fused_matmul",
    )(*args)
    return outs if side_w is not None else outs[0]


def _s5_prep_kernel(lre_ref, lim_ref, ldt_ref, btre_ref, btim_ref, cre_ref, cim_ref,
                    tz_ref, wsre_ref, wsim_ref, wcre_ref, wcim_ref, aq_ref, ab_ref, bbt_ref):
    q, ch = S5_Q, S5_CH
    lr = lre_ref[...]
    li = lim_ref[...]
    step = jnp.exp(ldt_ref[...])
    mag = jnp.exp(lr * step)
    ab_re = mag * jnp.cos(li * step)
    ab_im = mag * jnp.sin(li * step)
    den = lr * lr + li * li
    num_re = ab_re - 1.0
    coef_re = (num_re * lr + ab_im * li) / den
    coef_im = (ab_im * lr - num_re * li) / den
    bt_re = btre_ref[...]
    bt_im = btim_ref[...]
    bb_re = coef_re * bt_re - coef_im * bt_im
    bb_im = coef_re * bt_im + coef_im * bt_re
    c_re = cre_ref[...]
    c_im = cim_ref[...]

    pw = [(jnp.ones_like(ab_re), jnp.zeros_like(ab_re))]
    for _ in range(q):
        pr, pi = pw[-1]
        pw.append((pr * ab_re - pi * ab_im, pr * ab_im + pi * ab_re))

    ca_re = [c_re * pr - c_im * pi for pr, pi in pw]
    ca_im = [c_re * pi + c_im * pr for pr, pi in pw]
    wcre_ref[...] = jnp.concatenate(ca_re[1:], axis=0).astype(BF16)
    wcim_ref[...] = jnp.concatenate([-x for x in ca_im[1:]], axis=0).astype(BF16)

    pr_stack = jnp.concatenate(ca_re[:q], axis=0)
    pi_stack = jnp.concatenate(ca_im[:q], axis=0)
    krow = None
    for a, b, sign in ((bb_re, pr_stack, 1.0), (bb_im, pi_stack, -1.0)):
        a_hi, a_lo = _split2(a)
        b_hi, b_lo = _split2(b)
        t = _dot_nt(a_hi, b_hi) + (_dot_nt(a_hi, b_lo) + _dot_nt(a_lo, b_hi))
        krow = sign * t if krow is None else krow + sign * t
    lane = lax.broadcasted_iota(jnp.int32, krow.shape, 1)
    blocks = [krow]
    for s in range(1, q):
        blocks.append(jnp.where(lane >= s * ch, pltpu.roll(krow, s * ch, 1), 0.0))
    tz_ref[...] = jnp.concatenate(blocks, axis=0).astype(BF16)

    ws_re, ws_im = [], []
    for s in range(q):
        pr, pi = pw[q - 1 - s]
        ws_re.append(bb_re * pr - bb_im * pi)
        ws_im.append(bb_re * pi + bb_im * pr)
    wsre_ref[...] = jnp.concatenate(ws_re, axis=0).astype(BF16)
    wsim_ref[...] = jnp.concatenate(ws_im, axis=0).astype(BF16)

    aq_ref[0:1, :] = pw[q][0]
    aq_ref[1:2, :] = pw[q][1]
    ab_ref[0:1, :] = ab_re
    ab_ref[1:2, :] = ab_im
    bbt_ref[0:ch, :] = bb_re
    bbt_ref[ch:2 * ch, :] = bb_im


def s5_prepare(a_re, a_im, log_dt, b_re, b_im, c_re, c_im):
    g, n = a_re.shape
    ch, q = S5_CH, S5_Q
    qc = q * ch
    bt_re = jnp.swapaxes(b_re, 1, 2)
    bt_im = jnp.swapaxes(b_im, 1, 2)

    def per_g(*dims):
        return pl.BlockSpec((None,) + dims, lambda i: (i,) + (0,) * len(dims))

    return pl.pallas_call(
        _s5_prep_kernel,
        grid=(g,),
        in_specs=[per_g(1, n), per_g(1, n), per_g(1, 1), per_g(ch, n), per_g(ch, n), per_g(ch, n), per_g(ch, n)],
        out_specs=[per_g(qc, qc), per_g(qc, n), per_g(qc, n), per_g(qc, n), per_g(qc, n),
                   per_g(2, n), per_g(2, n), per_g(2 * ch, n)],
        out_shape=[jax.ShapeDtypeStruct((g, qc, qc), BF16),
                   jax.ShapeDtypeStruct((g, qc, n), BF16), jax.ShapeDtypeStruct((g, qc, n), BF16),
                   jax.ShapeDtypeStruct((g, qc, n), BF16), jax.ShapeDtypeStruct((g, qc, n), BF16),
                   jax.ShapeDtypeStruct((g, 2, n), F32), jax.ShapeDtypeStruct((g, 2, n), F32),
                   jax.ShapeDtypeStruct((g, 2 * ch, n), F32)],
        compiler_params=_cparams(("parallel",)),
        name="s5_prepare",
    )(a_re.reshape(g, 1, n), a_im.reshape(g, 1, n), log_dt.reshape(g, 1, 1), bt_re, bt_im, c_re, c_im)


def _s5_scan_kernel(h_ref, tz_ref, wsre_ref, wsim_ref, wcre_ref, wcim_ref, aq_ref,
                    y_ref, hfin_ref, xs_sc, u_sc, yg_sc, sre_sc, sim_sc, *, nb, nchunk):
    gb, q, ch = S5_GB, S5_Q, S5_CH
    rows = nb * nchunk
    per_vreg = LANES // ch
    assert gb == per_vreg and q % per_vreg == 0
    slot = lax.broadcasted_iota(jnp.int32, (rows, LANES), 1) // ch

    for s in range(q):
        xs_sc[s] = h_ref[pl.ds(s, rows, stride=q), :]
    for g in range(gb):
        for hf in range(q // per_vreg):
            acc = jnp.zeros((rows, LANES), F32)
            for k in range(per_vreg):
                x = xs_sc[hf * per_vreg + k]
                shift = ((k - g) * ch) % LANES
                acc = jnp.where(slot == k, pltpu.roll(x, shift, 1) if shift else x, acc)
            u_sc[g, :, hf * LANES:(hf + 1) * LANES] = acc.astype(BF16)

    for g in range(gb):
        u = u_sc[g]
        yg_sc[g] = _dot(u, tz_ref[g])
        sre_sc[g] = _dot(u, wsre_ref[g])
        sim_sc[g] = _dot(u, wsim_ref[g])

    ar = [jnp.broadcast_to(aq_ref[g, 0:1, :], (nb, S5_N)) for g in range(gb)]
    ai = [jnp.broadcast_to(aq_ref[g, 1:2, :], (nb, S5_N)) for g in range(gb)]

    def step(c, carry):
        at = pl.ds(c, nb, stride=nchunk)
        new = []
        for g in range(gb):
            hr, hi = carry[g]
            sr = sre_sc[g, at, :]
            si = sim_sc[g, at, :]
            sre_sc[g, at, :] = hr
            sim_sc[g, at, :] = hi
            new.append((ar[g] * hr - ai[g] * hi + sr, ar[g] * hi + ai[g] * hr + si))
        return tuple(new)

    zero = jnp.zeros((nb, S5_N), F32)
    fin = lax.fori_loop(0, nchunk, step, tuple((zero, zero) for _ in range(gb)))
    for g in range(gb):
        hfin_ref[g, 0] = fin[g][0]
        hfin_ref[g, 1] = fin[g][1]
        yg_sc[g] += (_dot_nt(sre_sc[g].astype(BF16), wcre_ref[g])
                     + _dot_nt(sim_sc[g].astype(BF16), wcim_ref[g]))

    for t in range(q):
        hf, k = divmod(t, per_vreg)
        acc = jnp.zeros((rows, LANES), F32)
        for g in range(gb):
            y = yg_sc[g, :, hf * LANES:(hf + 1) * LANES]
            shift = ((g - k) * ch) % LANES
            acc = jnp.where(slot == g, pltpu.roll(y, shift, 1) if shift else y, acc)
        y_ref[pl.ds(t, rows, stride=q), :] = acc


def s5_scan(h, prep, nb, seq):
    tz, ws_re, ws_im, wc_re, wc_im, aq = prep[:6]
    g, qc, _ = tz.shape
    n = S5_N
    gb = S5_GB
    q = S5_Q
    assert gb * S5_CH == LANES and qc == q * S5_CH and seq % q == 0
    nchunk = seq // q
    rows = nb * nchunk
    m = nb * seq

    def blk(*dims):
        return pl.BlockSpec((gb,) + dims, lambda i: (i,) + (0,) * len(dims))

    return pl.pallas_call(
        functools.partial(_s5_scan_kernel, nb=nb, nchunk=nchunk),
        grid=(g // gb,),
        in_specs=[pl.BlockSpec((m, LANES), lambda i: (0, i)),
                  blk(qc, qc), blk(qc, n), blk(qc, n), blk(qc, n), blk(qc, n), blk(2, n)],
        out_specs=[pl.BlockSpec((m, LANES), lambda i: (0, i)), blk(2, nb, n)],
        out_shape=[jax.ShapeDtypeStruct((m, g * S5_CH), F32), jax.ShapeDtypeStruct((g, 2, nb, n), F32)],
        scratch_shapes=[pltpu.VMEM((q, rows, LANES), F32), pltpu.VMEM((gb, rows, qc), BF16),
                        pltpu.VMEM((gb, rows, qc), F32),
                        pltpu.VMEM((gb, rows, n), F32), pltpu.VMEM((gb, rows, n), F32)],
        compiler_params=_cparams(("parallel",)),
        name="s5_scan",
    )(h, tz, ws_re, ws_im, wc_re, wc_im, aq)


def _s5_step_kernel(u_ref, hre_ref, him_ref, ab_ref, bbt_ref, cre_ref, cim_ref, y_ref, ore_ref, oim_ref):
    gb = u_ref.shape[0]
    ch = S5_CH
    for g in range(gb):
        u = u_ref[g].astype(BF16)
        bb_re = bbt_ref[g, 0:ch, :].astype(BF16)
        bb_im = bbt_ref[g, ch:2 * ch, :].astype(BF16)
        ar = ab_ref[g, 0:1, :]
        ai = ab_ref[g, 1:2, :]
        hr0 = hre_ref[g]
        hi0 = him_ref[g]
        hr = _dot(u, bb_re) + (ar * hr0 - ai * hi0)
        hi = _dot(u, bb_im) + (ar * hi0 + ai * hr0)
        ore_ref[g] = hr
        oim_ref[g] = hi
        y_ref[g] = (_dot_nt(hr.astype(BF16), cre_ref[g].astype(BF16))
                    - _dot_nt(hi.astype(BF16), cim_ref[g].astype(BF16)))


def s5_step(u_t, h_re, h_im, prep, c_re, c_im):
    ab, bbt = prep[6], prep[7]
    g, b, ch = u_t.shape
    n = S5_N
    gb = S5_GB

    def blk(*dims):
        return pl.BlockSpec((gb,) + dims, lambda i: (i,) + (0,) * len(dims))

    return pl.pallas_call(
        _s5_step_kernel,
        grid=(g // gb,),
        in_specs=[blk(b, ch), blk(b, n), blk(b, n), blk(2, n), blk(2 * ch, n), blk(ch, n), blk(ch, n)],
        out_specs=[blk(b, ch), blk(b, n), blk(b, n)],
        out_shape=[jax.ShapeDtypeStruct((g, b, ch), F32), jax.ShapeDtypeStruct((g, b, n), F32),
                   jax.ShapeDtypeStruct((g, b, n), F32)],
        compiler_params=_cparams(("parallel",)),
        name="s5_step",
    )(u_t, h_re, h_im, ab, bbt, c_re, c_im)


def _s5_head_kernel(y_ref, u_ref, d_ref, w_ref, b_ref, nw_ref, o_ref):
    y = y_ref[...] + d_ref[...] * u_ref[...]
    g = _gelu_tanh(y)
    gate = _sigmoid(_dot(g.astype(BF16), w_ref[...].astype(BF16)) + b_ref[...])
    o_ref[...] = _rms(g * gate, nw_ref[...])


def s5_head(y, h, d, glu_w, glu_b, norm_w, *, tm):
    m, ds = y.shape
    row = lambda a: a.reshape(1, ds)
    vec = pl.BlockSpec((1, ds), lambda i: (0, 0))
    return pl.pallas_call(
        _s5_head_kernel,
        grid=(m // tm,),
        in_specs=[pl.BlockSpec((tm, ds), lambda i: (i, 0)), pl.BlockSpec((tm, ds), lambda i: (i, 0)), vec,
                  pl.BlockSpec((ds, ds), lambda i: (0, 0)), vec, vec],
        out_specs=pl.BlockSpec((tm, ds), lambda i: (i, 0)),
        out_shape=jax.ShapeDtypeStruct((m, ds), F32),
        compiler_params=_cparams(("parallel",)),
        name="s5_head",
    )(y, h, row(d), glu_w, row(glu_b), row(norm_w))


def _pair_select(first, col0, col1, shape):
    return jnp.where(first, jnp.broadcast_to(col0, shape), jnp.broadcast_to(col1, shape))


def _ssd_chunk_kernel(*refs, d_inner, nheads, n_xparts):
    xparts = refs[:n_xparts]
    (z_ref, dt_ref, cw_ref, cb_ref, dtb_ref, alog_ref, dvec_ref, nw_ref,
     out_ref, ssm_ref, conv_ref, state_sc, xpad_sc, y_sc) = refs[n_xparts:]
    c = pl.program_id(1)
    q = M2_CHUNK
    hp = M2_HEADDIM
    ns = M2_DSTATE
    heads_per_group = nheads // M2_NGROUPS
    halo = 8

    @pl.when(c == 0)
    def _():
        state_sc[...] = jnp.zeros_like(state_sc)
        xpad_sc[0:halo, :] = jnp.zeros((halo, xpad_sc.shape[1]), F32)

    wpart = xparts[0].shape[1]
    for i, xr in enumerate(xparts):
        xpad_sc[halo:halo + q, i * wpart:(i + 1) * wpart] = xr[...]
    cw = cw_ref[...]
    conv = cb_ref[...] + cw[M2_CONV - 1:M2_CONV, :] * xpad_sc[halo:halo + q, :]
    for k in range(1, M2_CONV):
        conv = conv + cw[M2_CONV - 1 - k:M2_CONV - k, :] * xpad_sc[halo - k:halo - k + q, :]
    xpad_sc[0:halo, :] = xpad_sc[q:q + halo, :]
    xc = _silu(conv)

    dt = _softplus(dt_ref[...] + dtb_ref[...])
    a = -jnp.exp(alog_ref[...])
    da = dt * a
    row = lax.broadcasted_iota(jnp.int32, (q, q), 0)
    col = lax.broadcasted_iota(jnp.int32, (q, q), 1)
    causal = row >= col
    tri = jnp.where(causal, 1.0, 0.0).astype(BF16)
    d_hi, d_mid, d_lo = _split3(da)
    acum = _dot(tri, d_hi) + (_dot(tri, d_mid) + _dot(tri, d_lo))
    acum_t = acum.T
    alast = acum[q - 1:q, :]
    first = col < hp
    first_rows = row < hp

    for pr in range(nheads // 2):
        grp = (2 * pr) // heads_per_group
        b_bf = xc[:, d_inner + grp * ns:d_inner + (grp + 1) * ns].astype(BF16)
        c_bf = xc[:, d_inner + (M2_NGROUPS + grp) * ns:d_inner + (M2_NGROUPS + grp + 1) * ns].astype(BF16)
        cb = _dot_nt(c_bf, b_bf)
        xpair = xc[:, pr * 2 * hp:(pr + 1) * 2 * hp]
        h0, h1 = 2 * pr, 2 * pr + 1
        acol = [acum[:, h:h + 1] for h in (h0, h1)]
        m = []
        for k, h in enumerate((h0, h1)):
            seg = jnp.broadcast_to(acol[k], (q, q)) - jnp.broadcast_to(acum_t[h:h + 1, :], (q, q))
            lmat = jnp.exp(jnp.where(causal, seg, -1e30))
            m.append((cb * lmat).astype(BF16))
        dtp = _pair_select(first, dt[:, h0:h0 + 1], dt[:, h1:h1 + 1], (q, q))
        xdt = xpair * dtp
        xdt_bf = xdt.astype(BF16)
        y_diag = jnp.where(first, _dot(m[0], xdt_bf), _dot(m[1], xdt_bf))
        dec_end = _pair_select(first, jnp.exp(alast[:, h0:h0 + 1] - acol[0]),
                               jnp.exp(alast[:, h1:h1 + 1] - acol[1]), (q, q))
        xw_t = (xdt * dec_end).T.astype(BF16)
        chunk_state = _dot(xw_t, b_bf)
        rows = pl.ds(pr * 2 * hp, 2 * hp)
        prev = state_sc[rows, :]
        y_off = _dot_nt(c_bf, prev.astype(BF16)) * _pair_select(first, jnp.exp(acol[0]), jnp.exp(acol[1]), (q, q))
        sdec = jnp.where(first_rows, jnp.broadcast_to(jnp.exp(alast[:, h0:h0 + 1]), (q, q)),
                         jnp.broadcast_to(jnp.exp(alast[:, h1:h1 + 1]), (q, q)))
        state_sc[rows, :] = prev * sdec + chunk_state
        y_sc[:, pr * 2 * hp:(pr + 1) * 2 * hp] = y_diag + y_off + dvec_ref[:, pr * 2 * hp:(pr + 1) * 2 * hp] * xpair

    out_ref[...] = _rms(y_sc[...] * _silu(z_ref[...]), nw_ref[...])

    @pl.when(c == pl.num_programs(1) - 1)
    def _():
        ssm_ref[...] = state_sc[...]
        conv_ref[...] = xpad_sc[halo + q - (M2_CONV - 1):halo + q, :]


def ssd_prompt(h, dt_raw, nb, seq, p, *, d_s5, d_inner, nheads):
    q = M2_CHUNK
    nc = seq // q
    conv_dim = d_inner + 2 * M2_NGROUPS * M2_DSTATE
    xw = 512
    xoff = d_s5 + d_inner
    assert d_s5 % d_inner == 0 and xoff % xw == 0 and conv_dim % xw == 0
    assert M2_CHUNK == 2 * M2_HEADDIM == M2_DSTATE == LANES
    zblk = d_s5 // d_inner
    n_xparts = conv_dim // xw
    m = nb * seq
    pad = lambda v: jnp.pad(v, (0, LANES - v.shape[0])).reshape(1, LANES)
    dvec = jnp.repeat(p['m2_d'], M2_HEADDIM).reshape(1, d_inner)
    vec = lambda n: pl.BlockSpec((1, n), lambda b, c: (0, 0))
    tok = lambda w, j: pl.BlockSpec((q, w), lambda b, c, j=j: (b * nc + c, j))
    out, ssm, conv = pl.pallas_call(
        functools.partial(_ssd_chunk_kernel, d_inner=d_inner, nheads=nheads, n_xparts=n_xparts),
        grid=(nb, nc),
        in_specs=[tok(xw, xoff // xw + i) for i in range(n_xparts)] + [tok(d_inner, zblk), tok(LANES, 0),
                  pl.BlockSpec((M2_CONV, conv_dim), lambda b, c: (0, 0)), vec(conv_dim), vec(LANES), vec(LANES),
                  vec(d_inner), vec(d_inner)],
        out_specs=[tok(d_inner, 0),
                   pl.BlockSpec((None, nheads * M2_HEADDIM, M2_DSTATE), lambda b, c: (b, 0, 0)),
                   pl.BlockSpec((None, M2_CONV - 1, conv_dim), lambda b, c: (b, 0, 0))],
        out_shape=[jax.ShapeDtypeStruct((m, d_inner), F32),
                   jax.ShapeDtypeStruct((nb, nheads * M2_HEADDIM, M2_DSTATE), F32),
                   jax.ShapeDtypeStruct((nb, M2_CONV - 1, conv_dim), F32)],
        scratch_shapes=[pltpu.VMEM((nheads * M2_HEADDIM, M2_DSTATE), F32),
                        pltpu.VMEM((q + 8, conv_dim), F32),
                        pltpu.VMEM((q, d_inner), F32)],
        compiler_params=_cparams(("parallel", "arbitrary")),
        name="ssd_chunk",
    )(*([h] * n_xparts), h, dt_raw, p['m2_conv_w'], p['m2_conv_b'].reshape(1, conv_dim), pad(p['m2_dt_bias']),
      pad(p['m2_a_log']), dvec, p['m2_norm_w'].reshape(1, d_inner))
    return out, ssm.reshape(nb, nheads, M2_HEADDIM, M2_DSTATE), conv


SSD_STEP_SEQS = 4


def _ssd_step_kernel(*refs, d_inner, nheads, n_xparts):
    xparts = refs[:n_xparts]
    (z_ref, dt_ref, cs0_ref, cs1_ref, cs2_ref, cw_ref, cb_ref, dtb_ref, alog_ref, dvec_ref, nw_ref, st_ref,
     out_ref, so_ref, lhs_sc, bfull_sc, ct_sc, yt_sc, xs_sc) = refs[n_xparts:]
    i = pl.program_id(0)
    nb = z_ref.shape[0]
    ns = M2_DSTATE
    rows_g = (nheads // M2_NGROUPS) * M2_HEADDIM

    @pl.when(i == 0)
    def _():
        cw = cw_ref[...]
        xbc = jnp.concatenate([xr[...] for xr in xparts], axis=1)
        conv = (cb_ref[...] + cw[3:4, :] * xbc + cw[2:3, :] * cs2_ref[...]
                + cw[1:2, :] * cs1_ref[...] + cw[0:1, :] * cs0_ref[...])
        xc = _silu(conv)
        dt = _softplus(dt_ref[...] + dtb_ref[...])
        dec = jnp.exp(dt * (-jnp.exp(alog_ref[...])))
        hrow = lax.broadcasted_iota(jnp.int32, (LANES, d_inner), 0)
        hcol = lax.broadcasted_iota(jnp.int32, (LANES, d_inner), 1)
        expand = jnp.where(hcol // M2_HEADDIM == hrow, 1.0, 0.0).astype(BF16)

        def expand_heads(v):
            a, b_, c = _split3(v)
            return _dot(a, expand) + (_dot(b_, expand) + _dot(c, expand))

        xs = xc[:, :d_inner]
        xs_sc[...] = xs
        xdt_t = (xs * expand_heads(dt)).T
        d_hi, d_mid, d_lo = _split3(expand_heads(dec).T)
        for g in range(M2_NGROUPS):
            r = slice(g * rows_g, (g + 1) * rows_g)
            lhs_sc[g] = jnp.concatenate([xdt_t[r].astype(BF16), d_hi[r], d_mid[r], d_lo[r]], axis=1)
            b_g = xc[:, d_inner + g * ns:d_inner + (g + 1) * ns]
            bfull_sc[g] = jnp.concatenate([b_g, jnp.zeros_like(b_g)], axis=1)
            c_g = xc[:, d_inner + (M2_NGROUPS + g) * ns:d_inner + (M2_NGROUPS + g + 1) * ns]
            ct_sc[g] = c_g.T
        yt_sc[...] = jnp.zeros_like(yt_sc)

    row_id = lax.broadcasted_iota(jnp.int32, (nb, 2 * ns), 0)
    lane_id = lax.broadcasted_iota(jnp.int32, (nb, 2 * ns), 1)
    col_id = lax.broadcasted_iota(jnp.int32, (ns, nb), 1)
    for j in range(st_ref.shape[0]):
        b = i * st_ref.shape[0] + j
        r_bot = jnp.where((row_id == b) & (lane_id >= ns), 1.0, 0.0).astype(BF16)
        for g in range(M2_NGROUPS):
            r = pl.ds(g * rows_g, rows_g)
            r_top = jnp.where(row_id == b, bfull_sc[g], 0.0).astype(BF16)
            rhs = jnp.concatenate([r_top, r_bot, r_bot, r_bot], axis=0)
            o = _dot(lhs_sc[g], rhs)
            hnew = st_ref[j, r, :] * o[:, ns:] + o[:, :ns]
            so_ref[j, r, :] = hnew
            cm = jnp.where(col_id == b, ct_sc[g], 0.0).astype(BF16)
            yt_sc[r, :] += _dot(hnew.astype(BF16), cm)

    @pl.when(i == pl.num_programs(0) - 1)
    def _():
        y = yt_sc[...].T + dvec_ref[...] * xs_sc[...]
        out_ref[...] = _rms(y * _silu(z_ref[...]), nw_ref[...])


def ssd_sample(h, dt_raw, state, conv_state, p, *, d_s5, d_inner, nheads):
    nb = h.shape[0]
    conv_dim = d_inner + 2 * M2_NGROUPS * M2_DSTATE
    xw = 512
    xoff = d_s5 + d_inner
    assert nb == LANES and M2_DSTATE == LANES and M2_CONV == 4
    assert xoff % xw == 0 and conv_dim % xw == 0 and d_s5 % d_inner == 0 and nb % SSD_STEP_SEQS == 0
    n_xparts = conv_dim // xw
    rows = nheads * M2_HEADDIM
    rows_g = rows // M2_NGROUPS
    pad = lambda v: jnp.pad(v, (0, LANES - v.shape[0])).reshape(1, LANES)
    dvec = jnp.repeat(p['m2_d'], M2_HEADDIM).reshape(1, d_inner)
    full = lambda a, b, j=0: pl.BlockSpec((a, b), lambda i, j=j: (0, j))
    st_spec = pl.BlockSpec((SSD_STEP_SEQS, rows, M2_DSTATE), lambda i: (i, 0, 0))
    out, new_state = pl.pallas_call(
        functools.partial(_ssd_step_kernel, d_inner=d_inner, nheads=nheads, n_xparts=n_xparts),
        grid=(nb // SSD_STEP_SEQS,),
        in_specs=[full(nb, xw, xoff // xw + k) for k in range(n_xparts)]
        + [full(nb, d_inner, d_s5 // d_inner), full(nb, LANES)]
        + [full(nb, conv_dim)] * 3
        + [full(M2_CONV, conv_dim), full(1, conv_dim), full(1, LANES), full(1, LANES), full(1, d_inner),
           full(1, d_inner), st_spec],
        out_specs=[full(nb, d_inner), st_spec],
        out_shape=[jax.ShapeDtypeStruct((nb, d_inner), F32), jax.ShapeDtypeStruct((nb, rows, M2_DSTATE), F32)],
        scratch_shapes=[pltpu.VMEM((M2_NGROUPS, rows_g, 4 * nb), BF16),
                        pltpu.VMEM((M2_NGROUPS, nb, 2 * M2_DSTATE), F32),
                        pltpu.VMEM((M2_NGROUPS, M2_DSTATE, nb), F32),
                        pltpu.VMEM((rows, nb), F32),
                        pltpu.VMEM((nb, d_inner), F32)],
        compiler_params=_cparams(("arbitrary",)),
        name="ssd_step",
    )(*([h] * n_xparts), h, dt_raw, conv_state[:, 0], conv_state[:, 1], conv_state[:, 2],
      p['m2_conv_w'], p['m2_conv_b'].reshape(1, conv_dim), pad(p['m2_dt_bias']), pad(p['m2_a_log']),
      dvec, p['m2_norm_w'].reshape(1, d_inner), state.reshape(nb, rows, M2_DSTATE))
    xbc = lax.slice_in_dim(h, xoff, xoff + conv_dim, axis=1)
    new_conv = jnp.concatenate([conv_state[:, 1:], xbc[:, None, :]], axis=1)
    return out, new_state.reshape(state.shape), new_conv


def _softmax_rows(s):
    e = jnp.exp(s - jnp.max(s, axis=-1, keepdims=True))
    return e / jnp.sum(e, axis=-1, keepdims=True)


def _attn_kernel(q_ref, k_ref, v_ref, o_ref, *, scale):
    s = _dot_nt(q_ref[...].astype(BF16), k_ref[...].astype(BF16)) * scale
    o_ref[...] = _dot(_softmax_rows(s).astype(BF16), v_ref[...].astype(BF16))


def attention_prompt(q, k, v, nb, seq, mem, heads, *, tq):
    d = q.shape[1]
    hd = d // heads
    nq = seq // tq
    kv_spec = pl.BlockSpec((mem, hd), lambda b, h, i: (b, h))
    q_spec = pl.BlockSpec((tq, hd), lambda b, h, i: (b * nq + i, h))
    return pl.pallas_call(
        functools.partial(_attn_kernel, scale=hd ** -0.5),
        grid=(nb, heads, nq),
        in_specs=[q_spec, kv_spec, kv_spec],
        out_specs=q_spec,
        out_shape=jax.ShapeDtypeStruct(q.shape, F32),
        compiler_params=_cparams(("parallel", "parallel", "parallel")),
        name="attention_prompt",
    )(q, k, v)


ATTN_STEP_SEQS = 2


def _attn_step_kernel(q_ref, k_ref, v_ref, o_ref, *, scale):
    for j in range(q_ref.shape[0]):
        s = jnp.sum(k_ref[j] * q_ref[j], axis=-1, keepdims=True) * scale
        e = jnp.exp(s - jnp.max(s, axis=0, keepdims=True))
        att = e / jnp.sum(e, axis=0, keepdims=True)
        o_ref[j] = jnp.sum(att * v_ref[j], axis=0)


def attention_sample(q, k_cache, v_cache):
    b, mem, heads, hd = k_cache.shape
    nseq = ATTN_STEP_SEQS
    q_spec = pl.BlockSpec((nseq, heads, hd), lambda i: (i, 0, 0))
    kv_spec = pl.BlockSpec((nseq, mem, heads, hd), lambda i: (i, 0, 0, 0))
    out = pl.pallas_call(
        functools.partial(_attn_step_kernel, scale=hd ** -0.5),
        grid=(b // nseq,),
        in_specs=[q_spec, kv_spec, kv_spec],
        out_specs=q_spec,
        out_shape=jax.ShapeDtypeStruct((b, heads, hd), F32),
        compiler_params=_cparams(("parallel",)),
        name="attention_step",
    )(q.reshape(b, heads, hd), k_cache, v_cache)
    return out.reshape(b, heads * hd)


NEG = -1e30


INFO_G1, INFO_G2, INFO_E1, INFO_E2 = 0, 1, 2, 3


def _lane_pack(lane, values):
    out = 0.0
    for k, v in values:
        out = jnp.where(lane == k, v, out)
    return out


def _router_kernel(x_ref, nw_ref, wr_ref, br_ref, xn_ref, sel_ref, info_ref, *, n_experts):
    xn = _rms(x_ref[...], nw_ref[...])
    xn_ref[...] = xn
    x_hi, x_lo = _split2(xn)
    w_hi, w_lo = _split2(wr_ref[...])
    logits = _dot(x_hi, w_hi) + (_dot(x_hi, w_lo) + _dot(x_lo, w_hi)) + br_ref[...]
    lane = lax.broadcasted_iota(jnp.int32, logits.shape, 1)
    big = jnp.int32(2 ** 30)
    is_c = (lane >= n_experts) & (lane < n_experts + MOE_GROUPS)
    lc = jnp.where(is_c, logits, NEG)
    cmax = jnp.max(lc, axis=-1, keepdims=True)
    gsel = jnp.min(jnp.where(lc == cmax, lane, big), axis=-1, keepdims=True) - n_experts
    gate_c = 1.0 / jnp.sum(jnp.where(is_c, jnp.exp(lc - cmax), 0.0), axis=-1, keepdims=True)
    in_group = (lane < n_experts) & (lane // MOE_PER_GROUP == gsel)
    lf = jnp.where(in_group, logits, NEG)
    t1 = jnp.max(lf, axis=-1, keepdims=True)
    i1 = jnp.min(jnp.where(lf == t1, lane, big), axis=-1, keepdims=True)
    lf2 = jnp.where(lane == i1, NEG, lf)
    t2 = jnp.max(lf2, axis=-1, keepdims=True)
    i2 = jnp.min(jnp.where(lf2 == t2, lane, big), axis=-1, keepdims=True)
    r = jnp.exp(t2 - t1)
    g1 = gate_c / (1.0 + r)
    g2 = gate_c * r / (1.0 + r)
    sel_ref[...] = jnp.where((lane == i1) | (lane == i2), 1.0, 0.0)
    info_ref[...] = _lane_pack(lane, ((INFO_G1, g1), (INFO_G2, g2), (INFO_E1, i1.astype(F32)),
                                      (INFO_E2, i2.astype(F32))))


def moe_router(x, norm_w, w_coarse, b_coarse, w_fine, b_fine, *, tm):
    m, d = x.shape
    e = w_fine.shape[1]
    padw = LANES - e - MOE_GROUPS
    wr = jnp.concatenate([w_fine, w_coarse, jnp.zeros((d, padw), F32)], axis=1)
    br = jnp.concatenate([b_fine, b_coarse, jnp.zeros((padw,), F32)]).reshape(1, LANES)
    tok = pl.BlockSpec((tm, LANES), lambda i: (i, 0))
    return pl.pallas_call(
        functools.partial(_router_kernel, n_experts=e),
        grid=(m // tm,),
        in_specs=[pl.BlockSpec((tm, d), lambda i: (i, 0)), pl.BlockSpec((1, d), lambda i: (0, 0)),
                  pl.BlockSpec((d, LANES), lambda i: (0, 0)), pl.BlockSpec((1, LANES), lambda i: (0, 0))],
        out_specs=[pl.BlockSpec((tm, d), lambda i: (i, 0)), tok, tok],
        out_shape=[jax.ShapeDtypeStruct((m, d), F32), jax.ShapeDtypeStruct((m, LANES), F32),
                   jax.ShapeDtypeStruct((m, LANES), F32)],
        compiler_params=_cparams(("parallel",)),
        name="moe_router",
    )(x, norm_w.reshape(1, d), wr, br)


def _moe_rank_kernel(sel_ref, info_ref, rank_ref, counts_ref, carry_sc):
    i = pl.program_id(0)

    @pl.when(i == 0)
    def _():
        carry_sc[...] = jnp.zeros_like(carry_sc)

    sel = sel_ref[...]
    tm = sel.shape[0]
    row = lax.broadcasted_iota(jnp.int32, (tm, tm), 0)
    col = lax.broadcasted_iota(jnp.int32, (tm, tm), 1)
    before = jnp.where(row > col, 1.0, 0.0).astype(BF16)
    rank = _dot(before, sel.astype(BF16)) + carry_sc[...]
    info = info_ref[...]
    lane = lax.broadcasted_iota(jnp.int32, sel.shape, 1)
    e1 = info[:, INFO_E1:INFO_E1 + 1].astype(jnp.int32)
    e2 = info[:, INFO_E2:INFO_E2 + 1].astype(jnp.int32)
    r1 = jnp.sum(jnp.where(lane == e1, rank, 0.0), axis=-1, keepdims=True)
    r2 = jnp.sum(jnp.where(lane == e2, rank, 0.0), axis=-1, keepdims=True)
    rank_ref[...] = _lane_pack(lane, ((0, r1), (1, r2)))
    carry_sc[...] += jnp.sum(sel, axis=0, keepdims=True)

    @pl.when(i == pl.num_programs(0) - 1)
    def _():
        counts_ref[...] = carry_sc[...]


def _moe_plan_kernel(rank_ref, info_ref, counts_ref, dest_ref, tiles_ref, *, tile, n_experts):
    counts = counts_ref[...]
    ntile_e = jnp.floor((counts + (tile - 1)) * (1.0 / tile))
    padded = jnp.broadcast_to(ntile_e * tile, (8, LANES))
    r = lax.broadcasted_iota(jnp.int32, (LANES, LANES), 0)
    c = lax.broadcasted_iota(jnp.int32, (LANES, LANES), 1)
    lower = jnp.where(r < c, 1.0, 0.0).astype(BF16)
    p_hi, p_mid, p_lo = _split3(padded)
    offs = (_dot(p_hi, lower) + (_dot(p_mid, lower) + _dot(p_lo, lower)))[0:1, :]
    ends = offs + padded[0:1, :]

    info = info_ref[...]
    rank = rank_ref[...]
    lane = lax.broadcasted_iota(jnp.int32, info.shape, 1)
    e1 = info[:, INFO_E1:INFO_E1 + 1].astype(jnp.int32)
    e2 = info[:, INFO_E2:INFO_E2 + 1].astype(jnp.int32)
    d1 = jnp.sum(jnp.where(lane == e1, offs, 0.0), axis=-1, keepdims=True) + rank[:, 0:1]
    d2 = jnp.sum(jnp.where(lane == e2, offs, 0.0), axis=-1, keepdims=True) + rank[:, 1:2]
    dest_ref[...] = _lane_pack(lane, ((0, d1), (1, d2))).astype(jnp.int32)

    ends_col = jnp.broadcast_to(ends, (LANES, LANES)).T
    start = (c * tile).astype(F32)
    n_before = jnp.sum(jnp.where((ends_col <= start) & (r < n_experts), 1.0, 0.0), axis=0, keepdims=True)
    lane1 = lax.broadcasted_iota(jnp.int32, (1, LANES), 1)
    total = jnp.sum(jnp.where(lane1 == n_experts - 1, ends, 0.0), axis=-1, keepdims=True)
    tail = jnp.where(counts > 0.0, ends - tile, -1.0)
    tiles_ref[...] = jnp.zeros_like(tiles_ref)
    tiles_ref[0:1, :] = jnp.minimum(n_before, n_experts - 1.0).astype(jnp.int32)
    tiles_ref[1:2, :] = jnp.broadcast_to(total * (1.0 / tile), (1, LANES)).astype(jnp.int32)
    tiles_ref[2:3, :] = tail.astype(jnp.int32)


def _moe_dispatch_kernel(dest_ref, tails_ref, ntiles_ref, xn_ref, xs_ref, zero_sc, sems, *, tile, n_experts):
    i = pl.program_id(0)
    tm = xn_ref.shape[0]
    n_tiles_max = xs_ref.shape[0] // tile

    def fill_copy(j):
        return pltpu.make_async_copy(zero_sc, xs_ref.at[pl.ds(pl.multiple_of(j * tile, tile), tile)], sems.at[2])

    def tail_copy(e):
        return pltpu.make_async_copy(zero_sc, xs_ref.at[pl.ds(pl.multiple_of(tails_ref[e], tile), tile)],
                                     sems.at[1])

    @pl.when(i == 0)
    def _():
        zero_sc[...] = jnp.zeros_like(zero_sc)

        def fill(j, _):
            fill_copy(j).start()
            return 0

        def clear(e, _):
            @pl.when(tails_ref[e] >= 0)
            def _():
                tail_copy(e).start()
            return 0

        def clear_wait(e, _):
            @pl.when(tails_ref[e] >= 0)
            def _():
                tail_copy(e).wait()
            return 0

        lax.fori_loop(ntiles_ref[0], n_tiles_max, fill, 0)
        lax.fori_loop(0, n_experts, clear, 0)
        lax.fori_loop(0, n_experts, clear_wait, 0)

    def row_copy(r, k):
        return pltpu.make_async_copy(xn_ref.at[pl.ds(r, 1)],
                                     xs_ref.at[pl.ds(dest_ref[2 * (i * tm + r) + k], 1)], sems.at[0])

    def issue(r, _):
        row_copy(r, 0).start()
        row_copy(r, 1).start()
        return 0

    def drain(r, _):
        row_copy(r, 0).wait()
        row_copy(r, 1).wait()
        return 0

    lax.fori_loop(0, tm, issue, 0)
    lax.fori_loop(0, tm, drain, 0)

    @pl.when(i == pl.num_programs(0) - 1)
    def _():
        def fill_wait(j, _):
            fill_copy(j).wait()
            return 0

        lax.fori_loop(ntiles_ref[0], n_tiles_max, fill_wait, 0)


def _moe_expert_kernel(texp_ref, ntiles_ref, xs_ref, wg_ref, wu_ref, wd_ref, ys_ref):
    @pl.when(pl.program_id(0) < ntiles_ref[0])
    def _():
        x = xs_ref[...].astype(BF16)
        hg = _dot(x, wg_ref[...].astype(BF16))
        hu = _dot(x, wu_ref[...].astype(BF16))
        ys_ref[...] = _dot((_silu(hg) * hu).astype(BF16), wd_ref[...].astype(BF16))


def _moe_combine_kernel(dest_ref, ys_ref, x_ref, info_ref, fw_ref, y_ref, buf_sc, sem):
    i = pl.program_id(0)
    tm = x_ref.shape[0]

    def row_copy(r, k):
        t = i * tm + r
        return pltpu.make_async_copy(ys_ref.at[pl.ds(dest_ref[2 * t + k], 1)], buf_sc.at[k, pl.ds(r, 1)], sem)

    def issue(r, _):
        row_copy(r, 0).start()
        row_copy(r, 1).start()
        return 0

    def drain(r, _):
        row_copy(r, 0).wait()
        row_copy(r, 1).wait()
        return 0

    lax.fori_loop(0, tm, issue, 0)
    lax.fori_loop(0, tm, drain, 0)
    info = info_ref[...]
    y = x_ref[...] + info[:, INFO_G1:INFO_G1 + 1] * buf_sc[0] + info[:, INFO_G2:INFO_G2 + 1] * buf_sc[1]
    y_ref[...] = _rms(y, fw_ref[...])


MOE_TILE = 256
MOE_TOKENS_PER_STEP = 256


def moe_routed_final(x, p, final_w):
    m, d = x.shape
    ne, _, f = p['moe_w_gate'].shape
    tile = min(MOE_TILE, m)
    tm = min(m, 512)
    xn, sel, info = moe_router(x, p['norm_ffn_w'], p['router_coarse_w'], p['router_coarse_b'],
                               p['router_fine_w'], p['router_fine_b'], tm=tm)
    tok = pl.BlockSpec((tm, LANES), lambda i: (i, 0))
    rank, counts = pl.pallas_call(
        _moe_rank_kernel,
        grid=(m // tm,),
        in_specs=[tok, tok],
        out_specs=[tok, pl.BlockSpec((1, LANES), lambda i: (0, 0))],
        out_shape=[jax.ShapeDtypeStruct((m, LANES), F32), jax.ShapeDtypeStruct((1, LANES), F32)],
        scratch_shapes=[pltpu.VMEM((1, LANES), F32)],
        compiler_params=_cparams(("arbitrary",)),
        name="moe_rank",
    )(sel, info)
    n_tiles_max = (2 * m) // tile + ne
    assert n_tiles_max <= LANES
    dest, tiles = pl.pallas_call(
        functools.partial(_moe_plan_kernel, tile=tile, n_experts=ne),
        out_shape=[jax.ShapeDtypeStruct((m, LANES), jnp.int32), jax.ShapeDtypeStruct((8, LANES), jnp.int32)],
        compiler_params=pltpu.CompilerParams(vmem_limit_bytes=VMEM_LIMIT),
        name="moe_plan",
    )(rank, info, counts)
    dest_flat = dest[:, :2].reshape(2 * m)
    tile_expert = tiles[0, :n_tiles_max]
    n_tiles = tiles[1, :1]
    tails = tiles[2, :ne]

    rows = n_tiles_max * tile
    tc = min(m, MOE_TOKENS_PER_STEP)
    xs = pl.pallas_call(
        functools.partial(_moe_dispatch_kernel, tile=tile, n_experts=ne),
        grid_spec=pltpu.PrefetchScalarGridSpec(
            num_scalar_prefetch=3, grid=(m // tc,),
            in_specs=[pl.BlockSpec((tc, d), lambda i, *_: (i, 0))],
            out_specs=pl.BlockSpec(memory_space=pl.ANY),
            scratch_shapes=[pltpu.VMEM((tile, d), F32), pltpu.SemaphoreType.DMA((3,))]),
        out_shape=jax.ShapeDtypeStruct((rows, d), F32),
        compiler_params=_cparams(("arbitrary",)),
        name="moe_dispatch",
    )(dest_flat, tails, n_tiles, xn)

    def tile_idx(i, te, nt):
        return jnp.minimum(i, nt[0] - 1)

    ys = pl.pallas_call(
        _moe_expert_kernel,
        grid_spec=pltpu.PrefetchScalarGridSpec(
            num_scalar_prefetch=2, grid=(n_tiles_max,),
            in_specs=[pl.BlockSpec((tile, d), lambda i, te, nt: (tile_idx(i, te, nt), 0)),
                      pl.BlockSpec((None, d, f), lambda i, te, nt: (te[tile_idx(i, te, nt)], 0, 0)),
                      pl.BlockSpec((None, d, f), lambda i, te, nt: (te[tile_idx(i, te, nt)], 0, 0)),
                      pl.BlockSpec((None, f, d), lambda i, te, nt: (te[tile_idx(i, te, nt)], 0, 0))],
            out_specs=pl.BlockSpec((tile, d), lambda i, te, nt: (tile_idx(i, te, nt), 0))),
        out_shape=jax.ShapeDtypeStruct((rows, d), F32),
        input_output_aliases={2: 0},
        compiler_params=_cparams(("arbitrary",)),
        name="moe_experts",
    )(tile_expert, n_tiles, xs, p['moe_w_gate'], p['moe_w_up'], p['moe_w_down'])

    return pl.pallas_call(
        _moe_combine_kernel,
        grid_spec=pltpu.PrefetchScalarGridSpec(
            num_scalar_prefetch=1, grid=(m // tc,),
            in_specs=[pl.BlockSpec(memory_space=pl.ANY),
                      pl.BlockSpec((tc, d), lambda i, dref: (i, 0)),
                      pl.BlockSpec((tc, LANES), lambda i, dref: (i, 0)),
                      pl.BlockSpec((1, d), lambda i, dref: (0, 0))],
            out_specs=pl.BlockSpec((tc, d), lambda i, dref: (i, 0)),
            scratch_shapes=[pltpu.VMEM((2, tc, d), F32), pltpu.SemaphoreType.DMA(())]),
        out_shape=jax.ShapeDtypeStruct((m, d), F32),
        compiler_params=_cparams(("arbitrary",)),
        name="moe_combine",
    )(dest_flat, ys, x, info, final_w.reshape(1, d))


def s5_prompt(h, nb, seq, prep, p):
    y, hfin = s5_scan(h, prep, nb, seq)
    out = s5_head(y, h, p['s5_d'], p['s5_glu_w'], p['s5_glu_b'], p['s5_norm_w'], tm=min(512, nb * seq))
    return out, hfin[:, 0].transpose(1, 0, 2), hfin[:, 1].transpose(1, 0, 2)


def s5_sample(h, st_re, st_im, prep, p):
    g = prep[0].shape[0]
    ch = S5_CH
    ds = g * ch
    b = h.shape[0]
    u_t = h[:, :ds].reshape(b, g, ch).transpose(1, 0, 2)
    y_t, n_re, n_im = s5_step(u_t, st_re.transpose(1, 0, 2), st_im.transpose(1, 0, 2), prep,
                              p['s5_c_re'], p['s5_c_im'])
    y = y_t.transpose(1, 0, 2).reshape(b, ds)
    out = s5_head(y, h, p['s5_d'], p['s5_glu_w'], p['s5_glu_b'], p['s5_norm_w'], tm=b)
    return out, n_re.transpose(1, 0, 2), n_im.transpose(1, 0, 2)


def _row_tile(m):
    return min(m, 1024)


def _decoder_layer(x, p, s5_prep, final_w, *, nb, seq, mem_kv, xa_heads, states):
    m, d = x.shape
    g, n = p['s5_a_re'].shape
    d_s5 = g * S5_CH
    nheads = p['m2_a_log'].shape[0]
    d_inner = nheads * M2_HEADDIM
    conv_dim = d_inner + 2 * M2_NGROUPS * M2_DSTATE
    n_main = d_s5 + d_inner + conv_dim
    tm = _row_tile(m)

    w_in = p['w_in']
    w_dt = jnp.pad(w_in[:, n_main:], ((0, 0), (0, LANES - nheads)))
    h, dt_raw = fused_matmul([x], w_in, n_out=n_main, gain=p['norm_mix_w'], side_w=w_dt, tm=tm, tn=512)

    if states is None:
        s5_out, s5_re, s5_im = s5_prompt(h, nb, seq, s5_prep, p)
        m2_out, ssm, conv = ssd_prompt(h, dt_raw, nb, seq, p, d_s5=d_s5, d_inner=d_inner, nheads=nheads)
    else:
        s5_out, s5_re, s5_im = s5_sample(h, states[0], states[1], s5_prep, p)
        m2_out, ssm, conv = ssd_sample(h, dt_raw, states[2], states[3], p, d_s5=d_s5, d_inner=d_inner,
                                       nheads=nheads)
    x1 = fused_matmul([s5_out, m2_out], p['w_out'], n_out=d, res=x, tm=tm, tn=512)

    q = fused_matmul([x1], p['xa_wq'], n_out=d, gain=p['norm_xa_w'], tm=tm, tn=512)
    if states is None:
        mem = mem_kv[0].shape[0] // nb
        o = attention_prompt(q, mem_kv[0], mem_kv[1], nb, seq, mem, xa_heads, tq=min(seq, 512))
    else:
        o = attention_sample(q, mem_kv[0], mem_kv[1])
    x2 = fused_matmul([o], p['xa_wo'], n_out=d, res=x1, tm=tm, tn=512)

    y = moe_routed_final(x2, p, final_w)
    return y, s5_re, s5_im, ssm, conv


def kernel(x_prompt, x_sample, mem_prompt, state_s5_re, state_s5_im, state_ssm, state_conv, cache_mem_k, cache_mem_v, norm_mix_w, w_in, s5_a_re, s5_a_im, s5_log_dt, s5_b_re, s5_b_im, s5_c_re, s5_c_im, s5_d, s5_glu_w, s5_glu_b, s5_norm_w, m2_conv_w, m2_conv_b, m2_dt_bias, m2_a_log, m2_d, m2_norm_w, w_out, norm_xa_w, norm_mem_w, xa_wq, xa_wk, xa_wv, xa_wo, norm_ffn_w, router_coarse_w, router_coarse_b, router_fine_w, router_fine_b, moe_w_gate, moe_w_up, moe_w_down, norm_final_w):
    depth = w_in.shape[0]
    assert depth == 1, "the final norm is fused into the (only) layer"
    per_layer = dict(
        norm_mix_w=norm_mix_w, w_in=w_in, s5_a_re=s5_a_re, s5_a_im=s5_a_im, s5_log_dt=s5_log_dt,
        s5_b_re=s5_b_re, s5_b_im=s5_b_im, s5_c_re=s5_c_re, s5_c_im=s5_c_im, s5_d=s5_d, s5_glu_w=s5_glu_w,
        s5_glu_b=s5_glu_b, s5_norm_w=s5_norm_w, m2_conv_w=m2_conv_w, m2_conv_b=m2_conv_b, m2_dt_bias=m2_dt_bias,
        m2_a_log=m2_a_log, m2_d=m2_d, m2_norm_w=m2_norm_w, w_out=w_out, norm_xa_w=norm_xa_w,
        norm_mem_w=norm_mem_w, xa_wq=xa_wq, xa_wk=xa_wk, xa_wv=xa_wv, xa_wo=xa_wo, norm_ffn_w=norm_ffn_w,
        router_coarse_w=router_coarse_w, router_coarse_b=router_coarse_b, router_fine_w=router_fine_w,
        router_fine_b=router_fine_b, moe_w_gate=moe_w_gate, moe_w_up=moe_w_up, moe_w_down=moe_w_down)
    p = {k: v[0] for k, v in per_layer.items()}
    nb, seq, d = x_prompt.shape
    db, dseq, _ = x_sample.shape
    assert dseq == 1
    mem = mem_prompt.shape[1]
    xa_heads = cache_mem_k.shape[3]

    s5_prep = s5_prepare(p['s5_a_re'], p['s5_a_im'], p['s5_log_dt'], p['s5_b_re'], p['s5_b_im'],
                         p['s5_c_re'], p['s5_c_im'])

    memx = mem_prompt.reshape(nb * mem, d)
    mk = fused_matmul([memx], p['xa_wk'], n_out=d, gain=p['norm_mem_w'], tm=_row_tile(nb * mem), tn=512)
    mv = fused_matmul([memx], p['xa_wv'], n_out=d, gain=p['norm_mem_w'], tm=_row_tile(nb * mem), tn=512)
    yp, p_re, p_im, p_ssm, p_conv = _decoder_layer(
        x_prompt.reshape(nb * seq, d), p, s5_prep, norm_final_w, nb=nb, seq=seq, mem_kv=(mk, mv), xa_heads=xa_heads, states=None)

    ys, s_re, s_im, s_ssm, s_conv = _decoder_layer(
        x_sample.reshape(db, d), p, s5_prep, norm_final_w, nb=db, seq=1,
        mem_kv=(cache_mem_k[0], cache_mem_v[0]), xa_heads=xa_heads,
        states=(state_s5_re[0], state_s5_im[0], state_ssm[0], state_conv[0]))

    kv_shape = (1, nb, mem) + cache_mem_k.shape[3:]
    return (yp.reshape(nb, seq, d), ys.reshape(db, 1, d), p_re[None], p_im[None], p_ssm[None], p_conv[None],
            mk.reshape(kv_shape), mv.reshape(kv_shape), s_re[None], s_im[None], s_ssm[None], s_conv[None])
```

```python
import functools
import math

import jax
import jax.numpy as jnp
from jax import lax
from jax.experimental import pallas as pl
from jax.experimental.pallas import tpu as pltpu

F32 = jnp.float32
BF16 = jnp.bfloat16
RMS_EPS = 1e-6

V7X_VMEM_BYTES = 64 * 1024 * 1024
VMEM_LIMIT = V7X_VMEM_BYTES - 8 * 1024 * 1024
LANES = 128

S5_CH = 16
S5_N = 64
S5_Q = 16
S5_GB = 8
M2_HEADDIM = 64
M2_DSTATE = 128
M2_NGROUPS = 2
M2_CONV = 4
M2_CHUNK = 128
MOE_GROUPS = 4
MOE_PER_GROUP = 8


def _cparams(sem):
    return pltpu.CompilerParams(dimension_semantics=sem, vmem_limit_bytes=VMEM_LIMIT)


def _rms(x, w):
    return x * lax.rsqrt(jnp.mean(x * x, axis=-1, keepdims=True) + RMS_EPS) * w


def _sigmoid(x):
    return 1.0 / (1.0 + jnp.exp(-x))


def _silu(x):
    return x * _sigmoid(x)


def _softplus(x):
    return jnp.maximum(x, 0.0) + jnp.log1p(jnp.exp(-jnp.abs(x)))


def _gelu_tanh(x):
    return 0.5 * x * (1.0 + jnp.tanh(math.sqrt(2.0 / math.pi) * (x + 0.044715 * (x * x * x))))


def _dot(a, b):
    return jnp.dot(a, b, preferred_element_type=F32)


def _dot_nt(a, b):
    return lax.dot_general(a, b, (((1,), (1,)), ((), ())), preferred_element_type=F32)


def _split3(x):
    hi = x.astype(BF16)
    r = x - hi.astype(F32)
    mid = r.astype(BF16)
    lo = (r - mid.astype(F32)).astype(BF16)
    return hi, mid, lo


def _split2(x):
    hi = x.astype(BF16)
    lo = (x - hi.astype(F32)).astype(BF16)
    return hi, lo


def _mm_kernel(*refs, n_lhs, has_gain, has_res, has_side):
    it = iter(refs)
    lhs = [next(it) for _ in range(n_lhs)]
    gain = next(it) if has_gain else None
    ws = [next(it) for _ in range(n_lhs)]
    side_w = next(it) if has_side else None
    res = next(it) if has_res else None
    out = next(it)
    side_out = next(it) if has_side else None
    lhs_bf = next(it)

    @pl.when(pl.program_id(1) == 0)
    def _():
        for i in range(n_lhs):
            x = lhs[i][...]
            if has_gain:
                x = _rms(x, gain[...])
            lhs_bf[i] = x.astype(BF16)
        if has_side:
            side_out[...] = _dot(lhs_bf[0], side_w[...].astype(BF16))

    acc = None
    for i in range(n_lhs):
        p = _dot(lhs_bf[i], ws[i][...].astype(BF16))
        acc = p if acc is None else acc + p
    if has_res:
        acc = acc + res[...]
    out[...] = acc


def fused_matmul(lhs_list, w, *, n_out, gain=None, res=None, side_w=None, tm, tn):
    n_lhs = len(lhs_list)
    m, kp = lhs_list[0].shape
    assert all(a.shape == (m, kp) for a in lhs_list)
    assert w.shape[0] == n_lhs * kp and m % tm == 0 and n_out % tn == 0
    assert gain is None or n_lhs == 1
    grid = (m // tm, n_out // tn)
    in_specs = [pl.BlockSpec((tm, kp), lambda i, j: (i, 0)) for _ in range(n_lhs)]
    args = list(lhs_list)
    if gain is not None:
        in_specs.append(pl.BlockSpec((1, kp), lambda i, j: (0, 0)))
        args.append(gain.reshape(1, kp))
    for p in range(n_lhs):
        in_specs.append(pl.BlockSpec((kp, tn), lambda i, j, p=p: (p, j)))
        args.append(w)
    if side_w is not None:
        in_specs.append(pl.BlockSpec((kp, LANES), lambda i, j: (0, 0)))
        args.append(side_w)
    if res is not None:
        in_specs.append(pl.BlockSpec((tm, tn), lambda i, j: (i, j)))
        args.append(res)
    out_shape = [jax.ShapeDtypeStruct((m, n_out), F32)]
    out_specs = [pl.BlockSpec((tm, tn), lambda i, j: (i, j))]
    if side_w is not None:
        out_shape.append(jax.ShapeDtypeStruct((m, LANES), F32))
        out_specs.append(pl.BlockSpec((tm, LANES), lambda i, j: (i, 0)))
    outs = pl.pallas_call(
        functools.partial(_mm_kernel, n_lhs=n_lhs, has_gain=gain is not None,
                          has_res=res is not None, has_side=side_w is not None),
        grid=grid, in_specs=in_specs, out_specs=out_specs, out_shape=out_shape,
        scratch_shapes=[pltpu.VMEM((n_lhs, tm, kp), BF16)],
        compiler_params=_cparams(("parallel", "arbitrary")),
        name="fused_matmul",
    )(*args)
    return outs if side_w is not None else outs[0]


def _s5_prep_kernel(lre_ref, lim_ref, ldt_ref, btre_ref, btim_ref, cre_ref, cim_ref,
                    tz_ref, wsre_ref, wsim_ref, wcre_ref, wcim_ref, aq_ref, ab_ref, bbt_ref):
    q, ch = S5_Q, S5_CH
    lr = lre_ref[...]
    li = lim_ref[...]
    step = jnp.exp(ldt_ref[...])
    mag = jnp.exp(lr * step)
    ab_re = mag * jnp.cos(li * step)
    ab_im = mag * jnp.sin(li * step)
    den = lr * lr + li * li
    num_re = ab_re - 1.0
    coef_re = (num_re * lr + ab_im * li) / den
    coef_im = (ab_im * lr - num_re * li) / den
    bt_re = btre_ref[...]
    bt_im = btim_ref[...]
    bb_re = coef_re * bt_re - coef_im * bt_im
    bb_im = coef_re * bt_im + coef_im * bt_re
    c_re = cre_ref[...]
    c_im = cim_ref[...]

    pw = [(jnp.ones_like(ab_re), jnp.zeros_like(ab_re))]
    for _ in range(q):
        pr, pi = pw[-1]
        pw.append((pr * ab_re - pi * ab_im, pr * ab_im + pi * ab_re))

    ca_re = [c_re * pr - c_im * pi for pr, pi in pw]
    ca_im = [c_re * pi + c_im * pr for pr, pi in pw]
    wcre_ref[...] = jnp.concatenate(ca_re[1:], axis=0).astype(BF16)
    wcim_ref[...] = jnp.concatenate([-x for x in ca_im[1:]], axis=0).astype(BF16)

    pr_stack = jnp.concatenate(ca_re[:q], axis=0)
    pi_stack = jnp.concatenate(ca_im[:q], axis=0)
    krow = None
    for a, b, sign in ((bb_re, pr_stack, 1.0), (bb_im, pi_stack, -1.0)):
        a_hi, a_lo = _split2(a)
        b_hi, b_lo = _split2(b)
        t = _dot_nt(a_hi, b_hi) + (_dot_nt(a_hi, b_lo) + _dot_nt(a_lo, b_hi))
        krow = sign * t if krow is None else krow + sign * t
    lane = lax.broadcasted_iota(jnp.int32, krow.shape, 1)
    blocks = [krow]
    for s in range(1, q):
        blocks.append(jnp.where(lane >= s * ch, pltpu.roll(krow, s * ch, 1), 0.0))
    tz_ref[...] = jnp.concatenate(blocks, axis=0).astype(BF16)

    ws_re, ws_im = [], []
    for s in range(q):
        pr, pi = pw[q - 1 - s]
        ws_re.append(bb_re * pr - bb_im * pi)
        ws_im.append(bb_re * pi + bb_im * pr)
    wsre_ref[...] = jnp.concatenate(ws_re, axis=0).astype(BF16)
    wsim_ref[...] = jnp.concatenate(ws_im, axis=0).astype(BF16)

    aq_ref[0:1, :] = pw[q][0]
    aq_ref[1:2, :] = pw[q][1]
    ab_ref[0:1, :] = ab_re
    ab_ref[1:2, :] = ab_im
    bbt_ref[0:ch, :] = bb_re
    bbt_ref[ch:2 * ch, :] = bb_im


def s5_prepare(a_re, a_im, log_dt, b_re, b_im, c_re, c_im):
    g, n = a_re.shape
    ch, q = S5_CH, S5_Q
    qc = q * ch
    bt_re = jnp.swapaxes(b_re, 1, 2)
    bt_im = jnp.swapaxes(b_im, 1, 2)

    def per_g(*dims):
        return pl.BlockSpec((None,) + dims, lambda i: (i,) + (0,) * len(dims))

    return pl.pallas_call(
        _s5_prep_kernel,
        grid=(g,),
        in_specs=[per_g(1, n), per_g(1, n), per_g(1, 1), per_g(ch, n), per_g(ch, n), per_g(ch, n), per_g(ch, n)],
        out_specs=[per_g(qc, qc), per_g(qc, n), per_g(qc, n), per_g(qc, n), per_g(qc, n),
                   per_g(2, n), per_g(2, n), per_g(2 * ch, n)],
        out_shape=[jax.ShapeDtypeStruct((g, qc, qc), BF16),
                   jax.ShapeDtypeStruct((g, qc, n), BF16), jax.ShapeDtypeStruct((g, qc, n), BF16),
                   jax.ShapeDtypeStruct((g, qc, n), BF16), jax.ShapeDtypeStruct((g, qc, n), BF16),
                   jax.ShapeDtypeStruct((g, 2, n), F32), jax.ShapeDtypeStruct((g, 2, n), F32),
                   jax.ShapeDtypeStruct((g, 2 * ch, n), F32)],
        compiler_params=_cparams(("parallel",)),
        name="s5_prepare",
    )(a_re.reshape(g, 1, n), a_im.reshape(g, 1, n), log_dt.reshape(g, 1, 1), bt_re, bt_im, c_re, c_im)


def _s5_scan_kernel(h_ref, tz_ref, wsre_ref, wsim_ref, wcre_ref, wcim_ref, aq_ref,
                    y_ref, hfin_ref, xs_sc, u_sc, yg_sc, sre_sc, sim_sc, *, nb, nchunk):
    gb, q, ch = S5_GB, S5_Q, S5_CH
    rows = nb * nchunk
    per_vreg = LANES // ch
    assert gb == per_vreg and q % per_vreg == 0
    slot = lax.broadcasted_iota(jnp.int32, (rows, LANES), 1) // ch

    halves = q // per_vreg

    def rot_rows(w, g):
        if g == 0:
            return w
        cut = (per_vreg - g) * ch
        parts = []
        for hf in range(halves):
            blk = w[hf * LANES:(hf + 1) * LANES]
            parts += [blk[cut:], blk[:cut]]
        return jnp.concatenate(parts, axis=0)

    for s in range(q):
        x = h_ref[pl.ds(s, rows, stride=q), :].astype(BF16)
        k = s % per_vreg
        xs_sc[s] = pltpu.roll(x, k * ch, 1) if k else x
    keep = [jnp.where(slot == j, 1.0, 0.0).astype(BF16) for j in range(per_vreg)]
    for g in range(gb):
        for hf in range(halves):
            acc = None
            for k in range(per_vreg):
                piece = xs_sc[hf * per_vreg + k] * keep[(g + k) % per_vreg]
                acc = piece if acc is None else acc + piece
            u_sc[g, :, hf * LANES:(hf + 1) * LANES] = acc

    for g in range(gb):
        u = u_sc[g]
        tz = rot_rows(tz_ref[g], g)
        tz = jnp.concatenate([pltpu.roll(tz[:, hf * LANES:(hf + 1) * LANES], g * ch, 1) if g
                              else tz[:, hf * LANES:(hf + 1) * LANES] for hf in range(halves)], axis=1)
        yg_sc[g] = _dot(u, tz)
        sre_sc[g] = _dot(u, rot_rows(wsre_ref[g], g))
        sim_sc[g] = _dot(u, rot_rows(wsim_ref[g], g))

    ar = [jnp.broadcast_to(aq_ref[g, 0:1, :], (nb, S5_N)) for g in range(gb)]
    ai = [jnp.broadcast_to(aq_ref[g, 1:2, :], (nb, S5_N)) for g in range(gb)]

    def step(c, carry):
        at = pl.ds(c, nb, stride=nchunk)
        new = []
        for g in range(gb):
            hr, hi = carry[g]
            sr = sre_sc[g, at, :]
            si = sim_sc[g, at, :]
            sre_sc[g, at, :] = hr
            sim_sc[g, at, :] = hi
            new.append((ar[g] * hr - ai[g] * hi + sr, ar[g] * hi + ai[g] * hr + si))
        return tuple(new)

    zero = jnp.zeros((nb, S5_N), F32)
    fin = lax.fori_loop(0, nchunk, step, tuple((zero, zero) for _ in range(gb)))
    for g in range(gb):
        hfin_ref[g, 0] = fin[g][0]
        hfin_ref[g, 1] = fin[g][1]
        yg_sc[g] += (_dot_nt(sre_sc[g].astype(BF16), rot_rows(wcre_ref[g], g))
                     + _dot_nt(sim_sc[g].astype(BF16), rot_rows(wcim_ref[g], g)))

    for t in range(q):
        hf, k = divmod(t, per_vreg)
        acc = jnp.zeros((rows, LANES), F32)
        for g in range(gb):
            acc = jnp.where(slot == (k + g) % per_vreg, yg_sc[g, :, hf * LANES:(hf + 1) * LANES], acc)
        y_ref[pl.ds(t, rows, stride=q), :] = pltpu.roll(acc, (per_vreg - k) * ch, 1) if k else acc


def s5_scan(h, prep, nb, seq):
    tz, ws_re, ws_im, wc_re, wc_im, aq = prep[:6]
    g, qc, _ = tz.shape
    n = S5_N
    gb = S5_GB
    q = S5_Q
    assert gb * S5_CH == LANES and qc == q * S5_CH and seq % q == 0
    nchunk = seq // q
    rows = nb * nchunk
    m = nb * seq

    def blk(*dims):
        return pl.BlockSpec((gb,) + dims, lambda i: (i,) + (0,) * len(dims))

    return pl.pallas_call(
        functools.partial(_s5_scan_kernel, nb=nb, nchunk=nchunk),
        grid=(g // gb,),
        in_specs=[pl.BlockSpec((m, LANES), lambda i: (0, i)),
                  blk(qc, qc), blk(qc, n), blk(qc, n), blk(qc, n), blk(qc, n), blk(2, n)],
        out_specs=[pl.BlockSpec((m, LANES), lambda i: (0, i)), blk(2, nb, n)],
        out_shape=[jax.ShapeDtypeStruct((m, g * S5_CH), F32), jax.ShapeDtypeStruct((g, 2, nb, n), F32)],
        scratch_shapes=[pltpu.VMEM((q, rows, LANES), BF16), pltpu.VMEM((gb, rows, qc), BF16),
                        pltpu.VMEM((gb, rows, qc), F32),
                        pltpu.VMEM((gb, rows, n), F32), pltpu.VMEM((gb, rows, n), F32)],
        compiler_params=_cparams(("parallel",)),
        name="s5_scan",
    )(h, tz, ws_re, ws_im, wc_re, wc_im, aq)


def _s5_step_kernel(u_ref, hre_ref, him_ref, ab_ref, bbt_ref, cre_ref, cim_ref, y_ref, ore_ref, oim_ref):
    gb = u_ref.shape[0]
    ch = S5_CH
    for g in range(gb):
        u = u_ref[g].astype(BF16)
        bb_re = bbt_ref[g, 0:ch, :].astype(BF16)
        bb_im = bbt_ref[g, ch:2 * ch, :].astype(BF16)
        ar = ab_ref[g, 0:1, :]
        ai = ab_ref[g, 1:2, :]
        hr0 = hre_ref[g]
        hi0 = him_ref[g]
        hr = _dot(u, bb_re) + (ar * hr0 - ai * hi0)
        hi = _dot(u, bb_im) + (ar * hi0 + ai * hr0)
        ore_ref[g] = hr
        oim_ref[g] = hi
        y_ref[g] = (_dot_nt(hr.astype(BF16), cre_ref[g].astype(BF16))
                    - _dot_nt(hi.astype(BF16), cim_ref[g].astype(BF16)))


def s5_step(u_t, h_re, h_im, prep, c_re, c_im):
    ab, bbt = prep[6], prep[7]
    g, b, ch = u_t.shape
    n = S5_N
    gb = S5_GB

    def blk(*dims):
        return pl.BlockSpec((gb,) + dims, lambda i: (i,) + (0,) * len(dims))

    return pl.pallas_call(
        _s5_step_kernel,
        grid=(g // gb,),
        in_specs=[blk(b, ch), blk(b, n), blk(b, n), blk(2, n), blk(2 * ch, n), blk(ch, n), blk(ch, n)],
        out_specs=[blk(b, ch), blk(b, n), blk(b, n)],
        out_shape=[jax.ShapeDtypeStruct((g, b, ch), F32), jax.ShapeDtypeStruct((g, b, n), F32),
                   jax.ShapeDtypeStruct((g, b, n), F32)],
        compiler_params=_cparams(("parallel",)),
        name="s5_step",
    )(u_t, h_re, h_im, ab, bbt, c_re, c_im)


def _s5_head_kernel(y_ref, u_ref, d_ref, w_ref, b_ref, nw_ref, o_ref):
    y = y_ref[...] + d_ref[...] * u_ref[...]
    g = _gelu_tanh(y)
    gate = _sigmoid(_dot(g.astype(BF16), w_ref[...].astype(BF16)) + b_ref[...])
    o_ref[...] = _rms(g * gate, nw_ref[...])


def s5_head(y, h, d, glu_w, glu_b, norm_w, *, tm):
    m, ds = y.shape
    row = lambda a: a.reshape(1, ds)
    vec = pl.BlockSpec((1, ds), lambda i: (0, 0))
    return pl.pallas_call(
        _s5_head_kernel,
        grid=(m // tm,),
        in_specs=[pl.BlockSpec((tm, ds), lambda i: (i, 0)), pl.BlockSpec((tm, ds), lambda i: (i, 0)), vec,
                  pl.BlockSpec((ds, ds), lambda i: (0, 0)), vec, vec],
        out_specs=pl.BlockSpec((tm, ds), lambda i: (i, 0)),
        out_shape=jax.ShapeDtypeStruct((m, ds), F32),
        compiler_params=_cparams(("parallel",)),
        name="s5_head",
    )(y, h, row(d), glu_w, row(glu_b), row(norm_w))


def _pair_select(first, col0, col1, shape):
    return jnp.where(first, jnp.broadcast_to(col0, shape), jnp.broadcast_to(col1, shape))


def _ssd_chunk_kernel(*refs, d_inner, nheads, n_xparts):
    xparts = refs[:n_xparts]
    (z_ref, dt_ref, cw_ref, cb_ref, dtb_ref, alog_ref, dvec_ref, nw_ref,
     out_ref, ssm_ref, conv_ref, state_sc, xpad_sc, y_sc) = refs[n_xparts:]
    c = pl.program_id(1)
    q = M2_CHUNK
    hp = M2_HEADDIM
    ns = M2_DSTATE
    heads_per_group = nheads // M2_NGROUPS
    halo = 8

    @pl.when(c == 0)
    def _():
        state_sc[...] = jnp.zeros_like(state_sc)
        xpad_sc[0:halo, :] = jnp.zeros((halo, xpad_sc.shape[1]), F32)

    wpart = xparts[0].shape[1]
    for i, xr in enumerate(xparts):
        xpad_sc[halo:halo + q, i * wpart:(i + 1) * wpart] = xr[...]
    cw = cw_ref[...]
    conv = cb_ref[...] + cw[M2_CONV - 1:M2_CONV, :] * xpad_sc[halo:halo + q, :]
    for k in range(1, M2_CONV):
        conv = conv + cw[M2_CONV - 1 - k:M2_CONV - k, :] * xpad_sc[halo - k:halo - k + q, :]
    xpad_sc[0:halo, :] = xpad_sc[q:q + halo, :]
    xc = _silu(conv)

    dt = _softplus(dt_ref[...] + dtb_ref[...])
    a = -jnp.exp(alog_ref[...])
    da = dt * a
    row = lax.broadcasted_iota(jnp.int32, (q, q), 0)
    col = lax.broadcasted_iota(jnp.int32, (q, q), 1)
    causal = row >= col
    tri = jnp.where(causal, 1.0, 0.0).astype(BF16)
    d_hi, d_mid, d_lo = _split3(da)
    acum = _dot(tri, d_hi) + (_dot(tri, d_mid) + _dot(tri, d_lo))
    acum_t = acum.T
    alast = acum[q - 1:q, :]
    first = col < hp
    first_rows = row < hp

    for pr in range(nheads // 2):
        grp = (2 * pr) // heads_per_group
        b_bf = xc[:, d_inner + grp * ns:d_inner + (grp + 1) * ns].astype(BF16)
        c_bf = xc[:, d_inner + (M2_NGROUPS + grp) * ns:d_inner + (M2_NGROUPS + grp + 1) * ns].astype(BF16)
        cb = _dot_nt(c_bf, b_bf)
        xpair = xc[:, pr * 2 * hp:(pr + 1) * 2 * hp]
        h0, h1 = 2 * pr, 2 * pr + 1
        acol = [acum[:, h:h + 1] for h in (h0, h1)]
        m = []
        for k, h in enumerate((h0, h1)):
            seg = jnp.broadcast_to(acol[k], (q, q)) - jnp.broadcast_to(acum_t[h:h + 1, :], (q, q))
            lmat = jnp.exp(jnp.where(causal, seg, -1e30))
            m.append((cb * lmat).astype(BF16))
        dtp = _pair_select(first, dt[:, h0:h0 + 1], dt[:, h1:h1 + 1], (q, q))
        xdt = xpair * dtp
        xdt_bf = xdt.astype(BF16)
        y_diag = jnp.where(first, _dot(m[0], xdt_bf), _dot(m[1], xdt_bf))
        dec_end = _pair_select(first, jnp.exp(alast[:, h0:h0 + 1] - acol[0]),
                               jnp.exp(alast[:, h1:h1 + 1] - acol[1]), (q, q))
        xw_t = (xdt * dec_end).T.astype(BF16)
        chunk_state = _dot(xw_t, b_bf)
        rows = pl.ds(pr * 2 * hp, 2 * hp)
        prev = state_sc[rows, :]
        y_off = _dot_nt(c_bf, prev.astype(BF16)) * _pair_select(first, jnp.exp(acol[0]), jnp.exp(acol[1]), (q, q))
        sdec = jnp.where(first_rows, jnp.broadcast_to(jnp.exp(alast[:, h0:h0 + 1]), (q, q)),
                         jnp.broadcast_to(jnp.exp(alast[:, h1:h1 + 1]), (q, q)))
        state_sc[rows, :] = prev * sdec + chunk_state
        y_sc[:, pr * 2 * hp:(pr + 1) * 2 * hp] = y_diag + y_off + dvec_ref[:, pr * 2 * hp:(pr + 1) * 2 * hp] * xpair

    out_ref[...] = _rms(y_sc[...] * _silu(z_ref[...]), nw_ref[...])

    @pl.when(c == pl.num_programs(1) - 1)
    def _():
        ssm_ref[...] = state_sc[...]
        conv_ref[...] = xpad_sc[halo + q - (M2_CONV - 1):halo + q, :]


def ssd_prompt(h, dt_raw, nb, seq, p, *, d_s5, d_inner, nheads):
    q = M2_CHUNK
    nc = seq // q
    conv_dim = d_inner + 2 * M2_NGROUPS * M2_DSTATE
    xw = 512
    xoff = d_s5 + d_inner
    assert d_s5 % d_inner == 0 and xoff % xw == 0 and conv_dim % xw == 0
    assert M2_CHUNK == 2 * M2_HEADDIM == M2_DSTATE == LANES
    zblk = d_s5 // d_inner
    n_xparts = conv_dim // xw
    m = nb * seq
    pad = lambda v: jnp.pad(v, (0, LANES - v.shape[0])).reshape(1, LANES)
    dvec = jnp.repeat(p['m2_d'], M2_HEADDIM).reshape(1, d_inner)
    vec = lambda n: pl.BlockSpec((1, n), lambda b, c: (0, 0))
    tok = lambda w, j: pl.BlockSpec((q, w), lambda b, c, j=j: (b * nc + c, j))
    out, ssm, conv = pl.pallas_call(
        functools.partial(_ssd_chunk_kernel, d_inner=d_inner, nheads=nheads, n_xparts=n_xparts),
        grid=(nb, nc),
        in_specs=[tok(xw, xoff // xw + i) for i in range(n_xparts)] + [tok(d_inner, zblk), tok(LANES, 0),
                  pl.BlockSpec((M2_CONV, conv_dim), lambda b, c: (0, 0)), vec(conv_dim), vec(LANES), vec(LANES),
                  vec(d_inner), vec(d_inner)],
        out_specs=[tok(d_inner, 0),
                   pl.BlockSpec((None, nheads * M2_HEADDIM, M2_DSTATE), lambda b, c: (b, 0, 0)),
                   pl.BlockSpec((None, M2_CONV - 1, conv_dim), lambda b, c: (b, 0, 0))],
        out_shape=[jax.ShapeDtypeStruct((m, d_inner), F32),
                   jax.ShapeDtypeStruct((nb, nheads * M2_HEADDIM, M2_DSTATE), F32),
                   jax.ShapeDtypeStruct((nb, M2_CONV - 1, conv_dim), F32)],
        scratch_shapes=[pltpu.VMEM((nheads * M2_HEADDIM, M2_DSTATE), F32),
                        pltpu.VMEM((q + 8, conv_dim), F32),
                        pltpu.VMEM((q, d_inner), F32)],
        compiler_params=_cparams(("parallel", "arbitrary")),
        name="ssd_chunk",
    )(*([h] * n_xparts), h, dt_raw, p['m2_conv_w'], p['m2_conv_b'].reshape(1, conv_dim), pad(p['m2_dt_bias']),
      pad(p['m2_a_log']), dvec, p['m2_norm_w'].reshape(1, d_inner))
    return out, ssm.reshape(nb, nheads, M2_HEADDIM, M2_DSTATE), conv


SSD_STEP_SEQS = 4


def _ssd_step_kernel(*refs, d_inner, nheads, n_xparts):
    xparts = refs[:n_xparts]
    (z_ref, dt_ref, cs0_ref, cs1_ref, cs2_ref, cw_ref, cb_ref, dtb_ref, alog_ref, dvec_ref, nw_ref, st_ref,
     out_ref, so_ref, lhs_sc, bfull_sc, ct_sc, yt_sc, xs_sc) = refs[n_xparts:]
    i = pl.program_id(0)
    nb = z_ref.shape[0]
    ns = M2_DSTATE
    rows_g = (nheads // M2_NGROUPS) * M2_HEADDIM

    @pl.when(i == 0)
    def _():
        cw = cw_ref[...]
        xbc = jnp.concatenate([xr[...] for xr in xparts], axis=1)
        conv = (cb_ref[...] + cw[3:4, :] * xbc + cw[2:3, :] * cs2_ref[...]
                + cw[1:2, :] * cs1_ref[...] + cw[0:1, :] * cs0_ref[...])
        xc = _silu(conv)
        dt = _softplus(dt_ref[...] + dtb_ref[...])
        dec = jnp.exp(dt * (-jnp.exp(alog_ref[...])))
        hrow = lax.broadcasted_iota(jnp.int32, (LANES, d_inner), 0)
        hcol = lax.broadcasted_iota(jnp.int32, (LANES, d_inner), 1)
        expand = jnp.where(hcol // M2_HEADDIM == hrow, 1.0, 0.0).astype(BF16)

        def expand_heads(v):
            a, b_, c = _split3(v)
            return _dot(a, expand) + (_dot(b_, expand) + _dot(c, expand))

        xs = xc[:, :d_inner]
        xs_sc[...] = xs
        xdt_t = (xs * expand_heads(dt)).T
        d_hi, d_mid, d_lo = _split3(expand_heads(dec).T)
        for g in range(M2_NGROUPS):
            r = slice(g * rows_g, (g + 1) * rows_g)
            lhs_sc[g] = jnp.concatenate([xdt_t[r].astype(BF16), d_hi[r], d_mid[r], d_lo[r]], axis=1)
            b_g = xc[:, d_inner + g * ns:d_inner + (g + 1) * ns]
            bfull_sc[g] = jnp.concatenate([b_g, jnp.zeros_like(b_g)], axis=1)
            c_g = xc[:, d_inner + (M2_NGROUPS + g) * ns:d_inner + (M2_NGROUPS + g + 1) * ns]
            ct_sc[g] = c_g.T
        yt_sc[...] = jnp.zeros_like(yt_sc)

    row_id = lax.broadcasted_iota(jnp.int32, (nb, 2 * ns), 0)
    lane_id = lax.broadcasted_iota(jnp.int32, (nb, 2 * ns), 1)
    col_id = lax.broadcasted_iota(jnp.int32, (ns, nb), 1)
    for j in range(st_ref.shape[0]):
        b = i * st_ref.shape[0] + j
        r_bot = jnp.where((row_id == b) & (lane_id >= ns), 1.0, 0.0).astype(BF16)
        for g in range(M2_NGROUPS):
            r = pl.ds(g * rows_g, rows_g)
            r_top = jnp.where(row_id == b, bfull_sc[g], 0.0).astype(BF16)
            rhs = jnp.concatenate([r_top, r_bot, r_bot, r_bot], axis=0)
            o = _dot(lhs_sc[g], rhs)
            hnew = st_ref[j, r, :] * o[:, ns:] + o[:, :ns]
            so_ref[j, r, :] = hnew
            cm = jnp.where(col_id == b, ct_sc[g], 0.0).astype(BF16)
            yt_sc[r, :] += _dot(hnew.astype(BF16), cm)

    @pl.when(i == pl.num_programs(0) - 1)
    def _():
        y = yt_sc[...].T + dvec_ref[...] * xs_sc[...]
        out_ref[...] = _rms(y * _silu(z_ref[...]), nw_ref[...])


def ssd_sample(h, dt_raw, state, conv_state, p, *, d_s5, d_inner, nheads):
    nb = h.shape[0]
    conv_dim = d_inner + 2 * M2_NGROUPS * M2_DSTATE
    xw = 512
    xoff = d_s5 + d_inner
    assert nb == LANES and M2_DSTATE == LANES and M2_CONV == 4
    assert xoff % xw == 0 and conv_dim % xw == 0 and d_s5 % d_inner == 0 and nb % SSD_STEP_SEQS == 0
    n_xparts = conv_dim // xw
    rows = nheads * M2_HEADDIM
    rows_g = rows // M2_NGROUPS
    pad = lambda v: jnp.pad(v, (0, LANES - v.shape[0])).reshape(1, LANES)
    dvec = jnp.repeat(p['m2_d'], M2_HEADDIM).reshape(1, d_inner)
    full = lambda a, b, j=0: pl.BlockSpec((a, b), lambda i, j=j: (0, j))
    st_spec = pl.BlockSpec((SSD_STEP_SEQS, rows, M2_DSTATE), lambda i: (i, 0, 0))
    out, new_state = pl.pallas_call(
        functools.partial(_ssd_step_kernel, d_inner=d_inner, nheads=nheads, n_xparts=n_xparts),
        grid=(nb // SSD_STEP_SEQS,),
        in_specs=[full(nb, xw, xoff // xw + k) for k in range(n_xparts)]
        + [full(nb, d_inner, d_s5 // d_inner), full(nb, LANES)]
        + [full(nb, conv_dim)] * 3
        + [full(M2_CONV, conv_dim), full(1, conv_dim), full(1, LANES), full(1, LANES), full(1, d_inner),
           full(1, d_inner), st_spec],
        out_specs=[full(nb, d_inner), st_spec],
        out_shape=[jax.ShapeDtypeStruct((nb, d_inner), F32), jax.ShapeDtypeStruct((nb, rows, M2_DSTATE), F32)],
        scratch_shapes=[pltpu.VMEM((M2_NGROUPS, rows_g, 4 * nb), BF16),
                        pltpu.VMEM((M2_NGROUPS, nb, 2 * M2_DSTATE), F32),
                        pltpu.VMEM((M2_NGROUPS, M2_DSTATE, nb), F32),
                        pltpu.VMEM((rows, nb), F32),
                        pltpu.VMEM((nb, d_inner), F32)],
        compiler_params=_cparams(("arbitrary",)),
        name="ssd_step",
    )(*([h] * n_xparts), h, dt_raw, conv_state[:, 0], conv_state[:, 1], conv_state[:, 2],
      p['m2_conv_w'], p['m2_conv_b'].reshape(1, conv_dim), pad(p['m2_dt_bias']), pad(p['m2_a_log']),
      dvec, p['m2_norm_w'].reshape(1, d_inner), state.reshape(nb, rows, M2_DSTATE))
    xbc = lax.slice_in_dim(h, xoff, xoff + conv_dim, axis=1)
    new_conv = jnp.concatenate([conv_state[:, 1:], xbc[:, None, :]], axis=1)
    return out, new_state.reshape(state.shape), new_conv


def _softmax_rows(s):
    e = jnp.exp(s - jnp.max(s, axis=-1, keepdims=True))
    return e / jnp.sum(e, axis=-1, keepdims=True)


def _attn_kernel(q_ref, k_ref, v_ref, o_ref, *, scale):
    s = _dot_nt(q_ref[...].astype(BF16), k_ref[...].astype(BF16)) * scale
    o_ref[...] = _dot(_softmax_rows(s).astype(BF16), v_ref[...].astype(BF16))


def attention_prompt(q, k, v, nb, seq, mem, heads, *, tq):
    d = q.shape[1]
    hd = d // heads
    nq = seq // tq
    kv_spec = pl.BlockSpec((mem, hd), lambda b, h, i: (b, h))
    q_spec = pl.BlockSpec((tq, hd), lambda b, h, i: (b * nq + i, h))
    return pl.pallas_call(
        functools.partial(_attn_kernel, scale=hd ** -0.5),
        grid=(nb, heads, nq),
        in_specs=[q_spec, kv_spec, kv_spec],
        out_specs=q_spec,
        out_shape=jax.ShapeDtypeStruct(q.shape, F32),
        compiler_params=_cparams(("parallel", "parallel", "parallel")),
        name="attention_prompt",
    )(q, k, v)


ATTN_STEP_SEQS = 2


def _attn_step_kernel(q_ref, k_ref, v_ref, o_ref, *, scale):
    for j in range(q_ref.shape[0]):
        s = jnp.sum(k_ref[j] * q_ref[j], axis=-1, keepdims=True) * scale
        e = jnp.exp(s - jnp.max(s, axis=0, keepdims=True))
        att = e / jnp.sum(e, axis=0, keepdims=True)
        o_ref[j] = jnp.sum(att * v_ref[j], axis=0)


def attention_sample(q, k_cache, v_cache):
    b, mem, heads, hd = k_cache.shape
    nseq = ATTN_STEP_SEQS
    q_spec = pl.BlockSpec((nseq, heads, hd), lambda i: (i, 0, 0))
    kv_spec = pl.BlockSpec((nseq, mem, heads, hd), lambda i: (i, 0, 0, 0))
    out = pl.pallas_call(
        functools.partial(_attn_step_kernel, scale=hd ** -0.5),
        grid=(b // nseq,),
        in_specs=[q_spec, kv_spec, kv_spec],
        out_specs=q_spec,
        out_shape=jax.ShapeDtypeStruct((b, heads, hd), F32),
        compiler_params=_cparams(("parallel",)),
        name="attention_step",
    )(q.reshape(b, heads, hd), k_cache, v_cache)
    return out.reshape(b, heads * hd)


NEG = -1e30


INFO_G1, INFO_G2, INFO_E1, INFO_E2 = 0, 1, 2, 3


def _lane_pack(lane, values):
    out = 0.0
    for k, v in values:
        out = jnp.where(lane == k, v, out)
    return out


def _router_kernel(x_ref, nw_ref, wr_ref, br_ref, xn_ref, sel_ref, info_ref, *, n_experts):
    xn = _rms(x_ref[...], nw_ref[...])
    xn_ref[...] = xn
    x_hi, x_lo = _split2(xn)
    w_hi, w_lo = _split2(wr_ref[...])
    logits = _dot(x_hi, w_hi) + (_dot(x_hi, w_lo) + _dot(x_lo, w_hi)) + br_ref[...]
    lane = lax.broadcasted_iota(jnp.int32, logits.shape, 1)
    big = jnp.int32(2 ** 30)
    is_c = (lane >= n_experts) & (lane < n_experts + MOE_GROUPS)
    lc = jnp.where(is_c, logits, NEG)
    cmax = jnp.max(lc, axis=-1, keepdims=True)
    gsel = jnp.min(jnp.where(lc == cmax, lane, big), axis=-1, keepdims=True) - n_experts
    gate_c = 1.0 / jnp.sum(jnp.where(is_c, jnp.exp(lc - cmax), 0.0), axis=-1, keepdims=True)
    in_group = (lane < n_experts) & (lane // MOE_PER_GROUP == gsel)
    lf = jnp.where(in_group, logits, NEG)
    t1 = jnp.max(lf, axis=-1, keepdims=True)
    i1 = jnp.min(jnp.where(lf == t1, lane, big), axis=-1, keepdims=True)
    lf2 = jnp.where(lane == i1, NEG, lf)
    t2 = jnp.max(lf2, axis=-1, keepdims=True)
    i2 = jnp.min(jnp.where(lf2 == t2, lane, big), axis=-1, keepdims=True)
    r = jnp.exp(t2 - t1)
    g1 = gate_c / (1.0 + r)
    g2 = gate_c * r / (1.0 + r)
    sel_ref[...] = jnp.where((lane == i1) | (lane == i2), 1.0, 0.0)
    info_ref[...] = _lane_pack(lane, ((INFO_G1, g1), (INFO_G2, g2), (INFO_E1, i1.astype(F32)),
                                      (INFO_E2, i2.astype(F32))))


def moe_router(x, norm_w, w_coarse, b_coarse, w_fine, b_fine, *, tm):
    m, d = x.shape
    e = w_fine.shape[1]
    padw = LANES - e - MOE_GROUPS
    wr = jnp.concatenate([w_fine, w_coarse, jnp.zeros((d, padw), F32)], axis=1)
    br = jnp.concatenate([b_fine, b_coarse, jnp.zeros((padw,), F32)]).reshape(1, LANES)
    tok = pl.BlockSpec((tm, LANES), lambda i: (i, 0))
    return pl.pallas_call(
        functools.partial(_router_kernel, n_experts=e),
        grid=(m // tm,),
        in_specs=[pl.BlockSpec((tm, d), lambda i: (i, 0)), pl.BlockSpec((1, d), lambda i: (0, 0)),
                  pl.BlockSpec((d, LANES), lambda i: (0, 0)), pl.BlockSpec((1, LANES), lambda i: (0, 0))],
        out_specs=[pl.BlockSpec((tm, d), lambda i: (i, 0)), tok, tok],
        out_shape=[jax.ShapeDtypeStruct((m, d), F32), jax.ShapeDtypeStruct((m, LANES), F32),
                   jax.ShapeDtypeStruct((m, LANES), F32)],
        compiler_params=_cparams(("parallel",)),
        name="moe_router",
    )(x, norm_w.reshape(1, d), wr, br)


def _moe_rank_kernel(sel_ref, info_ref, rank_ref, counts_ref, carry_sc):
    i = pl.program_id(0)

    @pl.when(i == 0)
    def _():
        carry_sc[...] = jnp.zeros_like(carry_sc)

    sel = sel_ref[...]
    tm = sel.shape[0]
    row = lax.broadcasted_iota(jnp.int32, (tm, tm), 0)
    col = lax.broadcasted_iota(jnp.int32, (tm, tm), 1)
    before = jnp.where(row > col, 1.0, 0.0).astype(BF16)
    rank = _dot(before, sel.astype(BF16)) + carry_sc[...]
    info = info_ref[...]
    lane = lax.broadcasted_iota(jnp.int32, sel.shape, 1)
    e1 = info[:, INFO_E1:INFO_E1 + 1].astype(jnp.int32)
    e2 = info[:, INFO_E2:INFO_E2 + 1].astype(jnp.int32)
    r1 = jnp.sum(jnp.where(lane == e1, rank, 0.0), axis=-1, keepdims=True)
    r2 = jnp.sum(jnp.where(lane == e2, rank, 0.0), axis=-1, keepdims=True)
    rank_ref[...] = _lane_pack(lane, ((0, r1), (1, r2)))
    carry_sc[...] += jnp.sum(sel, axis=0, keepdims=True)

    @pl.when(i == pl.num_programs(0) - 1)
    def _():
        counts_ref[...] = carry_sc[...]


def _moe_plan_kernel(rank_ref, info_ref, counts_ref, dest_ref, tiles_ref, *, tile, n_experts):
    counts = counts_ref[...]
    ntile_e = jnp.floor((counts + (tile - 1)) * (1.0 / tile))
    padded = jnp.broadcast_to(ntile_e * tile, (8, LANES))
    r = lax.broadcasted_iota(jnp.int32, (LANES, LANES), 0)
    c = lax.broadcasted_iota(jnp.int32, (LANES, LANES), 1)
    lower = jnp.where(r < c, 1.0, 0.0).astype(BF16)
    p_hi, p_mid, p_lo = _split3(padded)
    offs = (_dot(p_hi, lower) + (_dot(p_mid, lower) + _dot(p_lo, lower)))[0:1, :]
    ends = offs + padded[0:1, :]

    info = info_ref[...]
    rank = rank_ref[...]
    lane = lax.broadcasted_iota(jnp.int32, info.shape, 1)
    e1 = info[:, INFO_E1:INFO_E1 + 1].astype(jnp.int32)
    e2 = info[:, INFO_E2:INFO_E2 + 1].astype(jnp.int32)
    d1 = jnp.sum(jnp.where(lane == e1, offs, 0.0), axis=-1, keepdims=True) + rank[:, 0:1]
    d2 = jnp.sum(jnp.where(lane == e2, offs, 0.0), axis=-1, keepdims=True) + rank[:, 1:2]
    dest_ref[...] = _lane_pack(lane, ((0, d1), (1, d2))).astype(jnp.int32)

    ends_col = jnp.broadcast_to(ends, (LANES, LANES)).T
    start = (c * tile).astype(F32)
    n_before = jnp.sum(jnp.where((ends_col <= start) & (r < n_experts), 1.0, 0.0), axis=0, keepdims=True)
    lane1 = lax.broadcasted_iota(jnp.int32, (1, LANES), 1)
    total = jnp.sum(jnp.where(lane1 == n_experts - 1, ends, 0.0), axis=-1, keepdims=True)
    tail = jnp.where(counts > 0.0, ends - tile, -1.0)
    tiles_ref[...] = jnp.zeros_like(tiles_ref)
    tiles_ref[0:1, :] = jnp.minimum(n_before, n_experts - 1.0).astype(jnp.int32)
    tiles_ref[1:2, :] = jnp.broadcast_to(total * (1.0 / tile), (1, LANES)).astype(jnp.int32)
    tiles_ref[2:3, :] = tail.astype(jnp.int32)


def _moe_dispatch_kernel(dest_ref, tails_ref, ntiles_ref, xn_ref, xs_ref, zero_sc, sems, *, tile, n_experts):
    i = pl.program_id(0)
    tm = xn_ref.shape[0]
    n_tiles_max = xs_ref.shape[0] // tile

    def fill_copy(j):
        return pltpu.make_async_copy(zero_sc, xs_ref.at[pl.ds(pl.multiple_of(j * tile, tile), tile)], sems.at[2])

    def tail_copy(e):
        return pltpu.make_async_copy(zero_sc, xs_ref.at[pl.ds(pl.multiple_of(tails_ref[e], tile), tile)],
                                     sems.at[1])

    @pl.when(i == 0)
    def _():
        zero_sc[...] = jnp.zeros_like(zero_sc)

        def fill(j, _):
            fill_copy(j).start()
            return 0

        def clear(e, _):
            @pl.when(tails_ref[e] >= 0)
            def _():
                tail_copy(e).start()
            return 0

        def clear_wait(e, _):
            @pl.when(tails_ref[e] >= 0)
            def _():
                tail_copy(e).wait()
            return 0

        lax.fori_loop(ntiles_ref[0], n_tiles_max, fill, 0)
        lax.fori_loop(0, n_experts, clear, 0)
        lax.fori_loop(0, n_experts, clear_wait, 0)

    def row_copy(r, k):
        return pltpu.make_async_copy(xn_ref.at[pl.ds(r, 1)],
                                     xs_ref.at[pl.ds(dest_ref[2 * (i * tm + r) + k], 1)], sems.at[0])

    def issue(r, _):
        row_copy(r, 0).start()
        row_copy(r, 1).start()
        return 0

    for r in range(tm):
        issue(r, 0)
    for _ in range(2):
        pltpu.make_async_copy(xn_ref, xs_ref.at[pl.ds(0, tm)], sems.at[0]).wait()

    @pl.when(i == pl.num_programs(0) - 1)
    def _():
        def fill_wait(j, _):
            fill_copy(j).wait()
            return 0

        lax.fori_loop(ntiles_ref[0], n_tiles_max, fill_wait, 0)


def _moe_expert_kernel(texp_ref, ntiles_ref, xs_ref, wg_ref, wu_ref, wd_ref, ys_ref):
    @pl.when(pl.program_id(0) < ntiles_ref[0])
    def _():
        x = xs_ref[...].astype(BF16)
        hg = _dot(x, wg_ref[...].astype(BF16))
        hu = _dot(x, wu_ref[...].astype(BF16))
        ys_ref[...] = _dot((_silu(hg) * hu).astype(BF16), wd_ref[...].astype(BF16))


def _moe_combine_kernel(dest_ref, ys_ref, x_ref, info_ref, fw_ref, y_ref, buf_sc, sem):
    i = pl.program_id(0)
    tm = x_ref.shape[0]

    def row_copy(r, k):
        t = i * tm + r
        return pltpu.make_async_copy(ys_ref.at[pl.ds(dest_ref[2 * t + k], 1)], buf_sc.at[k, pl.ds(r, 1)], sem)

    def issue(r, _):
        row_copy(r, 0).start()
        row_copy(r, 1).start()
        return 0

    for r in range(tm):
        issue(r, 0)
    for k in range(2):
        pltpu.make_async_copy(ys_ref.at[pl.ds(0, tm)], buf_sc.at[k], sem).wait()
    info = info_ref[...]
    y = x_ref[...] + info[:, INFO_G1:INFO_G1 + 1] * buf_sc[0] + info[:, INFO_G2:INFO_G2 + 1] * buf_sc[1]
    y_ref[...] = _rms(y, fw_ref[...])


MOE_TILE = 256
MOE_TOKENS_PER_STEP = 256


def moe_routed_final(x, p, final_w):
    m, d = x.shape
    ne, _, f = p['moe_w_gate'].shape
    tile = min(MOE_TILE, m)
    tm = min(m, 512)
    xn, sel, info = moe_router(x, p['norm_ffn_w'], p['router_coarse_w'], p['router_coarse_b'],
                               p['router_fine_w'], p['router_fine_b'], tm=tm)
    tok = pl.BlockSpec((tm, LANES), lambda i: (i, 0))
    rank, counts = pl.pallas_call(
        _moe_rank_kernel,
        grid=(m // tm,),
        in_specs=[tok, tok],
        out_specs=[tok, pl.BlockSpec((1, LANES), lambda i: (0, 0))],
        out_shape=[jax.ShapeDtypeStruct((m, LANES), F32), jax.ShapeDtypeStruct((1, LANES), F32)],
        scratch_shapes=[pltpu.VMEM((1, LANES), F32)],
        compiler_params=_cparams(("arbitrary",)),
        name="moe_rank",
    )(sel, info)
    n_tiles_max = (2 * m) // tile + ne
    assert n_tiles_max <= LANES
    dest, tiles = pl.pallas_call(
        functools.partial(_moe_plan_kernel, tile=tile, n_experts=ne),
        out_shape=[jax.ShapeDtypeStruct((m, LANES), jnp.int32), jax.ShapeDtypeStruct((8, LANES), jnp.int32)],
        compiler_params=pltpu.CompilerParams(vmem_limit_bytes=VMEM_LIMIT),
        name="moe_plan",
    )(rank, info, counts)
    dest_flat = dest[:, :2].reshape(2 * m)
    tile_expert = tiles[0, :n_tiles_max]
    n_tiles = tiles[1, :1]
    tails = tiles[2, :ne]

    rows = n_tiles_max * tile
    tc = min(m, MOE_TOKENS_PER_STEP)
    xs = pl.pallas_call(
        functools.partial(_moe_dispatch_kernel, tile=tile, n_experts=ne),
        grid_spec=pltpu.PrefetchScalarGridSpec(
            num_scalar_prefetch=3, grid=(m // tc,),
            in_specs=[pl.BlockSpec((tc, d), lambda i, *_: (i, 0))],
            out_specs=pl.BlockSpec(memory_space=pl.ANY),
            scratch_shapes=[pltpu.VMEM((tile, d), F32), pltpu.SemaphoreType.DMA((3,))]),
        out_shape=jax.ShapeDtypeStruct((rows, d), F32),
        compiler_params=_cparams(("arbitrary",)),
        name="moe_dispatch",
    )(dest_flat, tails, n_tiles, xn)

    def tile_idx(i, te, nt):
        return jnp.minimum(i, nt[0] - 1)

    ys = pl.pallas_call(
        _moe_expert_kernel,
        grid_spec=pltpu.PrefetchScalarGridSpec(
            num_scalar_prefetch=2, grid=(n_tiles_max,),
            in_specs=[pl.BlockSpec((tile, d), lambda i, te, nt: (tile_idx(i, te, nt), 0)),
                      pl.BlockSpec((None, d, f), lambda i, te, nt: (te[tile_idx(i, te, nt)], 0, 0)),
                      pl.BlockSpec((None, d, f), lambda i, te, nt: (te[tile_idx(i, te, nt)], 0, 0)),
                      pl.BlockSpec((None, f, d), lambda i, te, nt: (te[tile_idx(i, te, nt)], 0, 0))],
            out_specs=pl.BlockSpec((tile, d), lambda i, te, nt: (tile_idx(i, te, nt), 0))),
        out_shape=jax.ShapeDtypeStruct((rows, d), F32),
        input_output_aliases={2: 0},
        compiler_params=_cparams(("arbitrary",)),
        name="moe_experts",
    )(tile_expert, n_tiles, xs, p['moe_w_gate'], p['moe_w_up'], p['moe_w_down'])

    return pl.pallas_call(
        _moe_combine_kernel,
        grid_spec=pltpu.PrefetchScalarGridSpec(
            num_scalar_prefetch=1, grid=(m // tc,),
            in_specs=[pl.BlockSpec(memory_space=pl.ANY),
                      pl.BlockSpec((tc, d), lambda i, dref: (i, 0)),
                      pl.BlockSpec((tc, LANES), lambda i, dref: (i, 0)),
                      pl.BlockSpec((1, d), lambda i, dref: (0, 0))],
            out_specs=pl.BlockSpec((tc, d), lambda i, dref: (i, 0)),
            scratch_shapes=[pltpu.VMEM((2, tc, d), F32), pltpu.SemaphoreType.DMA(())]),
        out_shape=jax.ShapeDtypeStruct((m, d), F32),
        compiler_params=_cparams(("arbitrary",)),
        name="moe_combine",
    )(dest_flat, ys, x, info, final_w.reshape(1, d))


def s5_prompt(h, nb, seq, prep, p):
    y, hfin = s5_scan(h, prep, nb, seq)
    out = s5_head(y, h, p['s5_d'], p['s5_glu_w'], p['s5_glu_b'], p['s5_norm_w'], tm=min(512, nb * seq))
    return out, hfin[:, 0].transpose(1, 0, 2), hfin[:, 1].transpose(1, 0, 2)


def s5_sample(h, st_re, st_im, prep, p):
    g = prep[0].shape[0]
    ch = S5_CH
    ds = g * ch
    b = h.shape[0]
    u_t = h[:, :ds].reshape(b, g, ch).transpose(1, 0, 2)
    y_t, n_re, n_im = s5_step(u_t, st_re.transpose(1, 0, 2), st_im.transpose(1, 0, 2), prep,
                              p['s5_c_re'], p['s5_c_im'])
    y = y_t.transpose(1, 0, 2).reshape(b, ds)
    out = s5_head(y, h, p['s5_d'], p['s5_glu_w'], p['s5_glu_b'], p['s5_norm_w'], tm=b)
    return out, n_re.transpose(1, 0, 2), n_im.transpose(1, 0, 2)


def _row_tile(m):
    return min(m, 1024)


def _decoder_layer(x, p, s5_prep, final_w, *, nb, seq, mem_kv, xa_heads, states):
    m, d = x.shape
    g, n = p['s5_a_re'].shape
    d_s5 = g * S5_CH
    nheads = p['m2_a_log'].shape[0]
    d_inner = nheads * M2_HEADDIM
    conv_dim = d_inner + 2 * M2_NGROUPS * M2_DSTATE
    n_main = d_s5 + d_inner + conv_dim
    tm = _row_tile(m)

    w_in = p['w_in']
    w_dt = jnp.pad(w_in[:, n_main:], ((0, 0), (0, LANES - nheads)))
    h, dt_raw = fused_matmul([x], w_in, n_out=n_main, gain=p['norm_mix_w'], side_w=w_dt, tm=tm, tn=512)

    if states is None:
        s5_out, s5_re, s5_im = s5_prompt(h, nb, seq, s5_prep, p)
        m2_out, ssm, conv = ssd_prompt(h, dt_raw, nb, seq, p, d_s5=d_s5, d_inner=d_inner, nheads=nheads)
    else:
        s5_out, s5_re, s5_im = s5_sample(h, states[0], states[1], s5_prep, p)
        m2_out, ssm, conv = ssd_sample(h, dt_raw, states[2], states[3], p, d_s5=d_s5, d_inner=d_inner,
                                       nheads=nheads)
    x1 = fused_matmul([s5_out, m2_out], p['w_out'], n_out=d, res=x, tm=tm, tn=512)

    q = fused_matmul([x1], p['xa_wq'], n_out=d, gain=p['norm_xa_w'], tm=tm, tn=512)
    if states is None:
        mem = mem_kv[0].shape[0] // nb
        o = attention_prompt(q, mem_kv[0], mem_kv[1], nb, seq, mem, xa_heads, tq=min(seq, 512))
    else:
        o = attention_sample(q, mem_kv[0], mem_kv[1])
    x2 = fused_matmul([o], p['xa_wo'], n_out=d, res=x1, tm=tm, tn=512)

    y = moe_routed_final(x2, p, final_w)
    return y, s5_re, s5_im, ssm, conv


def kernel(x_prompt, x_sample, mem_prompt, state_s5_re, state_s5_im, state_ssm, state_conv, cache_mem_k, cache_mem_v, norm_mix_w, w_in, s5_a_re, s5_a_im, s5_log_dt, s5_b_re, s5_b_im, s5_c_re, s5_c_im, s5_d, s5_glu_w, s5_glu_b, s5_norm_w, m2_conv_w, m2_conv_b, m2_dt_bias, m2_a_log, m2_d, m2_norm_w, w_out, norm_xa_w, norm_mem_w, xa_wq, xa_wk, xa_wv, xa_wo, norm_ffn_w, router_coarse_w, router_coarse_b, router_fine_w, router_fine_b, moe_w_gate, moe_w_up, moe_w_down, norm_final_w):
    depth = w_in.shape[0]
    assert depth == 1, "the final norm is fused into the (only) layer"
    per_layer = dict(
        norm_mix_w=norm_mix_w, w_in=w_in, s5_a_re=s5_a_re, s5_a_im=s5_a_im, s5_log_dt=s5_log_dt,
        s5_b_re=s5_b_re, s5_b_im=s5_b_im, s5_c_re=s5_c_re, s5_c_im=s5_c_im, s5_d=s5_d, s5_glu_w=s5_glu_w,
        s5_glu_b=s5_glu_b, s5_norm_w=s5_norm_w, m2_conv_w=m2_conv_w, m2_conv_b=m2_conv_b, m2_dt_bias=m2_dt_bias,
        m2_a_log=m2_a_log, m2_d=m2_d, m2_norm_w=m2_norm_w, w_out=w_out, norm_xa_w=norm_xa_w,
        norm_mem_w=norm_mem_w, xa_wq=xa_wq, xa_wk=xa_wk, xa_wv=xa_wv, xa_wo=xa_wo, norm_ffn_w=norm_ffn_w,
        router_coarse_w=router_coarse_w, router_coarse_b=router_coarse_b, router_fine_w=router_fine_w,
        router_fine_b=router_fine_b, moe_w_gate=moe_w_gate, moe_w_up=moe_w_up, moe_w_down=moe_w_down)
    p = {k: v[0] for k, v in per_layer.items()}
    for name in ('w_in', 'w_out', 'xa_wq', 'xa_wk', 'xa_wv', 'xa_wo', 's5_glu_w'):
        p[name] = p[name].astype(BF16)
    nb, seq, d = x_prompt.shape
    db, dseq, _ = x_sample.shape
    assert dseq == 1
    mem = mem_prompt.shape[1]
    xa_heads = cache_mem_k.shape[3]

    s5_prep = s5_prepare(p['s5_a_re'], p['s5_a_im'], p['s5_log_dt'], p['s5_b_re'], p['s5_b_im'],
                         p['s5_c_re'], p['s5_c_im'])

    memx = mem_prompt.reshape(nb * mem, d)
    mk = fused_matmul([memx], p['xa_wk'], n_out=d, gain=p['norm_mem_w'], tm=_row_tile(nb * mem), tn=512)
    mv = fused_matmul([memx], p['xa_wv'], n_out=d, gain=p['norm_mem_w'], tm=_row_tile(nb * mem), tn=512)
    yp, p_re, p_im, p_ssm, p_conv = _decoder_layer(
        x_prompt.reshape(nb * seq, d), p, s5_prep, norm_final_w, nb=nb, seq=seq, mem_kv=(mk, mv), xa_heads=xa_heads, states=None)

    ys, s_re, s_im, s_ssm, s_conv = _decoder_layer(
        x_sample.reshape(db, d), p, s5_prep, norm_final_w, nb=db, seq=1,
        mem_kv=(cache_mem_k[0], cache_mem_v[0]), xa_heads=xa_heads,
        states=(state_s5_re[0], state_s5_im[0], state_ssm[0], state_conv[0]))

    kv_shape = (1, nb, mem) + cache_mem_k.shape[3:]
    return (yp.reshape(nb, seq, d), ys.reshape(db, 1, d), p_re[None], p_im[None], p_ssm[None], p_conv[None],
            mk.reshape(kv_shape), mv.reshape(kv_shape), s_re[None], s_im[None], s_ssm[None], s_conv[None])
```

```python
import functools
import math

import jax
import jax.numpy as jnp
from jax import lax
from jax.experimental import pallas as pl
from jax.experimental.pallas import tpu as pltpu

F32 = jnp.float32
BF16 = jnp.bfloat16
RMS_EPS = 1e-6

V7X_VMEM_BYTES = 64 * 1024 * 1024
VMEM_LIMIT = V7X_VMEM_BYTES - 8 * 1024 * 1024
LANES = 128

S5_CH = 16
S5_N = 64
S5_Q = 16
S5_GB = 8
M2_HEADDIM = 64
M2_DSTATE = 128
M2_NGROUPS = 2
M2_CONV = 4
M2_CHUNK = 128
MOE_GROUPS = 4
MOE_PER_GROUP = 8


def _cparams(sem):
    return pltpu.CompilerParams(dimension_semantics=sem, vmem_limit_bytes=VMEM_LIMIT)


def _rms(x, w):
    return x * lax.rsqrt(jnp.mean(x * x, axis=-1, keepdims=True) + RMS_EPS) * w


def _sigmoid(x):
    return 1.0 / (1.0 + jnp.exp(-x))


def _silu(x):
    return x * _sigmoid(x)


def _softplus(x):
    return jnp.maximum(x, 0.0) + jnp.log1p(jnp.exp(-jnp.abs(x)))


def _gelu_tanh(x):
    return 0.5 * x * (1.0 + jnp.tanh(math.sqrt(2.0 / math.pi) * (x + 0.044715 * (x * x * x))))


def _dot(a, b):
    return jnp.dot(a, b, preferred_element_type=F32)


def _dot_nt(a, b):
    return lax.dot_general(a, b, (((1,), (1,)), ((), ())), preferred_element_type=F32)


def _split3(x):
    hi = x.astype(BF16)
    r = x - hi.astype(F32)
    mid = r.astype(BF16)
    lo = (r - mid.astype(F32)).astype(BF16)
    return hi, mid, lo


def _split2(x):
    hi = x.astype(BF16)
    lo = (x - hi.astype(F32)).astype(BF16)
    return hi, lo


def _mm_kernel(*refs, n_lhs, has_gain, has_res, has_side, staged):
    it = iter(refs)
    lhs = [next(it) for _ in range(n_lhs)]
    gain = next(it) if has_gain else None
    ws = [next(it) for _ in range(n_lhs)]
    side_w = next(it) if has_side else None
    res = next(it) if has_res else None
    out = next(it)
    side_out = next(it) if has_side else None
    lhs_bf = next(it) if staged else lhs

    if staged:
        @pl.when(pl.program_id(1) == 0)
        def _():
            for i in range(n_lhs):
                x = lhs[i][...]
                if has_gain:
                    x = _rms(x, gain[...])
                lhs_bf[i] = x.astype(BF16)
            if has_side:
                side_out[...] = _dot(lhs_bf[0], side_w[...].astype(BF16))

    acc = None
    for i in range(n_lhs):
        p = _dot(lhs_bf[i][...], ws[i][...].astype(BF16))
        acc = p if acc is None else acc + p
    if has_res:
        acc = acc + res[...]
    out[...] = acc.astype(out.dtype)


def fused_matmul(lhs_list, w, *, n_out, gain=None, res=None, side_w=None, out_dtype=F32, tm, tn):
    n_lhs = len(lhs_list)
    m, kp = lhs_list[0].shape
    assert all(a.shape == (m, kp) for a in lhs_list)
    assert w.shape[0] == n_lhs * kp and m % tm == 0 and n_out % tn == 0
    assert gain is None or n_lhs == 1
    staged = gain is not None or any(a.dtype != BF16 for a in lhs_list)
    assert staged or side_w is None
    grid = (m // tm, n_out // tn)
    in_specs = [pl.BlockSpec((tm, kp), lambda i, j: (i, 0)) for _ in range(n_lhs)]
    args = list(lhs_list)
    if gain is not None:
        in_specs.append(pl.BlockSpec((1, kp), lambda i, j: (0, 0)))
        args.append(gain.reshape(1, kp))
    for p in range(n_lhs):
        in_specs.append(pl.BlockSpec((kp, tn), lambda i, j, p=p: (p, j)))
        args.append(w)
    if side_w is not None:
        in_specs.append(pl.BlockSpec((kp, LANES), lambda i, j: (0, 0)))
        args.append(side_w)
    if res is not None:
        in_specs.append(pl.BlockSpec((tm, tn), lambda i, j: (i, j)))
        args.append(res)
    out_shape = [jax.ShapeDtypeStruct((m, n_out), out_dtype)]
    out_specs = [pl.BlockSpec((tm, tn), lambda i, j: (i, j))]
    if side_w is not None:
        out_shape.append(jax.ShapeDtypeStruct((m, LANES), F32))
        out_specs.append(pl.BlockSpec((tm, LANES), lambda i, j: (i, 0)))
    outs = pl.pallas_call(
        functools.partial(_mm_kernel, n_lhs=n_lhs, has_gain=gain is not None,
                          has_res=res is not None, has_side=side_w is not None, staged=staged),
        grid=grid, in_specs=in_specs, out_specs=out_specs, out_shape=out_shape,
        scratch_shapes=[pltpu.VMEM((n_lhs, tm, kp), BF16)] if staged else [],
        compiler_params=_cparams(("parallel", "arbitrary")),
        name="fused_matmul",
    )(*args)
    return outs if side_w is not None else outs[0]


def _s5_prep_kernel(lre_ref, lim_ref, ldt_ref, btre_ref, btim_ref, cre_ref, cim_ref,
                    tz_ref, wsre_ref, wsim_ref, wcre_ref, wcim_ref, aq_ref, ab_ref, bbt_ref):
    q, ch = S5_Q, S5_CH
    lr = lre_ref[...]
    li = lim_ref[...]
    step = jnp.exp(ldt_ref[...])
    mag = jnp.exp(lr * step)
    ab_re = mag * jnp.cos(li * step)
    ab_im = mag * jnp.sin(li * step)
    den = lr * lr + li * li
    num_re = ab_re - 1.0
    coef_re = (num_re * lr + ab_im * li) / den
    coef_im = (ab_im * lr - num_re * li) / den
    bt_re = btre_ref[...]
    bt_im = btim_ref[...]
    bb_re = coef_re * bt_re - coef_im * bt_im
    bb_im = coef_re * bt_im + coef_im * bt_re
    c_re = cre_ref[...]
    c_im = cim_ref[...]

    pw = [(jnp.ones_like(ab_re), jnp.zeros_like(ab_re))]
    for _ in range(q):
        pr, pi = pw[-1]
        pw.append((pr * ab_re - pi * ab_im, pr * ab_im + pi * ab_re))

    ca_re = [c_re * pr - c_im * pi for pr, pi in pw]
    ca_im = [c_re * pi + c_im * pr for pr, pi in pw]
    wcre_ref[...] = jnp.concatenate(ca_re[1:], axis=0).astype(BF16)
    wcim_ref[...] = jnp.concatenate([-x for x in ca_im[1:]], axis=0).astype(BF16)

    pr_stack = jnp.concatenate(ca_re[:q], axis=0)
    pi_stack = jnp.concatenate(ca_im[:q], axis=0)
    krow = None
    for a, b, sign in ((bb_re, pr_stack, 1.0), (bb_im, pi_stack, -1.0)):
        a_hi, a_lo = _split2(a)
        b_hi, b_lo = _split2(b)
        t = _dot_nt(a_hi, b_hi) + (_dot_nt(a_hi, b_lo) + _dot_nt(a_lo, b_hi))
        krow = sign * t if krow is None else krow + sign * t
    lane = lax.broadcasted_iota(jnp.int32, krow.shape, 1)
    blocks = [krow]
    for s in range(1, q):
        blocks.append(jnp.where(lane >= s * ch, pltpu.roll(krow, s * ch, 1), 0.0))
    tz_ref[...] = jnp.concatenate(blocks, axis=0).astype(BF16)

    ws_re, ws_im = [], []
    for s in range(q):
        pr, pi = pw[q - 1 - s]
        ws_re.append(bb_re * pr - bb_im * pi)
        ws_im.append(bb_re * pi + bb_im * pr)
    wsre_ref[...] = jnp.concatenate(ws_re, axis=0).astype(BF16)
    wsim_ref[...] = jnp.concatenate(ws_im, axis=0).astype(BF16)

    aq_ref[0:1, :] = pw[q][0]
    aq_ref[1:2, :] = pw[q][1]
    ab_ref[0:1, :] = ab_re
    ab_ref[1:2, :] = ab_im
    bbt_ref[0:ch, :] = bb_re
    bbt_ref[ch:2 * ch, :] = bb_im


def s5_prepare(a_re, a_im, log_dt, b_re, b_im, c_re, c_im):
    g, n = a_re.shape
    ch, q = S5_CH, S5_Q
    qc = q * ch
    bt_re = jnp.swapaxes(b_re, 1, 2)
    bt_im = jnp.swapaxes(b_im, 1, 2)

    def per_g(*dims):
        return pl.BlockSpec((None,) + dims, lambda i: (i,) + (0,) * len(dims))

    return pl.pallas_call(
        _s5_prep_kernel,
        grid=(g,),
        in_specs=[per_g(1, n), per_g(1, n), per_g(1, 1), per_g(ch, n), per_g(ch, n), per_g(ch, n), per_g(ch, n)],
        out_specs=[per_g(qc, qc), per_g(qc, n), per_g(qc, n), per_g(qc, n), per_g(qc, n),
                   per_g(2, n), per_g(2, n), per_g(2 * ch, n)],
        out_shape=[jax.ShapeDtypeStruct((g, qc, qc), BF16),
                   jax.ShapeDtypeStruct((g, qc, n), BF16), jax.ShapeDtypeStruct((g, qc, n), BF16),
                   jax.ShapeDtypeStruct((g, qc, n), BF16), jax.ShapeDtypeStruct((g, qc, n), BF16),
                   jax.ShapeDtypeStruct((g, 2, n), F32), jax.ShapeDtypeStruct((g, 2, n), F32),
                   jax.ShapeDtypeStruct((g, 2 * ch, n), F32)],
        compiler_params=_cparams(("parallel",)),
        name="s5_prepare",
    )(a_re.reshape(g, 1, n), a_im.reshape(g, 1, n), log_dt.reshape(g, 1, 1), bt_re, bt_im, c_re, c_im)


def _s5_scan_kernel(h_ref, tz_ref, wsre_ref, wsim_ref, wcre_ref, wcim_ref, aq_ref,
                    y_ref, hfin_ref, xs_sc, u_sc, yg_sc, sre_sc, sim_sc, *, nb, nchunk):
    gb, q, ch = S5_GB, S5_Q, S5_CH
    rows = nb * nchunk
    per_vreg = LANES // ch
    assert gb == per_vreg and q % per_vreg == 0
    slot = lax.broadcasted_iota(jnp.int32, (rows, LANES), 1) // ch

    halves = q // per_vreg

    def rot_rows(w, g):
        if g == 0:
            return w
        cut = (per_vreg - g) * ch
        parts = []
        for hf in range(halves):
            blk = w[hf * LANES:(hf + 1) * LANES]
            parts += [blk[cut:], blk[:cut]]
        return jnp.concatenate(parts, axis=0)

    for s in range(q):
        x = h_ref[pl.ds(s, rows, stride=q), :].astype(BF16)
        k = s % per_vreg
        xs_sc[s] = pltpu.roll(x, k * ch, 1) if k else x
    keep = [jnp.where(slot == j, 1.0, 0.0).astype(BF16) for j in range(per_vreg)]
    for g in range(gb):
        for hf in range(halves):
            acc = None
            for k in range(per_vreg):
                piece = xs_sc[hf * per_vreg + k] * keep[(g + k) % per_vreg]
                acc = piece if acc is None else acc + piece
            u_sc[g, :, hf * LANES:(hf + 1) * LANES] = acc

    for g in range(gb):
        u = u_sc[g]
        tz = rot_rows(tz_ref[g], g)
        tz = jnp.concatenate([pltpu.roll(tz[:, hf * LANES:(hf + 1) * LANES], g * ch, 1) if g
                              else tz[:, hf * LANES:(hf + 1) * LANES] for hf in range(halves)], axis=1)
        yg_sc[g] = _dot(u, tz)
        sre_sc[g] = _dot(u, rot_rows(wsre_ref[g], g))
        sim_sc[g] = _dot(u, rot_rows(wsim_ref[g], g))

    ar = [jnp.broadcast_to(aq_ref[g, 0:1, :], (nb, S5_N)) for g in range(gb)]
    ai = [jnp.broadcast_to(aq_ref[g, 1:2, :], (nb, S5_N)) for g in range(gb)]

    def step(c, carry):
        at = pl.ds(c, nb, stride=nchunk)
        new = []
        for g in range(gb):
            hr, hi = carry[g]
            sr = sre_sc[g, at, :]
            si = sim_sc[g, at, :]
            sre_sc[g, at, :] = hr
            sim_sc[g, at, :] = hi
            new.append((ar[g] * hr - ai[g] * hi + sr, ar[g] * hi + ai[g] * hr + si))
        return tuple(new)

    zero = jnp.zeros((nb, S5_N), F32)
    fin = lax.fori_loop(0, nchunk, step, tuple((zero, zero) for _ in range(gb)))
    for g in range(gb):
        hfin_ref[g, 0] = fin[g][0]
        hfin_ref[g, 1] = fin[g][1]
        yg_sc[g] += (_dot_nt(sre_sc[g].astype(BF16), rot_rows(wcre_ref[g], g))
                     + _dot_nt(sim_sc[g].astype(BF16), rot_rows(wcim_ref[g], g)))

    for t in range(q):
        hf, k = divmod(t, per_vreg)
        acc = jnp.zeros((rows, LANES), F32)
        for g in range(gb):
            acc = jnp.where(slot == (k + g) % per_vreg, yg_sc[g, :, hf * LANES:(hf + 1) * LANES], acc)
        y_ref[pl.ds(t, rows, stride=q), :] = pltpu.roll(acc, (per_vreg - k) * ch, 1) if k else acc


def s5_scan(h, prep, nb, seq):
    tz, ws_re, ws_im, wc_re, wc_im, aq = prep[:6]
    g, qc, _ = tz.shape
    n = S5_N
    gb = S5_GB
    q = S5_Q
    assert gb * S5_CH == LANES and qc == q * S5_CH and seq % q == 0
    nchunk = seq // q
    rows = nb * nchunk
    m = nb * seq

    def blk(*dims):
        return pl.BlockSpec((gb,) + dims, lambda i: (i,) + (0,) * len(dims))

    return pl.pallas_call(
        functools.partial(_s5_scan_kernel, nb=nb, nchunk=nchunk),
        grid=(g // gb,),
        in_specs=[pl.BlockSpec((m, LANES), lambda i: (0, i)),
                  blk(qc, qc), blk(qc, n), blk(qc, n), blk(qc, n), blk(qc, n), blk(2, n)],
        out_specs=[pl.BlockSpec((m, LANES), lambda i: (0, i)), blk(2, nb, n)],
        out_shape=[jax.ShapeDtypeStruct((m, g * S5_CH), F32), jax.ShapeDtypeStruct((g, 2, nb, n), F32)],
        scratch_shapes=[pltpu.VMEM((q, rows, LANES), BF16), pltpu.VMEM((gb, rows, qc), BF16),
                        pltpu.VMEM((gb, rows, qc), F32),
                        pltpu.VMEM((gb, rows, n), F32), pltpu.VMEM((gb, rows, n), F32)],
        compiler_params=_cparams(("parallel",)),
        name="s5_scan",
    )(h, tz, ws_re, ws_im, wc_re, wc_im, aq)


def _s5_step_kernel(u_ref, hre_ref, him_ref, ab_ref, bbt_ref, cre_ref, cim_ref, y_ref, ore_ref, oim_ref):
    gb = u_ref.shape[0]
    ch = S5_CH
    for g in range(gb):
        u = u_ref[g].astype(BF16)
        bb_re = bbt_ref[g, 0:ch, :].astype(BF16)
        bb_im = bbt_ref[g, ch:2 * ch, :].astype(BF16)
        ar = ab_ref[g, 0:1, :]
        ai = ab_ref[g, 1:2, :]
        hr0 = hre_ref[g]
        hi0 = him_ref[g]
        hr = _dot(u, bb_re) + (ar * hr0 - ai * hi0)
        hi = _dot(u, bb_im) + (ar * hi0 + ai * hr0)
        ore_ref[g] = hr
        oim_ref[g] = hi
        y_ref[g] = (_dot_nt(hr.astype(BF16), cre_ref[g].astype(BF16))
                    - _dot_nt(hi.astype(BF16), cim_ref[g].astype(BF16)))


def s5_step(u_t, h_re, h_im, prep, c_re, c_im):
    ab, bbt = prep[6], prep[7]
    g, b, ch = u_t.shape
    n = S5_N
    gb = S5_GB

    def blk(*dims):
        return pl.BlockSpec((gb,) + dims, lambda i: (i,) + (0,) * len(dims))

    return pl.pallas_call(
        _s5_step_kernel,
        grid=(g // gb,),
        in_specs=[blk(b, ch), blk(b, n), blk(b, n), blk(2, n), blk(2 * ch, n), blk(ch, n), blk(ch, n)],
        out_specs=[blk(b, ch), blk(b, n), blk(b, n)],
        out_shape=[jax.ShapeDtypeStruct((g, b, ch), F32), jax.ShapeDtypeStruct((g, b, n), F32),
                   jax.ShapeDtypeStruct((g, b, n), F32)],
        compiler_params=_cparams(("parallel",)),
        name="s5_step",
    )(u_t, h_re, h_im, ab, bbt, c_re, c_im)


def _s5_head_kernel(y_ref, u_ref, d_ref, w_ref, b_ref, nw_ref, o_ref):
    y = y_ref[...] + d_ref[...] * u_ref[...]
    g = _gelu_tanh(y)
    gate = _sigmoid(_dot(g.astype(BF16), w_ref[...].astype(BF16)) + b_ref[...])
    o_ref[...] = _rms(g * gate, nw_ref[...]).astype(o_ref.dtype)


def s5_head(y, h, d, glu_w, glu_b, norm_w, *, tm):
    m, ds = y.shape
    row = lambda a: a.reshape(1, ds)
    vec = pl.BlockSpec((1, ds), lambda i: (0, 0))
    return pl.pallas_call(
        _s5_head_kernel,
        grid=(m // tm,),
        in_specs=[pl.BlockSpec((tm, ds), lambda i: (i, 0)), pl.BlockSpec((tm, ds), lambda i: (i, 0)), vec,
                  pl.BlockSpec((ds, ds), lambda i: (0, 0)), vec, vec],
        out_specs=pl.BlockSpec((tm, ds), lambda i: (i, 0)),
        out_shape=jax.ShapeDtypeStruct((m, ds), BF16),
        compiler_params=_cparams(("parallel",)),
        name="s5_head",
    )(y, h, row(d), glu_w, row(glu_b), row(norm_w))


def _pair_select(first, col0, col1, shape):
    return jnp.where(first, jnp.broadcast_to(col0, shape), jnp.broadcast_to(col1, shape))


def _ssd_chunk_kernel(*refs, d_inner, nheads, n_xparts):
    xparts = refs[:n_xparts]
    (z_ref, dt_ref, cw_ref, cb_ref, dtb_ref, alog_ref, dvec_ref, nw_ref,
     out_ref, ssm_ref, conv_ref, state_sc, xpad_sc, y_sc) = refs[n_xparts:]
    c = pl.program_id(1)
    q = M2_CHUNK
    hp = M2_HEADDIM
    ns = M2_DSTATE
    heads_per_group = nheads // M2_NGROUPS
    halo = 8

    @pl.when(c == 0)
    def _():
        state_sc[...] = jnp.zeros_like(state_sc)
        xpad_sc[0:halo, :] = jnp.zeros((halo, xpad_sc.shape[1]), F32)

    wpart = xparts[0].shape[1]
    for i, xr in enumerate(xparts):
        xpad_sc[halo:halo + q, i * wpart:(i + 1) * wpart] = xr[...]
    cw = cw_ref[...]
    conv = cb_ref[...] + cw[M2_CONV - 1:M2_CONV, :] * xpad_sc[halo:halo + q, :]
    for k in range(1, M2_CONV):
        conv = conv + cw[M2_CONV - 1 - k:M2_CONV - k, :] * xpad_sc[halo - k:halo - k + q, :]
    xpad_sc[0:halo, :] = xpad_sc[q:q + halo, :]
    xc = _silu(conv)

    dt = _softplus(dt_ref[...] + dtb_ref[...])
    a = -jnp.exp(alog_ref[...])
    da = dt * a
    row = lax.broadcasted_iota(jnp.int32, (q, q), 0)
    col = lax.broadcasted_iota(jnp.int32, (q, q), 1)
    causal = row >= col
    tri = jnp.where(causal, 1.0, 0.0).astype(BF16)
    d_hi, d_mid, d_lo = _split3(da)
    acum = _dot(tri, d_hi) + (_dot(tri, d_mid) + _dot(tri, d_lo))
    acum_t = acum.T
    alast = acum[q - 1:q, :]
    first = col < hp
    first_rows = row < hp

    for pr in range(nheads // 2):
        grp = (2 * pr) // heads_per_group
        b_bf = xc[:, d_inner + grp * ns:d_inner + (grp + 1) * ns].astype(BF16)
        c_bf = xc[:, d_inner + (M2_NGROUPS + grp) * ns:d_inner + (M2_NGROUPS + grp + 1) * ns].astype(BF16)
        cb = _dot_nt(c_bf, b_bf)
        xpair = xc[:, pr * 2 * hp:(pr + 1) * 2 * hp]
        h0, h1 = 2 * pr, 2 * pr + 1
        acol = [acum[:, h:h + 1] for h in (h0, h1)]
        m = []
        for k, h in enumerate((h0, h1)):
            seg = jnp.broadcast_to(acol[k], (q, q)) - jnp.broadcast_to(acum_t[h:h + 1, :], (q, q))
            lmat = jnp.exp(jnp.where(causal, seg, -1e30))
            m.append((cb * lmat).astype(BF16))
        dtp = _pair_select(first, dt[:, h0:h0 + 1], dt[:, h1:h1 + 1], (q, q))
        xdt = xpair * dtp
        xdt_bf = xdt.astype(BF16)
        y_diag = jnp.where(first, _dot(m[0], xdt_bf), _dot(m[1], xdt_bf))
        dec_end = _pair_select(first, jnp.exp(alast[:, h0:h0 + 1] - acol[0]),
                               jnp.exp(alast[:, h1:h1 + 1] - acol[1]), (q, q))
        xw_t = (xdt * dec_end).T.astype(BF16)
        chunk_state = _dot(xw_t, b_bf)
        rows = pl.ds(pr * 2 * hp, 2 * hp)
        prev = state_sc[rows, :]
        y_off = _dot_nt(c_bf, prev.astype(BF16)) * _pair_select(first, jnp.exp(acol[0]), jnp.exp(acol[1]), (q, q))
        sdec = jnp.where(first_rows, jnp.broadcast_to(jnp.exp(alast[:, h0:h0 + 1]), (q, q)),
                         jnp.broadcast_to(jnp.exp(alast[:, h1:h1 + 1]), (q, q)))
        state_sc[rows, :] = prev * sdec + chunk_state
        y_sc[:, pr * 2 * hp:(pr + 1) * 2 * hp] = y_diag + y_off + dvec_ref[:, pr * 2 * hp:(pr + 1) * 2 * hp] * xpair

    out_ref[...] = _rms(y_sc[...] * _silu(z_ref[...]), nw_ref[...]).astype(out_ref.dtype)

    @pl.when(c == pl.num_programs(1) - 1)
    def _():
        ssm_ref[...] = state_sc[...]
        conv_ref[...] = xpad_sc[halo + q - (M2_CONV - 1):halo + q, :]


def ssd_prompt(h, dt_raw, nb, seq, p, *, d_s5, d_inner, nheads):
    q = M2_CHUNK
    nc = seq // q
    conv_dim = d_inner + 2 * M2_NGROUPS * M2_DSTATE
    xw = 512
    xoff = d_s5 + d_inner
    assert d_s5 % d_inner == 0 and xoff % xw == 0 and conv_dim % xw == 0
    assert M2_CHUNK == 2 * M2_HEADDIM == M2_DSTATE == LANES
    zblk = d_s5 // d_inner
    n_xparts = conv_dim // xw
    m = nb * seq
    pad = lambda v: jnp.pad(v, (0, LANES - v.shape[0])).reshape(1, LANES)
    dvec = jnp.repeat(p['m2_d'], M2_HEADDIM).reshape(1, d_inner)
    vec = lambda n: pl.BlockSpec((1, n), lambda b, c: (0, 0))
    tok = lambda w, j: pl.BlockSpec((q, w), lambda b, c, j=j: (b * nc + c, j))
    out, ssm, conv = pl.pallas_call(
        functools.partial(_ssd_chunk_kernel, d_inner=d_inner, nheads=nheads, n_xparts=n_xparts),
        grid=(nb, nc),
        in_specs=[tok(xw, xoff // xw + i) for i in range(n_xparts)] + [tok(d_inner, zblk), tok(LANES, 0),
                  pl.BlockSpec((M2_CONV, conv_dim), lambda b, c: (0, 0)), vec(conv_dim), vec(LANES), vec(LANES),
                  vec(d_inner), vec(d_inner)],
        out_specs=[tok(d_inner, 0),
                   pl.BlockSpec((None, nheads * M2_HEADDIM, M2_DSTATE), lambda b, c: (b, 0, 0)),
                   pl.BlockSpec((None, M2_CONV - 1, conv_dim), lambda b, c: (b, 0, 0))],
        out_shape=[jax.ShapeDtypeStruct((m, d_inner), BF16),
                   jax.ShapeDtypeStruct((nb, nheads * M2_HEADDIM, M2_DSTATE), F32),
                   jax.ShapeDtypeStruct((nb, M2_CONV - 1, conv_dim), F32)],
        scratch_shapes=[pltpu.VMEM((nheads * M2_HEADDIM, M2_DSTATE), F32),
                        pltpu.VMEM((q + 8, conv_dim), F32),
                        pltpu.VMEM((q, d_inner), F32)],
        compiler_params=_cparams(("parallel", "arbitrary")),
        name="ssd_chunk",
    )(*([h] * n_xparts), h, dt_raw, p['m2_conv_w'], p['m2_conv_b'].reshape(1, conv_dim), pad(p['m2_dt_bias']),
      pad(p['m2_a_log']), dvec, p['m2_norm_w'].reshape(1, d_inner))
    return out, ssm.reshape(nb, nheads, M2_HEADDIM, M2_DSTATE), conv


SSD_STEP_SEQS = 4


def _ssd_step_kernel(*refs, d_inner, nheads, n_xparts):
    xparts = refs[:n_xparts]
    (z_ref, dt_ref, cs0_ref, cs1_ref, cs2_ref, cw_ref, cb_ref, dtb_ref, alog_ref, dvec_ref, nw_ref, st_ref,
     out_ref, so_ref, lhs_sc, bfull_sc, ct_sc, yt_sc, xs_sc) = refs[n_xparts:]
    i = pl.program_id(0)
    nb = z_ref.shape[0]
    ns = M2_DSTATE
    rows_g = (nheads // M2_NGROUPS) * M2_HEADDIM

    @pl.when(i == 0)
    def _():
        cw = cw_ref[...]
        xbc = jnp.concatenate([xr[...] for xr in xparts], axis=1)
        conv = (cb_ref[...] + cw[3:4, :] * xbc + cw[2:3, :] * cs2_ref[...]
                + cw[1:2, :] * cs1_ref[...] + cw[0:1, :] * cs0_ref[...])
        xc = _silu(conv)
        dt = _softplus(dt_ref[...] + dtb_ref[...])
        dec = jnp.exp(dt * (-jnp.exp(alog_ref[...])))
        hrow = lax.broadcasted_iota(jnp.int32, (LANES, d_inner), 0)
        hcol = lax.broadcasted_iota(jnp.int32, (LANES, d_inner), 1)
        expand = jnp.where(hcol // M2_HEADDIM == hrow, 1.0, 0.0).astype(BF16)

        def expand_heads(v):
            a, b_, c = _split3(v)
            return _dot(a, expand) + (_dot(b_, expand) + _dot(c, expand))

        xs = xc[:, :d_inner]
        xs_sc[...] = xs
        xdt_t = (xs * expand_heads(dt)).T
        d_hi, d_mid, d_lo = _split3(expand_heads(dec).T)
        for g in range(M2_NGROUPS):
            r = slice(g * rows_g, (g + 1) * rows_g)
            lhs_sc[g] = jnp.concatenate([xdt_t[r].astype(BF16), d_hi[r], d_mid[r], d_lo[r]], axis=1)
            b_g = xc[:, d_inner + g * ns:d_inner + (g + 1) * ns]
            bfull_sc[g] = jnp.concatenate([b_g, jnp.zeros_like(b_g)], axis=1)
            c_g = xc[:, d_inner + (M2_NGROUPS + g) * ns:d_inner + (M2_NGROUPS + g + 1) * ns]
            ct_sc[g] = c_g.T
        yt_sc[...] = jnp.zeros_like(yt_sc)

    row_id = lax.broadcasted_iota(jnp.int32, (nb, 2 * ns), 0)
    lane_id = lax.broadcasted_iota(jnp.int32, (nb, 2 * ns), 1)
    col_id = lax.broadcasted_iota(jnp.int32, (ns, nb), 1)
    for j in range(st_ref.shape[0]):
        b = i * st_ref.shape[0] + j
        r_bot = jnp.where((row_id == b) & (lane_id >= ns), 1.0, 0.0).astype(BF16)
        for g in range(M2_NGROUPS):
            r = pl.ds(g * rows_g, rows_g)
            r_top = jnp.where(row_id == b, bfull_sc[g], 0.0).astype(BF16)
            rhs = jnp.concatenate([r_top, r_bot, r_bot, r_bot], axis=0)
            o = _dot(lhs_sc[g], rhs)
            hnew = st_ref[j, r, :] * o[:, ns:] + o[:, :ns]
            so_ref[j, r, :] = hnew
            cm = jnp.where(col_id == b, ct_sc[g], 0.0).astype(BF16)
            yt_sc[r, :] += _dot(hnew.astype(BF16), cm)

    @pl.when(i == pl.num_programs(0) - 1)
    def _():
        y = yt_sc[...].T + dvec_ref[...] * xs_sc[...]
        out_ref[...] = _rms(y * _silu(z_ref[...]), nw_ref[...]).astype(out_ref.dtype)


def ssd_sample(h, dt_raw, state, conv_state, p, *, d_s5, d_inner, nheads):
    nb = h.shape[0]
    conv_dim = d_inner + 2 * M2_NGROUPS * M2_DSTATE
    xw = 512
    xoff = d_s5 + d_inner
    assert nb == LANES and M2_DSTATE == LANES and M2_CONV == 4
    assert xoff % xw == 0 and conv_dim % xw == 0 and d_s5 % d_inner == 0 and nb % SSD_STEP_SEQS == 0
    n_xparts = conv_dim // xw
    rows = nheads * M2_HEADDIM
    rows_g = rows // M2_NGROUPS
    pad = lambda v: jnp.pad(v, (0, LANES - v.shape[0])).reshape(1, LANES)
    dvec = jnp.repeat(p['m2_d'], M2_HEADDIM).reshape(1, d_inner)
    full = lambda a, b, j=0: pl.BlockSpec((a, b), lambda i, j=j: (0, j))
    st_spec = pl.BlockSpec((SSD_STEP_SEQS, rows, M2_DSTATE), lambda i: (i, 0, 0))
    out, new_state = pl.pallas_call(
        functools.partial(_ssd_step_kernel, d_inner=d_inner, nheads=nheads, n_xparts=n_xparts),
        grid=(nb // SSD_STEP_SEQS,),
        in_specs=[full(nb, xw, xoff // xw + k) for k in range(n_xparts)]
        + [full(nb, d_inner, d_s5 // d_inner), full(nb, LANES)]
        + [full(nb, conv_dim)] * 3
        + [full(M2_CONV, conv_dim), full(1, conv_dim), full(1, LANES), full(1, LANES), full(1, d_inner),
           full(1, d_inner), st_spec],
        out_specs=[full(nb, d_inner), st_spec],
        out_shape=[jax.ShapeDtypeStruct((nb, d_inner), BF16), jax.ShapeDtypeStruct((nb, rows, M2_DSTATE), F32)],
        scratch_shapes=[pltpu.VMEM((M2_NGROUPS, rows_g, 4 * nb), BF16),
                        pltpu.VMEM((M2_NGROUPS, nb, 2 * M2_DSTATE), F32),
                        pltpu.VMEM((M2_NGROUPS, M2_DSTATE, nb), F32),
                        pltpu.VMEM((rows, nb), F32),
                        pltpu.VMEM((nb, d_inner), F32)],
        compiler_params=_cparams(("arbitrary",)),
        name="ssd_step",
    )(*([h] * n_xparts), h, dt_raw, conv_state[:, 0], conv_state[:, 1], conv_state[:, 2],
      p['m2_conv_w'], p['m2_conv_b'].reshape(1, conv_dim), pad(p['m2_dt_bias']), pad(p['m2_a_log']),
      dvec, p['m2_norm_w'].reshape(1, d_inner), state.reshape(nb, rows, M2_DSTATE))
    xbc = lax.slice_in_dim(h, xoff, xoff + conv_dim, axis=1)
    new_conv = jnp.concatenate([conv_state[:, 1:], xbc[:, None, :]], axis=1)
    return out, new_state.reshape(state.shape), new_conv


def _softmax_rows(s):
    e = jnp.exp(s - jnp.max(s, axis=-1, keepdims=True))
    return e / jnp.sum(e, axis=-1, keepdims=True)


def _attn_kernel(q_ref, k_ref, v_ref, o_ref, *, scale):
    s = _dot_nt(q_ref[...].astype(BF16), k_ref[...].astype(BF16)) * scale
    o_ref[...] = _dot(_softmax_rows(s).astype(BF16), v_ref[...].astype(BF16)).astype(o_ref.dtype)


def attention_prompt(q, k, v, nb, seq, mem, heads, *, tq):
    d = q.shape[1]
    hd = d // heads
    nq = seq // tq
    kv_spec = pl.BlockSpec((mem, hd), lambda b, h, i: (b, h))
    q_spec = pl.BlockSpec((tq, hd), lambda b, h, i: (b * nq + i, h))
    return pl.pallas_call(
        functools.partial(_attn_kernel, scale=hd ** -0.5),
        grid=(nb, heads, nq),
        in_specs=[q_spec, kv_spec, kv_spec],
        out_specs=q_spec,
        out_shape=jax.ShapeDtypeStruct(q.shape, BF16),
        compiler_params=_cparams(("parallel", "parallel", "parallel")),
        name="attention_prompt",
    )(q, k, v)


ATTN_STEP_SEQS = 2


ATTN_STEP_ROWS = 64


def _attn_step_kernel(q_ref, k_ref, v_ref, o_ref, s_sc, *, scale):
    nseq, mem, heads, hd = k_ref.shape
    ch = ATTN_STEP_ROWS
    pack = 8 // heads
    for j in range(nseq):
        q = jnp.concatenate([q_ref[j]] * pack, axis=0)

        def score(c, mx):
            rows = pl.ds(pl.multiple_of(c * ch, ch), ch)
            k = k_ref[j, rows].reshape(ch // pack, pack * heads, hd)
            s = jnp.sum(k * q, axis=-1, keepdims=True) * scale
            s_sc[pl.ds(pl.multiple_of(c * (ch // pack), ch // pack), ch // pack)] = s
            return jnp.maximum(mx, jnp.max(s, axis=0))

        mx = lax.fori_loop(0, mem // ch, score, jnp.full((pack * heads, 1), NEG, F32))
        mx1 = mx[0:heads]
        for i in range(1, pack):
            mx1 = jnp.maximum(mx1, mx[i * heads:(i + 1) * heads])
        mx = jnp.concatenate([mx1] * pack, axis=0)

        def accum(c, carry):
            den, acc = carry
            rows = pl.ds(pl.multiple_of(c * ch, ch), ch)
            v = v_ref[j, rows].reshape(ch // pack, pack * heads, hd)
            e = jnp.exp(s_sc[pl.ds(pl.multiple_of(c * (ch // pack), ch // pack), ch // pack)] - mx)
            return den + jnp.sum(e, axis=0), acc + jnp.sum(e * v, axis=0)

        den, acc = lax.fori_loop(0, mem // ch, accum,
                                 (jnp.zeros((pack * heads, 1), F32), jnp.zeros((pack * heads, hd), F32)))
        den1, acc1 = den[0:heads], acc[0:heads]
        for i in range(1, pack):
            den1 = den1 + den[i * heads:(i + 1) * heads]
            acc1 = acc1 + acc[i * heads:(i + 1) * heads]
        o_ref[j] = acc1 / den1


def attention_sample(q, k_cache, v_cache):
    b, mem, heads, hd = k_cache.shape
    nseq = ATTN_STEP_SEQS
    q_spec = pl.BlockSpec((nseq, heads, hd), lambda i: (i, 0, 0))
    kv_spec = pl.BlockSpec((nseq, mem, heads, hd), lambda i: (i, 0, 0, 0))
    out = pl.pallas_call(
        functools.partial(_attn_step_kernel, scale=hd ** -0.5),
        grid=(b // nseq,),
        in_specs=[q_spec, kv_spec, kv_spec],
        out_specs=q_spec,
        out_shape=jax.ShapeDtypeStruct((b, heads, hd), F32),
        scratch_shapes=[pltpu.VMEM((mem * heads // 8, 8, 1), F32)],
        compiler_params=_cparams(("parallel",)),
        name="attention_step",
    )(q.reshape(b, heads, hd), k_cache, v_cache)
    return out.reshape(b, heads * hd)


NEG = -1e30


INFO_G1, INFO_G2, INFO_E1, INFO_E2 = 0, 1, 2, 3


def _lane_pack(lane, values):
    out = 0.0
    for k, v in values:
        out = jnp.where(lane == k, v, out)
    return out


def _router_kernel(x_ref, nw_ref, wr_ref, br_ref, xn_ref, sel_ref, info_ref, *, n_experts):
    xn = _rms(x_ref[...], nw_ref[...])
    xn_ref[...] = xn
    x_hi, x_lo = _split2(xn)
    w_hi, w_lo = _split2(wr_ref[...])
    logits = _dot(x_hi, w_hi) + (_dot(x_hi, w_lo) + _dot(x_lo, w_hi)) + br_ref[...]
    lane = lax.broadcasted_iota(jnp.int32, logits.shape, 1)
    big = jnp.int32(2 ** 30)
    is_c = (lane >= n_experts) & (lane < n_experts + MOE_GROUPS)
    lc = jnp.where(is_c, logits, NEG)
    cmax = jnp.max(lc, axis=-1, keepdims=True)
    gsel = jnp.min(jnp.where(lc == cmax, lane, big), axis=-1, keepdims=True) - n_experts
    gate_c = 1.0 / jnp.sum(jnp.where(is_c, jnp.exp(lc - cmax), 0.0), axis=-1, keepdims=True)
    in_group = (lane < n_experts) & (lane // MOE_PER_GROUP == gsel)
    lf = jnp.where(in_group, logits, NEG)
    t1 = jnp.max(lf, axis=-1, keepdims=True)
    i1 = jnp.min(jnp.where(lf == t1, lane, big), axis=-1, keepdims=True)
    lf2 = jnp.where(lane == i1, NEG, lf)
    t2 = jnp.max(lf2, axis=-1, keepdims=True)
    i2 = jnp.min(jnp.where(lf2 == t2, lane, big), axis=-1, keepdims=True)
    r = jnp.exp(t2 - t1)
    g1 = gate_c / (1.0 + r)
    g2 = gate_c * r / (1.0 + r)
    sel_ref[...] = jnp.where((lane == i1) | (lane == i2), 1.0, 0.0)
    info_ref[...] = _lane_pack(lane, ((INFO_G1, g1), (INFO_G2, g2), (INFO_E1, i1.astype(F32)),
                                      (INFO_E2, i2.astype(F32))))


def moe_router(x, norm_w, w_coarse, b_coarse, w_fine, b_fine, *, tm):
    m, d = x.shape
    e = w_fine.shape[1]
    padw = LANES - e - MOE_GROUPS
    wr = jnp.concatenate([w_fine, w_coarse, jnp.zeros((d, padw), F32)], axis=1)
    br = jnp.concatenate([b_fine, b_coarse, jnp.zeros((padw,), F32)]).reshape(1, LANES)
    tok = pl.BlockSpec((tm, LANES), lambda i: (i, 0))
    return pl.pallas_call(
        functools.partial(_router_kernel, n_experts=e),
        grid=(m // tm,),
        in_specs=[pl.BlockSpec((tm, d), lambda i: (i, 0)), pl.BlockSpec((1, d), lambda i: (0, 0)),
                  pl.BlockSpec((d, LANES), lambda i: (0, 0)), pl.BlockSpec((1, LANES), lambda i: (0, 0))],
        out_specs=[pl.BlockSpec((tm, d), lambda i: (i, 0)), tok, tok],
        out_shape=[jax.ShapeDtypeStruct((m, d), F32), jax.ShapeDtypeStruct((m, LANES), F32),
                   jax.ShapeDtypeStruct((m, LANES), F32)],
        compiler_params=_cparams(("parallel",)),
        name="moe_router",
    )(x, norm_w.reshape(1, d), wr, br)


def _moe_rank_kernel(sel_ref, info_ref, rank_ref, counts_ref, carry_sc):
    i = pl.program_id(0)

    @pl.when(i == 0)
    def _():
        carry_sc[...] = jnp.zeros_like(carry_sc)

    sel = sel_ref[...]
    tm = sel.shape[0]
    row = lax.broadcasted_iota(jnp.int32, (tm, tm), 0)
    col = lax.broadcasted_iota(jnp.int32, (tm, tm), 1)
    before = jnp.where(row > col, 1.0, 0.0).astype(BF16)
    rank = _dot(before, sel.astype(BF16)) + carry_sc[...]
    info = info_ref[...]
    lane = lax.broadcasted_iota(jnp.int32, sel.shape, 1)
    e1 = info[:, INFO_E1:INFO_E1 + 1].astype(jnp.int32)
    e2 = info[:, INFO_E2:INFO_E2 + 1].astype(jnp.int32)
    r1 = jnp.sum(jnp.where(lane == e1, rank, 0.0), axis=-1, keepdims=True)
    r2 = jnp.sum(jnp.where(lane == e2, rank, 0.0), axis=-1, keepdims=True)
    rank_ref[...] = _lane_pack(lane, ((0, r1), (1, r2)))
    carry_sc[...] += jnp.sum(sel, axis=0, keepdims=True)

    @pl.when(i == pl.num_programs(0) - 1)
    def _():
        counts_ref[...] = carry_sc[...]


def _moe_plan_kernel(rank_ref, info_ref, counts_ref, dest_ref, tiles_ref, *, tile, n_experts):
    counts = counts_ref[...]
    ntile_e = jnp.floor((counts + (tile - 1)) * (1.0 / tile))
    padded = jnp.broadcast_to(ntile_e * tile, (8, LANES))
    r = lax.broadcasted_iota(jnp.int32, (LANES, LANES), 0)
    c = lax.broadcasted_iota(jnp.int32, (LANES, LANES), 1)
    lower = jnp.where(r < c, 1.0, 0.0).astype(BF16)
    p_hi, p_mid, p_lo = _split3(padded)
    offs = (_dot(p_hi, lower) + (_dot(p_mid, lower) + _dot(p_lo, lower)))[0:1, :]
    ends = offs + padded[0:1, :]

    info = info_ref[...]
    rank = rank_ref[...]
    lane = lax.broadcasted_iota(jnp.int32, info.shape, 1)
    e1 = info[:, INFO_E1:INFO_E1 + 1].astype(jnp.int32)
    e2 = info[:, INFO_E2:INFO_E2 + 1].astype(jnp.int32)
    d1 = jnp.sum(jnp.where(lane == e1, offs, 0.0), axis=-1, keepdims=True) + rank[:, 0:1]
    d2 = jnp.sum(jnp.where(lane == e2, offs, 0.0), axis=-1, keepdims=True) + rank[:, 1:2]
    dest_ref[...] = _lane_pack(lane, ((0, d1), (1, d2))).astype(jnp.int32)

    ends_col = jnp.broadcast_to(ends, (LANES, LANES)).T
    start = (c * tile).astype(F32)
    n_before = jnp.sum(jnp.where((ends_col <= start) & (r < n_experts), 1.0, 0.0), axis=0, keepdims=True)
    lane1 = lax.broadcasted_iota(jnp.int32, (1, LANES), 1)
    total = jnp.sum(jnp.where(lane1 == n_experts - 1, ends, 0.0), axis=-1, keepdims=True)
    tail = jnp.where(counts > 0.0, ends - tile, -1.0)
    tiles_ref[...] = jnp.zeros_like(tiles_ref)
    tiles_ref[0:1, :] = jnp.minimum(n_before, n_experts - 1.0).astype(jnp.int32)
    tiles_ref[1:2, :] = jnp.broadcast_to(total * (1.0 / tile), (1, LANES)).astype(jnp.int32)
    tiles_ref[2:3, :] = tail.astype(jnp.int32)


def _moe_dispatch_kernel(dest_ref, tails_ref, ntiles_ref, xn_ref, xs_ref, zero_sc, sems, *, tile, n_experts):
    i = pl.program_id(0)
    tm = xn_ref.shape[0]
    n_tiles_max = xs_ref.shape[0] // tile

    def fill_copy(j):
        return pltpu.make_async_copy(zero_sc, xs_ref.at[pl.ds(pl.multiple_of(j * tile, tile), tile)], sems.at[2])

    def tail_copy(e):
        return pltpu.make_async_copy(zero_sc, xs_ref.at[pl.ds(pl.multiple_of(tails_ref[e], tile), tile)],
                                     sems.at[1])

    @pl.when(i == 0)
    def _():
        zero_sc[...] = jnp.zeros_like(zero_sc)

        def fill(j, _):
            fill_copy(j).start()
            return 0

        def clear(e, _):
            @pl.when(tails_ref[e] >= 0)
            def _():
                tail_copy(e).start()
            return 0

        def clear_wait(e, _):
            @pl.when(tails_ref[e] >= 0)
            def _():
                tail_copy(e).wait()
            return 0

        lax.fori_loop(ntiles_ref[0], n_tiles_max, fill, 0)
        lax.fori_loop(0, n_experts, clear, 0)
        lax.fori_loop(0, n_experts, clear_wait, 0)

    def row_copy(r, k):
        return pltpu.make_async_copy(xn_ref.at[pl.ds(r, 1)],
                                     xs_ref.at[pl.ds(dest_ref[2 * (i * tm + r) + k], 1)], sems.at[0])

    for r in range(tm):
        row_copy(r, 0).start(priority=0)
        row_copy(r, 1).start(priority=1)
    for _ in range(2):
        pltpu.make_async_copy(xn_ref, xs_ref.at[pl.ds(0, tm)], sems.at[0]).wait()

    @pl.when(i == pl.num_programs(0) - 1)
    def _():
        def fill_wait(j, _):
            fill_copy(j).wait()
            return 0

        lax.fori_loop(ntiles_ref[0], n_tiles_max, fill_wait, 0)


def _moe_expert_kernel(texp_ref, ntiles_ref, xs_ref, wg_ref, wu_ref, wd_ref, ys_ref):
    @pl.when(pl.program_id(0) < ntiles_ref[0])
    def _():
        x = xs_ref[...].astype(BF16)
        hg = _dot(x, wg_ref[...].astype(BF16))
        hu = _dot(x, wu_ref[...].astype(BF16))
        ys_ref[...] = _dot((_silu(hg) * hu).astype(BF16), wd_ref[...].astype(BF16))


def _moe_combine_kernel(dest_ref, ys_ref, x_ref, info_ref, fw_ref, y_ref, buf_sc, sem):
    i = pl.program_id(0)
    tm = x_ref.shape[0]

    def row_copy(r, k):
        t = i * tm + r
        return pltpu.make_async_copy(ys_ref.at[pl.ds(dest_ref[2 * t + k], 1)], buf_sc.at[k, pl.ds(r, 1)], sem)

    for r in range(tm):
        row_copy(r, 0).start(priority=0)
        row_copy(r, 1).start(priority=1)
    for k in range(2):
        pltpu.make_async_copy(ys_ref.at[pl.ds(0, tm)], buf_sc.at[k], sem).wait()
    info = info_ref[...]
    y = x_ref[...] + info[:, INFO_G1:INFO_G1 + 1] * buf_sc[0] + info[:, INFO_G2:INFO_G2 + 1] * buf_sc[1]
    y_ref[...] = _rms(y, fw_ref[...])


MOE_TILE = 256
MOE_TOKENS_PER_STEP = 256


def moe_routed_final(x, p, final_w):
    m, d = x.shape
    ne, _, f = p['moe_w_gate'].shape
    tile = min(MOE_TILE, m)
    tm = min(m, 512)
    xn, sel, info = moe_router(x, p['norm_ffn_w'], p['router_coarse_w'], p['router_coarse_b'],
                               p['router_fine_w'], p['router_fine_b'], tm=tm)
    tok = pl.BlockSpec((tm, LANES), lambda i: (i, 0))
    rank, counts = pl.pallas_call(
        _moe_rank_kernel,
        grid=(m // tm,),
        in_specs=[tok, tok],
        out_specs=[tok, pl.BlockSpec((1, LANES), lambda i: (0, 0))],
        out_shape=[jax.ShapeDtypeStruct((m, LANES), F32), jax.ShapeDtypeStruct((1, LANES), F32)],
        scratch_shapes=[pltpu.VMEM((1, LANES), F32)],
        compiler_params=_cparams(("arbitrary",)),
        name="moe_rank",
    )(sel, info)
    n_tiles_max = (2 * m) // tile + ne
    assert n_tiles_max <= LANES
    dest, tiles = pl.pallas_call(
        functools.partial(_moe_plan_kernel, tile=tile, n_experts=ne),
        out_shape=[jax.ShapeDtypeStruct((m, LANES), jnp.int32), jax.ShapeDtypeStruct((8, LANES), jnp.int32)],
        compiler_params=pltpu.CompilerParams(vmem_limit_bytes=VMEM_LIMIT),
        name="moe_plan",
    )(rank, info, counts)
    dest_flat = dest[:, :2].reshape(2 * m)
    tile_expert = tiles[0, :n_tiles_max]
    n_tiles = tiles[1, :1]
    tails = tiles[2, :ne]

    rows = n_tiles_max * tile
    tc = min(m, MOE_TOKENS_PER_STEP)
    xs = pl.pallas_call(
        functools.partial(_moe_dispatch_kernel, tile=tile, n_experts=ne),
        grid_spec=pltpu.PrefetchScalarGridSpec(
            num_scalar_prefetch=3, grid=(m // tc,),
            in_specs=[pl.BlockSpec((tc, d), lambda i, *_: (i, 0))],
            out_specs=pl.BlockSpec(memory_space=pl.ANY),
            scratch_shapes=[pltpu.VMEM((tile, d), F32), pltpu.SemaphoreType.DMA((3,))]),
        out_shape=jax.ShapeDtypeStruct((rows, d), F32),
        compiler_params=_cparams(("arbitrary",)),
        name="moe_dispatch",
    )(dest_flat, tails, n_tiles, xn)

    def tile_idx(i, te, nt):
        return jnp.minimum(i, nt[0] - 1)

    ys = pl.pallas_call(
        _moe_expert_kernel,
        grid_spec=pltpu.PrefetchScalarGridSpec(
            num_scalar_prefetch=2, grid=(n_tiles_max,),
            in_specs=[pl.BlockSpec((tile, d), lambda i, te, nt: (tile_idx(i, te, nt), 0)),
                      pl.BlockSpec((None, d, f), lambda i, te, nt: (te[tile_idx(i, te, nt)], 0, 0)),
                      pl.BlockSpec((None, d, f), lambda i, te, nt: (te[tile_idx(i, te, nt)], 0, 0)),
                      pl.BlockSpec((None, f, d), lambda i, te, nt: (te[tile_idx(i, te, nt)], 0, 0))],
            out_specs=pl.BlockSpec((tile, d), lambda i, te, nt: (tile_idx(i, te, nt), 0))),
        out_shape=jax.ShapeDtypeStruct((rows, d), F32),
        input_output_aliases={2: 0},
        compiler_params=_cparams(("arbitrary",)),
        name="moe_experts",
    )(tile_expert, n_tiles, xs, p['moe_w_gate'], p['moe_w_up'], p['moe_w_down'])

    return pl.pallas_call(
        _moe_combine_kernel,
        grid_spec=pltpu.PrefetchScalarGridSpec(
            num_scalar_prefetch=1, grid=(m // tc,),
            in_specs=[pl.BlockSpec(memory_space=pl.ANY),
                      pl.BlockSpec((tc, d), lambda i, dref: (i, 0)),
                      pl.BlockSpec((tc, LANES), lambda i, dref: (i, 0)),
                      pl.BlockSpec((1, d), lambda i, dref: (0, 0))],
            out_specs=pl.BlockSpec((tc, d), lambda i, dref: (i, 0)),
            scratch_shapes=[pltpu.VMEM((2, tc, d), F32), pltpu.SemaphoreType.DMA(())]),
        out_shape=jax.ShapeDtypeStruct((m, d), F32),
        compiler_params=_cparams(("arbitrary",)),
        name="moe_combine",
    )(dest_flat, ys, x, info, final_w.reshape(1, d))


def s5_prompt(h, nb, seq, prep, p):
    y, hfin = s5_scan(h, prep, nb, seq)
    out = s5_head(y, h, p['s5_d'], p['s5_glu_w'], p['s5_glu_b'], p['s5_norm_w'], tm=min(512, nb * seq))
    return out, hfin[:, 0].transpose(1, 0, 2), hfin[:, 1].transpose(1, 0, 2)


def s5_sample(h, st_re, st_im, prep, p):
    g = prep[0].shape[0]
    ch = S5_CH
    ds = g * ch
    b = h.shape[0]
    u_t = h[:, :ds].reshape(b, g, ch).transpose(1, 0, 2)
    y_t, n_re, n_im = s5_step(u_t, st_re.transpose(1, 0, 2), st_im.transpose(1, 0, 2), prep,
                              p['s5_c_re'], p['s5_c_im'])
    y = y_t.transpose(1, 0, 2).reshape(b, ds)
    out = s5_head(y, h, p['s5_d'], p['s5_glu_w'], p['s5_glu_b'], p['s5_norm_w'], tm=b)
    return out, n_re.transpose(1, 0, 2), n_im.transpose(1, 0, 2)


def _row_tile(m):
    return min(m, 1024)


def _decoder_layer(x, p, s5_prep, final_w, *, nb, seq, mem_kv, xa_heads, states):
    m, d = x.shape
    g, n = p['s5_a_re'].shape
    d_s5 = g * S5_CH
    nheads = p['m2_a_log'].shape[0]
    d_inner = nheads * M2_HEADDIM
    conv_dim = d_inner + 2 * M2_NGROUPS * M2_DSTATE
    n_main = d_s5 + d_inner + conv_dim
    tm = _row_tile(m)

    w_in = p['w_in']
    w_dt = jnp.pad(w_in[:, n_main:], ((0, 0), (0, LANES - nheads)))
    h, dt_raw = fused_matmul([x], w_in, n_out=n_main, gain=p['norm_mix_w'], side_w=w_dt, tm=tm, tn=512)

    if states is None:
        s5_out, s5_re, s5_im = s5_prompt(h, nb, seq, s5_prep, p)
        m2_out, ssm, conv = ssd_prompt(h, dt_raw, nb, seq, p, d_s5=d_s5, d_inner=d_inner, nheads=nheads)
    else:
        s5_out, s5_re, s5_im = s5_sample(h, states[0], states[1], s5_prep, p)
        m2_out, ssm, conv = ssd_sample(h, dt_raw, states[2], states[3], p, d_s5=d_s5, d_inner=d_inner,
                                       nheads=nheads)
    x1 = fused_matmul([s5_out, m2_out], p['w_out'], n_out=d, res=x, tm=tm, tn=512)

    q = fused_matmul([x1], p['xa_wq'], n_out=d, gain=p['norm_xa_w'], tm=tm, tn=512,
                     out_dtype=BF16 if states is None else F32)
    if states is None:
        mem = mem_kv[0].shape[0] // nb
        o = attention_prompt(q, mem_kv[0], mem_kv[1], nb, seq, mem, xa_heads, tq=min(seq, 512))
    else:
        o = attention_sample(q, mem_kv[0], mem_kv[1])
    x2 = fused_matmul([o], p['xa_wo'], n_out=d, res=x1, tm=tm, tn=512)

    y = moe_routed_final(x2, p, final_w)
    return y, s5_re, s5_im, ssm, conv


def kernel(x_prompt, x_sample, mem_prompt, state_s5_re, state_s5_im, state_ssm, state_conv, cache_mem_k, cache_mem_v, norm_mix_w, w_in, s5_a_re, s5_a_im, s5_log_dt, s5_b_re, s5_b_im, s5_c_re, s5_c_im, s5_d, s5_glu_w, s5_glu_b, s5_norm_w, m2_conv_w, m2_conv_b, m2_dt_bias, m2_a_log, m2_d, m2_norm_w, w_out, norm_xa_w, norm_mem_w, xa_wq, xa_wk, xa_wv, xa_wo, norm_ffn_w, router_coarse_w, router_coarse_b, router_fine_w, router_fine_b, moe_w_gate, moe_w_up, moe_w_down, norm_final_w):
    depth = w_in.shape[0]
    assert depth == 1, "the final norm is fused into the (only) layer"
    per_layer = dict(
        norm_mix_w=norm_mix_w, w_in=w_in, s5_a_re=s5_a_re, s5_a_im=s5_a_im, s5_log_dt=s5_log_dt,
        s5_b_re=s5_b_re, s5_b_im=s5_b_im, s5_c_re=s5_c_re, s5_c_im=s5_c_im, s5_d=s5_d, s5_glu_w=s5_glu_w,
        s5_glu_b=s5_glu_b, s5_norm_w=s5_norm_w, m2_conv_w=m2_conv_w, m2_conv_b=m2_conv_b, m2_dt_bias=m2_dt_bias,
        m2_a_log=m2_a_log, m2_d=m2_d, m2_norm_w=m2_norm_w, w_out=w_out, norm_xa_w=norm_xa_w,
        norm_mem_w=norm_mem_w, xa_wq=xa_wq, xa_wk=xa_wk, xa_wv=xa_wv, xa_wo=xa_wo, norm_ffn_w=norm_ffn_w,
        router_coarse_w=router_coarse_w, router_coarse_b=router_coarse_b, router_fine_w=router_fine_w,
        router_fine_b=router_fine_b, moe_w_gate=moe_w_gate, moe_w_up=moe_w_up, moe_w_down=moe_w_down)
    p = {k: v[0] for k, v in per_layer.items()}
    for name in ('w_in', 'w_out', 'xa_wq', 'xa_wk', 'xa_wv', 'xa_wo', 's5_glu_w'):
        p[name] = p[name].astype(BF16)
    nb, seq, d = x_prompt.shape
    db, dseq, _ = x_sample.shape
    assert dseq == 1
    mem = mem_prompt.shape[1]
    xa_heads = cache_mem_k.shape[3]

    s5_prep = s5_prepare(p['s5_a_re'], p['s5_a_im'], p['s5_log_dt'], p['s5_b_re'], p['s5_b_im'],
                         p['s5_c_re'], p['s5_c_im'])

    memx = mem_prompt.reshape(nb * mem, d)
    mk = fused_matmul([memx], p['xa_wk'], n_out=d, gain=p['norm_mem_w'], tm=_row_tile(nb * mem), tn=512)
    mv = fused_matmul([memx], p['xa_wv'], n_out=d, gain=p['norm_mem_w'], tm=_row_tile(nb * mem), tn=512)
    yp, p_re, p_im, p_ssm, p_conv = _decoder_layer(
        x_prompt.reshape(nb * seq, d), p, s5_prep, norm_final_w, nb=nb, seq=seq, mem_kv=(mk, mv), xa_heads=xa_heads, states=None)

    ys, s_re, s_im, s_ssm, s_conv = _decoder_layer(
        x_sample.reshape(db, d), p, s5_prep, norm_final_w, nb=db, seq=1,
        mem_kv=(cache_mem_k[0], cache_mem_v[0]), xa_heads=xa_heads,
        states=(state_s5_re[0], state_s5_im[0], state_ssm[0], state_conv[0]))

    kv_shape = (1, nb, mem) + cache_mem_k.shape[3:]
    return (yp.reshape(nb, seq, d), ys.reshape(db, 1, d), p_re[None], p_im[None], p_ssm[None], p_conv[None],
            mk.reshape(kv_shape), mv.reshape(kv_shape), s_re[None], s_im[None], s_ssm[None], s_conv[None])
```

```python
import functools
import math

import jax
import jax.numpy as jnp
from jax import lax
from jax.experimental import pallas as pl
from jax.experimental.pallas import tpu as pltpu

F32 = jnp.float32
BF16 = jnp.bfloat16
RMS_EPS = 1e-6

V7X_VMEM_BYTES = 64 * 1024 * 1024
VMEM_LIMIT = V7X_VMEM_BYTES - 8 * 1024 * 1024
LANES = 128

S5_CH = 16
S5_N = 64
S5_Q = 16
S5_GB = 8
M2_HEADDIM = 64
M2_DSTATE = 128
M2_NGROUPS = 2
M2_CONV = 4
M2_CHUNK = 128
MOE_GROUPS = 4
MOE_PER_GROUP = 8


def _cparams(sem):
    return pltpu.CompilerParams(dimension_semantics=sem, vmem_limit_bytes=VMEM_LIMIT)


def _rms(x, w):
    return x * lax.rsqrt(jnp.mean(x * x, axis=-1, keepdims=True) + RMS_EPS) * w


def _sigmoid(x):
    return 1.0 / (1.0 + jnp.exp(-x))


def _silu(x):
    return x * _sigmoid(x)


def _softplus(x):
    return jnp.maximum(x, 0.0) + jnp.log1p(jnp.exp(-jnp.abs(x)))


def _gelu_tanh(x):
    return 0.5 * x * (1.0 + jnp.tanh(math.sqrt(2.0 / math.pi) * (x + 0.044715 * (x * x * x))))


def _dot(a, b):
    return jnp.dot(a, b, preferred_element_type=F32)


def _dot_nt(a, b):
    return lax.dot_general(a, b, (((1,), (1,)), ((), ())), preferred_element_type=F32)


def _split3(x):
    hi = x.astype(BF16)
    r = x - hi.astype(F32)
    mid = r.astype(BF16)
    lo = (r - mid.astype(F32)).astype(BF16)
    return hi, mid, lo


def _split2(x):
    hi = x.astype(BF16)
    lo = (x - hi.astype(F32)).astype(BF16)
    return hi, lo


def _mm_kernel(*refs, n_lhs, has_gain, has_res, has_side, staged):
    it = iter(refs)
    lhs = [next(it) for _ in range(n_lhs)]
    gain = next(it) if has_gain else None
    ws = [next(it) for _ in range(n_lhs)]
    side_w = next(it) if has_side else None
    res = next(it) if has_res else None
    out = next(it)
    side_out = next(it) if has_side else None
    lhs_bf = next(it) if staged else lhs

    if staged:
        @pl.when(pl.program_id(1) == 0)
        def _():
            for i in range(n_lhs):
                x = lhs[i][...]
                if has_gain:
                    x = _rms(x, gain[...])
                lhs_bf[i] = x.astype(BF16)
            if has_side:
                side_out[...] = _dot(lhs_bf[0], side_w[...].astype(BF16))

    acc = None
    for i in range(n_lhs):
        p = _dot(lhs_bf[i][...], ws[i][...].astype(BF16))
        acc = p if acc is None else acc + p
    if has_res:
        acc = acc + res[...]
    out[...] = acc.astype(out.dtype)


def fused_matmul(lhs_list, w, *, n_out, gain=None, res=None, side_w=None, out_dtype=F32, tm, tn):
    n_lhs = len(lhs_list)
    m, kp = lhs_list[0].shape
    assert all(a.shape == (m, kp) for a in lhs_list)
    assert w.shape[0] == n_lhs * kp and m % tm == 0 and n_out % tn == 0
    assert gain is None or n_lhs == 1
    staged = gain is not None or any(a.dtype != BF16 for a in lhs_list)
    assert staged or side_w is None
    grid = (m // tm, n_out // tn)
    in_specs = [pl.BlockSpec((tm, kp), lambda i, j: (i, 0)) for _ in range(n_lhs)]
    args = list(lhs_list)
    if gain is not None:
        in_specs.append(pl.BlockSpec((1, kp), lambda i, j: (0, 0)))
        args.append(gain.reshape(1, kp))
    for p in range(n_lhs):
        in_specs.append(pl.BlockSpec((kp, tn), lambda i, j, p=p: (p, j)))
        args.append(w)
    if side_w is not None:
        in_specs.append(pl.BlockSpec((kp, LANES), lambda i, j: (0, 0)))
        args.append(side_w)
    if res is not None:
        in_specs.append(pl.BlockSpec((tm, tn), lambda i, j: (i, j)))
        args.append(res)
    out_shape = [jax.ShapeDtypeStruct((m, n_out), out_dtype)]
    out_specs = [pl.BlockSpec((tm, tn), lambda i, j: (i, j))]
    if side_w is not None:
        out_shape.append(jax.ShapeDtypeStruct((m, LANES), F32))
        out_specs.append(pl.BlockSpec((tm, LANES), lambda i, j: (i, 0)))
    outs = pl.pallas_call(
        functools.partial(_mm_kernel, n_lhs=n_lhs, has_gain=gain is not None,
                          has_res=res is not None, has_side=side_w is not None, staged=staged),
        grid=grid, in_specs=in_specs, out_specs=out_specs, out_shape=out_shape,
        scratch_shapes=[pltpu.VMEM((n_lhs, tm, kp), BF16)] if staged else [],
        compiler_params=_cparams(("parallel", "arbitrary")),
        name="fused_matmul",
    )(*args)
    return outs if side_w is not None else outs[0]


def _s5_prep_kernel(*refs):
    for g in range(refs[0].shape[0]):
        _s5_prep_group(*[r.at[g] for r in refs])


def _s5_prep_group(lre_ref, lim_ref, ldt_ref, btre_ref, btim_ref, cre_ref, cim_ref,
                   tz_ref, wsre_ref, wsim_ref, wcre_ref, wcim_ref, aq_ref, ab_ref, bbt_ref):
    q, ch = S5_Q, S5_CH
    lr = lre_ref[...]
    li = lim_ref[...]
    step = jnp.exp(ldt_ref[...])
    mag = jnp.exp(lr * step)
    ab_re = mag * jnp.cos(li * step)
    ab_im = mag * jnp.sin(li * step)
    den = lr * lr + li * li
    num_re = ab_re - 1.0
    coef_re = (num_re * lr + ab_im * li) / den
    coef_im = (ab_im * lr - num_re * li) / den
    bt_re = btre_ref[...]
    bt_im = btim_ref[...]
    bb_re = coef_re * bt_re - coef_im * bt_im
    bb_im = coef_re * bt_im + coef_im * bt_re
    c_re = cre_ref[...]
    c_im = cim_ref[...]

    pw = [(jnp.ones_like(ab_re), jnp.zeros_like(ab_re))]
    for _ in range(q):
        pr, pi = pw[-1]
        pw.append((pr * ab_re - pi * ab_im, pr * ab_im + pi * ab_re))

    ca_re = [c_re * pr - c_im * pi for pr, pi in pw]
    ca_im = [c_re * pi + c_im * pr for pr, pi in pw]
    wcre_ref[...] = jnp.concatenate(ca_re[1:], axis=0).astype(BF16)
    wcim_ref[...] = jnp.concatenate([-x for x in ca_im[1:]], axis=0).astype(BF16)

    pr_stack = jnp.concatenate(ca_re[:q], axis=0)
    pi_stack = jnp.concatenate(ca_im[:q], axis=0)
    krow = None
    for a, b, sign in ((bb_re, pr_stack, 1.0), (bb_im, pi_stack, -1.0)):
        a_hi, a_lo = _split2(a)
        b_hi, b_lo = _split2(b)
        t = _dot_nt(a_hi, b_hi) + (_dot_nt(a_hi, b_lo) + _dot_nt(a_lo, b_hi))
        krow = sign * t if krow is None else krow + sign * t
    lane = lax.broadcasted_iota(jnp.int32, krow.shape, 1)
    blocks = [krow]
    for s in range(1, q):
        blocks.append(jnp.where(lane >= s * ch, pltpu.roll(krow, s * ch, 1), 0.0))
    tz_ref[...] = jnp.concatenate(blocks, axis=0).astype(BF16)

    ws_re, ws_im = [], []
    for s in range(q):
        pr, pi = pw[q - 1 - s]
        ws_re.append(bb_re * pr - bb_im * pi)
        ws_im.append(bb_re * pi + bb_im * pr)
    wsre_ref[...] = jnp.concatenate(ws_re, axis=0).astype(BF16)
    wsim_ref[...] = jnp.concatenate(ws_im, axis=0).astype(BF16)

    aq_ref[0:1, :] = pw[q][0]
    aq_ref[1:2, :] = pw[q][1]
    ab_ref[0:1, :] = ab_re
    ab_ref[1:2, :] = ab_im
    bbt_ref[0:ch, :] = bb_re
    bbt_ref[ch:2 * ch, :] = bb_im


def s5_prepare(a_re, a_im, log_dt, b_re, b_im, c_re, c_im):
    g, n = a_re.shape
    ch, q = S5_CH, S5_Q
    qc = q * ch
    bt_re = jnp.swapaxes(b_re, 1, 2)
    bt_im = jnp.swapaxes(b_im, 1, 2)

    def per_g(*dims):
        return pl.BlockSpec((S5_GB,) + dims, lambda i: (i,) + (0,) * len(dims))

    return pl.pallas_call(
        _s5_prep_kernel,
        grid=(g // S5_GB,),
        in_specs=[per_g(1, n), per_g(1, n), per_g(1, 1), per_g(ch, n), per_g(ch, n), per_g(ch, n), per_g(ch, n)],
        out_specs=[per_g(qc, qc), per_g(qc, n), per_g(qc, n), per_g(qc, n), per_g(qc, n),
                   per_g(2, n), per_g(2, n), per_g(2 * ch, n)],
        out_shape=[jax.ShapeDtypeStruct((g, qc, qc), BF16),
                   jax.ShapeDtypeStruct((g, qc, n), BF16), jax.ShapeDtypeStruct((g, qc, n), BF16),
                   jax.ShapeDtypeStruct((g, qc, n), BF16), jax.ShapeDtypeStruct((g, qc, n), BF16),
                   jax.ShapeDtypeStruct((g, 2, n), F32), jax.ShapeDtypeStruct((g, 2, n), F32),
                   jax.ShapeDtypeStruct((g, 2 * ch, n), F32)],
        compiler_params=_cparams(("parallel",)),
        name="s5_prepare",
    )(a_re.reshape(g, 1, n), a_im.reshape(g, 1, n), log_dt.reshape(g, 1, 1), bt_re, bt_im, c_re, c_im)


def _s5_scan_kernel(h_ref, tz_ref, wsre_ref, wsim_ref, wcre_ref, wcim_ref, aq_ref,
                    y_ref, hfin_ref, xs_sc, u_sc, yg_sc, sre_sc, sim_sc, *, nb, nchunk):
    gb, q, ch = S5_GB, S5_Q, S5_CH
    rows = nb * nchunk
    per_vreg = LANES // ch
    assert gb == per_vreg and q % per_vreg == 0
    slot = lax.broadcasted_iota(jnp.int32, (rows, LANES), 1) // ch

    halves = q // per_vreg

    def rot_rows(w, g):
        if g == 0:
            return w
        cut = (per_vreg - g) * ch
        parts = []
        for hf in range(halves):
            blk = w[hf * LANES:(hf + 1) * LANES]
            parts += [blk[cut:], blk[:cut]]
        return jnp.concatenate(parts, axis=0)

    for s in range(q):
        x = h_ref[pl.ds(s, rows, stride=q), :].astype(BF16)
        k = s % per_vreg
        xs_sc[s] = pltpu.roll(x, k * ch, 1) if k else x
    keep = [jnp.where(slot == j, 1.0, 0.0).astype(BF16) for j in range(per_vreg)]
    for g in range(gb):
        for hf in range(halves):
            acc = None
            for k in range(per_vreg):
                piece = xs_sc[hf * per_vreg + k] * keep[(g + k) % per_vreg]
                acc = piece if acc is None else acc + piece
            u_sc[g, :, hf * LANES:(hf + 1) * LANES] = acc

    for g in range(gb):
        u = u_sc[g]
        tz = rot_rows(tz_ref[g], g)
        tz = jnp.concatenate([pltpu.roll(tz[:, hf * LANES:(hf + 1) * LANES], g * ch, 1) if g
                              else tz[:, hf * LANES:(hf + 1) * LANES] for hf in range(halves)], axis=1)
        yg_sc[g] = _dot(u, tz)
        sre_sc[g] = _dot(u, rot_rows(wsre_ref[g], g))
        sim_sc[g] = _dot(u, rot_rows(wsim_ref[g], g))

    ar = [jnp.broadcast_to(aq_ref[g, 0:1, :], (nb, S5_N)) for g in range(gb)]
    ai = [jnp.broadcast_to(aq_ref[g, 1:2, :], (nb, S5_N)) for g in range(gb)]

    def step(c, carry):
        at = pl.ds(c, nb, stride=nchunk)
        new = []
        for g in range(gb):
            hr, hi = carry[g]
            sr = sre_sc[g, at, :]
            si = sim_sc[g, at, :]
            sre_sc[g, at, :] = hr
            sim_sc[g, at, :] = hi
            new.append((ar[g] * hr - ai[g] * hi + sr, ar[g] * hi + ai[g] * hr + si))
        return tuple(new)

    zero = jnp.zeros((nb, S5_N), F32)
    fin = lax.fori_loop(0, nchunk, step, tuple((zero, zero) for _ in range(gb)))
    for g in range(gb):
        hfin_ref[g, 0] = fin[g][0]
        hfin_ref[g, 1] = fin[g][1]
        yg_sc[g] += (_dot_nt(sre_sc[g].astype(BF16), rot_rows(wcre_ref[g], g))
                     + _dot_nt(sim_sc[g].astype(BF16), rot_rows(wcim_ref[g], g)))

    for t in range(q):
        hf, k = divmod(t, per_vreg)
        acc = jnp.zeros((rows, LANES), F32)
        for g in range(gb):
            acc = jnp.where(slot == (k + g) % per_vreg, yg_sc[g, :, hf * LANES:(hf + 1) * LANES], acc)
        y_ref[pl.ds(t, rows, stride=q), :] = pltpu.roll(acc, (per_vreg - k) * ch, 1) if k else acc


def s5_scan(h, prep, nb, seq):
    tz, ws_re, ws_im, wc_re, wc_im, aq = prep[:6]
    g, qc, _ = tz.shape
    n = S5_N
    gb = S5_GB
    q = S5_Q
    assert gb * S5_CH == LANES and qc == q * S5_CH and seq % q == 0
    nchunk = seq // q
    rows = nb * nchunk
    m = nb * seq

    def blk(*dims):
        return pl.BlockSpec((gb,) + dims, lambda i: (i,) + (0,) * len(dims))

    return pl.pallas_call(
        functools.partial(_s5_scan_kernel, nb=nb, nchunk=nchunk),
        grid=(g // gb,),
        in_specs=[pl.BlockSpec((m, LANES), lambda i: (0, i)),
                  blk(qc, qc), blk(qc, n), blk(qc, n), blk(qc, n), blk(qc, n), blk(2, n)],
        out_specs=[pl.BlockSpec((m, LANES), lambda i: (0, i)), blk(2, nb, n)],
        out_shape=[jax.ShapeDtypeStruct((m, g * S5_CH), F32), jax.ShapeDtypeStruct((g, 2, nb, n), F32)],
        scratch_shapes=[pltpu.VMEM((q, rows, LANES), BF16), pltpu.VMEM((gb, rows, qc), BF16),
                        pltpu.VMEM((gb, rows, qc), F32),
                        pltpu.VMEM((gb, rows, n), F32), pltpu.VMEM((gb, rows, n), F32)],
        compiler_params=_cparams(("parallel",)),
        name="s5_scan",
    )(h, tz, ws_re, ws_im, wc_re, wc_im, aq)


def _s5_step_kernel(u_ref, hre_ref, him_ref, ab_ref, bbt_ref, cre_ref, cim_ref, y_ref, ore_ref, oim_ref):
    gb = u_ref.shape[0]
    ch = S5_CH
    for g in range(gb):
        u = u_ref[g].astype(BF16)
        bb_re = bbt_ref[g, 0:ch, :].astype(BF16)
        bb_im = bbt_ref[g, ch:2 * ch, :].astype(BF16)
        ar = ab_ref[g, 0:1, :]
        ai = ab_ref[g, 1:2, :]
        hr0 = hre_ref[g]
        hi0 = him_ref[g]
        hr = _dot(u, bb_re) + (ar * hr0 - ai * hi0)
        hi = _dot(u, bb_im) + (ar * hi0 + ai * hr0)
        ore_ref[g] = hr
        oim_ref[g] = hi
        y_ref[g] = (_dot_nt(hr.astype(BF16), cre_ref[g].astype(BF16))
                    - _dot_nt(hi.astype(BF16), cim_ref[g].astype(BF16)))


def s5_step(u_t, h_re, h_im, prep, c_re, c_im):
    ab, bbt = prep[6], prep[7]
    g, b, ch = u_t.shape
    n = S5_N
    gb = S5_GB

    def blk(*dims):
        return pl.BlockSpec((gb,) + dims, lambda i: (i,) + (0,) * len(dims))

    return pl.pallas_call(
        _s5_step_kernel,
        grid=(g // gb,),
        in_specs=[blk(b, ch), blk(b, n), blk(b, n), blk(2, n), blk(2 * ch, n), blk(ch, n), blk(ch, n)],
        out_specs=[blk(b, ch), blk(b, n), blk(b, n)],
        out_shape=[jax.ShapeDtypeStruct((g, b, ch), F32), jax.ShapeDtypeStruct((g, b, n), F32),
                   jax.ShapeDtypeStruct((g, b, n), F32)],
        compiler_params=_cparams(("parallel",)),
        name="s5_step",
    )(u_t, h_re, h_im, ab, bbt, c_re, c_im)


def _s5_head_kernel(y_ref, u_ref, d_ref, w_ref, b_ref, nw_ref, o_ref):
    y = y_ref[...] + d_ref[...] * u_ref[...]
    g = _gelu_tanh(y)
    gate = _sigmoid(_dot(g.astype(BF16), w_ref[...].astype(BF16)) + b_ref[...])
    o_ref[...] = _rms(g * gate, nw_ref[...]).astype(o_ref.dtype)


def s5_head(y, h, d, glu_w, glu_b, norm_w, *, tm):
    m, ds = y.shape
    row = lambda a: a.reshape(1, ds)
    vec = pl.BlockSpec((1, ds), lambda i: (0, 0))
    return pl.pallas_call(
        _s5_head_kernel,
        grid=(m // tm,),
        in_specs=[pl.BlockSpec((tm, ds), lambda i: (i, 0)), pl.BlockSpec((tm, ds), lambda i: (i, 0)), vec,
                  pl.BlockSpec((ds, ds), lambda i: (0, 0)), vec, vec],
        out_specs=pl.BlockSpec((tm, ds), lambda i: (i, 0)),
        out_shape=jax.ShapeDtypeStruct((m, ds), BF16),
        compiler_params=_cparams(("parallel",)),
        name="s5_head",
    )(y, h, row(d), glu_w, row(glu_b), row(norm_w))


def _pair_select(first, col0, col1, shape):
    return jnp.where(first, jnp.broadcast_to(col0, shape), jnp.broadcast_to(col1, shape))


def _ssd_chunk_kernel(*refs, d_inner, nheads, n_xparts):
    xparts = refs[:n_xparts]
    (z_ref, dt_ref, cw_ref, cb_ref, dtb_ref, alog_ref, dvec_ref, nw_ref,
     out_ref, ssm_ref, conv_ref, state_sc, xpad_sc, y_sc) = refs[n_xparts:]
    c = pl.program_id(1)
    q = M2_CHUNK
    hp = M2_HEADDIM
    ns = M2_DSTATE
    heads_per_group = nheads // M2_NGROUPS
    halo = 8

    @pl.when(c == 0)
    def _():
        state_sc[...] = jnp.zeros_like(state_sc)
        xpad_sc[0:halo, :] = jnp.zeros((halo, xpad_sc.shape[1]), F32)

    wpart = xparts[0].shape[1]
    for i, xr in enumerate(xparts):
        xpad_sc[halo:halo + q, i * wpart:(i + 1) * wpart] = xr[...]
    cw = cw_ref[...]
    conv = cb_ref[...] + cw[M2_CONV - 1:M2_CONV, :] * xpad_sc[halo:halo + q, :]
    for k in range(1, M2_CONV):
        conv = conv + cw[M2_CONV - 1 - k:M2_CONV - k, :] * xpad_sc[halo - k:halo - k + q, :]
    xpad_sc[0:halo, :] = xpad_sc[q:q + halo, :]
    xc = _silu(conv)

    dt = _softplus(dt_ref[...] + dtb_ref[...])
    a = -jnp.exp(alog_ref[...])
    da = dt * a
    row = lax.broadcasted_iota(jnp.int32, (q, q), 0)
    col = lax.broadcasted_iota(jnp.int32, (q, q), 1)
    causal = row >= col
    tri = jnp.where(causal, 1.0, 0.0).astype(BF16)
    d_hi, d_mid, d_lo = _split3(da)
    acum = _dot(tri, d_hi) + (_dot(tri, d_mid) + _dot(tri, d_lo))
    acum_t = acum.T
    alast = acum[q - 1:q, :]
    first = col < hp
    first_rows = row < hp

    for pr in range(nheads // 2):
        grp = (2 * pr) // heads_per_group
        b_bf = xc[:, d_inner + grp * ns:d_inner + (grp + 1) * ns].astype(BF16)
        c_bf = xc[:, d_inner + (M2_NGROUPS + grp) * ns:d_inner + (M2_NGROUPS + grp + 1) * ns].astype(BF16)
        cb = _dot_nt(c_bf, b_bf)
        xpair = xc[:, pr * 2 * hp:(pr + 1) * 2 * hp]
        h0, h1 = 2 * pr, 2 * pr + 1
        acol = [acum[:, h:h + 1] for h in (h0, h1)]
        m = []
        for k, h in enumerate((h0, h1)):
            seg = jnp.broadcast_to(acol[k], (q, q)) - jnp.broadcast_to(acum_t[h:h + 1, :], (q, q))
            lmat = jnp.exp(jnp.where(causal, seg, -1e30))
            m.append((cb * lmat).astype(BF16))
        dtp = _pair_select(first, dt[:, h0:h0 + 1], dt[:, h1:h1 + 1], (q, q))
        xdt = xpair * dtp
        xdt_bf = xdt.astype(BF16)
        y_diag = jnp.where(first, _dot(m[0], xdt_bf), _dot(m[1], xdt_bf))
        dec_end = _pair_select(first, jnp.exp(alast[:, h0:h0 + 1] - acol[0]),
                               jnp.exp(alast[:, h1:h1 + 1] - acol[1]), (q, q))
        xw_t = (xdt * dec_end).T.astype(BF16)
        chunk_state = _dot(xw_t, b_bf)
        rows = pl.ds(pr * 2 * hp, 2 * hp)
        prev = state_sc[rows, :]
        y_off = _dot_nt(c_bf, prev.astype(BF16)) * _pair_select(first, jnp.exp(acol[0]), jnp.exp(acol[1]), (q, q))
        sdec = jnp.where(first_rows, jnp.broadcast_to(jnp.exp(alast[:, h0:h0 + 1]), (q, q)),
                         jnp.broadcast_to(jnp.exp(alast[:, h1:h1 + 1]), (q, q)))
        state_sc[rows, :] = prev * sdec + chunk_state
        y_sc[:, pr * 2 * hp:(pr + 1) * 2 * hp] = y_diag + y_off + dvec_ref[:, pr * 2 * hp:(pr + 1) * 2 * hp] * xpair

    out_ref[...] = _rms(y_sc[...] * _silu(z_ref[...]), nw_ref[...]).astype(out_ref.dtype)

    @pl.when(c == pl.num_programs(1) - 1)
    def _():
        ssm_ref[...] = state_sc[...]
        conv_ref[...] = xpad_sc[halo + q - (M2_CONV - 1):halo + q, :]


def ssd_prompt(h, dt_raw, nb, seq, p, *, d_s5, d_inner, nheads):
    q = M2_CHUNK
    nc = seq // q
    conv_dim = d_inner + 2 * M2_NGROUPS * M2_DSTATE
    xw = 512
    xoff = d_s5 + d_inner
    assert d_s5 % d_inner == 0 and xoff % xw == 0 and conv_dim % xw == 0
    assert M2_CHUNK == 2 * M2_HEADDIM == M2_DSTATE == LANES
    zblk = d_s5 // d_inner
    n_xparts = conv_dim // xw
    m = nb * seq
    pad = lambda v: jnp.pad(v, (0, LANES - v.shape[0])).reshape(1, LANES)
    dvec = jnp.repeat(p['m2_d'], M2_HEADDIM).reshape(1, d_inner)
    vec = lambda n: pl.BlockSpec((1, n), lambda b, c: (0, 0))
    tok = lambda w, j: pl.BlockSpec((q, w), lambda b, c, j=j: (b * nc + c, j))
    out, ssm, conv = pl.pallas_call(
        functools.partial(_ssd_chunk_kernel, d_inner=d_inner, nheads=nheads, n_xparts=n_xparts),
        grid=(nb, nc),
        in_specs=[tok(xw, xoff // xw + i) for i in range(n_xparts)] + [tok(d_inner, zblk), tok(LANES, 0),
                  pl.BlockSpec((M2_CONV, conv_dim), lambda b, c: (0, 0)), vec(conv_dim), vec(LANES), vec(LANES),
                  vec(d_inner), vec(d_inner)],
        out_specs=[tok(d_inner, 0),
                   pl.BlockSpec((None, nheads * M2_HEADDIM, M2_DSTATE), lambda b, c: (b, 0, 0)),
                   pl.BlockSpec((None, M2_CONV - 1, conv_dim), lambda b, c: (b, 0, 0))],
        out_shape=[jax.ShapeDtypeStruct((m, d_inner), BF16),
                   jax.ShapeDtypeStruct((nb, nheads * M2_HEADDIM, M2_DSTATE), F32),
                   jax.ShapeDtypeStruct((nb, M2_CONV - 1, conv_dim), F32)],
        scratch_shapes=[pltpu.VMEM((nheads * M2_HEADDIM, M2_DSTATE), F32),
                        pltpu.VMEM((q + 8, conv_dim), F32),
                        pltpu.VMEM((q, d_inner), F32)],
        compiler_params=_cparams(("parallel", "arbitrary")),
        name="ssd_chunk",
    )(*([h] * n_xparts), h, dt_raw, p['m2_conv_w'], p['m2_conv_b'].reshape(1, conv_dim), pad(p['m2_dt_bias']),
      pad(p['m2_a_log']), dvec, p['m2_norm_w'].reshape(1, d_inner))
    return out, ssm.reshape(nb, nheads, M2_HEADDIM, M2_DSTATE), conv


SSD_STEP_SEQS = 4


def _ssd_step_kernel(*refs, d_inner, nheads, n_xparts):
    xparts = refs[:n_xparts]
    (z_ref, dt_ref, cs0_ref, cs1_ref, cs2_ref, cw_ref, cb_ref, dtb_ref, alog_ref, dvec_ref, nw_ref, st_ref,
     out_ref, so_ref, lhs_sc, bfull_sc, ct_sc, yt_sc, xs_sc) = refs[n_xparts:]
    i = pl.program_id(0)
    nb = z_ref.shape[0]
    ns = M2_DSTATE
    rows_g = (nheads // M2_NGROUPS) * M2_HEADDIM

    @pl.when(i == 0)
    def _():
        cw = cw_ref[...]
        xbc = jnp.concatenate([xr[...] for xr in xparts], axis=1)
        conv = (cb_ref[...] + cw[3:4, :] * xbc + cw[2:3, :] * cs2_ref[...]
                + cw[1:2, :] * cs1_ref[...] + cw[0:1, :] * cs0_ref[...])
        xc = _silu(conv)
        dt = _softplus(dt_ref[...] + dtb_ref[...])
        dec = jnp.exp(dt * (-jnp.exp(alog_ref[...])))
        hrow = lax.broadcasted_iota(jnp.int32, (LANES, d_inner), 0)
        hcol = lax.broadcasted_iota(jnp.int32, (LANES, d_inner), 1)
        expand = jnp.where(hcol // M2_HEADDIM == hrow, 1.0, 0.0).astype(BF16)

        def expand_heads(v):
            a, b_, c = _split3(v)
            return _dot(a, expand) + (_dot(b_, expand) + _dot(c, expand))

        xs = xc[:, :d_inner]
        xs_sc[...] = xs
        xdt_t = (xs * expand_heads(dt)).T
        d_hi, d_mid, d_lo = _split3(expand_heads(dec).T)
        for g in range(M2_NGROUPS):
            r = slice(g * rows_g, (g + 1) * rows_g)
            lhs_sc[g] = jnp.concatenate([xdt_t[r].astype(BF16), d_hi[r], d_mid[r], d_lo[r]], axis=1)
            b_g = xc[:, d_inner + g * ns:d_inner + (g + 1) * ns]
            bfull_sc[g] = jnp.concatenate([b_g, jnp.zeros_like(b_g)], axis=1)
            c_g = xc[:, d_inner + (M2_NGROUPS + g) * ns:d_inner + (M2_NGROUPS + g + 1) * ns]
            ct_sc[g] = c_g.T
        yt_sc[...] = jnp.zeros_like(yt_sc)

    row_id = lax.broadcasted_iota(jnp.int32, (nb, 2 * ns), 0)
    lane_id = lax.broadcasted_iota(jnp.int32, (nb, 2 * ns), 1)
    col_id = lax.broadcasted_iota(jnp.int32, (ns, nb), 1)
    for j in range(st_ref.shape[0]):
        b = i * st_ref.shape[0] + j
        r_bot = jnp.where((row_id == b) & (lane_id >= ns), 1.0, 0.0).astype(BF16)
        for g in range(M2_NGROUPS):
            r = pl.ds(g * rows_g, rows_g)
            r_top = jnp.where(row_id == b, bfull_sc[g], 0.0).astype(BF16)
            rhs = jnp.concatenate([r_top, r_bot, r_bot, r_bot], axis=0)
            o = _dot(lhs_sc[g], rhs)
            hnew = st_ref[j, r, :] * o[:, ns:] + o[:, :ns]
            so_ref[j, r, :] = hnew
            cm = jnp.where(col_id == b, ct_sc[g], 0.0).astype(BF16)
            yt_sc[r, :] += _dot(hnew.astype(BF16), cm)

    @pl.when(i == pl.num_programs(0) - 1)
    def _():
        y = yt_sc[...].T + dvec_ref[...] * xs_sc[...]
        out_ref[...] = _rms(y * _silu(z_ref[...]), nw_ref[...]).astype(out_ref.dtype)


def ssd_sample(h, dt_raw, state, conv_state, p, *, d_s5, d_inner, nheads):
    nb = h.shape[0]
    conv_dim = d_inner + 2 * M2_NGROUPS * M2_DSTATE
    xw = 512
    xoff = d_s5 + d_inner
    assert nb == LANES and M2_DSTATE == LANES and M2_CONV == 4
    assert xoff % xw == 0 and conv_dim % xw == 0 and d_s5 % d_inner == 0 and nb % SSD_STEP_SEQS == 0
    n_xparts = conv_dim // xw
    rows = nheads * M2_HEADDIM
    rows_g = rows // M2_NGROUPS
    pad = lambda v: jnp.pad(v, (0, LANES - v.shape[0])).reshape(1, LANES)
    dvec = jnp.repeat(p['m2_d'], M2_HEADDIM).reshape(1, d_inner)
    full = lambda a, b, j=0: pl.BlockSpec((a, b), lambda i, j=j: (0, j))
    st_spec = pl.BlockSpec((SSD_STEP_SEQS, rows, M2_DSTATE), lambda i: (i, 0, 0))
    out, new_state = pl.pallas_call(
        functools.partial(_ssd_step_kernel, d_inner=d_inner, nheads=nheads, n_xparts=n_xparts),
        grid=(nb // SSD_STEP_SEQS,),
        in_specs=[full(nb, xw, xoff // xw + k) for k in range(n_xparts)]
        + [full(nb, d_inner, d_s5 // d_inner), full(nb, LANES)]
        + [full(nb, conv_dim)] * 3
        + [full(M2_CONV, conv_dim), full(1, conv_dim), full(1, LANES), full(1, LANES), full(1, d_inner),
           full(1, d_inner), st_spec],
        out_specs=[full(nb, d_inner), st_spec],
        out_shape=[jax.ShapeDtypeStruct((nb, d_inner), BF16), jax.ShapeDtypeStruct((nb, rows, M2_DSTATE), F32)],
        scratch_shapes=[pltpu.VMEM((M2_NGROUPS, rows_g, 4 * nb), BF16),
                        pltpu.VMEM((M2_NGROUPS, nb, 2 * M2_DSTATE), F32),
                        pltpu.VMEM((M2_NGROUPS, M2_DSTATE, nb), F32),
                        pltpu.VMEM((rows, nb), F32),
                        pltpu.VMEM((nb, d_inner), F32)],
        compiler_params=_cparams(("arbitrary",)),
        name="ssd_step",
    )(*([h] * n_xparts), h, dt_raw, conv_state[:, 0], conv_state[:, 1], conv_state[:, 2],
      p['m2_conv_w'], p['m2_conv_b'].reshape(1, conv_dim), pad(p['m2_dt_bias']), pad(p['m2_a_log']),
      dvec, p['m2_norm_w'].reshape(1, d_inner), state.reshape(nb, rows, M2_DSTATE))
    xbc = lax.slice_in_dim(h, xoff, xoff + conv_dim, axis=1)
    new_conv = jnp.concatenate([conv_state[:, 1:], xbc[:, None, :]], axis=1)
    return out, new_state.reshape(state.shape), new_conv


def _softmax_rows(s):
    e = jnp.exp(s - jnp.max(s, axis=-1, keepdims=True))
    return e / jnp.sum(e, axis=-1, keepdims=True)


def _attn_kernel(q_ref, k_ref, v_ref, o_ref, *, scale):
    s = _dot_nt(q_ref[...].astype(BF16), k_ref[...].astype(BF16)) * scale
    o_ref[...] = _dot(_softmax_rows(s).astype(BF16), v_ref[...].astype(BF16)).astype(o_ref.dtype)


def attention_prompt(q, k, v, nb, seq, mem, heads, *, tq):
    d = q.shape[1]
    hd = d // heads
    nq = seq // tq
    kv_spec = pl.BlockSpec((mem, hd), lambda b, h, i: (b, h))
    q_spec = pl.BlockSpec((tq, hd), lambda b, h, i: (b * nq + i, h))
    return pl.pallas_call(
        functools.partial(_attn_kernel, scale=hd ** -0.5),
        grid=(nb, heads, nq),
        in_specs=[q_spec, kv_spec, kv_spec],
        out_specs=q_spec,
        out_shape=jax.ShapeDtypeStruct(q.shape, BF16),
        compiler_params=_cparams(("parallel", "parallel", "parallel")),
        name="attention_prompt",
    )(q, k, v)


ATTN_STEP_SEQS = 4


ATTN_STEP_ROWS = 64


def _attn_step_kernel(q_ref, k_ref, v_ref, o_ref, s_sc, *, scale):
    nseq, mem, heads, hd = k_ref.shape
    ch = ATTN_STEP_ROWS
    pack = 8 // heads
    for j in range(nseq):
        q = jnp.concatenate([q_ref[j]] * pack, axis=0)

        def score(c, mx):
            rows = pl.ds(pl.multiple_of(c * ch, ch), ch)
            k = k_ref[j, rows].reshape(ch // pack, pack * heads, hd)
            s = jnp.sum(k * q, axis=-1, keepdims=True) * scale
            s_sc[pl.ds(pl.multiple_of(c * (ch // pack), ch // pack), ch // pack)] = s
            return jnp.maximum(mx, jnp.max(s, axis=0))

        mx = lax.fori_loop(0, mem // ch, score, jnp.full((pack * heads, 1), NEG, F32))
        mx1 = mx[0:heads]
        for i in range(1, pack):
            mx1 = jnp.maximum(mx1, mx[i * heads:(i + 1) * heads])
        mx = jnp.concatenate([mx1] * pack, axis=0)

        def accum(c, carry):
            den, acc = carry
            rows = pl.ds(pl.multiple_of(c * ch, ch), ch)
            v = v_ref[j, rows].reshape(ch // pack, pack * heads, hd)
            e = jnp.exp(s_sc[pl.ds(pl.multiple_of(c * (ch // pack), ch // pack), ch // pack)] - mx)
            return den + jnp.sum(e, axis=0), acc + jnp.sum(e * v, axis=0)

        den, acc = lax.fori_loop(0, mem // ch, accum,
                                 (jnp.zeros((pack * heads, 1), F32), jnp.zeros((pack * heads, hd), F32)))
        den1, acc1 = den[0:heads], acc[0:heads]
        for i in range(1, pack):
            den1 = den1 + den[i * heads:(i + 1) * heads]
            acc1 = acc1 + acc[i * heads:(i + 1) * heads]
        o_ref[j] = acc1 / den1


def attention_sample(q, k_cache, v_cache):
    b, mem, heads, hd = k_cache.shape
    nseq = ATTN_STEP_SEQS
    q_spec = pl.BlockSpec((nseq, heads, hd), lambda i: (i, 0, 0))
    kv_spec = pl.BlockSpec((nseq, mem, heads, hd), lambda i: (i, 0, 0, 0))
    out = pl.pallas_call(
        functools.partial(_attn_step_kernel, scale=hd ** -0.5),
        grid=(b // nseq,),
        in_specs=[q_spec, kv_spec, kv_spec],
        out_specs=q_spec,
        out_shape=jax.ShapeDtypeStruct((b, heads, hd), F32),
        scratch_shapes=[pltpu.VMEM((mem * heads // 8, 8, 1), F32)],
        compiler_params=_cparams(("parallel",)),
        name="attention_step",
    )(q.reshape(b, heads, hd), k_cache, v_cache)
    return out.reshape(b, heads * hd)


NEG = -1e30


INFO_G1, INFO_G2, INFO_E1, INFO_E2 = 0, 1, 2, 3


def _lane_pack(lane, values):
    out = 0.0
    for k, v in values:
        out = jnp.where(lane == k, v, out)
    return out


def _router_kernel(x_ref, nw_ref, wr_ref, br_ref, xn_ref, sel_ref, info_ref, *, n_experts):
    xn = _rms(x_ref[...], nw_ref[...])
    xn_ref[...] = xn
    x_hi, x_lo = _split2(xn)
    w_hi, w_lo = _split2(wr_ref[...])
    logits = _dot(x_hi, w_hi) + (_dot(x_hi, w_lo) + _dot(x_lo, w_hi)) + br_ref[...]
    lane = lax.broadcasted_iota(jnp.int32, logits.shape, 1)
    big = jnp.int32(2 ** 30)
    is_c = (lane >= n_experts) & (lane < n_experts + MOE_GROUPS)
    lc = jnp.where(is_c, logits, NEG)
    cmax = jnp.max(lc, axis=-1, keepdims=True)
    gsel = jnp.min(jnp.where(lc == cmax, lane, big), axis=-1, keepdims=True) - n_experts
    gate_c = 1.0 / jnp.sum(jnp.where(is_c, jnp.exp(lc - cmax), 0.0), axis=-1, keepdims=True)
    in_group = (lane < n_experts) & (lane // MOE_PER_GROUP == gsel)
    lf = jnp.where(in_group, logits, NEG)
    t1 = jnp.max(lf, axis=-1, keepdims=True)
    i1 = jnp.min(jnp.where(lf == t1, lane, big), axis=-1, keepdims=True)
    lf2 = jnp.where(lane == i1, NEG, lf)
    t2 = jnp.max(lf2, axis=-1, keepdims=True)
    i2 = jnp.min(jnp.where(lf2 == t2, lane, big), axis=-1, keepdims=True)
    r = jnp.exp(t2 - t1)
    g1 = gate_c / (1.0 + r)
    g2 = gate_c * r / (1.0 + r)
    sel_ref[...] = jnp.where((lane == i1) | (lane == i2), 1.0, 0.0)
    info_ref[...] = _lane_pack(lane, ((INFO_G1, g1), (INFO_G2, g2), (INFO_E1, i1.astype(F32)),
                                      (INFO_E2, i2.astype(F32))))


def moe_router(x, norm_w, w_coarse, b_coarse, w_fine, b_fine, *, tm):
    m, d = x.shape
    e = w_fine.shape[1]
    padw = LANES - e - MOE_GROUPS
    wr = jnp.concatenate([w_fine, w_coarse, jnp.zeros((d, padw), F32)], axis=1)
    br = jnp.concatenate([b_fine, b_coarse, jnp.zeros((padw,), F32)]).reshape(1, LANES)
    tok = pl.BlockSpec((tm, LANES), lambda i: (i, 0))
    return pl.pallas_call(
        functools.partial(_router_kernel, n_experts=e),
        grid=(m // tm,),
        in_specs=[pl.BlockSpec((tm, d), lambda i: (i, 0)), pl.BlockSpec((1, d), lambda i: (0, 0)),
                  pl.BlockSpec((d, LANES), lambda i: (0, 0)), pl.BlockSpec((1, LANES), lambda i: (0, 0))],
        out_specs=[pl.BlockSpec((tm, d), lambda i: (i, 0)), tok, tok],
        out_shape=[jax.ShapeDtypeStruct((m, d), F32), jax.ShapeDtypeStruct((m, LANES), F32),
                   jax.ShapeDtypeStruct((m, LANES), F32)],
        compiler_params=_cparams(("parallel",)),
        name="moe_router",
    )(x, norm_w.reshape(1, d), wr, br)


def _moe_rank_kernel(sel_ref, info_ref, rank_ref, counts_ref, carry_sc):
    i = pl.program_id(0)

    @pl.when(i == 0)
    def _():
        carry_sc[...] = jnp.zeros_like(carry_sc)

    sel = sel_ref[...]
    tm = sel.shape[0]
    row = lax.broadcasted_iota(jnp.int32, (tm, tm), 0)
    col = lax.broadcasted_iota(jnp.int32, (tm, tm), 1)
    before = jnp.where(row > col, 1.0, 0.0).astype(BF16)
    rank = _dot(before, sel.astype(BF16)) + carry_sc[...]
    info = info_ref[...]
    lane = lax.broadcasted_iota(jnp.int32, sel.shape, 1)
    e1 = info[:, INFO_E1:INFO_E1 + 1].astype(jnp.int32)
    e2 = info[:, INFO_E2:INFO_E2 + 1].astype(jnp.int32)
    r1 = jnp.sum(jnp.where(lane == e1, rank, 0.0), axis=-1, keepdims=True)
    r2 = jnp.sum(jnp.where(lane == e2, rank, 0.0), axis=-1, keepdims=True)
    rank_ref[...] = _lane_pack(lane, ((0, r1), (1, r2)))
    carry_sc[...] += jnp.sum(sel, axis=0, keepdims=True)

    @pl.when(i == pl.num_programs(0) - 1)
    def _():
        counts_ref[...] = carry_sc[...]


def _moe_plan_kernel(rank_ref, info_ref, counts_ref, dest_ref, tiles_ref, *, tile, n_experts):
    counts = counts_ref[...]
    ntile_e = jnp.floor((counts + (tile - 1)) * (1.0 / tile))
    padded = jnp.broadcast_to(ntile_e * tile, (8, LANES))
    r = lax.broadcasted_iota(jnp.int32, (LANES, LANES), 0)
    c = lax.broadcasted_iota(jnp.int32, (LANES, LANES), 1)
    lower = jnp.where(r < c, 1.0, 0.0).astype(BF16)
    p_hi, p_mid, p_lo = _split3(padded)
    offs = (_dot(p_hi, lower) + (_dot(p_mid, lower) + _dot(p_lo, lower)))[0:1, :]
    ends = offs + padded[0:1, :]

    info = info_ref[...]
    rank = rank_ref[...]
    lane = lax.broadcasted_iota(jnp.int32, info.shape, 1)
    e1 = info[:, INFO_E1:INFO_E1 + 1].astype(jnp.int32)
    e2 = info[:, INFO_E2:INFO_E2 + 1].astype(jnp.int32)
    d1 = jnp.sum(jnp.where(lane == e1, offs, 0.0), axis=-1, keepdims=True) + rank[:, 0:1]
    d2 = jnp.sum(jnp.where(lane == e2, offs, 0.0), axis=-1, keepdims=True) + rank[:, 1:2]
    dest_ref[...] = _lane_pack(lane, ((0, d1), (1, d2))).astype(jnp.int32)

    ends_col = jnp.broadcast_to(ends, (LANES, LANES)).T
    start = (c * tile).astype(F32)
    n_before = jnp.sum(jnp.where((ends_col <= start) & (r < n_experts), 1.0, 0.0), axis=0, keepdims=True)
    lane1 = lax.broadcasted_iota(jnp.int32, (1, LANES), 1)
    total = jnp.sum(jnp.where(lane1 == n_experts - 1, ends, 0.0), axis=-1, keepdims=True)
    tail = jnp.where(counts > 0.0, ends - tile, -1.0)
    tiles_ref[...] = jnp.zeros_like(tiles_ref)
    tiles_ref[0:1, :] = jnp.minimum(n_before, n_experts - 1.0).astype(jnp.int32)
    tiles_ref[1:2, :] = jnp.broadcast_to(total * (1.0 / tile), (1, LANES)).astype(jnp.int32)
    tiles_ref[2:3, :] = tail.astype(jnp.int32)


def _moe_dispatch_kernel(dest_ref, tails_ref, ntiles_ref, xn_ref, xs_ref, zero_sc, sems, *, tile, n_experts):
    i = pl.program_id(0)
    tm = xn_ref.shape[0]
    n_tiles_max = xs_ref.shape[0] // tile

    def fill_copy(j):
        return pltpu.make_async_copy(zero_sc, xs_ref.at[pl.ds(pl.multiple_of(j * tile, tile), tile)], sems.at[2])

    def tail_copy(e):
        return pltpu.make_async_copy(zero_sc, xs_ref.at[pl.ds(pl.multiple_of(tails_ref[e], tile), tile)],
                                     sems.at[1])

    @pl.when(i == 0)
    def _():
        zero_sc[...] = jnp.zeros_like(zero_sc)

        def fill(j, _):
            fill_copy(j).start()
            return 0

        def clear(e, _):
            @pl.when(tails_ref[e] >= 0)
            def _():
                tail_copy(e).start()
            return 0

        def clear_wait(e, _):
            @pl.when(tails_ref[e] >= 0)
            def _():
                tail_copy(e).wait()
            return 0

        lax.fori_loop(ntiles_ref[0], n_tiles_max, fill, 0)
        lax.fori_loop(0, n_experts, clear, 0)
        lax.fori_loop(0, n_experts, clear_wait, 0)

    def row_copy(r, k):
        return pltpu.make_async_copy(xn_ref.at[pl.ds(r, 1)],
                                     xs_ref.at[pl.ds(dest_ref[2 * (i * tm + r) + k], 1)], sems.at[0])

    for r in range(tm):
        row_copy(r, 0).start(priority=0)
        row_copy(r, 1).start(priority=1)
    for _ in range(2):
        pltpu.make_async_copy(xn_ref, xs_ref.at[pl.ds(0, tm)], sems.at[0]).wait()

    @pl.when(i == pl.num_programs(0) - 1)
    def _():
        def fill_wait(j, _):
            fill_copy(j).wait()
            return 0

        lax.fori_loop(ntiles_ref[0], n_tiles_max, fill_wait, 0)


def _moe_expert_kernel(texp_ref, ntiles_ref, xs_ref, wg_ref, wu_ref, wd_ref, ys_ref):
    @pl.when(pl.program_id(0) < ntiles_ref[0])
    def _():
        x = xs_ref[...].astype(BF16)
        hg = _dot(x, wg_ref[...].astype(BF16))
        hu = _dot(x, wu_ref[...].astype(BF16))
        ys_ref[...] = _dot((_silu(hg) * hu).astype(BF16), wd_ref[...].astype(BF16))


def _moe_combine_kernel(dest_ref, ys_ref, x_ref, info_ref, fw_ref, y_ref, buf_sc, sems):
    s = pl.program_id(0)
    n_blocks = pl.num_programs(0) - 1
    tm = x_ref.shape[0]
    slot = s % 2

    @pl.when(s < n_blocks)
    def _():
        for r in range(tm):
            for k in range(2):
                pltpu.make_async_copy(ys_ref.at[pl.ds(dest_ref[2 * (s * tm + r) + k], 1)],
                                      buf_sc.at[slot, k, pl.ds(r, 1)], sems.at[slot]).start(priority=k)

    @pl.when(s > 0)
    def _():
        prev = 1 - slot
        for k in range(2):
            pltpu.make_async_copy(ys_ref.at[pl.ds(0, tm)], buf_sc.at[prev, k], sems.at[prev]).wait()
        info = info_ref[...]
        y = (x_ref[...] + info[:, INFO_G1:INFO_G1 + 1] * buf_sc[prev, 0]
             + info[:, INFO_G2:INFO_G2 + 1] * buf_sc[prev, 1])
        y_ref[...] = _rms(y, fw_ref[...])


MOE_TILE = 256
MOE_TOKENS_PER_STEP = 256


def moe_routed_final(x, p, final_w):
    m, d = x.shape
    ne, _, f = p['moe_w_gate'].shape
    tile = min(MOE_TILE, m)
    tm = min(m, 512)
    xn, sel, info = moe_router(x, p['norm_ffn_w'], p['router_coarse_w'], p['router_coarse_b'],
                               p['router_fine_w'], p['router_fine_b'], tm=tm)
    tok = pl.BlockSpec((tm, LANES), lambda i: (i, 0))
    rank, counts = pl.pallas_call(
        _moe_rank_kernel,
        grid=(m // tm,),
        in_specs=[tok, tok],
        out_specs=[tok, pl.BlockSpec((1, LANES), lambda i: (0, 0))],
        out_shape=[jax.ShapeDtypeStruct((m, LANES), F32), jax.ShapeDtypeStruct((1, LANES), F32)],
        scratch_shapes=[pltpu.VMEM((1, LANES), F32)],
        compiler_params=_cparams(("arbitrary",)),
        name="moe_rank",
    )(sel, info)
    n_tiles_max = (2 * m) // tile + ne
    assert n_tiles_max <= LANES
    dest, tiles = pl.pallas_call(
        functools.partial(_moe_plan_kernel, tile=tile, n_experts=ne),
        out_shape=[jax.ShapeDtypeStruct((m, LANES), jnp.int32), jax.ShapeDtypeStruct((8, LANES), jnp.int32)],
        compiler_params=pltpu.CompilerParams(vmem_limit_bytes=VMEM_LIMIT),
        name="moe_plan",
    )(rank, info, counts)
    dest_flat = dest[:, :2].reshape(2 * m)
    tile_expert = tiles[0, :n_tiles_max]
    n_tiles = tiles[1, :1]
    tails = tiles[2, :ne]

    rows = n_tiles_max * tile
    tc = min(m, MOE_TOKENS_PER_STEP)
    td = min(m, 2 * MOE_TOKENS_PER_STEP)
    xs = pl.pallas_call(
        functools.partial(_moe_dispatch_kernel, tile=tile, n_experts=ne),
        grid_spec=pltpu.PrefetchScalarGridSpec(
            num_scalar_prefetch=3, grid=(m // td,),
            in_specs=[pl.BlockSpec((td, d), lambda i, *_: (i, 0))],
            out_specs=pl.BlockSpec(memory_space=pl.ANY),
            scratch_shapes=[pltpu.VMEM((tile, d), F32), pltpu.SemaphoreType.DMA((3,))]),
        out_shape=jax.ShapeDtypeStruct((rows, d), F32),
        compiler_params=_cparams(("arbitrary",)),
        name="moe_dispatch",
    )(dest_flat, tails, n_tiles, xn)

    def tile_idx(i, te, nt):
        return jnp.minimum(i, nt[0] - 1)

    ys = pl.pallas_call(
        _moe_expert_kernel,
        grid_spec=pltpu.PrefetchScalarGridSpec(
            num_scalar_prefetch=2, grid=(n_tiles_max,),
            in_specs=[pl.BlockSpec((tile, d), lambda i, te, nt: (tile_idx(i, te, nt), 0)),
                      pl.BlockSpec((None, d, f), lambda i, te, nt: (te[tile_idx(i, te, nt)], 0, 0)),
                      pl.BlockSpec((None, d, f), lambda i, te, nt: (te[tile_idx(i, te, nt)], 0, 0)),
                      pl.BlockSpec((None, f, d), lambda i, te, nt: (te[tile_idx(i, te, nt)], 0, 0))],
            out_specs=pl.BlockSpec((tile, d), lambda i, te, nt: (tile_idx(i, te, nt), 0))),
        out_shape=jax.ShapeDtypeStruct((rows, d), F32),
        input_output_aliases={2: 0},
        compiler_params=_cparams(("arbitrary",)),
        name="moe_experts",
    )(tile_expert, n_tiles, xs, p['moe_w_gate'], p['moe_w_up'], p['moe_w_down'])

    return pl.pallas_call(
        _moe_combine_kernel,
        grid_spec=pltpu.PrefetchScalarGridSpec(
            num_scalar_prefetch=1, grid=(m // tc + 1,),
            in_specs=[pl.BlockSpec(memory_space=pl.ANY),
                      pl.BlockSpec((tc, d), lambda s, dref: (jnp.maximum(s - 1, 0), 0)),
                      pl.BlockSpec((tc, LANES), lambda s, dref: (jnp.maximum(s - 1, 0), 0)),
                      pl.BlockSpec((1, d), lambda s, dref: (0, 0))],
            out_specs=pl.BlockSpec((tc, d), lambda s, dref: (jnp.maximum(s - 1, 0), 0)),
            scratch_shapes=[pltpu.VMEM((2, 2, tc, d), F32), pltpu.SemaphoreType.DMA((2,))]),
        out_shape=jax.ShapeDtypeStruct((m, d), F32),
        compiler_params=_cparams(("arbitrary",)),
        name="moe_combine",
    )(dest_flat, ys, x, info, final_w.reshape(1, d))


def s5_prompt(h, nb, seq, prep, p):
    y, hfin = s5_scan(h, prep, nb, seq)
    out = s5_head(y, h, p['s5_d'], p['s5_glu_w'], p['s5_glu_b'], p['s5_norm_w'], tm=min(512, nb * seq))
    return out, hfin[:, 0].transpose(1, 0, 2), hfin[:, 1].transpose(1, 0, 2)


def s5_sample(h, st_re, st_im, prep, p):
    g = prep[0].shape[0]
    ch = S5_CH
    ds = g * ch
    b = h.shape[0]
    u_t = h[:, :ds].reshape(b, g, ch).transpose(1, 0, 2)
    y_t, n_re, n_im = s5_step(u_t, st_re.transpose(1, 0, 2), st_im.transpose(1, 0, 2), prep,
                              p['s5_c_re'], p['s5_c_im'])
    y = y_t.transpose(1, 0, 2).reshape(b, ds)
    out = s5_head(y, h, p['s5_d'], p['s5_glu_w'], p['s5_glu_b'], p['s5_norm_w'], tm=b)
    return out, n_re.transpose(1, 0, 2), n_im.transpose(1, 0, 2)


def _row_tile(m):
    return min(m, 1024)


def _col_tile(m, n):
    return n if m <= 256 else 512


def _decoder_layer(x, p, s5_prep, final_w, *, nb, seq, mem_kv, xa_heads, states):
    m, d = x.shape
    g, n = p['s5_a_re'].shape
    d_s5 = g * S5_CH
    nheads = p['m2_a_log'].shape[0]
    d_inner = nheads * M2_HEADDIM
    conv_dim = d_inner + 2 * M2_NGROUPS * M2_DSTATE
    n_main = d_s5 + d_inner + conv_dim
    tm = _row_tile(m)

    w_in = p['w_in']
    w_dt = jnp.pad(w_in[:, n_main:], ((0, 0), (0, LANES - nheads)))
    tn = _col_tile(m, d)
    h, dt_raw = fused_matmul([x], w_in, n_out=n_main, gain=p['norm_mix_w'], side_w=w_dt, tm=tm,
                             tn=_col_tile(m, n_main))

    if states is None:
        s5_out, s5_re, s5_im = s5_prompt(h, nb, seq, s5_prep, p)
        m2_out, ssm, conv = ssd_prompt(h, dt_raw, nb, seq, p, d_s5=d_s5, d_inner=d_inner, nheads=nheads)
    else:
        s5_out, s5_re, s5_im = s5_sample(h, states[0], states[1], s5_prep, p)
        m2_out, ssm, conv = ssd_sample(h, dt_raw, states[2], states[3], p, d_s5=d_s5, d_inner=d_inner,
                                       nheads=nheads)
    x1 = fused_matmul([s5_out, m2_out], p['w_out'], n_out=d, res=x, tm=tm, tn=tn)

    q = fused_matmul([x1], p['xa_wq'], n_out=d, gain=p['norm_xa_w'], tm=tm, tn=tn,
                     out_dtype=BF16 if states is None else F32)
    if states is None:
        mem = mem_kv[0].shape[0] // nb
        o = attention_prompt(q, mem_kv[0], mem_kv[1], nb, seq, mem, xa_heads, tq=min(seq, 2048))
    else:
        o = attention_sample(q, mem_kv[0], mem_kv[1])
    x2 = fused_matmul([o], p['xa_wo'], n_out=d, res=x1, tm=tm, tn=tn)

    y = moe_routed_final(x2, p, final_w)
    return y, s5_re, s5_im, ssm, conv


def kernel(x_prompt, x_sample, mem_prompt, state_s5_re, state_s5_im, state_ssm, state_conv, cache_mem_k, cache_mem_v, norm_mix_w, w_in, s5_a_re, s5_a_im, s5_log_dt, s5_b_re, s5_b_im, s5_c_re, s5_c_im, s5_d, s5_glu_w, s5_glu_b, s5_norm_w, m2_conv_w, m2_conv_b, m2_dt_bias, m2_a_log, m2_d, m2_norm_w, w_out, norm_xa_w, norm_mem_w, xa_wq, xa_wk, xa_wv, xa_wo, norm_ffn_w, router_coarse_w, router_coarse_b, router_fine_w, router_fine_b, moe_w_gate, moe_w_up, moe_w_down, norm_final_w):
    depth = w_in.shape[0]
    assert depth == 1, "the final norm is fused into the (only) layer"
    per_layer = dict(
        norm_mix_w=norm_mix_w, w_in=w_in, s5_a_re=s5_a_re, s5_a_im=s5_a_im, s5_log_dt=s5_log_dt,
        s5_b_re=s5_b_re, s5_b_im=s5_b_im, s5_c_re=s5_c_re, s5_c_im=s5_c_im, s5_d=s5_d, s5_glu_w=s5_glu_w,
        s5_glu_b=s5_glu_b, s5_norm_w=s5_norm_w, m2_conv_w=m2_conv_w, m2_conv_b=m2_conv_b, m2_dt_bias=m2_dt_bias,
        m2_a_log=m2_a_log, m2_d=m2_d, m2_norm_w=m2_norm_w, w_out=w_out, norm_xa_w=norm_xa_w,
        norm_mem_w=norm_mem_w, xa_wq=xa_wq, xa_wk=xa_wk, xa_wv=xa_wv, xa_wo=xa_wo, norm_ffn_w=norm_ffn_w,
        router_coarse_w=router_coarse_w, router_coarse_b=router_coarse_b, router_fine_w=router_fine_w,
        router_fine_b=router_fine_b, moe_w_gate=moe_w_gate, moe_w_up=moe_w_up, moe_w_down=moe_w_down)
    p = {k: v[0] for k, v in per_layer.items()}
    for name in ('w_in', 'w_out', 'xa_wq', 'xa_wk', 'xa_wv', 'xa_wo', 's5_glu_w'):
        p[name] = p[name].astype(BF16)
    nb, seq, d = x_prompt.shape
    db, dseq, _ = x_sample.shape
    assert dseq == 1
    mem = mem_prompt.shape[1]
    xa_heads = cache_mem_k.shape[3]

    s5_prep = s5_prepare(p['s5_a_re'], p['s5_a_im'], p['s5_log_dt'], p['s5_b_re'], p['s5_b_im'],
                         p['s5_c_re'], p['s5_c_im'])

    memx = mem_prompt.reshape(nb * mem, d)
    mk = fused_matmul([memx], p['xa_wk'], n_out=d, gain=p['norm_mem_w'], tm=_row_tile(nb * mem), tn=512)
    mv = fused_matmul([memx], p['xa_wv'], n_out=d, gain=p['norm_mem_w'], tm=_row_tile(nb * mem), tn=512)
    yp, p_re, p_im, p_ssm, p_conv = _decoder_layer(
        x_prompt.reshape(nb * seq, d), p, s5_prep, norm_final_w, nb=nb, seq=seq, mem_kv=(mk, mv), xa_heads=xa_heads, states=None)

    ys, s_re, s_im, s_ssm, s_conv = _decoder_layer(
        x_sample.reshape(db, d), p, s5_prep, norm_final_w, nb=db, seq=1,
        mem_kv=(cache_mem_k[0], cache_mem_v[0]), xa_heads=xa_heads,
        states=(state_s5_re[0], state_s5_im[0], state_ssm[0], state_conv[0]))

    kv_shape = (1, nb, mem) + cache_mem_k.shape[3:]
    return (yp.reshape(nb, seq, d), ys.reshape(db, 1, d), p_re[None], p_im[None], p_ssm[None], p_conv[None],
            mk.reshape(kv_shape), mv.reshape(kv_shape), s_re[None], s_im[None], s_ssm[None], s_conv[None])
```

```python
import functools
import math

import jax
import jax.numpy as jnp
from jax import lax
from jax.experimental import pallas as pl
from jax.experimental.pallas import tpu as pltpu

F32 = jnp.float32
BF16 = jnp.bfloat16
RMS_EPS = 1e-6

V7X_VMEM_BYTES = 64 * 1024 * 1024
VMEM_LIMIT = V7X_VMEM_BYTES - 8 * 1024 * 1024
LANES = 128

S5_CH = 16
S5_N = 64
S5_Q = 16
S5_GB = 8
M2_HEADDIM = 64
M2_DSTATE = 128
M2_NGROUPS = 2
M2_CONV = 4
M2_CHUNK = 128
MOE_GROUPS = 4
MOE_PER_GROUP = 8


def _cparams(sem):
    return pltpu.CompilerParams(dimension_semantics=sem, vmem_limit_bytes=VMEM_LIMIT)


def _rms(x, w):
    return x * lax.rsqrt(jnp.mean(x * x, axis=-1, keepdims=True) + RMS_EPS) * w


def _sigmoid(x):
    return 1.0 / (1.0 + jnp.exp(-x))


def _silu(x):
    return x * _sigmoid(x)


def _softplus(x):
    return jnp.maximum(x, 0.0) + jnp.log1p(jnp.exp(-jnp.abs(x)))


def _gelu_tanh(x):
    return 0.5 * x * (1.0 + jnp.tanh(math.sqrt(2.0 / math.pi) * (x + 0.044715 * (x * x * x))))


def _dot(a, b):
    return jnp.dot(a, b, preferred_element_type=F32)


def _dot_nt(a, b):
    return lax.dot_general(a, b, (((1,), (1,)), ((), ())), preferred_element_type=F32)


def _split3(x):
    hi = x.astype(BF16)
    r = x - hi.astype(F32)
    mid = r.astype(BF16)
    lo = (r - mid.astype(F32)).astype(BF16)
    return hi, mid, lo


def _split2(x):
    hi = x.astype(BF16)
    lo = (x - hi.astype(F32)).astype(BF16)
    return hi, lo


def _mm_kernel(*refs, n_lhs, has_gain, has_res, has_side, staged):
    it = iter(refs)
    lhs = [next(it) for _ in range(n_lhs)]
    gain = next(it) if has_gain else None
    ws = [next(it) for _ in range(n_lhs)]
    side_w = next(it) if has_side else None
    res = next(it) if has_res else None
    out = next(it)
    side_out = next(it) if has_side else None
    lhs_bf = next(it) if staged else lhs

    if staged:
        @pl.when(pl.program_id(1) == 0)
        def _():
            for i in range(n_lhs):
                x = lhs[i][...]
                if has_gain:
                    x = _rms(x, gain[...])
                lhs_bf[i] = x.astype(BF16)
            if has_side:
                side_out[...] = _dot(lhs_bf[0], side_w[...].astype(BF16))

    acc = None
    for i in range(n_lhs):
        p = _dot(lhs_bf[i][...], ws[i][...].astype(BF16))
        acc = p if acc is None else acc + p
    if has_res:
        acc = acc + res[...]
    out[...] = acc.astype(out.dtype)


def fused_matmul(lhs_list, w, *, n_out, gain=None, res=None, side_w=None, out_dtype=F32, tm, tn):
    n_lhs = len(lhs_list)
    m, kp = lhs_list[0].shape
    assert all(a.shape == (m, kp) for a in lhs_list)
    assert w.shape[0] == n_lhs * kp and m % tm == 0 and n_out % tn == 0
    assert gain is None or n_lhs == 1
    staged = gain is not None or any(a.dtype != BF16 for a in lhs_list)
    assert staged or side_w is None
    grid = (m // tm, n_out // tn)
    in_specs = [pl.BlockSpec((tm, kp), lambda i, j: (i, 0)) for _ in range(n_lhs)]
    args = list(lhs_list)
    if gain is not None:
        in_specs.append(pl.BlockSpec((1, kp), lambda i, j: (0, 0)))
        args.append(gain.reshape(1, kp))
    for p in range(n_lhs):
        in_specs.append(pl.BlockSpec((kp, tn), lambda i, j, p=p: (p, j)))
        args.append(w)
    if side_w is not None:
        in_specs.append(pl.BlockSpec((kp, LANES), lambda i, j: (0, 0)))
        args.append(side_w)
    if res is not None:
        in_specs.append(pl.BlockSpec((tm, tn), lambda i, j: (i, j)))
        args.append(res)
    out_shape = [jax.ShapeDtypeStruct((m, n_out), out_dtype)]
    out_specs = [pl.BlockSpec((tm, tn), lambda i, j: (i, j))]
    if side_w is not None:
        out_shape.append(jax.ShapeDtypeStruct((m, LANES), F32))
        out_specs.append(pl.BlockSpec((tm, LANES), lambda i, j: (i, 0)))
    outs = pl.pallas_call(
        functools.partial(_mm_kernel, n_lhs=n_lhs, has_gain=gain is not None,
                          has_res=res is not None, has_side=side_w is not None, staged=staged),
        grid=grid, in_specs=in_specs, out_specs=out_specs, out_shape=out_shape,
        scratch_shapes=[pltpu.VMEM((n_lhs, tm, kp), BF16)] if staged else [],
        compiler_params=_cparams(("parallel", "arbitrary")),
        name="fused_matmul",
    )(*args)
    return outs if side_w is not None else outs[0]


def _s5_prep_kernel(*refs):
    for g in range(refs[0].shape[0]):
        _s5_prep_group(*[r.at[g] for r in refs])


def _s5_prep_group(lre_ref, lim_ref, ldt_ref, btre_ref, btim_ref, cre_ref, cim_ref,
                   tz_ref, wsre_ref, wsim_ref, wcre_ref, wcim_ref, aq_ref, ab_ref, bbt_ref):
    q, ch = S5_Q, S5_CH
    lr = lre_ref[...]
    li = lim_ref[...]
    step = jnp.exp(ldt_ref[...])
    mag = jnp.exp(lr * step)
    ab_re = mag * jnp.cos(li * step)
    ab_im = mag * jnp.sin(li * step)
    den = lr * lr + li * li
    num_re = ab_re - 1.0
    coef_re = (num_re * lr + ab_im * li) / den
    coef_im = (ab_im * lr - num_re * li) / den
    bt_re = btre_ref[...]
    bt_im = btim_ref[...]
    bb_re = coef_re * bt_re - coef_im * bt_im
    bb_im = coef_re * bt_im + coef_im * bt_re
    c_re = cre_ref[...]
    c_im = cim_ref[...]

    pw = [(jnp.ones_like(ab_re), jnp.zeros_like(ab_re))]
    for _ in range(q):
        pr, pi = pw[-1]
        pw.append((pr * ab_re - pi * ab_im, pr * ab_im + pi * ab_re))

    ca_re = [c_re * pr - c_im * pi for pr, pi in pw]
    ca_im = [c_re * pi + c_im * pr for pr, pi in pw]
    wcre_ref[...] = jnp.concatenate(ca_re[1:], axis=0).astype(BF16)
    wcim_ref[...] = jnp.concatenate([-x for x in ca_im[1:]], axis=0).astype(BF16)

    pr_stack = jnp.concatenate(ca_re[:q], axis=0)
    pi_stack = jnp.concatenate(ca_im[:q], axis=0)
    krow = None
    for a, b, sign in ((bb_re, pr_stack, 1.0), (bb_im, pi_stack, -1.0)):
        a_hi, a_lo = _split2(a)
        b_hi, b_lo = _split2(b)
        t = _dot_nt(a_hi, b_hi) + (_dot_nt(a_hi, b_lo) + _dot_nt(a_lo, b_hi))
        krow = sign * t if krow is None else krow + sign * t
    lane = lax.broadcasted_iota(jnp.int32, krow.shape, 1)
    blocks = [krow]
    for s in range(1, q):
        blocks.append(jnp.where(lane >= s * ch, pltpu.roll(krow, s * ch, 1), 0.0))
    tz_ref[...] = jnp.concatenate(blocks, axis=0).astype(BF16)

    ws_re, ws_im = [], []
    for s in range(q):
        pr, pi = pw[q - 1 - s]
        ws_re.append(bb_re * pr - bb_im * pi)
        ws_im.append(bb_re * pi + bb_im * pr)
    wsre_ref[...] = jnp.concatenate(ws_re, axis=0).astype(BF16)
    wsim_ref[...] = jnp.concatenate(ws_im, axis=0).astype(BF16)

    aq_ref[0:1, :] = pw[q][0]
    aq_ref[1:2, :] = pw[q][1]
    ab_ref[0:1, :] = ab_re
    ab_ref[1:2, :] = ab_im
    bbt_ref[0:ch, :] = bb_re
    bbt_ref[ch:2 * ch, :] = bb_im


def s5_prepare(a_re, a_im, log_dt, b_re, b_im, c_re, c_im):
    g, n = a_re.shape
    ch, q = S5_CH, S5_Q
    qc = q * ch
    bt_re = jnp.swapaxes(b_re, 1, 2)
    bt_im = jnp.swapaxes(b_im, 1, 2)

    def per_g(*dims):
        return pl.BlockSpec((S5_GB,) + dims, lambda i: (i,) + (0,) * len(dims))

    return pl.pallas_call(
        _s5_prep_kernel,
        grid=(g // S5_GB,),
        in_specs=[per_g(1, n), per_g(1, n), per_g(1, 1), per_g(ch, n), per_g(ch, n), per_g(ch, n), per_g(ch, n)],
        out_specs=[per_g(qc, qc), per_g(qc, n), per_g(qc, n), per_g(qc, n), per_g(qc, n),
                   per_g(2, n), per_g(2, n), per_g(2 * ch, n)],
        out_shape=[jax.ShapeDtypeStruct((g, qc, qc), BF16),
                   jax.ShapeDtypeStruct((g, qc, n), BF16), jax.ShapeDtypeStruct((g, qc, n), BF16),
                   jax.ShapeDtypeStruct((g, qc, n), BF16), jax.ShapeDtypeStruct((g, qc, n), BF16),
                   jax.ShapeDtypeStruct((g, 2, n), F32), jax.ShapeDtypeStruct((g, 2, n), F32),
                   jax.ShapeDtypeStruct((g, 2 * ch, n), F32)],
        compiler_params=_cparams(("parallel",)),
        name="s5_prepare",
    )(a_re.reshape(g, 1, n), a_im.reshape(g, 1, n), log_dt.reshape(g, 1, 1), bt_re, bt_im, c_re, c_im)


def _s5_scan_kernel(h_ref, tz_ref, wsre_ref, wsim_ref, wcre_ref, wcim_ref, aq_ref,
                    y_ref, hfin_ref, xs_sc, u_sc, yg_sc, sre_sc, sim_sc, *, nb, nchunk):
    gb, q, ch = S5_GB, S5_Q, S5_CH
    rows = nb * nchunk
    per_vreg = LANES // ch
    assert gb == per_vreg and q % per_vreg == 0
    slot = lax.broadcasted_iota(jnp.int32, (rows, LANES), 1) // ch

    halves = q // per_vreg

    def rot_rows(w, g):
        if g == 0:
            return w
        cut = (per_vreg - g) * ch
        parts = []
        for hf in range(halves):
            blk = w[hf * LANES:(hf + 1) * LANES]
            parts += [blk[cut:], blk[:cut]]
        return jnp.concatenate(parts, axis=0)

    for s in range(q):
        x = h_ref[pl.ds(s, rows, stride=q), :].astype(BF16)
        k = s % per_vreg
        xs_sc[s] = pltpu.roll(x, k * ch, 1) if k else x
    keep = [jnp.where(slot == j, 1.0, 0.0).astype(BF16) for j in range(per_vreg)]
    for g in range(gb):
        for hf in range(halves):
            acc = None
            for k in range(per_vreg):
                piece = xs_sc[hf * per_vreg + k] * keep[(g + k) % per_vreg]
                acc = piece if acc is None else acc + piece
            u_sc[g, :, hf * LANES:(hf + 1) * LANES] = acc

    for g in range(gb):
        u = u_sc[g]
        tz = rot_rows(tz_ref[g], g)
        tz = jnp.concatenate([pltpu.roll(tz[:, hf * LANES:(hf + 1) * LANES], g * ch, 1) if g
                              else tz[:, hf * LANES:(hf + 1) * LANES] for hf in range(halves)], axis=1)
        yg_sc[g] = _dot(u, tz)
        sre_sc[g] = _dot(u, rot_rows(wsre_ref[g], g))
        sim_sc[g] = _dot(u, rot_rows(wsim_ref[g], g))

    ar = [jnp.broadcast_to(aq_ref[g, 0:1, :], (nb, S5_N)) for g in range(gb)]
    ai = [jnp.broadcast_to(aq_ref[g, 1:2, :], (nb, S5_N)) for g in range(gb)]

    def step(c, carry):
        at = pl.ds(c, nb, stride=nchunk)
        new = []
        for g in range(gb):
            hr, hi = carry[g]
            sr = sre_sc[g, at, :]
            si = sim_sc[g, at, :]
            sre_sc[g, at, :] = hr
            sim_sc[g, at, :] = hi
            new.append((ar[g] * hr - ai[g] * hi + sr, ar[g] * hi + ai[g] * hr + si))
        return tuple(new)

    zero = jnp.zeros((nb, S5_N), F32)
    fin = lax.fori_loop(0, nchunk, step, tuple((zero, zero) for _ in range(gb)))
    for g in range(gb):
        hfin_ref[g, 0] = fin[g][0]
        hfin_ref[g, 1] = fin[g][1]
        yg_sc[g] += (_dot_nt(sre_sc[g].astype(BF16), rot_rows(wcre_ref[g], g))
                     + _dot_nt(sim_sc[g].astype(BF16), rot_rows(wcim_ref[g], g)))

    for t in range(q):
        hf, k = divmod(t, per_vreg)
        acc = jnp.zeros((rows, LANES), F32)
        for g in range(gb):
            acc = jnp.where(slot == (k + g) % per_vreg, yg_sc[g, :, hf * LANES:(hf + 1) * LANES], acc)
        y_ref[pl.ds(t, rows, stride=q), :] = pltpu.roll(acc, (per_vreg - k) * ch, 1) if k else acc


def s5_scan(h, prep, nb, seq):
    tz, ws_re, ws_im, wc_re, wc_im, aq = prep[:6]
    g, qc, _ = tz.shape
    n = S5_N
    gb = S5_GB
    q = S5_Q
    assert gb * S5_CH == LANES and qc == q * S5_CH and seq % q == 0
    nchunk = seq // q
    rows = nb * nchunk
    m = nb * seq

    def blk(*dims):
        return pl.BlockSpec((gb,) + dims, lambda i: (i,) + (0,) * len(dims))

    return pl.pallas_call(
        functools.partial(_s5_scan_kernel, nb=nb, nchunk=nchunk),
        grid=(g // gb,),
        in_specs=[pl.BlockSpec((m, LANES), lambda i: (0, i)),
                  blk(qc, qc), blk(qc, n), blk(qc, n), blk(qc, n), blk(qc, n), blk(2, n)],
        out_specs=[pl.BlockSpec((m, LANES), lambda i: (0, i)), blk(2, nb, n)],
        out_shape=[jax.ShapeDtypeStruct((m, g * S5_CH), F32), jax.ShapeDtypeStruct((g, 2, nb, n), F32)],
        scratch_shapes=[pltpu.VMEM((q, rows, LANES), BF16), pltpu.VMEM((gb, rows, qc), BF16),
                        pltpu.VMEM((gb, rows, qc), F32),
                        pltpu.VMEM((gb, rows, n), F32), pltpu.VMEM((gb, rows, n), F32)],
        compiler_params=_cparams(("parallel",)),
        name="s5_scan",
    )(h, tz, ws_re, ws_im, wc_re, wc_im, aq)


def _s5_step_kernel(u_ref, hre_ref, him_ref, ab_ref, bbt_ref, cre_ref, cim_ref, y_ref, ore_ref, oim_ref):
    gb = u_ref.shape[0]
    ch = S5_CH
    for g in range(gb):
        u = u_ref[g].astype(BF16)
        bb_re = bbt_ref[g, 0:ch, :].astype(BF16)
        bb_im = bbt_ref[g, ch:2 * ch, :].astype(BF16)
        ar = ab_ref[g, 0:1, :]
        ai = ab_ref[g, 1:2, :]
        hr0 = hre_ref[g]
        hi0 = him_ref[g]
        hr = _dot(u, bb_re) + (ar * hr0 - ai * hi0)
        hi = _dot(u, bb_im) + (ar * hi0 + ai * hr0)
        ore_ref[g] = hr
        oim_ref[g] = hi
        y_ref[g] = (_dot_nt(hr.astype(BF16), cre_ref[g].astype(BF16))
                    - _dot_nt(hi.astype(BF16), cim_ref[g].astype(BF16)))


def s5_step(u_t, h_re, h_im, prep, c_re, c_im):
    ab, bbt = prep[6], prep[7]
    g, b, ch = u_t.shape
    n = S5_N
    gb = S5_GB

    def blk(*dims):
        return pl.BlockSpec((gb,) + dims, lambda i: (i,) + (0,) * len(dims))

    return pl.pallas_call(
        _s5_step_kernel,
        grid=(g // gb,),
        in_specs=[blk(b, ch), blk(b, n), blk(b, n), blk(2, n), blk(2 * ch, n), blk(ch, n), blk(ch, n)],
        out_specs=[blk(b, ch), blk(b, n), blk(b, n)],
        out_shape=[jax.ShapeDtypeStruct((g, b, ch), F32), jax.ShapeDtypeStruct((g, b, n), F32),
                   jax.ShapeDtypeStruct((g, b, n), F32)],
        compiler_params=_cparams(("parallel",)),
        name="s5_step",
    )(u_t, h_re, h_im, ab, bbt, c_re, c_im)


def _s5_head_kernel(y_ref, u_ref, d_ref, w_ref, b_ref, nw_ref, o_ref):
    y = y_ref[...] + d_ref[...] * u_ref[...]
    g = _gelu_tanh(y)
    gate = _sigmoid(_dot(g.astype(BF16), w_ref[...].astype(BF16)) + b_ref[...])
    o_ref[...] = _rms(g * gate, nw_ref[...]).astype(o_ref.dtype)


def s5_head(y, h, d, glu_w, glu_b, norm_w, *, tm):
    m, ds = y.shape
    row = lambda a: a.reshape(1, ds)
    vec = pl.BlockSpec((1, ds), lambda i: (0, 0))
    return pl.pallas_call(
        _s5_head_kernel,
        grid=(m // tm,),
        in_specs=[pl.BlockSpec((tm, ds), lambda i: (i, 0)), pl.BlockSpec((tm, ds), lambda i: (i, 0)), vec,
                  pl.BlockSpec((ds, ds), lambda i: (0, 0)), vec, vec],
        out_specs=pl.BlockSpec((tm, ds), lambda i: (i, 0)),
        out_shape=jax.ShapeDtypeStruct((m, ds), BF16),
        compiler_params=_cparams(("parallel",)),
        name="s5_head",
    )(y, h, row(d), glu_w, row(glu_b), row(norm_w))


def _pair_select(first, col0, col1, shape):
    return jnp.where(first, jnp.broadcast_to(col0, shape), jnp.broadcast_to(col1, shape))


def _ssd_chunk_kernel(*refs, d_inner, nheads, n_xparts):
    xparts = refs[:n_xparts]
    (z_ref, dt_ref, cw_ref, cb_ref, dtb_ref, alog_ref, dvec_ref, nw_ref,
     out_ref, ssm_ref, conv_ref, state_sc, xpad_sc, y_sc) = refs[n_xparts:]
    c = pl.program_id(1)
    q = M2_CHUNK
    hp = M2_HEADDIM
    ns = M2_DSTATE
    heads_per_group = nheads // M2_NGROUPS
    halo = 8

    @pl.when(c == 0)
    def _():
        state_sc[...] = jnp.zeros_like(state_sc)
        xpad_sc[0:halo, :] = jnp.zeros((halo, xpad_sc.shape[1]), F32)

    wpart = xparts[0].shape[1]
    for i, xr in enumerate(xparts):
        xpad_sc[halo:halo + q, i * wpart:(i + 1) * wpart] = xr[...]
    cw = cw_ref[...]
    conv = cb_ref[...] + cw[M2_CONV - 1:M2_CONV, :] * xpad_sc[halo:halo + q, :]
    for k in range(1, M2_CONV):
        conv = conv + cw[M2_CONV - 1 - k:M2_CONV - k, :] * xpad_sc[halo - k:halo - k + q, :]
    xpad_sc[0:halo, :] = xpad_sc[q:q + halo, :]
    xc = _silu(conv)

    dt = _softplus(dt_ref[...] + dtb_ref[...])
    a = -jnp.exp(alog_ref[...])
    da = dt * a
    row = lax.broadcasted_iota(jnp.int32, (q, q), 0)
    col = lax.broadcasted_iota(jnp.int32, (q, q), 1)
    causal = row >= col
    tri = jnp.where(causal, 1.0, 0.0).astype(BF16)
    d_hi, d_mid, d_lo = _split3(da)
    acum = _dot(tri, d_hi) + (_dot(tri, d_mid) + _dot(tri, d_lo))
    acum_t = acum.T
    alast = acum[q - 1:q, :]
    first = col < hp
    first_rows = row < hp

    for pr in range(nheads // 2):
        grp = (2 * pr) // heads_per_group
        b_bf = xc[:, d_inner + grp * ns:d_inner + (grp + 1) * ns].astype(BF16)
        c_bf = xc[:, d_inner + (M2_NGROUPS + grp) * ns:d_inner + (M2_NGROUPS + grp + 1) * ns].astype(BF16)
        cb = _dot_nt(c_bf, b_bf)
        xpair = xc[:, pr * 2 * hp:(pr + 1) * 2 * hp]
        h0, h1 = 2 * pr, 2 * pr + 1
        acol = [acum[:, h:h + 1] for h in (h0, h1)]
        m = []
        for k, h in enumerate((h0, h1)):
            seg = jnp.broadcast_to(acol[k], (q, q)) - jnp.broadcast_to(acum_t[h:h + 1, :], (q, q))
            lmat = jnp.exp(jnp.where(causal, seg, -1e30))
            m.append((cb * lmat).astype(BF16))
        dtp = _pair_select(first, dt[:, h0:h0 + 1], dt[:, h1:h1 + 1], (q, q))
        xdt = xpair * dtp
        xdt_bf = xdt.astype(BF16)
        y_diag = jnp.where(first, _dot(m[0], xdt_bf), _dot(m[1], xdt_bf))
        dec_end = _pair_select(first, jnp.exp(alast[:, h0:h0 + 1] - acol[0]),
                               jnp.exp(alast[:, h1:h1 + 1] - acol[1]), (q, q))
        xw_t = (xdt * dec_end).T.astype(BF16)
        chunk_state = _dot(xw_t, b_bf)
        rows = pl.ds(pr * 2 * hp, 2 * hp)
        prev = state_sc[rows, :]
        y_off = _dot_nt(c_bf, prev.astype(BF16)) * _pair_select(first, jnp.exp(acol[0]), jnp.exp(acol[1]), (q, q))
        sdec = jnp.where(first_rows, jnp.broadcast_to(jnp.exp(alast[:, h0:h0 + 1]), (q, q)),
                         jnp.broadcast_to(jnp.exp(alast[:, h1:h1 + 1]), (q, q)))
        state_sc[rows, :] = prev * sdec + chunk_state
        y_sc[:, pr * 2 * hp:(pr + 1) * 2 * hp] = y_diag + y_off + dvec_ref[:, pr * 2 * hp:(pr + 1) * 2 * hp] * xpair

    out_ref[...] = _rms(y_sc[...] * _silu(z_ref[...]), nw_ref[...]).astype(out_ref.dtype)

    @pl.when(c == pl.num_programs(1) - 1)
    def _():
        ssm_ref[...] = state_sc[...]
        conv_ref[...] = xpad_sc[halo + q - (M2_CONV - 1):halo + q, :]


def ssd_prompt(h, dt_raw, nb, seq, p, *, d_s5, d_inner, nheads):
    q = M2_CHUNK
    nc = seq // q
    conv_dim = d_inner + 2 * M2_NGROUPS * M2_DSTATE
    xw = 512
    xoff = d_s5 + d_inner
    assert d_s5 % d_inner == 0 and xoff % xw == 0 and conv_dim % xw == 0
    assert M2_CHUNK == 2 * M2_HEADDIM == M2_DSTATE == LANES
    zblk = d_s5 // d_inner
    n_xparts = conv_dim // xw
    m = nb * seq
    pad = lambda v: jnp.pad(v, (0, LANES - v.shape[0])).reshape(1, LANES)
    dvec = jnp.repeat(p['m2_d'], M2_HEADDIM).reshape(1, d_inner)
    vec = lambda n: pl.BlockSpec((1, n), lambda b, c: (0, 0))
    tok = lambda w, j: pl.BlockSpec((q, w), lambda b, c, j=j: (b * nc + c, j))
    out, ssm, conv = pl.pallas_call(
        functools.partial(_ssd_chunk_kernel, d_inner=d_inner, nheads=nheads, n_xparts=n_xparts),
        grid=(nb, nc),
        in_specs=[tok(xw, xoff // xw + i) for i in range(n_xparts)] + [tok(d_inner, zblk), tok(LANES, 0),
                  pl.BlockSpec((M2_CONV, conv_dim), lambda b, c: (0, 0)), vec(conv_dim), vec(LANES), vec(LANES),
                  vec(d_inner), vec(d_inner)],
        out_specs=[tok(d_inner, 0),
                   pl.BlockSpec((None, nheads * M2_HEADDIM, M2_DSTATE), lambda b, c: (b, 0, 0)),
                   pl.BlockSpec((None, M2_CONV - 1, conv_dim), lambda b, c: (b, 0, 0))],
        out_shape=[jax.ShapeDtypeStruct((m, d_inner), BF16),
                   jax.ShapeDtypeStruct((nb, nheads * M2_HEADDIM, M2_DSTATE), F32),
                   jax.ShapeDtypeStruct((nb, M2_CONV - 1, conv_dim), F32)],
        scratch_shapes=[pltpu.VMEM((nheads * M2_HEADDIM, M2_DSTATE), F32),
                        pltpu.VMEM((q + 8, conv_dim), F32),
                        pltpu.VMEM((q, d_inner), F32)],
        compiler_params=_cparams(("parallel", "arbitrary")),
        name="ssd_chunk",
    )(*([h] * n_xparts), h, dt_raw, p['m2_conv_w'], p['m2_conv_b'].reshape(1, conv_dim), pad(p['m2_dt_bias']),
      pad(p['m2_a_log']), dvec, p['m2_norm_w'].reshape(1, d_inner))
    return out, ssm.reshape(nb, nheads, M2_HEADDIM, M2_DSTATE), conv


SSD_STEP_SEQS = 8


def _ssd_step_kernel(*refs, d_inner, nheads, n_xparts):
    xparts = refs[:n_xparts]
    (z_ref, dt_ref, cs0_ref, cs1_ref, cs2_ref, cw_ref, cb_ref, dtb_ref, alog_ref, dvec_ref, nw_ref, st_ref,
     out_ref, so_ref, lhs_sc, bfull_sc, ct_sc, yt_sc, xs_sc) = refs[n_xparts:]
    i = pl.program_id(0)
    nb = z_ref.shape[0]
    ns = M2_DSTATE
    rows_g = (nheads // M2_NGROUPS) * M2_HEADDIM

    @pl.when(i == 0)
    def _():
        cw = cw_ref[...]
        xbc = jnp.concatenate([xr[...] for xr in xparts], axis=1)
        conv = (cb_ref[...] + cw[3:4, :] * xbc + cw[2:3, :] * cs2_ref[...]
                + cw[1:2, :] * cs1_ref[...] + cw[0:1, :] * cs0_ref[...])
        xc = _silu(conv)
        dt = _softplus(dt_ref[...] + dtb_ref[...])
        dec = jnp.exp(dt * (-jnp.exp(alog_ref[...])))
        hrow = lax.broadcasted_iota(jnp.int32, (LANES, d_inner), 0)
        hcol = lax.broadcasted_iota(jnp.int32, (LANES, d_inner), 1)
        expand = jnp.where(hcol // M2_HEADDIM == hrow, 1.0, 0.0).astype(BF16)

        def expand_heads(v):
            a, b_, c = _split3(v)
            return _dot(a, expand) + (_dot(b_, expand) + _dot(c, expand))

        xs = xc[:, :d_inner]
        xs_sc[...] = xs
        xdt_t = (xs * expand_heads(dt)).T
        d_hi, d_mid, d_lo = _split3(expand_heads(dec).T)
        for g in range(M2_NGROUPS):
            r = slice(g * rows_g, (g + 1) * rows_g)
            lhs_sc[g] = jnp.concatenate([xdt_t[r].astype(BF16), d_hi[r], d_mid[r], d_lo[r]], axis=1)
            b_g = xc[:, d_inner + g * ns:d_inner + (g + 1) * ns]
            bfull_sc[g] = jnp.concatenate([b_g, jnp.zeros_like(b_g)], axis=1)
            c_g = xc[:, d_inner + (M2_NGROUPS + g) * ns:d_inner + (M2_NGROUPS + g + 1) * ns]
            ct_sc[g] = c_g.T
        yt_sc[...] = jnp.zeros_like(yt_sc)

    row_id = lax.broadcasted_iota(jnp.int32, (nb, 2 * ns), 0)
    lane_id = lax.broadcasted_iota(jnp.int32, (nb, 2 * ns), 1)
    col_id = lax.broadcasted_iota(jnp.int32, (ns, nb), 1)
    for j in range(st_ref.shape[0]):
        b = i * st_ref.shape[0] + j
        r_bot = jnp.where((row_id == b) & (lane_id >= ns), 1.0, 0.0).astype(BF16)
        for g in range(M2_NGROUPS):
            r = pl.ds(g * rows_g, rows_g)
            r_top = jnp.where(row_id == b, bfull_sc[g], 0.0).astype(BF16)
            rhs = jnp.concatenate([r_top, r_bot, r_bot, r_bot], axis=0)
            o = _dot(lhs_sc[g], rhs)
            hnew = st_ref[j, r, :] * o[:, ns:] + o[:, :ns]
            so_ref[j, r, :] = hnew
            cm = jnp.where(col_id == b, ct_sc[g], 0.0).astype(BF16)
            yt_sc[r, :] += _dot(hnew.astype(BF16), cm)

    @pl.when(i == pl.num_programs(0) - 1)
    def _():
        y = yt_sc[...].T + dvec_ref[...] * xs_sc[...]
        out_ref[...] = _rms(y * _silu(z_ref[...]), nw_ref[...]).astype(out_ref.dtype)


def ssd_sample(h, dt_raw, state, conv_state, p, *, d_s5, d_inner, nheads):
    nb = h.shape[0]
    conv_dim = d_inner + 2 * M2_NGROUPS * M2_DSTATE
    xw = 512
    xoff = d_s5 + d_inner
    assert nb == LANES and M2_DSTATE == LANES and M2_CONV == 4
    assert xoff % xw == 0 and conv_dim % xw == 0 and d_s5 % d_inner == 0 and nb % SSD_STEP_SEQS == 0
    n_xparts = conv_dim // xw
    rows = nheads * M2_HEADDIM
    rows_g = rows // M2_NGROUPS
    pad = lambda v: jnp.pad(v, (0, LANES - v.shape[0])).reshape(1, LANES)
    dvec = jnp.repeat(p['m2_d'], M2_HEADDIM).reshape(1, d_inner)
    full = lambda a, b, j=0: pl.BlockSpec((a, b), lambda i, j=j: (0, j))
    st_spec = pl.BlockSpec((SSD_STEP_SEQS, rows, M2_DSTATE), lambda i: (i, 0, 0))
    out, new_state = pl.pallas_call(
        functools.partial(_ssd_step_kernel, d_inner=d_inner, nheads=nheads, n_xparts=n_xparts),
        grid=(nb // SSD_STEP_SEQS,),
        in_specs=[full(nb, xw, xoff // xw + k) for k in range(n_xparts)]
        + [full(nb, d_inner, d_s5 // d_inner), full(nb, LANES)]
        + [full(nb, conv_dim)] * 3
        + [full(M2_CONV, conv_dim), full(1, conv_dim), full(1, LANES), full(1, LANES), full(1, d_inner),
           full(1, d_inner), st_spec],
        out_specs=[full(nb, d_inner), st_spec],
        out_shape=[jax.ShapeDtypeStruct((nb, d_inner), BF16), jax.ShapeDtypeStruct((nb, rows, M2_DSTATE), F32)],
        scratch_shapes=[pltpu.VMEM((M2_NGROUPS, rows_g, 4 * nb), BF16),
                        pltpu.VMEM((M2_NGROUPS, nb, 2 * M2_DSTATE), F32),
                        pltpu.VMEM((M2_NGROUPS, M2_DSTATE, nb), F32),
                        pltpu.VMEM((rows, nb), F32),
                        pltpu.VMEM((nb, d_inner), F32)],
        compiler_params=_cparams(("arbitrary",)),
        name="ssd_step",
    )(*([h] * n_xparts), h, dt_raw, conv_state[:, 0], conv_state[:, 1], conv_state[:, 2],
      p['m2_conv_w'], p['m2_conv_b'].reshape(1, conv_dim), pad(p['m2_dt_bias']), pad(p['m2_a_log']),
      dvec, p['m2_norm_w'].reshape(1, d_inner), state.reshape(nb, rows, M2_DSTATE))
    xbc = lax.slice_in_dim(h, xoff, xoff + conv_dim, axis=1)
    new_conv = jnp.concatenate([conv_state[:, 1:], xbc[:, None, :]], axis=1)
    return out, new_state.reshape(state.shape), new_conv


def _softmax_rows(s):
    e = jnp.exp(s - jnp.max(s, axis=-1, keepdims=True))
    return e / jnp.sum(e, axis=-1, keepdims=True)


def _attn_kernel(q_ref, k_ref, v_ref, o_ref, *, scale):
    s = _dot_nt(q_ref[...].astype(BF16), k_ref[...].astype(BF16)) * scale
    o_ref[...] = _dot(_softmax_rows(s).astype(BF16), v_ref[...].astype(BF16)).astype(o_ref.dtype)


def attention_prompt(q, k, v, nb, seq, mem, heads, *, tq):
    d = q.shape[1]
    hd = d // heads
    nq = seq // tq
    kv_spec = pl.BlockSpec((mem, hd), lambda b, h, i: (b, h))
    q_spec = pl.BlockSpec((tq, hd), lambda b, h, i: (b * nq + i, h))
    return pl.pallas_call(
        functools.partial(_attn_kernel, scale=hd ** -0.5),
        grid=(nb, heads, nq),
        in_specs=[q_spec, kv_spec, kv_spec],
        out_specs=q_spec,
        out_shape=jax.ShapeDtypeStruct(q.shape, BF16),
        compiler_params=_cparams(("parallel", "parallel", "parallel")),
        name="attention_prompt",
    )(q, k, v)


ATTN_STEP_SEQS = 4


ATTN_STEP_ROWS = 64


def _attn_step_kernel(q_ref, k_ref, v_ref, o_ref, s_sc, *, scale):
    nseq, mem, heads, hd = k_ref.shape
    ch = ATTN_STEP_ROWS
    pack = 8 // heads
    for j in range(nseq):
        q = jnp.concatenate([q_ref[j]] * pack, axis=0)

        def score(c, mx):
            rows = pl.ds(pl.multiple_of(c * ch, ch), ch)
            k = k_ref[j, rows].reshape(ch // pack, pack * heads, hd)
            s = jnp.sum(k * q, axis=-1, keepdims=True) * scale
            s_sc[pl.ds(pl.multiple_of(c * (ch // pack), ch // pack), ch // pack)] = s
            return jnp.maximum(mx, jnp.max(s, axis=0))

        mx = lax.fori_loop(0, mem // ch, score, jnp.full((pack * heads, 1), NEG, F32))
        mx1 = mx[0:heads]
        for i in range(1, pack):
            mx1 = jnp.maximum(mx1, mx[i * heads:(i + 1) * heads])
        mx = jnp.concatenate([mx1] * pack, axis=0)

        def accum(c, carry):
            den, acc = carry
            rows = pl.ds(pl.multiple_of(c * ch, ch), ch)
            v = v_ref[j, rows].reshape(ch // pack, pack * heads, hd)
            e = jnp.exp(s_sc[pl.ds(pl.multiple_of(c * (ch // pack), ch // pack), ch // pack)] - mx)
            return den + jnp.sum(e, axis=0), acc + jnp.sum(e * v, axis=0)

        den, acc = lax.fori_loop(0, mem // ch, accum,
                                 (jnp.zeros((pack * heads, 1), F32), jnp.zeros((pack * heads, hd), F32)))
        den1, acc1 = den[0:heads], acc[0:heads]
        for i in range(1, pack):
            den1 = den1 + den[i * heads:(i + 1) * heads]
            acc1 = acc1 + acc[i * heads:(i + 1) * heads]
        o_ref[j] = acc1 / den1


def attention_sample(q, k_cache, v_cache):
    b, mem, heads, hd = k_cache.shape
    nseq = ATTN_STEP_SEQS
    q_spec = pl.BlockSpec((nseq, heads, hd), lambda i: (i, 0, 0))
    kv_spec = pl.BlockSpec((nseq, mem, heads, hd), lambda i: (i, 0, 0, 0))
    out = pl.pallas_call(
        functools.partial(_attn_step_kernel, scale=hd ** -0.5),
        grid=(b // nseq,),
        in_specs=[q_spec, kv_spec, kv_spec],
        out_specs=q_spec,
        out_shape=jax.ShapeDtypeStruct((b, heads, hd), F32),
        scratch_shapes=[pltpu.VMEM((mem * heads // 8, 8, 1), F32)],
        compiler_params=_cparams(("parallel",)),
        name="attention_step",
    )(q.reshape(b, heads, hd), k_cache, v_cache)
    return out.reshape(b, heads * hd)


NEG = -1e30


INFO_G1, INFO_G2, INFO_E1, INFO_E2 = 0, 1, 2, 3


def _lane_pack(lane, values):
    out = 0.0
    for k, v in values:
        out = jnp.where(lane == k, v, out)
    return out


def _router_kernel(x_ref, nw_ref, wr_ref, br_ref, xn_ref, sel_ref, info_ref, *, n_experts):
    xn = _rms(x_ref[...], nw_ref[...])
    xn_ref[...] = xn
    x_hi, x_lo = _split2(xn)
    w_hi, w_lo = _split2(wr_ref[...])
    logits = _dot(x_hi, w_hi) + (_dot(x_hi, w_lo) + _dot(x_lo, w_hi)) + br_ref[...]
    lane = lax.broadcasted_iota(jnp.int32, logits.shape, 1)
    big = jnp.int32(2 ** 30)
    is_c = (lane >= n_experts) & (lane < n_experts + MOE_GROUPS)
    lc = jnp.where(is_c, logits, NEG)
    cmax = jnp.max(lc, axis=-1, keepdims=True)
    gsel = jnp.min(jnp.where(lc == cmax, lane, big), axis=-1, keepdims=True) - n_experts
    gate_c = 1.0 / jnp.sum(jnp.where(is_c, jnp.exp(lc - cmax), 0.0), axis=-1, keepdims=True)
    in_group = (lane < n_experts) & (lane // MOE_PER_GROUP == gsel)
    lf = jnp.where(in_group, logits, NEG)
    t1 = jnp.max(lf, axis=-1, keepdims=True)
    i1 = jnp.min(jnp.where(lf == t1, lane, big), axis=-1, keepdims=True)
    lf2 = jnp.where(lane == i1, NEG, lf)
    t2 = jnp.max(lf2, axis=-1, keepdims=True)
    i2 = jnp.min(jnp.where(lf2 == t2, lane, big), axis=-1, keepdims=True)
    r = jnp.exp(t2 - t1)
    g1 = gate_c / (1.0 + r)
    g2 = gate_c * r / (1.0 + r)
    sel_ref[...] = jnp.where((lane == i1) | (lane == i2), 1.0, 0.0)
    info_ref[...] = _lane_pack(lane, ((INFO_G1, g1), (INFO_G2, g2), (INFO_E1, i1.astype(F32)),
                                      (INFO_E2, i2.astype(F32))))


def moe_router(x, norm_w, w_coarse, b_coarse, w_fine, b_fine, *, tm):
    m, d = x.shape
    e = w_fine.shape[1]
    padw = LANES - e - MOE_GROUPS
    wr = jnp.concatenate([w_fine, w_coarse, jnp.zeros((d, padw), F32)], axis=1)
    br = jnp.concatenate([b_fine, b_coarse, jnp.zeros((padw,), F32)]).reshape(1, LANES)
    tok = pl.BlockSpec((tm, LANES), lambda i: (i, 0))
    return pl.pallas_call(
        functools.partial(_router_kernel, n_experts=e),
        grid=(m // tm,),
        in_specs=[pl.BlockSpec((tm, d), lambda i: (i, 0)), pl.BlockSpec((1, d), lambda i: (0, 0)),
                  pl.BlockSpec((d, LANES), lambda i: (0, 0)), pl.BlockSpec((1, LANES), lambda i: (0, 0))],
        out_specs=[pl.BlockSpec((tm, d), lambda i: (i, 0)), tok, tok],
        out_shape=[jax.ShapeDtypeStruct((m, d), F32), jax.ShapeDtypeStruct((m, LANES), F32),
                   jax.ShapeDtypeStruct((m, LANES), F32)],
        compiler_params=_cparams(("parallel",)),
        name="moe_router",
    )(x, norm_w.reshape(1, d), wr, br)


def _moe_rank_kernel(sel_ref, info_ref, rank_ref, counts_ref, carry_sc):
    i = pl.program_id(0)

    @pl.when(i == 0)
    def _():
        carry_sc[...] = jnp.zeros_like(carry_sc)

    sel = sel_ref[...]
    tm = sel.shape[0]
    row = lax.broadcasted_iota(jnp.int32, (tm, tm), 0)
    col = lax.broadcasted_iota(jnp.int32, (tm, tm), 1)
    before = jnp.where(row > col, 1.0, 0.0).astype(BF16)
    rank = _dot(before, sel.astype(BF16)) + carry_sc[...]
    info = info_ref[...]
    lane = lax.broadcasted_iota(jnp.int32, sel.shape, 1)
    e1 = info[:, INFO_E1:INFO_E1 + 1].astype(jnp.int32)
    e2 = info[:, INFO_E2:INFO_E2 + 1].astype(jnp.int32)
    r1 = jnp.sum(jnp.where(lane == e1, rank, 0.0), axis=-1, keepdims=True)
    r2 = jnp.sum(jnp.where(lane == e2, rank, 0.0), axis=-1, keepdims=True)
    rank_ref[...] = _lane_pack(lane, ((0, r1), (1, r2)))
    carry_sc[...] += jnp.sum(sel, axis=0, keepdims=True)

    @pl.when(i == pl.num_programs(0) - 1)
    def _():
        counts_ref[...] = carry_sc[...]


def _moe_plan_kernel(rank_ref, info_ref, counts_ref, dest_ref, tiles_ref, *, tile, n_experts):
    counts = counts_ref[...]
    ntile_e = jnp.floor((counts + (tile - 1)) * (1.0 / tile))
    padded = jnp.broadcast_to(ntile_e * tile, (8, LANES))
    r = lax.broadcasted_iota(jnp.int32, (LANES, LANES), 0)
    c = lax.broadcasted_iota(jnp.int32, (LANES, LANES), 1)
    lower = jnp.where(r < c, 1.0, 0.0).astype(BF16)
    p_hi, p_mid, p_lo = _split3(padded)
    offs = (_dot(p_hi, lower) + (_dot(p_mid, lower) + _dot(p_lo, lower)))[0:1, :]
    ends = offs + padded[0:1, :]

    info = info_ref[...]
    rank = rank_ref[...]
    lane = lax.broadcasted_iota(jnp.int32, info.shape, 1)
    e1 = info[:, INFO_E1:INFO_E1 + 1].astype(jnp.int32)
    e2 = info[:, INFO_E2:INFO_E2 + 1].astype(jnp.int32)
    d1 = jnp.sum(jnp.where(lane == e1, offs, 0.0), axis=-1, keepdims=True) + rank[:, 0:1]
    d2 = jnp.sum(jnp.where(lane == e2, offs, 0.0), axis=-1, keepdims=True) + rank[:, 1:2]
    dest_ref[...] = _lane_pack(lane, ((0, d1), (1, d2))).astype(jnp.int32)

    ends_col = jnp.broadcast_to(ends, (LANES, LANES)).T
    start = (c * tile).astype(F32)
    n_before = jnp.sum(jnp.where((ends_col <= start) & (r < n_experts), 1.0, 0.0), axis=0, keepdims=True)
    lane1 = lax.broadcasted_iota(jnp.int32, (1, LANES), 1)
    total = jnp.sum(jnp.where(lane1 == n_experts - 1, ends, 0.0), axis=-1, keepdims=True)
    tail = jnp.where(counts > 0.0, ends - tile, -1.0)
    tiles_ref[...] = jnp.zeros_like(tiles_ref)
    tiles_ref[0:1, :] = jnp.minimum(n_before, n_experts - 1.0).astype(jnp.int32)
    tiles_ref[1:2, :] = jnp.broadcast_to(total * (1.0 / tile), (1, LANES)).astype(jnp.int32)
    tiles_ref[2:3, :] = tail.astype(jnp.int32)


def _moe_dispatch_kernel(dest_ref, tails_ref, ntiles_ref, *rest, tile, n_experts, group_steps, group_offsets):
    n_groups = len(group_offsets)
    xn_refs = rest[:n_groups]
    xs_ref, zero_sc, sems = rest[n_groups:]
    i = pl.program_id(0)
    n_tiles_max = xs_ref.shape[0] // tile

    def fill_copy(j):
        return pltpu.make_async_copy(zero_sc, xs_ref.at[pl.ds(pl.multiple_of(j * tile, tile), tile)], sems.at[2])

    def tail_copy(e):
        return pltpu.make_async_copy(zero_sc, xs_ref.at[pl.ds(pl.multiple_of(tails_ref[e], tile), tile)],
                                     sems.at[1])

    @pl.when(i == 0)
    def _():
        zero_sc[...] = jnp.zeros_like(zero_sc)

        def fill(j, _):
            fill_copy(j).start()
            return 0

        def clear(e, _):
            @pl.when(tails_ref[e] >= 0)
            def _():
                tail_copy(e).start()
            return 0

        def clear_wait(e, _):
            @pl.when(tails_ref[e] >= 0)
            def _():
                tail_copy(e).wait()
            return 0

        lax.fori_loop(ntiles_ref[0], n_tiles_max, fill, 0)
        lax.fori_loop(0, n_experts, clear, 0)
        lax.fori_loop(0, n_experts, clear_wait, 0)

    for g, xn_ref in enumerate(xn_refs):
        @pl.when((i >= group_steps[g]) & (i < group_steps[g + 1]))
        def _(g=g, xn_ref=xn_ref):
            tm = xn_ref.shape[0]
            base = 2 * (group_offsets[g] + (i - group_steps[g]) * tm)
            for r in range(tm):
                for k in range(2):
                    pltpu.make_async_copy(xn_ref.at[pl.ds(r, 1)], xs_ref.at[pl.ds(dest_ref[base + 2 * r + k], 1)],
                                          sems.at[0]).start(priority=k)
            for _ in range(2):
                pltpu.make_async_copy(xn_ref, xs_ref.at[pl.ds(0, tm)], sems.at[0]).wait()

    @pl.when(i == pl.num_programs(0) - 1)
    def _():
        def fill_wait(j, _):
            fill_copy(j).wait()
            return 0

        lax.fori_loop(ntiles_ref[0], n_tiles_max, fill_wait, 0)


def _moe_expert_kernel(texp_ref, ntiles_ref, xs_ref, wg_ref, wu_ref, wd_ref, ys_ref):
    @pl.when(pl.program_id(0) < ntiles_ref[0])
    def _():
        x = xs_ref[...].astype(BF16)
        hg = _dot(x, wg_ref[...].astype(BF16))
        hu = _dot(x, wu_ref[...].astype(BF16))
        ys_ref[...] = _dot((_silu(hg) * hu).astype(BF16), wd_ref[...].astype(BF16))


def _moe_combine_kernel(dest_ref, ys_ref, x_ref, info_ref, fw_ref, y_ref, buf_sc, sems):
    s = pl.program_id(0)
    n_blocks = pl.num_programs(0) - 1
    tm = x_ref.shape[0]
    slot = s % 2

    @pl.when(s < n_blocks)
    def _():
        for r in range(tm):
            for k in range(2):
                pltpu.make_async_copy(ys_ref.at[pl.ds(dest_ref[2 * (s * tm + r) + k], 1)],
                                      buf_sc.at[slot, k, pl.ds(r, 1)], sems.at[slot]).start(priority=k)

    @pl.when(s > 0)
    def _():
        prev = 1 - slot
        for k in range(2):
            pltpu.make_async_copy(ys_ref.at[pl.ds(0, tm)], buf_sc.at[prev, k], sems.at[prev]).wait()
        info = info_ref[...]
        y = (x_ref[...] + info[:, INFO_G1:INFO_G1 + 1] * buf_sc[prev, 0]
             + info[:, INFO_G2:INFO_G2 + 1] * buf_sc[prev, 1])
        y_ref[...] = _rms(y, fw_ref[...])


MOE_TILE = 256
MOE_TOKENS_PER_STEP = 256


def _largest_tile(m, cap):
    return max(t for t in range(8, cap + 1, 8) if m % t == 0)


def moe_routed_final(x_list, p, final_w):
    d = x_list[0].shape[1]
    ne, _, f = p['moe_w_gate'].shape
    sizes = [x.shape[0] for x in x_list]
    m = sum(sizes)
    tile = min(MOE_TILE, m)
    routed = [moe_router(x, p['norm_ffn_w'], p['router_coarse_w'], p['router_coarse_b'],
                         p['router_fine_w'], p['router_fine_b'], tm=min(x.shape[0], 512)) for x in x_list]
    sel = jnp.concatenate([r[1] for r in routed], axis=0)
    info = jnp.concatenate([r[2] for r in routed], axis=0)
    tm = _largest_tile(m, 1024)
    tok = pl.BlockSpec((tm, LANES), lambda i: (i, 0))
    rank, counts = pl.pallas_call(
        _moe_rank_kernel,
        grid=(m // tm,),
        in_specs=[tok, tok],
        out_specs=[tok, pl.BlockSpec((1, LANES), lambda i: (0, 0))],
        out_shape=[jax.ShapeDtypeStruct((m, LANES), F32), jax.ShapeDtypeStruct((1, LANES), F32)],
        scratch_shapes=[pltpu.VMEM((1, LANES), F32)],
        compiler_params=_cparams(("arbitrary",)),
        name="moe_rank",
    )(sel, info)
    n_tiles_max = (2 * m) // tile + ne
    assert n_tiles_max <= LANES
    dest, tiles = pl.pallas_call(
        functools.partial(_moe_plan_kernel, tile=tile, n_experts=ne),
        out_shape=[jax.ShapeDtypeStruct((m, LANES), jnp.int32), jax.ShapeDtypeStruct((8, LANES), jnp.int32)],
        compiler_params=pltpu.CompilerParams(vmem_limit_bytes=VMEM_LIMIT),
        name="moe_plan",
    )(rank, info, counts)
    dest_flat = dest[:, :2].reshape(2 * m)
    tile_expert = tiles[0, :n_tiles_max]
    n_tiles = tiles[1, :1]
    tails = tiles[2, :ne]

    rows = n_tiles_max * tile

    tds = [min(mg, 2 * MOE_TOKENS_PER_STEP) for mg in sizes]
    group_steps = [0]
    for mg, td in zip(sizes, tds):
        group_steps.append(group_steps[-1] + mg // td)
    group_offsets = [sum(sizes[:g]) for g in range(len(sizes))]

    def group_spec(g):
        first, last = group_steps[g], group_steps[g + 1] - 1
        return pl.BlockSpec((tds[g], d), lambda i, *_: (jnp.clip(i, first, last) - first, 0))

    xs = pl.pallas_call(
        functools.partial(_moe_dispatch_kernel, tile=tile, n_experts=ne, group_steps=tuple(group_steps),
                          group_offsets=tuple(group_offsets)),
        grid_spec=pltpu.PrefetchScalarGridSpec(
            num_scalar_prefetch=3, grid=(group_steps[-1],),
            in_specs=[group_spec(g) for g in range(len(sizes))],
            out_specs=pl.BlockSpec(memory_space=pl.ANY),
            scratch_shapes=[pltpu.VMEM((tile, d), F32), pltpu.SemaphoreType.DMA((3,))]),
        out_shape=jax.ShapeDtypeStruct((rows, d), F32),
        compiler_params=_cparams(("arbitrary",)),
        name="moe_dispatch",
    )(dest_flat, tails, n_tiles, *[r[0] for r in routed])

    def tile_idx(i, te, nt):
        return jnp.minimum(i, nt[0] - 1)

    ys = pl.pallas_call(
        _moe_expert_kernel,
        grid_spec=pltpu.PrefetchScalarGridSpec(
            num_scalar_prefetch=2, grid=(n_tiles_max,),
            in_specs=[pl.BlockSpec((tile, d), lambda i, te, nt: (tile_idx(i, te, nt), 0)),
                      pl.BlockSpec((None, d, f), lambda i, te, nt: (te[tile_idx(i, te, nt)], 0, 0)),
                      pl.BlockSpec((None, d, f), lambda i, te, nt: (te[tile_idx(i, te, nt)], 0, 0)),
                      pl.BlockSpec((None, f, d), lambda i, te, nt: (te[tile_idx(i, te, nt)], 0, 0))],
            out_specs=pl.BlockSpec((tile, d), lambda i, te, nt: (tile_idx(i, te, nt), 0))),
        out_shape=jax.ShapeDtypeStruct((rows, d), F32),
        input_output_aliases={2: 0},
        compiler_params=_cparams(("arbitrary",)),
        name="moe_experts",
    )(tile_expert, n_tiles, xs, p['moe_w_gate'], p['moe_w_up'], p['moe_w_down'])

    outs = []
    off = 0
    for x, (_, _, info_g), mg in zip(x_list, routed, sizes):
        tc = min(mg, MOE_TOKENS_PER_STEP)
        outs.append(pl.pallas_call(
            _moe_combine_kernel,
            grid_spec=pltpu.PrefetchScalarGridSpec(
                num_scalar_prefetch=1, grid=(mg // tc + 1,),
                in_specs=[pl.BlockSpec(memory_space=pl.ANY),
                          pl.BlockSpec((tc, d), lambda s, dref: (jnp.maximum(s - 1, 0), 0)),
                          pl.BlockSpec((tc, LANES), lambda s, dref: (jnp.maximum(s - 1, 0), 0)),
                          pl.BlockSpec((1, d), lambda s, dref: (0, 0))],
                out_specs=pl.BlockSpec((tc, d), lambda s, dref: (jnp.maximum(s - 1, 0), 0)),
                scratch_shapes=[pltpu.VMEM((2, 2, tc, d), F32), pltpu.SemaphoreType.DMA((2,))]),
            out_shape=jax.ShapeDtypeStruct((mg, d), F32),
            compiler_params=_cparams(("arbitrary",)),
            name="moe_combine",
        )(dest_flat[2 * off:2 * (off + mg)], ys, x, info_g, final_w.reshape(1, d)))
        off += mg
    return outs


def s5_prompt(h, nb, seq, prep, p):
    y, hfin = s5_scan(h, prep, nb, seq)
    out = s5_head(y, h, p['s5_d'], p['s5_glu_w'], p['s5_glu_b'], p['s5_norm_w'], tm=min(512, nb * seq))
    return out, hfin[:, 0].transpose(1, 0, 2), hfin[:, 1].transpose(1, 0, 2)


def s5_sample(h, st_re, st_im, prep, p):
    g = prep[0].shape[0]
    ch = S5_CH
    ds = g * ch
    b = h.shape[0]
    u_t = h[:, :ds].reshape(b, g, ch).transpose(1, 0, 2)
    y_t, n_re, n_im = s5_step(u_t, st_re.transpose(1, 0, 2), st_im.transpose(1, 0, 2), prep,
                              p['s5_c_re'], p['s5_c_im'])
    y = y_t.transpose(1, 0, 2).reshape(b, ds)
    out = s5_head(y, h, p['s5_d'], p['s5_glu_w'], p['s5_glu_b'], p['s5_norm_w'], tm=b)
    return out, n_re.transpose(1, 0, 2), n_im.transpose(1, 0, 2)


def _row_tile(m):
    return min(m, 1024)


def _col_tile(m, n):
    if m <= 256:
        return n
    return max(t for t in range(256, 1793, 256) if n % t == 0)


def _mixer_and_attention(x, p, s5_prep, *, nb, seq, mem_kv, xa_heads, states):
    m, d = x.shape
    g, n = p['s5_a_re'].shape
    d_s5 = g * S5_CH
    nheads = p['m2_a_log'].shape[0]
    d_inner = nheads * M2_HEADDIM
    conv_dim = d_inner + 2 * M2_NGROUPS * M2_DSTATE
    n_main = d_s5 + d_inner + conv_dim
    tm = _row_tile(m)

    w_in = p['w_in']
    w_dt = jnp.pad(w_in[:, n_main:], ((0, 0), (0, LANES - nheads)))
    tn = _col_tile(m, d)
    h, dt_raw = fused_matmul([x], w_in, n_out=n_main, gain=p['norm_mix_w'], side_w=w_dt, tm=tm,
                             tn=_col_tile(m, n_main))

    if states is None:
        s5_out, s5_re, s5_im = s5_prompt(h, nb, seq, s5_prep, p)
        m2_out, ssm, conv = ssd_prompt(h, dt_raw, nb, seq, p, d_s5=d_s5, d_inner=d_inner, nheads=nheads)
    else:
        s5_out, s5_re, s5_im = s5_sample(h, states[0], states[1], s5_prep, p)
        m2_out, ssm, conv = ssd_sample(h, dt_raw, states[2], states[3], p, d_s5=d_s5, d_inner=d_inner,
                                       nheads=nheads)
    x1 = fused_matmul([s5_out, m2_out], p['w_out'], n_out=d, res=x, tm=tm, tn=tn)

    q = fused_matmul([x1], p['xa_wq'], n_out=d, gain=p['norm_xa_w'], tm=tm, tn=tn,
                     out_dtype=BF16 if states is None else F32)
    if states is None:
        mem = mem_kv[0].shape[0] // nb
        o = attention_prompt(q, mem_kv[0], mem_kv[1], nb, seq, mem, xa_heads, tq=min(seq, 2048))
    else:
        o = attention_sample(q, mem_kv[0], mem_kv[1])
    x2 = fused_matmul([o], p['xa_wo'], n_out=d, res=x1, tm=tm, tn=tn)

    return x2, s5_re, s5_im, ssm, conv


def kernel(x_prompt, x_sample, mem_prompt, state_s5_re, state_s5_im, state_ssm, state_conv, cache_mem_k, cache_mem_v, norm_mix_w, w_in, s5_a_re, s5_a_im, s5_log_dt, s5_b_re, s5_b_im, s5_c_re, s5_c_im, s5_d, s5_glu_w, s5_glu_b, s5_norm_w, m2_conv_w, m2_conv_b, m2_dt_bias, m2_a_log, m2_d, m2_norm_w, w_out, norm_xa_w, norm_mem_w, xa_wq, xa_wk, xa_wv, xa_wo, norm_ffn_w, router_coarse_w, router_coarse_b, router_fine_w, router_fine_b, moe_w_gate, moe_w_up, moe_w_down, norm_final_w):
    depth = w_in.shape[0]
    assert depth == 1, "the final norm is fused into the (only) layer"
    per_layer = dict(
        norm_mix_w=norm_mix_w, w_in=w_in, s5_a_re=s5_a_re, s5_a_im=s5_a_im, s5_log_dt=s5_log_dt,
        s5_b_re=s5_b_re, s5_b_im=s5_b_im, s5_c_re=s5_c_re, s5_c_im=s5_c_im, s5_d=s5_d, s5_glu_w=s5_glu_w,
        s5_glu_b=s5_glu_b, s5_norm_w=s5_norm_w, m2_conv_w=m2_conv_w, m2_conv_b=m2_conv_b, m2_dt_bias=m2_dt_bias,
        m2_a_log=m2_a_log, m2_d=m2_d, m2_norm_w=m2_norm_w, w_out=w_out, norm_xa_w=norm_xa_w,
        norm_mem_w=norm_mem_w, xa_wq=xa_wq, xa_wk=xa_wk, xa_wv=xa_wv, xa_wo=xa_wo, norm_ffn_w=norm_ffn_w,
        router_coarse_w=router_coarse_w, router_coarse_b=router_coarse_b, router_fine_w=router_fine_w,
        router_fine_b=router_fine_b, moe_w_gate=moe_w_gate, moe_w_up=moe_w_up, moe_w_down=moe_w_down)
    p = {k: v[0] for k, v in per_layer.items()}
    for name in ('w_in', 'w_out', 'xa_wq', 'xa_wk', 'xa_wv', 'xa_wo', 's5_glu_w'):
        p[name] = p[name].astype(BF16)
    nb, seq, d = x_prompt.shape
    db, dseq, _ = x_sample.shape
    assert dseq == 1
    mem = mem_prompt.shape[1]
    xa_heads = cache_mem_k.shape[3]

    s5_prep = s5_prepare(p['s5_a_re'], p['s5_a_im'], p['s5_log_dt'], p['s5_b_re'], p['s5_b_im'],
                         p['s5_c_re'], p['s5_c_im'])

    memx = mem_prompt.reshape(nb * mem, d)
    mk = fused_matmul([memx], p['xa_wk'], n_out=d, gain=p['norm_mem_w'], tm=_row_tile(nb * mem),
                      tn=_col_tile(nb * mem, d))
    mv = fused_matmul([memx], p['xa_wv'], n_out=d, gain=p['norm_mem_w'], tm=_row_tile(nb * mem),
                      tn=_col_tile(nb * mem, d))
    xp, p_re, p_im, p_ssm, p_conv = _mixer_and_attention(
        x_prompt.reshape(nb * seq, d), p, s5_prep, nb=nb, seq=seq, mem_kv=(mk, mv), xa_heads=xa_heads, states=None)

    xs, s_re, s_im, s_ssm, s_conv = _mixer_and_attention(
        x_sample.reshape(db, d), p, s5_prep, nb=db, seq=1,
        mem_kv=(cache_mem_k[0], cache_mem_v[0]), xa_heads=xa_heads,
        states=(state_s5_re[0], state_s5_im[0], state_ssm[0], state_conv[0]))

    yp, ys = moe_routed_final([xp, xs], p, norm_final_w)

    kv_shape = (1, nb, mem) + cache_mem_k.shape[3:]
    return (yp.reshape(nb, seq, d), ys.reshape(db, 1, d), p_re[None], p_im[None], p_ssm[None], p_conv[None],
            mk.reshape(kv_shape), mv.reshape(kv_shape), s_re[None], s_im[None], s_ssm[None], s_conv[None])
```

```python
import functools
import math

import jax
import jax.numpy as jnp
from jax import lax
from jax.experimental import pallas as pl
from jax.experimental.pallas import tpu as pltpu

F32 = jnp.float32
BF16 = jnp.bfloat16
RMS_EPS = 1e-6

V7X_VMEM_BYTES = 64 * 1024 * 1024
VMEM_LIMIT = V7X_VMEM_BYTES - 8 * 1024 * 1024
LANES = 128

S5_CH = 16
S5_N = 64
S5_Q = 16
S5_GB = 8
S5_SEQ_PAD = 8
M2_HEADDIM = 64
M2_DSTATE = 128
M2_NGROUPS = 2
M2_CONV = 4
M2_CHUNK = 128
MOE_GROUPS = 4
MOE_PER_GROUP = 8


def _cparams(sem):
    return pltpu.CompilerParams(dimension_semantics=sem, vmem_limit_bytes=VMEM_LIMIT)


def _rms(x, w):
    return x * lax.rsqrt(jnp.mean(x * x, axis=-1, keepdims=True) + RMS_EPS) * w


def _sigmoid(x):
    return 1.0 / (1.0 + jnp.exp(-x))


def _silu(x):
    return x * _sigmoid(x)


def _softplus(x):
    return jnp.maximum(x, 0.0) + jnp.log1p(jnp.exp(-jnp.abs(x)))


def _gelu_tanh(x):
    return 0.5 * x * (1.0 + jnp.tanh(math.sqrt(2.0 / math.pi) * (x + 0.044715 * (x * x * x))))


def _dot(a, b):
    return jnp.dot(a, b, preferred_element_type=F32)


def _dot_nt(a, b):
    return lax.dot_general(a, b, (((1,), (1,)), ((), ())), preferred_element_type=F32)


def _split3(x):
    hi = x.astype(BF16)
    r = x - hi.astype(F32)
    mid = r.astype(BF16)
    lo = (r - mid.astype(F32)).astype(BF16)
    return hi, mid, lo


def _split2(x):
    hi = x.astype(BF16)
    lo = (x - hi.astype(F32)).astype(BF16)
    return hi, lo


def _mm_kernel(*refs, n_lhs, has_gain, has_res, has_side, staged):
    it = iter(refs)
    lhs = [next(it) for _ in range(n_lhs)]
    gain = next(it) if has_gain else None
    ws = [next(it) for _ in range(n_lhs)]
    side_w = next(it) if has_side else None
    res = next(it) if has_res else None
    out = next(it)
    side_out = next(it) if has_side else None
    lhs_bf = next(it) if staged else lhs

    if staged:
        @pl.when(pl.program_id(1) == 0)
        def _():
            for i in range(n_lhs):
                x = lhs[i][...]
                if has_gain:
                    x = _rms(x, gain[...])
                lhs_bf[i] = x.astype(BF16)
            if has_side:
                side_out[...] = _dot(lhs_bf[0], side_w[...].astype(BF16))

    acc = None
    for i in range(n_lhs):
        p = _dot(lhs_bf[i][...], ws[i][...].astype(BF16))
        acc = p if acc is None else acc + p
    if has_res:
        acc = acc + res[...]
    out[...] = acc.astype(out.dtype)


def fused_matmul(lhs_list, w, *, n_out, gain=None, res=None, side_w=None, out_dtype=F32, tm, tn):
    n_lhs = len(lhs_list)
    m, kp = lhs_list[0].shape
    assert all(a.shape == (m, kp) for a in lhs_list)
    assert w.shape[0] == n_lhs * kp and m % tm == 0 and n_out % tn == 0
    assert gain is None or n_lhs == 1
    staged = gain is not None or any(a.dtype != BF16 for a in lhs_list)
    assert staged or side_w is None
    grid = (m // tm, n_out // tn)
    in_specs = [pl.BlockSpec((tm, kp), lambda i, j: (i, 0)) for _ in range(n_lhs)]
    args = list(lhs_list)
    if gain is not None:
        in_specs.append(pl.BlockSpec((1, kp), lambda i, j: (0, 0)))
        args.append(gain.reshape(1, kp))
    for p in range(n_lhs):
        in_specs.append(pl.BlockSpec((kp, tn), lambda i, j, p=p: (p, j)))
        args.append(w)
    if side_w is not None:
        in_specs.append(pl.BlockSpec((kp, LANES), lambda i, j: (0, 0)))
        args.append(side_w)
    if res is not None:
        in_specs.append(pl.BlockSpec((tm, tn), lambda i, j: (i, j)))
        args.append(res)
    out_shape = [jax.ShapeDtypeStruct((m, n_out), out_dtype)]
    out_specs = [pl.BlockSpec((tm, tn), lambda i, j: (i, j))]
    if side_w is not None:
        out_shape.append(jax.ShapeDtypeStruct((m, LANES), F32))
        out_specs.append(pl.BlockSpec((tm, LANES), lambda i, j: (i, 0)))
    outs = pl.pallas_call(
        functools.partial(_mm_kernel, n_lhs=n_lhs, has_gain=gain is not None,
                          has_res=res is not None, has_side=side_w is not None, staged=staged),
        grid=grid, in_specs=in_specs, out_specs=out_specs, out_shape=out_shape,
        scratch_shapes=[pltpu.VMEM((n_lhs, tm, kp), BF16)] if staged else [],
        compiler_params=_cparams(("parallel", "arbitrary")),
        name="fused_matmul",
    )(*args)
    return outs if side_w is not None else outs[0]


def _s5_prep_kernel(*refs):
    for g in range(refs[0].shape[0]):
        _s5_prep_group(*[r.at[g] for r in refs])


def _s5_prep_group(lre_ref, lim_ref, ldt_ref, btre_ref, btim_ref, cre_ref, cim_ref,
                   tz_ref, wsre_ref, wsim_ref, wcre_ref, wcim_ref, aq_ref, ab_ref, bbt_ref):
    q, ch = S5_Q, S5_CH
    lr = lre_ref[...]
    li = lim_ref[...]
    step = jnp.exp(ldt_ref[...])
    mag = jnp.exp(lr * step)
    ab_re = mag * jnp.cos(li * step)
    ab_im = mag * jnp.sin(li * step)
    den = lr * lr + li * li
    num_re = ab_re - 1.0
    coef_re = (num_re * lr + ab_im * li) / den
    coef_im = (ab_im * lr - num_re * li) / den
    bt_re = btre_ref[...]
    bt_im = btim_ref[...]
    bb_re = coef_re * bt_re - coef_im * bt_im
    bb_im = coef_re * bt_im + coef_im * bt_re
    c_re = cre_ref[...]
    c_im = cim_ref[...]

    pw = [(jnp.ones_like(ab_re), jnp.zeros_like(ab_re))]
    for _ in range(q):
        pr, pi = pw[-1]
        pw.append((pr * ab_re - pi * ab_im, pr * ab_im + pi * ab_re))

    ca_re = [c_re * pr - c_im * pi for pr, pi in pw]
    ca_im = [c_re * pi + c_im * pr for pr, pi in pw]
    wcre_ref[...] = jnp.concatenate(ca_re[1:], axis=0).astype(BF16)
    wcim_ref[...] = jnp.concatenate([-x for x in ca_im[1:]], axis=0).astype(BF16)

    pr_stack = jnp.concatenate(ca_re[:q], axis=0)
    pi_stack = jnp.concatenate(ca_im[:q], axis=0)
    krow = None
    for a, b, sign in ((bb_re, pr_stack, 1.0), (bb_im, pi_stack, -1.0)):
        a_hi, a_lo = _split2(a)
        b_hi, b_lo = _split2(b)
        t = _dot_nt(a_hi, b_hi) + (_dot_nt(a_hi, b_lo) + _dot_nt(a_lo, b_hi))
        krow = sign * t if krow is None else krow + sign * t
    lane = lax.broadcasted_iota(jnp.int32, krow.shape, 1)
    blocks = [krow]
    for s in range(1, q):
        blocks.append(jnp.where(lane >= s * ch, pltpu.roll(krow, s * ch, 1), 0.0))
    tz_ref[...] = jnp.concatenate(blocks, axis=0).astype(BF16)

    ws_re, ws_im = [], []
    for s in range(q):
        pr, pi = pw[q - 1 - s]
        ws_re.append(bb_re * pr - bb_im * pi)
        ws_im.append(bb_re * pi + bb_im * pr)
    wsre_ref[...] = jnp.concatenate(ws_re, axis=0).astype(BF16)
    wsim_ref[...] = jnp.concatenate(ws_im, axis=0).astype(BF16)

    aq_ref[0:1, :] = pw[q][0]
    aq_ref[1:2, :] = pw[q][1]
    ab_ref[0:1, :] = ab_re
    ab_ref[1:2, :] = ab_im
    bbt_ref[0:ch, :] = bb_re
    bbt_ref[ch:2 * ch, :] = bb_im


def s5_prepare(a_re, a_im, log_dt, b_re, b_im, c_re, c_im):
    g, n = a_re.shape
    ch, q = S5_CH, S5_Q
    qc = q * ch
    bt_re = jnp.swapaxes(b_re, 1, 2)
    bt_im = jnp.swapaxes(b_im, 1, 2)

    def per_g(*dims):
        return pl.BlockSpec((S5_GB,) + dims, lambda i: (i,) + (0,) * len(dims))

    return pl.pallas_call(
        _s5_prep_kernel,
        grid=(g // S5_GB,),
        in_specs=[per_g(1, n), per_g(1, n), per_g(1, 1), per_g(ch, n), per_g(ch, n), per_g(ch, n), per_g(ch, n)],
        out_specs=[per_g(qc, qc), per_g(qc, n), per_g(qc, n), per_g(qc, n), per_g(qc, n),
                   per_g(2, n), per_g(2, n), per_g(2 * ch, n)],
        out_shape=[jax.ShapeDtypeStruct((g, qc, qc), BF16),
                   jax.ShapeDtypeStruct((g, qc, n), BF16), jax.ShapeDtypeStruct((g, qc, n), BF16),
                   jax.ShapeDtypeStruct((g, qc, n), BF16), jax.ShapeDtypeStruct((g, qc, n), BF16),
                   jax.ShapeDtypeStruct((g, 2, n), F32), jax.ShapeDtypeStruct((g, 2, n), F32),
                   jax.ShapeDtypeStruct((g, 2 * ch, n), F32)],
        compiler_params=_cparams(("parallel",)),
        name="s5_prepare",
    )(a_re.reshape(g, 1, n), a_im.reshape(g, 1, n), log_dt.reshape(g, 1, 1), bt_re, bt_im, c_re, c_im)


def _s5_scan_kernel(h_ref, tz_ref, wsre_ref, wsim_ref, wcre_ref, wcim_ref, aq_ref,
                    y_ref, hfin_ref, xs_sc, u_sc, yg_sc, sre_sc, sim_sc, *, nb, nchunk):
    gb, q, ch = S5_GB, S5_Q, S5_CH
    rows = nb * nchunk
    seq_stride = nchunk + S5_SEQ_PAD
    per_vreg = LANES // ch
    assert gb == per_vreg and q % per_vreg == 0
    slot = lax.broadcasted_iota(jnp.int32, (rows, LANES), 1) // ch

    halves = q // per_vreg

    def rot_rows(w, g):
        if g == 0:
            return w
        cut = (per_vreg - g) * ch
        parts = []
        for hf in range(halves):
            blk = w[hf * LANES:(hf + 1) * LANES]
            parts += [blk[cut:], blk[:cut]]
        return jnp.concatenate(parts, axis=0)

    for s in range(q):
        x = h_ref[pl.ds(s, rows, stride=q), :].astype(BF16)
        k = s % per_vreg
        xs_sc[s] = pltpu.roll(x, k * ch, 1) if k else x
    keep = [jnp.where(slot == j, 1.0, 0.0).astype(BF16) for j in range(per_vreg)]
    for g in range(gb):
        for hf in range(halves):
            acc = None
            for k in range(per_vreg):
                piece = xs_sc[hf * per_vreg + k] * keep[(g + k) % per_vreg]
                acc = piece if acc is None else acc + piece
            u_sc[g, :, hf * LANES:(hf + 1) * LANES] = acc

    for g in range(gb):
        u = u_sc[g]
        tz = rot_rows(tz_ref[g], g)
        tz = jnp.concatenate([pltpu.roll(tz[:, hf * LANES:(hf + 1) * LANES], g * ch, 1) if g
                              else tz[:, hf * LANES:(hf + 1) * LANES] for hf in range(halves)], axis=1)
        yg_sc[g] = _dot(u, tz)
        for sc, w_ref in ((sre_sc, wsre_ref), (sim_sc, wsim_ref)):
            s_all = _dot(u, rot_rows(w_ref[g], g))
            for b in range(nb):
                sc[g, b * seq_stride:b * seq_stride + nchunk, :] = s_all[b * nchunk:(b + 1) * nchunk]

    ar = [jnp.broadcast_to(aq_ref[g, 0:1, :], (nb, S5_N)) for g in range(gb)]
    ai = [jnp.broadcast_to(aq_ref[g, 1:2, :], (nb, S5_N)) for g in range(gb)]

    def step(c, carry):
        at = pl.ds(c, nb, stride=seq_stride)
        new = []
        for g in range(gb):
            hr, hi = carry[g]
            sr = sre_sc[g, at, :]
            si = sim_sc[g, at, :]
            sre_sc[g, at, :] = hr
            sim_sc[g, at, :] = hi
            new.append((ar[g] * hr - ai[g] * hi + sr, ar[g] * hi + ai[g] * hr + si))
        return tuple(new)

    zero = jnp.zeros((nb, S5_N), F32)
    fin = lax.fori_loop(0, nchunk, step, tuple((zero, zero) for _ in range(gb)), unroll=4)
    for g in range(gb):
        hfin_ref[g, 0] = fin[g][0]
        hfin_ref[g, 1] = fin[g][1]
        h_in = [jnp.concatenate([sc[g, b * seq_stride:b * seq_stride + nchunk, :] for b in range(nb)], axis=0)
                for sc in (sre_sc, sim_sc)]
        yg_sc[g] += (_dot_nt(h_in[0].astype(BF16), rot_rows(wcre_ref[g], g))
                     + _dot_nt(h_in[1].astype(BF16), rot_rows(wcim_ref[g], g)))

    for t in range(q):
        hf, k = divmod(t, per_vreg)
        acc = jnp.zeros((rows, LANES), F32)
        for g in range(gb):
            acc = jnp.where(slot == (k + g) % per_vreg, yg_sc[g, :, hf * LANES:(hf + 1) * LANES], acc)
        y_ref[pl.ds(t, rows, stride=q), :] = pltpu.roll(acc, (per_vreg - k) * ch, 1) if k else acc


def s5_scan(h, prep, nb, seq):
    tz, ws_re, ws_im, wc_re, wc_im, aq = prep[:6]
    g, qc, _ = tz.shape
    n = S5_N
    gb = S5_GB
    q = S5_Q
    assert gb * S5_CH == LANES and qc == q * S5_CH and seq % q == 0
    nchunk = seq // q
    rows = nb * nchunk
    m = nb * seq

    def blk(*dims):
        return pl.BlockSpec((gb,) + dims, lambda i: (i,) + (0,) * len(dims))

    return pl.pallas_call(
        functools.partial(_s5_scan_kernel, nb=nb, nchunk=nchunk),
        grid=(g // gb,),
        in_specs=[pl.BlockSpec((m, LANES), lambda i: (0, i)),
                  blk(qc, qc), blk(qc, n), blk(qc, n), blk(qc, n), blk(qc, n), blk(2, n)],
        out_specs=[pl.BlockSpec((m, LANES), lambda i: (0, i)), blk(2, nb, n)],
        out_shape=[jax.ShapeDtypeStruct((m, g * S5_CH), F32), jax.ShapeDtypeStruct((g, 2, nb, n), F32)],
        scratch_shapes=[pltpu.VMEM((q, rows, LANES), BF16), pltpu.VMEM((gb, rows, qc), BF16),
                        pltpu.VMEM((gb, rows, qc), F32),
                        pltpu.VMEM((gb, nb * (nchunk + S5_SEQ_PAD), n), F32),
                        pltpu.VMEM((gb, nb * (nchunk + S5_SEQ_PAD), n), F32)],
        compiler_params=_cparams(("parallel",)),
        name="s5_scan",
    )(h, tz, ws_re, ws_im, wc_re, wc_im, aq)


def _s5_step_kernel(u_ref, hre_ref, him_ref, ab_ref, bbt_ref, cre_ref, cim_ref, y_ref, ore_ref, oim_ref):
    gb = u_ref.shape[0]
    ch = S5_CH
    for g in range(gb):
        u = u_ref[g].astype(BF16)
        bb_re = bbt_ref[g, 0:ch, :].astype(BF16)
        bb_im = bbt_ref[g, ch:2 * ch, :].astype(BF16)
        ar = ab_ref[g, 0:1, :]
        ai = ab_ref[g, 1:2, :]
        hr0 = hre_ref[g]
        hi0 = him_ref[g]
        hr = _dot(u, bb_re) + (ar * hr0 - ai * hi0)
        hi = _dot(u, bb_im) + (ar * hi0 + ai * hr0)
        ore_ref[g] = hr
        oim_ref[g] = hi
        y_ref[g] = (_dot_nt(hr.astype(BF16), cre_ref[g].astype(BF16))
                    - _dot_nt(hi.astype(BF16), cim_ref[g].astype(BF16)))


def s5_step(u_t, h_re, h_im, prep, c_re, c_im):
    ab, bbt = prep[6], prep[7]
    g, b, ch = u_t.shape
    n = S5_N
    gb = S5_GB

    def blk(*dims):
        return pl.BlockSpec((gb,) + dims, lambda i: (i,) + (0,) * len(dims))

    return pl.pallas_call(
        _s5_step_kernel,
        grid=(g // gb,),
        in_specs=[blk(b, ch), blk(b, n), blk(b, n), blk(2, n), blk(2 * ch, n), blk(ch, n), blk(ch, n)],
        out_specs=[blk(b, ch), blk(b, n), blk(b, n)],
        out_shape=[jax.ShapeDtypeStruct((g, b, ch), F32), jax.ShapeDtypeStruct((g, b, n), F32),
                   jax.ShapeDtypeStruct((g, b, n), F32)],
        compiler_params=_cparams(("parallel",)),
        name="s5_step",
    )(u_t, h_re, h_im, ab, bbt, c_re, c_im)


def _s5_head_kernel(y_ref, u_ref, d_ref, w_ref, b_ref, nw_ref, o_ref):
    y = y_ref[...] + d_ref[...] * u_ref[...]
    g = _gelu_tanh(y)
    gate = _sigmoid(_dot(g.astype(BF16), w_ref[...].astype(BF16)) + b_ref[...])
    o_ref[...] = _rms(g * gate, nw_ref[...]).astype(o_ref.dtype)


def s5_head(y, h, d, glu_w, glu_b, norm_w, *, tm):
    m, ds = y.shape
    row = lambda a: a.reshape(1, ds)
    vec = pl.BlockSpec((1, ds), lambda i: (0, 0))
    return pl.pallas_call(
        _s5_head_kernel,
        grid=(m // tm,),
        in_specs=[pl.BlockSpec((tm, ds), lambda i: (i, 0)), pl.BlockSpec((tm, ds), lambda i: (i, 0)), vec,
                  pl.BlockSpec((ds, ds), lambda i: (0, 0)), vec, vec],
        out_specs=pl.BlockSpec((tm, ds), lambda i: (i, 0)),
        out_shape=jax.ShapeDtypeStruct((m, ds), BF16),
        compiler_params=_cparams(("parallel",)),
        name="s5_head",
    )(y, h, row(d), glu_w, row(glu_b), row(norm_w))


def _pair_select(first, col0, col1, shape):
    return jnp.where(first, jnp.broadcast_to(col0, shape), jnp.broadcast_to(col1, shape))


def _ssd_chunk_kernel(*refs, d_inner, nheads, n_xparts):
    xparts = refs[:n_xparts]
    (z_ref, dt_ref, cw_ref, cb_ref, dtb_ref, alog_ref, dvec_ref, nw_ref,
     out_ref, ssm_ref, conv_ref, state_sc, xpad_sc, y_sc) = refs[n_xparts:]
    c = pl.program_id(1)
    q = M2_CHUNK
    hp = M2_HEADDIM
    ns = M2_DSTATE
    heads_per_group = nheads // M2_NGROUPS
    halo = 8

    @pl.when(c == 0)
    def _():
        state_sc[...] = jnp.zeros_like(state_sc)
        xpad_sc[0:halo, :] = jnp.zeros((halo, xpad_sc.shape[1]), F32)

    wpart = xparts[0].shape[1]
    for i, xr in enumerate(xparts):
        xpad_sc[halo:halo + q, i * wpart:(i + 1) * wpart] = xr[...]
    cw = cw_ref[...]
    conv = cb_ref[...] + cw[M2_CONV - 1:M2_CONV, :] * xpad_sc[halo:halo + q, :]
    for k in range(1, M2_CONV):
        conv = conv + cw[M2_CONV - 1 - k:M2_CONV - k, :] * xpad_sc[halo - k:halo - k + q, :]
    xpad_sc[0:halo, :] = xpad_sc[q:q + halo, :]
    xc = _silu(conv)

    dt = _softplus(dt_ref[...] + dtb_ref[...])
    a = -jnp.exp(alog_ref[...])
    da = dt * a
    row = lax.broadcasted_iota(jnp.int32, (q, q), 0)
    col = lax.broadcasted_iota(jnp.int32, (q, q), 1)
    causal = row >= col
    tri = jnp.where(causal, 1.0, 0.0).astype(BF16)
    d_hi, d_mid, d_lo = _split3(da)
    acum = _dot(tri, d_hi) + (_dot(tri, d_mid) + _dot(tri, d_lo))
    acum_t = acum.T
    alast = acum[q - 1:q, :]
    first = col < hp
    first_rows = row < hp

    for pr in range(nheads // 2):
        grp = (2 * pr) // heads_per_group
        b_bf = xc[:, d_inner + grp * ns:d_inner + (grp + 1) * ns].astype(BF16)
        c_bf = xc[:, d_inner + (M2_NGROUPS + grp) * ns:d_inner + (M2_NGROUPS + grp + 1) * ns].astype(BF16)
        cb = _dot_nt(c_bf, b_bf)
        xpair = xc[:, pr * 2 * hp:(pr + 1) * 2 * hp]
        h0, h1 = 2 * pr, 2 * pr + 1
        acol = [acum[:, h:h + 1] for h in (h0, h1)]
        m = []
        for k, h in enumerate((h0, h1)):
            seg = jnp.broadcast_to(acol[k], (q, q)) - jnp.broadcast_to(acum_t[h:h + 1, :], (q, q))
            lmat = jnp.exp(jnp.where(causal, seg, -1e30))
            m.append((cb * lmat).astype(BF16))
        dtp = _pair_select(first, dt[:, h0:h0 + 1], dt[:, h1:h1 + 1], (q, q))
        xdt = xpair * dtp
        xdt_bf = xdt.astype(BF16)
        y_diag = jnp.where(first, _dot(m[0], xdt_bf), _dot(m[1], xdt_bf))
        dec_end = _pair_select(first, jnp.exp(alast[:, h0:h0 + 1] - acol[0]),
                               jnp.exp(alast[:, h1:h1 + 1] - acol[1]), (q, q))
        xw_t = (xdt * dec_end).T.astype(BF16)
        chunk_state = _dot(xw_t, b_bf)
        rows = pl.ds(pr * 2 * hp, 2 * hp)
        prev = state_sc[rows, :]
        y_off = _dot_nt(c_bf, prev.astype(BF16)) * _pair_select(first, jnp.exp(acol[0]), jnp.exp(acol[1]), (q, q))
        sdec = jnp.where(first_rows, jnp.broadcast_to(jnp.exp(alast[:, h0:h0 + 1]), (q, q)),
                         jnp.broadcast_to(jnp.exp(alast[:, h1:h1 + 1]), (q, q)))
        state_sc[rows, :] = prev * sdec + chunk_state
        y_sc[:, pr * 2 * hp:(pr + 1) * 2 * hp] = y_diag + y_off + dvec_ref[:, pr * 2 * hp:(pr + 1) * 2 * hp] * xpair

    out_ref[...] = _rms(y_sc[...] * _silu(z_ref[...]), nw_ref[...]).astype(out_ref.dtype)

    @pl.when(c == pl.num_programs(1) - 1)
    def _():
        ssm_ref[...] = state_sc[...]
        conv_ref[...] = xpad_sc[halo + q - (M2_CONV - 1):halo + q, :]


def ssd_prompt(h, dt_raw, nb, seq, p, *, d_s5, d_inner, nheads):
    q = M2_CHUNK
    nc = seq // q
    conv_dim = d_inner + 2 * M2_NGROUPS * M2_DSTATE
    xw = 512
    xoff = d_s5 + d_inner
    assert d_s5 % d_inner == 0 and xoff % xw == 0 and conv_dim % xw == 0
    assert M2_CHUNK == 2 * M2_HEADDIM == M2_DSTATE == LANES
    zblk = d_s5 // d_inner
    n_xparts = conv_dim // xw
    m = nb * seq
    pad = lambda v: jnp.pad(v, (0, LANES - v.shape[0])).reshape(1, LANES)
    dvec = jnp.repeat(p['m2_d'], M2_HEADDIM).reshape(1, d_inner)
    vec = lambda n: pl.BlockSpec((1, n), lambda b, c: (0, 0))
    tok = lambda w, j: pl.BlockSpec((q, w), lambda b, c, j=j: (b * nc + c, j))
    out, ssm, conv = pl.pallas_call(
        functools.partial(_ssd_chunk_kernel, d_inner=d_inner, nheads=nheads, n_xparts=n_xparts),
        grid=(nb, nc),
        in_specs=[tok(xw, xoff // xw + i) for i in range(n_xparts)] + [tok(d_inner, zblk), tok(LANES, 0),
                  pl.BlockSpec((M2_CONV, conv_dim), lambda b, c: (0, 0)), vec(conv_dim), vec(LANES), vec(LANES),
                  vec(d_inner), vec(d_inner)],
        out_specs=[tok(d_inner, 0),
                   pl.BlockSpec((None, nheads * M2_HEADDIM, M2_DSTATE), lambda b, c: (b, 0, 0)),
                   pl.BlockSpec((None, M2_CONV - 1, conv_dim), lambda b, c: (b, 0, 0))],
        out_shape=[jax.ShapeDtypeStruct((m, d_inner), BF16),
                   jax.ShapeDtypeStruct((nb, nheads * M2_HEADDIM, M2_DSTATE), F32),
                   jax.ShapeDtypeStruct((nb, M2_CONV - 1, conv_dim), F32)],
        scratch_shapes=[pltpu.VMEM((nheads * M2_HEADDIM, M2_DSTATE), F32),
                        pltpu.VMEM((q + 8, conv_dim), F32),
                        pltpu.VMEM((q, d_inner), F32)],
        compiler_params=_cparams(("parallel", "arbitrary")),
        name="ssd_chunk",
    )(*([h] * n_xparts), h, dt_raw, p['m2_conv_w'], p['m2_conv_b'].reshape(1, conv_dim), pad(p['m2_dt_bias']),
      pad(p['m2_a_log']), dvec, p['m2_norm_w'].reshape(1, d_inner))
    return out, ssm.reshape(nb, nheads, M2_HEADDIM, M2_DSTATE), conv


SSD_STEP_SEQS = 8


def _ssd_step_kernel(*refs, d_inner, nheads, n_xparts):
    xparts = refs[:n_xparts]
    (z_ref, dt_ref, cs0_ref, cs1_ref, cs2_ref, cw_ref, cb_ref, dtb_ref, alog_ref, dvec_ref, nw_ref, st_ref,
     out_ref, so_ref, lhs_sc, bfull_sc, ct_sc, yt_sc, xs_sc) = refs[n_xparts:]
    i = pl.program_id(0)
    nb = z_ref.shape[0]
    ns = M2_DSTATE
    rows_g = (nheads // M2_NGROUPS) * M2_HEADDIM

    @pl.when(i == 0)
    def _():
        cw = cw_ref[...]
        xbc = jnp.concatenate([xr[...] for xr in xparts], axis=1)
        conv = (cb_ref[...] + cw[3:4, :] * xbc + cw[2:3, :] * cs2_ref[...]
                + cw[1:2, :] * cs1_ref[...] + cw[0:1, :] * cs0_ref[...])
        xc = _silu(conv)
        dt = _softplus(dt_ref[...] + dtb_ref[...])
        dec = jnp.exp(dt * (-jnp.exp(alog_ref[...])))
        hrow = lax.broadcasted_iota(jnp.int32, (LANES, d_inner), 0)
        hcol = lax.broadcasted_iota(jnp.int32, (LANES, d_inner), 1)
        expand = jnp.where(hcol // M2_HEADDIM == hrow, 1.0, 0.0).astype(BF16)

        def expand_heads(v):
            a, b_, c = _split3(v)
            return _dot(a, expand) + (_dot(b_, expand) + _dot(c, expand))

        xs = xc[:, :d_inner]
        xs_sc[...] = xs
        xdt_t = (xs * expand_heads(dt)).T
        d_hi, d_mid, d_lo = _split3(expand_heads(dec).T)
        for g in range(M2_NGROUPS):
            r = slice(g * rows_g, (g + 1) * rows_g)
            lhs_sc[g] = jnp.concatenate([xdt_t[r].astype(BF16), d_hi[r], d_mid[r], d_lo[r]], axis=1)
            b_g = xc[:, d_inner + g * ns:d_inner + (g + 1) * ns]
            bfull_sc[g] = jnp.concatenate([b_g, jnp.zeros_like(b_g)], axis=1)
            c_g = xc[:, d_inner + (M2_NGROUPS + g) * ns:d_inner + (M2_NGROUPS + g + 1) * ns]
            ct_sc[g] = c_g.T
        yt_sc[...] = jnp.zeros_like(yt_sc)

    row_id = lax.broadcasted_iota(jnp.int32, (nb, 2 * ns), 0)
    lane_id = lax.broadcasted_iota(jnp.int32, (nb, 2 * ns), 1)
    col_id = lax.broadcasted_iota(jnp.int32, (ns, nb), 1)
    for j in range(st_ref.shape[0]):
        b = i * st_ref.shape[0] + j
        r_bot = jnp.where((row_id == b) & (lane_id >= ns), 1.0, 0.0).astype(BF16)
        for g in range(M2_NGROUPS):
            r = pl.ds(g * rows_g, rows_g)
            r_top = jnp.where(row_id == b, bfull_sc[g], 0.0).astype(BF16)
            rhs = jnp.concatenate([r_top, r_bot, r_bot, r_bot], axis=0)
            o = _dot(lhs_sc[g], rhs)
            hnew = st_ref[j, r, :] * o[:, ns:] + o[:, :ns]
            so_ref[j, r, :] = hnew
            cm = jnp.where(col_id == b, ct_sc[g], 0.0).astype(BF16)
            yt_sc[r, :] += _dot(hnew.astype(BF16), cm)

    @pl.when(i == pl.num_programs(0) - 1)
    def _():
        y = yt_sc[...].T + dvec_ref[...] * xs_sc[...]
        out_ref[...] = _rms(y * _silu(z_ref[...]), nw_ref[...]).astype(out_ref.dtype)


def ssd_sample(h, dt_raw, state, conv_state, p, *, d_s5, d_inner, nheads):
    nb = h.shape[0]
    conv_dim = d_inner + 2 * M2_NGROUPS * M2_DSTATE
    xw = 512
    xoff = d_s5 + d_inner
    assert nb == LANES and M2_DSTATE == LANES and M2_CONV == 4
    assert xoff % xw == 0 and conv_dim % xw == 0 and d_s5 % d_inner == 0 and nb % SSD_STEP_SEQS == 0
    n_xparts = conv_dim // xw
    rows = nheads * M2_HEADDIM
    rows_g = rows // M2_NGROUPS
    pad = lambda v: jnp.pad(v, (0, LANES - v.shape[0])).reshape(1, LANES)
    dvec = jnp.repeat(p['m2_d'], M2_HEADDIM).reshape(1, d_inner)
    full = lambda a, b, j=0: pl.BlockSpec((a, b), lambda i, j=j: (0, j))
    st_spec = pl.BlockSpec((SSD_STEP_SEQS, rows, M2_DSTATE), lambda i: (i, 0, 0))
    out, new_state = pl.pallas_call(
        functools.partial(_ssd_step_kernel, d_inner=d_inner, nheads=nheads, n_xparts=n_xparts),
        grid=(nb // SSD_STEP_SEQS,),
        in_specs=[full(nb, xw, xoff // xw + k) for k in range(n_xparts)]
        + [full(nb, d_inner, d_s5 // d_inner), full(nb, LANES)]
        + [full(nb, conv_dim)] * 3
        + [full(M2_CONV, conv_dim), full(1, conv_dim), full(1, LANES), full(1, LANES), full(1, d_inner),
           full(1, d_inner), st_spec],
        out_specs=[full(nb, d_inner), st_spec],
        out_shape=[jax.ShapeDtypeStruct((nb, d_inner), BF16), jax.ShapeDtypeStruct((nb, rows, M2_DSTATE), F32)],
        scratch_shapes=[pltpu.VMEM((M2_NGROUPS, rows_g, 4 * nb), BF16),
                        pltpu.VMEM((M2_NGROUPS, nb, 2 * M2_DSTATE), F32),
                        pltpu.VMEM((M2_NGROUPS, M2_DSTATE, nb), F32),
                        pltpu.VMEM((rows, nb), F32),
                        pltpu.VMEM((nb, d_inner), F32)],
        compiler_params=_cparams(("arbitrary",)),
        name="ssd_step",
    )(*([h] * n_xparts), h, dt_raw, conv_state[:, 0], conv_state[:, 1], conv_state[:, 2],
      p['m2_conv_w'], p['m2_conv_b'].reshape(1, conv_dim), pad(p['m2_dt_bias']), pad(p['m2_a_log']),
      dvec, p['m2_norm_w'].reshape(1, d_inner), state.reshape(nb, rows, M2_DSTATE))
    xbc = lax.slice_in_dim(h, xoff, xoff + conv_dim, axis=1)
    new_conv = jnp.concatenate([conv_state[:, 1:], xbc[:, None, :]], axis=1)
    return out, new_state.reshape(state.shape), new_conv


def _softmax_rows(s):
    e = jnp.exp(s - jnp.max(s, axis=-1, keepdims=True))
    return e / jnp.sum(e, axis=-1, keepdims=True)


def _attn_kernel(q_ref, k_ref, v_ref, o_ref, *, scale):
    s = _dot_nt(q_ref[...].astype(BF16), k_ref[...].astype(BF16)) * scale
    o_ref[...] = _dot(_softmax_rows(s).astype(BF16), v_ref[...].astype(BF16)).astype(o_ref.dtype)


def attention_prompt(q, k, v, nb, seq, mem, heads, *, tq):
    d = q.shape[1]
    hd = d // heads
    nq = seq // tq
    kv_spec = pl.BlockSpec((mem, hd), lambda b, h, i: (b, h))
    q_spec = pl.BlockSpec((tq, hd), lambda b, h, i: (b * nq + i, h))
    return pl.pallas_call(
        functools.partial(_attn_kernel, scale=hd ** -0.5),
        grid=(nb, heads, nq),
        in_specs=[q_spec, kv_spec, kv_spec],
        out_specs=q_spec,
        out_shape=jax.ShapeDtypeStruct(q.shape, BF16),
        compiler_params=_cparams(("parallel", "parallel", "parallel")),
        name="attention_prompt",
    )(q, k, v)


ATTN_STEP_SEQS = 4


ATTN_STEP_ROWS = 64


def _attn_step_kernel(q_ref, k_ref, v_ref, o_ref, s_sc, *, scale):
    nseq, mem, heads, hd = k_ref.shape
    ch = ATTN_STEP_ROWS
    pack = 8 // heads
    for j in range(nseq):
        q = jnp.concatenate([q_ref[j]] * pack, axis=0)

        def score(c, mx):
            rows = pl.ds(pl.multiple_of(c * ch, ch), ch)
            k = k_ref[j, rows].reshape(ch // pack, pack * heads, hd)
            s = jnp.sum(k * q, axis=-1, keepdims=True) * scale
            s_sc[pl.ds(pl.multiple_of(c * (ch // pack), ch // pack), ch // pack)] = s
            return jnp.maximum(mx, jnp.max(s, axis=0))

        mx = lax.fori_loop(0, mem // ch, score, jnp.full((pack * heads, 1), NEG, F32))
        mx1 = mx[0:heads]
        for i in range(1, pack):
            mx1 = jnp.maximum(mx1, mx[i * heads:(i + 1) * heads])
        mx = jnp.concatenate([mx1] * pack, axis=0)

        def accum(c, carry):
            den, acc = carry
            rows = pl.ds(pl.multiple_of(c * ch, ch), ch)
            v = v_ref[j, rows].reshape(ch // pack, pack * heads, hd)
            e = jnp.exp(s_sc[pl.ds(pl.multiple_of(c * (ch // pack), ch // pack), ch // pack)] - mx)
            return den + jnp.sum(e, axis=0), acc + jnp.sum(e * v, axis=0)

        den, acc = lax.fori_loop(0, mem // ch, accum,
                                 (jnp.zeros((pack * heads, 1), F32), jnp.zeros((pack * heads, hd), F32)))
        den1, acc1 = den[0:heads], acc[0:heads]
        for i in range(1, pack):
            den1 = den1 + den[i * heads:(i + 1) * heads]
            acc1 = acc1 + acc[i * heads:(i + 1) * heads]
        o_ref[j] = acc1 / den1


def attention_sample(q, k_cache, v_cache):
    b, mem, heads, hd = k_cache.shape
    nseq = ATTN_STEP_SEQS
    q_spec = pl.BlockSpec((nseq, heads, hd), lambda i: (i, 0, 0))
    kv_spec = pl.BlockSpec((nseq, mem, heads, hd), lambda i: (i, 0, 0, 0))
    out = pl.pallas_call(
        functools.partial(_attn_step_kernel, scale=hd ** -0.5),
        grid=(b // nseq,),
        in_specs=[q_spec, kv_spec, kv_spec],
        out_specs=q_spec,
        out_shape=jax.ShapeDtypeStruct((b, heads, hd), F32),
        scratch_shapes=[pltpu.VMEM((mem * heads // 8, 8, 1), F32)],
        compiler_params=_cparams(("parallel",)),
        name="attention_step",
    )(q.reshape(b, heads, hd), k_cache, v_cache)
    return out.reshape(b, heads * hd)


NEG = -1e30


INFO_G1, INFO_G2, INFO_E1, INFO_E2 = 0, 1, 2, 3


def _lane_pack(lane, values):
    out = 0.0
    for k, v in values:
        out = jnp.where(lane == k, v, out)
    return out


def _router_kernel(x_ref, nw_ref, wr_ref, br_ref, sel_ref, info_ref, *, n_experts):
    xn = _rms(x_ref[...], nw_ref[...])
    x_hi, x_lo = _split2(xn)
    w_hi, w_lo = _split2(wr_ref[...])
    logits = _dot(x_hi, w_hi) + (_dot(x_hi, w_lo) + _dot(x_lo, w_hi)) + br_ref[...]
    lane = lax.broadcasted_iota(jnp.int32, logits.shape, 1)
    big = jnp.int32(2 ** 30)
    is_c = (lane >= n_experts) & (lane < n_experts + MOE_GROUPS)
    lc = jnp.where(is_c, logits, NEG)
    cmax = jnp.max(lc, axis=-1, keepdims=True)
    gsel = jnp.min(jnp.where(lc == cmax, lane, big), axis=-1, keepdims=True) - n_experts
    gate_c = 1.0 / jnp.sum(jnp.where(is_c, jnp.exp(lc - cmax), 0.0), axis=-1, keepdims=True)
    in_group = (lane < n_experts) & (lane // MOE_PER_GROUP == gsel)
    lf = jnp.where(in_group, logits, NEG)
    t1 = jnp.max(lf, axis=-1, keepdims=True)
    i1 = jnp.min(jnp.where(lf == t1, lane, big), axis=-1, keepdims=True)
    lf2 = jnp.where(lane == i1, NEG, lf)
    t2 = jnp.max(lf2, axis=-1, keepdims=True)
    i2 = jnp.min(jnp.where(lf2 == t2, lane, big), axis=-1, keepdims=True)
    r = jnp.exp(t2 - t1)
    g1 = gate_c / (1.0 + r)
    g2 = gate_c * r / (1.0 + r)
    sel_ref[...] = jnp.where((lane == i1) | (lane == i2), 1.0, 0.0)
    info_ref[...] = _lane_pack(lane, ((INFO_G1, g1), (INFO_G2, g2), (INFO_E1, i1.astype(F32)),
                                      (INFO_E2, i2.astype(F32))))


def moe_router(x, norm_w, w_coarse, b_coarse, w_fine, b_fine, *, tm):
    m, d = x.shape
    e = w_fine.shape[1]
    padw = LANES - e - MOE_GROUPS
    wr = jnp.concatenate([w_fine, w_coarse, jnp.zeros((d, padw), F32)], axis=1)
    br = jnp.concatenate([b_fine, b_coarse, jnp.zeros((padw,), F32)]).reshape(1, LANES)
    tok = pl.BlockSpec((tm, LANES), lambda i: (i, 0))
    return pl.pallas_call(
        functools.partial(_router_kernel, n_experts=e),
        grid=(m // tm,),
        in_specs=[pl.BlockSpec((tm, d), lambda i: (i, 0)), pl.BlockSpec((1, d), lambda i: (0, 0)),
                  pl.BlockSpec((d, LANES), lambda i: (0, 0)), pl.BlockSpec((1, LANES), lambda i: (0, 0))],
        out_specs=[tok, tok],
        out_shape=[jax.ShapeDtypeStruct((m, LANES), F32), jax.ShapeDtypeStruct((m, LANES), F32)],
        compiler_params=_cparams(("parallel",)),
        name="moe_router",
    )(x, norm_w.reshape(1, d), wr, br)


def _moe_rank_kernel(sel_ref, info_ref, rank_ref, counts_ref, carry_sc):
    i = pl.program_id(0)

    @pl.when(i == 0)
    def _():
        carry_sc[...] = jnp.zeros_like(carry_sc)

    sel = sel_ref[...]
    tm = sel.shape[0]
    row = lax.broadcasted_iota(jnp.int32, (tm, tm), 0)
    col = lax.broadcasted_iota(jnp.int32, (tm, tm), 1)
    before = jnp.where(row > col, 1.0, 0.0).astype(BF16)
    rank = _dot(before, sel.astype(BF16)) + carry_sc[...]
    info = info_ref[...]
    lane = lax.broadcasted_iota(jnp.int32, sel.shape, 1)
    e1 = info[:, INFO_E1:INFO_E1 + 1].astype(jnp.int32)
    e2 = info[:, INFO_E2:INFO_E2 + 1].astype(jnp.int32)
    r1 = jnp.sum(jnp.where(lane == e1, rank, 0.0), axis=-1, keepdims=True)
    r2 = jnp.sum(jnp.where(lane == e2, rank, 0.0), axis=-1, keepdims=True)
    rank_ref[...] = _lane_pack(lane, ((0, r1), (1, r2)))
    carry_sc[...] += jnp.sum(sel, axis=0, keepdims=True)

    @pl.when(i == pl.num_programs(0) - 1)
    def _():
        counts_ref[...] = carry_sc[...]


def _moe_plan_kernel(rank_ref, info_ref, counts_ref, dest_ref, tiles_ref, *, tile, n_experts):
    counts = counts_ref[...]
    ntile_e = jnp.floor((counts + (tile - 1)) * (1.0 / tile))
    padded = jnp.broadcast_to(ntile_e * tile, (8, LANES))
    r = lax.broadcasted_iota(jnp.int32, (LANES, LANES), 0)
    c = lax.broadcasted_iota(jnp.int32, (LANES, LANES), 1)
    lower = jnp.where(r < c, 1.0, 0.0).astype(BF16)
    p_hi, p_mid, p_lo = _split3(padded)
    offs = (_dot(p_hi, lower) + (_dot(p_mid, lower) + _dot(p_lo, lower)))[0:1, :]
    ends = offs + padded[0:1, :]

    info = info_ref[...]
    rank = rank_ref[...]
    lane = lax.broadcasted_iota(jnp.int32, info.shape, 1)
    e1 = info[:, INFO_E1:INFO_E1 + 1].astype(jnp.int32)
    e2 = info[:, INFO_E2:INFO_E2 + 1].astype(jnp.int32)
    d1 = jnp.sum(jnp.where(lane == e1, offs, 0.0), axis=-1, keepdims=True) + rank[:, 0:1]
    d2 = jnp.sum(jnp.where(lane == e2, offs, 0.0), axis=-1, keepdims=True) + rank[:, 1:2]
    dest_ref[...] = _lane_pack(lane, ((0, d1), (1, d2))).astype(jnp.int32)

    ends_col = jnp.broadcast_to(ends, (LANES, LANES)).T
    start = (c * tile).astype(F32)
    n_before = jnp.sum(jnp.where((ends_col <= start) & (r < n_experts), 1.0, 0.0), axis=0, keepdims=True)
    lane1 = lax.broadcasted_iota(jnp.int32, (1, LANES), 1)
    total = jnp.sum(jnp.where(lane1 == n_experts - 1, ends, 0.0), axis=-1, keepdims=True)
    tail = jnp.where(counts > 0.0, ends - tile, -1.0)
    tiles_ref[...] = jnp.zeros_like(tiles_ref)
    tiles_ref[0:1, :] = jnp.minimum(n_before, n_experts - 1.0).astype(jnp.int32)
    tiles_ref[1:2, :] = jnp.broadcast_to(total * (1.0 / tile), (1, LANES)).astype(jnp.int32)
    tiles_ref[2:3, :] = tail.astype(jnp.int32)


def _moe_dispatch_kernel(dest_ref, tails_ref, ntiles_ref, *rest, tile, n_experts, group_steps, group_offsets):
    n_groups = len(group_offsets)
    nw_ref = rest[0]
    x_refs = rest[1:1 + n_groups]
    xs_ref, zero_sc, xn_sc, sems = rest[1 + n_groups:]
    i = pl.program_id(0)
    n_tiles_max = xs_ref.shape[0] // tile

    def fill_copy(j):
        return pltpu.make_async_copy(zero_sc, xs_ref.at[pl.ds(pl.multiple_of(j * tile, tile), tile)], sems.at[2])

    def tail_copy(e):
        return pltpu.make_async_copy(zero_sc, xs_ref.at[pl.ds(pl.multiple_of(tails_ref[e], tile), tile)],
                                     sems.at[1])

    @pl.when(i == 0)
    def _():
        zero_sc[...] = jnp.zeros_like(zero_sc)

        def fill(j, _):
            fill_copy(j).start()
            return 0

        def clear(e, _):
            @pl.when(tails_ref[e] >= 0)
            def _():
                tail_copy(e).start()
            return 0

        def clear_wait(e, _):
            @pl.when(tails_ref[e] >= 0)
            def _():
                tail_copy(e).wait()
            return 0

        lax.fori_loop(ntiles_ref[0], n_tiles_max, fill, 0)
        lax.fori_loop(0, n_experts, clear, 0)
        lax.fori_loop(0, n_experts, clear_wait, 0)

    for g, x_ref in enumerate(x_refs):
        @pl.when((i >= group_steps[g]) & (i < group_steps[g + 1]))
        def _(g=g, x_ref=x_ref):
            tm = x_ref.shape[0]
            xn_ref = xn_sc.at[pl.ds(0, tm)]
            xn_ref[...] = _rms(x_ref[...], nw_ref[...])
            base = 2 * (group_offsets[g] + (i - group_steps[g]) * tm)
            for r in range(tm):
                for k in range(2):
                    pltpu.make_async_copy(xn_ref.at[pl.ds(r, 1)], xs_ref.at[pl.ds(dest_ref[base + 2 * r + k], 1)],
                                          sems.at[0]).start(priority=k)
            for _ in range(2):
                pltpu.make_async_copy(xn_ref, xs_ref.at[pl.ds(0, tm)], sems.at[0]).wait()

    @pl.when(i == pl.num_programs(0) - 1)
    def _():
        def fill_wait(j, _):
            fill_copy(j).wait()
            return 0

        lax.fori_loop(ntiles_ref[0], n_tiles_max, fill_wait, 0)


def _moe_expert_kernel(texp_ref, ntiles_ref, xs_ref, wg_ref, wu_ref, wd_ref, ys_ref):
    @pl.when(pl.program_id(0) < ntiles_ref[0])
    def _():
        x = xs_ref[...].astype(BF16)
        hg = _dot(x, wg_ref[...].astype(BF16))
        hu = _dot(x, wu_ref[...].astype(BF16))
        ys_ref[...] = _dot((_silu(hg) * hu).astype(BF16), wd_ref[...].astype(BF16))


def _moe_combine_kernel(dest_ref, ys_ref, x_ref, info_ref, fw_ref, y_ref, buf_sc, sems):
    s = pl.program_id(0)
    n_blocks = pl.num_programs(0) - 1
    tm = x_ref.shape[0]
    slot = s % 2

    @pl.when(s < n_blocks)
    def _():
        for r in range(tm):
            for k in range(2):
                pltpu.make_async_copy(ys_ref.at[pl.ds(dest_ref[2 * (s * tm + r) + k], 1)],
                                      buf_sc.at[slot, k, pl.ds(r, 1)], sems.at[slot]).start(priority=k)

    @pl.when(s > 0)
    def _():
        prev = 1 - slot
        for k in range(2):
            pltpu.make_async_copy(ys_ref.at[pl.ds(0, tm)], buf_sc.at[prev, k], sems.at[prev]).wait()
        info = info_ref[...]
        y = (x_ref[...] + info[:, INFO_G1:INFO_G1 + 1] * buf_sc[prev, 0]
             + info[:, INFO_G2:INFO_G2 + 1] * buf_sc[prev, 1])
        y_ref[...] = _rms(y, fw_ref[...])


MOE_TILE = 256
MOE_TOKENS_PER_STEP = 256


def _largest_tile(m, cap):
    return max(t for t in range(8, cap + 1, 8) if m % t == 0)


def moe_routed_final(x_list, p, final_w):
    d = x_list[0].shape[1]
    ne, _, f = p['moe_w_gate'].shape
    sizes = [x.shape[0] for x in x_list]
    m = sum(sizes)
    tile = min(MOE_TILE, m)
    routed = [moe_router(x, p['norm_ffn_w'], p['router_coarse_w'], p['router_coarse_b'],
                         p['router_fine_w'], p['router_fine_b'], tm=min(x.shape[0], 512)) for x in x_list]
    sel = jnp.concatenate([r[0] for r in routed], axis=0)
    info = jnp.concatenate([r[1] for r in routed], axis=0)
    tm = _largest_tile(m, 1024)
    tok = pl.BlockSpec((tm, LANES), lambda i: (i, 0))
    rank, counts = pl.pallas_call(
        _moe_rank_kernel,
        grid=(m // tm,),
        in_specs=[tok, tok],
        out_specs=[tok, pl.BlockSpec((1, LANES), lambda i: (0, 0))],
        out_shape=[jax.ShapeDtypeStruct((m, LANES), F32), jax.ShapeDtypeStruct((1, LANES), F32)],
        scratch_shapes=[pltpu.VMEM((1, LANES), F32)],
        compiler_params=_cparams(("arbitrary",)),
        name="moe_rank",
    )(sel, info)
    n_tiles_max = (2 * m) // tile + ne
    assert n_tiles_max <= LANES
    dest, tiles = pl.pallas_call(
        functools.partial(_moe_plan_kernel, tile=tile, n_experts=ne),
        out_shape=[jax.ShapeDtypeStruct((m, LANES), jnp.int32), jax.ShapeDtypeStruct((8, LANES), jnp.int32)],
        compiler_params=pltpu.CompilerParams(vmem_limit_bytes=VMEM_LIMIT),
        name="moe_plan",
    )(rank, info, counts)
    dest_flat = dest[:, :2].reshape(2 * m)
    tile_expert = tiles[0, :n_tiles_max]
    n_tiles = tiles[1, :1]
    tails = tiles[2, :ne]

    rows = n_tiles_max * tile

    tds = [min(mg, 2 * MOE_TOKENS_PER_STEP) for mg in sizes]
    group_steps = [0]
    for mg, td in zip(sizes, tds):
        group_steps.append(group_steps[-1] + mg // td)
    group_offsets = [sum(sizes[:g]) for g in range(len(sizes))]

    def group_spec(g):
        first, last = group_steps[g], group_steps[g + 1] - 1
        return pl.BlockSpec((tds[g], d), lambda i, *_: (jnp.clip(i, first, last) - first, 0))

    xs = pl.pallas_call(
        functools.partial(_moe_dispatch_kernel, tile=tile, n_experts=ne, group_steps=tuple(group_steps),
                          group_offsets=tuple(group_offsets)),
        grid_spec=pltpu.PrefetchScalarGridSpec(
            num_scalar_prefetch=3, grid=(group_steps[-1],),
            in_specs=[pl.BlockSpec((1, d), lambda i, *_: (0, 0))] + [group_spec(g) for g in range(len(sizes))],
            out_specs=pl.BlockSpec(memory_space=pl.ANY),
            scratch_shapes=[pltpu.VMEM((tile, d), F32), pltpu.VMEM((max(tds), d), F32),
                            pltpu.SemaphoreType.DMA((3,))]),
        out_shape=jax.ShapeDtypeStruct((rows, d), F32),
        compiler_params=_cparams(("arbitrary",)),
        name="moe_dispatch",
    )(dest_flat, tails, n_tiles, p['norm_ffn_w'].reshape(1, d), *x_list)

    def tile_idx(i, te, nt):
        return jnp.minimum(i, nt[0] - 1)

    ys = pl.pallas_call(
        _moe_expert_kernel,
        grid_spec=pltpu.PrefetchScalarGridSpec(
            num_scalar_prefetch=2, grid=(n_tiles_max,),
            in_specs=[pl.BlockSpec((tile, d), lambda i, te, nt: (tile_idx(i, te, nt), 0)),
                      pl.BlockSpec((None, d, f), lambda i, te, nt: (te[tile_idx(i, te, nt)], 0, 0)),
                      pl.BlockSpec((None, d, f), lambda i, te, nt: (te[tile_idx(i, te, nt)], 0, 0)),
                      pl.BlockSpec((None, f, d), lambda i, te, nt: (te[tile_idx(i, te, nt)], 0, 0))],
            out_specs=pl.BlockSpec((tile, d), lambda i, te, nt: (tile_idx(i, te, nt), 0))),
        out_shape=jax.ShapeDtypeStruct((rows, d), F32),
        input_output_aliases={2: 0},
        compiler_params=_cparams(("arbitrary",)),
        name="moe_experts",
    )(tile_expert, n_tiles, xs, p['moe_w_gate'], p['moe_w_up'], p['moe_w_down'])

    outs = []
    off = 0
    for x, (_, info_g), mg in zip(x_list, routed, sizes):
        tc = min(mg, MOE_TOKENS_PER_STEP)
        outs.append(pl.pallas_call(
            _moe_combine_kernel,
            grid_spec=pltpu.PrefetchScalarGridSpec(
                num_scalar_prefetch=1, grid=(mg // tc + 1,),
                in_specs=[pl.BlockSpec(memory_space=pl.ANY),
                          pl.BlockSpec((tc, d), lambda s, dref: (jnp.maximum(s - 1, 0), 0)),
                          pl.BlockSpec((tc, LANES), lambda s, dref: (jnp.maximum(s - 1, 0), 0)),
                          pl.BlockSpec((1, d), lambda s, dref: (0, 0))],
                out_specs=pl.BlockSpec((tc, d), lambda s, dref: (jnp.maximum(s - 1, 0), 0)),
                scratch_shapes=[pltpu.VMEM((2, 2, tc, d), F32), pltpu.SemaphoreType.DMA((2,))]),
            out_shape=jax.ShapeDtypeStruct((mg, d), F32),
            compiler_params=_cparams(("arbitrary",)),
            name="moe_combine",
        )(dest_flat[2 * off:2 * (off + mg)], ys, x, info_g, final_w.reshape(1, d)))
        off += mg
    return outs


def s5_prompt(h, nb, seq, prep, p):
    y, hfin = s5_scan(h, prep, nb, seq)
    out = s5_head(y, h, p['s5_d'], p['s5_glu_w'], p['s5_glu_b'], p['s5_norm_w'], tm=min(512, nb * seq))
    return out, hfin[:, 0].transpose(1, 0, 2), hfin[:, 1].transpose(1, 0, 2)


def s5_sample(h, st_re, st_im, prep, p):
    g = prep[0].shape[0]
    ch = S5_CH
    ds = g * ch
    b = h.shape[0]
    u_t = h[:, :ds].reshape(b, g, ch).transpose(1, 0, 2)
    y_t, n_re, n_im = s5_step(u_t, st_re.transpose(1, 0, 2), st_im.transpose(1, 0, 2), prep,
                              p['s5_c_re'], p['s5_c_im'])
    y = y_t.transpose(1, 0, 2).reshape(b, ds)
    out = s5_head(y, h, p['s5_d'], p['s5_glu_w'], p['s5_glu_b'], p['s5_norm_w'], tm=b)
    return out, n_re.transpose(1, 0, 2), n_im.transpose(1, 0, 2)


def _row_tile(m):
    return min(m, 1024)


def _col_tile(m, n):
    if m <= 256:
        return n
    return max(t for t in range(256, 1793, 256) if n % t == 0)


def _mixer_and_attention(x, p, s5_prep, *, nb, seq, mem_kv, xa_heads, states):
    m, d = x.shape
    g, n = p['s5_a_re'].shape
    d_s5 = g * S5_CH
    nheads = p['m2_a_log'].shape[0]
    d_inner = nheads * M2_HEADDIM
    conv_dim = d_inner + 2 * M2_NGROUPS * M2_DSTATE
    n_main = d_s5 + d_inner + conv_dim
    tm = _row_tile(m)

    w_in = p['w_in']
    w_dt = jnp.pad(w_in[:, n_main:], ((0, 0), (0, LANES - nheads)))
    tn = _col_tile(m, d)
    h, dt_raw = fused_matmul([x], w_in, n_out=n_main, gain=p['norm_mix_w'], side_w=w_dt, tm=tm,
                             tn=_col_tile(m, n_main))

    if states is None:
        s5_out, s5_re, s5_im = s5_prompt(h, nb, seq, s5_prep, p)
        m2_out, ssm, conv = ssd_prompt(h, dt_raw, nb, seq, p, d_s5=d_s5, d_inner=d_inner, nheads=nheads)
    else:
        s5_out, s5_re, s5_im = s5_sample(h, states[0], states[1], s5_prep, p)
        m2_out, ssm, conv = ssd_sample(h, dt_raw, states[2], states[3], p, d_s5=d_s5, d_inner=d_inner,
                                       nheads=nheads)
    x1 = fused_matmul([s5_out, m2_out], p['w_out'], n_out=d, res=x, tm=tm, tn=tn)

    q = fused_matmul([x1], p['xa_wq'], n_out=d, gain=p['norm_xa_w'], tm=tm, tn=tn,
                     out_dtype=BF16 if states is None else F32)
    if states is None:
        mem = mem_kv[0].shape[0] // nb
        o = attention_prompt(q, mem_kv[0], mem_kv[1], nb, seq, mem, xa_heads, tq=min(seq, 2048))
    else:
        o = attention_sample(q, mem_kv[0], mem_kv[1])
    x2 = fused_matmul([o], p['xa_wo'], n_out=d, res=x1, tm=tm, tn=tn)

    return x2, s5_re, s5_im, ssm, conv


def kernel(x_prompt, x_sample, mem_prompt, state_s5_re, state_s5_im, state_ssm, state_conv, cache_mem_k, cache_mem_v, norm_mix_w, w_in, s5_a_re, s5_a_im, s5_log_dt, s5_b_re, s5_b_im, s5_c_re, s5_c_im, s5_d, s5_glu_w, s5_glu_b, s5_norm_w, m2_conv_w, m2_conv_b, m2_dt_bias, m2_a_log, m2_d, m2_norm_w, w_out, norm_xa_w, norm_mem_w, xa_wq, xa_wk, xa_wv, xa_wo, norm_ffn_w, router_coarse_w, router_coarse_b, router_fine_w, router_fine_b, moe_w_gate, moe_w_up, moe_w_down, norm_final_w):
    depth = w_in.shape[0]
    assert depth == 1, "the final norm is fused into the (only) layer"
    per_layer = dict(
        norm_mix_w=norm_mix_w, w_in=w_in, s5_a_re=s5_a_re, s5_a_im=s5_a_im, s5_log_dt=s5_log_dt,
        s5_b_re=s5_b_re, s5_b_im=s5_b_im, s5_c_re=s5_c_re, s5_c_im=s5_c_im, s5_d=s5_d, s5_glu_w=s5_glu_w,
        s5_glu_b=s5_glu_b, s5_norm_w=s5_norm_w, m2_conv_w=m2_conv_w, m2_conv_b=m2_conv_b, m2_dt_bias=m2_dt_bias,
        m2_a_log=m2_a_log, m2_d=m2_d, m2_norm_w=m2_norm_w, w_out=w_out, norm_xa_w=norm_xa_w,
        norm_mem_w=norm_mem_w, xa_wq=xa_wq, xa_wk=xa_wk, xa_wv=xa_wv, xa_wo=xa_wo, norm_ffn_w=norm_ffn_w,
        router_coarse_w=router_coarse_w, router_coarse_b=router_coarse_b, router_fine_w=router_fine_w,
        router_fine_b=router_fine_b, moe_w_gate=moe_w_gate, moe_w_up=moe_w_up, moe_w_down=moe_w_down)
    p = {k: v[0] for k, v in per_layer.items()}
    for name in ('w_in', 'w_out', 'xa_wq', 'xa_wk', 'xa_wv', 'xa_wo', 's5_glu_w'):
        p[name] = p[name].astype(BF16)
    nb, seq, d = x_prompt.shape
    db, dseq, _ = x_sample.shape
    assert dseq == 1
    mem = mem_prompt.shape[1]
    xa_heads = cache_mem_k.shape[3]

    s5_prep = s5_prepare(p['s5_a_re'], p['s5_a_im'], p['s5_log_dt'], p['s5_b_re'], p['s5_b_im'],
                         p['s5_c_re'], p['s5_c_im'])

    memx = mem_prompt.reshape(nb * mem, d)
    mk = fused_matmul([memx], p['xa_wk'], n_out=d, gain=p['norm_mem_w'], tm=_row_tile(nb * mem),
                      tn=_col_tile(nb * mem, d))
    mv = fused_matmul([memx], p['xa_wv'], n_out=d, gain=p['norm_mem_w'], tm=_row_tile(nb * mem),
                      tn=_col_tile(nb * mem, d))
    xp, p_re, p_im, p_ssm, p_conv = _mixer_and_attention(
        x_prompt.reshape(nb * seq, d), p, s5_prep, nb=nb, seq=seq, mem_kv=(mk, mv), xa_heads=xa_heads, states=None)

    xs, s_re, s_im, s_ssm, s_conv = _mixer_and_attention(
        x_sample.reshape(db, d), p, s5_prep, nb=db, seq=1,
        mem_kv=(cache_mem_k[0], cache_mem_v[0]), xa_heads=xa_heads,
        states=(state_s5_re[0], state_s5_im[0], state_ssm[0], state_conv[0]))

    yp, ys = moe_routed_final([xp, xs], p, norm_final_w)

    kv_shape = (1, nb, mem) + cache_mem_k.shape[3:]
    return (yp.reshape(nb, seq, d), ys.reshape(db, 1, d), p_re[None], p_im[None], p_ssm[None], p_conv[None],
            mk.reshape(kv_shape), mv.reshape(kv_shape), s_re[None], s_im[None], s_ssm[None], s_conv[None])
```

```python
import functools
import math

import jax
import jax.numpy as jnp
from jax import lax
from jax.experimental import pallas as pl
from jax.experimental.pallas import tpu as pltpu

F32 = jnp.float32
BF16 = jnp.bfloat16
RMS_EPS = 1e-6

V7X_VMEM_BYTES = 64 * 1024 * 1024
VMEM_LIMIT = V7X_VMEM_BYTES - 8 * 1024 * 1024
LANES = 128

S5_CH = 16
S5_N = 64
S5_Q = 16
S5_GB = 8
S5_SEQ_PAD = 8
M2_HEADDIM = 64
M2_DSTATE = 128
M2_NGROUPS = 2
M2_CONV = 4
M2_CHUNK = 128
MOE_GROUPS = 4
MOE_PER_GROUP = 8


def _cparams(sem):
    return pltpu.CompilerParams(dimension_semantics=sem, vmem_limit_bytes=VMEM_LIMIT)


def _rms(x, w):
    return x * lax.rsqrt(jnp.mean(x * x, axis=-1, keepdims=True) + RMS_EPS) * w


def _sigmoid(x):
    return 1.0 / (1.0 + jnp.exp(-x))


def _silu(x):
    return x * _sigmoid(x)


def _softplus(x):
    return jnp.maximum(x, 0.0) + jnp.log1p(jnp.exp(-jnp.abs(x)))


def _gelu_tanh(x):
    return 0.5 * x * (1.0 + jnp.tanh(math.sqrt(2.0 / math.pi) * (x + 0.044715 * (x * x * x))))


def _dot(a, b):
    return jnp.dot(a, b, preferred_element_type=F32)


def _dot_nt(a, b):
    return lax.dot_general(a, b, (((1,), (1,)), ((), ())), preferred_element_type=F32)


def _split3(x):
    hi = x.astype(BF16)
    r = x - hi.astype(F32)
    mid = r.astype(BF16)
    lo = (r - mid.astype(F32)).astype(BF16)
    return hi, mid, lo


def _split2(x):
    hi = x.astype(BF16)
    lo = (x - hi.astype(F32)).astype(BF16)
    return hi, lo


def _mm_kernel(*refs, n_lhs, has_gain, has_res, has_side, staged):
    it = iter(refs)
    lhs = [next(it) for _ in range(n_lhs)]
    gain = next(it) if has_gain else None
    ws = [next(it) for _ in range(n_lhs)]
    side_w = next(it) if has_side else None
    res = next(it) if has_res else None
    out = next(it)
    side_out = next(it) if has_side else None
    lhs_bf = next(it) if staged else lhs

    if staged:
        @pl.when(pl.program_id(1) == 0)
        def _():
            for i in range(n_lhs):
                x = lhs[i][...]
                if has_gain:
                    x = _rms(x, gain[...])
                lhs_bf[i] = x.astype(BF16)
            if has_side:
                side_out[...] = _dot(lhs_bf[0], side_w[...].astype(BF16))

    acc = None
    for i in range(n_lhs):
        p = _dot(lhs_bf[i][...], ws[i][...].astype(BF16))
        acc = p if acc is None else acc + p
    if has_res:
        acc = acc + res[...]
    out[...] = acc.astype(out.dtype)


def fused_matmul(lhs_list, w, *, n_out, gain=None, res=None, side_w=None, out_dtype=F32, tm, tn):
    n_lhs = len(lhs_list)
    m, kp = lhs_list[0].shape
    assert all(a.shape == (m, kp) for a in lhs_list)
    assert w.shape[0] == n_lhs * kp and m % tm == 0 and n_out % tn == 0
    assert gain is None or n_lhs == 1
    staged = gain is not None or any(a.dtype != BF16 for a in lhs_list)
    assert staged or side_w is None
    grid = (m // tm, n_out // tn)
    in_specs = [pl.BlockSpec((tm, kp), lambda i, j: (i, 0)) for _ in range(n_lhs)]
    args = list(lhs_list)
    if gain is not None:
        in_specs.append(pl.BlockSpec((1, kp), lambda i, j: (0, 0)))
        args.append(gain.reshape(1, kp))
    for p in range(n_lhs):
        in_specs.append(pl.BlockSpec((kp, tn), lambda i, j, p=p: (p, j)))
        args.append(w)
    if side_w is not None:
        in_specs.append(pl.BlockSpec((kp, LANES), lambda i, j: (0, 0)))
        args.append(side_w)
    if res is not None:
        in_specs.append(pl.BlockSpec((tm, tn), lambda i, j: (i, j)))
        args.append(res)
    out_shape = [jax.ShapeDtypeStruct((m, n_out), out_dtype)]
    out_specs = [pl.BlockSpec((tm, tn), lambda i, j: (i, j))]
    if side_w is not None:
        out_shape.append(jax.ShapeDtypeStruct((m, LANES), F32))
        out_specs.append(pl.BlockSpec((tm, LANES), lambda i, j: (i, 0)))
    outs = pl.pallas_call(
        functools.partial(_mm_kernel, n_lhs=n_lhs, has_gain=gain is not None,
                          has_res=res is not None, has_side=side_w is not None, staged=staged),
        grid=grid, in_specs=in_specs, out_specs=out_specs, out_shape=out_shape,
        scratch_shapes=[pltpu.VMEM((n_lhs, tm, kp), BF16)] if staged else [],
        compiler_params=_cparams(("parallel", "arbitrary")),
        name="fused_matmul",
    )(*args)
    return outs if side_w is not None else outs[0]


def _s5_prep_kernel(*refs):
    for g in range(refs[0].shape[0]):
        _s5_prep_group(*[r.at[g] for r in refs])


def _s5_prep_group(lre_ref, lim_ref, ldt_ref, btre_ref, btim_ref, cre_ref, cim_ref,
                   tz_ref, wsre_ref, wsim_ref, wcre_ref, wcim_ref, aq_ref, ab_ref, bbt_ref):
    q, ch = S5_Q, S5_CH
    lr = lre_ref[...]
    li = lim_ref[...]
    step = jnp.exp(ldt_ref[...])
    mag = jnp.exp(lr * step)
    ab_re = mag * jnp.cos(li * step)
    ab_im = mag * jnp.sin(li * step)
    den = lr * lr + li * li
    num_re = ab_re - 1.0
    coef_re = (num_re * lr + ab_im * li) / den
    coef_im = (ab_im * lr - num_re * li) / den
    bt_re = btre_ref[...]
    bt_im = btim_ref[...]
    bb_re = coef_re * bt_re - coef_im * bt_im
    bb_im = coef_re * bt_im + coef_im * bt_re
    c_re = cre_ref[...]
    c_im = cim_ref[...]

    pw = [(jnp.ones_like(ab_re), jnp.zeros_like(ab_re))]
    for _ in range(q):
        pr, pi = pw[-1]
        pw.append((pr * ab_re - pi * ab_im, pr * ab_im + pi * ab_re))

    ca_re = [c_re * pr - c_im * pi for pr, pi in pw]
    ca_im = [c_re * pi + c_im * pr for pr, pi in pw]
    wcre_ref[...] = jnp.concatenate(ca_re[1:], axis=0).astype(BF16)
    wcim_ref[...] = jnp.concatenate([-x for x in ca_im[1:]], axis=0).astype(BF16)

    pr_stack = jnp.concatenate(ca_re[:q], axis=0)
    pi_stack = jnp.concatenate(ca_im[:q], axis=0)
    krow = None
    for a, b, sign in ((bb_re, pr_stack, 1.0), (bb_im, pi_stack, -1.0)):
        a_hi, a_lo = _split2(a)
        b_hi, b_lo = _split2(b)
        t = _dot_nt(a_hi, b_hi) + (_dot_nt(a_hi, b_lo) + _dot_nt(a_lo, b_hi))
        krow = sign * t if krow is None else krow + sign * t
    lane = lax.broadcasted_iota(jnp.int32, krow.shape, 1)
    blocks = [krow]
    for s in range(1, q):
        blocks.append(jnp.where(lane >= s * ch, pltpu.roll(krow, s * ch, 1), 0.0))
    tz_ref[...] = jnp.concatenate(blocks, axis=0).astype(BF16)

    ws_re, ws_im = [], []
    for s in range(q):
        pr, pi = pw[q - 1 - s]
        ws_re.append(bb_re * pr - bb_im * pi)
        ws_im.append(bb_re * pi + bb_im * pr)
    wsre_ref[...] = jnp.concatenate(ws_re, axis=0).astype(BF16)
    wsim_ref[...] = jnp.concatenate(ws_im, axis=0).astype(BF16)

    aq_ref[0:1, :] = pw[q][0]
    aq_ref[1:2, :] = pw[q][1]
    ab_ref[0:1, :] = ab_re
    ab_ref[1:2, :] = ab_im
    bbt_ref[0:ch, :] = bb_re
    bbt_ref[ch:2 * ch, :] = bb_im


def s5_prepare(a_re, a_im, log_dt, b_re, b_im, c_re, c_im):
    g, n = a_re.shape
    ch, q = S5_CH, S5_Q
    qc = q * ch
    bt_re = jnp.swapaxes(b_re, 1, 2)
    bt_im = jnp.swapaxes(b_im, 1, 2)

    def per_g(*dims):
        return pl.BlockSpec((S5_GB,) + dims, lambda i: (i,) + (0,) * len(dims))

    return pl.pallas_call(
        _s5_prep_kernel,
        grid=(g // S5_GB,),
        in_specs=[per_g(1, n), per_g(1, n), per_g(1, 1), per_g(ch, n), per_g(ch, n), per_g(ch, n), per_g(ch, n)],
        out_specs=[per_g(qc, qc), per_g(qc, n), per_g(qc, n), per_g(qc, n), per_g(qc, n),
                   per_g(2, n), per_g(2, n), per_g(2 * ch, n)],
        out_shape=[jax.ShapeDtypeStruct((g, qc, qc), BF16),
                   jax.ShapeDtypeStruct((g, qc, n), BF16), jax.ShapeDtypeStruct((g, qc, n), BF16),
                   jax.ShapeDtypeStruct((g, qc, n), BF16), jax.ShapeDtypeStruct((g, qc, n), BF16),
                   jax.ShapeDtypeStruct((g, 2, n), F32), jax.ShapeDtypeStruct((g, 2, n), F32),
                   jax.ShapeDtypeStruct((g, 2 * ch, n), F32)],
        compiler_params=_cparams(("parallel",)),
        name="s5_prepare",
    )(a_re.reshape(g, 1, n), a_im.reshape(g, 1, n), log_dt.reshape(g, 1, 1), bt_re, bt_im, c_re, c_im)


def _s5_scan_kernel(h_ref, tz_ref, wsre_ref, wsim_ref, wcre_ref, wcim_ref, aq_ref,
                    y_ref, hfin_ref, xs_sc, u_sc, yg_sc, sre_sc, sim_sc, *, nb, nchunk):
    gb, q, ch = S5_GB, S5_Q, S5_CH
    rows = nb * nchunk
    seq_stride = nchunk + S5_SEQ_PAD
    per_vreg = LANES // ch
    assert gb == per_vreg and q % per_vreg == 0
    slot = lax.broadcasted_iota(jnp.int32, (rows, LANES), 1) // ch

    halves = q // per_vreg

    def rot_rows(w, g):
        if g == 0:
            return w
        cut = (per_vreg - g) * ch
        parts = []
        for hf in range(halves):
            blk = w[hf * LANES:(hf + 1) * LANES]
            parts += [blk[cut:], blk[:cut]]
        return jnp.concatenate(parts, axis=0)

    for s in range(q):
        x = h_ref[pl.ds(s, rows, stride=q), :].astype(BF16)
        k = s % per_vreg
        xs_sc[s] = pltpu.roll(x, k * ch, 1) if k else x
    keep = [jnp.where(slot == j, 1.0, 0.0).astype(BF16) for j in range(per_vreg)]
    for g in range(gb):
        for hf in range(halves):
            acc = None
            for k in range(per_vreg):
                piece = xs_sc[hf * per_vreg + k] * keep[(g + k) % per_vreg]
                acc = piece if acc is None else acc + piece
            u_sc[g, :, hf * LANES:(hf + 1) * LANES] = acc

    for g in range(gb):
        u = u_sc[g]
        tz = rot_rows(tz_ref[g], g)
        tz = jnp.concatenate([pltpu.roll(tz[:, hf * LANES:(hf + 1) * LANES], g * ch, 1) if g
                              else tz[:, hf * LANES:(hf + 1) * LANES] for hf in range(halves)], axis=1)
        yg_sc[g] = _dot(u, tz)
        for sc, w_ref in ((sre_sc, wsre_ref), (sim_sc, wsim_ref)):
            s_all = _dot(u, rot_rows(w_ref[g], g))
            for b in range(nb):
                sc[g, b * seq_stride:b * seq_stride + nchunk, :] = s_all[b * nchunk:(b + 1) * nchunk]

    ar = [jnp.broadcast_to(aq_ref[g, 0:1, :], (nb, S5_N)) for g in range(gb)]
    ai = [jnp.broadcast_to(aq_ref[g, 1:2, :], (nb, S5_N)) for g in range(gb)]

    def step(c, carry):
        at = pl.ds(c, nb, stride=seq_stride)
        new = []
        for g in range(gb):
            hr, hi = carry[g]
            sr = sre_sc[g, at, :]
            si = sim_sc[g, at, :]
            sre_sc[g, at, :] = hr
            sim_sc[g, at, :] = hi
            new.append((ar[g] * hr - ai[g] * hi + sr, ar[g] * hi + ai[g] * hr + si))
        return tuple(new)

    zero = jnp.zeros((nb, S5_N), F32)
    fin = lax.fori_loop(0, nchunk, step, tuple((zero, zero) for _ in range(gb)), unroll=4)
    for g in range(gb):
        hfin_ref[g, 0] = fin[g][0]
        hfin_ref[g, 1] = fin[g][1]
        h_in = [jnp.concatenate([sc[g, b * seq_stride:b * seq_stride + nchunk, :] for b in range(nb)], axis=0)
                for sc in (sre_sc, sim_sc)]
        yg_sc[g] += (_dot_nt(h_in[0].astype(BF16), rot_rows(wcre_ref[g], g))
                     + _dot_nt(h_in[1].astype(BF16), rot_rows(wcim_ref[g], g)))

    for t in range(q):
        hf, k = divmod(t, per_vreg)
        acc = jnp.zeros((rows, LANES), F32)
        for g in range(gb):
            acc = jnp.where(slot == (k + g) % per_vreg, yg_sc[g, :, hf * LANES:(hf + 1) * LANES], acc)
        y_ref[pl.ds(t, rows, stride=q), :] = pltpu.roll(acc, (per_vreg - k) * ch, 1) if k else acc


def s5_scan(h, prep, nb, seq):
    tz, ws_re, ws_im, wc_re, wc_im, aq = prep[:6]
    g, qc, _ = tz.shape
    n = S5_N
    gb = S5_GB
    q = S5_Q
    assert gb * S5_CH == LANES and qc == q * S5_CH and seq % q == 0
    nchunk = seq // q
    rows = nb * nchunk
    m = nb * seq

    def blk(*dims):
        return pl.BlockSpec((gb,) + dims, lambda i: (i,) + (0,) * len(dims))

    return pl.pallas_call(
        functools.partial(_s5_scan_kernel, nb=nb, nchunk=nchunk),
        grid=(g // gb,),
        in_specs=[pl.BlockSpec((m, LANES), lambda i: (0, i)),
                  blk(qc, qc), blk(qc, n), blk(qc, n), blk(qc, n), blk(qc, n), blk(2, n)],
        out_specs=[pl.BlockSpec((m, LANES), lambda i: (0, i)), blk(2, nb, n)],
        out_shape=[jax.ShapeDtypeStruct((m, g * S5_CH), F32), jax.ShapeDtypeStruct((g, 2, nb, n), F32)],
        scratch_shapes=[pltpu.VMEM((q, rows, LANES), BF16), pltpu.VMEM((gb, rows, qc), BF16),
                        pltpu.VMEM((gb, rows, qc), F32),
                        pltpu.VMEM((gb, nb * (nchunk + S5_SEQ_PAD), n), F32),
                        pltpu.VMEM((gb, nb * (nchunk + S5_SEQ_PAD), n), F32)],
        compiler_params=_cparams(("parallel",)),
        name="s5_scan",
    )(h, tz, ws_re, ws_im, wc_re, wc_im, aq)


def _s5_step_kernel(u_ref, hre_ref, him_ref, ab_ref, bbt_ref, cre_ref, cim_ref, y_ref, ore_ref, oim_ref):
    gb = u_ref.shape[0]
    ch = S5_CH
    for g in range(gb):
        u = u_ref[g].astype(BF16)
        bb_re = bbt_ref[g, 0:ch, :].astype(BF16)
        bb_im = bbt_ref[g, ch:2 * ch, :].astype(BF16)
        ar = ab_ref[g, 0:1, :]
        ai = ab_ref[g, 1:2, :]
        hr0 = hre_ref[g]
        hi0 = him_ref[g]
        hr = _dot(u, bb_re) + (ar * hr0 - ai * hi0)
        hi = _dot(u, bb_im) + (ar * hi0 + ai * hr0)
        ore_ref[g] = hr
        oim_ref[g] = hi
        y_ref[g] = (_dot_nt(hr.astype(BF16), cre_ref[g].astype(BF16))
                    - _dot_nt(hi.astype(BF16), cim_ref[g].astype(BF16)))


def s5_step(u_t, h_re, h_im, prep, c_re, c_im):
    ab, bbt = prep[6], prep[7]
    g, b, ch = u_t.shape
    n = S5_N
    gb = S5_GB

    def blk(*dims):
        return pl.BlockSpec((gb,) + dims, lambda i: (i,) + (0,) * len(dims))

    return pl.pallas_call(
        _s5_step_kernel,
        grid=(g // gb,),
        in_specs=[blk(b, ch), blk(b, n), blk(b, n), blk(2, n), blk(2 * ch, n), blk(ch, n), blk(ch, n)],
        out_specs=[blk(b, ch), blk(b, n), blk(b, n)],
        out_shape=[jax.ShapeDtypeStruct((g, b, ch), F32), jax.ShapeDtypeStruct((g, b, n), F32),
                   jax.ShapeDtypeStruct((g, b, n), F32)],
        compiler_params=_cparams(("parallel",)),
        name="s5_step",
    )(u_t, h_re, h_im, ab, bbt, c_re, c_im)


def _s5_head_kernel(y_ref, u_ref, d_ref, w_ref, b_ref, nw_ref, o_ref):
    y = y_ref[...] + d_ref[...] * u_ref[...]
    g = _gelu_tanh(y)
    gate = _sigmoid(_dot(g.astype(BF16), w_ref[...].astype(BF16)) + b_ref[...])
    o_ref[...] = _rms(g * gate, nw_ref[...]).astype(o_ref.dtype)


def s5_head(y, h, d, glu_w, glu_b, norm_w, *, tm):
    m, ds = y.shape
    row = lambda a: a.reshape(1, ds)
    vec = pl.BlockSpec((1, ds), lambda i: (0, 0))
    return pl.pallas_call(
        _s5_head_kernel,
        grid=(m // tm,),
        in_specs=[pl.BlockSpec((tm, ds), lambda i: (i, 0)), pl.BlockSpec((tm, ds), lambda i: (i, 0)), vec,
                  pl.BlockSpec((ds, ds), lambda i: (0, 0)), vec, vec],
        out_specs=pl.BlockSpec((tm, ds), lambda i: (i, 0)),
        out_shape=jax.ShapeDtypeStruct((m, ds), BF16),
        compiler_params=_cparams(("parallel",)),
        name="s5_head",
    )(y, h, row(d), glu_w, row(glu_b), row(norm_w))


def _pair_select(first, col0, col1, shape):
    return jnp.where(first, jnp.broadcast_to(col0, shape), jnp.broadcast_to(col1, shape))


def _ssd_chunk_kernel(*refs, d_inner, nheads, n_xparts):
    xparts = refs[:n_xparts]
    (z_ref, dt_ref, cw_ref, cb_ref, dtb_ref, alog_ref, dvec_ref, nw_ref,
     out_ref, ssm_ref, conv_ref, state_sc, xpad_sc, y_sc) = refs[n_xparts:]
    c = pl.program_id(1)
    q = M2_CHUNK
    hp = M2_HEADDIM
    ns = M2_DSTATE
    heads_per_group = nheads // M2_NGROUPS
    halo = 8

    @pl.when(c == 0)
    def _():
        state_sc[...] = jnp.zeros_like(state_sc)
        xpad_sc[0:halo, :] = jnp.zeros((halo, xpad_sc.shape[1]), F32)

    wpart = xparts[0].shape[1]
    for i, xr in enumerate(xparts):
        xpad_sc[halo:halo + q, i * wpart:(i + 1) * wpart] = xr[...]
    cw = cw_ref[...]
    conv = cb_ref[...] + cw[M2_CONV - 1:M2_CONV, :] * xpad_sc[halo:halo + q, :]
    for k in range(1, M2_CONV):
        conv = conv + cw[M2_CONV - 1 - k:M2_CONV - k, :] * xpad_sc[halo - k:halo - k + q, :]
    xpad_sc[0:halo, :] = xpad_sc[q:q + halo, :]
    xc = _silu(conv)

    dt = _softplus(dt_ref[...] + dtb_ref[...])
    a = -jnp.exp(alog_ref[...])
    da = dt * a
    row = lax.broadcasted_iota(jnp.int32, (q, q), 0)
    col = lax.broadcasted_iota(jnp.int32, (q, q), 1)
    causal = row >= col
    tri = jnp.where(causal, 1.0, 0.0).astype(BF16)
    d_hi, d_mid, d_lo = _split3(da)
    acum = _dot(tri, d_hi) + (_dot(tri, d_mid) + _dot(tri, d_lo))
    acum_t = acum.T
    alast = acum[q - 1:q, :]
    first = col < hp
    first_rows = row < hp

    for pr in range(nheads // 2):
        grp = (2 * pr) // heads_per_group
        b_bf = xc[:, d_inner + grp * ns:d_inner + (grp + 1) * ns].astype(BF16)
        c_bf = xc[:, d_inner + (M2_NGROUPS + grp) * ns:d_inner + (M2_NGROUPS + grp + 1) * ns].astype(BF16)
        cb = _dot_nt(c_bf, b_bf)
        xpair = xc[:, pr * 2 * hp:(pr + 1) * 2 * hp]
        h0, h1 = 2 * pr, 2 * pr + 1
        acol = [acum[:, h:h + 1] for h in (h0, h1)]
        m = []
        for k, h in enumerate((h0, h1)):
            seg = jnp.broadcast_to(acol[k], (q, q)) - jnp.broadcast_to(acum_t[h:h + 1, :], (q, q))
            lmat = jnp.exp(jnp.where(causal, seg, -1e30))
            m.append((cb * lmat).astype(BF16))
        dtp = _pair_select(first, dt[:, h0:h0 + 1], dt[:, h1:h1 + 1], (q, q))
        xdt = xpair * dtp
        xdt_bf = xdt.astype(BF16)
        y_diag = jnp.where(first, _dot(m[0], xdt_bf), _dot(m[1], xdt_bf))
        dec_end = _pair_select(first, jnp.exp(alast[:, h0:h0 + 1] - acol[0]),
                               jnp.exp(alast[:, h1:h1 + 1] - acol[1]), (q, q))
        xw_t = (xdt * dec_end).T.astype(BF16)
        chunk_state = _dot(xw_t, b_bf)
        rows = pl.ds(pr * 2 * hp, 2 * hp)
        prev = state_sc[rows, :]
        y_off = _dot_nt(c_bf, prev.astype(BF16)) * _pair_select(first, jnp.exp(acol[0]), jnp.exp(acol[1]), (q, q))
        sdec = jnp.where(first_rows, jnp.broadcast_to(jnp.exp(alast[:, h0:h0 + 1]), (q, q)),
                         jnp.broadcast_to(jnp.exp(alast[:, h1:h1 + 1]), (q, q)))
        state_sc[rows, :] = prev * sdec + chunk_state
        y_sc[:, pr * 2 * hp:(pr + 1) * 2 * hp] = y_diag + y_off + dvec_ref[:, pr * 2 * hp:(pr + 1) * 2 * hp] * xpair

    out_ref[...] = _rms(y_sc[...] * _silu(z_ref[...]), nw_ref[...]).astype(out_ref.dtype)

    @pl.when(c == pl.num_programs(1) - 1)
    def _():
        ssm_ref[...] = state_sc[...]
        conv_ref[...] = xpad_sc[halo + q - (M2_CONV - 1):halo + q, :]


def ssd_prompt(h, dt_raw, nb, seq, p, *, d_s5, d_inner, nheads):
    q = M2_CHUNK
    nc = seq // q
    conv_dim = d_inner + 2 * M2_NGROUPS * M2_DSTATE
    xw = 512
    xoff = d_s5 + d_inner
    assert d_s5 % d_inner == 0 and xoff % xw == 0 and conv_dim % xw == 0
    assert M2_CHUNK == 2 * M2_HEADDIM == M2_DSTATE == LANES
    zblk = d_s5 // d_inner
    n_xparts = conv_dim // xw
    m = nb * seq
    pad = lambda v: jnp.pad(v, (0, LANES - v.shape[0])).reshape(1, LANES)
    dvec = jnp.repeat(p['m2_d'], M2_HEADDIM).reshape(1, d_inner)
    vec = lambda n: pl.BlockSpec((1, n), lambda b, c: (0, 0))
    tok = lambda w, j: pl.BlockSpec((q, w), lambda b, c, j=j: (b * nc + c, j))
    out, ssm, conv = pl.pallas_call(
        functools.partial(_ssd_chunk_kernel, d_inner=d_inner, nheads=nheads, n_xparts=n_xparts),
        grid=(nb, nc),
        in_specs=[tok(xw, xoff // xw + i) for i in range(n_xparts)] + [tok(d_inner, zblk), tok(LANES, 0),
                  pl.BlockSpec((M2_CONV, conv_dim), lambda b, c: (0, 0)), vec(conv_dim), vec(LANES), vec(LANES),
                  vec(d_inner), vec(d_inner)],
        out_specs=[tok(d_inner, 0),
                   pl.BlockSpec((None, nheads * M2_HEADDIM, M2_DSTATE), lambda b, c: (b, 0, 0)),
                   pl.BlockSpec((None, M2_CONV - 1, conv_dim), lambda b, c: (b, 0, 0))],
        out_shape=[jax.ShapeDtypeStruct((m, d_inner), BF16),
                   jax.ShapeDtypeStruct((nb, nheads * M2_HEADDIM, M2_DSTATE), F32),
                   jax.ShapeDtypeStruct((nb, M2_CONV - 1, conv_dim), F32)],
        scratch_shapes=[pltpu.VMEM((nheads * M2_HEADDIM, M2_DSTATE), F32),
                        pltpu.VMEM((q + 8, conv_dim), F32),
                        pltpu.VMEM((q, d_inner), F32)],
        compiler_params=_cparams(("parallel", "arbitrary")),
        name="ssd_chunk",
    )(*([h] * n_xparts), h, dt_raw, p['m2_conv_w'], p['m2_conv_b'].reshape(1, conv_dim), pad(p['m2_dt_bias']),
      pad(p['m2_a_log']), dvec, p['m2_norm_w'].reshape(1, d_inner))
    return out, ssm.reshape(nb, nheads, M2_HEADDIM, M2_DSTATE), conv


SSD_STEP_SEQS = 8


def _ssd_step_kernel(*refs, d_inner, nheads, n_xparts):
    xparts = refs[:n_xparts]
    (z_ref, dt_ref, cs0_ref, cs1_ref, cs2_ref, cw_ref, cb_ref, dtb_ref, alog_ref, dvec_ref, nw_ref, st_ref,
     out_ref, so_ref, lhs_sc, bfull_sc, ct_sc, yt_sc, xs_sc) = refs[n_xparts:]
    i = pl.program_id(0)
    nb = z_ref.shape[0]
    ns = M2_DSTATE
    rows_g = (nheads // M2_NGROUPS) * M2_HEADDIM

    @pl.when(i == 0)
    def _():
        cw = cw_ref[...]
        xbc = jnp.concatenate([xr[...] for xr in xparts], axis=1)
        conv = (cb_ref[...] + cw[3:4, :] * xbc + cw[2:3, :] * cs2_ref[...]
                + cw[1:2, :] * cs1_ref[...] + cw[0:1, :] * cs0_ref[...])
        xc = _silu(conv)
        dt = _softplus(dt_ref[...] + dtb_ref[...])
        dec = jnp.exp(dt * (-jnp.exp(alog_ref[...])))
        hrow = lax.broadcasted_iota(jnp.int32, (LANES, d_inner), 0)
        hcol = lax.broadcasted_iota(jnp.int32, (LANES, d_inner), 1)
        expand = jnp.where(hcol // M2_HEADDIM == hrow, 1.0, 0.0).astype(BF16)

        def expand_heads(v):
            a, b_, c = _split3(v)
            return _dot(a, expand) + (_dot(b_, expand) + _dot(c, expand))

        xs = xc[:, :d_inner]
        xs_sc[...] = xs
        xdt_t = (xs * expand_heads(dt)).T
        d_hi, d_mid, d_lo = _split3(expand_heads(dec).T)
        for g in range(M2_NGROUPS):
            r = slice(g * rows_g, (g + 1) * rows_g)
            lhs_sc[g] = jnp.concatenate([xdt_t[r].astype(BF16), d_hi[r], d_mid[r], d_lo[r]], axis=1)
            b_g = xc[:, d_inner + g * ns:d_inner + (g + 1) * ns]
            bfull_sc[g] = jnp.concatenate([b_g, jnp.zeros_like(b_g)], axis=1)
            c_g = xc[:, d_inner + (M2_NGROUPS + g) * ns:d_inner + (M2_NGROUPS + g + 1) * ns]
            ct_sc[g] = c_g.T
        yt_sc[...] = jnp.zeros_like(yt_sc)

    row_id = lax.broadcasted_iota(jnp.int32, (nb, 2 * ns), 0)
    lane_id = lax.broadcasted_iota(jnp.int32, (nb, 2 * ns), 1)
    col_id = lax.broadcasted_iota(jnp.int32, (ns, nb), 1)
    for j in range(st_ref.shape[0]):
        b = i * st_ref.shape[0] + j
        r_bot = jnp.where((row_id == b) & (lane_id >= ns), 1.0, 0.0).astype(BF16)
        for g in range(M2_NGROUPS):
            r = pl.ds(g * rows_g, rows_g)
            r_top = jnp.where(row_id == b, bfull_sc[g], 0.0).astype(BF16)
            rhs = jnp.concatenate([r_top, r_bot, r_bot, r_bot], axis=0)
            o = _dot(lhs_sc[g], rhs)
            hnew = st_ref[j, r, :] * o[:, ns:] + o[:, :ns]
            so_ref[j, r, :] = hnew
            cm = jnp.where(col_id == b, ct_sc[g], 0.0).astype(BF16)
            yt_sc[r, :] += _dot(hnew.astype(BF16), cm)

    @pl.when(i == pl.num_programs(0) - 1)
    def _():
        y = yt_sc[...].T + dvec_ref[...] * xs_sc[...]
        out_ref[...] = _rms(y * _silu(z_ref[...]), nw_ref[...]).astype(out_ref.dtype)


def ssd_sample(h, dt_raw, state, conv_state, p, *, d_s5, d_inner, nheads):
    nb = h.shape[0]
    conv_dim = d_inner + 2 * M2_NGROUPS * M2_DSTATE
    xw = 512
    xoff = d_s5 + d_inner
    assert nb == LANES and M2_DSTATE == LANES and M2_CONV == 4
    assert xoff % xw == 0 and conv_dim % xw == 0 and d_s5 % d_inner == 0 and nb % SSD_STEP_SEQS == 0
    n_xparts = conv_dim // xw
    rows = nheads * M2_HEADDIM
    rows_g = rows // M2_NGROUPS
    pad = lambda v: jnp.pad(v, (0, LANES - v.shape[0])).reshape(1, LANES)
    dvec = jnp.repeat(p['m2_d'], M2_HEADDIM).reshape(1, d_inner)
    full = lambda a, b, j=0: pl.BlockSpec((a, b), lambda i, j=j: (0, j))
    st_spec = pl.BlockSpec((SSD_STEP_SEQS, rows, M2_DSTATE), lambda i: (i, 0, 0))
    out, new_state = pl.pallas_call(
        functools.partial(_ssd_step_kernel, d_inner=d_inner, nheads=nheads, n_xparts=n_xparts),
        grid=(nb // SSD_STEP_SEQS,),
        in_specs=[full(nb, xw, xoff // xw + k) for k in range(n_xparts)]
        + [full(nb, d_inner, d_s5 // d_inner), full(nb, LANES)]
        + [full(nb, conv_dim)] * 3
        + [full(M2_CONV, conv_dim), full(1, conv_dim), full(1, LANES), full(1, LANES), full(1, d_inner),
           full(1, d_inner), st_spec],
        out_specs=[full(nb, d_inner), st_spec],
        out_shape=[jax.ShapeDtypeStruct((nb, d_inner), BF16), jax.ShapeDtypeStruct((nb, rows, M2_DSTATE), F32)],
        scratch_shapes=[pltpu.VMEM((M2_NGROUPS, rows_g, 4 * nb), BF16),
                        pltpu.VMEM((M2_NGROUPS, nb, 2 * M2_DSTATE), F32),
                        pltpu.VMEM((M2_NGROUPS, M2_DSTATE, nb), F32),
                        pltpu.VMEM((rows, nb), F32),
                        pltpu.VMEM((nb, d_inner), F32)],
        compiler_params=_cparams(("arbitrary",)),
        name="ssd_step",
    )(*([h] * n_xparts), h, dt_raw, conv_state[:, 0], conv_state[:, 1], conv_state[:, 2],
      p['m2_conv_w'], p['m2_conv_b'].reshape(1, conv_dim), pad(p['m2_dt_bias']), pad(p['m2_a_log']),
      dvec, p['m2_norm_w'].reshape(1, d_inner), state.reshape(nb, rows, M2_DSTATE))
    xbc = lax.slice_in_dim(h, xoff, xoff + conv_dim, axis=1)
    new_conv = jnp.concatenate([conv_state[:, 1:], xbc[:, None, :]], axis=1)
    return out, new_state.reshape(state.shape), new_conv


def _softmax_rows(s):
    e = jnp.exp(s - jnp.max(s, axis=-1, keepdims=True))
    return e / jnp.sum(e, axis=-1, keepdims=True)


def _attn_kernel(q_ref, k_ref, v_ref, o_ref, *, scale):
    s = _dot_nt(q_ref[...].astype(BF16), k_ref[...].astype(BF16)) * scale
    o_ref[...] = _dot(_softmax_rows(s).astype(BF16), v_ref[...].astype(BF16)).astype(o_ref.dtype)


def attention_prompt(q, k, v, nb, seq, mem, heads, *, tq):
    d = q.shape[1]
    hd = d // heads
    nq = seq // tq
    kv_spec = pl.BlockSpec((mem, hd), lambda b, h, i: (b, h))
    q_spec = pl.BlockSpec((tq, hd), lambda b, h, i: (b * nq + i, h))
    return pl.pallas_call(
        functools.partial(_attn_kernel, scale=hd ** -0.5),
        grid=(nb, heads, nq),
        in_specs=[q_spec, kv_spec, kv_spec],
        out_specs=q_spec,
        out_shape=jax.ShapeDtypeStruct(q.shape, BF16),
        compiler_params=_cparams(("parallel", "parallel", "parallel")),
        name="attention_prompt",
    )(q, k, v)


ATTN_STEP_SEQS = 4


ATTN_STEP_ROWS = 64


def _attn_step_kernel(q_ref, k_ref, v_ref, o_ref, s_sc, *, scale):
    nseq, mem, heads, hd = k_ref.shape
    ch = ATTN_STEP_ROWS
    pack = 8 // heads
    for j in range(nseq):
        q = jnp.concatenate([q_ref[j]] * pack, axis=0)

        def score(c, mx):
            rows = pl.ds(pl.multiple_of(c * ch, ch), ch)
            k = k_ref[j, rows].reshape(ch // pack, pack * heads, hd)
            s = jnp.sum(k * q, axis=-1, keepdims=True) * scale
            s_sc[pl.ds(pl.multiple_of(c * (ch // pack), ch // pack), ch // pack)] = s
            return jnp.maximum(mx, jnp.max(s, axis=0))

        mx = lax.fori_loop(0, mem // ch, score, jnp.full((pack * heads, 1), NEG, F32))
        mx1 = mx[0:heads]
        for i in range(1, pack):
            mx1 = jnp.maximum(mx1, mx[i * heads:(i + 1) * heads])
        mx = jnp.concatenate([mx1] * pack, axis=0)

        def accum(c, carry):
            den, acc = carry
            rows = pl.ds(pl.multiple_of(c * ch, ch), ch)
            v = v_ref[j, rows].reshape(ch // pack, pack * heads, hd)
            e = jnp.exp(s_sc[pl.ds(pl.multiple_of(c * (ch // pack), ch // pack), ch // pack)] - mx)
            return den + jnp.sum(e, axis=0), acc + jnp.sum(e * v, axis=0)

        den, acc = lax.fori_loop(0, mem // ch, accum,
                                 (jnp.zeros((pack * heads, 1), F32), jnp.zeros((pack * heads, hd), F32)))
        den1, acc1 = den[0:heads], acc[0:heads]
        for i in range(1, pack):
            den1 = den1 + den[i * heads:(i + 1) * heads]
            acc1 = acc1 + acc[i * heads:(i + 1) * heads]
        o_ref[j] = acc1 / den1


def attention_sample(q, k_cache, v_cache):
    b, mem, heads, hd = k_cache.shape
    nseq = ATTN_STEP_SEQS
    q_spec = pl.BlockSpec((nseq, heads, hd), lambda i: (i, 0, 0))
    kv_spec = pl.BlockSpec((nseq, mem, heads, hd), lambda i: (i, 0, 0, 0))
    out = pl.pallas_call(
        functools.partial(_attn_step_kernel, scale=hd ** -0.5),
        grid=(b // nseq,),
        in_specs=[q_spec, kv_spec, kv_spec],
        out_specs=q_spec,
        out_shape=jax.ShapeDtypeStruct((b, heads, hd), F32),
        scratch_shapes=[pltpu.VMEM((mem * heads // 8, 8, 1), F32)],
        compiler_params=_cparams(("parallel",)),
        name="attention_step",
    )(q.reshape(b, heads, hd), k_cache, v_cache)
    return out.reshape(b, heads * hd)


NEG = -1e30


INFO_G1, INFO_G2, INFO_E1, INFO_E2 = 0, 1, 2, 3


def _lane_pack(lane, values):
    out = 0.0
    for k, v in values:
        out = jnp.where(lane == k, v, out)
    return out


def _router_kernel(x_ref, nw_ref, wr_ref, br_ref, sel_ref, info_ref, *, n_experts):
    xn = _rms(x_ref[...], nw_ref[...])
    x_hi, x_lo = _split2(xn)
    w_hi, w_lo = _split2(wr_ref[...])
    hi_terms = _dot(x_hi, jnp.concatenate([w_hi, w_lo], axis=1))
    logits = hi_terms[:, :LANES] + (hi_terms[:, LANES:] + _dot(x_lo, w_hi)) + br_ref[...]
    lane = lax.broadcasted_iota(jnp.int32, logits.shape, 1)
    big = jnp.int32(2 ** 30)
    is_c = (lane >= n_experts) & (lane < n_experts + MOE_GROUPS)
    lc = jnp.where(is_c, logits, NEG)
    cmax = jnp.max(lc, axis=-1, keepdims=True)
    gsel = jnp.min(jnp.where(lc == cmax, lane, big), axis=-1, keepdims=True) - n_experts
    gate_c = 1.0 / jnp.sum(jnp.where(is_c, jnp.exp(lc - cmax), 0.0), axis=-1, keepdims=True)
    in_group = (lane < n_experts) & (lane // MOE_PER_GROUP == gsel)
    lf = jnp.where(in_group, logits, NEG)
    t1 = jnp.max(lf, axis=-1, keepdims=True)
    i1 = jnp.min(jnp.where(lf == t1, lane, big), axis=-1, keepdims=True)
    lf2 = jnp.where(lane == i1, NEG, lf)
    t2 = jnp.max(lf2, axis=-1, keepdims=True)
    i2 = jnp.min(jnp.where(lf2 == t2, lane, big), axis=-1, keepdims=True)
    r = jnp.exp(t2 - t1)
    g1 = gate_c / (1.0 + r)
    g2 = gate_c * r / (1.0 + r)
    sel_ref[...] = jnp.where((lane == i1) | (lane == i2), 1.0, 0.0)
    info_ref[...] = _lane_pack(lane, ((INFO_G1, g1), (INFO_G2, g2), (INFO_E1, i1.astype(F32)),
                                      (INFO_E2, i2.astype(F32))))


def moe_router(x, norm_w, w_coarse, b_coarse, w_fine, b_fine, *, tm):
    m, d = x.shape
    e = w_fine.shape[1]
    padw = LANES - e - MOE_GROUPS
    wr = jnp.concatenate([w_fine, w_coarse, jnp.zeros((d, padw), F32)], axis=1)
    br = jnp.concatenate([b_fine, b_coarse, jnp.zeros((padw,), F32)]).reshape(1, LANES)
    tok = pl.BlockSpec((tm, LANES), lambda i: (i, 0))
    return pl.pallas_call(
        functools.partial(_router_kernel, n_experts=e),
        grid=(m // tm,),
        in_specs=[pl.BlockSpec((tm, d), lambda i: (i, 0)), pl.BlockSpec((1, d), lambda i: (0, 0)),
                  pl.BlockSpec((d, LANES), lambda i: (0, 0)), pl.BlockSpec((1, LANES), lambda i: (0, 0))],
        out_specs=[tok, tok],
        out_shape=[jax.ShapeDtypeStruct((m, LANES), F32), jax.ShapeDtypeStruct((m, LANES), F32)],
        compiler_params=_cparams(("parallel",)),
        name="moe_router",
    )(x, norm_w.reshape(1, d), wr, br)


def _moe_rank_kernel(sel_ref, info_ref, rank_ref, counts_ref, carry_sc):
    i = pl.program_id(0)

    @pl.when(i == 0)
    def _():
        carry_sc[...] = jnp.zeros_like(carry_sc)

    sel = sel_ref[...]
    tm = sel.shape[0]
    row = lax.broadcasted_iota(jnp.int32, (tm, tm), 0)
    col = lax.broadcasted_iota(jnp.int32, (tm, tm), 1)
    before = jnp.where(row > col, 1.0, 0.0).astype(BF16)
    rank = _dot(before, sel.astype(BF16)) + carry_sc[...]
    info = info_ref[...]
    lane = lax.broadcasted_iota(jnp.int32, sel.shape, 1)
    e1 = info[:, INFO_E1:INFO_E1 + 1].astype(jnp.int32)
    e2 = info[:, INFO_E2:INFO_E2 + 1].astype(jnp.int32)
    r1 = jnp.sum(jnp.where(lane == e1, rank, 0.0), axis=-1, keepdims=True)
    r2 = jnp.sum(jnp.where(lane == e2, rank, 0.0), axis=-1, keepdims=True)
    rank_ref[...] = _lane_pack(lane, ((0, r1), (1, r2)))
    carry_sc[...] += jnp.sum(sel, axis=0, keepdims=True)

    @pl.when(i == pl.num_programs(0) - 1)
    def _():
        counts_ref[...] = carry_sc[...]


def _moe_plan_kernel(rank_ref, info_ref, counts_ref, dest_ref, tiles_ref, *, tile, n_experts):
    counts = counts_ref[...]
    ntile_e = jnp.floor((counts + (tile - 1)) * (1.0 / tile))
    padded = jnp.broadcast_to(ntile_e * tile, (8, LANES))
    r = lax.broadcasted_iota(jnp.int32, (LANES, LANES), 0)
    c = lax.broadcasted_iota(jnp.int32, (LANES, LANES), 1)
    lower = jnp.where(r < c, 1.0, 0.0).astype(BF16)
    p_hi, p_mid, p_lo = _split3(padded)
    offs = (_dot(p_hi, lower) + (_dot(p_mid, lower) + _dot(p_lo, lower)))[0:1, :]
    ends = offs + padded[0:1, :]

    info = info_ref[...]
    rank = rank_ref[...]
    lane = lax.broadcasted_iota(jnp.int32, info.shape, 1)
    e1 = info[:, INFO_E1:INFO_E1 + 1].astype(jnp.int32)
    e2 = info[:, INFO_E2:INFO_E2 + 1].astype(jnp.int32)
    d1 = jnp.sum(jnp.where(lane == e1, offs, 0.0), axis=-1, keepdims=True) + rank[:, 0:1]
    d2 = jnp.sum(jnp.where(lane == e2, offs, 0.0), axis=-1, keepdims=True) + rank[:, 1:2]
    dest_ref[...] = _lane_pack(lane, ((0, d1), (1, d2))).astype(jnp.int32)

    ends_col = jnp.broadcast_to(ends, (LANES, LANES)).T
    start = (c * tile).astype(F32)
    n_before = jnp.sum(jnp.where((ends_col <= start) & (r < n_experts), 1.0, 0.0), axis=0, keepdims=True)
    lane1 = lax.broadcasted_iota(jnp.int32, (1, LANES), 1)
    total = jnp.sum(jnp.where(lane1 == n_experts - 1, ends, 0.0), axis=-1, keepdims=True)
    tail = jnp.where(counts > 0.0, ends - tile, -1.0)
    tiles_ref[...] = jnp.zeros_like(tiles_ref)
    tiles_ref[0:1, :] = jnp.minimum(n_before, n_experts - 1.0).astype(jnp.int32)
    tiles_ref[1:2, :] = jnp.broadcast_to(total * (1.0 / tile), (1, LANES)).astype(jnp.int32)
    tiles_ref[2:3, :] = tail.astype(jnp.int32)


def _moe_dispatch_kernel(dest_ref, tails_ref, ntiles_ref, *rest, tile, n_experts, group_steps, group_offsets):
    n_groups = len(group_offsets)
    nw_ref = rest[0]
    x_refs = rest[1:1 + n_groups]
    xs_ref, zero_sc, xn_sc, sems = rest[1 + n_groups:]
    i = pl.program_id(0)
    n_tiles_max = xs_ref.shape[0] // tile

    def fill_copy(j):
        return pltpu.make_async_copy(zero_sc, xs_ref.at[pl.ds(pl.multiple_of(j * tile, tile), tile)], sems.at[2])

    def tail_copy(e):
        return pltpu.make_async_copy(zero_sc, xs_ref.at[pl.ds(pl.multiple_of(tails_ref[e], tile), tile)],
                                     sems.at[1])

    @pl.when(i == 0)
    def _():
        zero_sc[...] = jnp.zeros_like(zero_sc)

        def fill(j, _):
            fill_copy(j).start()
            return 0

        def clear(e, _):
            @pl.when(tails_ref[e] >= 0)
            def _():
                tail_copy(e).start()
            return 0

        def clear_wait(e, _):
            @pl.when(tails_ref[e] >= 0)
            def _():
                tail_copy(e).wait()
            return 0

        lax.fori_loop(ntiles_ref[0], n_tiles_max, fill, 0)
        lax.fori_loop(0, n_experts, clear, 0)
        lax.fori_loop(0, n_experts, clear_wait, 0)

    for g, x_ref in enumerate(x_refs):
        @pl.when((i >= group_steps[g]) & (i < group_steps[g + 1]))
        def _(g=g, x_ref=x_ref):
            tm = x_ref.shape[0]
            xn_ref = xn_sc.at[pl.ds(0, tm)]
            xn_ref[...] = _rms(x_ref[...], nw_ref[...])
            base = 2 * (group_offsets[g] + (i - group_steps[g]) * tm)
            for r in range(tm):
                for k in range(2):
                    pltpu.make_async_copy(xn_ref.at[pl.ds(r, 1)], xs_ref.at[pl.ds(dest_ref[base + 2 * r + k], 1)],
                                          sems.at[0]).start(priority=k)
            for _ in range(2):
                pltpu.make_async_copy(xn_ref, xs_ref.at[pl.ds(0, tm)], sems.at[0]).wait()

    @pl.when(i == pl.num_programs(0) - 1)
    def _():
        def fill_wait(j, _):
            fill_copy(j).wait()
            return 0

        lax.fori_loop(ntiles_ref[0], n_tiles_max, fill_wait, 0)


def _moe_expert_kernel(texp_ref, ntiles_ref, xs_ref, wg_ref, wu_ref, wd_ref, ys_ref):
    @pl.when(pl.program_id(0) < ntiles_ref[0])
    def _():
        x = xs_ref[...].astype(BF16)
        hg = _dot(x, wg_ref[...].astype(BF16))
        hu = _dot(x, wu_ref[...].astype(BF16))
        ys_ref[...] = _dot((_silu(hg) * hu).astype(BF16), wd_ref[...].astype(BF16))


def _moe_combine_kernel(dest_ref, ys_ref, x_ref, info_ref, fw_ref, y_ref, buf_sc, sems):
    s = pl.program_id(0)
    n_blocks = pl.num_programs(0) - 1
    tm = x_ref.shape[0]
    slot = s % 2

    @pl.when(s < n_blocks)
    def _():
        for r in range(tm):
            for k in range(2):
                pltpu.make_async_copy(ys_ref.at[pl.ds(dest_ref[2 * (s * tm + r) + k], 1)],
                                      buf_sc.at[slot, k, pl.ds(r, 1)], sems.at[slot]).start(priority=k)

    @pl.when(s > 0)
    def _():
        prev = 1 - slot
        for k in range(2):
            pltpu.make_async_copy(ys_ref.at[pl.ds(0, tm)], buf_sc.at[prev, k], sems.at[prev]).wait()
        info = info_ref[...]
        y = (x_ref[...] + info[:, INFO_G1:INFO_G1 + 1] * buf_sc[prev, 0]
             + info[:, INFO_G2:INFO_G2 + 1] * buf_sc[prev, 1])
        y_ref[...] = _rms(y, fw_ref[...])


MOE_TILE = 256
MOE_TOKENS_PER_STEP = 256


def _largest_tile(m, cap):
    return max(t for t in range(8, cap + 1, 8) if m % t == 0)


def moe_routed_final(x_list, p, final_w):
    d = x_list[0].shape[1]
    ne, _, f = p['moe_w_gate'].shape
    sizes = [x.shape[0] for x in x_list]
    m = sum(sizes)
    tile = min(MOE_TILE, m)
    routed = [moe_router(x, p['norm_ffn_w'], p['router_coarse_w'], p['router_coarse_b'],
                         p['router_fine_w'], p['router_fine_b'], tm=min(x.shape[0], 512)) for x in x_list]
    sel = jnp.concatenate([r[0] for r in routed], axis=0)
    info = jnp.concatenate([r[1] for r in routed], axis=0)
    tm = _largest_tile(m, 1024)
    tok = pl.BlockSpec((tm, LANES), lambda i: (i, 0))
    rank, counts = pl.pallas_call(
        _moe_rank_kernel,
        grid=(m // tm,),
        in_specs=[tok, tok],
        out_specs=[tok, pl.BlockSpec((1, LANES), lambda i: (0, 0))],
        out_shape=[jax.ShapeDtypeStruct((m, LANES), F32), jax.ShapeDtypeStruct((1, LANES), F32)],
        scratch_shapes=[pltpu.VMEM((1, LANES), F32)],
        compiler_params=_cparams(("arbitrary",)),
        name="moe_rank",
    )(sel, info)
    n_tiles_max = (2 * m) // tile + ne
    assert n_tiles_max <= LANES
    dest, tiles = pl.pallas_call(
        functools.partial(_moe_plan_kernel, tile=tile, n_experts=ne),
        out_shape=[jax.ShapeDtypeStruct((m, LANES), jnp.int32), jax.ShapeDtypeStruct((8, LANES), jnp.int32)],
        compiler_params=pltpu.CompilerParams(vmem_limit_bytes=VMEM_LIMIT),
        name="moe_plan",
    )(rank, info, counts)
    dest_flat = dest[:, :2].reshape(2 * m)
    tile_expert = tiles[0, :n_tiles_max]
    n_tiles = tiles[1, :1]
    tails = tiles[2, :ne]

    rows = n_tiles_max * tile

    tds = [min(mg, 2 * MOE_TOKENS_PER_STEP) for mg in sizes]
    group_steps = [0]
    for mg, td in zip(sizes, tds):
        group_steps.append(group_steps[-1] + mg // td)
    group_offsets = [sum(sizes[:g]) for g in range(len(sizes))]

    def group_spec(g):
        first, last = group_steps[g], group_steps[g + 1] - 1
        return pl.BlockSpec((tds[g], d), lambda i, *_: (jnp.clip(i, first, last) - first, 0))

    xs = pl.pallas_call(
        functools.partial(_moe_dispatch_kernel, tile=tile, n_experts=ne, group_steps=tuple(group_steps),
                          group_offsets=tuple(group_offsets)),
        grid_spec=pltpu.PrefetchScalarGridSpec(
            num_scalar_prefetch=3, grid=(group_steps[-1],),
            in_specs=[pl.BlockSpec((1, d), lambda i, *_: (0, 0))] + [group_spec(g) for g in range(len(sizes))],
            out_specs=pl.BlockSpec(memory_space=pl.ANY),
            scratch_shapes=[pltpu.VMEM((tile, d), F32), pltpu.VMEM((max(tds), d), F32),
                            pltpu.SemaphoreType.DMA((3,))]),
        out_shape=jax.ShapeDtypeStruct((rows, d), F32),
        compiler_params=_cparams(("arbitrary",)),
        name="moe_dispatch",
    )(dest_flat, tails, n_tiles, p['norm_ffn_w'].reshape(1, d), *x_list)

    def tile_idx(i, te, nt):
        return jnp.minimum(i, nt[0] - 1)

    ys = pl.pallas_call(
        _moe_expert_kernel,
        grid_spec=pltpu.PrefetchScalarGridSpec(
            num_scalar_prefetch=2, grid=(n_tiles_max,),
            in_specs=[pl.BlockSpec((tile, d), lambda i, te, nt: (tile_idx(i, te, nt), 0)),
                      pl.BlockSpec((None, d, f), lambda i, te, nt: (te[tile_idx(i, te, nt)], 0, 0)),
                      pl.BlockSpec((None, d, f), lambda i, te, nt: (te[tile_idx(i, te, nt)], 0, 0)),
                      pl.BlockSpec((None, f, d), lambda i, te, nt: (te[tile_idx(i, te, nt)], 0, 0))],
            out_specs=pl.BlockSpec((tile, d), lambda i, te, nt: (tile_idx(i, te, nt), 0))),
        out_shape=jax.ShapeDtypeStruct((rows, d), F32),
        input_output_aliases={2: 0},
        compiler_params=_cparams(("arbitrary",)),
        name="moe_experts",
    )(tile_expert, n_tiles, xs, p['moe_w_gate'], p['moe_w_up'], p['moe_w_down'])

    outs = []
    off = 0
    for x, (_, info_g), mg in zip(x_list, routed, sizes):
        tc = min(mg, MOE_TOKENS_PER_STEP)
        outs.append(pl.pallas_call(
            _moe_combine_kernel,
            grid_spec=pltpu.PrefetchScalarGridSpec(
                num_scalar_prefetch=1, grid=(mg // tc + 1,),
                in_specs=[pl.BlockSpec(memory_space=pl.ANY),
                          pl.BlockSpec((tc, d), lambda s, dref: (jnp.maximum(s - 1, 0), 0)),
                          pl.BlockSpec((tc, LANES), lambda s, dref: (jnp.maximum(s - 1, 0), 0)),
                          pl.BlockSpec((1, d), lambda s, dref: (0, 0))],
                out_specs=pl.BlockSpec((tc, d), lambda s, dref: (jnp.maximum(s - 1, 0), 0)),
                scratch_shapes=[pltpu.VMEM((2, 2, tc, d), F32), pltpu.SemaphoreType.DMA((2,))]),
            out_shape=jax.ShapeDtypeStruct((mg, d), F32),
            compiler_params=_cparams(("arbitrary",)),
            name="moe_combine",
        )(dest_flat[2 * off:2 * (off + mg)], ys, x, info_g, final_w.reshape(1, d)))
        off += mg
    return outs


def s5_prompt(h, nb, seq, prep, p):
    y, hfin = s5_scan(h, prep, nb, seq)
    out = s5_head(y, h, p['s5_d'], p['s5_glu_w'], p['s5_glu_b'], p['s5_norm_w'], tm=min(512, nb * seq))
    return out, hfin[:, 0].transpose(1, 0, 2), hfin[:, 1].transpose(1, 0, 2)


def s5_sample(h, st_re, st_im, prep, p):
    g = prep[0].shape[0]
    ch = S5_CH
    ds = g * ch
    b = h.shape[0]
    u_t = h[:, :ds].reshape(b, g, ch).transpose(1, 0, 2)
    y_t, n_re, n_im = s5_step(u_t, st_re.transpose(1, 0, 2), st_im.transpose(1, 0, 2), prep,
                              p['s5_c_re'], p['s5_c_im'])
    y = y_t.transpose(1, 0, 2).reshape(b, ds)
    out = s5_head(y, h, p['s5_d'], p['s5_glu_w'], p['s5_glu_b'], p['s5_norm_w'], tm=b)
    return out, n_re.transpose(1, 0, 2), n_im.transpose(1, 0, 2)


def _row_tile(m):
    return min(m, 1024)


def _col_tile(m, n):
    if m <= 256:
        return n
    return max(t for t in range(256, 1793, 256) if n % t == 0)


def _mixer_and_attention(x, p, s5_prep, *, nb, seq, mem_kv, xa_heads, states):
    m, d = x.shape
    g, n = p['s5_a_re'].shape
    d_s5 = g * S5_CH
    nheads = p['m2_a_log'].shape[0]
    d_inner = nheads * M2_HEADDIM
    conv_dim = d_inner + 2 * M2_NGROUPS * M2_DSTATE
    n_main = d_s5 + d_inner + conv_dim
    tm = _row_tile(m)

    w_in = p['w_in']
    w_dt = jnp.pad(w_in[:, n_main:], ((0, 0), (0, LANES - nheads)))
    tn = _col_tile(m, d)
    h, dt_raw = fused_matmul([x], w_in, n_out=n_main, gain=p['norm_mix_w'], side_w=w_dt, tm=tm,
                             tn=_col_tile(m, n_main))

    if states is None:
        s5_out, s5_re, s5_im = s5_prompt(h, nb, seq, s5_prep, p)
        m2_out, ssm, conv = ssd_prompt(h, dt_raw, nb, seq, p, d_s5=d_s5, d_inner=d_inner, nheads=nheads)
    else:
        s5_out, s5_re, s5_im = s5_sample(h, states[0], states[1], s5_prep, p)
        m2_out, ssm, conv = ssd_sample(h, dt_raw, states[2], states[3], p, d_s5=d_s5, d_inner=d_inner,
                                       nheads=nheads)
    x1 = fused_matmul([s5_out, m2_out], p['w_out'], n_out=d, res=x, tm=tm, tn=tn)

    q = fused_matmul([x1], p['xa_wq'], n_out=d, gain=p['norm_xa_w'], tm=tm, tn=tn,
                     out_dtype=BF16 if states is None else F32)
    if states is None:
        mem = mem_kv[0].shape[0] // nb
        o = attention_prompt(q, mem_kv[0], mem_kv[1], nb, seq, mem, xa_heads, tq=min(seq, 2048))
    else:
        o = attention_sample(q, mem_kv[0], mem_kv[1])
    x2 = fused_matmul([o], p['xa_wo'], n_out=d, res=x1, tm=tm, tn=tn)

    return x2, s5_re, s5_im, ssm, conv


def kernel(x_prompt, x_sample, mem_prompt, state_s5_re, state_s5_im, state_ssm, state_conv, cache_mem_k, cache_mem_v, norm_mix_w, w_in, s5_a_re, s5_a_im, s5_log_dt, s5_b_re, s5_b_im, s5_c_re, s5_c_im, s5_d, s5_glu_w, s5_glu_b, s5_norm_w, m2_conv_w, m2_conv_b, m2_dt_bias, m2_a_log, m2_d, m2_norm_w, w_out, norm_xa_w, norm_mem_w, xa_wq, xa_wk, xa_wv, xa_wo, norm_ffn_w, router_coarse_w, router_coarse_b, router_fine_w, router_fine_b, moe_w_gate, moe_w_up, moe_w_down, norm_final_w):
    depth = w_in.shape[0]
    assert depth == 1, "the final norm is fused into the (only) layer"
    per_layer = dict(
        norm_mix_w=norm_mix_w, w_in=w_in, s5_a_re=s5_a_re, s5_a_im=s5_a_im, s5_log_dt=s5_log_dt,
        s5_b_re=s5_b_re, s5_b_im=s5_b_im, s5_c_re=s5_c_re, s5_c_im=s5_c_im, s5_d=s5_d, s5_glu_w=s5_glu_w,
        s5_glu_b=s5_glu_b, s5_norm_w=s5_norm_w, m2_conv_w=m2_conv_w, m2_conv_b=m2_conv_b, m2_dt_bias=m2_dt_bias,
        m2_a_log=m2_a_log, m2_d=m2_d, m2_norm_w=m2_norm_w, w_out=w_out, norm_xa_w=norm_xa_w,
        norm_mem_w=norm_mem_w, xa_wq=xa_wq, xa_wk=xa_wk, xa_wv=xa_wv, xa_wo=xa_wo, norm_ffn_w=norm_ffn_w,
        router_coarse_w=router_coarse_w, router_coarse_b=router_coarse_b, router_fine_w=router_fine_w,
        router_fine_b=router_fine_b, moe_w_gate=moe_w_gate, moe_w_up=moe_w_up, moe_w_down=moe_w_down)
    p = {k: v[0] for k, v in per_layer.items()}
    for name in ('w_in', 'w_out', 'xa_wq', 'xa_wo'):
        p[name] = p[name].astype(BF16)
    nb, seq, d = x_prompt.shape
    db, dseq, _ = x_sample.shape
    assert dseq == 1
    mem = mem_prompt.shape[1]
    xa_heads = cache_mem_k.shape[3]

    s5_prep = s5_prepare(p['s5_a_re'], p['s5_a_im'], p['s5_log_dt'], p['s5_b_re'], p['s5_b_im'],
                         p['s5_c_re'], p['s5_c_im'])

    memx = mem_prompt.reshape(nb * mem, d)
    mk = fused_matmul([memx], p['xa_wk'], n_out=d, gain=p['norm_mem_w'], tm=_row_tile(nb * mem),
                      tn=_col_tile(nb * mem, d))
    mv = fused_matmul([memx], p['xa_wv'], n_out=d, gain=p['norm_mem_w'], tm=_row_tile(nb * mem),
                      tn=_col_tile(nb * mem, d))
    xp, p_re, p_im, p_ssm, p_conv = _mixer_and_attention(
        x_prompt.reshape(nb * seq, d), p, s5_prep, nb=nb, seq=seq, mem_kv=(mk, mv), xa_heads=xa_heads, states=None)

    xs, s_re, s_im, s_ssm, s_conv = _mixer_and_attention(
        x_sample.reshape(db, d), p, s5_prep, nb=db, seq=1,
        mem_kv=(cache_mem_k[0], cache_mem_v[0]), xa_heads=xa_heads,
        states=(state_s5_re[0], state_s5_im[0], state_ssm[0], state_conv[0]))

    yp, ys = moe_routed_final([xp, xs], p, norm_final_w)

    kv_shape = (1, nb, mem) + cache_mem_k.shape[3:]
    return (yp.reshape(nb, seq, d), ys.reshape(db, 1, d), p_re[None], p_im[None], p_ssm[None], p_conv[None],
            mk.reshape(kv_shape), mv.reshape(kv_shape), s_re[None], s_im[None], s_ssm[None], s_conv[None])
```

```python
import functools
import math

import jax
import jax.numpy as jnp
from jax import lax
from jax.experimental import pallas as pl
from jax.experimental.pallas import tpu as pltpu

F32 = jnp.float32
BF16 = jnp.bfloat16
RMS_EPS = 1e-6

V7X_VMEM_BYTES = 64 * 1024 * 1024
VMEM_LIMIT = V7X_VMEM_BYTES - 8 * 1024 * 1024
LANES = 128

S5_CH = 16
S5_N = 64
S5_Q = 16
S5_GB = 8
S5_SEQ_PAD = 8
M2_HEADDIM = 64
M2_DSTATE = 128
M2_NGROUPS = 2
M2_CONV = 4
M2_CHUNK = 128
MOE_GROUPS = 4
MOE_PER_GROUP = 8


def _cparams(sem):
    return pltpu.CompilerParams(dimension_semantics=sem, vmem_limit_bytes=VMEM_LIMIT)


def _rms(x, w):
    return x * lax.rsqrt(jnp.mean(x * x, axis=-1, keepdims=True) + RMS_EPS) * w


def _sigmoid(x):
    return 1.0 / (1.0 + jnp.exp(-x))


def _silu(x):
    return x * _sigmoid(x)


def _softplus(x):
    return jnp.maximum(x, 0.0) + jnp.log1p(jnp.exp(-jnp.abs(x)))


def _gelu_tanh(x):
    return 0.5 * x * (1.0 + jnp.tanh(math.sqrt(2.0 / math.pi) * (x + 0.044715 * (x * x * x))))


def _dot(a, b):
    return jnp.dot(a, b, preferred_element_type=F32)


def _dot_nt(a, b):
    return lax.dot_general(a, b, (((1,), (1,)), ((), ())), preferred_element_type=F32)


def _split3(x):
    hi = x.astype(BF16)
    r = x - hi.astype(F32)
    mid = r.astype(BF16)
    lo = (r - mid.astype(F32)).astype(BF16)
    return hi, mid, lo


def _split2(x):
    hi = x.astype(BF16)
    lo = (x - hi.astype(F32)).astype(BF16)
    return hi, lo


def _mm_kernel(*refs, n_lhs, has_gain, has_res, has_side, staged):
    it = iter(refs)
    lhs = [next(it) for _ in range(n_lhs)]
    gain = next(it) if has_gain else None
    ws = [next(it) for _ in range(n_lhs)]
    side_w = next(it) if has_side else None
    res = next(it) if has_res else None
    out = next(it)
    side_out = next(it) if has_side else None
    lhs_bf = next(it) if staged else lhs

    if staged:
        @pl.when(pl.program_id(1) == 0)
        def _():
            for i in range(n_lhs):
                x = lhs[i][...]
                if has_gain:
                    x = _rms(x, gain[...])
                lhs_bf[i] = x.astype(BF16)
            if has_side:
                side_out[...] = _dot(lhs_bf[0], side_w[...].astype(BF16))

    acc = None
    for i in range(n_lhs):
        p = _dot(lhs_bf[i][...], ws[i][...].astype(BF16))
        acc = p if acc is None else acc + p
    if has_res:
        acc = acc + res[...]
    out[...] = acc.astype(out.dtype)


def fused_matmul(lhs_list, w, *, n_out, gain=None, res=None, side_w=None, out_dtype=F32, tm, tn):
    n_lhs = len(lhs_list)
    m, kp = lhs_list[0].shape
    assert all(a.shape == (m, kp) for a in lhs_list)
    assert w.shape[0] == n_lhs * kp and m % tm == 0 and n_out % tn == 0
    assert gain is None or n_lhs == 1
    staged = gain is not None or any(a.dtype != BF16 for a in lhs_list)
    assert staged or side_w is None
    grid = (m // tm, n_out // tn)
    in_specs = [pl.BlockSpec((tm, kp), lambda i, j: (i, 0)) for _ in range(n_lhs)]
    args = list(lhs_list)
    if gain is not None:
        in_specs.append(pl.BlockSpec((1, kp), lambda i, j: (0, 0)))
        args.append(gain.reshape(1, kp))
    for p in range(n_lhs):
        in_specs.append(pl.BlockSpec((kp, tn), lambda i, j, p=p: (p, j)))
        args.append(w)
    if side_w is not None:
        in_specs.append(pl.BlockSpec((kp, LANES), lambda i, j: (0, 0)))
        args.append(side_w)
    if res is not None:
        in_specs.append(pl.BlockSpec((tm, tn), lambda i, j: (i, j)))
        args.append(res)
    out_shape = [jax.ShapeDtypeStruct((m, n_out), out_dtype)]
    out_specs = [pl.BlockSpec((tm, tn), lambda i, j: (i, j))]
    if side_w is not None:
        out_shape.append(jax.ShapeDtypeStruct((m, LANES), F32))
        out_specs.append(pl.BlockSpec((tm, LANES), lambda i, j: (i, 0)))
    outs = pl.pallas_call(
        functools.partial(_mm_kernel, n_lhs=n_lhs, has_gain=gain is not None,
                          has_res=res is not None, has_side=side_w is not None, staged=staged),
        grid=grid, in_specs=in_specs, out_specs=out_specs, out_shape=out_shape,
        scratch_shapes=[pltpu.VMEM((n_lhs, tm, kp), BF16)] if staged else [],
        compiler_params=_cparams(("parallel", "arbitrary")),
        name="fused_matmul",
    )(*args)
    return outs if side_w is not None else outs[0]


def _s5_prep_kernel(*refs):
    for g in range(refs[0].shape[0]):
        _s5_prep_group(*[r.at[g] for r in refs])


def _s5_prep_group(lre_ref, lim_ref, ldt_ref, btre_ref, btim_ref, cre_ref, cim_ref,
                   tz_ref, wsre_ref, wsim_ref, wcre_ref, wcim_ref, aq_ref, ab_ref, bbt_ref):
    q, ch = S5_Q, S5_CH
    lr = lre_ref[...]
    li = lim_ref[...]
    step = jnp.exp(ldt_ref[...])
    mag = jnp.exp(lr * step)
    ab_re = mag * jnp.cos(li * step)
    ab_im = mag * jnp.sin(li * step)
    den = lr * lr + li * li
    num_re = ab_re - 1.0
    coef_re = (num_re * lr + ab_im * li) / den
    coef_im = (ab_im * lr - num_re * li) / den
    bt_re = btre_ref[...]
    bt_im = btim_ref[...]
    bb_re = coef_re * bt_re - coef_im * bt_im
    bb_im = coef_re * bt_im + coef_im * bt_re
    c_re = cre_ref[...]
    c_im = cim_ref[...]

    pw = [(jnp.ones_like(ab_re), jnp.zeros_like(ab_re))]
    for _ in range(q):
        pr, pi = pw[-1]
        pw.append((pr * ab_re - pi * ab_im, pr * ab_im + pi * ab_re))

    ca_re = [c_re * pr - c_im * pi for pr, pi in pw]
    ca_im = [c_re * pi + c_im * pr for pr, pi in pw]
    wcre_ref[...] = jnp.concatenate(ca_re[1:], axis=0).astype(BF16)
    wcim_ref[...] = jnp.concatenate([-x for x in ca_im[1:]], axis=0).astype(BF16)

    pr_stack = jnp.concatenate(ca_re[:q], axis=0)
    pi_stack = jnp.concatenate(ca_im[:q], axis=0)
    krow = None
    for a, b, sign in ((bb_re, pr_stack, 1.0), (bb_im, pi_stack, -1.0)):
        a_hi, a_lo = _split2(a)
        b_hi, b_lo = _split2(b)
        t = _dot_nt(a_hi, b_hi) + (_dot_nt(a_hi, b_lo) + _dot_nt(a_lo, b_hi))
        krow = sign * t if krow is None else krow + sign * t
    lane = lax.broadcasted_iota(jnp.int32, krow.shape, 1)
    blocks = [krow]
    for s in range(1, q):
        blocks.append(jnp.where(lane >= s * ch, pltpu.roll(krow, s * ch, 1), 0.0))
    tz_ref[...] = jnp.concatenate(blocks, axis=0).astype(BF16)

    ws_re, ws_im = [], []
    for s in range(q):
        pr, pi = pw[q - 1 - s]
        ws_re.append(bb_re * pr - bb_im * pi)
        ws_im.append(bb_re * pi + bb_im * pr)
    wsre_ref[...] = jnp.concatenate(ws_re, axis=0).astype(BF16)
    wsim_ref[...] = jnp.concatenate(ws_im, axis=0).astype(BF16)

    aq_ref[0:1, :] = pw[q][0]
    aq_ref[1:2, :] = pw[q][1]
    ab_ref[0:1, :] = ab_re
    ab_ref[1:2, :] = ab_im
    bbt_ref[0:ch, :] = bb_re
    bbt_ref[ch:2 * ch, :] = bb_im


def s5_prepare(a_re, a_im, log_dt, b_re, b_im, c_re, c_im):
    g, n = a_re.shape
    ch, q = S5_CH, S5_Q
    qc = q * ch
    bt_re = jnp.swapaxes(b_re, 1, 2)
    bt_im = jnp.swapaxes(b_im, 1, 2)

    def per_g(*dims):
        return pl.BlockSpec((S5_GB,) + dims, lambda i: (i,) + (0,) * len(dims))

    return pl.pallas_call(
        _s5_prep_kernel,
        grid=(g // S5_GB,),
        in_specs=[per_g(1, n), per_g(1, n), per_g(1, 1), per_g(ch, n), per_g(ch, n), per_g(ch, n), per_g(ch, n)],
        out_specs=[per_g(qc, qc), per_g(qc, n), per_g(qc, n), per_g(qc, n), per_g(qc, n),
                   per_g(2, n), per_g(2, n), per_g(2 * ch, n)],
        out_shape=[jax.ShapeDtypeStruct((g, qc, qc), BF16),
                   jax.ShapeDtypeStruct((g, qc, n), BF16), jax.ShapeDtypeStruct((g, qc, n), BF16),
                   jax.ShapeDtypeStruct((g, qc, n), BF16), jax.ShapeDtypeStruct((g, qc, n), BF16),
                   jax.ShapeDtypeStruct((g, 2, n), F32), jax.ShapeDtypeStruct((g, 2, n), F32),
                   jax.ShapeDtypeStruct((g, 2 * ch, n), F32)],
        compiler_params=_cparams(("parallel",)),
        name="s5_prepare",
    )(a_re.reshape(g, 1, n), a_im.reshape(g, 1, n), log_dt.reshape(g, 1, 1), bt_re, bt_im, c_re, c_im)


def _s5_scan_kernel(h_ref, tz_ref, wsre_ref, wsim_ref, wcre_ref, wcim_ref, aq_ref,
                    y_ref, hfin_ref, xs_sc, u_sc, yg_sc, sre_sc, sim_sc, *, nb, nchunk):
    gb, q, ch = S5_GB, S5_Q, S5_CH
    rows = nb * nchunk
    seq_stride = nchunk + S5_SEQ_PAD
    per_vreg = LANES // ch
    assert gb == per_vreg and q % per_vreg == 0
    slot = lax.broadcasted_iota(jnp.int32, (rows, LANES), 1) // ch

    halves = q // per_vreg

    def rot_rows(w, g):
        if g == 0:
            return w
        cut = (per_vreg - g) * ch
        parts = []
        for hf in range(halves):
            blk = w[hf * LANES:(hf + 1) * LANES]
            parts += [blk[cut:], blk[:cut]]
        return jnp.concatenate(parts, axis=0)

    for s in range(q):
        x = h_ref[pl.ds(s, rows, stride=q), :].astype(BF16)
        k = s % per_vreg
        xs_sc[s] = pltpu.roll(x, k * ch, 1) if k else x
    keep = [jnp.where(slot == j, 1.0, 0.0).astype(BF16) for j in range(per_vreg)]
    for g in range(gb):
        for hf in range(halves):
            acc = None
            for k in range(per_vreg):
                piece = xs_sc[hf * per_vreg + k] * keep[(g + k) % per_vreg]
                acc = piece if acc is None else acc + piece
            u_sc[g, :, hf * LANES:(hf + 1) * LANES] = acc

    for g in range(gb):
        u = u_sc[g]
        tz = rot_rows(tz_ref[g], g)
        tz = jnp.concatenate([pltpu.roll(tz[:, hf * LANES:(hf + 1) * LANES], g * ch, 1) if g
                              else tz[:, hf * LANES:(hf + 1) * LANES] for hf in range(halves)], axis=1)
        yg_sc[g] = _dot(u, tz)
        for sc, w_ref in ((sre_sc, wsre_ref), (sim_sc, wsim_ref)):
            s_all = _dot(u, rot_rows(w_ref[g], g))
            for b in range(nb):
                sc[g, b * seq_stride:b * seq_stride + nchunk, :] = s_all[b * nchunk:(b + 1) * nchunk]

    ar = [jnp.broadcast_to(aq_ref[g, 0:1, :], (nb, S5_N)) for g in range(gb)]
    ai = [jnp.broadcast_to(aq_ref[g, 1:2, :], (nb, S5_N)) for g in range(gb)]

    def step(c, carry):
        at = pl.ds(c, nb, stride=seq_stride)
        new = []
        for g in range(gb):
            hr, hi = carry[g]
            sr = sre_sc[g, at, :]
            si = sim_sc[g, at, :]
            sre_sc[g, at, :] = hr
            sim_sc[g, at, :] = hi
            new.append((ar[g] * hr - ai[g] * hi + sr, ar[g] * hi + ai[g] * hr + si))
        return tuple(new)

    zero = jnp.zeros((nb, S5_N), F32)
    fin = lax.fori_loop(0, nchunk, step, tuple((zero, zero) for _ in range(gb)), unroll=4)
    for g in range(gb):
        hfin_ref[g, 0] = fin[g][0]
        hfin_ref[g, 1] = fin[g][1]
        h_in = [jnp.concatenate([sc[g, b * seq_stride:b * seq_stride + nchunk, :] for b in range(nb)], axis=0)
                for sc in (sre_sc, sim_sc)]
        yg_sc[g] += (_dot_nt(h_in[0].astype(BF16), rot_rows(wcre_ref[g], g))
                     + _dot_nt(h_in[1].astype(BF16), rot_rows(wcim_ref[g], g)))

    for t in range(q):
        hf, k = divmod(t, per_vreg)
        acc = jnp.zeros((rows, LANES), F32)
        for g in range(gb):
            acc = jnp.where(slot == (k + g) % per_vreg, yg_sc[g, :, hf * LANES:(hf + 1) * LANES], acc)
        y_ref[pl.ds(t, rows, stride=q), :] = pltpu.roll(acc, (per_vreg - k) * ch, 1) if k else acc


def s5_scan(h, prep, nb, seq):
    tz, ws_re, ws_im, wc_re, wc_im, aq = prep[:6]
    g, qc, _ = tz.shape
    n = S5_N
    gb = S5_GB
    q = S5_Q
    assert gb * S5_CH == LANES and qc == q * S5_CH and seq % q == 0
    nchunk = seq // q
    rows = nb * nchunk
    m = nb * seq

    def blk(*dims):
        return pl.BlockSpec((gb,) + dims, lambda i: (i,) + (0,) * len(dims))

    return pl.pallas_call(
        functools.partial(_s5_scan_kernel, nb=nb, nchunk=nchunk),
        grid=(g // gb,),
        in_specs=[pl.BlockSpec((m, LANES), lambda i: (0, i)),
                  blk(qc, qc), blk(qc, n), blk(qc, n), blk(qc, n), blk(qc, n), blk(2, n)],
        out_specs=[pl.BlockSpec((m, LANES), lambda i: (0, i)), blk(2, nb, n)],
        out_shape=[jax.ShapeDtypeStruct((m, g * S5_CH), F32), jax.ShapeDtypeStruct((g, 2, nb, n), F32)],
        scratch_shapes=[pltpu.VMEM((q, rows, LANES), BF16), pltpu.VMEM((gb, rows, qc), BF16),
                        pltpu.VMEM((gb, rows, qc), F32),
                        pltpu.VMEM((gb, nb * (nchunk + S5_SEQ_PAD), n), F32),
                        pltpu.VMEM((gb, nb * (nchunk + S5_SEQ_PAD), n), F32)],
        compiler_params=_cparams(("parallel",)),
        name="s5_scan",
    )(h, tz, ws_re, ws_im, wc_re, wc_im, aq)


def _s5_step_kernel(u_ref, hre_ref, him_ref, ab_ref, bbt_ref, cre_ref, cim_ref, y_ref, ore_ref, oim_ref):
    gb = u_ref.shape[0]
    ch = S5_CH
    for g in range(gb):
        u = u_ref[g].astype(BF16)
        bb_re = bbt_ref[g, 0:ch, :].astype(BF16)
        bb_im = bbt_ref[g, ch:2 * ch, :].astype(BF16)
        ar = ab_ref[g, 0:1, :]
        ai = ab_ref[g, 1:2, :]
        hr0 = hre_ref[g]
        hi0 = him_ref[g]
        hr = _dot(u, bb_re) + (ar * hr0 - ai * hi0)
        hi = _dot(u, bb_im) + (ar * hi0 + ai * hr0)
        ore_ref[g] = hr
        oim_ref[g] = hi
        y_ref[g] = (_dot_nt(hr.astype(BF16), cre_ref[g].astype(BF16))
                    - _dot_nt(hi.astype(BF16), cim_ref[g].astype(BF16)))


def s5_step(u_t, h_re, h_im, prep, c_re, c_im):
    ab, bbt = prep[6], prep[7]
    g, b, ch = u_t.shape
    n = S5_N
    gb = S5_GB

    def blk(*dims):
        return pl.BlockSpec((gb,) + dims, lambda i: (i,) + (0,) * len(dims))

    return pl.pallas_call(
        _s5_step_kernel,
        grid=(g // gb,),
        in_specs=[blk(b, ch), blk(b, n), blk(b, n), blk(2, n), blk(2 * ch, n), blk(ch, n), blk(ch, n)],
        out_specs=[blk(b, ch), blk(b, n), blk(b, n)],
        out_shape=[jax.ShapeDtypeStruct((g, b, ch), F32), jax.ShapeDtypeStruct((g, b, n), F32),
                   jax.ShapeDtypeStruct((g, b, n), F32)],
        compiler_params=_cparams(("parallel",)),
        name="s5_step",
    )(u_t, h_re, h_im, ab, bbt, c_re, c_im)


def _s5_head_kernel(y_ref, u_ref, d_ref, w_ref, b_ref, nw_ref, o_ref):
    y = y_ref[...] + d_ref[...] * u_ref[...]
    g = _gelu_tanh(y)
    gate = _sigmoid(_dot(g.astype(BF16), w_ref[...].astype(BF16)) + b_ref[...])
    o_ref[...] = _rms(g * gate, nw_ref[...]).astype(o_ref.dtype)


def s5_head(y, h, d, glu_w, glu_b, norm_w, *, tm):
    m, ds = y.shape
    row = lambda a: a.reshape(1, ds)
    vec = pl.BlockSpec((1, ds), lambda i: (0, 0))
    return pl.pallas_call(
        _s5_head_kernel,
        grid=(m // tm,),
        in_specs=[pl.BlockSpec((tm, ds), lambda i: (i, 0)), pl.BlockSpec((tm, ds), lambda i: (i, 0)), vec,
                  pl.BlockSpec((ds, ds), lambda i: (0, 0)), vec, vec],
        out_specs=pl.BlockSpec((tm, ds), lambda i: (i, 0)),
        out_shape=jax.ShapeDtypeStruct((m, ds), BF16),
        compiler_params=_cparams(("parallel",)),
        name="s5_head",
    )(y, h, row(d), glu_w, row(glu_b), row(norm_w))


def _pair_select(first, col0, col1, shape):
    return jnp.where(first, jnp.broadcast_to(col0, shape), jnp.broadcast_to(col1, shape))


def _ssd_chunk_kernel(*refs, d_inner, nheads, n_xparts):
    xparts = refs[:n_xparts]
    (z_ref, dt_ref, cw_ref, cb_ref, dtb_ref, alog_ref, dvec_ref, nw_ref,
     out_ref, ssm_ref, conv_ref, state_sc, xpad_sc, y_sc) = refs[n_xparts:]
    c = pl.program_id(1)
    q = M2_CHUNK
    hp = M2_HEADDIM
    ns = M2_DSTATE
    heads_per_group = nheads // M2_NGROUPS
    halo = 8

    @pl.when(c == 0)
    def _():
        state_sc[...] = jnp.zeros_like(state_sc)
        xpad_sc[0:halo, :] = jnp.zeros((halo, xpad_sc.shape[1]), F32)

    wpart = xparts[0].shape[1]
    for i, xr in enumerate(xparts):
        xpad_sc[halo:halo + q, i * wpart:(i + 1) * wpart] = xr[...]
    cw = cw_ref[...]
    conv = cb_ref[...] + cw[M2_CONV - 1:M2_CONV, :] * xpad_sc[halo:halo + q, :]
    for k in range(1, M2_CONV):
        conv = conv + cw[M2_CONV - 1 - k:M2_CONV - k, :] * xpad_sc[halo - k:halo - k + q, :]
    xpad_sc[0:halo, :] = xpad_sc[q:q + halo, :]
    xc = _silu(conv)

    dt = _softplus(dt_ref[...] + dtb_ref[...])
    a = -jnp.exp(alog_ref[...])
    da = dt * a
    row = lax.broadcasted_iota(jnp.int32, (q, q), 0)
    col = lax.broadcasted_iota(jnp.int32, (q, q), 1)
    causal = row >= col
    tri = jnp.where(causal, 1.0, 0.0).astype(BF16)
    d_hi, d_mid, d_lo = _split3(da)
    acum = _dot(tri, d_hi) + (_dot(tri, d_mid) + _dot(tri, d_lo))
    acum_t = acum.T
    alast = acum[q - 1:q, :]
    first = col < hp
    first_rows = row < hp

    for pr in range(nheads // 2):
        grp = (2 * pr) // heads_per_group
        b_bf = xc[:, d_inner + grp * ns:d_inner + (grp + 1) * ns].astype(BF16)
        c_bf = xc[:, d_inner + (M2_NGROUPS + grp) * ns:d_inner + (M2_NGROUPS + grp + 1) * ns].astype(BF16)
        cb = _dot_nt(c_bf, b_bf)
        xpair = xc[:, pr * 2 * hp:(pr + 1) * 2 * hp]
        h0, h1 = 2 * pr, 2 * pr + 1
        acol = [acum[:, h:h + 1] for h in (h0, h1)]
        m = []
        for k, h in enumerate((h0, h1)):
            seg = jnp.broadcast_to(acol[k], (q, q)) - jnp.broadcast_to(acum_t[h:h + 1, :], (q, q))
            lmat = jnp.exp(jnp.where(causal, seg, -1e30))
            m.append((cb * lmat).astype(BF16))
        dtp = _pair_select(first, dt[:, h0:h0 + 1], dt[:, h1:h1 + 1], (q, q))
        xdt = xpair * dtp
        xdt_bf = xdt.astype(BF16)
        y_diag = jnp.where(first, _dot(m[0], xdt_bf), _dot(m[1], xdt_bf))
        dec_end = _pair_select(first, jnp.exp(alast[:, h0:h0 + 1] - acol[0]),
                               jnp.exp(alast[:, h1:h1 + 1] - acol[1]), (q, q))
        xw_t = (xdt * dec_end).T.astype(BF16)
        chunk_state = _dot(xw_t, b_bf)
        rows = pl.ds(pr * 2 * hp, 2 * hp)
        prev = state_sc[rows, :]
        y_off = _dot_nt(c_bf, prev.astype(BF16)) * _pair_select(first, jnp.exp(acol[0]), jnp.exp(acol[1]), (q, q))
        sdec = jnp.where(first_rows, jnp.broadcast_to(jnp.exp(alast[:, h0:h0 + 1]), (q, q)),
                         jnp.broadcast_to(jnp.exp(alast[:, h1:h1 + 1]), (q, q)))
        state_sc[rows, :] = prev * sdec + chunk_state
        y_sc[:, pr * 2 * hp:(pr + 1) * 2 * hp] = y_diag + y_off + dvec_ref[:, pr * 2 * hp:(pr + 1) * 2 * hp] * xpair

    out_ref[...] = _rms(y_sc[...] * _silu(z_ref[...]), nw_ref[...]).astype(out_ref.dtype)

    @pl.when(c == pl.num_programs(1) - 1)
    def _():
        ssm_ref[...] = state_sc[...]
        conv_ref[...] = xpad_sc[halo + q - (M2_CONV - 1):halo + q, :]


def ssd_prompt(h, dt_raw, nb, seq, p, *, d_s5, d_inner, nheads):
    q = M2_CHUNK
    nc = seq // q
    conv_dim = d_inner + 2 * M2_NGROUPS * M2_DSTATE
    xw = 512
    xoff = d_s5 + d_inner
    assert d_s5 % d_inner == 0 and xoff % xw == 0 and conv_dim % xw == 0
    assert M2_CHUNK == 2 * M2_HEADDIM == M2_DSTATE == LANES
    zblk = d_s5 // d_inner
    n_xparts = conv_dim // xw
    m = nb * seq
    pad = lambda v: jnp.pad(v, (0, LANES - v.shape[0])).reshape(1, LANES)
    dvec = jnp.repeat(p['m2_d'], M2_HEADDIM).reshape(1, d_inner)
    vec = lambda n: pl.BlockSpec((1, n), lambda b, c: (0, 0))
    tok = lambda w, j: pl.BlockSpec((q, w), lambda b, c, j=j: (b * nc + c, j))
    out, ssm, conv = pl.pallas_call(
        functools.partial(_ssd_chunk_kernel, d_inner=d_inner, nheads=nheads, n_xparts=n_xparts),
        grid=(nb, nc),
        in_specs=[tok(xw, xoff // xw + i) for i in range(n_xparts)] + [tok(d_inner, zblk), tok(LANES, 0),
                  pl.BlockSpec((M2_CONV, conv_dim), lambda b, c: (0, 0)), vec(conv_dim), vec(LANES), vec(LANES),
                  vec(d_inner), vec(d_inner)],
        out_specs=[tok(d_inner, 0),
                   pl.BlockSpec((None, nheads * M2_HEADDIM, M2_DSTATE), lambda b, c: (b, 0, 0)),
                   pl.BlockSpec((None, M2_CONV - 1, conv_dim), lambda b, c: (b, 0, 0))],
        out_shape=[jax.ShapeDtypeStruct((m, d_inner), BF16),
                   jax.ShapeDtypeStruct((nb, nheads * M2_HEADDIM, M2_DSTATE), F32),
                   jax.ShapeDtypeStruct((nb, M2_CONV - 1, conv_dim), F32)],
        scratch_shapes=[pltpu.VMEM((nheads * M2_HEADDIM, M2_DSTATE), F32),
                        pltpu.VMEM((q + 8, conv_dim), F32),
                        pltpu.VMEM((q, d_inner), F32)],
        compiler_params=_cparams(("parallel", "arbitrary")),
        name="ssd_chunk",
    )(*([h] * n_xparts), h, dt_raw, p['m2_conv_w'], p['m2_conv_b'].reshape(1, conv_dim), pad(p['m2_dt_bias']),
      pad(p['m2_a_log']), dvec, p['m2_norm_w'].reshape(1, d_inner))
    return out, ssm.reshape(nb, nheads, M2_HEADDIM, M2_DSTATE), conv


SSD_STEP_SEQS = 8


def _ssd_step_kernel(*refs, d_inner, nheads, n_xparts):
    xparts = refs[:n_xparts]
    (z_ref, dt_ref, cs0_ref, cs1_ref, cs2_ref, cw_ref, cb_ref, dtb_ref, alog_ref, dvec_ref, nw_ref, st_ref,
     out_ref, so_ref, lhs_sc, bfull_sc, ct_sc, yt_sc, xs_sc) = refs[n_xparts:]
    i = pl.program_id(0)
    nb = z_ref.shape[0]
    ns = M2_DSTATE
    rows_g = (nheads // M2_NGROUPS) * M2_HEADDIM

    @pl.when(i == 0)
    def _():
        cw = cw_ref[...]
        xbc = jnp.concatenate([xr[...] for xr in xparts], axis=1)
        conv = (cb_ref[...] + cw[3:4, :] * xbc + cw[2:3, :] * cs2_ref[...]
                + cw[1:2, :] * cs1_ref[...] + cw[0:1, :] * cs0_ref[...])
        xc = _silu(conv)
        dt = _softplus(dt_ref[...] + dtb_ref[...])
        dec = jnp.exp(dt * (-jnp.exp(alog_ref[...])))
        hrow = lax.broadcasted_iota(jnp.int32, (LANES, d_inner), 0)
        hcol = lax.broadcasted_iota(jnp.int32, (LANES, d_inner), 1)
        expand = jnp.where(hcol // M2_HEADDIM == hrow, 1.0, 0.0).astype(BF16)

        def expand_heads(v):
            a, b_, c = _split3(v)
            return _dot(a, expand) + (_dot(b_, expand) + _dot(c, expand))

        xs = xc[:, :d_inner]
        xs_sc[...] = xs
        xdt_t = (xs * expand_heads(dt)).T
        d_hi, d_mid, d_lo = _split3(expand_heads(dec).T)
        for g in range(M2_NGROUPS):
            r = slice(g * rows_g, (g + 1) * rows_g)
            lhs_sc[g] = jnp.concatenate([xdt_t[r].astype(BF16), d_hi[r], d_mid[r], d_lo[r]], axis=1)
            b_g = xc[:, d_inner + g * ns:d_inner + (g + 1) * ns]
            bfull_sc[g] = jnp.concatenate([b_g, jnp.zeros_like(b_g)], axis=1)
            c_g = xc[:, d_inner + (M2_NGROUPS + g) * ns:d_inner + (M2_NGROUPS + g + 1) * ns]
            ct_sc[g] = c_g.T
        yt_sc[...] = jnp.zeros_like(yt_sc)

    row_id = lax.broadcasted_iota(jnp.int32, (nb, 2 * ns), 0)
    lane_id = lax.broadcasted_iota(jnp.int32, (nb, 2 * ns), 1)
    col_id = lax.broadcasted_iota(jnp.int32, (ns, nb), 1)
    for j in range(st_ref.shape[0]):
        b = i * st_ref.shape[0] + j
        r_bot = jnp.where((row_id == b) & (lane_id >= ns), 1.0, 0.0).astype(BF16)
        for g in range(M2_NGROUPS):
            r = pl.ds(g * rows_g, rows_g)
            r_top = jnp.where(row_id == b, bfull_sc[g], 0.0).astype(BF16)
            rhs = jnp.concatenate([r_top, r_bot, r_bot, r_bot], axis=0)
            o = _dot(lhs_sc[g], rhs)
            hnew = st_ref[j, r, :] * o[:, ns:] + o[:, :ns]
            so_ref[j, r, :] = hnew
            cm = jnp.where(col_id == b, ct_sc[g], 0.0).astype(BF16)
            yt_sc[r, :] += _dot(hnew.astype(BF16), cm)

    @pl.when(i == pl.num_programs(0) - 1)
    def _():
        y = yt_sc[...].T + dvec_ref[...] * xs_sc[...]
        out_ref[...] = _rms(y * _silu(z_ref[...]), nw_ref[...]).astype(out_ref.dtype)


def ssd_sample(h, dt_raw, state, conv_state, p, *, d_s5, d_inner, nheads):
    nb = h.shape[0]
    conv_dim = d_inner + 2 * M2_NGROUPS * M2_DSTATE
    xw = 512
    xoff = d_s5 + d_inner
    assert nb == LANES and M2_DSTATE == LANES and M2_CONV == 4
    assert xoff % xw == 0 and conv_dim % xw == 0 and d_s5 % d_inner == 0 and nb % SSD_STEP_SEQS == 0
    n_xparts = conv_dim // xw
    rows = nheads * M2_HEADDIM
    rows_g = rows // M2_NGROUPS
    pad = lambda v: jnp.pad(v, (0, LANES - v.shape[0])).reshape(1, LANES)
    dvec = jnp.repeat(p['m2_d'], M2_HEADDIM).reshape(1, d_inner)
    full = lambda a, b, j=0: pl.BlockSpec((a, b), lambda i, j=j: (0, j))
    st_spec = pl.BlockSpec((SSD_STEP_SEQS, rows, M2_DSTATE), lambda i: (i, 0, 0))
    out, new_state = pl.pallas_call(
        functools.partial(_ssd_step_kernel, d_inner=d_inner, nheads=nheads, n_xparts=n_xparts),
        grid=(nb // SSD_STEP_SEQS,),
        in_specs=[full(nb, xw, xoff // xw + k) for k in range(n_xparts)]
        + [full(nb, d_inner, d_s5 // d_inner), full(nb, LANES)]
        + [full(nb, conv_dim)] * 3
        + [full(M2_CONV, conv_dim), full(1, conv_dim), full(1, LANES), full(1, LANES), full(1, d_inner),
           full(1, d_inner), st_spec],
        out_specs=[full(nb, d_inner), st_spec],
        out_shape=[jax.ShapeDtypeStruct((nb, d_inner), BF16), jax.ShapeDtypeStruct((nb, rows, M2_DSTATE), F32)],
        scratch_shapes=[pltpu.VMEM((M2_NGROUPS, rows_g, 4 * nb), BF16),
                        pltpu.VMEM((M2_NGROUPS, nb, 2 * M2_DSTATE), F32),
                        pltpu.VMEM((M2_NGROUPS, M2_DSTATE, nb), F32),
                        pltpu.VMEM((rows, nb), F32),
                        pltpu.VMEM((nb, d_inner), F32)],
        compiler_params=_cparams(("arbitrary",)),
        name="ssd_step",
    )(*([h] * n_xparts), h, dt_raw, conv_state[:, 0], conv_state[:, 1], conv_state[:, 2],
      p['m2_conv_w'], p['m2_conv_b'].reshape(1, conv_dim), pad(p['m2_dt_bias']), pad(p['m2_a_log']),
      dvec, p['m2_norm_w'].reshape(1, d_inner), state.reshape(nb, rows, M2_DSTATE))
    xbc = lax.slice_in_dim(h, xoff, xoff + conv_dim, axis=1)
    new_conv = jnp.concatenate([conv_state[:, 1:], xbc[:, None, :]], axis=1)
    return out, new_state.reshape(state.shape), new_conv


def _softmax_rows(s):
    e = jnp.exp(s - jnp.max(s, axis=-1, keepdims=True))
    return e / jnp.sum(e, axis=-1, keepdims=True)


def _attn_kernel(q_ref, k_ref, v_ref, o_ref, *, scale):
    s = _dot_nt(q_ref[...].astype(BF16), k_ref[...].astype(BF16)) * scale
    o_ref[...] = _dot(_softmax_rows(s).astype(BF16), v_ref[...].astype(BF16)).astype(o_ref.dtype)


def attention_prompt(q, k, v, nb, seq, mem, heads, *, tq):
    d = q.shape[1]
    hd = d // heads
    nq = seq // tq
    kv_spec = pl.BlockSpec((mem, hd), lambda b, h, i: (b, h))
    q_spec = pl.BlockSpec((tq, hd), lambda b, h, i: (b * nq + i, h))
    return pl.pallas_call(
        functools.partial(_attn_kernel, scale=hd ** -0.5),
        grid=(nb, heads, nq),
        in_specs=[q_spec, kv_spec, kv_spec],
        out_specs=q_spec,
        out_shape=jax.ShapeDtypeStruct(q.shape, BF16),
        compiler_params=_cparams(("parallel", "parallel", "parallel")),
        name="attention_prompt",
    )(q, k, v)


ATTN_STEP_SEQS = 4


ATTN_STEP_ROWS = 64


def _attn_step_kernel(q_ref, k_ref, v_ref, o_ref, s_sc, *, scale):
    nseq, mem, heads, hd = k_ref.shape
    ch = ATTN_STEP_ROWS
    pack = 8 // heads
    for j in range(nseq):
        q = jnp.concatenate([q_ref[j]] * pack, axis=0)

        def score(c, mx):
            rows = pl.ds(pl.multiple_of(c * ch, ch), ch)
            k = k_ref[j, rows].reshape(ch // pack, pack * heads, hd)
            s = jnp.sum(k * q, axis=-1, keepdims=True) * scale
            s_sc[pl.ds(pl.multiple_of(c * (ch // pack), ch // pack), ch // pack)] = s
            return jnp.maximum(mx, jnp.max(s, axis=0))

        mx = lax.fori_loop(0, mem // ch, score, jnp.full((pack * heads, 1), NEG, F32))
        mx1 = mx[0:heads]
        for i in range(1, pack):
            mx1 = jnp.maximum(mx1, mx[i * heads:(i + 1) * heads])
        mx = jnp.concatenate([mx1] * pack, axis=0)

        def accum(c, carry):
            den, acc = carry
            rows = pl.ds(pl.multiple_of(c * ch, ch), ch)
            v = v_ref[j, rows].reshape(ch // pack, pack * heads, hd)
            e = jnp.exp(s_sc[pl.ds(pl.multiple_of(c * (ch // pack), ch // pack), ch // pack)] - mx)
            return den + jnp.sum(e, axis=0), acc + jnp.sum(e * v, axis=0)

        den, acc = lax.fori_loop(0, mem // ch, accum,
                                 (jnp.zeros((pack * heads, 1), F32), jnp.zeros((pack * heads, hd), F32)))
        den1, acc1 = den[0:heads], acc[0:heads]
        for i in range(1, pack):
            den1 = den1 + den[i * heads:(i + 1) * heads]
            acc1 = acc1 + acc[i * heads:(i + 1) * heads]
        o_ref[j] = acc1 / den1


def attention_sample(q, k_cache, v_cache):
    b, mem, heads, hd = k_cache.shape
    nseq = ATTN_STEP_SEQS
    q_spec = pl.BlockSpec((nseq, heads, hd), lambda i: (i, 0, 0))
    kv_spec = pl.BlockSpec((nseq, mem, heads, hd), lambda i: (i, 0, 0, 0))
    out = pl.pallas_call(
        functools.partial(_attn_step_kernel, scale=hd ** -0.5),
        grid=(b // nseq,),
        in_specs=[q_spec, kv_spec, kv_spec],
        out_specs=q_spec,
        out_shape=jax.ShapeDtypeStruct((b, heads, hd), F32),
        scratch_shapes=[pltpu.VMEM((mem * heads // 8, 8, 1), F32)],
        compiler_params=_cparams(("parallel",)),
        name="attention_step",
    )(q.reshape(b, heads, hd), k_cache, v_cache)
    return out.reshape(b, heads * hd)


NEG = -1e30


INFO_G1, INFO_G2, INFO_E1, INFO_E2 = 0, 1, 2, 3


def _lane_pack(lane, values):
    out = 0.0
    for k, v in values:
        out = jnp.where(lane == k, v, out)
    return out


def _router_kernel(x_ref, nw_ref, wr_ref, br_ref, sel_ref, info_ref, *, n_experts):
    xn = _rms(x_ref[...], nw_ref[...])
    x_hi, x_lo = _split2(xn)
    w_hi, w_lo = _split2(wr_ref[...])
    hi_terms = _dot(x_hi, jnp.concatenate([w_hi, w_lo], axis=1))
    logits = hi_terms[:, :LANES] + (hi_terms[:, LANES:] + _dot(x_lo, w_hi)) + br_ref[...]
    lane = lax.broadcasted_iota(jnp.int32, logits.shape, 1)
    big = jnp.int32(2 ** 30)
    is_c = (lane >= n_experts) & (lane < n_experts + MOE_GROUPS)
    lc = jnp.where(is_c, logits, NEG)
    cmax = jnp.max(lc, axis=-1, keepdims=True)
    gsel = jnp.min(jnp.where(lc == cmax, lane, big), axis=-1, keepdims=True) - n_experts
    gate_c = 1.0 / jnp.sum(jnp.where(is_c, jnp.exp(lc - cmax), 0.0), axis=-1, keepdims=True)
    in_group = (lane < n_experts) & (lane // MOE_PER_GROUP == gsel)
    lf = jnp.where(in_group, logits, NEG)
    t1 = jnp.max(lf, axis=-1, keepdims=True)
    i1 = jnp.min(jnp.where(lf == t1, lane, big), axis=-1, keepdims=True)
    lf2 = jnp.where(lane == i1, NEG, lf)
    t2 = jnp.max(lf2, axis=-1, keepdims=True)
    i2 = jnp.min(jnp.where(lf2 == t2, lane, big), axis=-1, keepdims=True)
    r = jnp.exp(t2 - t1)
    g1 = gate_c / (1.0 + r)
    g2 = gate_c * r / (1.0 + r)
    sel_ref[...] = jnp.where((lane == i1) | (lane == i2), 1.0, 0.0)
    info_ref[...] = _lane_pack(lane, ((INFO_G1, g1), (INFO_G2, g2), (INFO_E1, i1.astype(F32)),
                                      (INFO_E2, i2.astype(F32))))


def moe_router(x, norm_w, w_coarse, b_coarse, w_fine, b_fine, *, tm):
    m, d = x.shape
    e = w_fine.shape[1]
    padw = LANES - e - MOE_GROUPS
    wr = jnp.concatenate([w_fine, w_coarse, jnp.zeros((d, padw), F32)], axis=1)
    br = jnp.concatenate([b_fine, b_coarse, jnp.zeros((padw,), F32)]).reshape(1, LANES)
    tok = pl.BlockSpec((tm, LANES), lambda i: (i, 0))
    return pl.pallas_call(
        functools.partial(_router_kernel, n_experts=e),
        grid=(m // tm,),
        in_specs=[pl.BlockSpec((tm, d), lambda i: (i, 0)), pl.BlockSpec((1, d), lambda i: (0, 0)),
                  pl.BlockSpec((d, LANES), lambda i: (0, 0)), pl.BlockSpec((1, LANES), lambda i: (0, 0))],
        out_specs=[tok, tok],
        out_shape=[jax.ShapeDtypeStruct((m, LANES), F32), jax.ShapeDtypeStruct((m, LANES), F32)],
        compiler_params=_cparams(("parallel",)),
        name="moe_router",
    )(x, norm_w.reshape(1, d), wr, br)


def _moe_rank_kernel(sel_ref, info_ref, rank_ref, counts_ref, carry_sc):
    i = pl.program_id(0)

    @pl.when(i == 0)
    def _():
        carry_sc[...] = jnp.zeros_like(carry_sc)

    sel = sel_ref[...]
    tm = sel.shape[0]
    row = lax.broadcasted_iota(jnp.int32, (tm, tm), 0)
    col = lax.broadcasted_iota(jnp.int32, (tm, tm), 1)
    before = jnp.where(row > col, 1.0, 0.0).astype(BF16)
    rank = _dot(before, sel.astype(BF16)) + carry_sc[...]
    info = info_ref[...]
    lane = lax.broadcasted_iota(jnp.int32, sel.shape, 1)
    e1 = info[:, INFO_E1:INFO_E1 + 1].astype(jnp.int32)
    e2 = info[:, INFO_E2:INFO_E2 + 1].astype(jnp.int32)
    r1 = jnp.sum(jnp.where(lane == e1, rank, 0.0), axis=-1, keepdims=True)
    r2 = jnp.sum(jnp.where(lane == e2, rank, 0.0), axis=-1, keepdims=True)
    rank_ref[...] = _lane_pack(lane, ((0, r1), (1, r2)))
    carry_sc[...] += jnp.sum(sel, axis=0, keepdims=True)

    @pl.when(i == pl.num_programs(0) - 1)
    def _():
        counts_ref[...] = carry_sc[...]


def _moe_plan_kernel(rank_ref, info_ref, counts_ref, dest_ref, tiles_ref, *, tile, n_experts):
    counts = counts_ref[...]
    ntile_e = jnp.floor((counts + (tile - 1)) * (1.0 / tile))
    padded = jnp.broadcast_to(ntile_e * tile, (8, LANES))
    r = lax.broadcasted_iota(jnp.int32, (LANES, LANES), 0)
    c = lax.broadcasted_iota(jnp.int32, (LANES, LANES), 1)
    lower = jnp.where(r < c, 1.0, 0.0).astype(BF16)
    p_hi, p_mid, p_lo = _split3(padded)
    offs = (_dot(p_hi, lower) + (_dot(p_mid, lower) + _dot(p_lo, lower)))[0:1, :]
    ends = offs + padded[0:1, :]

    info = info_ref[...]
    rank = rank_ref[...]
    lane = lax.broadcasted_iota(jnp.int32, info.shape, 1)
    e1 = info[:, INFO_E1:INFO_E1 + 1].astype(jnp.int32)
    e2 = info[:, INFO_E2:INFO_E2 + 1].astype(jnp.int32)
    d1 = jnp.sum(jnp.where(lane == e1, offs, 0.0), axis=-1, keepdims=True) + rank[:, 0:1]
    d2 = jnp.sum(jnp.where(lane == e2, offs, 0.0), axis=-1, keepdims=True) + rank[:, 1:2]
    dest_ref[...] = _lane_pack(lane, ((0, d1), (1, d2))).astype(jnp.int32)

    ends_col = jnp.broadcast_to(ends, (LANES, LANES)).T
    start = (c * tile).astype(F32)
    n_before = jnp.sum(jnp.where((ends_col <= start) & (r < n_experts), 1.0, 0.0), axis=0, keepdims=True)
    lane1 = lax.broadcasted_iota(jnp.int32, (1, LANES), 1)
    total = jnp.sum(jnp.where(lane1 == n_experts - 1, ends, 0.0), axis=-1, keepdims=True)
    tail = jnp.where(counts > 0.0, ends - tile, -1.0)
    tiles_ref[...] = jnp.zeros_like(tiles_ref)
    tiles_ref[0:1, :] = jnp.minimum(n_before, n_experts - 1.0).astype(jnp.int32)
    tiles_ref[1:2, :] = jnp.broadcast_to(total * (1.0 / tile), (1, LANES)).astype(jnp.int32)
    tiles_ref[2:3, :] = tail.astype(jnp.int32)


def _moe_dispatch_kernel(dest_ref, tails_ref, ntiles_ref, *rest, tile, n_experts, group_steps, group_offsets):
    n_groups = len(group_offsets)
    nw_ref = rest[0]
    x_refs = rest[1:1 + n_groups]
    xs_ref, zero_sc, xn_sc, sems = rest[1 + n_groups:]
    i = pl.program_id(0)
    n_tiles_max = xs_ref.shape[0] // tile

    def fill_copy(j):
        return pltpu.make_async_copy(zero_sc, xs_ref.at[pl.ds(pl.multiple_of(j * tile, tile), tile)], sems.at[2])

    def tail_copy(e):
        return pltpu.make_async_copy(zero_sc, xs_ref.at[pl.ds(pl.multiple_of(tails_ref[e], tile), tile)],
                                     sems.at[1])

    @pl.when(i == 0)
    def _():
        zero_sc[...] = jnp.zeros_like(zero_sc)

        def fill(j, _):
            fill_copy(j).start()
            return 0

        def clear(e, _):
            @pl.when(tails_ref[e] >= 0)
            def _():
                tail_copy(e).start()
            return 0

        def clear_wait(e, _):
            @pl.when(tails_ref[e] >= 0)
            def _():
                tail_copy(e).wait()
            return 0

        lax.fori_loop(ntiles_ref[0], n_tiles_max, fill, 0)
        lax.fori_loop(0, n_experts, clear, 0)
        lax.fori_loop(0, n_experts, clear_wait, 0)

    for g, x_ref in enumerate(x_refs):
        @pl.when((i >= group_steps[g]) & (i < group_steps[g + 1]))
        def _(g=g, x_ref=x_ref):
            tm = x_ref.shape[0]
            xn_ref = xn_sc.at[pl.ds(0, tm)]
            xn_ref[...] = _rms(x_ref[...], nw_ref[...])
            base = 2 * (group_offsets[g] + (i - group_steps[g]) * tm)
            for r in range(tm):
                for k in range(2):
                    pltpu.make_async_copy(xn_ref.at[pl.ds(r, 1)], xs_ref.at[pl.ds(dest_ref[base + 2 * r + k], 1)],
                                          sems.at[0]).start(priority=k)
            for _ in range(2):
                pltpu.make_async_copy(xn_ref, xs_ref.at[pl.ds(0, tm)], sems.at[0]).wait()

    @pl.when(i == pl.num_programs(0) - 1)
    def _():
        def fill_wait(j, _):
            fill_copy(j).wait()
            return 0

        lax.fori_loop(ntiles_ref[0], n_tiles_max, fill_wait, 0)


def _moe_expert_kernel(texp_ref, ntiles_ref, xs_ref, wg_ref, wu_ref, wd_ref, ys_ref):
    @pl.when(pl.program_id(0) < ntiles_ref[0])
    def _():
        x = xs_ref[...].astype(BF16)
        hg = _dot(x, wg_ref[...].astype(BF16))
        hu = _dot(x, wu_ref[...].astype(BF16))
        ys_ref[...] = _dot((_silu(hg) * hu).astype(BF16), wd_ref[...].astype(BF16))


def _moe_combine_kernel(dest_ref, ys_ref, x_ref, info_ref, fw_ref, y_ref, buf_sc, sems):
    s = pl.program_id(0)
    n_blocks = pl.num_programs(0) - 1
    tm = x_ref.shape[0]
    slot = s % 2

    @pl.when(s < n_blocks)
    def _():
        for r in range(tm):
            for k in range(2):
                pltpu.make_async_copy(ys_ref.at[pl.ds(dest_ref[2 * (s * tm + r) + k], 1)],
                                      buf_sc.at[slot, k, pl.ds(r, 1)], sems.at[slot]).start(priority=k)

    @pl.when(s > 0)
    def _():
        prev = 1 - slot
        for k in range(2):
            pltpu.make_async_copy(ys_ref.at[pl.ds(0, tm)], buf_sc.at[prev, k], sems.at[prev]).wait()
        info = info_ref[...]
        y = (x_ref[...] + info[:, INFO_G1:INFO_G1 + 1] * buf_sc[prev, 0]
             + info[:, INFO_G2:INFO_G2 + 1] * buf_sc[prev, 1])
        y_ref[...] = _rms(y, fw_ref[...])


MOE_TILE = 256
MOE_TOKENS_PER_STEP = 512


def _largest_tile(m, cap):
    return max(t for t in range(8, cap + 1, 8) if m % t == 0)


def moe_routed_final(x_list, p, final_w):
    d = x_list[0].shape[1]
    ne, _, f = p['moe_w_gate'].shape
    sizes = [x.shape[0] for x in x_list]
    m = sum(sizes)
    tile = min(MOE_TILE, m)
    routed = [moe_router(x, p['norm_ffn_w'], p['router_coarse_w'], p['router_coarse_b'],
                         p['router_fine_w'], p['router_fine_b'], tm=min(x.shape[0], 512)) for x in x_list]
    sel = jnp.concatenate([r[0] for r in routed], axis=0)
    info = jnp.concatenate([r[1] for r in routed], axis=0)
    tm = _largest_tile(m, 1024)
    tok = pl.BlockSpec((tm, LANES), lambda i: (i, 0))
    rank, counts = pl.pallas_call(
        _moe_rank_kernel,
        grid=(m // tm,),
        in_specs=[tok, tok],
        out_specs=[tok, pl.BlockSpec((1, LANES), lambda i: (0, 0))],
        out_shape=[jax.ShapeDtypeStruct((m, LANES), F32), jax.ShapeDtypeStruct((1, LANES), F32)],
        scratch_shapes=[pltpu.VMEM((1, LANES), F32)],
        compiler_params=_cparams(("arbitrary",)),
        name="moe_rank",
    )(sel, info)
    n_tiles_max = (2 * m) // tile + ne
    assert n_tiles_max <= LANES
    dest, tiles = pl.pallas_call(
        functools.partial(_moe_plan_kernel, tile=tile, n_experts=ne),
        out_shape=[jax.ShapeDtypeStruct((m, LANES), jnp.int32), jax.ShapeDtypeStruct((8, LANES), jnp.int32)],
        compiler_params=pltpu.CompilerParams(vmem_limit_bytes=VMEM_LIMIT),
        name="moe_plan",
    )(rank, info, counts)
    dest_flat = dest[:, :2].reshape(2 * m)
    tile_expert = tiles[0, :n_tiles_max]
    n_tiles = tiles[1, :1]
    tails = tiles[2, :ne]

    rows = n_tiles_max * tile

    tds = [min(mg, 2 * MOE_TOKENS_PER_STEP) for mg in sizes]
    group_steps = [0]
    for mg, td in zip(sizes, tds):
        group_steps.append(group_steps[-1] + mg // td)
    group_offsets = [sum(sizes[:g]) for g in range(len(sizes))]

    def group_spec(g):
        first, last = group_steps[g], group_steps[g + 1] - 1
        return pl.BlockSpec((tds[g], d), lambda i, *_: (jnp.clip(i, first, last) - first, 0))

    xs = pl.pallas_call(
        functools.partial(_moe_dispatch_kernel, tile=tile, n_experts=ne, group_steps=tuple(group_steps),
                          group_offsets=tuple(group_offsets)),
        grid_spec=pltpu.PrefetchScalarGridSpec(
            num_scalar_prefetch=3, grid=(group_steps[-1],),
            in_specs=[pl.BlockSpec((1, d), lambda i, *_: (0, 0))] + [group_spec(g) for g in range(len(sizes))],
            out_specs=pl.BlockSpec(memory_space=pl.ANY),
            scratch_shapes=[pltpu.VMEM((tile, d), F32), pltpu.VMEM((max(tds), d), F32),
                            pltpu.SemaphoreType.DMA((3,))]),
        out_shape=jax.ShapeDtypeStruct((rows, d), F32),
        compiler_params=_cparams(("arbitrary",)),
        name="moe_dispatch",
    )(dest_flat, tails, n_tiles, p['norm_ffn_w'].reshape(1, d), *x_list)

    def tile_idx(i, te, nt):
        return jnp.minimum(i, nt[0] - 1)

    ys = pl.pallas_call(
        _moe_expert_kernel,
        grid_spec=pltpu.PrefetchScalarGridSpec(
            num_scalar_prefetch=2, grid=(n_tiles_max,),
            in_specs=[pl.BlockSpec((tile, d), lambda i, te, nt: (tile_idx(i, te, nt), 0)),
                      pl.BlockSpec((None, d, f), lambda i, te, nt: (te[tile_idx(i, te, nt)], 0, 0)),
                      pl.BlockSpec((None, d, f), lambda i, te, nt: (te[tile_idx(i, te, nt)], 0, 0)),
                      pl.BlockSpec((None, f, d), lambda i, te, nt: (te[tile_idx(i, te, nt)], 0, 0))],
            out_specs=pl.BlockSpec((tile, d), lambda i, te, nt: (tile_idx(i, te, nt), 0))),
        out_shape=jax.ShapeDtypeStruct((rows, d), F32),
        input_output_aliases={2: 0},
        compiler_params=_cparams(("arbitrary",)),
        name="moe_experts",
    )(tile_expert, n_tiles, xs, p['moe_w_gate'], p['moe_w_up'], p['moe_w_down'])

    outs = []
    off = 0
    for x, (_, info_g), mg in zip(x_list, routed, sizes):
        tc = min(mg, MOE_TOKENS_PER_STEP)
        outs.append(pl.pallas_call(
            _moe_combine_kernel,
            grid_spec=pltpu.PrefetchScalarGridSpec(
                num_scalar_prefetch=1, grid=(mg // tc + 1,),
                in_specs=[pl.BlockSpec(memory_space=pl.ANY),
                          pl.BlockSpec((tc, d), lambda s, dref: (jnp.maximum(s - 1, 0), 0)),
                          pl.BlockSpec((tc, LANES), lambda s, dref: (jnp.maximum(s - 1, 0), 0)),
                          pl.BlockSpec((1, d), lambda s, dref: (0, 0))],
                out_specs=pl.BlockSpec((tc, d), lambda s, dref: (jnp.maximum(s - 1, 0), 0)),
                scratch_shapes=[pltpu.VMEM((2, 2, tc, d), F32), pltpu.SemaphoreType.DMA((2,))]),
            out_shape=jax.ShapeDtypeStruct((mg, d), F32),
            compiler_params=_cparams(("arbitrary",)),
            name="moe_combine",
        )(dest_flat[2 * off:2 * (off + mg)], ys, x, info_g, final_w.reshape(1, d)))
        off += mg
    return outs


def s5_prompt(h, nb, seq, prep, p):
    y, hfin = s5_scan(h, prep, nb, seq)
    out = s5_head(y, h, p['s5_d'], p['s5_glu_w'], p['s5_glu_b'], p['s5_norm_w'], tm=min(512, nb * seq))
    return out, hfin[:, 0].transpose(1, 0, 2), hfin[:, 1].transpose(1, 0, 2)


def s5_sample(h, st_re, st_im, prep, p):
    g = prep[0].shape[0]
    ch = S5_CH
    ds = g * ch
    b = h.shape[0]
    u_t = h[:, :ds].reshape(b, g, ch).transpose(1, 0, 2)
    y_t, n_re, n_im = s5_step(u_t, st_re.transpose(1, 0, 2), st_im.transpose(1, 0, 2), prep,
                              p['s5_c_re'], p['s5_c_im'])
    y = y_t.transpose(1, 0, 2).reshape(b, ds)
    out = s5_head(y, h, p['s5_d'], p['s5_glu_w'], p['s5_glu_b'], p['s5_norm_w'], tm=b)
    return out, n_re.transpose(1, 0, 2), n_im.transpose(1, 0, 2)


def _row_tile(m):
    return min(m, 1024)


def _col_tile(m, n):
    if m <= 256:
        return n
    return max(t for t in range(256, 1793, 256) if n % t == 0)


def _mixer_and_attention(x, p, s5_prep, *, nb, seq, mem_kv, xa_heads, states):
    m, d = x.shape
    g, n = p['s5_a_re'].shape
    d_s5 = g * S5_CH
    nheads = p['m2_a_log'].shape[0]
    d_inner = nheads * M2_HEADDIM
    conv_dim = d_inner + 2 * M2_NGROUPS * M2_DSTATE
    n_main = d_s5 + d_inner + conv_dim
    tm = _row_tile(m)

    w_in = p['w_in']
    w_dt = jnp.pad(w_in[:, n_main:], ((0, 0), (0, LANES - nheads)))
    tm2, tn2 = (2048, 512) if m % 2048 == 0 else (tm, _col_tile(m, d))
    h, dt_raw = fused_matmul([x], w_in, n_out=n_main, gain=p['norm_mix_w'], side_w=w_dt, tm=tm,
                             tn=_col_tile(m, n_main))

    if states is None:
        s5_out, s5_re, s5_im = s5_prompt(h, nb, seq, s5_prep, p)
        m2_out, ssm, conv = ssd_prompt(h, dt_raw, nb, seq, p, d_s5=d_s5, d_inner=d_inner, nheads=nheads)
    else:
        s5_out, s5_re, s5_im = s5_sample(h, states[0], states[1], s5_prep, p)
        m2_out, ssm, conv = ssd_sample(h, dt_raw, states[2], states[3], p, d_s5=d_s5, d_inner=d_inner,
                                       nheads=nheads)
    x1 = fused_matmul([s5_out, m2_out], p['w_out'], n_out=d, res=x, tm=tm2, tn=tn2)

    q = fused_matmul([x1], p['xa_wq'], n_out=d, gain=p['norm_xa_w'], tm=tm2, tn=tn2,
                     out_dtype=BF16 if states is None else F32)
    if states is None:
        mem = mem_kv[0].shape[0] // nb
        o = attention_prompt(q, mem_kv[0], mem_kv[1], nb, seq, mem, xa_heads, tq=min(seq, 2048))
    else:
        o = attention_sample(q, mem_kv[0], mem_kv[1])
    x2 = fused_matmul([o], p['xa_wo'], n_out=d, res=x1, tm=tm2, tn=tn2)

    return x2, s5_re, s5_im, ssm, conv


def kernel(x_prompt, x_sample, mem_prompt, state_s5_re, state_s5_im, state_ssm, state_conv, cache_mem_k, cache_mem_v, norm_mix_w, w_in, s5_a_re, s5_a_im, s5_log_dt, s5_b_re, s5_b_im, s5_c_re, s5_c_im, s5_d, s5_glu_w, s5_glu_b, s5_norm_w, m2_conv_w, m2_conv_b, m2_dt_bias, m2_a_log, m2_d, m2_norm_w, w_out, norm_xa_w, norm_mem_w, xa_wq, xa_wk, xa_wv, xa_wo, norm_ffn_w, router_coarse_w, router_coarse_b, router_fine_w, router_fine_b, moe_w_gate, moe_w_up, moe_w_down, norm_final_w):
    depth = w_in.shape[0]
    assert depth == 1, "the final norm is fused into the (only) layer"
    per_layer = dict(
        norm_mix_w=norm_mix_w, w_in=w_in, s5_a_re=s5_a_re, s5_a_im=s5_a_im, s5_log_dt=s5_log_dt,
        s5_b_re=s5_b_re, s5_b_im=s5_b_im, s5_c_re=s5_c_re, s5_c_im=s5_c_im, s5_d=s5_d, s5_glu_w=s5_glu_w,
        s5_glu_b=s5_glu_b, s5_norm_w=s5_norm_w, m2_conv_w=m2_conv_w, m2_conv_b=m2_conv_b, m2_dt_bias=m2_dt_bias,
        m2_a_log=m2_a_log, m2_d=m2_d, m2_norm_w=m2_norm_w, w_out=w_out, norm_xa_w=norm_xa_w,
        norm_mem_w=norm_mem_w, xa_wq=xa_wq, xa_wk=xa_wk, xa_wv=xa_wv, xa_wo=xa_wo, norm_ffn_w=norm_ffn_w,
        router_coarse_w=router_coarse_w, router_coarse_b=router_coarse_b, router_fine_w=router_fine_w,
        router_fine_b=router_fine_b, moe_w_gate=moe_w_gate, moe_w_up=moe_w_up, moe_w_down=moe_w_down)
    p = {k: v[0] for k, v in per_layer.items()}
    for name in ('w_in', 'w_out', 'xa_wq', 'xa_wo'):
        p[name] = p[name].astype(BF16)
    nb, seq, d = x_prompt.shape
    db, dseq, _ = x_sample.shape
    assert dseq == 1
    mem = mem_prompt.shape[1]
    xa_heads = cache_mem_k.shape[3]

    s5_prep = s5_prepare(p['s5_a_re'], p['s5_a_im'], p['s5_log_dt'], p['s5_b_re'], p['s5_b_im'],
                         p['s5_c_re'], p['s5_c_im'])

    memx = mem_prompt.reshape(nb * mem, d)
    mk = fused_matmul([memx], p['xa_wk'], n_out=d, gain=p['norm_mem_w'], tm=_row_tile(nb * mem),
                      tn=_col_tile(nb * mem, d))
    mv = fused_matmul([memx], p['xa_wv'], n_out=d, gain=p['norm_mem_w'], tm=_row_tile(nb * mem),
                      tn=_col_tile(nb * mem, d))
    xp, p_re, p_im, p_ssm, p_conv = _mixer_and_attention(
        x_prompt.reshape(nb * seq, d), p, s5_prep, nb=nb, seq=seq, mem_kv=(mk, mv), xa_heads=xa_heads, states=None)

    xs, s_re, s_im, s_ssm, s_conv = _mixer_and_attention(
        x_sample.reshape(db, d), p, s5_prep, nb=db, seq=1,
        mem_kv=(cache_mem_k[0], cache_mem_v[0]), xa_heads=xa_heads,
        states=(state_s5_re[0], state_s5_im[0], state_ssm[0], state_conv[0]))

    yp, ys = moe_routed_final([xp, xs], p, norm_final_w)

    kv_shape = (1, nb, mem) + cache_mem_k.shape[3:]
    return (yp.reshape(nb, seq, d), ys.reshape(db, 1, d), p_re[None], p_im[None], p_ssm[None], p_conv[None],
            mk.reshape(kv_shape), mv.reshape(kv_shape), s_re[None], s_im[None], s_ssm[None], s_conv[None])
```

```python
import functools
import math

import jax
import jax.numpy as jnp
from jax import lax
from jax.experimental import pallas as pl
from jax.experimental.pallas import tpu as pltpu

F32 = jnp.float32
BF16 = jnp.bfloat16
RMS_EPS = 1e-6

V7X_VMEM_BYTES = 64 * 1024 * 1024
VMEM_LIMIT = V7X_VMEM_BYTES - 8 * 1024 * 1024
LANES = 128

S5_CH = 16
S5_N = 64
S5_Q = 16
S5_GB = 8
S5_SEQ_PAD = 8
M2_HEADDIM = 64
M2_DSTATE = 128
M2_NGROUPS = 2
M2_CONV = 4
M2_CHUNK = 128
MOE_GROUPS = 4
MOE_PER_GROUP = 8


def _cparams(sem):
    return pltpu.CompilerParams(dimension_semantics=sem, vmem_limit_bytes=VMEM_LIMIT)


def _rms(x, w):
    return x * lax.rsqrt(jnp.mean(x * x, axis=-1, keepdims=True) + RMS_EPS) * w


def _sigmoid(x):
    return 1.0 / (1.0 + jnp.exp(-x))


def _silu(x):
    return x * _sigmoid(x)


def _softplus(x):
    return jnp.maximum(x, 0.0) + jnp.log1p(jnp.exp(-jnp.abs(x)))


def _gelu_tanh(x):
    return 0.5 * x * (1.0 + jnp.tanh(math.sqrt(2.0 / math.pi) * (x + 0.044715 * (x * x * x))))


def _dot(a, b):
    return jnp.dot(a, b, preferred_element_type=F32)


def _dot_nt(a, b):
    return lax.dot_general(a, b, (((1,), (1,)), ((), ())), preferred_element_type=F32)


def _split3(x):
    hi = x.astype(BF16)
    r = x - hi.astype(F32)
    mid = r.astype(BF16)
    lo = (r - mid.astype(F32)).astype(BF16)
    return hi, mid, lo


def _split2(x):
    hi = x.astype(BF16)
    lo = (x - hi.astype(F32)).astype(BF16)
    return hi, lo


def _mm_kernel(*refs, n_lhs, has_gain, has_res, has_side, staged):
    it = iter(refs)
    lhs = [next(it) for _ in range(n_lhs)]
    gain = next(it) if has_gain else None
    ws = [next(it) for _ in range(n_lhs)]
    side_w = next(it) if has_side else None
    res = next(it) if has_res else None
    out = next(it)
    side_out = next(it) if has_side else None
    lhs_bf = next(it) if staged else lhs

    if staged:
        @pl.when(pl.program_id(1) == 0)
        def _():
            for i in range(n_lhs):
                x = lhs[i][...]
                if has_gain:
                    x = _rms(x, gain[...])
                lhs_bf[i] = x.astype(BF16)
            if has_side:
                side_out[...] = _dot(lhs_bf[0], side_w[...].astype(BF16))

    acc = None
    for i in range(n_lhs):
        p = _dot(lhs_bf[i][...], ws[i][...].astype(BF16))
        acc = p if acc is None else acc + p
    if has_res:
        acc = acc + res[...]
    out[...] = acc.astype(out.dtype)


def fused_matmul(lhs_list, w, *, n_out, gain=None, res=None, side_w=None, out_dtype=F32, tm, tn):
    n_lhs = len(lhs_list)
    m, kp = lhs_list[0].shape
    assert all(a.shape == (m, kp) for a in lhs_list)
    assert w.shape[0] == n_lhs * kp and m % tm == 0 and n_out % tn == 0
    assert gain is None or n_lhs == 1
    staged = gain is not None or any(a.dtype != BF16 for a in lhs_list)
    assert staged or side_w is None
    grid = (m // tm, n_out // tn)
    in_specs = [pl.BlockSpec((tm, kp), lambda i, j: (i, 0)) for _ in range(n_lhs)]
    args = list(lhs_list)
    if gain is not None:
        in_specs.append(pl.BlockSpec((1, kp), lambda i, j: (0, 0)))
        args.append(gain.reshape(1, kp))
    for p in range(n_lhs):
        in_specs.append(pl.BlockSpec((kp, tn), lambda i, j, p=p: (p, j)))
        args.append(w)
    if side_w is not None:
        in_specs.append(pl.BlockSpec((kp, LANES), lambda i, j: (0, 0)))
        args.append(side_w)
    if res is not None:
        in_specs.append(pl.BlockSpec((tm, tn), lambda i, j: (i, j)))
        args.append(res)
    out_shape = [jax.ShapeDtypeStruct((m, n_out), out_dtype)]
    out_specs = [pl.BlockSpec((tm, tn), lambda i, j: (i, j))]
    if side_w is not None:
        out_shape.append(jax.ShapeDtypeStruct((m, LANES), F32))
        out_specs.append(pl.BlockSpec((tm, LANES), lambda i, j: (i, 0)))
    outs = pl.pallas_call(
        functools.partial(_mm_kernel, n_lhs=n_lhs, has_gain=gain is not None,
                          has_res=res is not None, has_side=side_w is not None, staged=staged),
        grid=grid, in_specs=in_specs, out_specs=out_specs, out_shape=out_shape,
        scratch_shapes=[pltpu.VMEM((n_lhs, tm, kp), BF16)] if staged else [],
        compiler_params=_cparams(("parallel", "arbitrary")),
        name="fused_matmul",
    )(*args)
    return outs if side_w is not None else outs[0]


def _s5_prep_kernel(*refs):
    for g in range(refs[0].shape[0]):
        _s5_prep_group(*[r.at[g] for r in refs])


def _s5_prep_group(lre_ref, lim_ref, ldt_ref, btre_ref, btim_ref, cre_ref, cim_ref,
                   tz_ref, wsre_ref, wsim_ref, wcre_ref, wcim_ref, aq_ref, ab_ref, bbt_ref):
    q, ch = S5_Q, S5_CH
    lr = lre_ref[...]
    li = lim_ref[...]
    step = jnp.exp(ldt_ref[...])
    mag = jnp.exp(lr * step)
    ab_re = mag * jnp.cos(li * step)
    ab_im = mag * jnp.sin(li * step)
    den = lr * lr + li * li
    num_re = ab_re - 1.0
    coef_re = (num_re * lr + ab_im * li) / den
    coef_im = (ab_im * lr - num_re * li) / den
    bt_re = btre_ref[...]
    bt_im = btim_ref[...]
    bb_re = coef_re * bt_re - coef_im * bt_im
    bb_im = coef_re * bt_im + coef_im * bt_re
    c_re = cre_ref[...]
    c_im = cim_ref[...]

    pw = [(jnp.ones_like(ab_re), jnp.zeros_like(ab_re))]
    for _ in range(q):
        pr, pi = pw[-1]
        pw.append((pr * ab_re - pi * ab_im, pr * ab_im + pi * ab_re))

    ca_re = [c_re * pr - c_im * pi for pr, pi in pw]
    ca_im = [c_re * pi + c_im * pr for pr, pi in pw]
    wcre_ref[...] = jnp.concatenate(ca_re[1:], axis=0).astype(BF16)
    wcim_ref[...] = jnp.concatenate([-x for x in ca_im[1:]], axis=0).astype(BF16)

    pr_stack = jnp.concatenate(ca_re[:q], axis=0)
    pi_stack = jnp.concatenate(ca_im[:q], axis=0)
    krow = None
    for a, b, sign in ((bb_re, pr_stack, 1.0), (bb_im, pi_stack, -1.0)):
        a_hi, a_lo = _split2(a)
        b_hi, b_lo = _split2(b)
        t = _dot_nt(a_hi, b_hi) + (_dot_nt(a_hi, b_lo) + _dot_nt(a_lo, b_hi))
        krow = sign * t if krow is None else krow + sign * t
    lane = lax.broadcasted_iota(jnp.int32, krow.shape, 1)
    blocks = [krow]
    for s in range(1, q):
        blocks.append(jnp.where(lane >= s * ch, pltpu.roll(krow, s * ch, 1), 0.0))
    tz_ref[...] = jnp.concatenate(blocks, axis=0).astype(BF16)

    ws_re, ws_im = [], []
    for s in range(q):
        pr, pi = pw[q - 1 - s]
        ws_re.append(bb_re * pr - bb_im * pi)
        ws_im.append(bb_re * pi + bb_im * pr)
    wsre_ref[...] = jnp.concatenate(ws_re, axis=0).astype(BF16)
    wsim_ref[...] = jnp.concatenate(ws_im, axis=0).astype(BF16)

    aq_ref[0:1, :] = pw[q][0]
    aq_ref[1:2, :] = pw[q][1]
    ab_ref[0:1, :] = ab_re
    ab_ref[1:2, :] = ab_im
    bbt_ref[0:ch, :] = bb_re
    bbt_ref[ch:2 * ch, :] = bb_im


def s5_prepare(a_re, a_im, log_dt, b_re, b_im, c_re, c_im):
    g, n = a_re.shape
    ch, q = S5_CH, S5_Q
    qc = q * ch
    bt_re = jnp.swapaxes(b_re, 1, 2)
    bt_im = jnp.swapaxes(b_im, 1, 2)

    def per_g(*dims):
        return pl.BlockSpec((S5_GB,) + dims, lambda i: (i,) + (0,) * len(dims))

    return pl.pallas_call(
        _s5_prep_kernel,
        grid=(g // S5_GB,),
        in_specs=[per_g(1, n), per_g(1, n), per_g(1, 1), per_g(ch, n), per_g(ch, n), per_g(ch, n), per_g(ch, n)],
        out_specs=[per_g(qc, qc), per_g(qc, n), per_g(qc, n), per_g(qc, n), per_g(qc, n),
                   per_g(2, n), per_g(2, n), per_g(2 * ch, n)],
        out_shape=[jax.ShapeDtypeStruct((g, qc, qc), BF16),
                   jax.ShapeDtypeStruct((g, qc, n), BF16), jax.ShapeDtypeStruct((g, qc, n), BF16),
                   jax.ShapeDtypeStruct((g, qc, n), BF16), jax.ShapeDtypeStruct((g, qc, n), BF16),
                   jax.ShapeDtypeStruct((g, 2, n), F32), jax.ShapeDtypeStruct((g, 2, n), F32),
                   jax.ShapeDtypeStruct((g, 2 * ch, n), F32)],
        compiler_params=_cparams(("parallel",)),
        name="s5_prepare",
    )(a_re.reshape(g, 1, n), a_im.reshape(g, 1, n), log_dt.reshape(g, 1, 1), bt_re, bt_im, c_re, c_im)


def _s5_scan_kernel(h_ref, tz_ref, wsre_ref, wsim_ref, wcre_ref, wcim_ref, aq_ref,
                    y_ref, hfin_ref, xs_sc, u_sc, yg_sc, sre_sc, sim_sc, *, nb, nchunk):
    gb, q, ch = S5_GB, S5_Q, S5_CH
    rows = nb * nchunk
    seq_stride = nchunk + S5_SEQ_PAD
    per_vreg = LANES // ch
    assert gb == per_vreg and q % per_vreg == 0
    slot = lax.broadcasted_iota(jnp.int32, (rows, LANES), 1) // ch

    halves = q // per_vreg

    def rot_rows(w, g):
        if g == 0:
            return w
        cut = (per_vreg - g) * ch
        parts = []
        for hf in range(halves):
            blk = w[hf * LANES:(hf + 1) * LANES]
            parts += [blk[cut:], blk[:cut]]
        return jnp.concatenate(parts, axis=0)

    for s in range(q):
        x = h_ref[pl.ds(s, rows, stride=q), :].astype(BF16)
        k = s % per_vreg
        xs_sc[s] = pltpu.roll(x, k * ch, 1) if k else x
    keep = [jnp.where(slot == j, 1.0, 0.0).astype(BF16) for j in range(per_vreg)]
    for g in range(gb):
        for hf in range(halves):
            acc = None
            for k in range(per_vreg):
                piece = xs_sc[hf * per_vreg + k] * keep[(g + k) % per_vreg]
                acc = piece if acc is None else acc + piece
            u_sc[g, :, hf * LANES:(hf + 1) * LANES] = acc

    for g in range(gb):
        u = u_sc[g]
        tz = rot_rows(tz_ref[g], g)
        tz = jnp.concatenate([pltpu.roll(tz[:, hf * LANES:(hf + 1) * LANES], g * ch, 1) if g
                              else tz[:, hf * LANES:(hf + 1) * LANES] for hf in range(halves)], axis=1)
        yg_sc[g] = _dot(u, tz)
        for sc, w_ref in ((sre_sc, wsre_ref), (sim_sc, wsim_ref)):
            s_all = _dot(u, rot_rows(w_ref[g], g))
            for b in range(nb):
                sc[g, b * seq_stride:b * seq_stride + nchunk, :] = s_all[b * nchunk:(b + 1) * nchunk]

    ar = [jnp.broadcast_to(aq_ref[g, 0:1, :], (nb, S5_N)) for g in range(gb)]
    ai = [jnp.broadcast_to(aq_ref[g, 1:2, :], (nb, S5_N)) for g in range(gb)]

    def step(c, carry):
        at = pl.ds(c, nb, stride=seq_stride)
        new = []
        for g in range(gb):
            hr, hi = carry[g]
            sr = sre_sc[g, at, :]
            si = sim_sc[g, at, :]
            sre_sc[g, at, :] = hr
            sim_sc[g, at, :] = hi
            new.append((ar[g] * hr - ai[g] * hi + sr, ar[g] * hi + ai[g] * hr + si))
        return tuple(new)

    zero = jnp.zeros((nb, S5_N), F32)
    fin = lax.fori_loop(0, nchunk, step, tuple((zero, zero) for _ in range(gb)), unroll=4)
    for g in range(gb):
        hfin_ref[g, 0] = fin[g][0]
        hfin_ref[g, 1] = fin[g][1]
        h_in = [jnp.concatenate([sc[g, b * seq_stride:b * seq_stride + nchunk, :] for b in range(nb)], axis=0)
                for sc in (sre_sc, sim_sc)]
        yg_sc[g] += (_dot_nt(h_in[0].astype(BF16), rot_rows(wcre_ref[g], g))
                     + _dot_nt(h_in[1].astype(BF16), rot_rows(wcim_ref[g], g)))

    for t in range(q):
        hf, k = divmod(t, per_vreg)
        acc = jnp.zeros((rows, LANES), F32)
        for g in range(gb):
            acc = jnp.where(slot == (k + g) % per_vreg, yg_sc[g, :, hf * LANES:(hf + 1) * LANES], acc)
        y_ref[pl.ds(t, rows, stride=q), :] = pltpu.roll(acc, (per_vreg - k) * ch, 1) if k else acc


def s5_scan(h, prep, nb, seq):
    tz, ws_re, ws_im, wc_re, wc_im, aq = prep[:6]
    g, qc, _ = tz.shape
    n = S5_N
    gb = S5_GB
    q = S5_Q
    assert gb * S5_CH == LANES and qc == q * S5_CH and seq % q == 0
    nchunk = seq // q
    rows = nb * nchunk
    m = nb * seq

    def blk(*dims):
        return pl.BlockSpec((gb,) + dims, lambda i: (i,) + (0,) * len(dims))

    return pl.pallas_call(
        functools.partial(_s5_scan_kernel, nb=nb, nchunk=nchunk),
        grid=(g // gb,),
        in_specs=[pl.BlockSpec((m, LANES), lambda i: (0, i)),
                  blk(qc, qc), blk(qc, n), blk(qc, n), blk(qc, n), blk(qc, n), blk(2, n)],
        out_specs=[pl.BlockSpec((m, LANES), lambda i: (0, i)), blk(2, nb, n)],
        out_shape=[jax.ShapeDtypeStruct((m, g * S5_CH), F32), jax.ShapeDtypeStruct((g, 2, nb, n), F32)],
        scratch_shapes=[pltpu.VMEM((q, rows, LANES), BF16), pltpu.VMEM((gb, rows, qc), BF16),
                        pltpu.VMEM((gb, rows, qc), F32),
                        pltpu.VMEM((gb, nb * (nchunk + S5_SEQ_PAD), n), F32),
                        pltpu.VMEM((gb, nb * (nchunk + S5_SEQ_PAD), n), F32)],
        compiler_params=_cparams(("parallel",)),
        name="s5_scan",
    )(h, tz, ws_re, ws_im, wc_re, wc_im, aq)


def _s5_step_kernel(u_ref, hre_ref, him_ref, ab_ref, bbt_ref, cre_ref, cim_ref, y_ref, ore_ref, oim_ref):
    gb = u_ref.shape[0]
    ch = S5_CH
    for g in range(gb):
        u = u_ref[g].astype(BF16)
        bb_re = bbt_ref[g, 0:ch, :].astype(BF16)
        bb_im = bbt_ref[g, ch:2 * ch, :].astype(BF16)
        ar = ab_ref[g, 0:1, :]
        ai = ab_ref[g, 1:2, :]
        hr0 = hre_ref[g]
        hi0 = him_ref[g]
        hr = _dot(u, bb_re) + (ar * hr0 - ai * hi0)
        hi = _dot(u, bb_im) + (ar * hi0 + ai * hr0)
        ore_ref[g] = hr
        oim_ref[g] = hi
        y_ref[g] = (_dot_nt(hr.astype(BF16), cre_ref[g].astype(BF16))
                    - _dot_nt(hi.astype(BF16), cim_ref[g].astype(BF16)))


def s5_step(u_t, h_re, h_im, prep, c_re, c_im):
    ab, bbt = prep[6], prep[7]
    g, b, ch = u_t.shape
    n = S5_N
    gb = S5_GB

    def blk(*dims):
        return pl.BlockSpec((gb,) + dims, lambda i: (i,) + (0,) * len(dims))

    return pl.pallas_call(
        _s5_step_kernel,
        grid=(g // gb,),
        in_specs=[blk(b, ch), blk(b, n), blk(b, n), blk(2, n), blk(2 * ch, n), blk(ch, n), blk(ch, n)],
        out_specs=[blk(b, ch), blk(b, n), blk(b, n)],
        out_shape=[jax.ShapeDtypeStruct((g, b, ch), F32), jax.ShapeDtypeStruct((g, b, n), F32),
                   jax.ShapeDtypeStruct((g, b, n), F32)],
        compiler_params=_cparams(("parallel",)),
        name="s5_step",
    )(u_t, h_re, h_im, ab, bbt, c_re, c_im)


def _s5_head_kernel(y_ref, u_ref, d_ref, w_ref, b_ref, nw_ref, o_ref):
    y = y_ref[...] + d_ref[...] * u_ref[...]
    g = _gelu_tanh(y)
    gate = _sigmoid(_dot(g.astype(BF16), w_ref[...].astype(BF16)) + b_ref[...])
    o_ref[...] = _rms(g * gate, nw_ref[...]).astype(o_ref.dtype)


def s5_head(y, h, d, glu_w, glu_b, norm_w, *, tm):
    m, ds = y.shape
    row = lambda a: a.reshape(1, ds)
    vec = pl.BlockSpec((1, ds), lambda i: (0, 0))
    return pl.pallas_call(
        _s5_head_kernel,
        grid=(m // tm,),
        in_specs=[pl.BlockSpec((tm, ds), lambda i: (i, 0)), pl.BlockSpec((tm, ds), lambda i: (i, 0)), vec,
                  pl.BlockSpec((ds, ds), lambda i: (0, 0)), vec, vec],
        out_specs=pl.BlockSpec((tm, ds), lambda i: (i, 0)),
        out_shape=jax.ShapeDtypeStruct((m, ds), BF16),
        compiler_params=_cparams(("parallel",)),
        name="s5_head",
    )(y, h, row(d), glu_w, row(glu_b), row(norm_w))


SSD_CHUNKS_PER_STEP = 4


def _pair_select(first, col0, col1, shape):
    return jnp.where(first, jnp.broadcast_to(col0, shape), jnp.broadcast_to(col1, shape))


def _ssd_chunk_kernel(*refs, d_inner, nheads, n_xparts):
    xparts = refs[:n_xparts]
    (z_ref, dt_ref, cw_ref, cb_ref, dtb_ref, alog_ref, dvec_ref, nw_ref,
     out_ref, ssm_ref, conv_ref, state_sc, xpad_sc, y_sc) = refs[n_xparts:]
    c = pl.program_id(1)
    q = M2_CHUNK
    hp = M2_HEADDIM
    ns = M2_DSTATE
    heads_per_group = nheads // M2_NGROUPS
    halo = 8

    @pl.when(c == 0)
    def _():
        state_sc[...] = jnp.zeros_like(state_sc)
        xpad_sc[0:halo, :] = jnp.zeros((halo, xpad_sc.shape[1]), F32)

    nrows = z_ref.shape[0]
    wpart = xparts[0].shape[1]
    for i, xr in enumerate(xparts):
        xpad_sc[halo:halo + nrows, i * wpart:(i + 1) * wpart] = xr[...]
    cw = cw_ref[...]
    a = -jnp.exp(alog_ref[...])
    row = lax.broadcasted_iota(jnp.int32, (q, q), 0)
    col = lax.broadcasted_iota(jnp.int32, (q, q), 1)
    causal = row >= col
    tri = jnp.where(causal, 1.0, 0.0).astype(BF16)
    first = col < hp
    first_rows = row < hp

    for r0 in range(0, nrows, q):
        conv = cb_ref[...] + cw[M2_CONV - 1:M2_CONV, :] * xpad_sc[halo + r0:halo + r0 + q, :]
        for k in range(1, M2_CONV):
            conv = conv + cw[M2_CONV - 1 - k:M2_CONV - k, :] * xpad_sc[halo + r0 - k:halo + r0 - k + q, :]
        xc = _silu(conv)

        dt = _softplus(dt_ref[r0:r0 + q, :] + dtb_ref[...])
        da = dt * a
        d_hi, d_mid, d_lo = _split3(da)
        acum = _dot(tri, d_hi) + (_dot(tri, d_mid) + _dot(tri, d_lo))
        acum_t = acum.T
        alast = acum[q - 1:q, :]

        for pr in range(nheads // 2):
            grp = (2 * pr) // heads_per_group
            b_bf = xc[:, d_inner + grp * ns:d_inner + (grp + 1) * ns].astype(BF16)
            c_bf = xc[:, d_inner + (M2_NGROUPS + grp) * ns:d_inner + (M2_NGROUPS + grp + 1) * ns].astype(BF16)
            cb = _dot_nt(c_bf, b_bf)
            xpair = xc[:, pr * 2 * hp:(pr + 1) * 2 * hp]
            h0, h1 = 2 * pr, 2 * pr + 1
            acol = [acum[:, h:h + 1] for h in (h0, h1)]
            m = []
            for k, h in enumerate((h0, h1)):
                seg = jnp.broadcast_to(acol[k], (q, q)) - jnp.broadcast_to(acum_t[h:h + 1, :], (q, q))
                lmat = jnp.exp(jnp.where(causal, seg, -1e30))
                m.append((cb * lmat).astype(BF16))
            dtp = _pair_select(first, dt[:, h0:h0 + 1], dt[:, h1:h1 + 1], (q, q))
            xdt = xpair * dtp
            xdt_bf = xdt.astype(BF16)
            y_diag = jnp.where(first, _dot(m[0], xdt_bf), _dot(m[1], xdt_bf))
            dec_end = _pair_select(first, jnp.exp(alast[:, h0:h0 + 1] - acol[0]),
                                   jnp.exp(alast[:, h1:h1 + 1] - acol[1]), (q, q))
            xw_t = (xdt * dec_end).T.astype(BF16)
            chunk_state = _dot(xw_t, b_bf)
            rows = pl.ds(pr * 2 * hp, 2 * hp)
            prev = state_sc[rows, :]
            y_off = _dot_nt(c_bf, prev.astype(BF16)) * _pair_select(first, jnp.exp(acol[0]), jnp.exp(acol[1]),
                                                                     (q, q))
            sdec = jnp.where(first_rows, jnp.broadcast_to(jnp.exp(alast[:, h0:h0 + 1]), (q, q)),
                             jnp.broadcast_to(jnp.exp(alast[:, h1:h1 + 1]), (q, q)))
            state_sc[rows, :] = prev * sdec + chunk_state
            y_sc[:, pr * 2 * hp:(pr + 1) * 2 * hp] = (y_diag + y_off
                                                       + dvec_ref[:, pr * 2 * hp:(pr + 1) * 2 * hp] * xpair)

        out_ref[r0:r0 + q, :] = _rms(y_sc[...] * _silu(z_ref[r0:r0 + q, :]), nw_ref[...]).astype(out_ref.dtype)

    xpad_sc[0:halo, :] = xpad_sc[nrows:nrows + halo, :]

    @pl.when(c == pl.num_programs(1) - 1)
    def _():
        ssm_ref[...] = state_sc[...]
        conv_ref[...] = xpad_sc[halo + nrows - (M2_CONV - 1):halo + nrows, :]


def ssd_prompt(h, dt_raw, nb, seq, p, *, d_s5, d_inner, nheads):
    q = M2_CHUNK
    rows_step = SSD_CHUNKS_PER_STEP * q if seq % (SSD_CHUNKS_PER_STEP * q) == 0 else q
    nc = seq // rows_step
    conv_dim = d_inner + 2 * M2_NGROUPS * M2_DSTATE
    xw = 512
    xoff = d_s5 + d_inner
    assert d_s5 % d_inner == 0 and xoff % xw == 0 and conv_dim % xw == 0
    assert M2_CHUNK == 2 * M2_HEADDIM == M2_DSTATE == LANES
    zblk = d_s5 // d_inner
    n_xparts = conv_dim // xw
    m = nb * seq
    pad = lambda v: jnp.pad(v, (0, LANES - v.shape[0])).reshape(1, LANES)
    dvec = jnp.repeat(p['m2_d'], M2_HEADDIM).reshape(1, d_inner)
    vec = lambda n: pl.BlockSpec((1, n), lambda b, c: (0, 0))
    tok = lambda w, j: pl.BlockSpec((rows_step, w), lambda b, c, j=j: (b * nc + c, j))
    out, ssm, conv = pl.pallas_call(
        functools.partial(_ssd_chunk_kernel, d_inner=d_inner, nheads=nheads, n_xparts=n_xparts),
        grid=(nb, nc),
        in_specs=[tok(xw, xoff // xw + i) for i in range(n_xparts)] + [tok(d_inner, zblk), tok(LANES, 0),
                  pl.BlockSpec((M2_CONV, conv_dim), lambda b, c: (0, 0)), vec(conv_dim), vec(LANES), vec(LANES),
                  vec(d_inner), vec(d_inner)],
        out_specs=[tok(d_inner, 0),
                   pl.BlockSpec((None, nheads * M2_HEADDIM, M2_DSTATE), lambda b, c: (b, 0, 0)),
                   pl.BlockSpec((None, M2_CONV - 1, conv_dim), lambda b, c: (b, 0, 0))],
        out_shape=[jax.ShapeDtypeStruct((m, d_inner), BF16),
                   jax.ShapeDtypeStruct((nb, nheads * M2_HEADDIM, M2_DSTATE), F32),
                   jax.ShapeDtypeStruct((nb, M2_CONV - 1, conv_dim), F32)],
        scratch_shapes=[pltpu.VMEM((nheads * M2_HEADDIM, M2_DSTATE), F32),
                        pltpu.VMEM((rows_step + 8, conv_dim), F32),
                        pltpu.VMEM((q, d_inner), F32)],
        compiler_params=_cparams(("parallel", "arbitrary")),
        name="ssd_chunk",
    )(*([h] * n_xparts), h, dt_raw, p['m2_conv_w'], p['m2_conv_b'].reshape(1, conv_dim), pad(p['m2_dt_bias']),
      pad(p['m2_a_log']), dvec, p['m2_norm_w'].reshape(1, d_inner))
    return out, ssm.reshape(nb, nheads, M2_HEADDIM, M2_DSTATE), conv


SSD_STEP_SEQS = 8


def _ssd_step_kernel(*refs, d_inner, nheads, n_xparts):
    xparts = refs[:n_xparts]
    (z_ref, dt_ref, cs0_ref, cs1_ref, cs2_ref, cw_ref, cb_ref, dtb_ref, alog_ref, dvec_ref, nw_ref, st_ref,
     out_ref, so_ref, lhs_sc, bfull_sc, ct_sc, yt_sc, xs_sc) = refs[n_xparts:]
    i = pl.program_id(0)
    nb = z_ref.shape[0]
    ns = M2_DSTATE
    rows_g = (nheads // M2_NGROUPS) * M2_HEADDIM

    @pl.when(i == 0)
    def _():
        cw = cw_ref[...]
        xbc = jnp.concatenate([xr[...] for xr in xparts], axis=1)
        conv = (cb_ref[...] + cw[3:4, :] * xbc + cw[2:3, :] * cs2_ref[...]
                + cw[1:2, :] * cs1_ref[...] + cw[0:1, :] * cs0_ref[...])
        xc = _silu(conv)
        dt = _softplus(dt_ref[...] + dtb_ref[...])
        dec = jnp.exp(dt * (-jnp.exp(alog_ref[...])))
        hrow = lax.broadcasted_iota(jnp.int32, (LANES, d_inner), 0)
        hcol = lax.broadcasted_iota(jnp.int32, (LANES, d_inner), 1)
        expand = jnp.where(hcol // M2_HEADDIM == hrow, 1.0, 0.0).astype(BF16)

        def expand_heads(v):
            a, b_, c = _split3(v)
            return _dot(a, expand) + (_dot(b_, expand) + _dot(c, expand))

        xs = xc[:, :d_inner]
        xs_sc[...] = xs
        xdt_t = (xs * expand_heads(dt)).T
        d_hi, d_mid, d_lo = _split3(expand_heads(dec).T)
        for g in range(M2_NGROUPS):
            r = slice(g * rows_g, (g + 1) * rows_g)
            lhs_sc[g] = jnp.concatenate([xdt_t[r].astype(BF16), d_hi[r], d_mid[r], d_lo[r]], axis=1)
            b_g = xc[:, d_inner + g * ns:d_inner + (g + 1) * ns]
            bfull_sc[g] = jnp.concatenate([b_g, jnp.zeros_like(b_g)], axis=1)
            c_g = xc[:, d_inner + (M2_NGROUPS + g) * ns:d_inner + (M2_NGROUPS + g + 1) * ns]
            ct_sc[g] = c_g.T
        yt_sc[...] = jnp.zeros_like(yt_sc)

    row_id = lax.broadcasted_iota(jnp.int32, (nb, 2 * ns), 0)
    lane_id = lax.broadcasted_iota(jnp.int32, (nb, 2 * ns), 1)
    col_id = lax.broadcasted_iota(jnp.int32, (ns, nb), 1)
    for j in range(st_ref.shape[0]):
        b = i * st_ref.shape[0] + j
        r_bot = jnp.where((row_id == b) & (lane_id >= ns), 1.0, 0.0).astype(BF16)
        for g in range(M2_NGROUPS):
            r = pl.ds(g * rows_g, rows_g)
            r_top = jnp.where(row_id == b, bfull_sc[g], 0.0).astype(BF16)
            rhs = jnp.concatenate([r_top, r_bot, r_bot, r_bot], axis=0)
            o = _dot(lhs_sc[g], rhs)
            hnew = st_ref[j, r, :] * o[:, ns:] + o[:, :ns]
            so_ref[j, r, :] = hnew
            cm = jnp.where(col_id == b, ct_sc[g], 0.0).astype(BF16)
            yt_sc[r, :] += _dot(hnew.astype(BF16), cm)

    @pl.when(i == pl.num_programs(0) - 1)
    def _():
        y = yt_sc[...].T + dvec_ref[...] * xs_sc[...]
        out_ref[...] = _rms(y * _silu(z_ref[...]), nw_ref[...]).astype(out_ref.dtype)


def ssd_sample(h, dt_raw, state, conv_state, p, *, d_s5, d_inner, nheads):
    nb = h.shape[0]
    conv_dim = d_inner + 2 * M2_NGROUPS * M2_DSTATE
    xw = 512
    xoff = d_s5 + d_inner
    assert nb == LANES and M2_DSTATE == LANES and M2_CONV == 4
    assert xoff % xw == 0 and conv_dim % xw == 0 and d_s5 % d_inner == 0 and nb % SSD_STEP_SEQS == 0
    n_xparts = conv_dim // xw
    rows = nheads * M2_HEADDIM
    rows_g = rows // M2_NGROUPS
    pad = lambda v: jnp.pad(v, (0, LANES - v.shape[0])).reshape(1, LANES)
    dvec = jnp.repeat(p['m2_d'], M2_HEADDIM).reshape(1, d_inner)
    full = lambda a, b, j=0: pl.BlockSpec((a, b), lambda i, j=j: (0, j))
    st_spec = pl.BlockSpec((SSD_STEP_SEQS, rows, M2_DSTATE), lambda i: (i, 0, 0))
    out, new_state = pl.pallas_call(
        functools.partial(_ssd_step_kernel, d_inner=d_inner, nheads=nheads, n_xparts=n_xparts),
        grid=(nb // SSD_STEP_SEQS,),
        in_specs=[full(nb, xw, xoff // xw + k) for k in range(n_xparts)]
        + [full(nb, d_inner, d_s5 // d_inner), full(nb, LANES)]
        + [full(nb, conv_dim)] * 3
        + [full(M2_CONV, conv_dim), full(1, conv_dim), full(1, LANES), full(1, LANES), full(1, d_inner),
           full(1, d_inner), st_spec],
        out_specs=[full(nb, d_inner), st_spec],
        out_shape=[jax.ShapeDtypeStruct((nb, d_inner), BF16), jax.ShapeDtypeStruct((nb, rows, M2_DSTATE), F32)],
        scratch_shapes=[pltpu.VMEM((M2_NGROUPS, rows_g, 4 * nb), BF16),
                        pltpu.VMEM((M2_NGROUPS, nb, 2 * M2_DSTATE), F32),
                        pltpu.VMEM((M2_NGROUPS, M2_DSTATE, nb), F32),
                        pltpu.VMEM((rows, nb), F32),
                        pltpu.VMEM((nb, d_inner), F32)],
        compiler_params=_cparams(("arbitrary",)),
        name="ssd_step",
    )(*([h] * n_xparts), h, dt_raw, conv_state[:, 0], conv_state[:, 1], conv_state[:, 2],
      p['m2_conv_w'], p['m2_conv_b'].reshape(1, conv_dim), pad(p['m2_dt_bias']), pad(p['m2_a_log']),
      dvec, p['m2_norm_w'].reshape(1, d_inner), state.reshape(nb, rows, M2_DSTATE))
    xbc = lax.slice_in_dim(h, xoff, xoff + conv_dim, axis=1)
    new_conv = jnp.concatenate([conv_state[:, 1:], xbc[:, None, :]], axis=1)
    return out, new_state.reshape(state.shape), new_conv


def _softmax_rows(s):
    e = jnp.exp(s - jnp.max(s, axis=-1, keepdims=True))
    return e / jnp.sum(e, axis=-1, keepdims=True)


def _attn_kernel(q_ref, k_ref, v_ref, o_ref, *, scale):
    s = _dot_nt(q_ref[...].astype(BF16), k_ref[...].astype(BF16)) * scale
    o_ref[...] = _dot(_softmax_rows(s).astype(BF16), v_ref[...].astype(BF16)).astype(o_ref.dtype)


def attention_prompt(q, k, v, nb, seq, mem, heads, *, tq):
    d = q.shape[1]
    hd = d // heads
    nq = seq // tq
    kv_spec = pl.BlockSpec((mem, hd), lambda b, h, i: (b, h))
    q_spec = pl.BlockSpec((tq, hd), lambda b, h, i: (b * nq + i, h))
    return pl.pallas_call(
        functools.partial(_attn_kernel, scale=hd ** -0.5),
        grid=(nb, heads, nq),
        in_specs=[q_spec, kv_spec, kv_spec],
        out_specs=q_spec,
        out_shape=jax.ShapeDtypeStruct(q.shape, BF16),
        compiler_params=_cparams(("parallel", "parallel", "parallel")),
        name="attention_prompt",
    )(q, k, v)


ATTN_STEP_SEQS = 4


ATTN_STEP_ROWS = 64


def _attn_step_kernel(q_ref, k_ref, v_ref, o_ref, s_sc, *, scale):
    nseq, mem, heads, hd = k_ref.shape
    ch = ATTN_STEP_ROWS
    pack = 8 // heads
    for j in range(nseq):
        q = jnp.concatenate([q_ref[j]] * pack, axis=0)

        def score(c, mx):
            rows = pl.ds(pl.multiple_of(c * ch, ch), ch)
            k = k_ref[j, rows].reshape(ch // pack, pack * heads, hd)
            s = jnp.sum(k * q, axis=-1, keepdims=True) * scale
            s_sc[pl.ds(pl.multiple_of(c * (ch // pack), ch // pack), ch // pack)] = s
            return jnp.maximum(mx, jnp.max(s, axis=0))

        mx = lax.fori_loop(0, mem // ch, score, jnp.full((pack * heads, 1), NEG, F32))
        mx1 = mx[0:heads]
        for i in range(1, pack):
            mx1 = jnp.maximum(mx1, mx[i * heads:(i + 1) * heads])
        mx = jnp.concatenate([mx1] * pack, axis=0)

        def accum(c, carry):
            den, acc = carry
            rows = pl.ds(pl.multiple_of(c * ch, ch), ch)
            v = v_ref[j, rows].reshape(ch // pack, pack * heads, hd)
            e = jnp.exp(s_sc[pl.ds(pl.multiple_of(c * (ch // pack), ch // pack), ch // pack)] - mx)
            return den + jnp.sum(e, axis=0), acc + jnp.sum(e * v, axis=0)

        den, acc = lax.fori_loop(0, mem // ch, accum,
                                 (jnp.zeros((pack * heads, 1), F32), jnp.zeros((pack * heads, hd), F32)))
        den1, acc1 = den[0:heads], acc[0:heads]
        for i in range(1, pack):
            den1 = den1 + den[i * heads:(i + 1) * heads]
            acc1 = acc1 + acc[i * heads:(i + 1) * heads]
        o_ref[j] = acc1 / den1


def attention_sample(q, k_cache, v_cache):
    b, mem, heads, hd = k_cache.shape
    nseq = ATTN_STEP_SEQS
    q_spec = pl.BlockSpec((nseq, heads, hd), lambda i: (i, 0, 0))
    kv_spec = pl.BlockSpec((nseq, mem, heads, hd), lambda i: (i, 0, 0, 0))
    out = pl.pallas_call(
        functools.partial(_attn_step_kernel, scale=hd ** -0.5),
        grid=(b // nseq,),
        in_specs=[q_spec, kv_spec, kv_spec],
        out_specs=q_spec,
        out_shape=jax.ShapeDtypeStruct((b, heads, hd), F32),
        scratch_shapes=[pltpu.VMEM((mem * heads // 8, 8, 1), F32)],
        compiler_params=_cparams(("parallel",)),
        name="attention_step",
    )(q.reshape(b, heads, hd), k_cache, v_cache)
    return out.reshape(b, heads * hd)


NEG = -1e30


INFO_G1, INFO_G2, INFO_E1, INFO_E2 = 0, 1, 2, 3


def _lane_pack(lane, values):
    out = 0.0
    for k, v in values:
        out = jnp.where(lane == k, v, out)
    return out


def _router_kernel(x_ref, nw_ref, wr_ref, br_ref, sel_ref, info_ref, *, n_experts):
    xn = _rms(x_ref[...], nw_ref[...])
    x_hi, x_lo = _split2(xn)
    w_hi, w_lo = _split2(wr_ref[...])
    hi_terms = _dot(x_hi, jnp.concatenate([w_hi, w_lo], axis=1))
    logits = hi_terms[:, :LANES] + (hi_terms[:, LANES:] + _dot(x_lo, w_hi)) + br_ref[...]
    lane = lax.broadcasted_iota(jnp.int32, logits.shape, 1)
    big = jnp.int32(2 ** 30)
    is_c = (lane >= n_experts) & (lane < n_experts + MOE_GROUPS)
    lc = jnp.where(is_c, logits, NEG)
    cmax = jnp.max(lc, axis=-1, keepdims=True)
    gsel = jnp.min(jnp.where(lc == cmax, lane, big), axis=-1, keepdims=True) - n_experts
    gate_c = 1.0 / jnp.sum(jnp.where(is_c, jnp.exp(lc - cmax), 0.0), axis=-1, keepdims=True)
    in_group = (lane < n_experts) & (lane // MOE_PER_GROUP == gsel)
    lf = jnp.where(in_group, logits, NEG)
    t1 = jnp.max(lf, axis=-1, keepdims=True)
    i1 = jnp.min(jnp.where(lf == t1, lane, big), axis=-1, keepdims=True)
    lf2 = jnp.where(lane == i1, NEG, lf)
    t2 = jnp.max(lf2, axis=-1, keepdims=True)
    i2 = jnp.min(jnp.where(lf2 == t2, lane, big), axis=-1, keepdims=True)
    r = jnp.exp(t2 - t1)
    g1 = gate_c / (1.0 + r)
    g2 = gate_c * r / (1.0 + r)
    sel_ref[...] = jnp.where((lane == i1) | (lane == i2), 1.0, 0.0)
    info_ref[...] = _lane_pack(lane, ((INFO_G1, g1), (INFO_G2, g2), (INFO_E1, i1.astype(F32)),
                                      (INFO_E2, i2.astype(F32))))


def moe_router(x, norm_w, w_coarse, b_coarse, w_fine, b_fine, *, tm):
    m, d = x.shape
    e = w_fine.shape[1]
    padw = LANES - e - MOE_GROUPS
    wr = jnp.concatenate([w_fine, w_coarse, jnp.zeros((d, padw), F32)], axis=1)
    br = jnp.concatenate([b_fine, b_coarse, jnp.zeros((padw,), F32)]).reshape(1, LANES)
    tok = pl.BlockSpec((tm, LANES), lambda i: (i, 0))
    return pl.pallas_call(
        functools.partial(_router_kernel, n_experts=e),
        grid=(m // tm,),
        in_specs=[pl.BlockSpec((tm, d), lambda i: (i, 0)), pl.BlockSpec((1, d), lambda i: (0, 0)),
                  pl.BlockSpec((d, LANES), lambda i: (0, 0)), pl.BlockSpec((1, LANES), lambda i: (0, 0))],
        out_specs=[tok, tok],
        out_shape=[jax.ShapeDtypeStruct((m, LANES), F32), jax.ShapeDtypeStruct((m, LANES), F32)],
        compiler_params=_cparams(("parallel",)),
        name="moe_router",
    )(x, norm_w.reshape(1, d), wr, br)


def _moe_rank_kernel(sel_ref, info_ref, rank_ref, counts_ref, carry_sc):
    i = pl.program_id(0)

    @pl.when(i == 0)
    def _():
        carry_sc[...] = jnp.zeros_like(carry_sc)

    sel = sel_ref[...]
    tm = sel.shape[0]
    row = lax.broadcasted_iota(jnp.int32, (tm, tm), 0)
    col = lax.broadcasted_iota(jnp.int32, (tm, tm), 1)
    before = jnp.where(row > col, 1.0, 0.0).astype(BF16)
    rank = _dot(before, sel.astype(BF16)) + carry_sc[...]
    info = info_ref[...]
    lane = lax.broadcasted_iota(jnp.int32, sel.shape, 1)
    e1 = info[:, INFO_E1:INFO_E1 + 1].astype(jnp.int32)
    e2 = info[:, INFO_E2:INFO_E2 + 1].astype(jnp.int32)
    r1 = jnp.sum(jnp.where(lane == e1, rank, 0.0), axis=-1, keepdims=True)
    r2 = jnp.sum(jnp.where(lane == e2, rank, 0.0), axis=-1, keepdims=True)
    rank_ref[...] = _lane_pack(lane, ((0, r1), (1, r2)))
    carry_sc[...] += jnp.sum(sel, axis=0, keepdims=True)

    @pl.when(i == pl.num_programs(0) - 1)
    def _():
        counts_ref[...] = carry_sc[...]


def _moe_plan_kernel(rank_ref, info_ref, counts_ref, dest_ref, tiles_ref, *, tile, n_experts):
    counts = counts_ref[...]
    ntile_e = jnp.floor((counts + (tile - 1)) * (1.0 / tile))
    padded = jnp.broadcast_to(ntile_e * tile, (8, LANES))
    r = lax.broadcasted_iota(jnp.int32, (LANES, LANES), 0)
    c = lax.broadcasted_iota(jnp.int32, (LANES, LANES), 1)
    lower = jnp.where(r < c, 1.0, 0.0).astype(BF16)
    p_hi, p_mid, p_lo = _split3(padded)
    offs = (_dot(p_hi, lower) + (_dot(p_mid, lower) + _dot(p_lo, lower)))[0:1, :]
    ends = offs + padded[0:1, :]

    info = info_ref[...]
    rank = rank_ref[...]
    lane = lax.broadcasted_iota(jnp.int32, info.shape, 1)
    e1 = info[:, INFO_E1:INFO_E1 + 1].astype(jnp.int32)
    e2 = info[:, INFO_E2:INFO_E2 + 1].astype(jnp.int32)
    d1 = jnp.sum(jnp.where(lane == e1, offs, 0.0), axis=-1, keepdims=True) + rank[:, 0:1]
    d2 = jnp.sum(jnp.where(lane == e2, offs, 0.0), axis=-1, keepdims=True) + rank[:, 1:2]
    dest_ref[...] = _lane_pack(lane, ((0, d1), (1, d2))).astype(jnp.int32)

    ends_col = jnp.broadcast_to(ends, (LANES, LANES)).T
    start = (c * tile).astype(F32)
    n_before = jnp.sum(jnp.where((ends_col <= start) & (r < n_experts), 1.0, 0.0), axis=0, keepdims=True)
    lane1 = lax.broadcasted_iota(jnp.int32, (1, LANES), 1)
    total = jnp.sum(jnp.where(lane1 == n_experts - 1, ends, 0.0), axis=-1, keepdims=True)
    tail = jnp.where(counts > 0.0, ends - tile, -1.0)
    tiles_ref[...] = jnp.zeros_like(tiles_ref)
    tiles_ref[0:1, :] = jnp.minimum(n_before, n_experts - 1.0).astype(jnp.int32)
    tiles_ref[1:2, :] = jnp.broadcast_to(total * (1.0 / tile), (1, LANES)).astype(jnp.int32)
    tiles_ref[2:3, :] = tail.astype(jnp.int32)


def _moe_dispatch_kernel(dest_ref, tails_ref, ntiles_ref, *rest, tile, n_experts, group_steps, group_offsets):
    n_groups = len(group_offsets)
    nw_ref = rest[0]
    x_refs = rest[1:1 + n_groups]
    xs_ref, zero_sc, xn_sc, sems = rest[1 + n_groups:]
    i = pl.program_id(0)
    n_tiles_max = xs_ref.shape[0] // tile

    def fill_copy(j):
        return pltpu.make_async_copy(zero_sc, xs_ref.at[pl.ds(pl.multiple_of(j * tile, tile), tile)], sems.at[2])

    def tail_copy(e):
        return pltpu.make_async_copy(zero_sc, xs_ref.at[pl.ds(pl.multiple_of(tails_ref[e], tile), tile)],
                                     sems.at[1])

    @pl.when(i == 0)
    def _():
        zero_sc[...] = jnp.zeros_like(zero_sc)

        def fill(j, _):
            fill_copy(j).start()
            return 0

        def clear(e, _):
            @pl.when(tails_ref[e] >= 0)
            def _():
                tail_copy(e).start()
            return 0

        def clear_wait(e, _):
            @pl.when(tails_ref[e] >= 0)
            def _():
                tail_copy(e).wait()
            return 0

        lax.fori_loop(ntiles_ref[0], n_tiles_max, fill, 0)
        lax.fori_loop(0, n_experts, clear, 0)
        lax.fori_loop(0, n_experts, clear_wait, 0)

    for g, x_ref in enumerate(x_refs):
        @pl.when((i >= group_steps[g]) & (i < group_steps[g + 1]))
        def _(g=g, x_ref=x_ref):
            tm = x_ref.shape[0]
            xn_ref = xn_sc.at[pl.ds(0, tm)]
            xn_ref[...] = _rms(x_ref[...], nw_ref[...])
            base = 2 * (group_offsets[g] + (i - group_steps[g]) * tm)
            for r in range(tm):
                for k in range(2):
                    pltpu.make_async_copy(xn_ref.at[pl.ds(r, 1)], xs_ref.at[pl.ds(dest_ref[base + 2 * r + k], 1)],
                                          sems.at[0]).start(priority=k)
            for _ in range(2):
                pltpu.make_async_copy(xn_ref, xs_ref.at[pl.ds(0, tm)], sems.at[0]).wait()

    @pl.when(i == pl.num_programs(0) - 1)
    def _():
        def fill_wait(j, _):
            fill_copy(j).wait()
            return 0

        lax.fori_loop(ntiles_ref[0], n_tiles_max, fill_wait, 0)


def _moe_expert_kernel(texp_ref, ntiles_ref, xs_ref, wg_hbm, wu_hbm, wd_hbm, ys_ref,
                       wg_sc, wu_sc, wd_sc, slot_sm, sems):
    i = pl.program_id(0)
    n = ntiles_ref[0]

    def weight_copies(e, slot):
        return [pltpu.make_async_copy(w.at[e], sc.at[slot], sems.at[slot])
                for w, sc in ((wg_hbm, wg_sc), (wu_hbm, wu_sc), (wd_hbm, wd_sc))]

    @pl.when(i < n)
    def _():
        e = texp_ref[i]
        first = (i == 0) | (texp_ref[jnp.maximum(i - 1, 0)] != e)

        @pl.when(i == 0)
        def _():
            slot_sm[0] = 0
            for c in weight_copies(e, 0):
                c.start()

        @pl.when(first & (i > 0))
        def _():
            slot_sm[0] = 1 - slot_sm[0]

        slot = slot_sm[0]

        @pl.when(first)
        def _():
            for c in weight_copies(e, slot):
                c.wait()
            j = lax.while_loop(lambda j: (j < n) & (texp_ref[jnp.minimum(j, n - 1)] == e), lambda j: j + 1, i + 1)

            @pl.when(j < n)
            def _():
                for c in weight_copies(texp_ref[jnp.minimum(j, n - 1)], 1 - slot):
                    c.start()

        x = xs_ref[...].astype(BF16)
        hg = _dot(x, wg_sc[slot].astype(BF16))
        hu = _dot(x, wu_sc[slot].astype(BF16))
        ys_ref[...] = _dot((_silu(hg) * hu).astype(BF16), wd_sc[slot].astype(BF16))


def _moe_combine_kernel(dest_ref, ys_ref, x_ref, info_ref, fw_ref, y_ref, buf_sc, sems):
    s = pl.program_id(0)
    n_blocks = pl.num_programs(0) - 1
    tm = x_ref.shape[0]
    slot = s % 2

    @pl.when(s < n_blocks)
    def _():
        for r in range(tm):
            for k in range(2):
                pltpu.make_async_copy(ys_ref.at[pl.ds(dest_ref[2 * (s * tm + r) + k], 1)],
                                      buf_sc.at[slot, k, pl.ds(r, 1)], sems.at[slot]).start(priority=k)

    @pl.when(s > 0)
    def _():
        prev = 1 - slot
        for k in range(2):
            pltpu.make_async_copy(ys_ref.at[pl.ds(0, tm)], buf_sc.at[prev, k], sems.at[prev]).wait()
        info = info_ref[...]
        y = (x_ref[...] + info[:, INFO_G1:INFO_G1 + 1] * buf_sc[prev, 0]
             + info[:, INFO_G2:INFO_G2 + 1] * buf_sc[prev, 1])
        y_ref[...] = _rms(y, fw_ref[...])


MOE_TILE = 256
MOE_TOKENS_PER_STEP = 256


def _largest_tile(m, cap):
    return max(t for t in range(8, cap + 1, 8) if m % t == 0)


def moe_routed_final(x_list, p, final_w):
    d = x_list[0].shape[1]
    ne, _, f = p['moe_w_gate'].shape
    sizes = [x.shape[0] for x in x_list]
    m = sum(sizes)
    tile = min(MOE_TILE, m)
    routed = [moe_router(x, p['norm_ffn_w'], p['router_coarse_w'], p['router_coarse_b'],
                         p['router_fine_w'], p['router_fine_b'], tm=min(x.shape[0], 512)) for x in x_list]
    sel = jnp.concatenate([r[0] for r in routed], axis=0)
    info = jnp.concatenate([r[1] for r in routed], axis=0)
    tm = _largest_tile(m, 1024)
    tok = pl.BlockSpec((tm, LANES), lambda i: (i, 0))
    rank, counts = pl.pallas_call(
        _moe_rank_kernel,
        grid=(m // tm,),
        in_specs=[tok, tok],
        out_specs=[tok, pl.BlockSpec((1, LANES), lambda i: (0, 0))],
        out_shape=[jax.ShapeDtypeStruct((m, LANES), F32), jax.ShapeDtypeStruct((1, LANES), F32)],
        scratch_shapes=[pltpu.VMEM((1, LANES), F32)],
        compiler_params=_cparams(("arbitrary",)),
        name="moe_rank",
    )(sel, info)
    n_tiles_max = (2 * m) // tile + ne
    assert n_tiles_max <= LANES
    dest, tiles = pl.pallas_call(
        functools.partial(_moe_plan_kernel, tile=tile, n_experts=ne),
        out_shape=[jax.ShapeDtypeStruct((m, LANES), jnp.int32), jax.ShapeDtypeStruct((8, LANES), jnp.int32)],
        compiler_params=pltpu.CompilerParams(vmem_limit_bytes=VMEM_LIMIT),
        name="moe_plan",
    )(rank, info, counts)
    dest_flat = dest[:, :2].reshape(2 * m)
    tile_expert = tiles[0, :n_tiles_max]
    n_tiles = tiles[1, :1]
    tails = tiles[2, :ne]

    rows = n_tiles_max * tile

    tds = [min(mg, 2 * MOE_TOKENS_PER_STEP) for mg in sizes]
    group_steps = [0]
    for mg, td in zip(sizes, tds):
        group_steps.append(group_steps[-1] + mg // td)
    group_offsets = [sum(sizes[:g]) for g in range(len(sizes))]

    def group_spec(g):
        first, last = group_steps[g], group_steps[g + 1] - 1
        return pl.BlockSpec((tds[g], d), lambda i, *_: (jnp.clip(i, first, last) - first, 0))

    xs = pl.pallas_call(
        functools.partial(_moe_dispatch_kernel, tile=tile, n_experts=ne, group_steps=tuple(group_steps),
                          group_offsets=tuple(group_offsets)),
        grid_spec=pltpu.PrefetchScalarGridSpec(
            num_scalar_prefetch=3, grid=(group_steps[-1],),
            in_specs=[pl.BlockSpec((1, d), lambda i, *_: (0, 0))] + [group_spec(g) for g in range(len(sizes))],
            out_specs=pl.BlockSpec(memory_space=pl.ANY),
            scratch_shapes=[pltpu.VMEM((tile, d), F32), pltpu.VMEM((max(tds), d), F32),
                            pltpu.SemaphoreType.DMA((3,))]),
        out_shape=jax.ShapeDtypeStruct((rows, d), F32),
        compiler_params=_cparams(("arbitrary",)),
        name="moe_dispatch",
    )(dest_flat, tails, n_tiles, p['norm_ffn_w'].reshape(1, d), *x_list)

    def tile_idx(i, te, nt):
        return jnp.minimum(i, nt[0] - 1)

    ys = pl.pallas_call(
        _moe_expert_kernel,
        grid_spec=pltpu.PrefetchScalarGridSpec(
            num_scalar_prefetch=2, grid=(n_tiles_max,),
            in_specs=[pl.BlockSpec((tile, d), lambda i, te, nt: (tile_idx(i, te, nt), 0)),
                      pl.BlockSpec(memory_space=pl.ANY), pl.BlockSpec(memory_space=pl.ANY),
                      pl.BlockSpec(memory_space=pl.ANY)],
            out_specs=pl.BlockSpec((tile, d), lambda i, te, nt: (tile_idx(i, te, nt), 0)),
            scratch_shapes=[pltpu.VMEM((2, d, f), F32), pltpu.VMEM((2, d, f), F32), pltpu.VMEM((2, f, d), F32),
                            pltpu.SMEM((1,), jnp.int32), pltpu.SemaphoreType.DMA((2,))]),
        out_shape=jax.ShapeDtypeStruct((rows, d), F32),
        input_output_aliases={2: 0},
        compiler_params=_cparams(("arbitrary",)),
        name="moe_experts",
    )(tile_expert, n_tiles, xs, p['moe_w_gate'], p['moe_w_up'], p['moe_w_down'])

    outs = []
    off = 0
    for x, (_, info_g), mg in zip(x_list, routed, sizes):
        tc = min(mg, MOE_TOKENS_PER_STEP)
        outs.append(pl.pallas_call(
            _moe_combine_kernel,
            grid_spec=pltpu.PrefetchScalarGridSpec(
                num_scalar_prefetch=1, grid=(mg // tc + 1,),
                in_specs=[pl.BlockSpec(memory_space=pl.ANY),
                          pl.BlockSpec((tc, d), lambda s, dref: (jnp.maximum(s - 1, 0), 0)),
                          pl.BlockSpec((tc, LANES), lambda s, dref: (jnp.maximum(s - 1, 0), 0)),
                          pl.BlockSpec((1, d), lambda s, dref: (0, 0))],
                out_specs=pl.BlockSpec((tc, d), lambda s, dref: (jnp.maximum(s - 1, 0), 0)),
                scratch_shapes=[pltpu.VMEM((2, 2, tc, d), F32), pltpu.SemaphoreType.DMA((2,))]),
            out_shape=jax.ShapeDtypeStruct((mg, d), F32),
            compiler_params=_cparams(("arbitrary",)),
            name="moe_combine",
        )(dest_flat[2 * off:2 * (off + mg)], ys, x, info_g, final_w.reshape(1, d)))
        off += mg
    return outs


def s5_prompt(h, nb, seq, prep, p):
    y, hfin = s5_scan(h, prep, nb, seq)
    out = s5_head(y, h, p['s5_d'], p['s5_glu_w'], p['s5_glu_b'], p['s5_norm_w'], tm=min(512, nb * seq))
    return out, hfin[:, 0].transpose(1, 0, 2), hfin[:, 1].transpose(1, 0, 2)


def s5_sample(h, st_re, st_im, prep, p):
    g = prep[0].shape[0]
    ch = S5_CH
    ds = g * ch
    b = h.shape[0]
    u_t = h[:, :ds].reshape(b, g, ch).transpose(1, 0, 2)
    y_t, n_re, n_im = s5_step(u_t, st_re.transpose(1, 0, 2), st_im.transpose(1, 0, 2), prep,
                              p['s5_c_re'], p['s5_c_im'])
    y = y_t.transpose(1, 0, 2).reshape(b, ds)
    out = s5_head(y, h, p['s5_d'], p['s5_glu_w'], p['s5_glu_b'], p['s5_norm_w'], tm=b)
    return out, n_re.transpose(1, 0, 2), n_im.transpose(1, 0, 2)


def _row_tile(m):
    return min(m, 1024)


def _col_tile(m, n):
    if m <= 256:
        return n
    return max(t for t in range(256, 1793, 256) if n % t == 0)


def _mixer_and_attention(x, p, s5_prep, *, nb, seq, mem_kv, xa_heads, states):
    m, d = x.shape
    g, n = p['s5_a_re'].shape
    d_s5 = g * S5_CH
    nheads = p['m2_a_log'].shape[0]
    d_inner = nheads * M2_HEADDIM
    conv_dim = d_inner + 2 * M2_NGROUPS * M2_DSTATE
    n_main = d_s5 + d_inner + conv_dim
    tm = _row_tile(m)

    w_in = p['w_in']
    w_dt = jnp.pad(w_in[:, n_main:], ((0, 0), (0, LANES - nheads)))
    tm2, tn2 = tm, _col_tile(m, d)
    h, dt_raw = fused_matmul([x], w_in, n_out=n_main, gain=p['norm_mix_w'], side_w=w_dt, tm=tm,
                             tn=_col_tile(m, n_main))

    if states is None:
        s5_out, s5_re, s5_im = s5_prompt(h, nb, seq, s5_prep, p)
        m2_out, ssm, conv = ssd_prompt(h, dt_raw, nb, seq, p, d_s5=d_s5, d_inner=d_inner, nheads=nheads)
    else:
        s5_out, s5_re, s5_im = s5_sample(h, states[0], states[1], s5_prep, p)
        m2_out, ssm, conv = ssd_sample(h, dt_raw, states[2], states[3], p, d_s5=d_s5, d_inner=d_inner,
                                       nheads=nheads)
    x1 = fused_matmul([s5_out, m2_out], p['w_out'], n_out=d, res=x, tm=tm2, tn=tn2)

    q = fused_matmul([x1], p['xa_wq'], n_out=d, gain=p['norm_xa_w'], tm=tm2, tn=tn2,
                     out_dtype=BF16 if states is None else F32)
    if states is None:
        mem = mem_kv[0].shape[0] // nb
        o = attention_prompt(q, mem_kv[0], mem_kv[1], nb, seq, mem, xa_heads, tq=min(seq, 2048))
    else:
        o = attention_sample(q, mem_kv[0], mem_kv[1])
    x2 = fused_matmul([o], p['xa_wo'], n_out=d, res=x1, tm=tm2, tn=tn2)

    return x2, s5_re, s5_im, ssm, conv


def kernel(x_prompt, x_sample, mem_prompt, state_s5_re, state_s5_im, state_ssm, state_conv, cache_mem_k, cache_mem_v, norm_mix_w, w_in, s5_a_re, s5_a_im, s5_log_dt, s5_b_re, s5_b_im, s5_c_re, s5_c_im, s5_d, s5_glu_w, s5_glu_b, s5_norm_w, m2_conv_w, m2_conv_b, m2_dt_bias, m2_a_log, m2_d, m2_norm_w, w_out, norm_xa_w, norm_mem_w, xa_wq, xa_wk, xa_wv, xa_wo, norm_ffn_w, router_coarse_w, router_coarse_b, router_fine_w, router_fine_b, moe_w_gate, moe_w_up, moe_w_down, norm_final_w):
    depth = w_in.shape[0]
    assert depth == 1, "the final norm is fused into the (only) layer"
    per_layer = dict(
        norm_mix_w=norm_mix_w, w_in=w_in, s5_a_re=s5_a_re, s5_a_im=s5_a_im, s5_log_dt=s5_log_dt,
        s5_b_re=s5_b_re, s5_b_im=s5_b_im, s5_c_re=s5_c_re, s5_c_im=s5_c_im, s5_d=s5_d, s5_glu_w=s5_glu_w,
        s5_glu_b=s5_glu_b, s5_norm_w=s5_norm_w, m2_conv_w=m2_conv_w, m2_conv_b=m2_conv_b, m2_dt_bias=m2_dt_bias,
        m2_a_log=m2_a_log, m2_d=m2_d, m2_norm_w=m2_norm_w, w_out=w_out, norm_xa_w=norm_xa_w,
        norm_mem_w=norm_mem_w, xa_wq=xa_wq, xa_wk=xa_wk, xa_wv=xa_wv, xa_wo=xa_wo, norm_ffn_w=norm_ffn_w,
        router_coarse_w=router_coarse_w, router_coarse_b=router_coarse_b, router_fine_w=router_fine_w,
        router_fine_b=router_fine_b, moe_w_gate=moe_w_gate, moe_w_up=moe_w_up, moe_w_down=moe_w_down)
    p = {k: v[0] for k, v in per_layer.items()}
    for name in ('w_in', 'w_out', 'xa_wq', 'xa_wo'):
        p[name] = p[name].astype(BF16)
    nb, seq, d = x_prompt.shape
    db, dseq, _ = x_sample.shape
    assert dseq == 1
    mem = mem_prompt.shape[1]
    xa_heads = cache_mem_k.shape[3]

    s5_prep = s5_prepare(p['s5_a_re'], p['s5_a_im'], p['s5_log_dt'], p['s5_b_re'], p['s5_b_im'],
                         p['s5_c_re'], p['s5_c_im'])

    memx = mem_prompt.reshape(nb * mem, d)
    mk = fused_matmul([memx], p['xa_wk'], n_out=d, gain=p['norm_mem_w'], tm=_row_tile(nb * mem),
                      tn=_col_tile(nb * mem, d))
    mv = fused_matmul([memx], p['xa_wv'], n_out=d, gain=p['norm_mem_w'], tm=_row_tile(nb * mem),
                      tn=_col_tile(nb * mem, d))
    xp, p_re, p_im, p_ssm, p_conv = _mixer_and_attention(
        x_prompt.reshape(nb * seq, d), p, s5_prep, nb=nb, seq=seq, mem_kv=(mk, mv), xa_heads=xa_heads, states=None)

    xs, s_re, s_im, s_ssm, s_conv = _mixer_and_attention(
        x_sample.reshape(db, d), p, s5_prep, nb=db, seq=1,
        mem_kv=(cache_mem_k[0], cache_mem_v[0]), xa_heads=xa_heads,
        states=(state_s5_re[0], state_s5_im[0], state_ssm[0], state_conv[0]))

    yp, ys = moe_routed_final([xp, xs], p, norm_final_w)

    kv_shape = (1, nb, mem) + cache_mem_k.shape[3:]
    return (yp.reshape(nb, seq, d), ys.reshape(db, 1, d), p_re[None], p_im[None], p_ssm[None], p_conv[None],
            mk.reshape(kv_shape), mv.reshape(kv_shape), s_re[None], s_im[None], s_ssm[None], s_conv[None])
```

```python
import functools
import math

import jax
import jax.numpy as jnp
from jax import lax
from jax.experimental import pallas as pl
from jax.experimental.pallas import tpu as pltpu

F32 = jnp.float32
BF16 = jnp.bfloat16
RMS_EPS = 1e-6

V7X_VMEM_BYTES = 64 * 1024 * 1024
VMEM_LIMIT = V7X_VMEM_BYTES - 8 * 1024 * 1024
LANES = 128

S5_CH = 16
S5_N = 64
S5_Q = 16
S5_GB = 8
S5_SEQ_PAD = 8
M2_HEADDIM = 64
M2_DSTATE = 128
M2_NGROUPS = 2
M2_CONV = 4
M2_CHUNK = 128
MOE_GROUPS = 4
MOE_PER_GROUP = 8


def _cparams(sem):
    return pltpu.CompilerParams(dimension_semantics=sem, vmem_limit_bytes=VMEM_LIMIT)


def _rms(x, w):
    return x * lax.rsqrt(jnp.mean(x * x, axis=-1, keepdims=True) + RMS_EPS) * w


def _sigmoid(x):
    return 1.0 / (1.0 + jnp.exp(-x))


def _silu(x):
    return x * _sigmoid(x)


def _softplus(x):
    return jnp.maximum(x, 0.0) + jnp.log1p(jnp.exp(-jnp.abs(x)))


def _gelu_tanh(x):
    return 0.5 * x * (1.0 + jnp.tanh(math.sqrt(2.0 / math.pi) * (x + 0.044715 * (x * x * x))))


def _dot(a, b):
    return jnp.dot(a, b, preferred_element_type=F32)


def _dot_nt(a, b):
    return lax.dot_general(a, b, (((1,), (1,)), ((), ())), preferred_element_type=F32)


def _split3(x):
    hi = x.astype(BF16)
    r = x - hi.astype(F32)
    mid = r.astype(BF16)
    lo = (r - mid.astype(F32)).astype(BF16)
    return hi, mid, lo


def _split2(x):
    hi = x.astype(BF16)
    lo = (x - hi.astype(F32)).astype(BF16)
    return hi, lo


def _mm_kernel(*refs, n_lhs, has_gain, has_res, has_side, staged):
    it = iter(refs)
    lhs = [next(it) for _ in range(n_lhs)]
    gain = next(it) if has_gain else None
    ws = [next(it) for _ in range(n_lhs)]
    side_w = next(it) if has_side else None
    res = next(it) if has_res else None
    out = next(it)
    side_out = next(it) if has_side else None
    lhs_bf = next(it) if staged else lhs

    if staged:
        @pl.when(pl.program_id(1) == 0)
        def _():
            for i in range(n_lhs):
                x = lhs[i][...]
                if has_gain:
                    x = _rms(x, gain[...])
                lhs_bf[i] = x.astype(BF16)
            if has_side:
                side_out[...] = _dot(lhs_bf[0], side_w[...].astype(BF16))

    acc = None
    for i in range(n_lhs):
        p = _dot(lhs_bf[i][...], ws[i][...].astype(BF16))
        acc = p if acc is None else acc + p
    if has_res:
        acc = acc + res[...]
    out[...] = acc.astype(out.dtype)


def fused_matmul(lhs_list, w, *, n_out, gain=None, res=None, side_w=None, out_dtype=F32, tm, tn):
    n_lhs = len(lhs_list)
    m, kp = lhs_list[0].shape
    assert all(a.shape == (m, kp) for a in lhs_list)
    assert w.shape[0] == n_lhs * kp and m % tm == 0 and n_out % tn == 0
    assert gain is None or n_lhs == 1
    staged = gain is not None or any(a.dtype != BF16 for a in lhs_list)
    assert staged or side_w is None
    grid = (m // tm, n_out // tn)
    in_specs = [pl.BlockSpec((tm, kp), lambda i, j: (i, 0)) for _ in range(n_lhs)]
    args = list(lhs_list)
    if gain is not None:
        in_specs.append(pl.BlockSpec((1, kp), lambda i, j: (0, 0)))
        args.append(gain.reshape(1, kp))
    for p in range(n_lhs):
        in_specs.append(pl.BlockSpec((kp, tn), lambda i, j, p=p: (p, j)))
        args.append(w)
    if side_w is not None:
        in_specs.append(pl.BlockSpec((kp, LANES), lambda i, j: (0, 0)))
        args.append(side_w)
    if res is not None:
        in_specs.append(pl.BlockSpec((tm, tn), lambda i, j: (i, j)))
        args.append(res)
    out_shape = [jax.ShapeDtypeStruct((m, n_out), out_dtype)]
    out_specs = [pl.BlockSpec((tm, tn), lambda i, j: (i, j))]
    if side_w is not None:
        out_shape.append(jax.ShapeDtypeStruct((m, LANES), F32))
        out_specs.append(pl.BlockSpec((tm, LANES), lambda i, j: (i, 0)))
    outs = pl.pallas_call(
        functools.partial(_mm_kernel, n_lhs=n_lhs, has_gain=gain is not None,
                          has_res=res is not None, has_side=side_w is not None, staged=staged),
        grid=grid, in_specs=in_specs, out_specs=out_specs, out_shape=out_shape,
        scratch_shapes=[pltpu.VMEM((n_lhs, tm, kp), BF16)] if staged else [],
        compiler_params=_cparams(("parallel", "arbitrary")),
        name="fused_matmul",
    )(*args)
    return outs if side_w is not None else outs[0]


def _proj_pair_kernel(a_ref, b_ref, w1a_ref, w1b_ref, res_ref, gain_ref, w2_ref, x1_ref, y_ref, x1_sc, lhs_sc, *, nt1):
    j = pl.program_id(1)
    tn = res_ref.shape[1]

    @pl.when(j < nt1)
    def _():
        acc = _dot(a_ref[...], w1a_ref[...]) + _dot(b_ref[...], w1b_ref[...]) + res_ref[...]
        x1_ref[...] = acc
        for jj in range(nt1):
            @pl.when(j == jj)
            def _(jj=jj):
                x1_sc[:, jj * tn:(jj + 1) * tn] = acc

    @pl.when(j == nt1)
    def _():
        lhs_sc[...] = _rms(x1_sc[...], gain_ref[...]).astype(BF16)

    @pl.when(j >= nt1)
    def _():
        y_ref[...] = _dot(lhs_sc[...], w2_ref[...]).astype(y_ref.dtype)


def proj_pair(a, b, w1, res, gain, w2, *, out_dtype, tm, tn):
    m, kp = a.shape
    d = res.shape[1]
    n2 = w2.shape[1]
    assert a.dtype == b.dtype == w1.dtype == w2.dtype == BF16
    assert w1.shape == (2 * kp, d) and w2.shape[0] == d and m % tm == 0 and d % tn == 0 and n2 % tn == 0
    nt1, nt2 = d // tn, n2 // tn
    first = lambda j: jnp.minimum(j, nt1 - 1)
    second = lambda j: jnp.maximum(j - nt1, 0)
    return pl.pallas_call(
        functools.partial(_proj_pair_kernel, nt1=nt1),
        grid=(m // tm, nt1 + nt2),
        in_specs=[pl.BlockSpec((tm, kp), lambda i, j: (i, 0)), pl.BlockSpec((tm, kp), lambda i, j: (i, 0)),
                  pl.BlockSpec((kp, tn), lambda i, j: (0, first(j))), pl.BlockSpec((kp, tn), lambda i, j: (1, first(j))),
                  pl.BlockSpec((tm, tn), lambda i, j: (i, first(j))), pl.BlockSpec((1, d), lambda i, j: (0, 0)),
                  pl.BlockSpec((d, tn), lambda i, j: (0, second(j)))],
        out_specs=[pl.BlockSpec((tm, tn), lambda i, j: (i, first(j))),
                   pl.BlockSpec((tm, tn), lambda i, j: (i, second(j)))],
        out_shape=[jax.ShapeDtypeStruct((m, d), F32), jax.ShapeDtypeStruct((m, n2), out_dtype)],
        scratch_shapes=[pltpu.VMEM((tm, d), F32), pltpu.VMEM((tm, d), BF16)],
        compiler_params=_cparams(("parallel", "arbitrary")),
        name="proj_pair",
    )(a, b, w1, w1, res, gain.reshape(1, d), w2)


def _s5_prep_kernel(*refs):
    for g in range(refs[0].shape[0]):
        _s5_prep_group(*[r.at[g] for r in refs])


def _s5_prep_group(lre_ref, lim_ref, ldt_ref, btre_ref, btim_ref, cre_ref, cim_ref,
                   tz_ref, wsre_ref, wsim_ref, wcre_ref, wcim_ref, aq_ref, ab_ref, bbt_ref):
    q, ch = S5_Q, S5_CH
    lr = lre_ref[...]
    li = lim_ref[...]
    step = jnp.exp(ldt_ref[...])
    mag = jnp.exp(lr * step)
    ab_re = mag * jnp.cos(li * step)
    ab_im = mag * jnp.sin(li * step)
    den = lr * lr + li * li
    num_re = ab_re - 1.0
    coef_re = (num_re * lr + ab_im * li) / den
    coef_im = (ab_im * lr - num_re * li) / den
    bt_re = btre_ref[...]
    bt_im = btim_ref[...]
    bb_re = coef_re * bt_re - coef_im * bt_im
    bb_im = coef_re * bt_im + coef_im * bt_re
    c_re = cre_ref[...]
    c_im = cim_ref[...]

    pw = [(jnp.ones_like(ab_re), jnp.zeros_like(ab_re))]
    for _ in range(q):
        pr, pi = pw[-1]
        pw.append((pr * ab_re - pi * ab_im, pr * ab_im + pi * ab_re))

    ca_re = [c_re * pr - c_im * pi for pr, pi in pw]
    ca_im = [c_re * pi + c_im * pr for pr, pi in pw]
    wcre_ref[...] = jnp.concatenate(ca_re[1:], axis=0).astype(BF16)
    wcim_ref[...] = jnp.concatenate([-x for x in ca_im[1:]], axis=0).astype(BF16)

    pr_stack = jnp.concatenate(ca_re[:q], axis=0)
    pi_stack = jnp.concatenate(ca_im[:q], axis=0)
    krow = None
    for a, b, sign in ((bb_re, pr_stack, 1.0), (bb_im, pi_stack, -1.0)):
        a_hi, a_lo = _split2(a)
        b_hi, b_lo = _split2(b)
        t = _dot_nt(a_hi, b_hi) + (_dot_nt(a_hi, b_lo) + _dot_nt(a_lo, b_hi))
        krow = sign * t if krow is None else krow + sign * t
    lane = lax.broadcasted_iota(jnp.int32, krow.shape, 1)
    blocks = [krow]
    for s in range(1, q):
        blocks.append(jnp.where(lane >= s * ch, pltpu.roll(krow, s * ch, 1), 0.0))
    tz_ref[...] = jnp.concatenate(blocks, axis=0).astype(BF16)

    ws_re, ws_im = [], []
    for s in range(q):
        pr, pi = pw[q - 1 - s]
        ws_re.append(bb_re * pr - bb_im * pi)
        ws_im.append(bb_re * pi + bb_im * pr)
    wsre_ref[...] = jnp.concatenate(ws_re, axis=0).astype(BF16)
    wsim_ref[...] = jnp.concatenate(ws_im, axis=0).astype(BF16)

    aq_ref[0:1, :] = pw[q][0]
    aq_ref[1:2, :] = pw[q][1]
    ab_ref[0:1, :] = ab_re
    ab_ref[1:2, :] = ab_im
    bbt_ref[0:ch, :] = bb_re
    bbt_ref[ch:2 * ch, :] = bb_im


def s5_prepare(a_re, a_im, log_dt, b_re, b_im, c_re, c_im):
    g, n = a_re.shape
    ch, q = S5_CH, S5_Q
    qc = q * ch
    bt_re = jnp.swapaxes(b_re, 1, 2)
    bt_im = jnp.swapaxes(b_im, 1, 2)

    def per_g(*dims):
        return pl.BlockSpec((S5_GB,) + dims, lambda i: (i,) + (0,) * len(dims))

    return pl.pallas_call(
        _s5_prep_kernel,
        grid=(g // S5_GB,),
        in_specs=[per_g(1, n), per_g(1, n), per_g(1, 1), per_g(ch, n), per_g(ch, n), per_g(ch, n), per_g(ch, n)],
        out_specs=[per_g(qc, qc), per_g(qc, n), per_g(qc, n), per_g(qc, n), per_g(qc, n),
                   per_g(2, n), per_g(2, n), per_g(2 * ch, n)],
        out_shape=[jax.ShapeDtypeStruct((g, qc, qc), BF16),
                   jax.ShapeDtypeStruct((g, qc, n), BF16), jax.ShapeDtypeStruct((g, qc, n), BF16),
                   jax.ShapeDtypeStruct((g, qc, n), BF16), jax.ShapeDtypeStruct((g, qc, n), BF16),
                   jax.ShapeDtypeStruct((g, 2, n), F32), jax.ShapeDtypeStruct((g, 2, n), F32),
                   jax.ShapeDtypeStruct((g, 2 * ch, n), F32)],
        compiler_params=_cparams(("parallel",)),
        name="s5_prepare",
    )(a_re.reshape(g, 1, n), a_im.reshape(g, 1, n), log_dt.reshape(g, 1, 1), bt_re, bt_im, c_re, c_im)


def _s5_scan_kernel(h_ref, tz_ref, wsre_ref, wsim_ref, wcre_ref, wcim_ref, aq_ref,
                    y_ref, hfin_ref, xs_sc, u_sc, yg_sc, sre_sc, sim_sc, *, nb, nchunk):
    gb, q, ch = S5_GB, S5_Q, S5_CH
    rows = nb * nchunk
    seq_stride = nchunk + S5_SEQ_PAD
    per_vreg = LANES // ch
    assert gb == per_vreg and q % per_vreg == 0
    slot = lax.broadcasted_iota(jnp.int32, (rows, LANES), 1) // ch

    halves = q // per_vreg

    def rot_rows(w, g):
        if g == 0:
            return w
        cut = (per_vreg - g) * ch
        parts = []
        for hf in range(halves):
            blk = w[hf * LANES:(hf + 1) * LANES]
            parts += [blk[cut:], blk[:cut]]
        return jnp.concatenate(parts, axis=0)

    for s in range(q):
        x = h_ref[pl.ds(s, rows, stride=q), :].astype(BF16)
        k = s % per_vreg
        xs_sc[s] = pltpu.roll(x, k * ch, 1) if k else x
    keep = [jnp.where(slot == j, 1.0, 0.0).astype(BF16) for j in range(per_vreg)]
    for g in range(gb):
        for hf in range(halves):
            acc = None
            for k in range(per_vreg):
                piece = xs_sc[hf * per_vreg + k] * keep[(g + k) % per_vreg]
                acc = piece if acc is None else acc + piece
            u_sc[g, :, hf * LANES:(hf + 1) * LANES] = acc

    for g in range(gb):
        u = u_sc[g]
        tz = rot_rows(tz_ref[g], g)
        tz = jnp.concatenate([pltpu.roll(tz[:, hf * LANES:(hf + 1) * LANES], g * ch, 1) if g
                              else tz[:, hf * LANES:(hf + 1) * LANES] for hf in range(halves)], axis=1)
        yg_sc[g] = _dot(u, tz)
        for sc, w_ref in ((sre_sc, wsre_ref), (sim_sc, wsim_ref)):
            s_all = _dot(u, rot_rows(w_ref[g], g))
            for b in range(nb):
                sc[g, b * seq_stride:b * seq_stride + nchunk, :] = s_all[b * nchunk:(b + 1) * nchunk]

    ar = [jnp.broadcast_to(aq_ref[g, 0:1, :], (nb, S5_N)) for g in range(gb)]
    ai = [jnp.broadcast_to(aq_ref[g, 1:2, :], (nb, S5_N)) for g in range(gb)]

    def step(c, carry):
        at = pl.ds(c, nb, stride=seq_stride)
        new = []
        for g in range(gb):
            hr, hi = carry[g]
            sr = sre_sc[g, at, :]
            si = sim_sc[g, at, :]
            sre_sc[g, at, :] = hr
            sim_sc[g, at, :] = hi
            new.append((ar[g] * hr - ai[g] * hi + sr, ar[g] * hi + ai[g] * hr + si))
        return tuple(new)

    zero = jnp.zeros((nb, S5_N), F32)
    fin = lax.fori_loop(0, nchunk, step, tuple((zero, zero) for _ in range(gb)), unroll=4)
    for g in range(gb):
        hfin_ref[g, 0] = fin[g][0]
        hfin_ref[g, 1] = fin[g][1]
        h_in = [jnp.concatenate([sc[g, b * seq_stride:b * seq_stride + nchunk, :] for b in range(nb)], axis=0)
                for sc in (sre_sc, sim_sc)]
        yg_sc[g] += (_dot_nt(h_in[0].astype(BF16), rot_rows(wcre_ref[g], g))
                     + _dot_nt(h_in[1].astype(BF16), rot_rows(wcim_ref[g], g)))

    for t in range(q):
        hf, k = divmod(t, per_vreg)
        acc = jnp.zeros((rows, LANES), F32)
        for g in range(gb):
            acc = jnp.where(slot == (k + g) % per_vreg, yg_sc[g, :, hf * LANES:(hf + 1) * LANES], acc)
        y_ref[pl.ds(t, rows, stride=q), :] = pltpu.roll(acc, (per_vreg - k) * ch, 1) if k else acc


def s5_scan(h, prep, nb, seq):
    tz, ws_re, ws_im, wc_re, wc_im, aq = prep[:6]
    g, qc, _ = tz.shape
    n = S5_N
    gb = S5_GB
    q = S5_Q
    assert gb * S5_CH == LANES and qc == q * S5_CH and seq % q == 0
    nchunk = seq // q
    rows = nb * nchunk
    m = nb * seq

    def blk(*dims):
        return pl.BlockSpec((gb,) + dims, lambda i: (i,) + (0,) * len(dims))

    return pl.pallas_call(
        functools.partial(_s5_scan_kernel, nb=nb, nchunk=nchunk),
        grid=(g // gb,),
        in_specs=[pl.BlockSpec((m, LANES), lambda i: (0, i)),
                  blk(qc, qc), blk(qc, n), blk(qc, n), blk(qc, n), blk(qc, n), blk(2, n)],
        out_specs=[pl.BlockSpec((m, LANES), lambda i: (0, i)), blk(2, nb, n)],
        out_shape=[jax.ShapeDtypeStruct((m, g * S5_CH), F32), jax.ShapeDtypeStruct((g, 2, nb, n), F32)],
        scratch_shapes=[pltpu.VMEM((q, rows, LANES), BF16), pltpu.VMEM((gb, rows, qc), BF16),
                        pltpu.VMEM((gb, rows, qc), F32),
                        pltpu.VMEM((gb, nb * (nchunk + S5_SEQ_PAD), n), F32),
                        pltpu.VMEM((gb, nb * (nchunk + S5_SEQ_PAD), n), F32)],
        compiler_params=_cparams(("parallel",)),
        name="s5_scan",
    )(h, tz, ws_re, ws_im, wc_re, wc_im, aq)


def _s5_step_kernel(u_ref, hre_ref, him_ref, ab_ref, bbt_ref, cre_ref, cim_ref, y_ref, ore_ref, oim_ref):
    gb = u_ref.shape[0]
    ch = S5_CH
    for g in range(gb):
        u = u_ref[g].astype(BF16)
        bb_re = bbt_ref[g, 0:ch, :].astype(BF16)
        bb_im = bbt_ref[g, ch:2 * ch, :].astype(BF16)
        ar = ab_ref[g, 0:1, :]
        ai = ab_ref[g, 1:2, :]
        hr0 = hre_ref[g]
        hi0 = him_ref[g]
        hr = _dot(u, bb_re) + (ar * hr0 - ai * hi0)
        hi = _dot(u, bb_im) + (ar * hi0 + ai * hr0)
        ore_ref[g] = hr
        oim_ref[g] = hi
        y_ref[g] = (_dot_nt(hr.astype(BF16), cre_ref[g].astype(BF16))
                    - _dot_nt(hi.astype(BF16), cim_ref[g].astype(BF16)))


def s5_step(u_t, h_re, h_im, prep, c_re, c_im):
    ab, bbt = prep[6], prep[7]
    g, b, ch = u_t.shape
    n = S5_N
    gb = S5_GB

    def blk(*dims):
        return pl.BlockSpec((gb,) + dims, lambda i: (i,) + (0,) * len(dims))

    return pl.pallas_call(
        _s5_step_kernel,
        grid=(g // gb,),
        in_specs=[blk(b, ch), blk(b, n), blk(b, n), blk(2, n), blk(2 * ch, n), blk(ch, n), blk(ch, n)],
        out_specs=[blk(b, ch), blk(b, n), blk(b, n)],
        out_shape=[jax.ShapeDtypeStruct((g, b, ch), F32), jax.ShapeDtypeStruct((g, b, n), F32),
                   jax.ShapeDtypeStruct((g, b, n), F32)],
        compiler_params=_cparams(("parallel",)),
        name="s5_step",
    )(u_t, h_re, h_im, ab, bbt, c_re, c_im)


def _s5_head_kernel(y_ref, u_ref, d_ref, w_ref, b_ref, nw_ref, o_ref):
    y = y_ref[...] + d_ref[...] * u_ref[...]
    g = _gelu_tanh(y)
    gate = _sigmoid(_dot(g.astype(BF16), w_ref[...].astype(BF16)) + b_ref[...])
    o_ref[...] = _rms(g * gate, nw_ref[...]).astype(o_ref.dtype)


def s5_head(y, h, d, glu_w, glu_b, norm_w, *, tm):
    m, ds = y.shape
    row = lambda a: a.reshape(1, ds)
    vec = pl.BlockSpec((1, ds), lambda i: (0, 0))
    return pl.pallas_call(
        _s5_head_kernel,
        grid=(m // tm,),
        in_specs=[pl.BlockSpec((tm, ds), lambda i: (i, 0)), pl.BlockSpec((tm, ds), lambda i: (i, 0)), vec,
                  pl.BlockSpec((ds, ds), lambda i: (0, 0)), vec, vec],
        out_specs=pl.BlockSpec((tm, ds), lambda i: (i, 0)),
        out_shape=jax.ShapeDtypeStruct((m, ds), BF16),
        compiler_params=_cparams(("parallel",)),
        name="s5_head",
    )(y, h, row(d), glu_w, row(glu_b), row(norm_w))


SSD_CHUNKS_PER_STEP = 4


def _pair_select(first, col0, col1, shape):
    return jnp.where(first, jnp.broadcast_to(col0, shape), jnp.broadcast_to(col1, shape))


def _ssd_chunk_kernel(*refs, d_inner, nheads, n_xparts):
    xparts = refs[:n_xparts]
    (z_ref, dt_ref, cw_ref, cb_ref, dtb_ref, alog_ref, dvec_ref, nw_ref,
     out_ref, ssm_ref, conv_ref, state_sc, xpad_sc, y_sc) = refs[n_xparts:]
    c = pl.program_id(1)
    q = M2_CHUNK
    hp = M2_HEADDIM
    ns = M2_DSTATE
    heads_per_group = nheads // M2_NGROUPS
    halo = 8

    @pl.when(c == 0)
    def _():
        state_sc[...] = jnp.zeros_like(state_sc)
        xpad_sc[0:halo, :] = jnp.zeros((halo, xpad_sc.shape[1]), F32)

    nrows = z_ref.shape[0]
    wpart = xparts[0].shape[1]
    for i, xr in enumerate(xparts):
        xpad_sc[halo:halo + nrows, i * wpart:(i + 1) * wpart] = xr[...]
    cw = cw_ref[...]
    a = -jnp.exp(alog_ref[...])
    row = lax.broadcasted_iota(jnp.int32, (q, q), 0)
    col = lax.broadcasted_iota(jnp.int32, (q, q), 1)
    causal = row >= col
    tri = jnp.where(causal, 1.0, 0.0).astype(BF16)
    first = col < hp
    first_rows = row < hp

    for r0 in range(0, nrows, q):
        conv = cb_ref[...] + cw[M2_CONV - 1:M2_CONV, :] * xpad_sc[halo + r0:halo + r0 + q, :]
        for k in range(1, M2_CONV):
            conv = conv + cw[M2_CONV - 1 - k:M2_CONV - k, :] * xpad_sc[halo + r0 - k:halo + r0 - k + q, :]
        xc = _silu(conv)

        dt = _softplus(dt_ref[r0:r0 + q, :] + dtb_ref[...])
        da = dt * a
        d_hi, d_mid, d_lo = _split3(da)
        acum = _dot(tri, d_hi) + (_dot(tri, d_mid) + _dot(tri, d_lo))
        acum_t = acum.T
        alast = acum[q - 1:q, :]

        for pr in range(nheads // 2):
            grp = (2 * pr) // heads_per_group
            b_bf = xc[:, d_inner + grp * ns:d_inner + (grp + 1) * ns].astype(BF16)
            c_bf = xc[:, d_inner + (M2_NGROUPS + grp) * ns:d_inner + (M2_NGROUPS + grp + 1) * ns].astype(BF16)
            cb = _dot_nt(c_bf, b_bf)
            xpair = xc[:, pr * 2 * hp:(pr + 1) * 2 * hp]
            h0, h1 = 2 * pr, 2 * pr + 1
            acol = [acum[:, h:h + 1] for h in (h0, h1)]
            m = []
            for k, h in enumerate((h0, h1)):
                seg = jnp.broadcast_to(acol[k], (q, q)) - jnp.broadcast_to(acum_t[h:h + 1, :], (q, q))
                lmat = jnp.exp(jnp.where(causal, seg, -1e30))
                m.append((cb * lmat).astype(BF16))
            dtp = _pair_select(first, dt[:, h0:h0 + 1], dt[:, h1:h1 + 1], (q, q))
            xdt = xpair * dtp
            xdt_bf = xdt.astype(BF16)
            y_diag = jnp.where(first, _dot(m[0], xdt_bf), _dot(m[1], xdt_bf))
            dec_end = _pair_select(first, jnp.exp(alast[:, h0:h0 + 1] - acol[0]),
                                   jnp.exp(alast[:, h1:h1 + 1] - acol[1]), (q, q))
            xw_t = (xdt * dec_end).T.astype(BF16)
            chunk_state = _dot(xw_t, b_bf)
            rows = pl.ds(pr * 2 * hp, 2 * hp)
            prev = state_sc[rows, :]
            y_off = _dot_nt(c_bf, prev.astype(BF16)) * _pair_select(first, jnp.exp(acol[0]), jnp.exp(acol[1]),
                                                                     (q, q))
            sdec = jnp.where(first_rows, jnp.broadcast_to(jnp.exp(alast[:, h0:h0 + 1]), (q, q)),
                             jnp.broadcast_to(jnp.exp(alast[:, h1:h1 + 1]), (q, q)))
            state_sc[rows, :] = prev * sdec + chunk_state
            y_sc[:, pr * 2 * hp:(pr + 1) * 2 * hp] = (y_diag + y_off
                                                       + dvec_ref[:, pr * 2 * hp:(pr + 1) * 2 * hp] * xpair)

        out_ref[r0:r0 + q, :] = _rms(y_sc[...] * _silu(z_ref[r0:r0 + q, :]), nw_ref[...]).astype(out_ref.dtype)

    xpad_sc[0:halo, :] = xpad_sc[nrows:nrows + halo, :]

    @pl.when(c == pl.num_programs(1) - 1)
    def _():
        ssm_ref[...] = state_sc[...]
        conv_ref[...] = xpad_sc[halo + nrows - (M2_CONV - 1):halo + nrows, :]


def ssd_prompt(h, dt_raw, nb, seq, p, *, d_s5, d_inner, nheads):
    q = M2_CHUNK
    rows_step = SSD_CHUNKS_PER_STEP * q if seq % (SSD_CHUNKS_PER_STEP * q) == 0 else q
    nc = seq // rows_step
    conv_dim = d_inner + 2 * M2_NGROUPS * M2_DSTATE
    xw = 512
    xoff = d_s5 + d_inner
    assert d_s5 % d_inner == 0 and xoff % xw == 0 and conv_dim % xw == 0
    assert M2_CHUNK == 2 * M2_HEADDIM == M2_DSTATE == LANES
    zblk = d_s5 // d_inner
    n_xparts = conv_dim // xw
    m = nb * seq
    pad = lambda v: jnp.pad(v, (0, LANES - v.shape[0])).reshape(1, LANES)
    dvec = jnp.repeat(p['m2_d'], M2_HEADDIM).reshape(1, d_inner)
    vec = lambda n: pl.BlockSpec((1, n), lambda b, c: (0, 0))
    tok = lambda w, j: pl.BlockSpec((rows_step, w), lambda b, c, j=j: (b * nc + c, j))
    out, ssm, conv = pl.pallas_call(
        functools.partial(_ssd_chunk_kernel, d_inner=d_inner, nheads=nheads, n_xparts=n_xparts),
        grid=(nb, nc),
        in_specs=[tok(xw, xoff // xw + i) for i in range(n_xparts)] + [tok(d_inner, zblk), tok(LANES, 0),
                  pl.BlockSpec((M2_CONV, conv_dim), lambda b, c: (0, 0)), vec(conv_dim), vec(LANES), vec(LANES),
                  vec(d_inner), vec(d_inner)],
        out_specs=[tok(d_inner, 0),
                   pl.BlockSpec((None, nheads * M2_HEADDIM, M2_DSTATE), lambda b, c: (b, 0, 0)),
                   pl.BlockSpec((None, M2_CONV - 1, conv_dim), lambda b, c: (b, 0, 0))],
        out_shape=[jax.ShapeDtypeStruct((m, d_inner), BF16),
                   jax.ShapeDtypeStruct((nb, nheads * M2_HEADDIM, M2_DSTATE), F32),
                   jax.ShapeDtypeStruct((nb, M2_CONV - 1, conv_dim), F32)],
        scratch_shapes=[pltpu.VMEM((nheads * M2_HEADDIM, M2_DSTATE), F32),
                        pltpu.VMEM((rows_step + 8, conv_dim), F32),
                        pltpu.VMEM((q, d_inner), F32)],
        compiler_params=_cparams(("parallel", "arbitrary")),
        name="ssd_chunk",
    )(*([h] * n_xparts), h, dt_raw, p['m2_conv_w'], p['m2_conv_b'].reshape(1, conv_dim), pad(p['m2_dt_bias']),
      pad(p['m2_a_log']), dvec, p['m2_norm_w'].reshape(1, d_inner))
    return out, ssm.reshape(nb, nheads, M2_HEADDIM, M2_DSTATE), conv


SSD_STEP_SEQS = 8


def _ssd_step_kernel(*refs, d_inner, nheads, n_xparts):
    xparts = refs[:n_xparts]
    (z_ref, dt_ref, cs0_ref, cs1_ref, cs2_ref, cw_ref, cb_ref, dtb_ref, alog_ref, dvec_ref, nw_ref, st_ref,
     out_ref, so_ref, lhs_sc, bfull_sc, ct_sc, yt_sc, xs_sc) = refs[n_xparts:]
    i = pl.program_id(0)
    nb = z_ref.shape[0]
    ns = M2_DSTATE
    rows_g = (nheads // M2_NGROUPS) * M2_HEADDIM

    @pl.when(i == 0)
    def _():
        cw = cw_ref[...]
        xbc = jnp.concatenate([xr[...] for xr in xparts], axis=1)
        conv = (cb_ref[...] + cw[3:4, :] * xbc + cw[2:3, :] * cs2_ref[...]
                + cw[1:2, :] * cs1_ref[...] + cw[0:1, :] * cs0_ref[...])
        xc = _silu(conv)
        dt = _softplus(dt_ref[...] + dtb_ref[...])
        dec = jnp.exp(dt * (-jnp.exp(alog_ref[...])))
        hrow = lax.broadcasted_iota(jnp.int32, (LANES, d_inner), 0)
        hcol = lax.broadcasted_iota(jnp.int32, (LANES, d_inner), 1)
        expand = jnp.where(hcol // M2_HEADDIM == hrow, 1.0, 0.0).astype(BF16)

        def expand_heads(v):
            a, b_, c = _split3(v)
            return _dot(a, expand) + (_dot(b_, expand) + _dot(c, expand))

        xs = xc[:, :d_inner]
        xs_sc[...] = xs
        xdt_t = (xs * expand_heads(dt)).T
        d_hi, d_mid, d_lo = _split3(expand_heads(dec).T)
        for g in range(M2_NGROUPS):
            r = slice(g * rows_g, (g + 1) * rows_g)
            lhs_sc[g] = jnp.concatenate([xdt_t[r].astype(BF16), d_hi[r], d_mid[r], d_lo[r]], axis=1)
            b_g = xc[:, d_inner + g * ns:d_inner + (g + 1) * ns]
            bfull_sc[g] = jnp.concatenate([b_g, jnp.zeros_like(b_g)], axis=1)
            c_g = xc[:, d_inner + (M2_NGROUPS + g) * ns:d_inner + (M2_NGROUPS + g + 1) * ns]
            ct_sc[g] = c_g.T
        yt_sc[...] = jnp.zeros_like(yt_sc)

    row_id = lax.broadcasted_iota(jnp.int32, (nb, 2 * ns), 0)
    lane_id = lax.broadcasted_iota(jnp.int32, (nb, 2 * ns), 1)
    col_id = lax.broadcasted_iota(jnp.int32, (ns, nb), 1)
    for j in range(st_ref.shape[0]):
        b = i * st_ref.shape[0] + j
        r_bot = jnp.where((row_id == b) & (lane_id >= ns), 1.0, 0.0).astype(BF16)
        for g in range(M2_NGROUPS):
            r = pl.ds(g * rows_g, rows_g)
            r_top = jnp.where(row_id == b, bfull_sc[g], 0.0).astype(BF16)
            rhs = jnp.concatenate([r_top, r_bot, r_bot, r_bot], axis=0)
            o = _dot(lhs_sc[g], rhs)
            hnew = st_ref[j, r, :] * o[:, ns:] + o[:, :ns]
            so_ref[j, r, :] = hnew
            cm = jnp.where(col_id == b, ct_sc[g], 0.0).astype(BF16)
            yt_sc[r, :] += _dot(hnew.astype(BF16), cm)

    @pl.when(i == pl.num_programs(0) - 1)
    def _():
        y = yt_sc[...].T + dvec_ref[...] * xs_sc[...]
        out_ref[...] = _rms(y * _silu(z_ref[...]), nw_ref[...]).astype(out_ref.dtype)


def ssd_sample(h, dt_raw, state, conv_state, p, *, d_s5, d_inner, nheads):
    nb = h.shape[0]
    conv_dim = d_inner + 2 * M2_NGROUPS * M2_DSTATE
    xw = 512
    xoff = d_s5 + d_inner
    assert nb == LANES and M2_DSTATE == LANES and M2_CONV == 4
    assert xoff % xw == 0 and conv_dim % xw == 0 and d_s5 % d_inner == 0 and nb % SSD_STEP_SEQS == 0
    n_xparts = conv_dim // xw
    rows = nheads * M2_HEADDIM
    rows_g = rows // M2_NGROUPS
    pad = lambda v: jnp.pad(v, (0, LANES - v.shape[0])).reshape(1, LANES)
    dvec = jnp.repeat(p['m2_d'], M2_HEADDIM).reshape(1, d_inner)
    full = lambda a, b, j=0: pl.BlockSpec((a, b), lambda i, j=j: (0, j))
    st_spec = pl.BlockSpec((SSD_STEP_SEQS, rows, M2_DSTATE), lambda i: (i, 0, 0))
    out, new_state = pl.pallas_call(
        functools.partial(_ssd_step_kernel, d_inner=d_inner, nheads=nheads, n_xparts=n_xparts),
        grid=(nb // SSD_STEP_SEQS,),
        in_specs=[full(nb, xw, xoff // xw + k) for k in range(n_xparts)]
        + [full(nb, d_inner, d_s5 // d_inner), full(nb, LANES)]
        + [full(nb, conv_dim)] * 3
        + [full(M2_CONV, conv_dim), full(1, conv_dim), full(1, LANES), full(1, LANES), full(1, d_inner),
           full(1, d_inner), st_spec],
        out_specs=[full(nb, d_inner), st_spec],
        out_shape=[jax.ShapeDtypeStruct((nb, d_inner), BF16), jax.ShapeDtypeStruct((nb, rows, M2_DSTATE), F32)],
        scratch_shapes=[pltpu.VMEM((M2_NGROUPS, rows_g, 4 * nb), BF16),
                        pltpu.VMEM((M2_NGROUPS, nb, 2 * M2_DSTATE), F32),
                        pltpu.VMEM((M2_NGROUPS, M2_DSTATE, nb), F32),
                        pltpu.VMEM((rows, nb), F32),
                        pltpu.VMEM((nb, d_inner), F32)],
        compiler_params=_cparams(("arbitrary",)),
        name="ssd_step",
    )(*([h] * n_xparts), h, dt_raw, conv_state[:, 0], conv_state[:, 1], conv_state[:, 2],
      p['m2_conv_w'], p['m2_conv_b'].reshape(1, conv_dim), pad(p['m2_dt_bias']), pad(p['m2_a_log']),
      dvec, p['m2_norm_w'].reshape(1, d_inner), state.reshape(nb, rows, M2_DSTATE))
    xbc = lax.slice_in_dim(h, xoff, xoff + conv_dim, axis=1)
    new_conv = jnp.concatenate([conv_state[:, 1:], xbc[:, None, :]], axis=1)
    return out, new_state.reshape(state.shape), new_conv


def _softmax_rows(s):
    e = jnp.exp(s - jnp.max(s, axis=-1, keepdims=True))
    return e / jnp.sum(e, axis=-1, keepdims=True)


def _attn_kernel(q_ref, k_ref, v_ref, o_ref, *, scale):
    s = _dot_nt(q_ref[...].astype(BF16), k_ref[...].astype(BF16)) * scale
    o_ref[...] = _dot(_softmax_rows(s).astype(BF16), v_ref[...].astype(BF16)).astype(o_ref.dtype)


def attention_prompt(q, k, v, nb, seq, mem, heads, *, tq):
    d = q.shape[1]
    hd = d // heads
    nq = seq // tq
    kv_spec = pl.BlockSpec((mem, hd), lambda b, h, i: (b, h))
    q_spec = pl.BlockSpec((tq, hd), lambda b, h, i: (b * nq + i, h))
    return pl.pallas_call(
        functools.partial(_attn_kernel, scale=hd ** -0.5),
        grid=(nb, heads, nq),
        in_specs=[q_spec, kv_spec, kv_spec],
        out_specs=q_spec,
        out_shape=jax.ShapeDtypeStruct(q.shape, BF16),
        compiler_params=_cparams(("parallel", "parallel", "parallel")),
        name="attention_prompt",
    )(q, k, v)


ATTN_STEP_SEQS = 4


ATTN_STEP_ROWS = 64


def _attn_step_kernel(q_ref, k_ref, v_ref, o_ref, s_sc, *, scale):
    nseq, mem, heads, hd = k_ref.shape
    ch = ATTN_STEP_ROWS
    pack = 8 // heads
    for j in range(nseq):
        q = jnp.concatenate([q_ref[j]] * pack, axis=0)

        def score(c, mx):
            rows = pl.ds(pl.multiple_of(c * ch, ch), ch)
            k = k_ref[j, rows].reshape(ch // pack, pack * heads, hd)
            s = jnp.sum(k * q, axis=-1, keepdims=True) * scale
            s_sc[pl.ds(pl.multiple_of(c * (ch // pack), ch // pack), ch // pack)] = s
            return jnp.maximum(mx, jnp.max(s, axis=0))

        mx = lax.fori_loop(0, mem // ch, score, jnp.full((pack * heads, 1), NEG, F32))
        mx1 = mx[0:heads]
        for i in range(1, pack):
            mx1 = jnp.maximum(mx1, mx[i * heads:(i + 1) * heads])
        mx = jnp.concatenate([mx1] * pack, axis=0)

        def accum(c, carry):
            den, acc = carry
            rows = pl.ds(pl.multiple_of(c * ch, ch), ch)
            v = v_ref[j, rows].reshape(ch // pack, pack * heads, hd)
            e = jnp.exp(s_sc[pl.ds(pl.multiple_of(c * (ch // pack), ch // pack), ch // pack)] - mx)
            return den + jnp.sum(e, axis=0), acc + jnp.sum(e * v, axis=0)

        den, acc = lax.fori_loop(0, mem // ch, accum,
                                 (jnp.zeros((pack * heads, 1), F32), jnp.zeros((pack * heads, hd), F32)))
        den1, acc1 = den[0:heads], acc[0:heads]
        for i in range(1, pack):
            den1 = den1 + den[i * heads:(i + 1) * heads]
            acc1 = acc1 + acc[i * heads:(i + 1) * heads]
        o_ref[j] = acc1 / den1


def attention_sample(q, k_cache, v_cache):
    b, mem, heads, hd = k_cache.shape
    nseq = ATTN_STEP_SEQS
    q_spec = pl.BlockSpec((nseq, heads, hd), lambda i: (i, 0, 0))
    kv_spec = pl.BlockSpec((nseq, mem, heads, hd), lambda i: (i, 0, 0, 0))
    out = pl.pallas_call(
        functools.partial(_attn_step_kernel, scale=hd ** -0.5),
        grid=(b // nseq,),
        in_specs=[q_spec, kv_spec, kv_spec],
        out_specs=q_spec,
        out_shape=jax.ShapeDtypeStruct((b, heads, hd), F32),
        scratch_shapes=[pltpu.VMEM((mem * heads // 8, 8, 1), F32)],
        compiler_params=_cparams(("parallel",)),
        name="attention_step",
    )(q.reshape(b, heads, hd), k_cache, v_cache)
    return out.reshape(b, heads * hd)


NEG = -1e30


INFO_G1, INFO_G2, INFO_E1, INFO_E2 = 0, 1, 2, 3


def _lane_pack(lane, values):
    out = 0.0
    for k, v in values:
        out = jnp.where(lane == k, v, out)
    return out


def _router_kernel(x_ref, nw_ref, wr_ref, br_ref, sel_ref, info_ref, *, n_experts):
    xn = _rms(x_ref[...], nw_ref[...])
    x_hi, x_lo = _split2(xn)
    w_hi, w_lo = _split2(wr_ref[...])
    hi_terms = _dot(x_hi, jnp.concatenate([w_hi, w_lo], axis=1))
    logits = hi_terms[:, :LANES] + (hi_terms[:, LANES:] + _dot(x_lo, w_hi)) + br_ref[...]
    lane = lax.broadcasted_iota(jnp.int32, logits.shape, 1)
    big = jnp.int32(2 ** 30)
    is_c = (lane >= n_experts) & (lane < n_experts + MOE_GROUPS)
    lc = jnp.where(is_c, logits, NEG)
    cmax = jnp.max(lc, axis=-1, keepdims=True)
    gsel = jnp.min(jnp.where(lc == cmax, lane, big), axis=-1, keepdims=True) - n_experts
    gate_c = 1.0 / jnp.sum(jnp.where(is_c, jnp.exp(lc - cmax), 0.0), axis=-1, keepdims=True)
    in_group = (lane < n_experts) & (lane // MOE_PER_GROUP == gsel)
    lf = jnp.where(in_group, logits, NEG)
    t1 = jnp.max(lf, axis=-1, keepdims=True)
    i1 = jnp.min(jnp.where(lf == t1, lane, big), axis=-1, keepdims=True)
    lf2 = jnp.where(lane == i1, NEG, lf)
    t2 = jnp.max(lf2, axis=-1, keepdims=True)
    i2 = jnp.min(jnp.where(lf2 == t2, lane, big), axis=-1, keepdims=True)
    r = jnp.exp(t2 - t1)
    g1 = gate_c / (1.0 + r)
    g2 = gate_c * r / (1.0 + r)
    sel_ref[...] = jnp.where((lane == i1) | (lane == i2), 1.0, 0.0)
    info_ref[...] = _lane_pack(lane, ((INFO_G1, g1), (INFO_G2, g2), (INFO_E1, i1.astype(F32)),
                                      (INFO_E2, i2.astype(F32))))


def moe_router(x, norm_w, w_coarse, b_coarse, w_fine, b_fine, *, tm):
    m, d = x.shape
    e = w_fine.shape[1]
    padw = LANES - e - MOE_GROUPS
    wr = jnp.concatenate([w_fine, w_coarse, jnp.zeros((d, padw), F32)], axis=1)
    br = jnp.concatenate([b_fine, b_coarse, jnp.zeros((padw,), F32)]).reshape(1, LANES)
    tok = pl.BlockSpec((tm, LANES), lambda i: (i, 0))
    return pl.pallas_call(
        functools.partial(_router_kernel, n_experts=e),
        grid=(m // tm,),
        in_specs=[pl.BlockSpec((tm, d), lambda i: (i, 0)), pl.BlockSpec((1, d), lambda i: (0, 0)),
                  pl.BlockSpec((d, LANES), lambda i: (0, 0)), pl.BlockSpec((1, LANES), lambda i: (0, 0))],
        out_specs=[tok, tok],
        out_shape=[jax.ShapeDtypeStruct((m, LANES), F32), jax.ShapeDtypeStruct((m, LANES), F32)],
        compiler_params=_cparams(("parallel",)),
        name="moe_router",
    )(x, norm_w.reshape(1, d), wr, br)


def _moe_rank_kernel(sel_ref, info_ref, rank_ref, counts_ref, carry_sc):
    i = pl.program_id(0)

    @pl.when(i == 0)
    def _():
        carry_sc[...] = jnp.zeros_like(carry_sc)

    sel = sel_ref[...]
    tm = sel.shape[0]
    row = lax.broadcasted_iota(jnp.int32, (tm, tm), 0)
    col = lax.broadcasted_iota(jnp.int32, (tm, tm), 1)
    before = jnp.where(row > col, 1.0, 0.0).astype(BF16)
    rank = _dot(before, sel.astype(BF16)) + carry_sc[...]
    info = info_ref[...]
    lane = lax.broadcasted_iota(jnp.int32, sel.shape, 1)
    e1 = info[:, INFO_E1:INFO_E1 + 1].astype(jnp.int32)
    e2 = info[:, INFO_E2:INFO_E2 + 1].astype(jnp.int32)
    r1 = jnp.sum(jnp.where(lane == e1, rank, 0.0), axis=-1, keepdims=True)
    r2 = jnp.sum(jnp.where(lane == e2, rank, 0.0), axis=-1, keepdims=True)
    rank_ref[...] = _lane_pack(lane, ((0, r1), (1, r2)))
    carry_sc[...] += jnp.sum(sel, axis=0, keepdims=True)

    @pl.when(i == pl.num_programs(0) - 1)
    def _():
        counts_ref[...] = carry_sc[...]


def _moe_plan_kernel(rank_ref, info_ref, counts_ref, dest_ref, tiles_ref, *, tile, n_experts):
    counts = counts_ref[...]
    ntile_e = jnp.floor((counts + (tile - 1)) * (1.0 / tile))
    padded = jnp.broadcast_to(ntile_e * tile, (8, LANES))
    r = lax.broadcasted_iota(jnp.int32, (LANES, LANES), 0)
    c = lax.broadcasted_iota(jnp.int32, (LANES, LANES), 1)
    lower = jnp.where(r < c, 1.0, 0.0).astype(BF16)
    p_hi, p_mid, p_lo = _split3(padded)
    offs = (_dot(p_hi, lower) + (_dot(p_mid, lower) + _dot(p_lo, lower)))[0:1, :]
    ends = offs + padded[0:1, :]

    info = info_ref[...]
    rank = rank_ref[...]
    lane = lax.broadcasted_iota(jnp.int32, info.shape, 1)
    e1 = info[:, INFO_E1:INFO_E1 + 1].astype(jnp.int32)
    e2 = info[:, INFO_E2:INFO_E2 + 1].astype(jnp.int32)
    d1 = jnp.sum(jnp.where(lane == e1, offs, 0.0), axis=-1, keepdims=True) + rank[:, 0:1]
    d2 = jnp.sum(jnp.where(lane == e2, offs, 0.0), axis=-1, keepdims=True) + rank[:, 1:2]
    dest_ref[...] = _lane_pack(lane, ((0, d1), (1, d2))).astype(jnp.int32)

    ends_col = jnp.broadcast_to(ends, (LANES, LANES)).T
    start = (c * tile).astype(F32)
    n_before = jnp.sum(jnp.where((ends_col <= start) & (r < n_experts), 1.0, 0.0), axis=0, keepdims=True)
    lane1 = lax.broadcasted_iota(jnp.int32, (1, LANES), 1)
    total = jnp.sum(jnp.where(lane1 == n_experts - 1, ends, 0.0), axis=-1, keepdims=True)
    tail = jnp.where(counts > 0.0, ends - tile, -1.0)
    tiles_ref[...] = jnp.zeros_like(tiles_ref)
    tiles_ref[0:1, :] = jnp.minimum(n_before, n_experts - 1.0).astype(jnp.int32)
    tiles_ref[1:2, :] = jnp.broadcast_to(total * (1.0 / tile), (1, LANES)).astype(jnp.int32)
    tiles_ref[2:3, :] = tail.astype(jnp.int32)


def _moe_dispatch_kernel(dest_ref, tails_ref, ntiles_ref, *rest, tile, n_experts, group_steps, group_offsets):
    TAIL_SEM, FILL_SEM, ROW_SEM = 0, 1, 2
    n_groups = len(group_offsets)
    nw_ref = rest[0]
    x_refs = rest[1:1 + n_groups]
    xs_ref, zero_sc, xn_sc, sems = rest[1 + n_groups:]
    i = pl.program_id(0)
    n_tiles_max = xs_ref.shape[0] // tile

    def fill_copy(j):
        return pltpu.make_async_copy(zero_sc, xs_ref.at[pl.ds(pl.multiple_of(j * tile, tile), tile)],
                                     sems.at[FILL_SEM])

    def tail_copy(e):
        return pltpu.make_async_copy(zero_sc, xs_ref.at[pl.ds(pl.multiple_of(tails_ref[e], tile), tile)],
                                     sems.at[TAIL_SEM])

    @pl.when(i == 0)
    def _():
        zero_sc[...] = jnp.zeros_like(zero_sc)

        def fill(j, _):
            fill_copy(j).start()
            return 0

        def clear(e, _):
            @pl.when(tails_ref[e] >= 0)
            def _():
                tail_copy(e).start()
            return 0

        def clear_wait(e, _):
            @pl.when(tails_ref[e] >= 0)
            def _():
                tail_copy(e).wait()
            return 0

        lax.fori_loop(ntiles_ref[0], n_tiles_max, fill, 0)
        lax.fori_loop(0, n_experts, clear, 0)
        lax.fori_loop(0, n_experts, clear_wait, 0)

    slot = i % 2

    def wait_rows(which, tmw):
        for _ in range(2):
            pltpu.make_async_copy(xn_sc.at[which, pl.ds(0, tmw)], xs_ref.at[pl.ds(0, tmw)],
                                  sems.at[ROW_SEM + which]).wait()

    for g, x_ref in enumerate(x_refs):
        @pl.when((i >= group_steps[g]) & (i < group_steps[g + 1]))
        def _(g=g, x_ref=x_ref):
            tm = x_ref.shape[0]
            xn_ref = xn_sc.at[slot, pl.ds(0, tm)]
            xn_ref[...] = _rms(x_ref[...], nw_ref[...])
            base = 2 * (group_offsets[g] + (i - group_steps[g]) * tm)
            for r in range(tm):
                for k in range(2):
                    pltpu.make_async_copy(xn_ref.at[pl.ds(r, 1)], xs_ref.at[pl.ds(dest_ref[base + 2 * r + k], 1)],
                                          sems.at[ROW_SEM + slot]).start(priority=k)

            @pl.when(i > group_steps[g])
            def _():
                wait_rows(1 - slot, tm)

            if g > 0:
                @pl.when(i == group_steps[g])
                def _():
                    wait_rows(1 - slot, x_refs[g - 1].shape[0])

            @pl.when(i == pl.num_programs(0) - 1)
            def _():
                wait_rows(slot, tm)

    @pl.when(i == pl.num_programs(0) - 1)
    def _():
        def fill_wait(j, _):
            fill_copy(j).wait()
            return 0

        lax.fori_loop(ntiles_ref[0], n_tiles_max, fill_wait, 0)


def _moe_expert_kernel(texp_ref, ntiles_ref, xs_ref, wg_hbm, wu_hbm, wd_hbm, ys_ref,
                       wg_sc, wu_sc, wd_sc, slot_sm, sems):
    i = pl.program_id(0)
    n = ntiles_ref[0]

    def weight_copies(e, slot):
        return [pltpu.make_async_copy(w.at[e], sc.at[slot], sems.at[slot])
                for w, sc in ((wg_hbm, wg_sc), (wu_hbm, wu_sc), (wd_hbm, wd_sc))]

    @pl.when(i < n)
    def _():
        e = texp_ref[i]
        first = (i == 0) | (texp_ref[jnp.maximum(i - 1, 0)] != e)

        @pl.when(i == 0)
        def _():
            slot_sm[0] = 0
            for c in weight_copies(e, 0):
                c.start()

        @pl.when(first & (i > 0))
        def _():
            slot_sm[0] = 1 - slot_sm[0]

        slot = slot_sm[0]

        @pl.when(first)
        def _():
            for c in weight_copies(e, slot):
                c.wait()
            j = lax.while_loop(lambda j: (j < n) & (texp_ref[jnp.minimum(j, n - 1)] == e), lambda j: j + 1, i + 1)

            @pl.when(j < n)
            def _():
                for c in weight_copies(texp_ref[jnp.minimum(j, n - 1)], 1 - slot):
                    c.start()

        x = xs_ref[...].astype(BF16)
        hg = _dot(x, wg_sc[slot].astype(BF16))
        hu = _dot(x, wu_sc[slot].astype(BF16))
        ys_ref[...] = _dot((_silu(hg) * hu).astype(BF16), wd_sc[slot].astype(BF16))


def _moe_combine_kernel(dest_ref, ys_ref, x_ref, info_ref, fw_ref, y_ref, buf_sc, sems):
    s = pl.program_id(0)
    n_blocks = pl.num_programs(0) - 1
    tm = x_ref.shape[0]
    slot = s % 2

    @pl.when(s < n_blocks)
    def _():
        for r in range(tm):
            for k in range(2):
                pltpu.make_async_copy(ys_ref.at[pl.ds(dest_ref[2 * (s * tm + r) + k], 1)],
                                      buf_sc.at[slot, k, pl.ds(r, 1)], sems.at[slot]).start(priority=k)

    @pl.when(s > 0)
    def _():
        prev = 1 - slot
        for k in range(2):
            pltpu.make_async_copy(ys_ref.at[pl.ds(0, tm)], buf_sc.at[prev, k], sems.at[prev]).wait()
        info = info_ref[...]
        y = (x_ref[...] + info[:, INFO_G1:INFO_G1 + 1] * buf_sc[prev, 0]
             + info[:, INFO_G2:INFO_G2 + 1] * buf_sc[prev, 1])
        y_ref[...] = _rms(y, fw_ref[...])


MOE_TILE = 256
MOE_TOKENS_PER_STEP = 256


def _largest_tile(m, cap):
    return max(t for t in range(8, cap + 1, 8) if m % t == 0)


def moe_routed_final(x_list, p, final_w):
    d = x_list[0].shape[1]
    ne, _, f = p['moe_w_gate'].shape
    sizes = [x.shape[0] for x in x_list]
    m = sum(sizes)
    tile = min(MOE_TILE, m)
    routed = [moe_router(x, p['norm_ffn_w'], p['router_coarse_w'], p['router_coarse_b'],
                         p['router_fine_w'], p['router_fine_b'], tm=min(x.shape[0], 512)) for x in x_list]
    sel = jnp.concatenate([r[0] for r in routed], axis=0)
    info = jnp.concatenate([r[1] for r in routed], axis=0)
    tm = _largest_tile(m, 1024)
    tok = pl.BlockSpec((tm, LANES), lambda i: (i, 0))
    rank, counts = pl.pallas_call(
        _moe_rank_kernel,
        grid=(m // tm,),
        in_specs=[tok, tok],
        out_specs=[tok, pl.BlockSpec((1, LANES), lambda i: (0, 0))],
        out_shape=[jax.ShapeDtypeStruct((m, LANES), F32), jax.ShapeDtypeStruct((1, LANES), F32)],
        scratch_shapes=[pltpu.VMEM((1, LANES), F32)],
        compiler_params=_cparams(("arbitrary",)),
        name="moe_rank",
    )(sel, info)
    n_tiles_max = (2 * m) // tile + ne
    assert n_tiles_max <= LANES
    dest, tiles = pl.pallas_call(
        functools.partial(_moe_plan_kernel, tile=tile, n_experts=ne),
        out_shape=[jax.ShapeDtypeStruct((m, LANES), jnp.int32), jax.ShapeDtypeStruct((8, LANES), jnp.int32)],
        compiler_params=pltpu.CompilerParams(vmem_limit_bytes=VMEM_LIMIT),
        name="moe_plan",
    )(rank, info, counts)
    dest_flat = dest[:, :2].reshape(2 * m)
    tile_expert = tiles[0, :n_tiles_max]
    n_tiles = tiles[1, :1]
    tails = tiles[2, :ne]

    rows = n_tiles_max * tile

    tds = [min(mg, 2 * MOE_TOKENS_PER_STEP) for mg in sizes]
    group_steps = [0]
    for mg, td in zip(sizes, tds):
        group_steps.append(group_steps[-1] + mg // td)
    group_offsets = [sum(sizes[:g]) for g in range(len(sizes))]

    def group_spec(g):
        first, last = group_steps[g], group_steps[g + 1] - 1
        return pl.BlockSpec((tds[g], d), lambda i, *_: (jnp.clip(i, first, last) - first, 0))

    xs = pl.pallas_call(
        functools.partial(_moe_dispatch_kernel, tile=tile, n_experts=ne, group_steps=tuple(group_steps),
                          group_offsets=tuple(group_offsets)),
        grid_spec=pltpu.PrefetchScalarGridSpec(
            num_scalar_prefetch=3, grid=(group_steps[-1],),
            in_specs=[pl.BlockSpec((1, d), lambda i, *_: (0, 0))] + [group_spec(g) for g in range(len(sizes))],
            out_specs=pl.BlockSpec(memory_space=pl.ANY),
            scratch_shapes=[pltpu.VMEM((tile, d), F32), pltpu.VMEM((2, max(tds), d), F32),
                            pltpu.SemaphoreType.DMA((4,))]),
        out_shape=jax.ShapeDtypeStruct((rows, d), F32),
        compiler_params=_cparams(("arbitrary",)),
        name="moe_dispatch",
    )(dest_flat, tails, n_tiles, p['norm_ffn_w'].reshape(1, d), *x_list)

    def tile_idx(i, te, nt):
        return jnp.minimum(i, nt[0] - 1)

    ys = pl.pallas_call(
        _moe_expert_kernel,
        grid_spec=pltpu.PrefetchScalarGridSpec(
            num_scalar_prefetch=2, grid=(n_tiles_max,),
            in_specs=[pl.BlockSpec((tile, d), lambda i, te, nt: (tile_idx(i, te, nt), 0)),
                      pl.BlockSpec(memory_space=pl.ANY), pl.BlockSpec(memory_space=pl.ANY),
                      pl.BlockSpec(memory_space=pl.ANY)],
            out_specs=pl.BlockSpec((tile, d), lambda i, te, nt: (tile_idx(i, te, nt), 0)),
            scratch_shapes=[pltpu.VMEM((2, d, f), F32), pltpu.VMEM((2, d, f), F32), pltpu.VMEM((2, f, d), F32),
                            pltpu.SMEM((1,), jnp.int32), pltpu.SemaphoreType.DMA((2,))]),
        out_shape=jax.ShapeDtypeStruct((rows, d), F32),
        input_output_aliases={2: 0},
        compiler_params=_cparams(("arbitrary",)),
        name="moe_experts",
    )(tile_expert, n_tiles, xs, p['moe_w_gate'], p['moe_w_up'], p['moe_w_down'])

    outs = []
    off = 0
    for x, (_, info_g), mg in zip(x_list, routed, sizes):
        tc = min(mg, MOE_TOKENS_PER_STEP)
        outs.append(pl.pallas_call(
            _moe_combine_kernel,
            grid_spec=pltpu.PrefetchScalarGridSpec(
                num_scalar_prefetch=1, grid=(mg // tc + 1,),
                in_specs=[pl.BlockSpec(memory_space=pl.ANY),
                          pl.BlockSpec((tc, d), lambda s, dref: (jnp.maximum(s - 1, 0), 0)),
                          pl.BlockSpec((tc, LANES), lambda s, dref: (jnp.maximum(s - 1, 0), 0)),
                          pl.BlockSpec((1, d), lambda s, dref: (0, 0))],
                out_specs=pl.BlockSpec((tc, d), lambda s, dref: (jnp.maximum(s - 1, 0), 0)),
                scratch_shapes=[pltpu.VMEM((2, 2, tc, d), F32), pltpu.SemaphoreType.DMA((2,))]),
            out_shape=jax.ShapeDtypeStruct((mg, d), F32),
            compiler_params=_cparams(("arbitrary",)),
            name="moe_combine",
        )(dest_flat[2 * off:2 * (off + mg)], ys, x, info_g, final_w.reshape(1, d)))
        off += mg
    return outs


def s5_prompt(h, nb, seq, prep, p):
    y, hfin = s5_scan(h, prep, nb, seq)
    out = s5_head(y, h, p['s5_d'], p['s5_glu_w'], p['s5_glu_b'], p['s5_norm_w'], tm=min(512, nb * seq))
    return out, hfin[:, 0].transpose(1, 0, 2), hfin[:, 1].transpose(1, 0, 2)


def s5_sample(h, st_re, st_im, prep, p):
    g = prep[0].shape[0]
    ch = S5_CH
    ds = g * ch
    b = h.shape[0]
    u_t = h[:, :ds].reshape(b, g, ch).transpose(1, 0, 2)
    y_t, n_re, n_im = s5_step(u_t, st_re.transpose(1, 0, 2), st_im.transpose(1, 0, 2), prep,
                              p['s5_c_re'], p['s5_c_im'])
    y = y_t.transpose(1, 0, 2).reshape(b, ds)
    out = s5_head(y, h, p['s5_d'], p['s5_glu_w'], p['s5_glu_b'], p['s5_norm_w'], tm=b)
    return out, n_re.transpose(1, 0, 2), n_im.transpose(1, 0, 2)


def _row_tile(m):
    return min(m, 1024)


def _col_tile(m, n):
    if m <= 256:
        return n
    return max(t for t in range(256, 1793, 256) if n % t == 0)


def _mixer_and_attention(x, p, s5_prep, *, nb, seq, mem_kv, xa_heads, states):
    m, d = x.shape
    g, n = p['s5_a_re'].shape
    d_s5 = g * S5_CH
    nheads = p['m2_a_log'].shape[0]
    d_inner = nheads * M2_HEADDIM
    conv_dim = d_inner + 2 * M2_NGROUPS * M2_DSTATE
    n_main = d_s5 + d_inner + conv_dim
    tm = _row_tile(m)

    w_in = p['w_in']
    w_dt = jnp.pad(w_in[:, n_main:], ((0, 0), (0, LANES - nheads)))
    tm2, tn2 = tm, _col_tile(m, d)
    h, dt_raw = fused_matmul([x], w_in, n_out=n_main, gain=p['norm_mix_w'], side_w=w_dt, tm=tm,
                             tn=_col_tile(m, n_main))

    if states is None:
        s5_out, s5_re, s5_im = s5_prompt(h, nb, seq, s5_prep, p)
        m2_out, ssm, conv = ssd_prompt(h, dt_raw, nb, seq, p, d_s5=d_s5, d_inner=d_inner, nheads=nheads)
    else:
        s5_out, s5_re, s5_im = s5_sample(h, states[0], states[1], s5_prep, p)
        m2_out, ssm, conv = ssd_sample(h, dt_raw, states[2], states[3], p, d_s5=d_s5, d_inner=d_inner,
                                       nheads=nheads)
    if states is None:
        x1, q = proj_pair(s5_out, m2_out, p['w_out'], x, p['norm_xa_w'], p['xa_wq'], out_dtype=BF16, tm=tm, tn=512)
    else:
        x1 = fused_matmul([s5_out, m2_out], p['w_out'], n_out=d, res=x, tm=tm2, tn=tn2)
        q = fused_matmul([x1], p['xa_wq'], n_out=d, gain=p['norm_xa_w'], tm=tm2, tn=tn2)
    if states is None:
        mem = mem_kv[0].shape[0] // nb
        o = attention_prompt(q, mem_kv[0], mem_kv[1], nb, seq, mem, xa_heads, tq=min(seq, 2048))
    else:
        o = attention_sample(q, mem_kv[0], mem_kv[1])
    x2 = fused_matmul([o], p['xa_wo'], n_out=d, res=x1, tm=tm2, tn=tn2)

    return x2, s5_re, s5_im, ssm, conv


def kernel(x_prompt, x_sample, mem_prompt, state_s5_re, state_s5_im, state_ssm, state_conv, cache_mem_k, cache_mem_v, norm_mix_w, w_in, s5_a_re, s5_a_im, s5_log_dt, s5_b_re, s5_b_im, s5_c_re, s5_c_im, s5_d, s5_glu_w, s5_glu_b, s5_norm_w, m2_conv_w, m2_conv_b, m2_dt_bias, m2_a_log, m2_d, m2_norm_w, w_out, norm_xa_w, norm_mem_w, xa_wq, xa_wk, xa_wv, xa_wo, norm_ffn_w, router_coarse_w, router_coarse_b, router_fine_w, router_fine_b, moe_w_gate, moe_w_up, moe_w_down, norm_final_w):
    depth = w_in.shape[0]
    assert depth == 1, "the final norm is fused into the (only) layer"
    per_layer = dict(
        norm_mix_w=norm_mix_w, w_in=w_in, s5_a_re=s5_a_re, s5_a_im=s5_a_im, s5_log_dt=s5_log_dt,
        s5_b_re=s5_b_re, s5_b_im=s5_b_im, s5_c_re=s5_c_re, s5_c_im=s5_c_im, s5_d=s5_d, s5_glu_w=s5_glu_w,
        s5_glu_b=s5_glu_b, s5_norm_w=s5_norm_w, m2_conv_w=m2_conv_w, m2_conv_b=m2_conv_b, m2_dt_bias=m2_dt_bias,
        m2_a_log=m2_a_log, m2_d=m2_d, m2_norm_w=m2_norm_w, w_out=w_out, norm_xa_w=norm_xa_w,
        norm_mem_w=norm_mem_w, xa_wq=xa_wq, xa_wk=xa_wk, xa_wv=xa_wv, xa_wo=xa_wo, norm_ffn_w=norm_ffn_w,
        router_coarse_w=router_coarse_w, router_coarse_b=router_coarse_b, router_fine_w=router_fine_w,
        router_fine_b=router_fine_b, moe_w_gate=moe_w_gate, moe_w_up=moe_w_up, moe_w_down=moe_w_down)
    p = {k: v[0] for k, v in per_layer.items()}
    for name in ('w_in', 'w_out', 'xa_wq', 'xa_wo'):
        p[name] = p[name].astype(BF16)
    nb, seq, d = x_prompt.shape
    db, dseq, _ = x_sample.shape
    assert dseq == 1
    mem = mem_prompt.shape[1]
    xa_heads = cache_mem_k.shape[3]

    s5_prep = s5_prepare(p['s5_a_re'], p['s5_a_im'], p['s5_log_dt'], p['s5_b_re'], p['s5_b_im'],
                         p['s5_c_re'], p['s5_c_im'])

    memx = mem_prompt.reshape(nb * mem, d)
    mk = fused_matmul([memx], p['xa_wk'], n_out=d, gain=p['norm_mem_w'], tm=_row_tile(nb * mem),
                      tn=_col_tile(nb * mem, d))
    mv = fused_matmul([memx], p['xa_wv'], n_out=d, gain=p['norm_mem_w'], tm=_row_tile(nb * mem),
                      tn=_col_tile(nb * mem, d))
    xp, p_re, p_im, p_ssm, p_conv = _mixer_and_attention(
        x_prompt.reshape(nb * seq, d), p, s5_prep, nb=nb, seq=seq, mem_kv=(mk, mv), xa_heads=xa_heads, states=None)

    xs, s_re, s_im, s_ssm, s_conv = _mixer_and_attention(
        x_sample.reshape(db, d), p, s5_prep, nb=db, seq=1,
        mem_kv=(cache_mem_k[0], cache_mem_v[0]), xa_heads=xa_heads,
        states=(state_s5_re[0], state_s5_im[0], state_ssm[0], state_conv[0]))

    yp, ys = moe_routed_final([xp, xs], p, norm_final_w)

    kv_shape = (1, nb, mem) + cache_mem_k.shape[3:]
    return (yp.reshape(nb, seq, d), ys.reshape(db, 1, d), p_re[None], p_im[None], p_ssm[None], p_conv[None],
            mk.reshape(kv_shape), mv.reshape(kv_shape), s_re[None], s_im[None], s_ssm[None], s_conv[None])
```

```python
import functools
import math

import jax
import jax.numpy as jnp
from jax import lax
from jax.experimental import pallas as pl
from jax.experimental.pallas import tpu as pltpu

F32 = jnp.float32
BF16 = jnp.bfloat16
RMS_EPS = 1e-6

V7X_VMEM_BYTES = 64 * 1024 * 1024
VMEM_LIMIT = V7X_VMEM_BYTES - 8 * 1024 * 1024
LANES = 128

S5_CH = 16
S5_N = 64
S5_Q = 16
S5_GB = 8
S5_SEQ_PAD = 8
M2_HEADDIM = 64
M2_DSTATE = 128
M2_NGROUPS = 2
M2_CONV = 4
M2_CHUNK = 128
MOE_GROUPS = 4
MOE_PER_GROUP = 8


def _cparams(sem):
    return pltpu.CompilerParams(dimension_semantics=sem, vmem_limit_bytes=VMEM_LIMIT)


def _rms(x, w):
    return x * lax.rsqrt(jnp.mean(x * x, axis=-1, keepdims=True) + RMS_EPS) * w


def _sigmoid(x):
    return 1.0 / (1.0 + jnp.exp(-x))


def _silu(x):
    return x * _sigmoid(x)


def _softplus(x):
    return jnp.maximum(x, 0.0) + jnp.log1p(jnp.exp(-jnp.abs(x)))


def _gelu_tanh(x):
    return 0.5 * x * (1.0 + jnp.tanh(math.sqrt(2.0 / math.pi) * (x + 0.044715 * (x * x * x))))


def _dot(a, b):
    return jnp.dot(a, b, preferred_element_type=F32)


def _dot_nt(a, b):
    return lax.dot_general(a, b, (((1,), (1,)), ((), ())), preferred_element_type=F32)


def _split3(x):
    hi = x.astype(BF16)
    r = x - hi.astype(F32)
    mid = r.astype(BF16)
    lo = (r - mid.astype(F32)).astype(BF16)
    return hi, mid, lo


def _split2(x):
    hi = x.astype(BF16)
    lo = (x - hi.astype(F32)).astype(BF16)
    return hi, lo


def _mm_kernel(*refs, n_lhs, has_gain, has_res, has_side, staged):
    it = iter(refs)
    lhs = [next(it) for _ in range(n_lhs)]
    gain = next(it) if has_gain else None
    ws = [next(it) for _ in range(n_lhs)]
    side_w = next(it) if has_side else None
    res = next(it) if has_res else None
    out = next(it)
    side_out = next(it) if has_side else None
    lhs_bf = next(it) if staged else lhs

    if staged:
        @pl.when(pl.program_id(1) == 0)
        def _():
            for i in range(n_lhs):
                x = lhs[i][...]
                if has_gain:
                    x = _rms(x, gain[...])
                lhs_bf[i] = x.astype(BF16)
            if has_side:
                side_out[...] = _dot(lhs_bf[0], side_w[...].astype(BF16))

    acc = None
    for i in range(n_lhs):
        p = _dot(lhs_bf[i][...], ws[i][...].astype(BF16))
        acc = p if acc is None else acc + p
    if has_res:
        acc = acc + res[...]
    out[...] = acc.astype(out.dtype)


def fused_matmul(lhs_list, w, *, n_out, gain=None, res=None, side_w=None, out_dtype=F32, tm, tn):
    n_lhs = len(lhs_list)
    m, kp = lhs_list[0].shape
    assert all(a.shape == (m, kp) for a in lhs_list)
    assert w.shape[0] == n_lhs * kp and m % tm == 0 and n_out % tn == 0
    assert gain is None or n_lhs == 1
    staged = gain is not None or any(a.dtype != BF16 for a in lhs_list)
    assert staged or side_w is None
    grid = (m // tm, n_out // tn)
    in_specs = [pl.BlockSpec((tm, kp), lambda i, j: (i, 0)) for _ in range(n_lhs)]
    args = list(lhs_list)
    if gain is not None:
        in_specs.append(pl.BlockSpec((1, kp), lambda i, j: (0, 0)))
        args.append(gain.reshape(1, kp))
    for p in range(n_lhs):
        in_specs.append(pl.BlockSpec((kp, tn), lambda i, j, p=p: (p, j)))
        args.append(w)
    if side_w is not None:
        in_specs.append(pl.BlockSpec((kp, LANES), lambda i, j: (0, 0)))
        args.append(side_w)
    if res is not None:
        in_specs.append(pl.BlockSpec((tm, tn), lambda i, j: (i, j)))
        args.append(res)
    out_shape = [jax.ShapeDtypeStruct((m, n_out), out_dtype)]
    out_specs = [pl.BlockSpec((tm, tn), lambda i, j: (i, j))]
    if side_w is not None:
        out_shape.append(jax.ShapeDtypeStruct((m, LANES), F32))
        out_specs.append(pl.BlockSpec((tm, LANES), lambda i, j: (i, 0)))
    outs = pl.pallas_call(
        functools.partial(_mm_kernel, n_lhs=n_lhs, has_gain=gain is not None,
                          has_res=res is not None, has_side=side_w is not None, staged=staged),
        grid=grid, in_specs=in_specs, out_specs=out_specs, out_shape=out_shape,
        scratch_shapes=[pltpu.VMEM((n_lhs, tm, kp), BF16)] if staged else [],
        compiler_params=_cparams(("parallel", "arbitrary")),
        name="fused_matmul",
    )(*args)
    return outs if side_w is not None else outs[0]


def _s5_prep_kernel(*refs):
    for g in range(refs[0].shape[0]):
        _s5_prep_group(*[r.at[g] for r in refs])


def _s5_prep_group(lre_ref, lim_ref, ldt_ref, btre_ref, btim_ref, cre_ref, cim_ref,
                   tz_ref, wsre_ref, wsim_ref, wcre_ref, wcim_ref, aq_ref, ab_ref, bbt_ref):
    q, ch = S5_Q, S5_CH
    lr = lre_ref[...]
    li = lim_ref[...]
    step = jnp.exp(ldt_ref[...])
    mag = jnp.exp(lr * step)
    ab_re = mag * jnp.cos(li * step)
    ab_im = mag * jnp.sin(li * step)
    den = lr * lr + li * li
    num_re = ab_re - 1.0
    coef_re = (num_re * lr + ab_im * li) / den
    coef_im = (ab_im * lr - num_re * li) / den
    bt_re = btre_ref[...]
    bt_im = btim_ref[...]
    bb_re = coef_re * bt_re - coef_im * bt_im
    bb_im = coef_re * bt_im + coef_im * bt_re
    c_re = cre_ref[...]
    c_im = cim_ref[...]

    pw = [(jnp.ones_like(ab_re), jnp.zeros_like(ab_re))]
    for _ in range(q):
        pr, pi = pw[-1]
        pw.append((pr * ab_re - pi * ab_im, pr * ab_im + pi * ab_re))

    ca_re = [c_re * pr - c_im * pi for pr, pi in pw]
    ca_im = [c_re * pi + c_im * pr for pr, pi in pw]
    wcre_ref[...] = jnp.concatenate(ca_re[1:], axis=0).astype(BF16)
    wcim_ref[...] = jnp.concatenate([-x for x in ca_im[1:]], axis=0).astype(BF16)

    pr_stack = jnp.concatenate(ca_re[:q], axis=0)
    pi_stack = jnp.concatenate(ca_im[:q], axis=0)
    krow = None
    for a, b, sign in ((bb_re, pr_stack, 1.0), (bb_im, pi_stack, -1.0)):
        a_hi, a_lo = _split2(a)
        b_hi, b_lo = _split2(b)
        t = _dot_nt(a_hi, b_hi) + (_dot_nt(a_hi, b_lo) + _dot_nt(a_lo, b_hi))
        krow = sign * t if krow is None else krow + sign * t
    lane = lax.broadcasted_iota(jnp.int32, krow.shape, 1)
    blocks = [krow]
    for s in range(1, q):
        blocks.append(jnp.where(lane >= s * ch, pltpu.roll(krow, s * ch, 1), 0.0))
    tz_ref[...] = jnp.concatenate(blocks, axis=0).astype(BF16)

    ws_re, ws_im = [], []
    for s in range(q):
        pr, pi = pw[q - 1 - s]
        ws_re.append(bb_re * pr - bb_im * pi)
        ws_im.append(bb_re * pi + bb_im * pr)
    wsre_ref[...] = jnp.concatenate(ws_re, axis=0).astype(BF16)
    wsim_ref[...] = jnp.concatenate(ws_im, axis=0).astype(BF16)

    aq_ref[0:1, :] = pw[q][0]
    aq_ref[1:2, :] = pw[q][1]
    ab_ref[0:1, :] = ab_re
    ab_ref[1:2, :] = ab_im
    bbt_ref[0:ch, :] = bb_re
    bbt_ref[ch:2 * ch, :] = bb_im


def s5_prepare(a_re, a_im, log_dt, b_re, b_im, c_re, c_im):
    g, n = a_re.shape
    ch, q = S5_CH, S5_Q
    qc = q * ch
    bt_re = jnp.swapaxes(b_re, 1, 2)
    bt_im = jnp.swapaxes(b_im, 1, 2)

    def per_g(*dims):
        return pl.BlockSpec((S5_GB,) + dims, lambda i: (i,) + (0,) * len(dims))

    return pl.pallas_call(
        _s5_prep_kernel,
        grid=(g // S5_GB,),
        in_specs=[per_g(1, n), per_g(1, n), per_g(1, 1), per_g(ch, n), per_g(ch, n), per_g(ch, n), per_g(ch, n)],
        out_specs=[per_g(qc, qc), per_g(qc, n), per_g(qc, n), per_g(qc, n), per_g(qc, n),
                   per_g(2, n), per_g(2, n), per_g(2 * ch, n)],
        out_shape=[jax.ShapeDtypeStruct((g, qc, qc), BF16),
                   jax.ShapeDtypeStruct((g, qc, n), BF16), jax.ShapeDtypeStruct((g, qc, n), BF16),
                   jax.ShapeDtypeStruct((g, qc, n), BF16), jax.ShapeDtypeStruct((g, qc, n), BF16),
                   jax.ShapeDtypeStruct((g, 2, n), F32), jax.ShapeDtypeStruct((g, 2, n), F32),
                   jax.ShapeDtypeStruct((g, 2 * ch, n), F32)],
        compiler_params=_cparams(("parallel",)),
        name="s5_prepare",
    )(a_re.reshape(g, 1, n), a_im.reshape(g, 1, n), log_dt.reshape(g, 1, 1), bt_re, bt_im, c_re, c_im)


def _s5_scan_kernel(h_ref, tz_ref, wsre_ref, wsim_ref, wcre_ref, wcim_ref, aq_ref,
                    y_ref, hfin_ref, xs_sc, u_sc, yg_sc, sre_sc, sim_sc, *, nb, nchunk):
    gb, q, ch = S5_GB, S5_Q, S5_CH
    rows = nb * nchunk
    seq_stride = nchunk + S5_SEQ_PAD
    per_vreg = LANES // ch
    assert gb == per_vreg and q % per_vreg == 0
    slot = lax.broadcasted_iota(jnp.int32, (rows, LANES), 1) // ch

    halves = q // per_vreg

    def rot_rows(w, g):
        if g == 0:
            return w
        cut = (per_vreg - g) * ch
        parts = []
        for hf in range(halves):
            blk = w[hf * LANES:(hf + 1) * LANES]
            parts += [blk[cut:], blk[:cut]]
        return jnp.concatenate(parts, axis=0)

    for s in range(q):
        x = h_ref[pl.ds(s, rows, stride=q), :].astype(BF16)
        k = s % per_vreg
        xs_sc[s] = pltpu.roll(x, k * ch, 1) if k else x
    keep = [jnp.where(slot == j, 1.0, 0.0).astype(BF16) for j in range(per_vreg)]
    for g in range(gb):
        for hf in range(halves):
            acc = None
            for k in range(per_vreg):
                piece = xs_sc[hf * per_vreg + k] * keep[(g + k) % per_vreg]
                acc = piece if acc is None else acc + piece
            u_sc[g, :, hf * LANES:(hf + 1) * LANES] = acc

    for g in range(gb):
        u = u_sc[g]
        tz = rot_rows(tz_ref[g], g)
        tz = jnp.concatenate([pltpu.roll(tz[:, hf * LANES:(hf + 1) * LANES], g * ch, 1) if g
                              else tz[:, hf * LANES:(hf + 1) * LANES] for hf in range(halves)], axis=1)
        yg_sc[g] = _dot(u, tz)
        for sc, w_ref in ((sre_sc, wsre_ref), (sim_sc, wsim_ref)):
            s_all = _dot(u, rot_rows(w_ref[g], g))
            for b in range(nb):
                sc[g, b * seq_stride:b * seq_stride + nchunk, :] = s_all[b * nchunk:(b + 1) * nchunk]

    ar = [jnp.broadcast_to(aq_ref[g, 0:1, :], (nb, S5_N)) for g in range(gb)]
    ai = [jnp.broadcast_to(aq_ref[g, 1:2, :], (nb, S5_N)) for g in range(gb)]

    def step(c, carry):
        at = pl.ds(c, nb, stride=seq_stride)
        new = []
        for g in range(gb):
            hr, hi = carry[g]
            sr = sre_sc[g, at, :]
            si = sim_sc[g, at, :]
            sre_sc[g, at, :] = hr
            sim_sc[g, at, :] = hi
            new.append((ar[g] * hr - ai[g] * hi + sr, ar[g] * hi + ai[g] * hr + si))
        return tuple(new)

    zero = jnp.zeros((nb, S5_N), F32)
    fin = lax.fori_loop(0, nchunk, step, tuple((zero, zero) for _ in range(gb)), unroll=4)
    for g in range(gb):
        hfin_ref[g, 0] = fin[g][0]
        hfin_ref[g, 1] = fin[g][1]
        h_in = [jnp.concatenate([sc[g, b * seq_stride:b * seq_stride + nchunk, :] for b in range(nb)], axis=0)
                for sc in (sre_sc, sim_sc)]
        yg_sc[g] += (_dot_nt(h_in[0].astype(BF16), rot_rows(wcre_ref[g], g))
                     + _dot_nt(h_in[1].astype(BF16), rot_rows(wcim_ref[g], g)))

    for t in range(q):
        hf, k = divmod(t, per_vreg)
        acc = jnp.zeros((rows, LANES), F32)
        for g in range(gb):
            acc = jnp.where(slot == (k + g) % per_vreg, yg_sc[g, :, hf * LANES:(hf + 1) * LANES], acc)
        y_ref[pl.ds(t, rows, stride=q), :] = pltpu.roll(acc, (per_vreg - k) * ch, 1) if k else acc


def s5_scan(h, prep, nb, seq):
    tz, ws_re, ws_im, wc_re, wc_im, aq = prep[:6]
    g, qc, _ = tz.shape
    n = S5_N
    gb = S5_GB
    q = S5_Q
    assert gb * S5_CH == LANES and qc == q * S5_CH and seq % q == 0
    nchunk = seq // q
    rows = nb * nchunk
    m = nb * seq

    def blk(*dims):
        return pl.BlockSpec((gb,) + dims, lambda i: (i,) + (0,) * len(dims))

    return pl.pallas_call(
        functools.partial(_s5_scan_kernel, nb=nb, nchunk=nchunk),
        grid=(g // gb,),
        in_specs=[pl.BlockSpec((m, LANES), lambda i: (0, i)),
                  blk(qc, qc), blk(qc, n), blk(qc, n), blk(qc, n), blk(qc, n), blk(2, n)],
        out_specs=[pl.BlockSpec((m, LANES), lambda i: (0, i)), blk(2, nb, n)],
        out_shape=[jax.ShapeDtypeStruct((m, g * S5_CH), F32), jax.ShapeDtypeStruct((g, 2, nb, n), F32)],
        scratch_shapes=[pltpu.VMEM((q, rows, LANES), BF16), pltpu.VMEM((gb, rows, qc), BF16),
                        pltpu.VMEM((gb, rows, qc), F32),
                        pltpu.VMEM((gb, nb * (nchunk + S5_SEQ_PAD), n), F32),
                        pltpu.VMEM((gb, nb * (nchunk + S5_SEQ_PAD), n), F32)],
        compiler_params=_cparams(("parallel",)),
        name="s5_scan",
    )(h, tz, ws_re, ws_im, wc_re, wc_im, aq)


def _s5_step_kernel(u_ref, hre_ref, him_ref, ab_ref, bbt_ref, cre_ref, cim_ref, y_ref, ore_ref, oim_ref):
    gb = hre_ref.shape[1]
    ch = S5_CH
    for g in range(gb):
        u = u_ref[:, g * ch:(g + 1) * ch].astype(BF16)
        bb_re = bbt_ref[g, 0:ch, :].astype(BF16)
        bb_im = bbt_ref[g, ch:2 * ch, :].astype(BF16)
        ar = ab_ref[g, 0:1, :]
        ai = ab_ref[g, 1:2, :]
        hr0 = hre_ref[:, g, :]
        hi0 = him_ref[:, g, :]
        hr = _dot(u, bb_re) + (ar * hr0 - ai * hi0)
        hi = _dot(u, bb_im) + (ar * hi0 + ai * hr0)
        ore_ref[:, g, :] = hr
        oim_ref[:, g, :] = hi
        y_ref[:, g * ch:(g + 1) * ch] = (_dot_nt(hr.astype(BF16), cre_ref[g].astype(BF16))
                                         - _dot_nt(hi.astype(BF16), cim_ref[g].astype(BF16)))


def s5_step(h, h_re, h_im, prep, c_re, c_im):
    ab, bbt = prep[6], prep[7]
    b, g, n = h_re.shape
    ch = S5_CH
    gb = S5_GB
    assert gb * ch == LANES

    def blk(*dims):
        return pl.BlockSpec((gb,) + dims, lambda i: (i,) + (0,) * len(dims))

    tok = pl.BlockSpec((b, LANES), lambda i: (0, i))
    st = pl.BlockSpec((b, gb, n), lambda i: (0, i, 0))
    return pl.pallas_call(
        _s5_step_kernel,
        grid=(g // gb,),
        in_specs=[tok, st, st, blk(2, n), blk(2 * ch, n), blk(ch, n), blk(ch, n)],
        out_specs=[tok, st, st],
        out_shape=[jax.ShapeDtypeStruct((b, g * ch), F32), jax.ShapeDtypeStruct((b, g, n), F32),
                   jax.ShapeDtypeStruct((b, g, n), F32)],
        compiler_params=_cparams(("parallel",)),
        name="s5_step",
    )(h, h_re, h_im, ab, bbt, c_re, c_im)


def _s5_head_kernel(y_ref, u_ref, d_ref, w_ref, b_ref, nw_ref, o_ref):
    y = y_ref[...] + d_ref[...] * u_ref[...]
    g = _gelu_tanh(y)
    gate = _sigmoid(_dot(g.astype(BF16), w_ref[...].astype(BF16)) + b_ref[...])
    o_ref[...] = _rms(g * gate, nw_ref[...]).astype(o_ref.dtype)


def s5_head(y, h, d, glu_w, glu_b, norm_w, *, tm):
    m, ds = y.shape
    row = lambda a: a.reshape(1, ds)
    vec = pl.BlockSpec((1, ds), lambda i: (0, 0))
    return pl.pallas_call(
        _s5_head_kernel,
        grid=(m // tm,),
        in_specs=[pl.BlockSpec((tm, ds), lambda i: (i, 0)), pl.BlockSpec((tm, ds), lambda i: (i, 0)), vec,
                  pl.BlockSpec((ds, ds), lambda i: (0, 0)), vec, vec],
        out_specs=pl.BlockSpec((tm, ds), lambda i: (i, 0)),
        out_shape=jax.ShapeDtypeStruct((m, ds), BF16),
        compiler_params=_cparams(("parallel",)),
        name="s5_head",
    )(y, h, row(d), glu_w, row(glu_b), row(norm_w))


SSD_CHUNKS_PER_STEP = 4


def _pair_select(first, col0, col1, shape):
    return jnp.where(first, jnp.broadcast_to(col0, shape), jnp.broadcast_to(col1, shape))


def _ssd_chunk_kernel(*refs, d_inner, nheads, n_xparts):
    xparts = refs[:n_xparts]
    (z_ref, dt_ref, cw_ref, cb_ref, dtb_ref, alog_ref, dvec_ref, nw_ref,
     out_ref, ssm_ref, conv_ref, state_sc, xpad_sc, y_sc) = refs[n_xparts:]
    c = pl.program_id(1)
    q = M2_CHUNK
    hp = M2_HEADDIM
    ns = M2_DSTATE
    heads_per_group = nheads // M2_NGROUPS
    halo = 8

    @pl.when(c == 0)
    def _():
        state_sc[...] = jnp.zeros_like(state_sc)
        xpad_sc[0:halo, :] = jnp.zeros((halo, xpad_sc.shape[1]), F32)

    nrows = z_ref.shape[0]
    wpart = xparts[0].shape[1]
    for i, xr in enumerate(xparts):
        xpad_sc[halo:halo + nrows, i * wpart:(i + 1) * wpart] = xr[...]
    cw = cw_ref[...]
    a = -jnp.exp(alog_ref[...])
    row = lax.broadcasted_iota(jnp.int32, (q, q), 0)
    col = lax.broadcasted_iota(jnp.int32, (q, q), 1)
    causal = row >= col
    tri = jnp.where(causal, 1.0, 0.0).astype(BF16)
    first = col < hp
    first_rows = row < hp

    for r0 in range(0, nrows, q):
        conv = cb_ref[...] + cw[M2_CONV - 1:M2_CONV, :] * xpad_sc[halo + r0:halo + r0 + q, :]
        for k in range(1, M2_CONV):
            conv = conv + cw[M2_CONV - 1 - k:M2_CONV - k, :] * xpad_sc[halo + r0 - k:halo + r0 - k + q, :]
        xc = _silu(conv)

        dt = _softplus(dt_ref[r0:r0 + q, :] + dtb_ref[...])
        da = dt * a
        d_hi, d_mid, d_lo = _split3(da)
        acum = _dot(tri, d_hi) + (_dot(tri, d_mid) + _dot(tri, d_lo))
        acum_t = acum.T
        alast = acum[q - 1:q, :]

        for pr in range(nheads // 2):
            grp = (2 * pr) // heads_per_group
            b_bf = xc[:, d_inner + grp * ns:d_inner + (grp + 1) * ns].astype(BF16)
            c_bf = xc[:, d_inner + (M2_NGROUPS + grp) * ns:d_inner + (M2_NGROUPS + grp + 1) * ns].astype(BF16)
            cb = _dot_nt(c_bf, b_bf)
            xpair = xc[:, pr * 2 * hp:(pr + 1) * 2 * hp]
            h0, h1 = 2 * pr, 2 * pr + 1
            acol = [acum[:, h:h + 1] for h in (h0, h1)]
            m = []
            for k, h in enumerate((h0, h1)):
                seg = jnp.broadcast_to(acol[k], (q, q)) - jnp.broadcast_to(acum_t[h:h + 1, :], (q, q))
                lmat = jnp.exp(jnp.where(causal, seg, -1e30))
                m.append((cb * lmat).astype(BF16))
            dtp = _pair_select(first, dt[:, h0:h0 + 1], dt[:, h1:h1 + 1], (q, q))
            xdt = xpair * dtp
            xdt_bf = xdt.astype(BF16)
            y_diag = jnp.where(first, _dot(m[0], xdt_bf), _dot(m[1], xdt_bf))
            dec_end = _pair_select(first, jnp.exp(alast[:, h0:h0 + 1] - acol[0]),
                                   jnp.exp(alast[:, h1:h1 + 1] - acol[1]), (q, q))
            xw_t = (xdt * dec_end).T.astype(BF16)
            chunk_state = _dot(xw_t, b_bf)
            rows = pl.ds(pr * 2 * hp, 2 * hp)
            prev = state_sc[rows, :]
            y_off = _dot_nt(c_bf, prev.astype(BF16)) * _pair_select(first, jnp.exp(acol[0]), jnp.exp(acol[1]),
                                                                     (q, q))
            sdec = jnp.where(first_rows, jnp.broadcast_to(jnp.exp(alast[:, h0:h0 + 1]), (q, q)),
                             jnp.broadcast_to(jnp.exp(alast[:, h1:h1 + 1]), (q, q)))
            state_sc[rows, :] = prev * sdec + chunk_state
            y_sc[:, pr * 2 * hp:(pr + 1) * 2 * hp] = (y_diag + y_off
                                                       + dvec_ref[:, pr * 2 * hp:(pr + 1) * 2 * hp] * xpair)

        out_ref[r0:r0 + q, :] = _rms(y_sc[...] * _silu(z_ref[r0:r0 + q, :]), nw_ref[...]).astype(out_ref.dtype)

    xpad_sc[0:halo, :] = xpad_sc[nrows:nrows + halo, :]

    @pl.when(c == pl.num_programs(1) - 1)
    def _():
        ssm_ref[...] = state_sc[...]
        conv_ref[...] = xpad_sc[halo + nrows - (M2_CONV - 1):halo + nrows, :]


def ssd_prompt(h, dt_raw, nb, seq, p, *, d_s5, d_inner, nheads):
    q = M2_CHUNK
    rows_step = SSD_CHUNKS_PER_STEP * q if seq % (SSD_CHUNKS_PER_STEP * q) == 0 else q
    nc = seq // rows_step
    conv_dim = d_inner + 2 * M2_NGROUPS * M2_DSTATE
    xw = 512
    xoff = d_s5 + d_inner
    assert d_s5 % d_inner == 0 and xoff % xw == 0 and conv_dim % xw == 0
    assert M2_CHUNK == 2 * M2_HEADDIM == M2_DSTATE == LANES
    zblk = d_s5 // d_inner
    n_xparts = conv_dim // xw
    m = nb * seq
    pad = lambda v: jnp.pad(v, (0, LANES - v.shape[0])).reshape(1, LANES)
    dvec = jnp.repeat(p['m2_d'], M2_HEADDIM).reshape(1, d_inner)
    vec = lambda n: pl.BlockSpec((1, n), lambda b, c: (0, 0))
    tok = lambda w, j: pl.BlockSpec((rows_step, w), lambda b, c, j=j: (b * nc + c, j))
    out, ssm, conv = pl.pallas_call(
        functools.partial(_ssd_chunk_kernel, d_inner=d_inner, nheads=nheads, n_xparts=n_xparts),
        grid=(nb, nc),
        in_specs=[tok(xw, xoff // xw + i) for i in range(n_xparts)] + [tok(d_inner, zblk), tok(LANES, 0),
                  pl.BlockSpec((M2_CONV, conv_dim), lambda b, c: (0, 0)), vec(conv_dim), vec(LANES), vec(LANES),
                  vec(d_inner), vec(d_inner)],
        out_specs=[tok(d_inner, 0),
                   pl.BlockSpec((None, nheads * M2_HEADDIM, M2_DSTATE), lambda b, c: (b, 0, 0)),
                   pl.BlockSpec((None, M2_CONV - 1, conv_dim), lambda b, c: (b, 0, 0))],
        out_shape=[jax.ShapeDtypeStruct((m, d_inner), BF16),
                   jax.ShapeDtypeStruct((nb, nheads * M2_HEADDIM, M2_DSTATE), F32),
                   jax.ShapeDtypeStruct((nb, M2_CONV - 1, conv_dim), F32)],
        scratch_shapes=[pltpu.VMEM((nheads * M2_HEADDIM, M2_DSTATE), F32),
                        pltpu.VMEM((rows_step + 8, conv_dim), F32),
                        pltpu.VMEM((q, d_inner), F32)],
        compiler_params=_cparams(("parallel", "arbitrary")),
        name="ssd_chunk",
    )(*([h] * n_xparts), h, dt_raw, p['m2_conv_w'], p['m2_conv_b'].reshape(1, conv_dim), pad(p['m2_dt_bias']),
      pad(p['m2_a_log']), dvec, p['m2_norm_w'].reshape(1, d_inner))
    return out, ssm.reshape(nb, nheads, M2_HEADDIM, M2_DSTATE), conv


SSD_STEP_SEQS = 8


def _ssd_step_kernel(*refs, d_inner, nheads, n_xparts):
    xparts = refs[:n_xparts]
    (z_ref, dt_ref, cs0_ref, cs1_ref, cs2_ref, cw_ref, cb_ref, dtb_ref, alog_ref, dvec_ref, nw_ref, st_ref,
     out_ref, so_ref, lhs_sc, bfull_sc, ct_sc, yt_sc, xs_sc) = refs[n_xparts:]
    i = pl.program_id(0)
    nb = z_ref.shape[0]
    ns = M2_DSTATE
    rows_g = (nheads // M2_NGROUPS) * M2_HEADDIM

    @pl.when(i == 0)
    def _():
        cw = cw_ref[...]
        xbc = jnp.concatenate([xr[...] for xr in xparts], axis=1)
        conv = (cb_ref[...] + cw[3:4, :] * xbc + cw[2:3, :] * cs2_ref[...]
                + cw[1:2, :] * cs1_ref[...] + cw[0:1, :] * cs0_ref[...])
        xc = _silu(conv)
        dt = _softplus(dt_ref[...] + dtb_ref[...])
        dec = jnp.exp(dt * (-jnp.exp(alog_ref[...])))
        hrow = lax.broadcasted_iota(jnp.int32, (LANES, d_inner), 0)
        hcol = lax.broadcasted_iota(jnp.int32, (LANES, d_inner), 1)
        expand = jnp.where(hcol // M2_HEADDIM == hrow, 1.0, 0.0).astype(BF16)

        def expand_heads(v):
            a, b_, c = _split3(v)
            return _dot(a, expand) + (_dot(b_, expand) + _dot(c, expand))

        xs = xc[:, :d_inner]
        xs_sc[...] = xs
        xdt_t = (xs * expand_heads(dt)).T
        d_hi, d_mid, d_lo = _split3(expand_heads(dec).T)
        for g in range(M2_NGROUPS):
            r = slice(g * rows_g, (g + 1) * rows_g)
            lhs_sc[g] = jnp.concatenate([xdt_t[r].astype(BF16), d_hi[r], d_mid[r], d_lo[r]], axis=1)
            b_g = xc[:, d_inner + g * ns:d_inner + (g + 1) * ns]
            bfull_sc[g] = jnp.concatenate([b_g, jnp.zeros_like(b_g)], axis=1)
            c_g = xc[:, d_inner + (M2_NGROUPS + g) * ns:d_inner + (M2_NGROUPS + g + 1) * ns]
            ct_sc[g] = c_g.T
        yt_sc[...] = jnp.zeros_like(yt_sc)

    row_id = lax.broadcasted_iota(jnp.int32, (nb, 2 * ns), 0)
    lane_id = lax.broadcasted_iota(jnp.int32, (nb, 2 * ns), 1)
    col_id = lax.broadcasted_iota(jnp.int32, (ns, nb), 1)
    for j in range(st_ref.shape[0]):
        b = i * st_ref.shape[0] + j
        r_bot = jnp.where((row_id == b) & (lane_id >= ns), 1.0, 0.0).astype(BF16)
        for g in range(M2_NGROUPS):
            r = pl.ds(g * rows_g, rows_g)
            r_top = jnp.where(row_id == b, bfull_sc[g], 0.0).astype(BF16)
            rhs = jnp.concatenate([r_top, r_bot, r_bot, r_bot], axis=0)
            o = _dot(lhs_sc[g], rhs)
            hnew = st_ref[j, r, :] * o[:, ns:] + o[:, :ns]
            so_ref[j, r, :] = hnew
            cm = jnp.where(col_id == b, ct_sc[g], 0.0).astype(BF16)
            yt_sc[r, :] += _dot(hnew.astype(BF16), cm)

    @pl.when(i == pl.num_programs(0) - 1)
    def _():
        y = yt_sc[...].T + dvec_ref[...] * xs_sc[...]
        out_ref[...] = _rms(y * _silu(z_ref[...]), nw_ref[...]).astype(out_ref.dtype)


def ssd_sample(h, dt_raw, state, conv_state, p, *, d_s5, d_inner, nheads):
    nb = h.shape[0]
    conv_dim = d_inner + 2 * M2_NGROUPS * M2_DSTATE
    xw = 512
    xoff = d_s5 + d_inner
    assert nb == LANES and M2_DSTATE == LANES and M2_CONV == 4
    assert xoff % xw == 0 and conv_dim % xw == 0 and d_s5 % d_inner == 0 and nb % SSD_STEP_SEQS == 0
    n_xparts = conv_dim // xw
    rows = nheads * M2_HEADDIM
    rows_g = rows // M2_NGROUPS
    pad = lambda v: jnp.pad(v, (0, LANES - v.shape[0])).reshape(1, LANES)
    dvec = jnp.repeat(p['m2_d'], M2_HEADDIM).reshape(1, d_inner)
    full = lambda a, b, j=0: pl.BlockSpec((a, b), lambda i, j=j: (0, j))
    st_spec = pl.BlockSpec((SSD_STEP_SEQS, rows, M2_DSTATE), lambda i: (i, 0, 0))
    out, new_state = pl.pallas_call(
        functools.partial(_ssd_step_kernel, d_inner=d_inner, nheads=nheads, n_xparts=n_xparts),
        grid=(nb // SSD_STEP_SEQS,),
        in_specs=[full(nb, xw, xoff // xw + k) for k in range(n_xparts)]
        + [full(nb, d_inner, d_s5 // d_inner), full(nb, LANES)]
        + [full(nb, conv_dim)] * 3
        + [full(M2_CONV, conv_dim), full(1, conv_dim), full(1, LANES), full(1, LANES), full(1, d_inner),
           full(1, d_inner), st_spec],
        out_specs=[full(nb, d_inner), st_spec],
        out_shape=[jax.ShapeDtypeStruct((nb, d_inner), BF16), jax.ShapeDtypeStruct((nb, rows, M2_DSTATE), F32)],
        scratch_shapes=[pltpu.VMEM((M2_NGROUPS, rows_g, 4 * nb), BF16),
                        pltpu.VMEM((M2_NGROUPS, nb, 2 * M2_DSTATE), F32),
                        pltpu.VMEM((M2_NGROUPS, M2_DSTATE, nb), F32),
                        pltpu.VMEM((rows, nb), F32),
                        pltpu.VMEM((nb, d_inner), F32)],
        compiler_params=_cparams(("arbitrary",)),
        name="ssd_step",
    )(*([h] * n_xparts), h, dt_raw, conv_state[:, 0], conv_state[:, 1], conv_state[:, 2],
      p['m2_conv_w'], p['m2_conv_b'].reshape(1, conv_dim), pad(p['m2_dt_bias']), pad(p['m2_a_log']),
      dvec, p['m2_norm_w'].reshape(1, d_inner), state.reshape(nb, rows, M2_DSTATE))
    xbc = lax.slice_in_dim(h, xoff, xoff + conv_dim, axis=1)
    new_conv = jnp.concatenate([conv_state[:, 1:], xbc[:, None, :]], axis=1)
    return out, new_state.reshape(state.shape), new_conv


def _softmax_rows(s):
    e = jnp.exp(s - jnp.max(s, axis=-1, keepdims=True))
    return e / jnp.sum(e, axis=-1, keepdims=True)


def _attn_kernel(q_ref, k_ref, v_ref, o_ref, *, scale):
    s = _dot_nt(q_ref[...].astype(BF16), k_ref[...].astype(BF16)) * scale
    o_ref[...] = _dot(_softmax_rows(s).astype(BF16), v_ref[...].astype(BF16)).astype(o_ref.dtype)


def attention_prompt(q, k, v, nb, seq, mem, heads, *, tq):
    d = q.shape[1]
    hd = d // heads
    nq = seq // tq
    kv_spec = pl.BlockSpec((mem, hd), lambda b, h, i: (b, h))
    q_spec = pl.BlockSpec((tq, hd), lambda b, h, i: (b * nq + i, h))
    return pl.pallas_call(
        functools.partial(_attn_kernel, scale=hd ** -0.5),
        grid=(nb, heads, nq),
        in_specs=[q_spec, kv_spec, kv_spec],
        out_specs=q_spec,
        out_shape=jax.ShapeDtypeStruct(q.shape, BF16),
        compiler_params=_cparams(("parallel", "parallel", "parallel")),
        name="attention_prompt",
    )(q, k, v)


ATTN_STEP_SEQS = 4


ATTN_STEP_ROWS = 64


def _attn_step_kernel(q_ref, k_ref, v_ref, o_ref, s_sc, *, scale):
    nseq, mem, heads, hd = k_ref.shape
    ch = ATTN_STEP_ROWS
    pack = 8 // heads
    for j in range(nseq):
        q = jnp.concatenate([q_ref[j]] * pack, axis=0)

        def score(c, mx):
            rows = pl.ds(pl.multiple_of(c * ch, ch), ch)
            k = k_ref[j, rows].reshape(ch // pack, pack * heads, hd)
            s = jnp.sum(k * q, axis=-1, keepdims=True) * scale
            s_sc[pl.ds(pl.multiple_of(c * (ch // pack), ch // pack), ch // pack)] = s
            return jnp.maximum(mx, jnp.max(s, axis=0))

        mx = lax.fori_loop(0, mem // ch, score, jnp.full((pack * heads, 1), NEG, F32))
        mx1 = mx[0:heads]
        for i in range(1, pack):
            mx1 = jnp.maximum(mx1, mx[i * heads:(i + 1) * heads])
        mx = jnp.concatenate([mx1] * pack, axis=0)

        def accum(c, carry):
            den, acc = carry
            rows = pl.ds(pl.multiple_of(c * ch, ch), ch)
            v = v_ref[j, rows].reshape(ch // pack, pack * heads, hd)
            e = jnp.exp(s_sc[pl.ds(pl.multiple_of(c * (ch // pack), ch // pack), ch // pack)] - mx)
            return den + jnp.sum(e, axis=0), acc + jnp.sum(e * v, axis=0)

        den, acc = lax.fori_loop(0, mem // ch, accum,
                                 (jnp.zeros((pack * heads, 1), F32), jnp.zeros((pack * heads, hd), F32)))
        den1, acc1 = den[0:heads], acc[0:heads]
        for i in range(1, pack):
            den1 = den1 + den[i * heads:(i + 1) * heads]
            acc1 = acc1 + acc[i * heads:(i + 1) * heads]
        o_ref[j] = acc1 / den1


def attention_sample(q, k_cache, v_cache):
    b, mem, heads, hd = k_cache.shape
    nseq = ATTN_STEP_SEQS
    q_spec = pl.BlockSpec((nseq, heads, hd), lambda i: (i, 0, 0))
    kv_spec = pl.BlockSpec((nseq, mem, heads, hd), lambda i: (i, 0, 0, 0))
    out = pl.pallas_call(
        functools.partial(_attn_step_kernel, scale=hd ** -0.5),
        grid=(b // nseq,),
        in_specs=[q_spec, kv_spec, kv_spec],
        out_specs=q_spec,
        out_shape=jax.ShapeDtypeStruct((b, heads, hd), F32),
        scratch_shapes=[pltpu.VMEM((mem * heads // 8, 8, 1), F32)],
        compiler_params=_cparams(("parallel",)),
        name="attention_step",
    )(q.reshape(b, heads, hd), k_cache, v_cache)
    return out.reshape(b, heads * hd)


NEG = -1e30


INFO_G1, INFO_G2, INFO_E1, INFO_E2 = 0, 1, 2, 3


def _lane_pack(lane, values):
    out = 0.0
    for k, v in values:
        out = jnp.where(lane == k, v, out)
    return out


def _router_kernel(x_ref, nw_ref, wr_ref, br_ref, sel_ref, info_ref, *, n_experts):
    xn = _rms(x_ref[...], nw_ref[...])
    x_hi, x_lo = _split2(xn)
    w_hi, w_lo = _split2(wr_ref[...])
    hi_terms = _dot(x_hi, jnp.concatenate([w_hi, w_lo], axis=1))
    logits = hi_terms[:, :LANES] + (hi_terms[:, LANES:] + _dot(x_lo, w_hi)) + br_ref[...]
    lane = lax.broadcasted_iota(jnp.int32, logits.shape, 1)
    big = jnp.int32(2 ** 30)
    is_c = (lane >= n_experts) & (lane < n_experts + MOE_GROUPS)
    lc = jnp.where(is_c, logits, NEG)
    cmax = jnp.max(lc, axis=-1, keepdims=True)
    gsel = jnp.min(jnp.where(lc == cmax, lane, big), axis=-1, keepdims=True) - n_experts
    gate_c = 1.0 / jnp.sum(jnp.where(is_c, jnp.exp(lc - cmax), 0.0), axis=-1, keepdims=True)
    in_group = (lane < n_experts) & (lane // MOE_PER_GROUP == gsel)
    lf = jnp.where(in_group, logits, NEG)
    t1 = jnp.max(lf, axis=-1, keepdims=True)
    i1 = jnp.min(jnp.where(lf == t1, lane, big), axis=-1, keepdims=True)
    lf2 = jnp.where(lane == i1, NEG, lf)
    t2 = jnp.max(lf2, axis=-1, keepdims=True)
    i2 = jnp.min(jnp.where(lf2 == t2, lane, big), axis=-1, keepdims=True)
    r = jnp.exp(t2 - t1)
    g1 = gate_c / (1.0 + r)
    g2 = gate_c * r / (1.0 + r)
    sel_ref[...] = jnp.where((lane == i1) | (lane == i2), 1.0, 0.0)
    info_ref[...] = _lane_pack(lane, ((INFO_G1, g1), (INFO_G2, g2), (INFO_E1, i1.astype(F32)),
                                      (INFO_E2, i2.astype(F32))))


def moe_router(x, norm_w, w_coarse, b_coarse, w_fine, b_fine, *, tm):
    m, d = x.shape
    e = w_fine.shape[1]
    padw = LANES - e - MOE_GROUPS
    wr = jnp.concatenate([w_fine, w_coarse, jnp.zeros((d, padw), F32)], axis=1)
    br = jnp.concatenate([b_fine, b_coarse, jnp.zeros((padw,), F32)]).reshape(1, LANES)
    tok = pl.BlockSpec((tm, LANES), lambda i: (i, 0))
    return pl.pallas_call(
        functools.partial(_router_kernel, n_experts=e),
        grid=(m // tm,),
        in_specs=[pl.BlockSpec((tm, d), lambda i: (i, 0)), pl.BlockSpec((1, d), lambda i: (0, 0)),
                  pl.BlockSpec((d, LANES), lambda i: (0, 0)), pl.BlockSpec((1, LANES), lambda i: (0, 0))],
        out_specs=[tok, tok],
        out_shape=[jax.ShapeDtypeStruct((m, LANES), F32), jax.ShapeDtypeStruct((m, LANES), F32)],
        compiler_params=_cparams(("parallel",)),
        name="moe_router",
    )(x, norm_w.reshape(1, d), wr, br)


def _moe_rank_kernel(sel_ref, info_ref, rank_ref, counts_ref, carry_sc):
    i = pl.program_id(0)

    @pl.when(i == 0)
    def _():
        carry_sc[...] = jnp.zeros_like(carry_sc)

    sel = sel_ref[...]
    tm = sel.shape[0]
    row = lax.broadcasted_iota(jnp.int32, (tm, tm), 0)
    col = lax.broadcasted_iota(jnp.int32, (tm, tm), 1)
    before = jnp.where(row > col, 1.0, 0.0).astype(BF16)
    rank = _dot(before, sel.astype(BF16)) + carry_sc[...]
    info = info_ref[...]
    lane = lax.broadcasted_iota(jnp.int32, sel.shape, 1)
    e1 = info[:, INFO_E1:INFO_E1 + 1].astype(jnp.int32)
    e2 = info[:, INFO_E2:INFO_E2 + 1].astype(jnp.int32)
    r1 = jnp.sum(jnp.where(lane == e1, rank, 0.0), axis=-1, keepdims=True)
    r2 = jnp.sum(jnp.where(lane == e2, rank, 0.0), axis=-1, keepdims=True)
    rank_ref[...] = _lane_pack(lane, ((0, r1), (1, r2)))
    carry_sc[...] += jnp.sum(sel, axis=0, keepdims=True)

    @pl.when(i == pl.num_programs(0) - 1)
    def _():
        counts_ref[...] = carry_sc[...]


def _moe_plan_kernel(rank_ref, info_ref, counts_ref, dest_ref, tiles_ref, *, tile, n_experts):
    counts = counts_ref[...]
    ntile_e = jnp.floor((counts + (tile - 1)) * (1.0 / tile))
    padded = jnp.broadcast_to(ntile_e * tile, (8, LANES))
    r = lax.broadcasted_iota(jnp.int32, (LANES, LANES), 0)
    c = lax.broadcasted_iota(jnp.int32, (LANES, LANES), 1)
    lower = jnp.where(r < c, 1.0, 0.0).astype(BF16)
    p_hi, p_mid, p_lo = _split3(padded)
    offs = (_dot(p_hi, lower) + (_dot(p_mid, lower) + _dot(p_lo, lower)))[0:1, :]
    ends = offs + padded[0:1, :]

    info = info_ref[...]
    rank = rank_ref[...]
    lane = lax.broadcasted_iota(jnp.int32, info.shape, 1)
    e1 = info[:, INFO_E1:INFO_E1 + 1].astype(jnp.int32)
    e2 = info[:, INFO_E2:INFO_E2 + 1].astype(jnp.int32)
    d1 = jnp.sum(jnp.where(lane == e1, offs, 0.0), axis=-1, keepdims=True) + rank[:, 0:1]
    d2 = jnp.sum(jnp.where(lane == e2, offs, 0.0), axis=-1, keepdims=True) + rank[:, 1:2]
    dest_ref[...] = _lane_pack(lane, ((0, d1), (1, d2))).astype(jnp.int32)

    ends_col = jnp.broadcast_to(ends, (LANES, LANES)).T
    start = (c * tile).astype(F32)
    n_before = jnp.sum(jnp.where((ends_col <= start) & (r < n_experts), 1.0, 0.0), axis=0, keepdims=True)
    lane1 = lax.broadcasted_iota(jnp.int32, (1, LANES), 1)
    total = jnp.sum(jnp.where(lane1 == n_experts - 1, ends, 0.0), axis=-1, keepdims=True)
    tail = jnp.where(counts > 0.0, ends - tile, -1.0)
    tiles_ref[...] = jnp.zeros_like(tiles_ref)
    tiles_ref[0:1, :] = jnp.minimum(n_before, n_experts - 1.0).astype(jnp.int32)
    tiles_ref[1:2, :] = jnp.broadcast_to(total * (1.0 / tile), (1, LANES)).astype(jnp.int32)
    tiles_ref[2:3, :] = tail.astype(jnp.int32)


def _moe_dispatch_kernel(dest_ref, tails_ref, ntiles_ref, *rest, tile, n_experts, group_steps, group_offsets):
    TAIL_SEM, FILL_SEM, ROW_SEM = 0, 1, 2
    n_groups = len(group_offsets)
    nw_ref = rest[0]
    x_refs = rest[1:1 + n_groups]
    xs_ref, zero_sc, xn_sc, sems = rest[1 + n_groups:]
    i = pl.program_id(0)
    n_tiles_max = xs_ref.shape[0] // tile

    def fill_copy(j):
        return pltpu.make_async_copy(zero_sc, xs_ref.at[pl.ds(pl.multiple_of(j * tile, tile), tile)],
                                     sems.at[FILL_SEM])

    def tail_copy(e):
        return pltpu.make_async_copy(zero_sc, xs_ref.at[pl.ds(pl.multiple_of(tails_ref[e], tile), tile)],
                                     sems.at[TAIL_SEM])

    @pl.when(i == 0)
    def _():
        zero_sc[...] = jnp.zeros_like(zero_sc)

        def fill(j, _):
            fill_copy(j).start()
            return 0

        def clear(e, _):
            @pl.when(tails_ref[e] >= 0)
            def _():
                tail_copy(e).start()
            return 0

        def clear_wait(e, _):
            @pl.when(tails_ref[e] >= 0)
            def _():
                tail_copy(e).wait()
            return 0

        lax.fori_loop(ntiles_ref[0], n_tiles_max, fill, 0)
        lax.fori_loop(0, n_experts, clear, 0)
        lax.fori_loop(0, n_experts, clear_wait, 0)

    slot = i % 2

    def wait_rows(which, tmw):
        for _ in range(2):
            pltpu.make_async_copy(xn_sc.at[which, pl.ds(0, tmw)], xs_ref.at[pl.ds(0, tmw)],
                                  sems.at[ROW_SEM + which]).wait()

    for g, x_ref in enumerate(x_refs):
        @pl.when((i >= group_steps[g]) & (i < group_steps[g + 1]))
        def _(g=g, x_ref=x_ref):
            tm = x_ref.shape[0]
            xn_ref = xn_sc.at[slot, pl.ds(0, tm)]
            xn_ref[...] = _rms(x_ref[...], nw_ref[...])
            base = 2 * (group_offsets[g] + (i - group_steps[g]) * tm)
            for r in range(tm):
                for k in range(2):
                    pltpu.make_async_copy(xn_ref.at[pl.ds(r, 1)], xs_ref.at[pl.ds(dest_ref[base + 2 * r + k], 1)],
                                          sems.at[ROW_SEM + slot]).start(priority=k)

            @pl.when(i > group_steps[g])
            def _():
                wait_rows(1 - slot, tm)

            if g > 0:
                @pl.when(i == group_steps[g])
                def _():
                    wait_rows(1 - slot, x_refs[g - 1].shape[0])

            @pl.when(i == pl.num_programs(0) - 1)
            def _():
                wait_rows(slot, tm)

    @pl.when(i == pl.num_programs(0) - 1)
    def _():
        def fill_wait(j, _):
            fill_copy(j).wait()
            return 0

        lax.fori_loop(ntiles_ref[0], n_tiles_max, fill_wait, 0)


def _moe_expert_kernel(texp_ref, ntiles_ref, xs_ref, wg_hbm, wu_hbm, wd_hbm, ys_ref,
                       wg_sc, wu_sc, wd_sc, slot_sm, sems):
    i = pl.program_id(0)
    n = ntiles_ref[0]

    def weight_copies(e, slot):
        return [pltpu.make_async_copy(w.at[e], sc.at[slot], sems.at[slot])
                for w, sc in ((wg_hbm, wg_sc), (wu_hbm, wu_sc), (wd_hbm, wd_sc))]

    @pl.when(i < n)
    def _():
        e = texp_ref[i]
        first = (i == 0) | (texp_ref[jnp.maximum(i - 1, 0)] != e)

        @pl.when(i == 0)
        def _():
            slot_sm[0] = 0
            for c in weight_copies(e, 0):
                c.start()

        @pl.when(first & (i > 0))
        def _():
            slot_sm[0] = 1 - slot_sm[0]

        slot = slot_sm[0]

        @pl.when(first)
        def _():
            for c in weight_copies(e, slot):
                c.wait()
            j = lax.while_loop(lambda j: (j < n) & (texp_ref[jnp.minimum(j, n - 1)] == e), lambda j: j + 1, i + 1)

            @pl.when(j < n)
            def _():
                for c in weight_copies(texp_ref[jnp.minimum(j, n - 1)], 1 - slot):
                    c.start()

        x = xs_ref[...].astype(BF16)
        hg = _dot(x, wg_sc[slot].astype(BF16))
        hu = _dot(x, wu_sc[slot].astype(BF16))
        ys_ref[...] = _dot((_silu(hg) * hu).astype(BF16), wd_sc[slot].astype(BF16))


def _moe_combine_kernel(dest_ref, ys_ref, x_ref, info_ref, fw_ref, y_ref, buf_sc, sems):
    s = pl.program_id(0)
    n_blocks = pl.num_programs(0) - 1
    tm = x_ref.shape[0]
    slot = s % 2

    @pl.when(s < n_blocks)
    def _():
        for r in range(tm):
            for k in range(2):
                pltpu.make_async_copy(ys_ref.at[pl.ds(dest_ref[2 * (s * tm + r) + k], 1)],
                                      buf_sc.at[slot, k, pl.ds(r, 1)], sems.at[slot]).start(priority=k)

    @pl.when(s > 0)
    def _():
        prev = 1 - slot
        for k in range(2):
            pltpu.make_async_copy(ys_ref.at[pl.ds(0, tm)], buf_sc.at[prev, k], sems.at[prev]).wait()
        info = info_ref[...]
        y = (x_ref[...] + info[:, INFO_G1:INFO_G1 + 1] * buf_sc[prev, 0]
             + info[:, INFO_G2:INFO_G2 + 1] * buf_sc[prev, 1])
        y_ref[...] = _rms(y, fw_ref[...])


MOE_TILE = 256
MOE_TOKENS_PER_STEP = 256


def _largest_tile(m, cap):
    return max(t for t in range(8, cap + 1, 8) if m % t == 0)


def moe_routed_final(x_list, p, final_w):
    d = x_list[0].shape[1]
    ne, _, f = p['moe_w_gate'].shape
    sizes = [x.shape[0] for x in x_list]
    m = sum(sizes)
    tile = min(MOE_TILE, m)
    routed = [moe_router(x, p['norm_ffn_w'], p['router_coarse_w'], p['router_coarse_b'],
                         p['router_fine_w'], p['router_fine_b'], tm=min(x.shape[0], 512)) for x in x_list]
    sel = jnp.concatenate([r[0] for r in routed], axis=0)
    info = jnp.concatenate([r[1] for r in routed], axis=0)
    tm = _largest_tile(m, 1024)
    tok = pl.BlockSpec((tm, LANES), lambda i: (i, 0))
    rank, counts = pl.pallas_call(
        _moe_rank_kernel,
        grid=(m // tm,),
        in_specs=[tok, tok],
        out_specs=[tok, pl.BlockSpec((1, LANES), lambda i: (0, 0))],
        out_shape=[jax.ShapeDtypeStruct((m, LANES), F32), jax.ShapeDtypeStruct((1, LANES), F32)],
        scratch_shapes=[pltpu.VMEM((1, LANES), F32)],
        compiler_params=_cparams(("arbitrary",)),
        name="moe_rank",
    )(sel, info)
    n_tiles_max = (2 * m) // tile + ne
    assert n_tiles_max <= LANES
    dest, tiles = pl.pallas_call(
        functools.partial(_moe_plan_kernel, tile=tile, n_experts=ne),
        out_shape=[jax.ShapeDtypeStruct((m, LANES), jnp.int32), jax.ShapeDtypeStruct((8, LANES), jnp.int32)],
        compiler_params=pltpu.CompilerParams(vmem_limit_bytes=VMEM_LIMIT),
        name="moe_plan",
    )(rank, info, counts)
    dest_flat = dest[:, :2].reshape(2 * m)
    tile_expert = tiles[0, :n_tiles_max]
    n_tiles = tiles[1, :1]
    tails = tiles[2, :ne]

    rows = n_tiles_max * tile

    tds = [min(mg, 2 * MOE_TOKENS_PER_STEP) for mg in sizes]
    group_steps = [0]
    for mg, td in zip(sizes, tds):
        group_steps.append(group_steps[-1] + mg // td)
    group_offsets = [sum(sizes[:g]) for g in range(len(sizes))]

    def group_spec(g):
        first, last = group_steps[g], group_steps[g + 1] - 1
        return pl.BlockSpec((tds[g], d), lambda i, *_: (jnp.clip(i, first, last) - first, 0))

    xs = pl.pallas_call(
        functools.partial(_moe_dispatch_kernel, tile=tile, n_experts=ne, group_steps=tuple(group_steps),
                          group_offsets=tuple(group_offsets)),
        grid_spec=pltpu.PrefetchScalarGridSpec(
            num_scalar_prefetch=3, grid=(group_steps[-1],),
            in_specs=[pl.BlockSpec((1, d), lambda i, *_: (0, 0))] + [group_spec(g) for g in range(len(sizes))],
            out_specs=pl.BlockSpec(memory_space=pl.ANY),
            scratch_shapes=[pltpu.VMEM((tile, d), F32), pltpu.VMEM((2, max(tds), d), F32),
                            pltpu.SemaphoreType.DMA((4,))]),
        out_shape=jax.ShapeDtypeStruct((rows, d), F32),
        compiler_params=_cparams(("arbitrary",)),
        name="moe_dispatch",
    )(dest_flat, tails, n_tiles, p['norm_ffn_w'].reshape(1, d), *x_list)

    def tile_idx(i, te, nt):
        return jnp.minimum(i, nt[0] - 1)

    ys = pl.pallas_call(
        _moe_expert_kernel,
        grid_spec=pltpu.PrefetchScalarGridSpec(
            num_scalar_prefetch=2, grid=(n_tiles_max,),
            in_specs=[pl.BlockSpec((tile, d), lambda i, te, nt: (tile_idx(i, te, nt), 0)),
                      pl.BlockSpec(memory_space=pl.ANY), pl.BlockSpec(memory_space=pl.ANY),
                      pl.BlockSpec(memory_space=pl.ANY)],
            out_specs=pl.BlockSpec((tile, d), lambda i, te, nt: (tile_idx(i, te, nt), 0)),
            scratch_shapes=[pltpu.VMEM((2, d, f), F32), pltpu.VMEM((2, d, f), F32), pltpu.VMEM((2, f, d), F32),
                            pltpu.SMEM((1,), jnp.int32), pltpu.SemaphoreType.DMA((2,))]),
        out_shape=jax.ShapeDtypeStruct((rows, d), F32),
        input_output_aliases={2: 0},
        compiler_params=_cparams(("arbitrary",)),
        name="moe_experts",
    )(tile_expert, n_tiles, xs, p['moe_w_gate'], p['moe_w_up'], p['moe_w_down'])

    outs = []
    off = 0
    for x, (_, info_g), mg in zip(x_list, routed, sizes):
        tc = min(mg, MOE_TOKENS_PER_STEP)
        outs.append(pl.pallas_call(
            _moe_combine_kernel,
            grid_spec=pltpu.PrefetchScalarGridSpec(
                num_scalar_prefetch=1, grid=(mg // tc + 1,),
                in_specs=[pl.BlockSpec(memory_space=pl.ANY),
                          pl.BlockSpec((tc, d), lambda s, dref: (jnp.maximum(s - 1, 0), 0)),
                          pl.BlockSpec((tc, LANES), lambda s, dref: (jnp.maximum(s - 1, 0), 0)),
                          pl.BlockSpec((1, d), lambda s, dref: (0, 0))],
                out_specs=pl.BlockSpec((tc, d), lambda s, dref: (jnp.maximum(s - 1, 0), 0)),
                scratch_shapes=[pltpu.VMEM((2, 2, tc, d), F32), pltpu.SemaphoreType.DMA((2,))]),
            out_shape=jax.ShapeDtypeStruct((mg, d), F32),
            compiler_params=_cparams(("arbitrary",)),
            name="moe_combine",
        )(dest_flat[2 * off:2 * (off + mg)], ys, x, info_g, final_w.reshape(1, d)))
        off += mg
    return outs


def s5_prompt(h, nb, seq, prep, p):
    y, hfin = s5_scan(h, prep, nb, seq)
    out = s5_head(y, h, p['s5_d'], p['s5_glu_w'], p['s5_glu_b'], p['s5_norm_w'], tm=min(1024, nb * seq))
    return out, hfin[:, 0].transpose(1, 0, 2), hfin[:, 1].transpose(1, 0, 2)


def s5_sample(h, st_re, st_im, prep, p):
    y, n_re, n_im = s5_step(h, st_re, st_im, prep, p['s5_c_re'], p['s5_c_im'])
    out = s5_head(y, h, p['s5_d'], p['s5_glu_w'], p['s5_glu_b'], p['s5_norm_w'], tm=h.shape[0])
    return out, n_re, n_im


def _row_tile(m):
    return min(m, 1024)


def _col_tile(m, n):
    if m <= 256:
        return n
    return max(t for t in range(256, 1793, 256) if n % t == 0)


def _mixer_and_attention(x, p, s5_prep, *, nb, seq, mem_kv, xa_heads, states):
    m, d = x.shape
    g, n = p['s5_a_re'].shape
    d_s5 = g * S5_CH
    nheads = p['m2_a_log'].shape[0]
    d_inner = nheads * M2_HEADDIM
    conv_dim = d_inner + 2 * M2_NGROUPS * M2_DSTATE
    n_main = d_s5 + d_inner + conv_dim
    tm = _row_tile(m)

    w_in = p['w_in']
    w_dt = jnp.pad(w_in[:, n_main:], ((0, 0), (0, LANES - nheads)))
    tm2, tn2 = tm, _col_tile(m, d)
    h, dt_raw = fused_matmul([x], w_in, n_out=n_main, gain=p['norm_mix_w'], side_w=w_dt, tm=tm,
                             tn=_col_tile(m, n_main))

    if states is None:
        s5_out, s5_re, s5_im = s5_prompt(h, nb, seq, s5_prep, p)
        m2_out, ssm, conv = ssd_prompt(h, dt_raw, nb, seq, p, d_s5=d_s5, d_inner=d_inner, nheads=nheads)
    else:
        s5_out, s5_re, s5_im = s5_sample(h, states[0], states[1], s5_prep, p)
        m2_out, ssm, conv = ssd_sample(h, dt_raw, states[2], states[3], p, d_s5=d_s5, d_inner=d_inner,
                                       nheads=nheads)
    x1 = fused_matmul([s5_out, m2_out], p['w_out'], n_out=d, res=x, tm=tm2, tn=tn2)

    q = fused_matmul([x1], p['xa_wq'], n_out=d, gain=p['norm_xa_w'], tm=tm2, tn=tn2,
                     out_dtype=BF16 if states is None else F32)
    if states is None:
        mem = mem_kv[0].shape[0] // nb
        o = attention_prompt(q, mem_kv[0], mem_kv[1], nb, seq, mem, xa_heads, tq=min(seq, 2048))
    else:
        o = attention_sample(q, mem_kv[0], mem_kv[1])
    x2 = fused_matmul([o], p['xa_wo'], n_out=d, res=x1, tm=tm2, tn=tn2)

    return x2, s5_re, s5_im, ssm, conv


def kernel(x_prompt, x_sample, mem_prompt, state_s5_re, state_s5_im, state_ssm, state_conv, cache_mem_k, cache_mem_v, norm_mix_w, w_in, s5_a_re, s5_a_im, s5_log_dt, s5_b_re, s5_b_im, s5_c_re, s5_c_im, s5_d, s5_glu_w, s5_glu_b, s5_norm_w, m2_conv_w, m2_conv_b, m2_dt_bias, m2_a_log, m2_d, m2_norm_w, w_out, norm_xa_w, norm_mem_w, xa_wq, xa_wk, xa_wv, xa_wo, norm_ffn_w, router_coarse_w, router_coarse_b, router_fine_w, router_fine_b, moe_w_gate, moe_w_up, moe_w_down, norm_final_w):
    depth = w_in.shape[0]
    assert depth == 1, "the final norm is fused into the (only) layer"
    per_layer = dict(
        norm_mix_w=norm_mix_w, w_in=w_in, s5_a_re=s5_a_re, s5_a_im=s5_a_im, s5_log_dt=s5_log_dt,
        s5_b_re=s5_b_re, s5_b_im=s5_b_im, s5_c_re=s5_c_re, s5_c_im=s5_c_im, s5_d=s5_d, s5_glu_w=s5_glu_w,
        s5_glu_b=s5_glu_b, s5_norm_w=s5_norm_w, m2_conv_w=m2_conv_w, m2_conv_b=m2_conv_b, m2_dt_bias=m2_dt_bias,
        m2_a_log=m2_a_log, m2_d=m2_d, m2_norm_w=m2_norm_w, w_out=w_out, norm_xa_w=norm_xa_w,
        norm_mem_w=norm_mem_w, xa_wq=xa_wq, xa_wk=xa_wk, xa_wv=xa_wv, xa_wo=xa_wo, norm_ffn_w=norm_ffn_w,
        router_coarse_w=router_coarse_w, router_coarse_b=router_coarse_b, router_fine_w=router_fine_w,
        router_fine_b=router_fine_b, moe_w_gate=moe_w_gate, moe_w_up=moe_w_up, moe_w_down=moe_w_down)
    p = {k: v[0] for k, v in per_layer.items()}
    for name in ('w_in', 'w_out', 'xa_wq', 'xa_wo'):
        p[name] = p[name].astype(BF16)
    nb, seq, d = x_prompt.shape
    db, dseq, _ = x_sample.shape
    assert dseq == 1
    mem = mem_prompt.shape[1]
    xa_heads = cache_mem_k.shape[3]

    s5_prep = s5_prepare(p['s5_a_re'], p['s5_a_im'], p['s5_log_dt'], p['s5_b_re'], p['s5_b_im'],
                         p['s5_c_re'], p['s5_c_im'])

    memx = mem_prompt.reshape(nb * mem, d)
    mk = fused_matmul([memx], p['xa_wk'], n_out=d, gain=p['norm_mem_w'], tm=_row_tile(nb * mem),
                      tn=_col_tile(nb * mem, d))
    mv = fused_matmul([memx], p['xa_wv'], n_out=d, gain=p['norm_mem_w'], tm=_row_tile(nb * mem),
                      tn=_col_tile(nb * mem, d))
    xp, p_re, p_im, p_ssm, p_conv = _mixer_and_attention(
        x_prompt.reshape(nb * seq, d), p, s5_prep, nb=nb, seq=seq, mem_kv=(mk, mv), xa_heads=xa_heads, states=None)

    xs, s_re, s_im, s_ssm, s_conv = _mixer_and_attention(
        x_sample.reshape(db, d), p, s5_prep, nb=db, seq=1,
        mem_kv=(cache_mem_k[0], cache_mem_v[0]), xa_heads=xa_heads,
        states=(state_s5_re[0], state_s5_im[0], state_ssm[0], state_conv[0]))

    yp, ys = moe_routed_final([xp, xs], p, norm_final_w)

    kv_shape = (1, nb, mem) + cache_mem_k.shape[3:]
    return (yp.reshape(nb, seq, d), ys.reshape(db, 1, d), p_re[None], p_im[None], p_ssm[None], p_conv[None],
            mk.reshape(kv_shape), mv.reshape(kv_shape), s_re[None], s_im[None], s_ssm[None], s_conv[None])
```

```python
import functools
import math

import jax
import jax.numpy as jnp
from jax import lax
from jax.experimental import pallas as pl
from jax.experimental.pallas import tpu as pltpu

F32 = jnp.float32
BF16 = jnp.bfloat16
RMS_EPS = 1e-6

V7X_VMEM_BYTES = 64 * 1024 * 1024
VMEM_LIMIT = V7X_VMEM_BYTES - 8 * 1024 * 1024
LANES = 128

S5_CH = 16
S5_N = 64
S5_Q = 16
S5_GB = 8
S5_SEQ_PAD = 8
M2_HEADDIM = 64
M2_DSTATE = 128
M2_NGROUPS = 2
M2_CONV = 4
M2_CHUNK = 128
MOE_GROUPS = 4
MOE_PER_GROUP = 8


def _cparams(sem):
    return pltpu.CompilerParams(dimension_semantics=sem, vmem_limit_bytes=VMEM_LIMIT)


def _rms(x, w):
    return x * lax.rsqrt(jnp.mean(x * x, axis=-1, keepdims=True) + RMS_EPS) * w


def _sigmoid(x):
    return 1.0 / (1.0 + jnp.exp(-x))


def _silu(x):
    return x * _sigmoid(x)


def _softplus(x):
    return jnp.maximum(x, 0.0) + jnp.log1p(jnp.exp(-jnp.abs(x)))


def _gelu_tanh(x):
    return 0.5 * x * (1.0 + jnp.tanh(math.sqrt(2.0 / math.pi) * (x + 0.044715 * (x * x * x))))


def _dot(a, b):
    return jnp.dot(a, b, preferred_element_type=F32)


def _dot_nt(a, b):
    return lax.dot_general(a, b, (((1,), (1,)), ((), ())), preferred_element_type=F32)


def _split3(x):
    hi = x.astype(BF16)
    r = x - hi.astype(F32)
    mid = r.astype(BF16)
    lo = (r - mid.astype(F32)).astype(BF16)
    return hi, mid, lo


def _split2(x):
    hi = x.astype(BF16)
    lo = (x - hi.astype(F32)).astype(BF16)
    return hi, lo


def _mm_kernel(*refs, n_lhs, has_gain, has_res, has_side, staged):
    it = iter(refs)
    lhs = [next(it) for _ in range(n_lhs)]
    gain = next(it) if has_gain else None
    ws = [next(it) for _ in range(n_lhs)]
    side_w = next(it) if has_side else None
    res = next(it) if has_res else None
    out = next(it)
    side_out = next(it) if has_side else None
    lhs_bf = next(it) if staged else lhs

    if staged:
        @pl.when(pl.program_id(1) == 0)
        def _():
            for i in range(n_lhs):
                x = lhs[i][...]
                if has_gain:
                    x = _rms(x, gain[...])
                lhs_bf[i] = x.astype(BF16)
            if has_side:
                side_out[...] = _dot(lhs_bf[0], side_w[...].astype(BF16))

    acc = None
    for i in range(n_lhs):
        p = _dot(lhs_bf[i][...], ws[i][...].astype(BF16))
        acc = p if acc is None else acc + p
    if has_res:
        acc = acc + res[...]
    out[...] = acc.astype(out.dtype)


def fused_matmul(lhs_list, w, *, n_out, gain=None, res=None, side_w=None, out_dtype=F32, tm, tn):
    n_lhs = len(lhs_list)
    m, kp = lhs_list[0].shape
    assert all(a.shape == (m, kp) for a in lhs_list)
    assert w.shape[0] == n_lhs * kp and m % tm == 0 and n_out % tn == 0
    assert gain is None or n_lhs == 1
    staged = gain is not None or any(a.dtype != BF16 for a in lhs_list)
    assert staged or side_w is None
    grid = (m // tm, n_out // tn)
    in_specs = [pl.BlockSpec((tm, kp), lambda i, j: (i, 0)) for _ in range(n_lhs)]
    args = list(lhs_list)
    if gain is not None:
        in_specs.append(pl.BlockSpec((1, kp), lambda i, j: (0, 0)))
        args.append(gain.reshape(1, kp))
    for p in range(n_lhs):
        in_specs.append(pl.BlockSpec((kp, tn), lambda i, j, p=p: (p, j)))
        args.append(w)
    if side_w is not None:
        in_specs.append(pl.BlockSpec((kp, LANES), lambda i, j: (0, 0)))
        args.append(side_w)
    if res is not None:
        in_specs.append(pl.BlockSpec((tm, tn), lambda i, j: (i, j)))
        args.append(res)
    out_shape = [jax.ShapeDtypeStruct((m, n_out), out_dtype)]
    out_specs = [pl.BlockSpec((tm, tn), lambda i, j: (i, j))]
    if side_w is not None:
        out_shape.append(jax.ShapeDtypeStruct((m, LANES), F32))
        out_specs.append(pl.BlockSpec((tm, LANES), lambda i, j: (i, 0)))
    outs = pl.pallas_call(
        functools.partial(_mm_kernel, n_lhs=n_lhs, has_gain=gain is not None,
                          has_res=res is not None, has_side=side_w is not None, staged=staged),
        grid=grid, in_specs=in_specs, out_specs=out_specs, out_shape=out_shape,
        scratch_shapes=[pltpu.VMEM((n_lhs, tm, kp), BF16)] if staged else [],
        compiler_params=_cparams(("parallel", "arbitrary")),
        name="fused_matmul",
    )(*args)
    return outs if side_w is not None else outs[0]


def _s5_prep_kernel(*refs):
    for g in range(refs[0].shape[0]):
        _s5_prep_group(*[r.at[g] for r in refs])


def _s5_prep_group(lre_ref, lim_ref, ldt_ref, btre_ref, btim_ref, cre_ref, cim_ref,
                   tz_ref, wsre_ref, wsim_ref, wcre_ref, wcim_ref, aq_ref, ab_ref, bbt_ref):
    q, ch = S5_Q, S5_CH
    lr = lre_ref[...]
    li = lim_ref[...]
    step = jnp.exp(ldt_ref[...])
    mag = jnp.exp(lr * step)
    ab_re = mag * jnp.cos(li * step)
    ab_im = mag * jnp.sin(li * step)
    den = lr * lr + li * li
    num_re = ab_re - 1.0
    coef_re = (num_re * lr + ab_im * li) / den
    coef_im = (ab_im * lr - num_re * li) / den
    bt_re = btre_ref[...]
    bt_im = btim_ref[...]
    bb_re = coef_re * bt_re - coef_im * bt_im
    bb_im = coef_re * bt_im + coef_im * bt_re
    c_re = cre_ref[...]
    c_im = cim_ref[...]

    pw = [(jnp.ones_like(ab_re), jnp.zeros_like(ab_re))]
    for _ in range(q):
        pr, pi = pw[-1]
        pw.append((pr * ab_re - pi * ab_im, pr * ab_im + pi * ab_re))

    ca_re = [c_re * pr - c_im * pi for pr, pi in pw]
    ca_im = [c_re * pi + c_im * pr for pr, pi in pw]
    wcre_ref[...] = jnp.concatenate(ca_re[1:], axis=0).astype(BF16)
    wcim_ref[...] = jnp.concatenate([-x for x in ca_im[1:]], axis=0).astype(BF16)

    pr_stack = jnp.concatenate(ca_re[:q], axis=0)
    pi_stack = jnp.concatenate(ca_im[:q], axis=0)
    krow = None
    for a, b, sign in ((bb_re, pr_stack, 1.0), (bb_im, pi_stack, -1.0)):
        a_hi, a_lo = _split2(a)
        b_hi, b_lo = _split2(b)
        t = _dot_nt(a_hi, b_hi) + (_dot_nt(a_hi, b_lo) + _dot_nt(a_lo, b_hi))
        krow = sign * t if krow is None else krow + sign * t
    lane = lax.broadcasted_iota(jnp.int32, krow.shape, 1)
    blocks = [krow]
    for s in range(1, q):
        blocks.append(jnp.where(lane >= s * ch, pltpu.roll(krow, s * ch, 1), 0.0))
    tz_ref[...] = jnp.concatenate(blocks, axis=0).astype(BF16)

    ws_re, ws_im = [], []
    for s in range(q):
        pr, pi = pw[q - 1 - s]
        ws_re.append(bb_re * pr - bb_im * pi)
        ws_im.append(bb_re * pi + bb_im * pr)
    wsre_ref[...] = jnp.concatenate(ws_re, axis=0).astype(BF16)
    wsim_ref[...] = jnp.concatenate(ws_im, axis=0).astype(BF16)

    aq_ref[0:1, :] = pw[q][0]
    aq_ref[1:2, :] = pw[q][1]
    ab_ref[0:1, :] = ab_re
    ab_ref[1:2, :] = ab_im
    bbt_ref[0:ch, :] = bb_re
    bbt_ref[ch:2 * ch, :] = bb_im


def s5_prepare(a_re, a_im, log_dt, b_re, b_im, c_re, c_im):
    g, n = a_re.shape
    ch, q = S5_CH, S5_Q
    qc = q * ch
    bt_re = jnp.swapaxes(b_re, 1, 2)
    bt_im = jnp.swapaxes(b_im, 1, 2)

    def per_g(*dims):
        return pl.BlockSpec((S5_GB,) + dims, lambda i: (i,) + (0,) * len(dims))

    return pl.pallas_call(
        _s5_prep_kernel,
        grid=(g // S5_GB,),
        in_specs=[per_g(1, n), per_g(1, n), per_g(1, 1), per_g(ch, n), per_g(ch, n), per_g(ch, n), per_g(ch, n)],
        out_specs=[per_g(qc, qc), per_g(qc, n), per_g(qc, n), per_g(qc, n), per_g(qc, n),
                   per_g(2, n), per_g(2, n), per_g(2 * ch, n)],
        out_shape=[jax.ShapeDtypeStruct((g, qc, qc), BF16),
                   jax.ShapeDtypeStruct((g, qc, n), BF16), jax.ShapeDtypeStruct((g, qc, n), BF16),
                   jax.ShapeDtypeStruct((g, qc, n), BF16), jax.ShapeDtypeStruct((g, qc, n), BF16),
                   jax.ShapeDtypeStruct((g, 2, n), F32), jax.ShapeDtypeStruct((g, 2, n), F32),
                   jax.ShapeDtypeStruct((g, 2 * ch, n), F32)],
        compiler_params=_cparams(("parallel",)),
        name="s5_prepare",
    )(a_re.reshape(g, 1, n), a_im.reshape(g, 1, n), log_dt.reshape(g, 1, 1), bt_re, bt_im, c_re, c_im)


def _s5_scan_kernel(h_ref, tz_ref, wsre_ref, wsim_ref, wcre_ref, wcim_ref, aq_ref,
                    y_ref, hfin_ref, xs_sc, u_sc, yg_sc, sre_sc, sim_sc, *, nb, nchunk):
    gb, q, ch = S5_GB, S5_Q, S5_CH
    rows = nb * nchunk
    seq_stride = nchunk + S5_SEQ_PAD
    per_vreg = LANES // ch
    assert gb == per_vreg and q % per_vreg == 0
    slot = lax.broadcasted_iota(jnp.int32, (rows, LANES), 1) // ch

    halves = q // per_vreg

    def rot_rows(w, g):
        if g == 0:
            return w
        cut = (per_vreg - g) * ch
        parts = []
        for hf in range(halves):
            blk = w[hf * LANES:(hf + 1) * LANES]
            parts += [blk[cut:], blk[:cut]]
        return jnp.concatenate(parts, axis=0)

    for s in range(q):
        x = h_ref[pl.ds(s, rows, stride=q), :].astype(BF16)
        k = s % per_vreg
        xs_sc[s] = pltpu.roll(x, k * ch, 1) if k else x
    keep = [jnp.where(slot == j, 1.0, 0.0).astype(BF16) for j in range(per_vreg)]
    for g in range(gb):
        for hf in range(halves):
            acc = None
            for k in range(per_vreg):
                piece = xs_sc[hf * per_vreg + k] * keep[(g + k) % per_vreg]
                acc = piece if acc is None else acc + piece
            u_sc[g, :, hf * LANES:(hf + 1) * LANES] = acc

    for g in range(gb):
        u = u_sc[g]
        tz = rot_rows(tz_ref[g], g)
        tz = jnp.concatenate([pltpu.roll(tz[:, hf * LANES:(hf + 1) * LANES], g * ch, 1) if g
                              else tz[:, hf * LANES:(hf + 1) * LANES] for hf in range(halves)], axis=1)
        yg_sc[g] = _dot(u, tz)
        for sc, w_ref in ((sre_sc, wsre_ref), (sim_sc, wsim_ref)):
            s_all = _dot(u, rot_rows(w_ref[g], g))
            for b in range(nb):
                sc[g, b * seq_stride:b * seq_stride + nchunk, :] = s_all[b * nchunk:(b + 1) * nchunk]

    ar = [jnp.broadcast_to(aq_ref[g, 0:1, :], (nb, S5_N)) for g in range(gb)]
    ai = [jnp.broadcast_to(aq_ref[g, 1:2, :], (nb, S5_N)) for g in range(gb)]

    def step(c, carry):
        at = pl.ds(c, nb, stride=seq_stride)
        new = []
        for g in range(gb):
            hr, hi = carry[g]
            sr = sre_sc[g, at, :]
            si = sim_sc[g, at, :]
            sre_sc[g, at, :] = hr
            sim_sc[g, at, :] = hi
            new.append((ar[g] * hr - ai[g] * hi + sr, ar[g] * hi + ai[g] * hr + si))
        return tuple(new)

    zero = jnp.zeros((nb, S5_N), F32)
    fin = lax.fori_loop(0, nchunk, step, tuple((zero, zero) for _ in range(gb)), unroll=4)
    for g in range(gb):
        hfin_ref[g, 0] = fin[g][0]
        hfin_ref[g, 1] = fin[g][1]
        h_in = [jnp.concatenate([sc[g, b * seq_stride:b * seq_stride + nchunk, :] for b in range(nb)], axis=0)
                for sc in (sre_sc, sim_sc)]
        yg_sc[g] += (_dot_nt(h_in[0].astype(BF16), rot_rows(wcre_ref[g], g))
                     + _dot_nt(h_in[1].astype(BF16), rot_rows(wcim_ref[g], g)))

    for t in range(q):
        hf, k = divmod(t, per_vreg)
        acc = jnp.zeros((rows, LANES), F32)
        for g in range(gb):
            acc = jnp.where(slot == (k + g) % per_vreg, yg_sc[g, :, hf * LANES:(hf + 1) * LANES], acc)
        y_ref[pl.ds(t, rows, stride=q), :] = pltpu.roll(acc, (per_vreg - k) * ch, 1) if k else acc


def s5_scan(h, prep, nb, seq):
    tz, ws_re, ws_im, wc_re, wc_im, aq = prep[:6]
    g, qc, _ = tz.shape
    n = S5_N
    gb = S5_GB
    q = S5_Q
    assert gb * S5_CH == LANES and qc == q * S5_CH and seq % q == 0
    nchunk = seq // q
    rows = nb * nchunk
    m = nb * seq

    def blk(*dims):
        return pl.BlockSpec((gb,) + dims, lambda i: (i,) + (0,) * len(dims))

    return pl.pallas_call(
        functools.partial(_s5_scan_kernel, nb=nb, nchunk=nchunk),
        grid=(g // gb,),
        in_specs=[pl.BlockSpec((m, LANES), lambda i: (0, i)),
                  blk(qc, qc), blk(qc, n), blk(qc, n), blk(qc, n), blk(qc, n), blk(2, n)],
        out_specs=[pl.BlockSpec((m, LANES), lambda i: (0, i)), blk(2, nb, n)],
        out_shape=[jax.ShapeDtypeStruct((m, g * S5_CH), F32), jax.ShapeDtypeStruct((g, 2, nb, n), F32)],
        scratch_shapes=[pltpu.VMEM((q, rows, LANES), BF16), pltpu.VMEM((gb, rows, qc), BF16),
                        pltpu.VMEM((gb, rows, qc), F32),
                        pltpu.VMEM((gb, nb * (nchunk + S5_SEQ_PAD), n), F32),
                        pltpu.VMEM((gb, nb * (nchunk + S5_SEQ_PAD), n), F32)],
        compiler_params=_cparams(("parallel",)),
        name="s5_scan",
    )(h, tz, ws_re, ws_im, wc_re, wc_im, aq)


def _s5_step_kernel(u_ref, hre_ref, him_ref, ab_ref, bbt_ref, cre_ref, cim_ref, y_ref, ore_ref, oim_ref):
    gb = hre_ref.shape[1]
    ch = S5_CH
    for g in range(gb):
        u = u_ref[:, g * ch:(g + 1) * ch].astype(BF16)
        bb_re = bbt_ref[g, 0:ch, :].astype(BF16)
        bb_im = bbt_ref[g, ch:2 * ch, :].astype(BF16)
        ar = ab_ref[g, 0:1, :]
        ai = ab_ref[g, 1:2, :]
        hr0 = hre_ref[:, g, :]
        hi0 = him_ref[:, g, :]
        hr = _dot(u, bb_re) + (ar * hr0 - ai * hi0)
        hi = _dot(u, bb_im) + (ar * hi0 + ai * hr0)
        ore_ref[:, g, :] = hr
        oim_ref[:, g, :] = hi
        y_ref[:, g * ch:(g + 1) * ch] = (_dot_nt(hr.astype(BF16), cre_ref[g].astype(BF16))
                                         - _dot_nt(hi.astype(BF16), cim_ref[g].astype(BF16)))


def s5_step(h, h_re, h_im, prep, c_re, c_im):
    ab, bbt = prep[6], prep[7]
    b, g, n = h_re.shape
    ch = S5_CH
    gb = S5_GB
    assert gb * ch == LANES

    def blk(*dims):
        return pl.BlockSpec((gb,) + dims, lambda i: (i,) + (0,) * len(dims))

    tok = pl.BlockSpec((b, LANES), lambda i: (0, i))
    st = pl.BlockSpec((b, gb, n), lambda i: (0, i, 0))
    return pl.pallas_call(
        _s5_step_kernel,
        grid=(g // gb,),
        in_specs=[tok, st, st, blk(2, n), blk(2 * ch, n), blk(ch, n), blk(ch, n)],
        out_specs=[tok, st, st],
        out_shape=[jax.ShapeDtypeStruct((b, g * ch), F32), jax.ShapeDtypeStruct((b, g, n), F32),
                   jax.ShapeDtypeStruct((b, g, n), F32)],
        compiler_params=_cparams(("parallel",)),
        name="s5_step",
    )(h, h_re, h_im, ab, bbt, c_re, c_im)


def _s5_head_kernel(y_ref, u_ref, d_ref, w_ref, b_ref, nw_ref, o_ref):
    y = y_ref[...] + d_ref[...] * u_ref[...]
    g = _gelu_tanh(y)
    gate = _sigmoid(_dot(g.astype(BF16), w_ref[...].astype(BF16)) + b_ref[...])
    o_ref[...] = _rms(g * gate, nw_ref[...]).astype(o_ref.dtype)


def s5_head(y, h, d, glu_w, glu_b, norm_w, *, tm):
    m, ds = y.shape
    row = lambda a: a.reshape(1, ds)
    vec = pl.BlockSpec((1, ds), lambda i: (0, 0))
    return pl.pallas_call(
        _s5_head_kernel,
        grid=(m // tm,),
        in_specs=[pl.BlockSpec((tm, ds), lambda i: (i, 0)), pl.BlockSpec((tm, ds), lambda i: (i, 0)), vec,
                  pl.BlockSpec((ds, ds), lambda i: (0, 0)), vec, vec],
        out_specs=pl.BlockSpec((tm, ds), lambda i: (i, 0)),
        out_shape=jax.ShapeDtypeStruct((m, ds), BF16),
        compiler_params=_cparams(("parallel",)),
        name="s5_head",
    )(y, h, row(d), glu_w, row(glu_b), row(norm_w))


SSD_CHUNKS_PER_STEP = 4


def _pair_select(first, col0, col1, shape):
    return jnp.where(first, jnp.broadcast_to(col0, shape), jnp.broadcast_to(col1, shape))


def _ssd_chunk_kernel(*refs, d_inner, nheads, n_xparts):
    xparts = refs[:n_xparts]
    (z_ref, dt_ref, cw_ref, cb_ref, dtb_ref, alog_ref, dvec_ref, nw_ref,
     out_ref, ssm_ref, conv_ref, state_sc, xpad_sc, y_sc) = refs[n_xparts:]
    c = pl.program_id(1)
    q = M2_CHUNK
    hp = M2_HEADDIM
    ns = M2_DSTATE
    heads_per_group = nheads // M2_NGROUPS
    halo = 8

    @pl.when(c == 0)
    def _():
        state_sc[...] = jnp.zeros_like(state_sc)
        xpad_sc[0:halo, :] = jnp.zeros((halo, xpad_sc.shape[1]), F32)

    nrows = z_ref.shape[0]
    wpart = xparts[0].shape[1]
    for i, xr in enumerate(xparts):
        xpad_sc[halo:halo + nrows, i * wpart:(i + 1) * wpart] = xr[...]
    cw = cw_ref[...]
    a = -jnp.exp(alog_ref[...])
    row = lax.broadcasted_iota(jnp.int32, (q, q), 0)
    col = lax.broadcasted_iota(jnp.int32, (q, q), 1)
    causal = row >= col
    tri = jnp.where(causal, 1.0, 0.0).astype(BF16)
    first = col < hp
    first_rows = row < hp

    for r0 in range(0, nrows, q):
        conv = cb_ref[...] + cw[M2_CONV - 1:M2_CONV, :] * xpad_sc[halo + r0:halo + r0 + q, :]
        for k in range(1, M2_CONV):
            conv = conv + cw[M2_CONV - 1 - k:M2_CONV - k, :] * xpad_sc[halo + r0 - k:halo + r0 - k + q, :]
        xc = _silu(conv)

        dt = _softplus(dt_ref[r0:r0 + q, :] + dtb_ref[...])
        da = dt * a
        d_hi, d_mid, d_lo = _split3(da)
        acum = _dot(tri, d_hi) + (_dot(tri, d_mid) + _dot(tri, d_lo))
        acum_t = acum.T
        alast = acum[q - 1:q, :]

        for pr in range(nheads // 2):
            grp = (2 * pr) // heads_per_group
            b_bf = xc[:, d_inner + grp * ns:d_inner + (grp + 1) * ns].astype(BF16)
            c_bf = xc[:, d_inner + (M2_NGROUPS + grp) * ns:d_inner + (M2_NGROUPS + grp + 1) * ns].astype(BF16)
            cb = _dot_nt(c_bf, b_bf)
            xpair = xc[:, pr * 2 * hp:(pr + 1) * 2 * hp]
            h0, h1 = 2 * pr, 2 * pr + 1
            acol = [acum[:, h:h + 1] for h in (h0, h1)]
            m = []
            for k, h in enumerate((h0, h1)):
                seg = jnp.broadcast_to(acol[k], (q, q)) - jnp.broadcast_to(acum_t[h:h + 1, :], (q, q))
                lmat = jnp.exp(jnp.where(causal, seg, -1e30))
                m.append((cb * lmat).astype(BF16))
            dtp = _pair_select(first, dt[:, h0:h0 + 1], dt[:, h1:h1 + 1], (q, q))
            xdt = xpair * dtp
            xdt_bf = xdt.astype(BF16)
            y_diag = jnp.where(first, _dot(m[0], xdt_bf), _dot(m[1], xdt_bf))
            dec_end = _pair_select(first, jnp.exp(alast[:, h0:h0 + 1] - acol[0]),
                                   jnp.exp(alast[:, h1:h1 + 1] - acol[1]), (q, q))
            xw_t = (xdt * dec_end).T.astype(BF16)
            chunk_state = _dot(xw_t, b_bf)
            rows = pl.ds(pr * 2 * hp, 2 * hp)
            prev = state_sc[rows, :]
            y_off = _dot_nt(c_bf, prev.astype(BF16)) * _pair_select(first, jnp.exp(acol[0]), jnp.exp(acol[1]),
                                                                     (q, q))
            sdec = jnp.where(first_rows, jnp.broadcast_to(jnp.exp(alast[:, h0:h0 + 1]), (q, q)),
                             jnp.broadcast_to(jnp.exp(alast[:, h1:h1 + 1]), (q, q)))
            state_sc[rows, :] = prev * sdec + chunk_state
            y_sc[:, pr * 2 * hp:(pr + 1) * 2 * hp] = (y_diag + y_off
                                                       + dvec_ref[:, pr * 2 * hp:(pr + 1) * 2 * hp] * xpair)

        out_ref[r0:r0 + q, :] = _rms(y_sc[...] * _silu(z_ref[r0:r0 + q, :]), nw_ref[...]).astype(out_ref.dtype)

    xpad_sc[0:halo, :] = xpad_sc[nrows:nrows + halo, :]

    @pl.when(c == pl.num_programs(1) - 1)
    def _():
        ssm_ref[...] = state_sc[...]
        conv_ref[...] = xpad_sc[halo + nrows - (M2_CONV - 1):halo + nrows, :]


def ssd_prompt(h, dt_raw, nb, seq, p, *, d_s5, d_inner, nheads):
    q = M2_CHUNK
    rows_step = SSD_CHUNKS_PER_STEP * q if seq % (SSD_CHUNKS_PER_STEP * q) == 0 else q
    nc = seq // rows_step
    conv_dim = d_inner + 2 * M2_NGROUPS * M2_DSTATE
    xw = 512
    xoff = d_s5 + d_inner
    assert d_s5 % d_inner == 0 and xoff % xw == 0 and conv_dim % xw == 0
    assert M2_CHUNK == 2 * M2_HEADDIM == M2_DSTATE == LANES
    zblk = d_s5 // d_inner
    n_xparts = conv_dim // xw
    m = nb * seq
    pad = lambda v: jnp.pad(v, (0, LANES - v.shape[0])).reshape(1, LANES)
    dvec = jnp.repeat(p['m2_d'], M2_HEADDIM).reshape(1, d_inner)
    vec = lambda n: pl.BlockSpec((1, n), lambda b, c: (0, 0))
    tok = lambda w, j: pl.BlockSpec((rows_step, w), lambda b, c, j=j: (b * nc + c, j))
    out, ssm, conv = pl.pallas_call(
        functools.partial(_ssd_chunk_kernel, d_inner=d_inner, nheads=nheads, n_xparts=n_xparts),
        grid=(nb, nc),
        in_specs=[tok(xw, xoff // xw + i) for i in range(n_xparts)] + [tok(d_inner, zblk), tok(LANES, 0),
                  pl.BlockSpec((M2_CONV, conv_dim), lambda b, c: (0, 0)), vec(conv_dim), vec(LANES), vec(LANES),
                  vec(d_inner), vec(d_inner)],
        out_specs=[tok(d_inner, 0),
                   pl.BlockSpec((None, nheads * M2_HEADDIM, M2_DSTATE), lambda b, c: (b, 0, 0)),
                   pl.BlockSpec((None, M2_CONV - 1, conv_dim), lambda b, c: (b, 0, 0))],
        out_shape=[jax.ShapeDtypeStruct((m, d_inner), BF16),
                   jax.ShapeDtypeStruct((nb, nheads * M2_HEADDIM, M2_DSTATE), F32),
                   jax.ShapeDtypeStruct((nb, M2_CONV - 1, conv_dim), F32)],
        scratch_shapes=[pltpu.VMEM((nheads * M2_HEADDIM, M2_DSTATE), F32),
                        pltpu.VMEM((rows_step + 8, conv_dim), F32),
                        pltpu.VMEM((q, d_inner), F32)],
        compiler_params=_cparams(("parallel", "arbitrary")),
        name="ssd_chunk",
    )(*([h] * n_xparts), h, dt_raw, p['m2_conv_w'], p['m2_conv_b'].reshape(1, conv_dim), pad(p['m2_dt_bias']),
      pad(p['m2_a_log']), dvec, p['m2_norm_w'].reshape(1, d_inner))
    return out, ssm.reshape(nb, nheads, M2_HEADDIM, M2_DSTATE), conv


SSD_STEP_SEQS = 8


def _ssd_step_kernel(*refs, d_inner, nheads, n_xparts):
    xparts = refs[:n_xparts]
    (z_ref, dt_ref, cs0_ref, cs1_ref, cs2_ref, cw_ref, cb_ref, dtb_ref, alog_ref, dvec_ref, nw_ref, st_ref,
     out_ref, so_ref, lhs_sc, bfull_sc, ct_sc, yt_sc, xs_sc) = refs[n_xparts:]
    i = pl.program_id(0)
    nb = z_ref.shape[0]
    ns = M2_DSTATE
    rows_g = (nheads // M2_NGROUPS) * M2_HEADDIM

    @pl.when(i == 0)
    def _():
        cw = cw_ref[...]
        xbc = jnp.concatenate([xr[...] for xr in xparts], axis=1)
        conv = (cb_ref[...] + cw[3:4, :] * xbc + cw[2:3, :] * cs2_ref[...]
                + cw[1:2, :] * cs1_ref[...] + cw[0:1, :] * cs0_ref[...])
        xc = _silu(conv)
        dt = _softplus(dt_ref[...] + dtb_ref[...])
        dec = jnp.exp(dt * (-jnp.exp(alog_ref[...])))
        hrow = lax.broadcasted_iota(jnp.int32, (LANES, d_inner), 0)
        hcol = lax.broadcasted_iota(jnp.int32, (LANES, d_inner), 1)
        expand = jnp.where(hcol // M2_HEADDIM == hrow, 1.0, 0.0).astype(BF16)

        def expand_heads(v):
            a, b_, c = _split3(v)
            return _dot(a, expand) + (_dot(b_, expand) + _dot(c, expand))

        xs = xc[:, :d_inner]
        xs_sc[...] = xs
        xdt_t = (xs * expand_heads(dt)).T
        d_hi, d_mid, d_lo = _split3(expand_heads(dec).T)
        for g in range(M2_NGROUPS):
            r = slice(g * rows_g, (g + 1) * rows_g)
            lhs_sc[g] = jnp.concatenate([xdt_t[r].astype(BF16), d_hi[r], d_mid[r], d_lo[r]], axis=1)
            b_g = xc[:, d_inner + g * ns:d_inner + (g + 1) * ns]
            bfull_sc[g] = jnp.concatenate([b_g, jnp.zeros_like(b_g)], axis=1)
            c_g = xc[:, d_inner + (M2_NGROUPS + g) * ns:d_inner + (M2_NGROUPS + g + 1) * ns]
            ct_sc[g] = c_g.T
        yt_sc[...] = jnp.zeros_like(yt_sc)

    row_id = lax.broadcasted_iota(jnp.int32, (nb, 2 * ns), 0)
    lane_id = lax.broadcasted_iota(jnp.int32, (nb, 2 * ns), 1)
    col_id = lax.broadcasted_iota(jnp.int32, (ns, nb), 1)
    for j in range(st_ref.shape[0]):
        b = i * st_ref.shape[0] + j
        r_bot = jnp.where((row_id == b) & (lane_id >= ns), 1.0, 0.0).astype(BF16)
        for g in range(M2_NGROUPS):
            r = pl.ds(g * rows_g, rows_g)
            r_top = jnp.where(row_id == b, bfull_sc[g], 0.0).astype(BF16)
            rhs = jnp.concatenate([r_top, r_bot, r_bot, r_bot], axis=0)
            o = _dot(lhs_sc[g], rhs)
            hnew = st_ref[j, r, :] * o[:, ns:] + o[:, :ns]
            so_ref[j, r, :] = hnew
            cm = jnp.where(col_id == b, ct_sc[g], 0.0).astype(BF16)
            yt_sc[r, :] += _dot(hnew.astype(BF16), cm)

    @pl.when(i == pl.num_programs(0) - 1)
    def _():
        y = yt_sc[...].T + dvec_ref[...] * xs_sc[...]
        out_ref[...] = _rms(y * _silu(z_ref[...]), nw_ref[...]).astype(out_ref.dtype)


def ssd_sample(h, dt_raw, state, conv_state, p, *, d_s5, d_inner, nheads):
    nb = h.shape[0]
    conv_dim = d_inner + 2 * M2_NGROUPS * M2_DSTATE
    xw = 512
    xoff = d_s5 + d_inner
    assert nb == LANES and M2_DSTATE == LANES and M2_CONV == 4
    assert xoff % xw == 0 and conv_dim % xw == 0 and d_s5 % d_inner == 0 and nb % SSD_STEP_SEQS == 0
    n_xparts = conv_dim // xw
    rows = nheads * M2_HEADDIM
    rows_g = rows // M2_NGROUPS
    pad = lambda v: jnp.pad(v, (0, LANES - v.shape[0])).reshape(1, LANES)
    dvec = jnp.repeat(p['m2_d'], M2_HEADDIM).reshape(1, d_inner)
    full = lambda a, b, j=0: pl.BlockSpec((a, b), lambda i, j=j: (0, j))
    st_spec = pl.BlockSpec((SSD_STEP_SEQS, rows, M2_DSTATE), lambda i: (i, 0, 0))
    out, new_state = pl.pallas_call(
        functools.partial(_ssd_step_kernel, d_inner=d_inner, nheads=nheads, n_xparts=n_xparts),
        grid=(nb // SSD_STEP_SEQS,),
        in_specs=[full(nb, xw, xoff // xw + k) for k in range(n_xparts)]
        + [full(nb, d_inner, d_s5 // d_inner), full(nb, LANES)]
        + [full(nb, conv_dim)] * 3
        + [full(M2_CONV, conv_dim), full(1, conv_dim), full(1, LANES), full(1, LANES), full(1, d_inner),
           full(1, d_inner), st_spec],
        out_specs=[full(nb, d_inner), st_spec],
        out_shape=[jax.ShapeDtypeStruct((nb, d_inner), BF16), jax.ShapeDtypeStruct((nb, rows, M2_DSTATE), F32)],
        scratch_shapes=[pltpu.VMEM((M2_NGROUPS, rows_g, 4 * nb), BF16),
                        pltpu.VMEM((M2_NGROUPS, nb, 2 * M2_DSTATE), F32),
                        pltpu.VMEM((M2_NGROUPS, M2_DSTATE, nb), F32),
                        pltpu.VMEM((rows, nb), F32),
                        pltpu.VMEM((nb, d_inner), F32)],
        compiler_params=_cparams(("arbitrary",)),
        name="ssd_step",
    )(*([h] * n_xparts), h, dt_raw, conv_state[:, 0], conv_state[:, 1], conv_state[:, 2],
      p['m2_conv_w'], p['m2_conv_b'].reshape(1, conv_dim), pad(p['m2_dt_bias']), pad(p['m2_a_log']),
      dvec, p['m2_norm_w'].reshape(1, d_inner), state.reshape(nb, rows, M2_DSTATE))
    xbc = lax.slice_in_dim(h, xoff, xoff + conv_dim, axis=1)
    new_conv = jnp.concatenate([conv_state[:, 1:], xbc[:, None, :]], axis=1)
    return out, new_state.reshape(state.shape), new_conv


def _softmax_rows(s):
    e = jnp.exp(s - jnp.max(s, axis=-1, keepdims=True))
    return e / jnp.sum(e, axis=-1, keepdims=True)


def _attn_kernel(q_ref, k_ref, v_ref, o_ref, *, scale):
    s = _dot_nt(q_ref[...].astype(BF16), k_ref[...].astype(BF16)) * scale
    o_ref[...] = _dot(_softmax_rows(s).astype(BF16), v_ref[...].astype(BF16)).astype(o_ref.dtype)


def attention_prompt(q, k, v, nb, seq, mem, heads, *, tq):
    d = q.shape[1]
    hd = d // heads
    nq = seq // tq
    kv_spec = pl.BlockSpec((mem, hd), lambda b, h, i: (b, h))
    q_spec = pl.BlockSpec((tq, hd), lambda b, h, i: (b * nq + i, h))
    return pl.pallas_call(
        functools.partial(_attn_kernel, scale=hd ** -0.5),
        grid=(nb, heads, nq),
        in_specs=[q_spec, kv_spec, kv_spec],
        out_specs=q_spec,
        out_shape=jax.ShapeDtypeStruct(q.shape, BF16),
        compiler_params=_cparams(("parallel", "parallel", "parallel")),
        name="attention_prompt",
    )(q, k, v)


ATTN_STEP_SEQS = 4


ATTN_STEP_ROWS = 64


def _attn_step_kernel(q_ref, k_ref, v_ref, o_ref, s_sc, *, scale):
    nseq, mem, heads, hd = k_ref.shape
    ch = ATTN_STEP_ROWS
    pack = 8 // heads
    for j in range(nseq):
        q = jnp.concatenate([q_ref[j]] * pack, axis=0)

        def score(c, mx):
            rows = pl.ds(pl.multiple_of(c * ch, ch), ch)
            k = k_ref[j, rows].reshape(ch // pack, pack * heads, hd)
            s = jnp.sum(k * q, axis=-1, keepdims=True) * scale
            s_sc[pl.ds(pl.multiple_of(c * (ch // pack), ch // pack), ch // pack)] = s
            return jnp.maximum(mx, jnp.max(s, axis=0))

        mx = lax.fori_loop(0, mem // ch, score, jnp.full((pack * heads, 1), NEG, F32))
        mx1 = mx[0:heads]
        for i in range(1, pack):
            mx1 = jnp.maximum(mx1, mx[i * heads:(i + 1) * heads])
        mx = jnp.concatenate([mx1] * pack, axis=0)

        def accum(c, carry):
            den, acc = carry
            rows = pl.ds(pl.multiple_of(c * ch, ch), ch)
            v = v_ref[j, rows].reshape(ch // pack, pack * heads, hd)
            e = jnp.exp(s_sc[pl.ds(pl.multiple_of(c * (ch // pack), ch // pack), ch // pack)] - mx)
            return den + jnp.sum(e, axis=0), acc + jnp.sum(e * v, axis=0)

        den, acc = lax.fori_loop(0, mem // ch, accum,
                                 (jnp.zeros((pack * heads, 1), F32), jnp.zeros((pack * heads, hd), F32)))
        den1, acc1 = den[0:heads], acc[0:heads]
        for i in range(1, pack):
            den1 = den1 + den[i * heads:(i + 1) * heads]
            acc1 = acc1 + acc[i * heads:(i + 1) * heads]
        o_ref[j] = acc1 / den1


def attention_sample(q, k_cache, v_cache):
    b, mem, heads, hd = k_cache.shape
    nseq = ATTN_STEP_SEQS
    q_spec = pl.BlockSpec((nseq, heads, hd), lambda i: (i, 0, 0))
    kv_spec = pl.BlockSpec((nseq, mem, heads, hd), lambda i: (i, 0, 0, 0))
    out = pl.pallas_call(
        functools.partial(_attn_step_kernel, scale=hd ** -0.5),
        grid=(b // nseq,),
        in_specs=[q_spec, kv_spec, kv_spec],
        out_specs=q_spec,
        out_shape=jax.ShapeDtypeStruct((b, heads, hd), F32),
        scratch_shapes=[pltpu.VMEM((mem * heads // 8, 8, 1), F32)],
        compiler_params=_cparams(("parallel",)),
        name="attention_step",
    )(q.reshape(b, heads, hd), k_cache, v_cache)
    return out.reshape(b, heads * hd)


NEG = -1e30


INFO_G1, INFO_G2, INFO_E1, INFO_E2 = 0, 1, 2, 3


def _lane_pack(lane, values):
    out = 0.0
    for k, v in values:
        out = jnp.where(lane == k, v, out)
    return out


def _router_kernel(x_ref, nw_ref, wr_ref, br_ref, sel_ref, info_ref, *, n_experts):
    xn = _rms(x_ref[...], nw_ref[...])
    x_hi, x_lo = _split2(xn)
    w_hi, w_lo = _split2(wr_ref[...])
    hi_terms = _dot(x_hi, jnp.concatenate([w_hi, w_lo], axis=1))
    logits = hi_terms[:, :LANES] + (hi_terms[:, LANES:] + _dot(x_lo, w_hi)) + br_ref[...]
    lane = lax.broadcasted_iota(jnp.int32, logits.shape, 1)
    big = jnp.int32(2 ** 30)
    is_c = (lane >= n_experts) & (lane < n_experts + MOE_GROUPS)
    lc = jnp.where(is_c, logits, NEG)
    cmax = jnp.max(lc, axis=-1, keepdims=True)
    gsel = jnp.min(jnp.where(lc == cmax, lane, big), axis=-1, keepdims=True) - n_experts
    gate_c = 1.0 / jnp.sum(jnp.where(is_c, jnp.exp(lc - cmax), 0.0), axis=-1, keepdims=True)
    in_group = (lane < n_experts) & (lane // MOE_PER_GROUP == gsel)
    lf = jnp.where(in_group, logits, NEG)
    t1 = jnp.max(lf, axis=-1, keepdims=True)
    i1 = jnp.min(jnp.where(lf == t1, lane, big), axis=-1, keepdims=True)
    lf2 = jnp.where(lane == i1, NEG, lf)
    t2 = jnp.max(lf2, axis=-1, keepdims=True)
    i2 = jnp.min(jnp.where(lf2 == t2, lane, big), axis=-1, keepdims=True)
    r = jnp.exp(t2 - t1)
    g1 = gate_c / (1.0 + r)
    g2 = gate_c * r / (1.0 + r)
    sel_ref[...] = jnp.where((lane == i1) | (lane == i2), 1.0, 0.0)
    info_ref[...] = _lane_pack(lane, ((INFO_G1, g1), (INFO_G2, g2), (INFO_E1, i1.astype(F32)),
                                      (INFO_E2, i2.astype(F32))))


def moe_router(x, norm_w, w_coarse, b_coarse, w_fine, b_fine, *, tm):
    m, d = x.shape
    e = w_fine.shape[1]
    padw = LANES - e - MOE_GROUPS
    wr = jnp.concatenate([w_fine, w_coarse, jnp.zeros((d, padw), F32)], axis=1)
    br = jnp.concatenate([b_fine, b_coarse, jnp.zeros((padw,), F32)]).reshape(1, LANES)
    tok = pl.BlockSpec((tm, LANES), lambda i: (i, 0))
    return pl.pallas_call(
        functools.partial(_router_kernel, n_experts=e),
        grid=(m // tm,),
        in_specs=[pl.BlockSpec((tm, d), lambda i: (i, 0)), pl.BlockSpec((1, d), lambda i: (0, 0)),
                  pl.BlockSpec((d, LANES), lambda i: (0, 0)), pl.BlockSpec((1, LANES), lambda i: (0, 0))],
        out_specs=[tok, tok],
        out_shape=[jax.ShapeDtypeStruct((m, LANES), F32), jax.ShapeDtypeStruct((m, LANES), F32)],
        compiler_params=_cparams(("parallel",)),
        name="moe_router",
    )(x, norm_w.reshape(1, d), wr, br)


def _moe_rank_kernel(sel_ref, info_ref, rank_ref, counts_ref, carry_sc):
    i = pl.program_id(0)

    @pl.when(i == 0)
    def _():
        carry_sc[...] = jnp.zeros_like(carry_sc)

    sel = sel_ref[...]
    tm = sel.shape[0]
    row = lax.broadcasted_iota(jnp.int32, (tm, tm), 0)
    col = lax.broadcasted_iota(jnp.int32, (tm, tm), 1)
    before = jnp.where(row > col, 1.0, 0.0).astype(BF16)
    rank = _dot(before, sel.astype(BF16)) + carry_sc[...]
    info = info_ref[...]
    lane = lax.broadcasted_iota(jnp.int32, sel.shape, 1)
    e1 = info[:, INFO_E1:INFO_E1 + 1].astype(jnp.int32)
    e2 = info[:, INFO_E2:INFO_E2 + 1].astype(jnp.int32)
    r1 = jnp.sum(jnp.where(lane == e1, rank, 0.0), axis=-1, keepdims=True)
    r2 = jnp.sum(jnp.where(lane == e2, rank, 0.0), axis=-1, keepdims=True)
    rank_ref[...] = _lane_pack(lane, ((0, r1), (1, r2)))
    carry_sc[...] += jnp.sum(sel, axis=0, keepdims=True)

    @pl.when(i == pl.num_programs(0) - 1)
    def _():
        counts_ref[...] = carry_sc[...]


def _moe_plan_kernel(rank_ref, info_ref, counts_ref, dest_ref, tiles_ref, *, tile, n_experts):
    counts = counts_ref[...]
    ntile_e = jnp.floor((counts + (tile - 1)) * (1.0 / tile))
    padded = jnp.broadcast_to(ntile_e * tile, (8, LANES))
    r = lax.broadcasted_iota(jnp.int32, (LANES, LANES), 0)
    c = lax.broadcasted_iota(jnp.int32, (LANES, LANES), 1)
    lower = jnp.where(r < c, 1.0, 0.0).astype(BF16)
    p_hi, p_mid, p_lo = _split3(padded)
    offs = (_dot(p_hi, lower) + (_dot(p_mid, lower) + _dot(p_lo, lower)))[0:1, :]
    ends = offs + padded[0:1, :]

    info = info_ref[...]
    rank = rank_ref[...]
    lane = lax.broadcasted_iota(jnp.int32, info.shape, 1)
    e1 = info[:, INFO_E1:INFO_E1 + 1].astype(jnp.int32)
    e2 = info[:, INFO_E2:INFO_E2 + 1].astype(jnp.int32)
    d1 = jnp.sum(jnp.where(lane == e1, offs, 0.0), axis=-1, keepdims=True) + rank[:, 0:1]
    d2 = jnp.sum(jnp.where(lane == e2, offs, 0.0), axis=-1, keepdims=True) + rank[:, 1:2]
    dest_ref[...] = _lane_pack(lane, ((0, d1), (1, d2))).astype(jnp.int32)

    ends_col = jnp.broadcast_to(ends, (LANES, LANES)).T
    start = (c * tile).astype(F32)
    n_before = jnp.sum(jnp.where((ends_col <= start) & (r < n_experts), 1.0, 0.0), axis=0, keepdims=True)
    lane1 = lax.broadcasted_iota(jnp.int32, (1, LANES), 1)
    total = jnp.sum(jnp.where(lane1 == n_experts - 1, ends, 0.0), axis=-1, keepdims=True)
    pad_start = offs + counts
    clear_from = jnp.floor(pad_start * (1.0 / MOE_CLEAR_ROWS)) * MOE_CLEAR_ROWS
    tail = jnp.where(pad_start < ends, clear_from, -1.0)
    tiles_ref[...] = jnp.zeros_like(tiles_ref)
    tiles_ref[0:1, :] = jnp.minimum(n_before, n_experts - 1.0).astype(jnp.int32)
    tiles_ref[1:2, :] = jnp.broadcast_to(total * (1.0 / tile), (1, LANES)).astype(jnp.int32)
    tiles_ref[2:3, :] = tail.astype(jnp.int32)
    tiles_ref[3:4, :] = ends.astype(jnp.int32)


def _moe_dispatch_kernel(dest_ref, tails_ref, ends_ref, ntiles_ref, *rest, tile, n_experts, group_steps,
                         group_offsets):
    TAIL_SEM, FILL_SEM, ROW_SEM = 0, 1, 2
    n_groups = len(group_offsets)
    nw_ref = rest[0]
    x_refs = rest[1:1 + n_groups]
    xs_ref, zero_sc, xn_sc, sems = rest[1 + n_groups:]
    i = pl.program_id(0)
    n_tiles_max = xs_ref.shape[0] // tile

    def fill_copy(j):
        return pltpu.make_async_copy(zero_sc, xs_ref.at[pl.ds(pl.multiple_of(j * tile, tile), tile)],
                                     sems.at[FILL_SEM])

    def tail_pieces(e, act):
        def piece(j, _):
            row = pl.multiple_of(tails_ref[e] + j * MOE_CLEAR_ROWS, MOE_CLEAR_ROWS)
            act(pltpu.make_async_copy(zero_sc.at[pl.ds(0, MOE_CLEAR_ROWS)], xs_ref.at[pl.ds(row, MOE_CLEAR_ROWS)],
                                      sems.at[TAIL_SEM]))
            return 0

        lax.fori_loop(0, (ends_ref[e] - tails_ref[e]) // MOE_CLEAR_ROWS, piece, 0)

    @pl.when(i == 0)
    def _():
        zero_sc[...] = jnp.zeros_like(zero_sc)

        def fill(j, _):
            fill_copy(j).start()
            return 0

        def clear(e, _):
            @pl.when(tails_ref[e] >= 0)
            def _():
                tail_pieces(e, lambda c: c.start())
            return 0

        def clear_wait(e, _):
            @pl.when(tails_ref[e] >= 0)
            def _():
                tail_pieces(e, lambda c: c.wait())
            return 0

        lax.fori_loop(ntiles_ref[0], n_tiles_max, fill, 0)
        lax.fori_loop(0, n_experts, clear, 0)
        lax.fori_loop(0, n_experts, clear_wait, 0)

    slot = i % 2

    def wait_rows(which, tmw):
        for _ in range(2):
            pltpu.make_async_copy(xn_sc.at[which, pl.ds(0, tmw)], xs_ref.at[pl.ds(0, tmw)],
                                  sems.at[ROW_SEM + which]).wait()

    for g, x_ref in enumerate(x_refs):
        @pl.when((i >= group_steps[g]) & (i < group_steps[g + 1]))
        def _(g=g, x_ref=x_ref):
            tm = x_ref.shape[0]
            xn_ref = xn_sc.at[slot, pl.ds(0, tm)]
            xn_ref[...] = _rms(x_ref[...], nw_ref[...])
            base = 2 * (group_offsets[g] + (i - group_steps[g]) * tm)
            for r in range(tm):
                for k in range(2):
                    pltpu.make_async_copy(xn_ref.at[pl.ds(r, 1)], xs_ref.at[pl.ds(dest_ref[base + 2 * r + k], 1)],
                                          sems.at[ROW_SEM + slot]).start(priority=k)

            @pl.when(i > group_steps[g])
            def _():
                wait_rows(1 - slot, tm)

            if g > 0:
                @pl.when(i == group_steps[g])
                def _():
                    wait_rows(1 - slot, x_refs[g - 1].shape[0])

            @pl.when(i == pl.num_programs(0) - 1)
            def _():
                wait_rows(slot, tm)

    @pl.when(i == pl.num_programs(0) - 1)
    def _():
        def fill_wait(j, _):
            fill_copy(j).wait()
            return 0

        lax.fori_loop(ntiles_ref[0], n_tiles_max, fill_wait, 0)


def _moe_expert_kernel(texp_ref, ntiles_ref, xs_ref, wg_hbm, wu_hbm, wd_hbm, ys_ref,
                       wg_sc, wu_sc, wd_sc, slot_sm, sems):
    i = pl.program_id(0)
    n = ntiles_ref[0]

    def weight_copies(e, slot):
        return [pltpu.make_async_copy(w.at[e], sc.at[slot], sems.at[slot])
                for w, sc in ((wg_hbm, wg_sc), (wu_hbm, wu_sc), (wd_hbm, wd_sc))]

    @pl.when(i < n)
    def _():
        e = texp_ref[i]
        first = (i == 0) | (texp_ref[jnp.maximum(i - 1, 0)] != e)

        @pl.when(i == 0)
        def _():
            slot_sm[0] = 0
            for c in weight_copies(e, 0):
                c.start()

        @pl.when(first & (i > 0))
        def _():
            slot_sm[0] = 1 - slot_sm[0]

        slot = slot_sm[0]

        @pl.when(first)
        def _():
            for c in weight_copies(e, slot):
                c.wait()
            j = lax.while_loop(lambda j: (j < n) & (texp_ref[jnp.minimum(j, n - 1)] == e), lambda j: j + 1, i + 1)

            @pl.when(j < n)
            def _():
                for c in weight_copies(texp_ref[jnp.minimum(j, n - 1)], 1 - slot):
                    c.start()

        x = xs_ref[...].astype(BF16)
        hg = _dot(x, wg_sc[slot].astype(BF16))
        hu = _dot(x, wu_sc[slot].astype(BF16))
        ys_ref[...] = _dot((_silu(hg) * hu).astype(BF16), wd_sc[slot].astype(BF16))


def _moe_combine_kernel(dest_ref, ys_ref, x_ref, info_ref, fw_ref, y_ref, buf_sc, sems):
    s = pl.program_id(0)
    n_blocks = pl.num_programs(0) - 1
    tm = x_ref.shape[0]
    slot = s % 2

    @pl.when(s < n_blocks)
    def _():
        for r in range(tm):
            for k in range(2):
                pltpu.make_async_copy(ys_ref.at[pl.ds(dest_ref[2 * (s * tm + r) + k], 1)],
                                      buf_sc.at[slot, k, pl.ds(r, 1)], sems.at[slot]).start(priority=k)

    @pl.when(s > 0)
    def _():
        prev = 1 - slot
        for k in range(2):
            pltpu.make_async_copy(ys_ref.at[pl.ds(0, tm)], buf_sc.at[prev, k], sems.at[prev]).wait()
        info = info_ref[...]
        y = (x_ref[...] + info[:, INFO_G1:INFO_G1 + 1] * buf_sc[prev, 0]
             + info[:, INFO_G2:INFO_G2 + 1] * buf_sc[prev, 1])
        y_ref[...] = _rms(y, fw_ref[...])


MOE_TILE = 256
MOE_CLEAR_ROWS = 32
MOE_TOKENS_PER_STEP = 256


def _largest_tile(m, cap):
    return max(t for t in range(8, cap + 1, 8) if m % t == 0)


def moe_routed_final(x_list, p, final_w):
    d = x_list[0].shape[1]
    ne, _, f = p['moe_w_gate'].shape
    sizes = [x.shape[0] for x in x_list]
    m = sum(sizes)
    tile = min(MOE_TILE, m)
    routed = [moe_router(x, p['norm_ffn_w'], p['router_coarse_w'], p['router_coarse_b'],
                         p['router_fine_w'], p['router_fine_b'], tm=min(x.shape[0], 512)) for x in x_list]
    sel = jnp.concatenate([r[0] for r in routed], axis=0)
    info = jnp.concatenate([r[1] for r in routed], axis=0)
    tm = _largest_tile(m, 1024)
    tok = pl.BlockSpec((tm, LANES), lambda i: (i, 0))
    rank, counts = pl.pallas_call(
        _moe_rank_kernel,
        grid=(m // tm,),
        in_specs=[tok, tok],
        out_specs=[tok, pl.BlockSpec((1, LANES), lambda i: (0, 0))],
        out_shape=[jax.ShapeDtypeStruct((m, LANES), F32), jax.ShapeDtypeStruct((1, LANES), F32)],
        scratch_shapes=[pltpu.VMEM((1, LANES), F32)],
        compiler_params=_cparams(("arbitrary",)),
        name="moe_rank",
    )(sel, info)
    n_tiles_max = (2 * m) // tile + ne
    assert n_tiles_max <= LANES
    dest, tiles = pl.pallas_call(
        functools.partial(_moe_plan_kernel, tile=tile, n_experts=ne),
        out_shape=[jax.ShapeDtypeStruct((m, LANES), jnp.int32), jax.ShapeDtypeStruct((8, LANES), jnp.int32)],
        compiler_params=pltpu.CompilerParams(vmem_limit_bytes=VMEM_LIMIT),
        name="moe_plan",
    )(rank, info, counts)
    dest_flat = dest[:, :2].reshape(2 * m)
    tile_expert = tiles[0, :n_tiles_max]
    n_tiles = tiles[1, :1]
    tails = tiles[2, :ne]
    seg_ends = tiles[3, :ne]

    rows = n_tiles_max * tile

    tds = [min(mg, 2 * MOE_TOKENS_PER_STEP) for mg in sizes]
    group_steps = [0]
    for mg, td in zip(sizes, tds):
        group_steps.append(group_steps[-1] + mg // td)
    group_offsets = [sum(sizes[:g]) for g in range(len(sizes))]

    def group_spec(g):
        first, last = group_steps[g], group_steps[g + 1] - 1
        return pl.BlockSpec((tds[g], d), lambda i, *_: (jnp.clip(i, first, last) - first, 0))

    xs = pl.pallas_call(
        functools.partial(_moe_dispatch_kernel, tile=tile, n_experts=ne, group_steps=tuple(group_steps),
                          group_offsets=tuple(group_offsets)),
        grid_spec=pltpu.PrefetchScalarGridSpec(
            num_scalar_prefetch=4, grid=(group_steps[-1],),
            in_specs=[pl.BlockSpec((1, d), lambda i, *_: (0, 0))] + [group_spec(g) for g in range(len(sizes))],
            out_specs=pl.BlockSpec(memory_space=pl.ANY),
            scratch_shapes=[pltpu.VMEM((tile, d), F32), pltpu.VMEM((2, max(tds), d), F32),
                            pltpu.SemaphoreType.DMA((4,))]),
        out_shape=jax.ShapeDtypeStruct((rows, d), F32),
        compiler_params=_cparams(("arbitrary",)),
        name="moe_dispatch",
    )(dest_flat, tails, seg_ends, n_tiles, p['norm_ffn_w'].reshape(1, d), *x_list)

    def tile_idx(i, te, nt):
        return jnp.minimum(i, nt[0] - 1)

    ys = pl.pallas_call(
        _moe_expert_kernel,
        grid_spec=pltpu.PrefetchScalarGridSpec(
            num_scalar_prefetch=2, grid=(n_tiles_max,),
            in_specs=[pl.BlockSpec((tile, d), lambda i, te, nt: (tile_idx(i, te, nt), 0)),
                      pl.BlockSpec(memory_space=pl.ANY), pl.BlockSpec(memory_space=pl.ANY),
                      pl.BlockSpec(memory_space=pl.ANY)],
            out_specs=pl.BlockSpec((tile, d), lambda i, te, nt: (tile_idx(i, te, nt), 0)),
            scratch_shapes=[pltpu.VMEM((2, d, f), F32), pltpu.VMEM((2, d, f), F32), pltpu.VMEM((2, f, d), F32),
                            pltpu.SMEM((1,), jnp.int32), pltpu.SemaphoreType.DMA((2,))]),
        out_shape=jax.ShapeDtypeStruct((rows, d), F32),
        input_output_aliases={2: 0},
        compiler_params=_cparams(("arbitrary",)),
        name="moe_experts",
    )(tile_expert, n_tiles, xs, p['moe_w_gate'], p['moe_w_up'], p['moe_w_down'])

    outs = []
    off = 0
    for x, (_, info_g), mg in zip(x_list, routed, sizes):
        tc = min(mg, MOE_TOKENS_PER_STEP)
        outs.append(pl.pallas_call(
            _moe_combine_kernel,
            grid_spec=pltpu.PrefetchScalarGridSpec(
                num_scalar_prefetch=1, grid=(mg // tc + 1,),
                in_specs=[pl.BlockSpec(memory_space=pl.ANY),
                          pl.BlockSpec((tc, d), lambda s, dref: (jnp.maximum(s - 1, 0), 0)),
                          pl.BlockSpec((tc, LANES), lambda s, dref: (jnp.maximum(s - 1, 0), 0)),
                          pl.BlockSpec((1, d), lambda s, dref: (0, 0))],
                out_specs=pl.BlockSpec((tc, d), lambda s, dref: (jnp.maximum(s - 1, 0), 0)),
                scratch_shapes=[pltpu.VMEM((2, 2, tc, d), F32), pltpu.SemaphoreType.DMA((2,))]),
            out_shape=jax.ShapeDtypeStruct((mg, d), F32),
            compiler_params=_cparams(("arbitrary",)),
            name="moe_combine",
        )(dest_flat[2 * off:2 * (off + mg)], ys, x, info_g, final_w.reshape(1, d)))
        off += mg
    return outs


def s5_prompt(h, nb, seq, prep, p):
    y, hfin = s5_scan(h, prep, nb, seq)
    out = s5_head(y, h, p['s5_d'], p['s5_glu_w'], p['s5_glu_b'], p['s5_norm_w'], tm=min(1024, nb * seq))
    return out, hfin[:, 0].transpose(1, 0, 2), hfin[:, 1].transpose(1, 0, 2)


def s5_sample(h, st_re, st_im, prep, p):
    y, n_re, n_im = s5_step(h, st_re, st_im, prep, p['s5_c_re'], p['s5_c_im'])
    out = s5_head(y, h, p['s5_d'], p['s5_glu_w'], p['s5_glu_b'], p['s5_norm_w'], tm=h.shape[0])
    return out, n_re, n_im


def _row_tile(m):
    return min(m, 1024)


def _col_tile(m, n):
    if m <= 256:
        return n
    return max(t for t in range(256, 1793, 256) if n % t == 0)


def _mixer_and_attention(x, p, s5_prep, *, nb, seq, mem_kv, xa_heads, states):
    m, d = x.shape
    g, n = p['s5_a_re'].shape
    d_s5 = g * S5_CH
    nheads = p['m2_a_log'].shape[0]
    d_inner = nheads * M2_HEADDIM
    conv_dim = d_inner + 2 * M2_NGROUPS * M2_DSTATE
    n_main = d_s5 + d_inner + conv_dim
    tm = _row_tile(m)

    w_in = p['w_in']
    w_dt = jnp.pad(w_in[:, n_main:], ((0, 0), (0, LANES - nheads)))
    tm2, tn2 = tm, _col_tile(m, d)
    h, dt_raw = fused_matmul([x], w_in, n_out=n_main, gain=p['norm_mix_w'], side_w=w_dt, tm=tm,
                             tn=_col_tile(m, n_main))

    if states is None:
        s5_out, s5_re, s5_im = s5_prompt(h, nb, seq, s5_prep, p)
        m2_out, ssm, conv = ssd_prompt(h, dt_raw, nb, seq, p, d_s5=d_s5, d_inner=d_inner, nheads=nheads)
    else:
        s5_out, s5_re, s5_im = s5_sample(h, states[0], states[1], s5_prep, p)
        m2_out, ssm, conv = ssd_sample(h, dt_raw, states[2], states[3], p, d_s5=d_s5, d_inner=d_inner,
                                       nheads=nheads)
    x1 = fused_matmul([s5_out, m2_out], p['w_out'], n_out=d, res=x, tm=tm2, tn=tn2)

    q = fused_matmul([x1], p['xa_wq'], n_out=d, gain=p['norm_xa_w'], tm=tm2, tn=tn2,
                     out_dtype=BF16 if states is None else F32)
    if states is None:
        mem = mem_kv[0].shape[0] // nb
        o = attention_prompt(q, mem_kv[0], mem_kv[1], nb, seq, mem, xa_heads, tq=min(seq, 2048))
    else:
        o = attention_sample(q, mem_kv[0], mem_kv[1])
    x2 = fused_matmul([o], p['xa_wo'], n_out=d, res=x1, tm=tm2, tn=tn2)

    return x2, s5_re, s5_im, ssm, conv


def kernel(x_prompt, x_sample, mem_prompt, state_s5_re, state_s5_im, state_ssm, state_conv, cache_mem_k, cache_mem_v, norm_mix_w, w_in, s5_a_re, s5_a_im, s5_log_dt, s5_b_re, s5_b_im, s5_c_re, s5_c_im, s5_d, s5_glu_w, s5_glu_b, s5_norm_w, m2_conv_w, m2_conv_b, m2_dt_bias, m2_a_log, m2_d, m2_norm_w, w_out, norm_xa_w, norm_mem_w, xa_wq, xa_wk, xa_wv, xa_wo, norm_ffn_w, router_coarse_w, router_coarse_b, router_fine_w, router_fine_b, moe_w_gate, moe_w_up, moe_w_down, norm_final_w):
    depth = w_in.shape[0]
    assert depth == 1, "the final norm is fused into the (only) layer"
    per_layer = dict(
        norm_mix_w=norm_mix_w, w_in=w_in, s5_a_re=s5_a_re, s5_a_im=s5_a_im, s5_log_dt=s5_log_dt,
        s5_b_re=s5_b_re, s5_b_im=s5_b_im, s5_c_re=s5_c_re, s5_c_im=s5_c_im, s5_d=s5_d, s5_glu_w=s5_glu_w,
        s5_glu_b=s5_glu_b, s5_norm_w=s5_norm_w, m2_conv_w=m2_conv_w, m2_conv_b=m2_conv_b, m2_dt_bias=m2_dt_bias,
        m2_a_log=m2_a_log, m2_d=m2_d, m2_norm_w=m2_norm_w, w_out=w_out, norm_xa_w=norm_xa_w,
        norm_mem_w=norm_mem_w, xa_wq=xa_wq, xa_wk=xa_wk, xa_wv=xa_wv, xa_wo=xa_wo, norm_ffn_w=norm_ffn_w,
        router_coarse_w=router_coarse_w, router_coarse_b=router_coarse_b, router_fine_w=router_fine_w,
        router_fine_b=router_fine_b, moe_w_gate=moe_w_gate, moe_w_up=moe_w_up, moe_w_down=moe_w_down)
    p = {k: v[0] for k, v in per_layer.items()}
    for name in ('w_in', 'w_out', 'xa_wq', 'xa_wo'):
        p[name] = p[name].astype(BF16)
    nb, seq, d = x_prompt.shape
    db, dseq, _ = x_sample.shape
    assert dseq == 1
    mem = mem_prompt.shape[1]
    xa_heads = cache_mem_k.shape[3]

    s5_prep = s5_prepare(p['s5_a_re'], p['s5_a_im'], p['s5_log_dt'], p['s5_b_re'], p['s5_b_im'],
                         p['s5_c_re'], p['s5_c_im'])

    memx = mem_prompt.reshape(nb * mem, d)
    mk = fused_matmul([memx], p['xa_wk'], n_out=d, gain=p['norm_mem_w'], tm=_row_tile(nb * mem),
                      tn=_col_tile(nb * mem, d))
    mv = fused_matmul([memx], p['xa_wv'], n_out=d, gain=p['norm_mem_w'], tm=_row_tile(nb * mem),
                      tn=_col_tile(nb * mem, d))
    xp, p_re, p_im, p_ssm, p_conv = _mixer_and_attention(
        x_prompt.reshape(nb * seq, d), p, s5_prep, nb=nb, seq=seq, mem_kv=(mk, mv), xa_heads=xa_heads, states=None)

    xs, s_re, s_im, s_ssm, s_conv = _mixer_and_attention(
        x_sample.reshape(db, d), p, s5_prep, nb=db, seq=1,
        mem_kv=(cache_mem_k[0], cache_mem_v[0]), xa_heads=xa_heads,
        states=(state_s5_re[0], state_s5_im[0], state_ssm[0], state_conv[0]))

    yp, ys = moe_routed_final([xp, xs], p, norm_final_w)

    kv_shape = (1, nb, mem) + cache_mem_k.shape[3:]
    return (yp.reshape(nb, seq, d), ys.reshape(db, 1, d), p_re[None], p_im[None], p_ssm[None], p_conv[None],
            mk.reshape(kv_shape), mv.reshape(kv_shape), s_re[None], s_im[None], s_ssm[None], s_conv[None])
```

```python
import functools
import math

import jax
import jax.numpy as jnp
from jax import lax
from jax.experimental import pallas as pl
from jax.experimental.pallas import tpu as pltpu

F32 = jnp.float32
BF16 = jnp.bfloat16
RMS_EPS = 1e-6

V7X_VMEM_BYTES = 64 * 1024 * 1024
VMEM_LIMIT = V7X_VMEM_BYTES - 8 * 1024 * 1024
LANES = 128

S5_CH = 16
S5_N = 64
S5_Q = 16
S5_GB = 8
S5_SEQ_PAD = 8
M2_HEADDIM = 64
M2_DSTATE = 128
M2_NGROUPS = 2
M2_CONV = 4
M2_CHUNK = 128
MOE_GROUPS = 4
MOE_PER_GROUP = 8


def _cparams(sem):
    return pltpu.CompilerParams(dimension_semantics=sem, vmem_limit_bytes=VMEM_LIMIT)


def _rms(x, w):
    return x * lax.rsqrt(jnp.mean(x * x, axis=-1, keepdims=True) + RMS_EPS) * w


def _sigmoid(x):
    return 1.0 / (1.0 + jnp.exp(-x))


def _silu(x):
    return x * _sigmoid(x)


def _softplus(x):
    return jnp.maximum(x, 0.0) + jnp.log1p(jnp.exp(-jnp.abs(x)))


def _gelu_tanh(x):
    return 0.5 * x * (1.0 + jnp.tanh(math.sqrt(2.0 / math.pi) * (x + 0.044715 * (x * x * x))))


def _dot(a, b):
    return jnp.dot(a, b, preferred_element_type=F32)


def _dot_nt(a, b):
    return lax.dot_general(a, b, (((1,), (1,)), ((), ())), preferred_element_type=F32)


def _split3(x):
    hi = x.astype(BF16)
    r = x - hi.astype(F32)
    mid = r.astype(BF16)
    lo = (r - mid.astype(F32)).astype(BF16)
    return hi, mid, lo


def _split2(x):
    hi = x.astype(BF16)
    lo = (x - hi.astype(F32)).astype(BF16)
    return hi, lo


def _mm_kernel(*refs, n_lhs, has_gain, has_res, has_side, staged):
    it = iter(refs)
    lhs = [next(it) for _ in range(n_lhs)]
    gain = next(it) if has_gain else None
    ws = [next(it) for _ in range(n_lhs)]
    side_w = next(it) if has_side else None
    res = next(it) if has_res else None
    out = next(it)
    side_out = next(it) if has_side else None
    lhs_bf = next(it) if staged else lhs

    if staged:
        @pl.when(pl.program_id(1) == 0)
        def _():
            for i in range(n_lhs):
                x = lhs[i][...]
                if has_gain:
                    x = _rms(x, gain[...])
                lhs_bf[i] = x.astype(BF16)
            if has_side:
                side_out[...] = _dot(lhs_bf[0], side_w[...].astype(BF16))

    acc = None
    for i in range(n_lhs):
        p = _dot(lhs_bf[i][...], ws[i][...].astype(BF16))
        acc = p if acc is None else acc + p
    if has_res:
        acc = acc + res[...]
    out[...] = acc.astype(out.dtype)


def fused_matmul(lhs_list, w, *, n_out, gain=None, res=None, side_w=None, out_dtype=F32, tm, tn):
    n_lhs = len(lhs_list)
    m, kp = lhs_list[0].shape
    assert all(a.shape == (m, kp) for a in lhs_list)
    assert w.shape[0] == n_lhs * kp and m % tm == 0 and n_out % tn == 0
    assert gain is None or n_lhs == 1
    staged = gain is not None or any(a.dtype != BF16 for a in lhs_list)
    assert staged or side_w is None
    grid = (m // tm, n_out // tn)
    in_specs = [pl.BlockSpec((tm, kp), lambda i, j: (i, 0)) for _ in range(n_lhs)]
    args = list(lhs_list)
    if gain is not None:
        in_specs.append(pl.BlockSpec((1, kp), lambda i, j: (0, 0)))
        args.append(gain.reshape(1, kp))
    for p in range(n_lhs):
        in_specs.append(pl.BlockSpec((kp, tn), lambda i, j, p=p: (p, j)))
        args.append(w)
    if side_w is not None:
        in_specs.append(pl.BlockSpec((kp, LANES), lambda i, j: (0, 0)))
        args.append(side_w)
    if res is not None:
        in_specs.append(pl.BlockSpec((tm, tn), lambda i, j: (i, j)))
        args.append(res)
    out_shape = [jax.ShapeDtypeStruct((m, n_out), out_dtype)]
    out_specs = [pl.BlockSpec((tm, tn), lambda i, j: (i, j))]
    if side_w is not None:
        out_shape.append(jax.ShapeDtypeStruct((m, LANES), F32))
        out_specs.append(pl.BlockSpec((tm, LANES), lambda i, j: (i, 0)))
    outs = pl.pallas_call(
        functools.partial(_mm_kernel, n_lhs=n_lhs, has_gain=gain is not None,
                          has_res=res is not None, has_side=side_w is not None, staged=staged),
        grid=grid, in_specs=in_specs, out_specs=out_specs, out_shape=out_shape,
        scratch_shapes=[pltpu.VMEM((n_lhs, tm, kp), BF16)] if staged else [],
        compiler_params=_cparams(("parallel", "arbitrary")),
        name="fused_matmul",
    )(*args)
    return outs if side_w is not None else outs[0]


def _s5_prep_kernel(*refs):
    for g in range(refs[0].shape[0]):
        _s5_prep_group(*[r.at[g] for r in refs])


def _s5_prep_group(lre_ref, lim_ref, ldt_ref, btre_ref, btim_ref, cre_ref, cim_ref,
                   tz_ref, wsre_ref, wsim_ref, wcre_ref, wcim_ref, aq_ref, ab_ref, bbt_ref):
    q, ch = S5_Q, S5_CH
    lr = lre_ref[...]
    li = lim_ref[...]
    step = jnp.exp(ldt_ref[...])
    mag = jnp.exp(lr * step)
    ab_re = mag * jnp.cos(li * step)
    ab_im = mag * jnp.sin(li * step)
    den = lr * lr + li * li
    num_re = ab_re - 1.0
    coef_re = (num_re * lr + ab_im * li) / den
    coef_im = (ab_im * lr - num_re * li) / den
    bt_re = btre_ref[...]
    bt_im = btim_ref[...]
    bb_re = coef_re * bt_re - coef_im * bt_im
    bb_im = coef_re * bt_im + coef_im * bt_re
    c_re = cre_ref[...]
    c_im = cim_ref[...]

    pw = [(jnp.ones_like(ab_re), jnp.zeros_like(ab_re))]
    for _ in range(q):
        pr, pi = pw[-1]
        pw.append((pr * ab_re - pi * ab_im, pr * ab_im + pi * ab_re))

    ca_re = [c_re * pr - c_im * pi for pr, pi in pw]
    ca_im = [c_re * pi + c_im * pr for pr, pi in pw]
    wcre_ref[...] = jnp.concatenate(ca_re[1:], axis=0).astype(BF16)
    wcim_ref[...] = jnp.concatenate([-x for x in ca_im[1:]], axis=0).astype(BF16)

    pr_stack = jnp.concatenate(ca_re[:q], axis=0)
    pi_stack = jnp.concatenate(ca_im[:q], axis=0)
    krow = None
    for a, b, sign in ((bb_re, pr_stack, 1.0), (bb_im, pi_stack, -1.0)):
        a_hi, a_lo = _split2(a)
        b_hi, b_lo = _split2(b)
        t = _dot_nt(a_hi, b_hi) + (_dot_nt(a_hi, b_lo) + _dot_nt(a_lo, b_hi))
        krow = sign * t if krow is None else krow + sign * t
    lane = lax.broadcasted_iota(jnp.int32, krow.shape, 1)
    blocks = [krow]
    for s in range(1, q):
        blocks.append(jnp.where(lane >= s * ch, pltpu.roll(krow, s * ch, 1), 0.0))
    tz_ref[...] = jnp.concatenate(blocks, axis=0).astype(BF16)

    ws_re, ws_im = [], []
    for s in range(q):
        pr, pi = pw[q - 1 - s]
        ws_re.append(bb_re * pr - bb_im * pi)
        ws_im.append(bb_re * pi + bb_im * pr)
    wsre_ref[...] = jnp.concatenate(ws_re, axis=0).astype(BF16)
    wsim_ref[...] = jnp.concatenate(ws_im, axis=0).astype(BF16)

    aq_ref[0:1, :] = pw[q][0]
    aq_ref[1:2, :] = pw[q][1]
    ab_ref[0:1, :] = ab_re
    ab_ref[1:2, :] = ab_im
    bbt_ref[0:ch, :] = bb_re
    bbt_ref[ch:2 * ch, :] = bb_im


def s5_prepare(a_re, a_im, log_dt, b_re, b_im, c_re, c_im):
    g, n = a_re.shape
    ch, q = S5_CH, S5_Q
    qc = q * ch
    bt_re = jnp.swapaxes(b_re, 1, 2)
    bt_im = jnp.swapaxes(b_im, 1, 2)

    def per_g(*dims):
        return pl.BlockSpec((S5_GB,) + dims, lambda i: (i,) + (0,) * len(dims))

    return pl.pallas_call(
        _s5_prep_kernel,
        grid=(g // S5_GB,),
        in_specs=[per_g(1, n), per_g(1, n), per_g(1, 1), per_g(ch, n), per_g(ch, n), per_g(ch, n), per_g(ch, n)],
        out_specs=[per_g(qc, qc), per_g(qc, n), per_g(qc, n), per_g(qc, n), per_g(qc, n),
                   per_g(2, n), per_g(2, n), per_g(2 * ch, n)],
        out_shape=[jax.ShapeDtypeStruct((g, qc, qc), BF16),
                   jax.ShapeDtypeStruct((g, qc, n), BF16), jax.ShapeDtypeStruct((g, qc, n), BF16),
                   jax.ShapeDtypeStruct((g, qc, n), BF16), jax.ShapeDtypeStruct((g, qc, n), BF16),
                   jax.ShapeDtypeStruct((g, 2, n), F32), jax.ShapeDtypeStruct((g, 2, n), F32),
                   jax.ShapeDtypeStruct((g, 2 * ch, n), F32)],
        compiler_params=_cparams(("parallel",)),
        name="s5_prepare",
    )(a_re.reshape(g, 1, n), a_im.reshape(g, 1, n), log_dt.reshape(g, 1, 1), bt_re, bt_im, c_re, c_im)


def _s5_scan_kernel(h_ref, tz_ref, wsre_ref, wsim_ref, wcre_ref, wcim_ref, aq_ref,
                    y_ref, hfin_ref, xs_sc, u_sc, yg_sc, sre_sc, sim_sc, *, nb, nchunk):
    gb, q, ch = S5_GB, S5_Q, S5_CH
    rows = nb * nchunk
    seq_stride = nchunk + S5_SEQ_PAD
    per_vreg = LANES // ch
    assert gb == per_vreg and q % per_vreg == 0
    slot = lax.broadcasted_iota(jnp.int32, (rows, LANES), 1) // ch

    halves = q // per_vreg

    def rot_rows(w, g):
        if g == 0:
            return w
        cut = (per_vreg - g) * ch
        parts = []
        for hf in range(halves):
            blk = w[hf * LANES:(hf + 1) * LANES]
            parts += [blk[cut:], blk[:cut]]
        return jnp.concatenate(parts, axis=0)

    for s in range(q):
        x = h_ref[pl.ds(s, rows, stride=q), :].astype(BF16)
        k = s % per_vreg
        xs_sc[s] = pltpu.roll(x, k * ch, 1) if k else x
    keep = [jnp.where(slot == j, 1.0, 0.0).astype(BF16) for j in range(per_vreg)]
    for g in range(gb):
        for hf in range(halves):
            acc = None
            for k in range(per_vreg):
                piece = xs_sc[hf * per_vreg + k] * keep[(g + k) % per_vreg]
                acc = piece if acc is None else acc + piece
            u_sc[g, :, hf * LANES:(hf + 1) * LANES] = acc

    for g in range(gb):
        u = u_sc[g]
        tz = rot_rows(tz_ref[g], g)
        tz = jnp.concatenate([pltpu.roll(tz[:, hf * LANES:(hf + 1) * LANES], g * ch, 1) if g
                              else tz[:, hf * LANES:(hf + 1) * LANES] for hf in range(halves)], axis=1)
        yg_sc[g] = _dot(u, tz)
        for sc, w_ref in ((sre_sc, wsre_ref), (sim_sc, wsim_ref)):
            s_all = _dot(u, rot_rows(w_ref[g], g))
            for b in range(nb):
                sc[g, b * seq_stride:b * seq_stride + nchunk, :] = s_all[b * nchunk:(b + 1) * nchunk]

    ar = [jnp.broadcast_to(aq_ref[g, 0:1, :], (nb, S5_N)) for g in range(gb)]
    ai = [jnp.broadcast_to(aq_ref[g, 1:2, :], (nb, S5_N)) for g in range(gb)]

    def step(c, carry):
        at = pl.ds(c, nb, stride=seq_stride)
        new = []
        for g in range(gb):
            hr, hi = carry[g]
            sr = sre_sc[g, at, :]
            si = sim_sc[g, at, :]
            sre_sc[g, at, :] = hr
            sim_sc[g, at, :] = hi
            new.append((ar[g] * hr - ai[g] * hi + sr, ar[g] * hi + ai[g] * hr + si))
        return tuple(new)

    zero = jnp.zeros((nb, S5_N), F32)
    fin = lax.fori_loop(0, nchunk, step, tuple((zero, zero) for _ in range(gb)), unroll=4)
    for g in range(gb):
        hfin_ref[g, 0] = fin[g][0]
        hfin_ref[g, 1] = fin[g][1]
        h_in = [jnp.concatenate([sc[g, b * seq_stride:b * seq_stride + nchunk, :] for b in range(nb)], axis=0)
                for sc in (sre_sc, sim_sc)]
        yg_sc[g] += (_dot_nt(h_in[0].astype(BF16), rot_rows(wcre_ref[g], g))
                     + _dot_nt(h_in[1].astype(BF16), rot_rows(wcim_ref[g], g)))

    for t in range(q):
        hf, k = divmod(t, per_vreg)
        acc = jnp.zeros((rows, LANES), F32)
        for g in range(gb):
            acc = jnp.where(slot == (k + g) % per_vreg, yg_sc[g, :, hf * LANES:(hf + 1) * LANES], acc)
        y_ref[pl.ds(t, rows, stride=q), :] = pltpu.roll(acc, (per_vreg - k) * ch, 1) if k else acc


def s5_scan(h, prep, nb, seq):
    tz, ws_re, ws_im, wc_re, wc_im, aq = prep[:6]
    g, qc, _ = tz.shape
    n = S5_N
    gb = S5_GB
    q = S5_Q
    assert gb * S5_CH == LANES and qc == q * S5_CH and seq % q == 0
    nchunk = seq // q
    rows = nb * nchunk
    m = nb * seq

    def blk(*dims):
        return pl.BlockSpec((gb,) + dims, lambda i: (i,) + (0,) * len(dims))

    return pl.pallas_call(
        functools.partial(_s5_scan_kernel, nb=nb, nchunk=nchunk),
        grid=(g // gb,),
        in_specs=[pl.BlockSpec((m, LANES), lambda i: (0, i)),
                  blk(qc, qc), blk(qc, n), blk(qc, n), blk(qc, n), blk(qc, n), blk(2, n)],
        out_specs=[pl.BlockSpec((m, LANES), lambda i: (0, i)), blk(2, nb, n)],
        out_shape=[jax.ShapeDtypeStruct((m, g * S5_CH), F32), jax.ShapeDtypeStruct((g, 2, nb, n), F32)],
        scratch_shapes=[pltpu.VMEM((q, rows, LANES), BF16), pltpu.VMEM((gb, rows, qc), BF16),
                        pltpu.VMEM((gb, rows, qc), F32),
                        pltpu.VMEM((gb, nb * (nchunk + S5_SEQ_PAD), n), F32),
                        pltpu.VMEM((gb, nb * (nchunk + S5_SEQ_PAD), n), F32)],
        compiler_params=_cparams(("parallel",)),
        name="s5_scan",
    )(h, tz, ws_re, ws_im, wc_re, wc_im, aq)


def _s5_step_kernel(u_ref, hre_ref, him_ref, ab_ref, bbt_ref, cre_ref, cim_ref, y_ref, ore_ref, oim_ref):
    gb = hre_ref.shape[1]
    ch = S5_CH
    for g in range(gb):
        u = u_ref[:, g * ch:(g + 1) * ch].astype(BF16)
        bb_re = bbt_ref[g, 0:ch, :].astype(BF16)
        bb_im = bbt_ref[g, ch:2 * ch, :].astype(BF16)
        ar = ab_ref[g, 0:1, :]
        ai = ab_ref[g, 1:2, :]
        hr0 = hre_ref[:, g, :]
        hi0 = him_ref[:, g, :]
        hr = _dot(u, bb_re) + (ar * hr0 - ai * hi0)
        hi = _dot(u, bb_im) + (ar * hi0 + ai * hr0)
        ore_ref[:, g, :] = hr
        oim_ref[:, g, :] = hi
        y_ref[:, g * ch:(g + 1) * ch] = (_dot_nt(hr.astype(BF16), cre_ref[g].astype(BF16))
                                         - _dot_nt(hi.astype(BF16), cim_ref[g].astype(BF16)))


def s5_step(h, h_re, h_im, prep, c_re, c_im):
    ab, bbt = prep[6], prep[7]
    b, g, n = h_re.shape
    ch = S5_CH
    gb = S5_GB
    assert gb * ch == LANES

    def blk(*dims):
        return pl.BlockSpec((gb,) + dims, lambda i: (i,) + (0,) * len(dims))

    tok = pl.BlockSpec((b, LANES), lambda i: (0, i))
    st = pl.BlockSpec((b, gb, n), lambda i: (0, i, 0))
    return pl.pallas_call(
        _s5_step_kernel,
        grid=(g // gb,),
        in_specs=[tok, st, st, blk(2, n), blk(2 * ch, n), blk(ch, n), blk(ch, n)],
        out_specs=[tok, st, st],
        out_shape=[jax.ShapeDtypeStruct((b, g * ch), F32), jax.ShapeDtypeStruct((b, g, n), F32),
                   jax.ShapeDtypeStruct((b, g, n), F32)],
        compiler_params=_cparams(("parallel",)),
        name="s5_step",
    )(h, h_re, h_im, ab, bbt, c_re, c_im)


def _s5_head_kernel(y_ref, u_ref, d_ref, w_ref, b_ref, nw_ref, o_ref):
    y = y_ref[...] + d_ref[...] * u_ref[...]
    g = _gelu_tanh(y)
    gate = _sigmoid(_dot(g.astype(BF16), w_ref[...].astype(BF16)) + b_ref[...])
    o_ref[...] = _rms(g * gate, nw_ref[...]).astype(o_ref.dtype)


def s5_head(y, h, d, glu_w, glu_b, norm_w, *, tm):
    m, ds = y.shape
    row = lambda a: a.reshape(1, ds)
    vec = pl.BlockSpec((1, ds), lambda i: (0, 0))
    return pl.pallas_call(
        _s5_head_kernel,
        grid=(m // tm,),
        in_specs=[pl.BlockSpec((tm, ds), lambda i: (i, 0)), pl.BlockSpec((tm, ds), lambda i: (i, 0)), vec,
                  pl.BlockSpec((ds, ds), lambda i: (0, 0)), vec, vec],
        out_specs=pl.BlockSpec((tm, ds), lambda i: (i, 0)),
        out_shape=jax.ShapeDtypeStruct((m, ds), BF16),
        compiler_params=_cparams(("parallel",)),
        name="s5_head",
    )(y, h, row(d), glu_w, row(glu_b), row(norm_w))


SSD_CHUNKS_PER_STEP = 4


def _pair_select(first, col0, col1, shape):
    return jnp.where(first, jnp.broadcast_to(col0, shape), jnp.broadcast_to(col1, shape))


def _ssd_chunk_kernel(*refs, d_inner, nheads, n_xparts):
    xparts = refs[:n_xparts]
    (z_ref, dt_ref, cw_ref, cb_ref, dtb_ref, alog_ref, dvec_ref, nw_ref,
     out_ref, ssm_ref, conv_ref, state_sc, xpad_sc, y_sc) = refs[n_xparts:]
    c = pl.program_id(1)
    q = M2_CHUNK
    hp = M2_HEADDIM
    ns = M2_DSTATE
    heads_per_group = nheads // M2_NGROUPS
    halo = 8

    @pl.when(c == 0)
    def _():
        state_sc[...] = jnp.zeros_like(state_sc)
        xpad_sc[0:halo, :] = jnp.zeros((halo, xpad_sc.shape[1]), F32)

    nrows = z_ref.shape[0]
    wpart = xparts[0].shape[1]
    for i, xr in enumerate(xparts):
        xpad_sc[halo:halo + nrows, i * wpart:(i + 1) * wpart] = xr[...]
    cw = cw_ref[...]
    a = -jnp.exp(alog_ref[...])
    row = lax.broadcasted_iota(jnp.int32, (q, q), 0)
    col = lax.broadcasted_iota(jnp.int32, (q, q), 1)
    causal = row >= col
    tri = jnp.where(causal, 1.0, 0.0).astype(BF16)
    first = col < hp
    first_rows = row < hp

    for r0 in range(0, nrows, q):
        conv = cb_ref[...] + cw[M2_CONV - 1:M2_CONV, :] * xpad_sc[halo + r0:halo + r0 + q, :]
        for k in range(1, M2_CONV):
            conv = conv + cw[M2_CONV - 1 - k:M2_CONV - k, :] * xpad_sc[halo + r0 - k:halo + r0 - k + q, :]
        xc = _silu(conv)

        dt = _softplus(dt_ref[r0:r0 + q, :] + dtb_ref[...])
        da = dt * a
        d_hi, d_mid, d_lo = _split3(da)
        acum = _dot(tri, d_hi) + (_dot(tri, d_mid) + _dot(tri, d_lo))
        acum_t = acum.T
        alast = acum[q - 1:q, :]

        for pr in range(nheads // 2):
            grp = (2 * pr) // heads_per_group
            b_bf = xc[:, d_inner + grp * ns:d_inner + (grp + 1) * ns].astype(BF16)
            c_bf = xc[:, d_inner + (M2_NGROUPS + grp) * ns:d_inner + (M2_NGROUPS + grp + 1) * ns].astype(BF16)
            cb = _dot_nt(c_bf, b_bf)
            xpair = xc[:, pr * 2 * hp:(pr + 1) * 2 * hp]
            h0, h1 = 2 * pr, 2 * pr + 1
            acol = [acum[:, h:h + 1] for h in (h0, h1)]
            m = []
            for k, h in enumerate((h0, h1)):
                seg = jnp.broadcast_to(acol[k], (q, q)) - jnp.broadcast_to(acum_t[h:h + 1, :], (q, q))
                lmat = jnp.exp(jnp.where(causal, seg, -1e30))
                m.append((cb * lmat).astype(BF16))
            dtp = _pair_select(first, dt[:, h0:h0 + 1], dt[:, h1:h1 + 1], (q, q))
            xdt = xpair * dtp
            xdt_bf = xdt.astype(BF16)
            y_diag = jnp.where(first, _dot(m[0], xdt_bf), _dot(m[1], xdt_bf))
            dec_end = _pair_select(first, jnp.exp(alast[:, h0:h0 + 1] - acol[0]),
                                   jnp.exp(alast[:, h1:h1 + 1] - acol[1]), (q, q))
            xw_t = (xdt * dec_end).T.astype(BF16)
            chunk_state = _dot(xw_t, b_bf)
            rows = pl.ds(pr * 2 * hp, 2 * hp)
            prev = state_sc[rows, :]
            y_off = _dot_nt(c_bf, prev.astype(BF16)) * _pair_select(first, jnp.exp(acol[0]), jnp.exp(acol[1]),
                                                                     (q, q))
            sdec = jnp.where(first_rows, jnp.broadcast_to(jnp.exp(alast[:, h0:h0 + 1]), (q, q)),
                             jnp.broadcast_to(jnp.exp(alast[:, h1:h1 + 1]), (q, q)))
            state_sc[rows, :] = prev * sdec + chunk_state
            y_sc[:, pr * 2 * hp:(pr + 1) * 2 * hp] = (y_diag + y_off
                                                       + dvec_ref[:, pr * 2 * hp:(pr + 1) * 2 * hp] * xpair)

        out_ref[r0:r0 + q, :] = _rms(y_sc[...] * _silu(z_ref[r0:r0 + q, :]), nw_ref[...]).astype(out_ref.dtype)

    xpad_sc[0:halo, :] = xpad_sc[nrows:nrows + halo, :]

    @pl.when(c == pl.num_programs(1) - 1)
    def _():
        ssm_ref[...] = state_sc[...]
        conv_ref[...] = xpad_sc[halo + nrows - (M2_CONV - 1):halo + nrows, :]


def ssd_prompt(h, dt_raw, nb, seq, p, *, d_s5, d_inner, nheads):
    q = M2_CHUNK
    rows_step = SSD_CHUNKS_PER_STEP * q if seq % (SSD_CHUNKS_PER_STEP * q) == 0 else q
    nc = seq // rows_step
    conv_dim = d_inner + 2 * M2_NGROUPS * M2_DSTATE
    xw = 512
    xoff = d_s5 + d_inner
    assert d_s5 % d_inner == 0 and xoff % xw == 0 and conv_dim % xw == 0
    assert M2_CHUNK == 2 * M2_HEADDIM == M2_DSTATE == LANES
    zblk = d_s5 // d_inner
    n_xparts = conv_dim // xw
    m = nb * seq
    pad = lambda v: jnp.pad(v, (0, LANES - v.shape[0])).reshape(1, LANES)
    dvec = jnp.repeat(p['m2_d'], M2_HEADDIM).reshape(1, d_inner)
    vec = lambda n: pl.BlockSpec((1, n), lambda b, c: (0, 0))
    tok = lambda w, j: pl.BlockSpec((rows_step, w), lambda b, c, j=j: (b * nc + c, j))
    out, ssm, conv = pl.pallas_call(
        functools.partial(_ssd_chunk_kernel, d_inner=d_inner, nheads=nheads, n_xparts=n_xparts),
        grid=(nb, nc),
        in_specs=[tok(xw, xoff // xw + i) for i in range(n_xparts)] + [tok(d_inner, zblk), tok(LANES, 0),
                  pl.BlockSpec((M2_CONV, conv_dim), lambda b, c: (0, 0)), vec(conv_dim), vec(LANES), vec(LANES),
                  vec(d_inner), vec(d_inner)],
        out_specs=[tok(d_inner, 0),
                   pl.BlockSpec((None, nheads * M2_HEADDIM, M2_DSTATE), lambda b, c: (b, 0, 0)),
                   pl.BlockSpec((None, M2_CONV - 1, conv_dim), lambda b, c: (b, 0, 0))],
        out_shape=[jax.ShapeDtypeStruct((m, d_inner), BF16),
                   jax.ShapeDtypeStruct((nb, nheads * M2_HEADDIM, M2_DSTATE), F32),
                   jax.ShapeDtypeStruct((nb, M2_CONV - 1, conv_dim), F32)],
        scratch_shapes=[pltpu.VMEM((nheads * M2_HEADDIM, M2_DSTATE), F32),
                        pltpu.VMEM((rows_step + 8, conv_dim), F32),
                        pltpu.VMEM((q, d_inner), F32)],
        compiler_params=_cparams(("parallel", "arbitrary")),
        name="ssd_chunk",
    )(*([h] * n_xparts), h, dt_raw, p['m2_conv_w'], p['m2_conv_b'].reshape(1, conv_dim), pad(p['m2_dt_bias']),
      pad(p['m2_a_log']), dvec, p['m2_norm_w'].reshape(1, d_inner))
    return out, ssm.reshape(nb, nheads, M2_HEADDIM, M2_DSTATE), conv


SSD_STEP_SEQS = 8


def _ssd_step_kernel(*refs, d_inner, nheads, n_xparts):
    xparts = refs[:n_xparts]
    (z_ref, dt_ref, cs0_ref, cs1_ref, cs2_ref, cw_ref, cb_ref, dtb_ref, alog_ref, dvec_ref, nw_ref, st_ref,
     out_ref, so_ref, lhs_sc, bfull_sc, ct_sc, yt_sc, xs_sc) = refs[n_xparts:]
    i = pl.program_id(0)
    nb = z_ref.shape[0]
    ns = M2_DSTATE
    rows_g = (nheads // M2_NGROUPS) * M2_HEADDIM

    @pl.when(i == 0)
    def _():
        cw = cw_ref[...]
        xbc = jnp.concatenate([xr[...] for xr in xparts], axis=1)
        conv = (cb_ref[...] + cw[3:4, :] * xbc + cw[2:3, :] * cs2_ref[...]
                + cw[1:2, :] * cs1_ref[...] + cw[0:1, :] * cs0_ref[...])
        xc = _silu(conv)
        dt = _softplus(dt_ref[...] + dtb_ref[...])
        dec = jnp.exp(dt * (-jnp.exp(alog_ref[...])))
        hrow = lax.broadcasted_iota(jnp.int32, (LANES, d_inner), 0)
        hcol = lax.broadcasted_iota(jnp.int32, (LANES, d_inner), 1)
        expand = jnp.where(hcol // M2_HEADDIM == hrow, 1.0, 0.0).astype(BF16)

        def expand_heads(v):
            a, b_, c = _split3(v)
            return _dot(a, expand) + (_dot(b_, expand) + _dot(c, expand))

        xs = xc[:, :d_inner]
        xs_sc[...] = xs
        xdt_t = (xs * expand_heads(dt)).T
        d_hi, d_mid, d_lo = _split3(expand_heads(dec).T)
        for g in range(M2_NGROUPS):
            r = slice(g * rows_g, (g + 1) * rows_g)
            lhs_sc[g] = jnp.concatenate([xdt_t[r].astype(BF16), d_hi[r], d_mid[r], d_lo[r]], axis=1)
            b_g = xc[:, d_inner + g * ns:d_inner + (g + 1) * ns]
            bfull_sc[g] = jnp.concatenate([b_g, jnp.zeros_like(b_g)], axis=1)
            c_g = xc[:, d_inner + (M2_NGROUPS + g) * ns:d_inner + (M2_NGROUPS + g + 1) * ns]
            ct_sc[g] = c_g.T
        yt_sc[...] = jnp.zeros_like(yt_sc)

    row_id = lax.broadcasted_iota(jnp.int32, (nb, 2 * ns), 0)
    lane_id = lax.broadcasted_iota(jnp.int32, (nb, 2 * ns), 1)
    col_id = lax.broadcasted_iota(jnp.int32, (ns, nb), 1)
    for j in range(st_ref.shape[0]):
        b = i * st_ref.shape[0] + j
        r_bot = jnp.where((row_id == b) & (lane_id >= ns), 1.0, 0.0).astype(BF16)
        for g in range(M2_NGROUPS):
            r = pl.ds(g * rows_g, rows_g)
            r_top = jnp.where(row_id == b, bfull_sc[g], 0.0).astype(BF16)
            rhs = jnp.concatenate([r_top, r_bot, r_bot, r_bot], axis=0)
            o = _dot(lhs_sc[g], rhs)
            hnew = st_ref[j, r, :] * o[:, ns:] + o[:, :ns]
            so_ref[j, r, :] = hnew
            cm = jnp.where(col_id == b, ct_sc[g], 0.0).astype(BF16)
            yt_sc[r, :] += _dot(hnew.astype(BF16), cm)

    @pl.when(i == pl.num_programs(0) - 1)
    def _():
        y = yt_sc[...].T + dvec_ref[...] * xs_sc[...]
        out_ref[...] = _rms(y * _silu(z_ref[...]), nw_ref[...]).astype(out_ref.dtype)


def ssd_sample(h, dt_raw, state, conv_state, p, *, d_s5, d_inner, nheads):
    nb = h.shape[0]
    conv_dim = d_inner + 2 * M2_NGROUPS * M2_DSTATE
    xw = 512
    xoff = d_s5 + d_inner
    assert nb == LANES and M2_DSTATE == LANES and M2_CONV == 4
    assert xoff % xw == 0 and conv_dim % xw == 0 and d_s5 % d_inner == 0 and nb % SSD_STEP_SEQS == 0
    n_xparts = conv_dim // xw
    rows = nheads * M2_HEADDIM
    rows_g = rows // M2_NGROUPS
    pad = lambda v: jnp.pad(v, (0, LANES - v.shape[0])).reshape(1, LANES)
    dvec = jnp.repeat(p['m2_d'], M2_HEADDIM).reshape(1, d_inner)
    full = lambda a, b, j=0: pl.BlockSpec((a, b), lambda i, j=j: (0, j))
    st_spec = pl.BlockSpec((SSD_STEP_SEQS, rows, M2_DSTATE), lambda i: (i, 0, 0))
    out, new_state = pl.pallas_call(
        functools.partial(_ssd_step_kernel, d_inner=d_inner, nheads=nheads, n_xparts=n_xparts),
        grid=(nb // SSD_STEP_SEQS,),
        in_specs=[full(nb, xw, xoff // xw + k) for k in range(n_xparts)]
        + [full(nb, d_inner, d_s5 // d_inner), full(nb, LANES)]
        + [full(nb, conv_dim)] * 3
        + [full(M2_CONV, conv_dim), full(1, conv_dim), full(1, LANES), full(1, LANES), full(1, d_inner),
           full(1, d_inner), st_spec],
        out_specs=[full(nb, d_inner), st_spec],
        out_shape=[jax.ShapeDtypeStruct((nb, d_inner), BF16), jax.ShapeDtypeStruct((nb, rows, M2_DSTATE), F32)],
        scratch_shapes=[pltpu.VMEM((M2_NGROUPS, rows_g, 4 * nb), BF16),
                        pltpu.VMEM((M2_NGROUPS, nb, 2 * M2_DSTATE), F32),
                        pltpu.VMEM((M2_NGROUPS, M2_DSTATE, nb), F32),
                        pltpu.VMEM((rows, nb), F32),
                        pltpu.VMEM((nb, d_inner), F32)],
        compiler_params=_cparams(("arbitrary",)),
        name="ssd_step",
    )(*([h] * n_xparts), h, dt_raw, conv_state[:, 0], conv_state[:, 1], conv_state[:, 2],
      p['m2_conv_w'], p['m2_conv_b'].reshape(1, conv_dim), pad(p['m2_dt_bias']), pad(p['m2_a_log']),
      dvec, p['m2_norm_w'].reshape(1, d_inner), state.reshape(nb, rows, M2_DSTATE))
    xbc = lax.slice_in_dim(h, xoff, xoff + conv_dim, axis=1)
    new_conv = jnp.concatenate([conv_state[:, 1:], xbc[:, None, :]], axis=1)
    return out, new_state.reshape(state.shape), new_conv


def _softmax_rows(s):
    e = jnp.exp(s - jnp.max(s, axis=-1, keepdims=True))
    return e / jnp.sum(e, axis=-1, keepdims=True)


def _attn_kernel(q_ref, k_ref, v_ref, o_ref, *, scale):
    s = _dot_nt(q_ref[...].astype(BF16), k_ref[...].astype(BF16)) * scale
    o_ref[...] = _dot(_softmax_rows(s).astype(BF16), v_ref[...].astype(BF16)).astype(o_ref.dtype)


def attention_prompt(q, k, v, nb, seq, mem, heads, *, tq):
    d = q.shape[1]
    hd = d // heads
    nq = seq // tq
    kv_spec = pl.BlockSpec((mem, hd), lambda b, h, i: (b, h))
    q_spec = pl.BlockSpec((tq, hd), lambda b, h, i: (b * nq + i, h))
    return pl.pallas_call(
        functools.partial(_attn_kernel, scale=hd ** -0.5),
        grid=(nb, heads, nq),
        in_specs=[q_spec, kv_spec, kv_spec],
        out_specs=q_spec,
        out_shape=jax.ShapeDtypeStruct(q.shape, BF16),
        compiler_params=_cparams(("parallel", "parallel", "parallel")),
        name="attention_prompt",
    )(q, k, v)


ATTN_STEP_SEQS = 4


ATTN_STEP_ROWS = 64


def _attn_step_kernel(q_ref, k_ref, v_ref, o_ref, s_sc, *, scale):
    nseq, mem, heads, hd = k_ref.shape
    ch = ATTN_STEP_ROWS
    pack = 8 // heads
    for j in range(nseq):
        q = jnp.concatenate([q_ref[j]] * pack, axis=0)

        def score(c, mx):
            rows = pl.ds(pl.multiple_of(c * ch, ch), ch)
            k = k_ref[j, rows].reshape(ch // pack, pack * heads, hd)
            s = jnp.sum(k * q, axis=-1, keepdims=True) * scale
            s_sc[pl.ds(pl.multiple_of(c * (ch // pack), ch // pack), ch // pack)] = s
            return jnp.maximum(mx, jnp.max(s, axis=0))

        mx = lax.fori_loop(0, mem // ch, score, jnp.full((pack * heads, 1), NEG, F32))
        mx1 = mx[0:heads]
        for i in range(1, pack):
            mx1 = jnp.maximum(mx1, mx[i * heads:(i + 1) * heads])
        mx = jnp.concatenate([mx1] * pack, axis=0)

        def accum(c, carry):
            den, acc = carry
            rows = pl.ds(pl.multiple_of(c * ch, ch), ch)
            v = v_ref[j, rows].reshape(ch // pack, pack * heads, hd)
            e = jnp.exp(s_sc[pl.ds(pl.multiple_of(c * (ch // pack), ch // pack), ch // pack)] - mx)
            return den + jnp.sum(e, axis=0), acc + jnp.sum(e * v, axis=0)

        den, acc = lax.fori_loop(0, mem // ch, accum,
                                 (jnp.zeros((pack * heads, 1), F32), jnp.zeros((pack * heads, hd), F32)))
        den1, acc1 = den[0:heads], acc[0:heads]
        for i in range(1, pack):
            den1 = den1 + den[i * heads:(i + 1) * heads]
            acc1 = acc1 + acc[i * heads:(i + 1) * heads]
        o_ref[j] = acc1 / den1


def attention_sample(q, k_cache, v_cache):
    b, mem, heads, hd = k_cache.shape
    nseq = ATTN_STEP_SEQS
    q_spec = pl.BlockSpec((nseq, heads, hd), lambda i: (i, 0, 0))
    kv_spec = pl.BlockSpec((nseq, mem, heads, hd), lambda i: (i, 0, 0, 0))
    out = pl.pallas_call(
        functools.partial(_attn_step_kernel, scale=hd ** -0.5),
        grid=(b // nseq,),
        in_specs=[q_spec, kv_spec, kv_spec],
        out_specs=q_spec,
        out_shape=jax.ShapeDtypeStruct((b, heads, hd), F32),
        scratch_shapes=[pltpu.VMEM((mem * heads // 8, 8, 1), F32)],
        compiler_params=_cparams(("parallel",)),
        name="attention_step",
    )(q.reshape(b, heads, hd), k_cache, v_cache)
    return out.reshape(b, heads * hd)


NEG = -1e30


INFO_G1, INFO_G2, INFO_E1, INFO_E2 = 0, 1, 2, 3


def _lane_pack(lane, values):
    out = 0.0
    for k, v in values:
        out = jnp.where(lane == k, v, out)
    return out


def _router_kernel(x_ref, nw_ref, wr_ref, br_ref, sel_ref, info_ref, *, n_experts):
    xn = _rms(x_ref[...], nw_ref[...])
    x_hi, x_lo = _split2(xn)
    w_hi, w_lo = _split2(wr_ref[...])
    hi_terms = _dot(x_hi, jnp.concatenate([w_hi, w_lo], axis=1))
    logits = hi_terms[:, :LANES] + (hi_terms[:, LANES:] + _dot(x_lo, w_hi)) + br_ref[...]
    lane = lax.broadcasted_iota(jnp.int32, logits.shape, 1)
    big = jnp.int32(2 ** 30)
    is_c = (lane >= n_experts) & (lane < n_experts + MOE_GROUPS)
    lc = jnp.where(is_c, logits, NEG)
    cmax = jnp.max(lc, axis=-1, keepdims=True)
    gsel = jnp.min(jnp.where(lc == cmax, lane, big), axis=-1, keepdims=True) - n_experts
    gate_c = 1.0 / jnp.sum(jnp.where(is_c, jnp.exp(lc - cmax), 0.0), axis=-1, keepdims=True)
    in_group = (lane < n_experts) & (lane // MOE_PER_GROUP == gsel)
    lf = jnp.where(in_group, logits, NEG)
    t1 = jnp.max(lf, axis=-1, keepdims=True)
    i1 = jnp.min(jnp.where(lf == t1, lane, big), axis=-1, keepdims=True)
    lf2 = jnp.where(lane == i1, NEG, lf)
    t2 = jnp.max(lf2, axis=-1, keepdims=True)
    i2 = jnp.min(jnp.where(lf2 == t2, lane, big), axis=-1, keepdims=True)
    r = jnp.exp(t2 - t1)
    g1 = gate_c / (1.0 + r)
    g2 = gate_c * r / (1.0 + r)
    sel_ref[...] = jnp.where((lane == i1) | (lane == i2), 1.0, 0.0)
    info_ref[...] = _lane_pack(lane, ((INFO_G1, g1), (INFO_G2, g2), (INFO_E1, i1.astype(F32)),
                                      (INFO_E2, i2.astype(F32))))


def moe_router(x, norm_w, w_coarse, b_coarse, w_fine, b_fine, *, tm):
    m, d = x.shape
    e = w_fine.shape[1]
    padw = LANES - e - MOE_GROUPS
    wr = jnp.concatenate([w_fine, w_coarse, jnp.zeros((d, padw), F32)], axis=1)
    br = jnp.concatenate([b_fine, b_coarse, jnp.zeros((padw,), F32)]).reshape(1, LANES)
    tok = pl.BlockSpec((tm, LANES), lambda i: (i, 0))
    return pl.pallas_call(
        functools.partial(_router_kernel, n_experts=e),
        grid=(m // tm,),
        in_specs=[pl.BlockSpec((tm, d), lambda i: (i, 0)), pl.BlockSpec((1, d), lambda i: (0, 0)),
                  pl.BlockSpec((d, LANES), lambda i: (0, 0)), pl.BlockSpec((1, LANES), lambda i: (0, 0))],
        out_specs=[tok, tok],
        out_shape=[jax.ShapeDtypeStruct((m, LANES), F32), jax.ShapeDtypeStruct((m, LANES), F32)],
        compiler_params=_cparams(("parallel",)),
        name="moe_router",
    )(x, norm_w.reshape(1, d), wr, br)


def _moe_rank_kernel(sel_ref, info_ref, rank_ref, counts_ref, carry_sc):
    i = pl.program_id(0)

    @pl.when(i == 0)
    def _():
        carry_sc[...] = jnp.zeros_like(carry_sc)

    sel = sel_ref[...]
    tm = sel.shape[0]
    row = lax.broadcasted_iota(jnp.int32, (tm, tm), 0)
    col = lax.broadcasted_iota(jnp.int32, (tm, tm), 1)
    before = jnp.where(row > col, 1.0, 0.0).astype(BF16)
    rank = _dot(before, sel.astype(BF16)) + carry_sc[...]
    info = info_ref[...]
    lane = lax.broadcasted_iota(jnp.int32, sel.shape, 1)
    e1 = info[:, INFO_E1:INFO_E1 + 1].astype(jnp.int32)
    e2 = info[:, INFO_E2:INFO_E2 + 1].astype(jnp.int32)
    r1 = jnp.sum(jnp.where(lane == e1, rank, 0.0), axis=-1, keepdims=True)
    r2 = jnp.sum(jnp.where(lane == e2, rank, 0.0), axis=-1, keepdims=True)
    rank_ref[...] = _lane_pack(lane, ((0, r1), (1, r2)))
    carry_sc[...] += jnp.sum(sel, axis=0, keepdims=True)

    @pl.when(i == pl.num_programs(0) - 1)
    def _():
        counts_ref[...] = carry_sc[...]


def _moe_plan_kernel(rank_ref, info_ref, counts_ref, dest_ref, tiles_ref, *, tile, n_experts):
    counts = counts_ref[...]
    ntile_e = jnp.floor((counts + (tile - 1)) * (1.0 / tile))
    padded = jnp.broadcast_to(ntile_e * tile, (8, LANES))
    r = lax.broadcasted_iota(jnp.int32, (LANES, LANES), 0)
    c = lax.broadcasted_iota(jnp.int32, (LANES, LANES), 1)
    lower = jnp.where(r < c, 1.0, 0.0).astype(BF16)
    p_hi, p_mid, p_lo = _split3(padded)
    offs = (_dot(p_hi, lower) + (_dot(p_mid, lower) + _dot(p_lo, lower)))[0:1, :]
    ends = offs + padded[0:1, :]

    info = info_ref[...]
    rank = rank_ref[...]
    lane = lax.broadcasted_iota(jnp.int32, info.shape, 1)
    e1 = info[:, INFO_E1:INFO_E1 + 1].astype(jnp.int32)
    e2 = info[:, INFO_E2:INFO_E2 + 1].astype(jnp.int32)
    d1 = jnp.sum(jnp.where(lane == e1, offs, 0.0), axis=-1, keepdims=True) + rank[:, 0:1]
    d2 = jnp.sum(jnp.where(lane == e2, offs, 0.0), axis=-1, keepdims=True) + rank[:, 1:2]
    dest_ref[...] = _lane_pack(lane, ((0, d1), (1, d2))).astype(jnp.int32)

    ends_col = jnp.broadcast_to(ends, (LANES, LANES)).T
    start = (c * tile).astype(F32)
    n_before = jnp.sum(jnp.where((ends_col <= start) & (r < n_experts), 1.0, 0.0), axis=0, keepdims=True)
    lane1 = lax.broadcasted_iota(jnp.int32, (1, LANES), 1)
    total = jnp.sum(jnp.where(lane1 == n_experts - 1, ends, 0.0), axis=-1, keepdims=True)
    pad_start = offs + counts
    clear_from = jnp.floor(pad_start * (1.0 / MOE_CLEAR_ROWS)) * MOE_CLEAR_ROWS
    tail = jnp.where(pad_start < ends, clear_from, -1.0)
    tiles_ref[...] = jnp.zeros_like(tiles_ref)
    tiles_ref[0:1, :] = jnp.minimum(n_before, n_experts - 1.0).astype(jnp.int32)
    tiles_ref[1:2, :] = jnp.broadcast_to(total * (1.0 / tile), (1, LANES)).astype(jnp.int32)
    tiles_ref[2:3, :] = tail.astype(jnp.int32)
    tiles_ref[3:4, :] = ends.astype(jnp.int32)


def _moe_dispatch_kernel(dest_ref, tails_ref, ends_ref, ntiles_ref, *rest, tile, n_experts, group_steps,
                         group_offsets):
    TAIL_SEM, FILL_SEM, ROW_SEM = 0, 1, 2
    n_groups = len(group_offsets)
    nw_ref = rest[0]
    x_refs = rest[1:1 + n_groups]
    xs_ref, zero_sc, xn_sc, sems = rest[1 + n_groups:]
    i = pl.program_id(0)
    n_tiles_max = xs_ref.shape[0] // tile

    def fill_copy(j):
        return pltpu.make_async_copy(zero_sc, xs_ref.at[pl.ds(pl.multiple_of(j * tile, tile), tile)],
                                     sems.at[FILL_SEM])

    def tail_pieces(e, act):
        def piece(j, _):
            row = pl.multiple_of(tails_ref[e] + j * MOE_CLEAR_ROWS, MOE_CLEAR_ROWS)
            act(pltpu.make_async_copy(zero_sc.at[pl.ds(0, MOE_CLEAR_ROWS)], xs_ref.at[pl.ds(row, MOE_CLEAR_ROWS)],
                                      sems.at[TAIL_SEM]))
            return 0

        lax.fori_loop(0, (ends_ref[e] - tails_ref[e]) // MOE_CLEAR_ROWS, piece, 0)

    @pl.when(i == 0)
    def _():
        zero_sc[...] = jnp.zeros_like(zero_sc)

        def fill(j, _):
            fill_copy(j).start()
            return 0

        def clear(e, _):
            @pl.when(tails_ref[e] >= 0)
            def _():
                tail_pieces(e, lambda c: c.start())
            return 0

        def clear_wait(e, _):
            @pl.when(tails_ref[e] >= 0)
            def _():
                tail_pieces(e, lambda c: c.wait())
            return 0

        lax.fori_loop(0, n_experts, clear, 0)
        lax.fori_loop(0, n_experts, clear_wait, 0)
        lax.fori_loop(ntiles_ref[0], n_tiles_max, fill, 0)

    slot = i % 2

    def wait_rows(which, tmw):
        for _ in range(2):
            pltpu.make_async_copy(xn_sc.at[which, pl.ds(0, tmw)], xs_ref.at[pl.ds(0, tmw)],
                                  sems.at[ROW_SEM + which]).wait()

    for g, x_ref in enumerate(x_refs):
        @pl.when((i >= group_steps[g]) & (i < group_steps[g + 1]))
        def _(g=g, x_ref=x_ref):
            tm = x_ref.shape[0]
            xn_ref = xn_sc.at[slot, pl.ds(0, tm)]
            xn_ref[...] = _rms(x_ref[...], nw_ref[...])
            base = 2 * (group_offsets[g] + (i - group_steps[g]) * tm)
            for r in range(tm):
                for k in range(2):
                    pltpu.make_async_copy(xn_ref.at[pl.ds(r, 1)], xs_ref.at[pl.ds(dest_ref[base + 2 * r + k], 1)],
                                          sems.at[ROW_SEM + slot]).start(priority=k)

            @pl.when(i > group_steps[g])
            def _():
                wait_rows(1 - slot, tm)

            if g > 0:
                @pl.when(i == group_steps[g])
                def _():
                    wait_rows(1 - slot, x_refs[g - 1].shape[0])

            @pl.when(i == pl.num_programs(0) - 1)
            def _():
                wait_rows(slot, tm)

    @pl.when(i == pl.num_programs(0) - 1)
    def _():
        def fill_wait(j, _):
            fill_copy(j).wait()
            return 0

        lax.fori_loop(ntiles_ref[0], n_tiles_max, fill_wait, 0)


def _moe_expert_kernel(texp_ref, ntiles_ref, xs_ref, wg_hbm, wu_hbm, wd_hbm, ys_ref,
                       wg_sc, wu_sc, wd_sc, slot_sm, sems):
    i = pl.program_id(0)
    n = ntiles_ref[0]

    def weight_copies(e, slot):
        return [pltpu.make_async_copy(w.at[e], sc.at[slot], sems.at[slot])
                for w, sc in ((wg_hbm, wg_sc), (wu_hbm, wu_sc), (wd_hbm, wd_sc))]

    @pl.when(i < n)
    def _():
        e = texp_ref[i]
        first = (i == 0) | (texp_ref[jnp.maximum(i - 1, 0)] != e)

        @pl.when(i == 0)
        def _():
            slot_sm[0] = 0
            for c in weight_copies(e, 0):
                c.start()

        @pl.when(first & (i > 0))
        def _():
            slot_sm[0] = 1 - slot_sm[0]

        slot = slot_sm[0]

        @pl.when(first)
        def _():
            for c in weight_copies(e, slot):
                c.wait()
            j = lax.while_loop(lambda j: (j < n) & (texp_ref[jnp.minimum(j, n - 1)] == e), lambda j: j + 1, i + 1)

            @pl.when(j < n)
            def _():
                for c in weight_copies(texp_ref[jnp.minimum(j, n - 1)], 1 - slot):
                    c.start()

        x = xs_ref[...].astype(BF16)
        hg = _dot(x, wg_sc[slot].astype(BF16))
        hu = _dot(x, wu_sc[slot].astype(BF16))
        ys_ref[...] = _dot((_silu(hg) * hu).astype(BF16), wd_sc[slot].astype(BF16))


def _moe_combine_kernel(dest_ref, ys_ref, x_ref, info_ref, fw_ref, y_ref, buf_sc, sems):
    s = pl.program_id(0)
    n_blocks = pl.num_programs(0) - 1
    tm = x_ref.shape[0]
    slot = s % 2

    @pl.when(s < n_blocks)
    def _():
        for r in range(tm):
            for k in range(2):
                pltpu.make_async_copy(ys_ref.at[pl.ds(dest_ref[2 * (s * tm + r) + k], 1)],
                                      buf_sc.at[slot, k, pl.ds(r, 1)], sems.at[slot]).start(priority=k)

    @pl.when(s > 0)
    def _():
        prev = 1 - slot
        for k in range(2):
            pltpu.make_async_copy(ys_ref.at[pl.ds(0, tm)], buf_sc.at[prev, k], sems.at[prev]).wait()
        info = info_ref[...]
        y = (x_ref[...] + info[:, INFO_G1:INFO_G1 + 1] * buf_sc[prev, 0]
             + info[:, INFO_G2:INFO_G2 + 1] * buf_sc[prev, 1])
        y_ref[...] = _rms(y, fw_ref[...])


MOE_TILE = 256
MOE_CLEAR_ROWS = 32
MOE_TOKENS_PER_STEP = 256


def _largest_tile(m, cap):
    return max(t for t in range(8, cap + 1, 8) if m % t == 0)


def moe_routed_final(x_list, p, final_w):
    d = x_list[0].shape[1]
    ne, _, f = p['moe_w_gate'].shape
    sizes = [x.shape[0] for x in x_list]
    m = sum(sizes)
    tile = min(MOE_TILE, m)
    routed = [moe_router(x, p['norm_ffn_w'], p['router_coarse_w'], p['router_coarse_b'],
                         p['router_fine_w'], p['router_fine_b'], tm=min(x.shape[0], 512)) for x in x_list]
    sel = jnp.concatenate([r[0] for r in routed], axis=0)
    info = jnp.concatenate([r[1] for r in routed], axis=0)
    tm = _largest_tile(m, 1024)
    tok = pl.BlockSpec((tm, LANES), lambda i: (i, 0))
    rank, counts = pl.pallas_call(
        _moe_rank_kernel,
        grid=(m // tm,),
        in_specs=[tok, tok],
        out_specs=[tok, pl.BlockSpec((1, LANES), lambda i: (0, 0))],
        out_shape=[jax.ShapeDtypeStruct((m, LANES), F32), jax.ShapeDtypeStruct((1, LANES), F32)],
        scratch_shapes=[pltpu.VMEM((1, LANES), F32)],
        compiler_params=_cparams(("arbitrary",)),
        name="moe_rank",
    )(sel, info)
    n_tiles_max = (2 * m) // tile + ne
    assert n_tiles_max <= LANES
    dest, tiles = pl.pallas_call(
        functools.partial(_moe_plan_kernel, tile=tile, n_experts=ne),
        out_shape=[jax.ShapeDtypeStruct((m, LANES), jnp.int32), jax.ShapeDtypeStruct((8, LANES), jnp.int32)],
        compiler_params=pltpu.CompilerParams(vmem_limit_bytes=VMEM_LIMIT),
        name="moe_plan",
    )(rank, info, counts)
    dest_flat = dest[:, :2].reshape(2 * m)
    tile_expert = tiles[0, :n_tiles_max]
    n_tiles = tiles[1, :1]
    tails = tiles[2, :ne]
    seg_ends = tiles[3, :ne]

    rows = n_tiles_max * tile

    tds = [min(mg, 2 * MOE_TOKENS_PER_STEP) for mg in sizes]
    group_steps = [0]
    for mg, td in zip(sizes, tds):
        group_steps.append(group_steps[-1] + mg // td)
    group_offsets = [sum(sizes[:g]) for g in range(len(sizes))]

    def group_spec(g):
        first, last = group_steps[g], group_steps[g + 1] - 1
        return pl.BlockSpec((tds[g], d), lambda i, *_: (jnp.clip(i, first, last) - first, 0))

    xs = pl.pallas_call(
        functools.partial(_moe_dispatch_kernel, tile=tile, n_experts=ne, group_steps=tuple(group_steps),
                          group_offsets=tuple(group_offsets)),
        grid_spec=pltpu.PrefetchScalarGridSpec(
            num_scalar_prefetch=4, grid=(group_steps[-1],),
            in_specs=[pl.BlockSpec((1, d), lambda i, *_: (0, 0))] + [group_spec(g) for g in range(len(sizes))],
            out_specs=pl.BlockSpec(memory_space=pl.ANY),
            scratch_shapes=[pltpu.VMEM((tile, d), F32), pltpu.VMEM((2, max(tds), d), F32),
                            pltpu.SemaphoreType.DMA((4,))]),
        out_shape=jax.ShapeDtypeStruct((rows, d), F32),
        compiler_params=_cparams(("arbitrary",)),
        name="moe_dispatch",
    )(dest_flat, tails, seg_ends, n_tiles, p['norm_ffn_w'].reshape(1, d), *x_list)

    def tile_idx(i, te, nt):
        return jnp.minimum(i, nt[0] - 1)

    ys = pl.pallas_call(
        _moe_expert_kernel,
        grid_spec=pltpu.PrefetchScalarGridSpec(
            num_scalar_prefetch=2, grid=(n_tiles_max,),
            in_specs=[pl.BlockSpec((tile, d), lambda i, te, nt: (tile_idx(i, te, nt), 0)),
                      pl.BlockSpec(memory_space=pl.ANY), pl.BlockSpec(memory_space=pl.ANY),
                      pl.BlockSpec(memory_space=pl.ANY)],
            out_specs=pl.BlockSpec((tile, d), lambda i, te, nt: (tile_idx(i, te, nt), 0)),
            scratch_shapes=[pltpu.VMEM((2, d, f), F32), pltpu.VMEM((2, d, f), F32), pltpu.VMEM((2, f, d), F32),
                            pltpu.SMEM((1,), jnp.int32), pltpu.SemaphoreType.DMA((2,))]),
        out_shape=jax.ShapeDtypeStruct((rows, d), F32),
        input_output_aliases={2: 0},
        compiler_params=_cparams(("arbitrary",)),
        name="moe_experts",
    )(tile_expert, n_tiles, xs, p['moe_w_gate'], p['moe_w_up'], p['moe_w_down'])

    outs = []
    off = 0
    for x, (_, info_g), mg in zip(x_list, routed, sizes):
        tc = min(mg, MOE_TOKENS_PER_STEP)
        outs.append(pl.pallas_call(
            _moe_combine_kernel,
            grid_spec=pltpu.PrefetchScalarGridSpec(
                num_scalar_prefetch=1, grid=(mg // tc + 1,),
                in_specs=[pl.BlockSpec(memory_space=pl.ANY),
                          pl.BlockSpec((tc, d), lambda s, dref: (jnp.maximum(s - 1, 0), 0)),
                          pl.BlockSpec((tc, LANES), lambda s, dref: (jnp.maximum(s - 1, 0), 0)),
                          pl.BlockSpec((1, d), lambda s, dref: (0, 0))],
                out_specs=pl.BlockSpec((tc, d), lambda s, dref: (jnp.maximum(s - 1, 0), 0)),
                scratch_shapes=[pltpu.VMEM((2, 2, tc, d), F32), pltpu.SemaphoreType.DMA((2,))]),
            out_shape=jax.ShapeDtypeStruct((mg, d), F32),
            compiler_params=_cparams(("arbitrary",)),
            name="moe_combine",
        )(dest_flat[2 * off:2 * (off + mg)], ys, x, info_g, final_w.reshape(1, d)))
        off += mg
    return outs


def s5_prompt(h, nb, seq, prep, p):
    y, hfin = s5_scan(h, prep, nb, seq)
    out = s5_head(y, h, p['s5_d'], p['s5_glu_w'], p['s5_glu_b'], p['s5_norm_w'], tm=min(1024, nb * seq))
    return out, hfin[:, 0].transpose(1, 0, 2), hfin[:, 1].transpose(1, 0, 2)


def s5_sample(h, st_re, st_im, prep, p):
    y, n_re, n_im = s5_step(h, st_re, st_im, prep, p['s5_c_re'], p['s5_c_im'])
    out = s5_head(y, h, p['s5_d'], p['s5_glu_w'], p['s5_glu_b'], p['s5_norm_w'], tm=h.shape[0])
    return out, n_re, n_im


def _row_tile(m):
    return min(m, 1024)


def _col_tile(m, n):
    if m <= 256:
        return n
    return max(t for t in range(256, 1793, 256) if n % t == 0)


def _mixer_and_attention(x, p, s5_prep, *, nb, seq, mem_kv, xa_heads, states):
    m, d = x.shape
    g, n = p['s5_a_re'].shape
    d_s5 = g * S5_CH
    nheads = p['m2_a_log'].shape[0]
    d_inner = nheads * M2_HEADDIM
    conv_dim = d_inner + 2 * M2_NGROUPS * M2_DSTATE
    n_main = d_s5 + d_inner + conv_dim
    tm = _row_tile(m)

    w_in = p['w_in']
    w_dt = jnp.pad(w_in[:, n_main:], ((0, 0), (0, LANES - nheads)))
    tm2, tn2 = tm, _col_tile(m, d)
    h, dt_raw = fused_matmul([x], w_in, n_out=n_main, gain=p['norm_mix_w'], side_w=w_dt, tm=tm,
                             tn=_col_tile(m, n_main))

    if states is None:
        s5_out, s5_re, s5_im = s5_prompt(h, nb, seq, s5_prep, p)
        m2_out, ssm, conv = ssd_prompt(h, dt_raw, nb, seq, p, d_s5=d_s5, d_inner=d_inner, nheads=nheads)
    else:
        s5_out, s5_re, s5_im = s5_sample(h, states[0], states[1], s5_prep, p)
        m2_out, ssm, conv = ssd_sample(h, dt_raw, states[2], states[3], p, d_s5=d_s5, d_inner=d_inner,
                                       nheads=nheads)
    x1 = fused_matmul([s5_out, m2_out], p['w_out'], n_out=d, res=x, tm=tm2, tn=tn2)

    q = fused_matmul([x1], p['xa_wq'], n_out=d, gain=p['norm_xa_w'], tm=tm2, tn=tn2,
                     out_dtype=BF16 if states is None else F32)
    if states is None:
        mem = mem_kv[0].shape[0] // nb
        o = attention_prompt(q, mem_kv[0], mem_kv[1], nb, seq, mem, xa_heads, tq=min(seq, 2048))
    else:
        o = attention_sample(q, mem_kv[0], mem_kv[1])
    x2 = fused_matmul([o], p['xa_wo'], n_out=d, res=x1, tm=tm2, tn=tn2)

    return x2, s5_re, s5_im, ssm, conv


def kernel(x_prompt, x_sample, mem_prompt, state_s5_re, state_s5_im, state_ssm, state_conv, cache_mem_k, cache_mem_v, norm_mix_w, w_in, s5_a_re, s5_a_im, s5_log_dt, s5_b_re, s5_b_im, s5_c_re, s5_c_im, s5_d, s5_glu_w, s5_glu_b, s5_norm_w, m2_conv_w, m2_conv_b, m2_dt_bias, m2_a_log, m2_d, m2_norm_w, w_out, norm_xa_w, norm_mem_w, xa_wq, xa_wk, xa_wv, xa_wo, norm_ffn_w, router_coarse_w, router_coarse_b, router_fine_w, router_fine_b, moe_w_gate, moe_w_up, moe_w_down, norm_final_w):
    depth = w_in.shape[0]
    assert depth == 1, "the final norm is fused into the (only) layer"
    per_layer = dict(
        norm_mix_w=norm_mix_w, w_in=w_in, s5_a_re=s5_a_re, s5_a_im=s5_a_im, s5_log_dt=s5_log_dt,
        s5_b_re=s5_b_re, s5_b_im=s5_b_im, s5_c_re=s5_c_re, s5_c_im=s5_c_im, s5_d=s5_d, s5_glu_w=s5_glu_w,
        s5_glu_b=s5_glu_b, s5_norm_w=s5_norm_w, m2_conv_w=m2_conv_w, m2_conv_b=m2_conv_b, m2_dt_bias=m2_dt_bias,
        m2_a_log=m2_a_log, m2_d=m2_d, m2_norm_w=m2_norm_w, w_out=w_out, norm_xa_w=norm_xa_w,
        norm_mem_w=norm_mem_w, xa_wq=xa_wq, xa_wk=xa_wk, xa_wv=xa_wv, xa_wo=xa_wo, norm_ffn_w=norm_ffn_w,
        router_coarse_w=router_coarse_w, router_coarse_b=router_coarse_b, router_fine_w=router_fine_w,
        router_fine_b=router_fine_b, moe_w_gate=moe_w_gate, moe_w_up=moe_w_up, moe_w_down=moe_w_down)
    p = {k: v[0] for k, v in per_layer.items()}
    for name in ('w_in', 'w_out', 'xa_wq', 'xa_wo'):
        p[name] = p[name].astype(BF16)
    nb, seq, d = x_prompt.shape
    db, dseq, _ = x_sample.shape
    assert dseq == 1
    mem = mem_prompt.shape[1]
    xa_heads = cache_mem_k.shape[3]

    s5_prep = s5_prepare(p['s5_a_re'], p['s5_a_im'], p['s5_log_dt'], p['s5_b_re'], p['s5_b_im'],
                         p['s5_c_re'], p['s5_c_im'])

    memx = mem_prompt.reshape(nb * mem, d)
    mk = fused_matmul([memx], p['xa_wk'], n_out=d, gain=p['norm_mem_w'], tm=_row_tile(nb * mem),
                      tn=_col_tile(nb * mem, d))
    mv = fused_matmul([memx], p['xa_wv'], n_out=d, gain=p['norm_mem_w'], tm=_row_tile(nb * mem),
                      tn=_col_tile(nb * mem, d))
    xp, p_re, p_im, p_ssm, p_conv = _mixer_and_attention(
        x_prompt.reshape(nb * seq, d), p, s5_prep, nb=nb, seq=seq, mem_kv=(mk, mv), xa_heads=xa_heads, states=None)

    xs, s_re, s_im, s_ssm, s_conv = _mixer_and_attention(
        x_sample.reshape(db, d), p, s5_prep, nb=db, seq=1,
        mem_kv=(cache_mem_k[0], cache_mem_v[0]), xa_heads=xa_heads,
        states=(state_s5_re[0], state_s5_im[0], state_ssm[0], state_conv[0]))

    yp, ys = moe_routed_final([xp, xs], p, norm_final_w)

    kv_shape = (1, nb, mem) + cache_mem_k.shape[3:]
    return (yp.reshape(nb, seq, d), ys.reshape(db, 1, d), p_re[None], p_im[None], p_ssm[None], p_conv[None],
            mk.reshape(kv_shape), mv.reshape(kv_shape), s_re[None], s_im[None], s_ssm[None], s_conv[None])
```

```python
import functools
import math

import jax
import jax.numpy as jnp
from jax import lax
from jax.experimental import pallas as pl
from jax.experimental.pallas import tpu as pltpu

F32 = jnp.float32
BF16 = jnp.bfloat16
RMS_EPS = 1e-6

V7X_VMEM_BYTES = 64 * 1024 * 1024
VMEM_LIMIT = V7X_VMEM_BYTES - 8 * 1024 * 1024
LANES = 128

S5_CH = 16
S5_N = 64
S5_Q = 16
S5_GB = 8
S5_SEQ_PAD = 8
M2_HEADDIM = 64
M2_DSTATE = 128
M2_NGROUPS = 2
M2_CONV = 4
M2_CHUNK = 128
MOE_GROUPS = 4
MOE_PER_GROUP = 8


def _cparams(sem):
    return pltpu.CompilerParams(dimension_semantics=sem, vmem_limit_bytes=VMEM_LIMIT)


def _rms(x, w):
    return x * lax.rsqrt(jnp.mean(x * x, axis=-1, keepdims=True) + RMS_EPS) * w


def _sigmoid(x):
    return 1.0 / (1.0 + jnp.exp(-x))


def _silu(x):
    return x * _sigmoid(x)


def _softplus(x):
    return jnp.maximum(x, 0.0) + jnp.log1p(jnp.exp(-jnp.abs(x)))


def _gelu_tanh(x):
    return 0.5 * x * (1.0 + jnp.tanh(math.sqrt(2.0 / math.pi) * (x + 0.044715 * (x * x * x))))


def _dot(a, b):
    return jnp.dot(a, b, preferred_element_type=F32)


def _dot_nt(a, b):
    return lax.dot_general(a, b, (((1,), (1,)), ((), ())), preferred_element_type=F32)


def _split3(x):
    hi = x.astype(BF16)
    r = x - hi.astype(F32)
    mid = r.astype(BF16)
    lo = (r - mid.astype(F32)).astype(BF16)
    return hi, mid, lo


def _split2(x):
    hi = x.astype(BF16)
    lo = (x - hi.astype(F32)).astype(BF16)
    return hi, lo


def _mm_kernel(*refs, n_lhs, has_gain, has_res, has_side, staged):
    it = iter(refs)
    lhs = [next(it) for _ in range(n_lhs)]
    gain = next(it) if has_gain else None
    ws = [next(it) for _ in range(n_lhs)]
    side_w = next(it) if has_side else None
    res = next(it) if has_res else None
    out = next(it)
    side_out = next(it) if has_side else None
    lhs_bf = next(it) if staged else lhs

    if staged:
        @pl.when(pl.program_id(1) == 0)
        def _():
            for i in range(n_lhs):
                x = lhs[i][...]
                if has_gain:
                    x = _rms(x, gain[...])
                lhs_bf[i] = x.astype(BF16)
            if has_side:
                side_out[...] = _dot(lhs_bf[0], side_w[...].astype(BF16))

    acc = None
    for i in range(n_lhs):
        p = _dot(lhs_bf[i][...], ws[i][...].astype(BF16))
        acc = p if acc is None else acc + p
    if has_res:
        acc = acc + res[...]
    out[...] = acc.astype(out.dtype)


def fused_matmul(lhs_list, w, *, n_out, gain=None, res=None, side_w=None, out_dtype=F32, tm, tn):
    n_lhs = len(lhs_list)
    m, kp = lhs_list[0].shape
    assert all(a.shape == (m, kp) for a in lhs_list)
    assert w.shape[0] == n_lhs * kp and m % tm == 0 and n_out % tn == 0
    assert gain is None or n_lhs == 1
    staged = gain is not None or any(a.dtype != BF16 for a in lhs_list)
    assert staged or side_w is None
    grid = (m // tm, n_out // tn)
    in_specs = [pl.BlockSpec((tm, kp), lambda i, j: (i, 0)) for _ in range(n_lhs)]
    args = list(lhs_list)
    if gain is not None:
        in_specs.append(pl.BlockSpec((1, kp), lambda i, j: (0, 0)))
        args.append(gain.reshape(1, kp))
    for p in range(n_lhs):
        in_specs.append(pl.BlockSpec((kp, tn), lambda i, j, p=p: (p, j)))
        args.append(w)
    if side_w is not None:
        in_specs.append(pl.BlockSpec((kp, LANES), lambda i, j: (0, 0)))
        args.append(side_w)
    if res is not None:
        in_specs.append(pl.BlockSpec((tm, tn), lambda i, j: (i, j)))
        args.append(res)
    out_shape = [jax.ShapeDtypeStruct((m, n_out), out_dtype)]
    out_specs = [pl.BlockSpec((tm, tn), lambda i, j: (i, j))]
    if side_w is not None:
        out_shape.append(jax.ShapeDtypeStruct((m, LANES), F32))
        out_specs.append(pl.BlockSpec((tm, LANES), lambda i, j: (i, 0)))
    outs = pl.pallas_call(
        functools.partial(_mm_kernel, n_lhs=n_lhs, has_gain=gain is not None,
                          has_res=res is not None, has_side=side_w is not None, staged=staged),
        grid=grid, in_specs=in_specs, out_specs=out_specs, out_shape=out_shape,
        scratch_shapes=[pltpu.VMEM((n_lhs, tm, kp), BF16)] if staged else [],
        compiler_params=_cparams(("parallel", "arbitrary")),
        name="fused_matmul",
    )(*args)
    return outs if side_w is not None else outs[0]


def _memory_kv_kernel(x_ref, g_ref, wk_ref, wv_ref, k_ref, v_ref, lhs_sc, *, ntn):
    j = pl.program_id(1)

    @pl.when(j == 0)
    def _():
        lhs_sc[...] = _rms(x_ref[...], g_ref[...]).astype(BF16)

    @pl.when(j < ntn)
    def _():
        k_ref[...] = _dot(lhs_sc[...], wk_ref[...].astype(BF16))

    @pl.when(j >= ntn)
    def _():
        v_ref[...] = _dot(lhs_sc[...], wv_ref[...].astype(BF16))


def memory_kv(x, gain, wk, wv, *, tm, tn):
    m, d = x.shape
    n = wk.shape[1]
    assert wk.shape == wv.shape == (d, n) and m % tm == 0 and n % tn == 0
    ntn = n // tn
    first = lambda j: jnp.minimum(j, ntn - 1)
    second = lambda j: jnp.maximum(j - ntn, 0)
    return pl.pallas_call(
        functools.partial(_memory_kv_kernel, ntn=ntn),
        grid=(m // tm, 2 * ntn),
        in_specs=[pl.BlockSpec((tm, d), lambda i, j: (i, 0)), pl.BlockSpec((1, d), lambda i, j: (0, 0)),
                  pl.BlockSpec((d, tn), lambda i, j: (0, first(j))), pl.BlockSpec((d, tn), lambda i, j: (0, second(j)))],
        out_specs=[pl.BlockSpec((tm, tn), lambda i, j: (i, first(j))),
                   pl.BlockSpec((tm, tn), lambda i, j: (i, second(j)))],
        out_shape=[jax.ShapeDtypeStruct((m, n), F32), jax.ShapeDtypeStruct((m, n), F32)],
        scratch_shapes=[pltpu.VMEM((tm, d), BF16)],
        compiler_params=_cparams(("parallel", "arbitrary")),
        name="memory_kv",
    )(x, gain.reshape(1, d), wk, wv)


def _s5_prep_kernel(*refs):
    for g in range(refs[0].shape[0]):
        _s5_prep_group(*[r.at[g] for r in refs])


def _s5_prep_group(lre_ref, lim_ref, ldt_ref, btre_ref, btim_ref, cre_ref, cim_ref,
                   tz_ref, wsre_ref, wsim_ref, wcre_ref, wcim_ref, aq_ref, ab_ref, bbt_ref):
    q, ch = S5_Q, S5_CH
    lr = lre_ref[...]
    li = lim_ref[...]
    step = jnp.exp(ldt_ref[...])
    mag = jnp.exp(lr * step)
    ab_re = mag * jnp.cos(li * step)
    ab_im = mag * jnp.sin(li * step)
    den = lr * lr + li * li
    num_re = ab_re - 1.0
    coef_re = (num_re * lr + ab_im * li) / den
    coef_im = (ab_im * lr - num_re * li) / den
    bt_re = btre_ref[...]
    bt_im = btim_ref[...]
    bb_re = coef_re * bt_re - coef_im * bt_im
    bb_im = coef_re * bt_im + coef_im * bt_re
    c_re = cre_ref[...]
    c_im = cim_ref[...]

    pw = [(jnp.ones_like(ab_re), jnp.zeros_like(ab_re))]
    for _ in range(q):
        pr, pi = pw[-1]
        pw.append((pr * ab_re - pi * ab_im, pr * ab_im + pi * ab_re))

    ca_re = [c_re * pr - c_im * pi for pr, pi in pw]
    ca_im = [c_re * pi + c_im * pr for pr, pi in pw]
    wcre_ref[...] = jnp.concatenate(ca_re[1:], axis=0).astype(BF16)
    wcim_ref[...] = jnp.concatenate([-x for x in ca_im[1:]], axis=0).astype(BF16)

    pr_stack = jnp.concatenate(ca_re[:q], axis=0)
    pi_stack = jnp.concatenate(ca_im[:q], axis=0)
    krow = None
    for a, b, sign in ((bb_re, pr_stack, 1.0), (bb_im, pi_stack, -1.0)):
        a_hi, a_lo = _split2(a)
        b_hi, b_lo = _split2(b)
        t = _dot_nt(a_hi, b_hi) + (_dot_nt(a_hi, b_lo) + _dot_nt(a_lo, b_hi))
        krow = sign * t if krow is None else krow + sign * t
    lane = lax.broadcasted_iota(jnp.int32, krow.shape, 1)
    blocks = [krow]
    for s in range(1, q):
        blocks.append(jnp.where(lane >= s * ch, pltpu.roll(krow, s * ch, 1), 0.0))
    tz_ref[...] = jnp.concatenate(blocks, axis=0).astype(BF16)

    ws_re, ws_im = [], []
    for s in range(q):
        pr, pi = pw[q - 1 - s]
        ws_re.append(bb_re * pr - bb_im * pi)
        ws_im.append(bb_re * pi + bb_im * pr)
    wsre_ref[...] = jnp.concatenate(ws_re, axis=0).astype(BF16)
    wsim_ref[...] = jnp.concatenate(ws_im, axis=0).astype(BF16)

    aq_ref[0:1, :] = pw[q][0]
    aq_ref[1:2, :] = pw[q][1]
    ab_ref[0:1, :] = ab_re
    ab_ref[1:2, :] = ab_im
    bbt_ref[0:ch, :] = bb_re
    bbt_ref[ch:2 * ch, :] = bb_im


def s5_prepare(a_re, a_im, log_dt, b_re, b_im, c_re, c_im):
    g, n = a_re.shape
    ch, q = S5_CH, S5_Q
    qc = q * ch
    bt_re = jnp.swapaxes(b_re, 1, 2)
    bt_im = jnp.swapaxes(b_im, 1, 2)

    def per_g(*dims):
        return pl.BlockSpec((S5_GB,) + dims, lambda i: (i,) + (0,) * len(dims))

    return pl.pallas_call(
        _s5_prep_kernel,
        grid=(g // S5_GB,),
        in_specs=[per_g(1, n), per_g(1, n), per_g(1, 1), per_g(ch, n), per_g(ch, n), per_g(ch, n), per_g(ch, n)],
        out_specs=[per_g(qc, qc), per_g(qc, n), per_g(qc, n), per_g(qc, n), per_g(qc, n),
                   per_g(2, n), per_g(2, n), per_g(2 * ch, n)],
        out_shape=[jax.ShapeDtypeStruct((g, qc, qc), BF16),
                   jax.ShapeDtypeStruct((g, qc, n), BF16), jax.ShapeDtypeStruct((g, qc, n), BF16),
                   jax.ShapeDtypeStruct((g, qc, n), BF16), jax.ShapeDtypeStruct((g, qc, n), BF16),
                   jax.ShapeDtypeStruct((g, 2, n), F32), jax.ShapeDtypeStruct((g, 2, n), F32),
                   jax.ShapeDtypeStruct((g, 2 * ch, n), F32)],
        compiler_params=_cparams(("parallel",)),
        name="s5_prepare",
    )(a_re.reshape(g, 1, n), a_im.reshape(g, 1, n), log_dt.reshape(g, 1, 1), bt_re, bt_im, c_re, c_im)


def _s5_scan_kernel(h_ref, tz_ref, wsre_ref, wsim_ref, wcre_ref, wcim_ref, aq_ref,
                    y_ref, hfin_ref, xs_sc, u_sc, yg_sc, sre_sc, sim_sc, *, nb, nchunk):
    gb, q, ch = S5_GB, S5_Q, S5_CH
    rows = nb * nchunk
    seq_stride = nchunk + S5_SEQ_PAD
    per_vreg = LANES // ch
    assert gb == per_vreg and q % per_vreg == 0
    slot = lax.broadcasted_iota(jnp.int32, (rows, LANES), 1) // ch

    halves = q // per_vreg

    def rot_rows(w, g):
        if g == 0:
            return w
        cut = (per_vreg - g) * ch
        parts = []
        for hf in range(halves):
            blk = w[hf * LANES:(hf + 1) * LANES]
            parts += [blk[cut:], blk[:cut]]
        return jnp.concatenate(parts, axis=0)

    for s in range(q):
        x = h_ref[pl.ds(s, rows, stride=q), :].astype(BF16)
        k = s % per_vreg
        xs_sc[s] = pltpu.roll(x, k * ch, 1) if k else x
    keep = [jnp.where(slot == j, 1.0, 0.0).astype(BF16) for j in range(per_vreg)]
    for g in range(gb):
        for hf in range(halves):
            acc = None
            for k in range(per_vreg):
                piece = xs_sc[hf * per_vreg + k] * keep[(g + k) % per_vreg]
                acc = piece if acc is None else acc + piece
            u_sc[g, :, hf * LANES:(hf + 1) * LANES] = acc

    for g in range(gb):
        u = u_sc[g]
        tz = rot_rows(tz_ref[g], g)
        tz = jnp.concatenate([pltpu.roll(tz[:, hf * LANES:(hf + 1) * LANES], g * ch, 1) if g
                              else tz[:, hf * LANES:(hf + 1) * LANES] for hf in range(halves)], axis=1)
        yg_sc[g] = _dot(u, tz)
        for sc, w_ref in ((sre_sc, wsre_ref), (sim_sc, wsim_ref)):
            s_all = _dot(u, rot_rows(w_ref[g], g))
            for b in range(nb):
                sc[g, b * seq_stride:b * seq_stride + nchunk, :] = s_all[b * nchunk:(b + 1) * nchunk]

    ar = [jnp.broadcast_to(aq_ref[g, 0:1, :], (nb, S5_N)) for g in range(gb)]
    ai = [jnp.broadcast_to(aq_ref[g, 1:2, :], (nb, S5_N)) for g in range(gb)]

    def step(c, carry):
        at = pl.ds(c, nb, stride=seq_stride)
        new = []
        for g in range(gb):
            hr, hi = carry[g]
            sr = sre_sc[g, at, :]
            si = sim_sc[g, at, :]
            sre_sc[g, at, :] = hr
            sim_sc[g, at, :] = hi
            new.append((ar[g] * hr - ai[g] * hi + sr, ar[g] * hi + ai[g] * hr + si))
        return tuple(new)

    zero = jnp.zeros((nb, S5_N), F32)
    fin = lax.fori_loop(0, nchunk, step, tuple((zero, zero) for _ in range(gb)), unroll=4)
    for g in range(gb):
        hfin_ref[g, 0] = fin[g][0]
        hfin_ref[g, 1] = fin[g][1]
        h_in = [jnp.concatenate([sc[g, b * seq_stride:b * seq_stride + nchunk, :] for b in range(nb)], axis=0)
                for sc in (sre_sc, sim_sc)]
        yg_sc[g] += (_dot_nt(h_in[0].astype(BF16), rot_rows(wcre_ref[g], g))
                     + _dot_nt(h_in[1].astype(BF16), rot_rows(wcim_ref[g], g)))

    for t in range(q):
        hf, k = divmod(t, per_vreg)
        acc = jnp.zeros((rows, LANES), F32)
        for g in range(gb):
            acc = jnp.where(slot == (k + g) % per_vreg, yg_sc[g, :, hf * LANES:(hf + 1) * LANES], acc)
        y_ref[pl.ds(t, rows, stride=q), :] = pltpu.roll(acc, (per_vreg - k) * ch, 1) if k else acc


def s5_scan(h, prep, nb, seq):
    tz, ws_re, ws_im, wc_re, wc_im, aq = prep[:6]
    g, qc, _ = tz.shape
    n = S5_N
    gb = S5_GB
    q = S5_Q
    assert gb * S5_CH == LANES and qc == q * S5_CH and seq % q == 0
    nchunk = seq // q
    rows = nb * nchunk
    m = nb * seq

    def blk(*dims):
        return pl.BlockSpec((gb,) + dims, lambda i: (i,) + (0,) * len(dims))

    return pl.pallas_call(
        functools.partial(_s5_scan_kernel, nb=nb, nchunk=nchunk),
        grid=(g // gb,),
        in_specs=[pl.BlockSpec((m, LANES), lambda i: (0, i)),
                  blk(qc, qc), blk(qc, n), blk(qc, n), blk(qc, n), blk(qc, n), blk(2, n)],
        out_specs=[pl.BlockSpec((m, LANES), lambda i: (0, i)), blk(2, nb, n)],
        out_shape=[jax.ShapeDtypeStruct((m, g * S5_CH), F32), jax.ShapeDtypeStruct((g, 2, nb, n), F32)],
        scratch_shapes=[pltpu.VMEM((q, rows, LANES), BF16), pltpu.VMEM((gb, rows, qc), BF16),
                        pltpu.VMEM((gb, rows, qc), F32),
                        pltpu.VMEM((gb, nb * (nchunk + S5_SEQ_PAD), n), F32),
                        pltpu.VMEM((gb, nb * (nchunk + S5_SEQ_PAD), n), F32)],
        compiler_params=_cparams(("parallel",)),
        name="s5_scan",
    )(h, tz, ws_re, ws_im, wc_re, wc_im, aq)


def _s5_step_kernel(u_ref, hre_ref, him_ref, ab_ref, bbt_ref, cre_ref, cim_ref, y_ref, ore_ref, oim_ref):
    gb = hre_ref.shape[1]
    ch = S5_CH
    for g in range(gb):
        u = u_ref[:, g * ch:(g + 1) * ch].astype(BF16)
        bb_re = bbt_ref[g, 0:ch, :].astype(BF16)
        bb_im = bbt_ref[g, ch:2 * ch, :].astype(BF16)
        ar = ab_ref[g, 0:1, :]
        ai = ab_ref[g, 1:2, :]
        hr0 = hre_ref[:, g, :]
        hi0 = him_ref[:, g, :]
        hr = _dot(u, bb_re) + (ar * hr0 - ai * hi0)
        hi = _dot(u, bb_im) + (ar * hi0 + ai * hr0)
        ore_ref[:, g, :] = hr
        oim_ref[:, g, :] = hi
        y_ref[:, g * ch:(g + 1) * ch] = (_dot_nt(hr.astype(BF16), cre_ref[g].astype(BF16))
                                         - _dot_nt(hi.astype(BF16), cim_ref[g].astype(BF16)))


def s5_step(h, h_re, h_im, prep, c_re, c_im):
    ab, bbt = prep[6], prep[7]
    b, g, n = h_re.shape
    ch = S5_CH
    gb = S5_GB
    assert gb * ch == LANES

    def blk(*dims):
        return pl.BlockSpec((gb,) + dims, lambda i: (i,) + (0,) * len(dims))

    tok = pl.BlockSpec((b, LANES), lambda i: (0, i))
    st = pl.BlockSpec((b, gb, n), lambda i: (0, i, 0))
    return pl.pallas_call(
        _s5_step_kernel,
        grid=(g // gb,),
        in_specs=[tok, st, st, blk(2, n), blk(2 * ch, n), blk(ch, n), blk(ch, n)],
        out_specs=[tok, st, st],
        out_shape=[jax.ShapeDtypeStruct((b, g * ch), F32), jax.ShapeDtypeStruct((b, g, n), F32),
                   jax.ShapeDtypeStruct((b, g, n), F32)],
        compiler_params=_cparams(("parallel",)),
        name="s5_step",
    )(h, h_re, h_im, ab, bbt, c_re, c_im)


def _s5_head_kernel(y_ref, u_ref, d_ref, w_ref, b_ref, nw_ref, o_ref):
    y = y_ref[...] + d_ref[...] * u_ref[...]
    g = _gelu_tanh(y)
    gate = _sigmoid(_dot(g.astype(BF16), w_ref[...].astype(BF16)) + b_ref[...])
    o_ref[...] = _rms(g * gate, nw_ref[...]).astype(o_ref.dtype)


def s5_head(y, h, d, glu_w, glu_b, norm_w, *, tm):
    m, ds = y.shape
    row = lambda a: a.reshape(1, ds)
    vec = pl.BlockSpec((1, ds), lambda i: (0, 0))
    return pl.pallas_call(
        _s5_head_kernel,
        grid=(m // tm,),
        in_specs=[pl.BlockSpec((tm, ds), lambda i: (i, 0)), pl.BlockSpec((tm, ds), lambda i: (i, 0)), vec,
                  pl.BlockSpec((ds, ds), lambda i: (0, 0)), vec, vec],
        out_specs=pl.BlockSpec((tm, ds), lambda i: (i, 0)),
        out_shape=jax.ShapeDtypeStruct((m, ds), BF16),
        compiler_params=_cparams(("parallel",)),
        name="s5_head",
    )(y, h, row(d), glu_w, row(glu_b), row(norm_w))


SSD_CHUNKS_PER_STEP = 4


def _pair_select(first, col0, col1, shape):
    return jnp.where(first, jnp.broadcast_to(col0, shape), jnp.broadcast_to(col1, shape))


def _ssd_chunk_kernel(*refs, d_inner, nheads, n_xparts):
    xparts = refs[:n_xparts]
    (z_ref, dt_ref, cw_ref, cb_ref, dtb_ref, alog_ref, dvec_ref, nw_ref,
     out_ref, ssm_ref, conv_ref, state_sc, xpad_sc, y_sc) = refs[n_xparts:]
    c = pl.program_id(1)
    q = M2_CHUNK
    hp = M2_HEADDIM
    ns = M2_DSTATE
    heads_per_group = nheads // M2_NGROUPS
    halo = 8

    @pl.when(c == 0)
    def _():
        state_sc[...] = jnp.zeros_like(state_sc)
        xpad_sc[0:halo, :] = jnp.zeros((halo, xpad_sc.shape[1]), F32)

    nrows = z_ref.shape[0]
    wpart = xparts[0].shape[1]
    for i, xr in enumerate(xparts):
        xpad_sc[halo:halo + nrows, i * wpart:(i + 1) * wpart] = xr[...]
    cw = cw_ref[...]
    a = -jnp.exp(alog_ref[...])
    row = lax.broadcasted_iota(jnp.int32, (q, q), 0)
    col = lax.broadcasted_iota(jnp.int32, (q, q), 1)
    causal = row >= col
    tri = jnp.where(causal, 1.0, 0.0).astype(BF16)
    first = col < hp
    first_rows = row < hp

    for r0 in range(0, nrows, q):
        conv = cb_ref[...] + cw[M2_CONV - 1:M2_CONV, :] * xpad_sc[halo + r0:halo + r0 + q, :]
        for k in range(1, M2_CONV):
            conv = conv + cw[M2_CONV - 1 - k:M2_CONV - k, :] * xpad_sc[halo + r0 - k:halo + r0 - k + q, :]
        xc = _silu(conv)

        dt = _softplus(dt_ref[r0:r0 + q, :] + dtb_ref[...])
        da = dt * a
        d_hi, d_mid, d_lo = _split3(da)
        acum = _dot(tri, d_hi) + (_dot(tri, d_mid) + _dot(tri, d_lo))
        acum_t = acum.T
        alast = acum[q - 1:q, :]

        for pr in range(nheads // 2):
            grp = (2 * pr) // heads_per_group
            b_bf = xc[:, d_inner + grp * ns:d_inner + (grp + 1) * ns].astype(BF16)
            c_bf = xc[:, d_inner + (M2_NGROUPS + grp) * ns:d_inner + (M2_NGROUPS + grp + 1) * ns].astype(BF16)
            cb = _dot_nt(c_bf, b_bf)
            xpair = xc[:, pr * 2 * hp:(pr + 1) * 2 * hp]
            h0, h1 = 2 * pr, 2 * pr + 1
            acol = [acum[:, h:h + 1] for h in (h0, h1)]
            m = []
            for k, h in enumerate((h0, h1)):
                seg = jnp.broadcast_to(acol[k], (q, q)) - jnp.broadcast_to(acum_t[h:h + 1, :], (q, q))
                lmat = jnp.exp(jnp.where(causal, seg, -1e30))
                m.append((cb * lmat).astype(BF16))
            dtp = _pair_select(first, dt[:, h0:h0 + 1], dt[:, h1:h1 + 1], (q, q))
            xdt = xpair * dtp
            xdt_bf = xdt.astype(BF16)
            y_diag = jnp.where(first, _dot(m[0], xdt_bf), _dot(m[1], xdt_bf))
            dec_end = _pair_select(first, jnp.exp(alast[:, h0:h0 + 1] - acol[0]),
                                   jnp.exp(alast[:, h1:h1 + 1] - acol[1]), (q, q))
            xw_t = (xdt * dec_end).T.astype(BF16)
            chunk_state = _dot(xw_t, b_bf)
            rows = pl.ds(pr * 2 * hp, 2 * hp)
            prev = state_sc[rows, :]
            y_off = _dot_nt(c_bf, prev.astype(BF16)) * _pair_select(first, jnp.exp(acol[0]), jnp.exp(acol[1]),
                                                                     (q, q))
            sdec = jnp.where(first_rows, jnp.broadcast_to(jnp.exp(alast[:, h0:h0 + 1]), (q, q)),
                             jnp.broadcast_to(jnp.exp(alast[:, h1:h1 + 1]), (q, q)))
            state_sc[rows, :] = prev * sdec + chunk_state
            y_sc[:, pr * 2 * hp:(pr + 1) * 2 * hp] = (y_diag + y_off
                                                       + dvec_ref[:, pr * 2 * hp:(pr + 1) * 2 * hp] * xpair)

        out_ref[r0:r0 + q, :] = _rms(y_sc[...] * _silu(z_ref[r0:r0 + q, :]), nw_ref[...]).astype(out_ref.dtype)

    xpad_sc[0:halo, :] = xpad_sc[nrows:nrows + halo, :]

    @pl.when(c == pl.num_programs(1) - 1)
    def _():
        ssm_ref[...] = state_sc[...]
        conv_ref[...] = xpad_sc[halo + nrows - (M2_CONV - 1):halo + nrows, :]


def ssd_prompt(h, dt_raw, nb, seq, p, *, d_s5, d_inner, nheads):
    q = M2_CHUNK
    rows_step = SSD_CHUNKS_PER_STEP * q if seq % (SSD_CHUNKS_PER_STEP * q) == 0 else q
    nc = seq // rows_step
    conv_dim = d_inner + 2 * M2_NGROUPS * M2_DSTATE
    xw = 512
    xoff = d_s5 + d_inner
    assert d_s5 % d_inner == 0 and xoff % xw == 0 and conv_dim % xw == 0
    assert M2_CHUNK == 2 * M2_HEADDIM == M2_DSTATE == LANES
    zblk = d_s5 // d_inner
    n_xparts = conv_dim // xw
    m = nb * seq
    pad = lambda v: jnp.pad(v, (0, LANES - v.shape[0])).reshape(1, LANES)
    dvec = jnp.repeat(p['m2_d'], M2_HEADDIM).reshape(1, d_inner)
    vec = lambda n: pl.BlockSpec((1, n), lambda b, c: (0, 0))
    tok = lambda w, j: pl.BlockSpec((rows_step, w), lambda b, c, j=j: (b * nc + c, j))
    out, ssm, conv = pl.pallas_call(
        functools.partial(_ssd_chunk_kernel, d_inner=d_inner, nheads=nheads, n_xparts=n_xparts),
        grid=(nb, nc),
        in_specs=[tok(xw, xoff // xw + i) for i in range(n_xparts)] + [tok(d_inner, zblk), tok(LANES, 0),
                  pl.BlockSpec((M2_CONV, conv_dim), lambda b, c: (0, 0)), vec(conv_dim), vec(LANES), vec(LANES),
                  vec(d_inner), vec(d_inner)],
        out_specs=[tok(d_inner, 0),
                   pl.BlockSpec((None, nheads * M2_HEADDIM, M2_DSTATE), lambda b, c: (b, 0, 0)),
                   pl.BlockSpec((None, M2_CONV - 1, conv_dim), lambda b, c: (b, 0, 0))],
        out_shape=[jax.ShapeDtypeStruct((m, d_inner), BF16),
                   jax.ShapeDtypeStruct((nb, nheads * M2_HEADDIM, M2_DSTATE), F32),
                   jax.ShapeDtypeStruct((nb, M2_CONV - 1, conv_dim), F32)],
        scratch_shapes=[pltpu.VMEM((nheads * M2_HEADDIM, M2_DSTATE), F32),
                        pltpu.VMEM((rows_step + 8, conv_dim), F32),
                        pltpu.VMEM((q, d_inner), F32)],
        compiler_params=_cparams(("parallel", "arbitrary")),
        name="ssd_chunk",
    )(*([h] * n_xparts), h, dt_raw, p['m2_conv_w'], p['m2_conv_b'].reshape(1, conv_dim), pad(p['m2_dt_bias']),
      pad(p['m2_a_log']), dvec, p['m2_norm_w'].reshape(1, d_inner))
    return out, ssm.reshape(nb, nheads, M2_HEADDIM, M2_DSTATE), conv


SSD_STEP_SEQS = 8


def _ssd_step_kernel(*refs, d_inner, nheads, n_xparts):
    xparts = refs[:n_xparts]
    (z_ref, dt_ref, cs0_ref, cs1_ref, cs2_ref, cw_ref, cb_ref, dtb_ref, alog_ref, dvec_ref, nw_ref, st_ref,
     out_ref, so_ref, lhs_sc, bfull_sc, ct_sc, yt_sc, xs_sc) = refs[n_xparts:]
    i = pl.program_id(0)
    nb = z_ref.shape[0]
    ns = M2_DSTATE
    rows_g = (nheads // M2_NGROUPS) * M2_HEADDIM

    @pl.when(i == 0)
    def _():
        cw = cw_ref[...]
        xbc = jnp.concatenate([xr[...] for xr in xparts], axis=1)
        conv = (cb_ref[...] + cw[3:4, :] * xbc + cw[2:3, :] * cs2_ref[...]
                + cw[1:2, :] * cs1_ref[...] + cw[0:1, :] * cs0_ref[...])
        xc = _silu(conv)
        dt = _softplus(dt_ref[...] + dtb_ref[...])
        dec = jnp.exp(dt * (-jnp.exp(alog_ref[...])))
        hrow = lax.broadcasted_iota(jnp.int32, (LANES, d_inner), 0)
        hcol = lax.broadcasted_iota(jnp.int32, (LANES, d_inner), 1)
        expand = jnp.where(hcol // M2_HEADDIM == hrow, 1.0, 0.0).astype(BF16)

        def expand_heads(v):
            a, b_, c = _split3(v)
            return _dot(a, expand) + (_dot(b_, expand) + _dot(c, expand))

        xs = xc[:, :d_inner]
        xs_sc[...] = xs
        xdt_t = (xs * expand_heads(dt)).T
        d_hi, d_mid, d_lo = _split3(expand_heads(dec).T)
        for g in range(M2_NGROUPS):
            r = slice(g * rows_g, (g + 1) * rows_g)
            lhs_sc[g] = jnp.concatenate([xdt_t[r].astype(BF16), d_hi[r], d_mid[r], d_lo[r]], axis=1)
            b_g = xc[:, d_inner + g * ns:d_inner + (g + 1) * ns]
            bfull_sc[g] = jnp.concatenate([b_g, jnp.zeros_like(b_g)], axis=1)
            c_g = xc[:, d_inner + (M2_NGROUPS + g) * ns:d_inner + (M2_NGROUPS + g + 1) * ns]
            ct_sc[g] = c_g.T
        yt_sc[...] = jnp.zeros_like(yt_sc)

    row_id = lax.broadcasted_iota(jnp.int32, (nb, 2 * ns), 0)
    lane_id = lax.broadcasted_iota(jnp.int32, (nb, 2 * ns), 1)
    col_id = lax.broadcasted_iota(jnp.int32, (ns, nb), 1)
    for j in range(st_ref.shape[0]):
        b = i * st_ref.shape[0] + j
        r_bot = jnp.where((row_id == b) & (lane_id >= ns), 1.0, 0.0).astype(BF16)
        for g in range(M2_NGROUPS):
            r = pl.ds(g * rows_g, rows_g)
            r_top = jnp.where(row_id == b, bfull_sc[g], 0.0).astype(BF16)
            rhs = jnp.concatenate([r_top, r_bot, r_bot, r_bot], axis=0)
            o = _dot(lhs_sc[g], rhs)
            hnew = st_ref[j, r, :] * o[:, ns:] + o[:, :ns]
            so_ref[j, r, :] = hnew
            cm = jnp.where(col_id == b, ct_sc[g], 0.0).astype(BF16)
            yt_sc[r, :] += _dot(hnew.astype(BF16), cm)

    @pl.when(i == pl.num_programs(0) - 1)
    def _():
        y = yt_sc[...].T + dvec_ref[...] * xs_sc[...]
        out_ref[...] = _rms(y * _silu(z_ref[...]), nw_ref[...]).astype(out_ref.dtype)


def ssd_sample(h, dt_raw, state, conv_state, p, *, d_s5, d_inner, nheads):
    nb = h.shape[0]
    conv_dim = d_inner + 2 * M2_NGROUPS * M2_DSTATE
    xw = 512
    xoff = d_s5 + d_inner
    assert nb == LANES and M2_DSTATE == LANES and M2_CONV == 4
    assert xoff % xw == 0 and conv_dim % xw == 0 and d_s5 % d_inner == 0 and nb % SSD_STEP_SEQS == 0
    n_xparts = conv_dim // xw
    rows = nheads * M2_HEADDIM
    rows_g = rows // M2_NGROUPS
    pad = lambda v: jnp.pad(v, (0, LANES - v.shape[0])).reshape(1, LANES)
    dvec = jnp.repeat(p['m2_d'], M2_HEADDIM).reshape(1, d_inner)
    full = lambda a, b, j=0: pl.BlockSpec((a, b), lambda i, j=j: (0, j))
    st_spec = pl.BlockSpec((SSD_STEP_SEQS, rows, M2_DSTATE), lambda i: (i, 0, 0))
    out, new_state = pl.pallas_call(
        functools.partial(_ssd_step_kernel, d_inner=d_inner, nheads=nheads, n_xparts=n_xparts),
        grid=(nb // SSD_STEP_SEQS,),
        in_specs=[full(nb, xw, xoff // xw + k) for k in range(n_xparts)]
        + [full(nb, d_inner, d_s5 // d_inner), full(nb, LANES)]
        + [full(nb, conv_dim)] * 3
        + [full(M2_CONV, conv_dim), full(1, conv_dim), full(1, LANES), full(1, LANES), full(1, d_inner),
           full(1, d_inner), st_spec],
        out_specs=[full(nb, d_inner), st_spec],
        out_shape=[jax.ShapeDtypeStruct((nb, d_inner), BF16), jax.ShapeDtypeStruct((nb, rows, M2_DSTATE), F32)],
        scratch_shapes=[pltpu.VMEM((M2_NGROUPS, rows_g, 4 * nb), BF16),
                        pltpu.VMEM((M2_NGROUPS, nb, 2 * M2_DSTATE), F32),
                        pltpu.VMEM((M2_NGROUPS, M2_DSTATE, nb), F32),
                        pltpu.VMEM((rows, nb), F32),
                        pltpu.VMEM((nb, d_inner), F32)],
        compiler_params=_cparams(("arbitrary",)),
        name="ssd_step",
    )(*([h] * n_xparts), h, dt_raw, conv_state[:, 0], conv_state[:, 1], conv_state[:, 2],
      p['m2_conv_w'], p['m2_conv_b'].reshape(1, conv_dim), pad(p['m2_dt_bias']), pad(p['m2_a_log']),
      dvec, p['m2_norm_w'].reshape(1, d_inner), state.reshape(nb, rows, M2_DSTATE))
    xbc = lax.slice_in_dim(h, xoff, xoff + conv_dim, axis=1)
    new_conv = jnp.concatenate([conv_state[:, 1:], xbc[:, None, :]], axis=1)
    return out, new_state.reshape(state.shape), new_conv


def _softmax_rows(s):
    e = jnp.exp(s - jnp.max(s, axis=-1, keepdims=True))
    return e / jnp.sum(e, axis=-1, keepdims=True)


def _attn_kernel(q_ref, k_ref, v_ref, o_ref, *, scale):
    s = _dot_nt(q_ref[...].astype(BF16), k_ref[...].astype(BF16)) * scale
    o_ref[...] = _dot(_softmax_rows(s).astype(BF16), v_ref[...].astype(BF16)).astype(o_ref.dtype)


def attention_prompt(q, k, v, nb, seq, mem, heads, *, tq):
    d = q.shape[1]
    hd = d // heads
    nq = seq // tq
    kv_spec = pl.BlockSpec((mem, hd), lambda b, h, i: (b, h))
    q_spec = pl.BlockSpec((tq, hd), lambda b, h, i: (b * nq + i, h))
    return pl.pallas_call(
        functools.partial(_attn_kernel, scale=hd ** -0.5),
        grid=(nb, heads, nq),
        in_specs=[q_spec, kv_spec, kv_spec],
        out_specs=q_spec,
        out_shape=jax.ShapeDtypeStruct(q.shape, BF16),
        compiler_params=_cparams(("parallel", "parallel", "parallel")),
        name="attention_prompt",
    )(q, k, v)


ATTN_STEP_SEQS = 4


ATTN_STEP_ROWS = 64


def _attn_step_kernel(q_ref, k_ref, v_ref, o_ref, s_sc, *, scale):
    nseq, mem, heads, hd = k_ref.shape
    ch = ATTN_STEP_ROWS
    pack = 8 // heads
    for j in range(nseq):
        q = jnp.concatenate([q_ref[j]] * pack, axis=0)

        def score(c, mx):
            rows = pl.ds(pl.multiple_of(c * ch, ch), ch)
            k = k_ref[j, rows].reshape(ch // pack, pack * heads, hd)
            s = jnp.sum(k * q, axis=-1, keepdims=True) * scale
            s_sc[pl.ds(pl.multiple_of(c * (ch // pack), ch // pack), ch // pack)] = s
            return jnp.maximum(mx, jnp.max(s, axis=0))

        mx = lax.fori_loop(0, mem // ch, score, jnp.full((pack * heads, 1), NEG, F32))
        mx1 = mx[0:heads]
        for i in range(1, pack):
            mx1 = jnp.maximum(mx1, mx[i * heads:(i + 1) * heads])
        mx = jnp.concatenate([mx1] * pack, axis=0)

        def accum(c, carry):
            den, acc = carry
            rows = pl.ds(pl.multiple_of(c * ch, ch), ch)
            v = v_ref[j, rows].reshape(ch // pack, pack * heads, hd)
            e = jnp.exp(s_sc[pl.ds(pl.multiple_of(c * (ch // pack), ch // pack), ch // pack)] - mx)
            return den + jnp.sum(e, axis=0), acc + jnp.sum(e * v, axis=0)

        den, acc = lax.fori_loop(0, mem // ch, accum,
                                 (jnp.zeros((pack * heads, 1), F32), jnp.zeros((pack * heads, hd), F32)))
        den1, acc1 = den[0:heads], acc[0:heads]
        for i in range(1, pack):
            den1 = den1 + den[i * heads:(i + 1) * heads]
            acc1 = acc1 + acc[i * heads:(i + 1) * heads]
        o_ref[j] = acc1 / den1


def attention_sample(q, k_cache, v_cache):
    b, mem, heads, hd = k_cache.shape
    nseq = ATTN_STEP_SEQS
    q_spec = pl.BlockSpec((nseq, heads, hd), lambda i: (i, 0, 0))
    kv_spec = pl.BlockSpec((nseq, mem, heads, hd), lambda i: (i, 0, 0, 0))
    out = pl.pallas_call(
        functools.partial(_attn_step_kernel, scale=hd ** -0.5),
        grid=(b // nseq,),
        in_specs=[q_spec, kv_spec, kv_spec],
        out_specs=q_spec,
        out_shape=jax.ShapeDtypeStruct((b, heads, hd), F32),
        scratch_shapes=[pltpu.VMEM((mem * heads // 8, 8, 1), F32)],
        compiler_params=_cparams(("parallel",)),
        name="attention_step",
    )(q.reshape(b, heads, hd), k_cache, v_cache)
    return out.reshape(b, heads * hd)


NEG = -1e30


INFO_G1, INFO_G2, INFO_E1, INFO_E2 = 0, 1, 2, 3


def _lane_pack(lane, values):
    out = 0.0
    for k, v in values:
        out = jnp.where(lane == k, v, out)
    return out


def _router_kernel(x_ref, nw_ref, wr_ref, br_ref, sel_ref, info_ref, *, n_experts):
    xn = _rms(x_ref[...], nw_ref[...])
    x_hi, x_lo = _split2(xn)
    w_hi, w_lo = _split2(wr_ref[...])
    hi_terms = _dot(x_hi, jnp.concatenate([w_hi, w_lo], axis=1))
    logits = hi_terms[:, :LANES] + (hi_terms[:, LANES:] + _dot(x_lo, w_hi)) + br_ref[...]
    lane = lax.broadcasted_iota(jnp.int32, logits.shape, 1)
    big = jnp.int32(2 ** 30)
    is_c = (lane >= n_experts) & (lane < n_experts + MOE_GROUPS)
    lc = jnp.where(is_c, logits, NEG)
    cmax = jnp.max(lc, axis=-1, keepdims=True)
    gsel = jnp.min(jnp.where(lc == cmax, lane, big), axis=-1, keepdims=True) - n_experts
    gate_c = 1.0 / jnp.sum(jnp.where(is_c, jnp.exp(lc - cmax), 0.0), axis=-1, keepdims=True)
    in_group = (lane < n_experts) & (lane // MOE_PER_GROUP == gsel)
    lf = jnp.where(in_group, logits, NEG)
    t1 = jnp.max(lf, axis=-1, keepdims=True)
    i1 = jnp.min(jnp.where(lf == t1, lane, big), axis=-1, keepdims=True)
    lf2 = jnp.where(lane == i1, NEG, lf)
    t2 = jnp.max(lf2, axis=-1, keepdims=True)
    i2 = jnp.min(jnp.where(lf2 == t2, lane, big), axis=-1, keepdims=True)
    r = jnp.exp(t2 - t1)
    g1 = gate_c / (1.0 + r)
    g2 = gate_c * r / (1.0 + r)
    sel_ref[...] = jnp.where((lane == i1) | (lane == i2), 1.0, 0.0)
    info_ref[...] = _lane_pack(lane, ((INFO_G1, g1), (INFO_G2, g2), (INFO_E1, i1.astype(F32)),
                                      (INFO_E2, i2.astype(F32))))


def moe_router(x, norm_w, w_coarse, b_coarse, w_fine, b_fine, *, tm):
    m, d = x.shape
    e = w_fine.shape[1]
    padw = LANES - e - MOE_GROUPS
    wr = jnp.concatenate([w_fine, w_coarse, jnp.zeros((d, padw), F32)], axis=1)
    br = jnp.concatenate([b_fine, b_coarse, jnp.zeros((padw,), F32)]).reshape(1, LANES)
    tok = pl.BlockSpec((tm, LANES), lambda i: (i, 0))
    return pl.pallas_call(
        functools.partial(_router_kernel, n_experts=e),
        grid=(m // tm,),
        in_specs=[pl.BlockSpec((tm, d), lambda i: (i, 0)), pl.BlockSpec((1, d), lambda i: (0, 0)),
                  pl.BlockSpec((d, LANES), lambda i: (0, 0)), pl.BlockSpec((1, LANES), lambda i: (0, 0))],
        out_specs=[tok, tok],
        out_shape=[jax.ShapeDtypeStruct((m, LANES), F32), jax.ShapeDtypeStruct((m, LANES), F32)],
        compiler_params=_cparams(("parallel",)),
        name="moe_router",
    )(x, norm_w.reshape(1, d), wr, br)


def _moe_rank_kernel(sel_ref, info_ref, rank_ref, counts_ref, carry_sc):
    i = pl.program_id(0)

    @pl.when(i == 0)
    def _():
        carry_sc[...] = jnp.zeros_like(carry_sc)

    sel = sel_ref[...]
    tm = sel.shape[0]
    row = lax.broadcasted_iota(jnp.int32, (tm, tm), 0)
    col = lax.broadcasted_iota(jnp.int32, (tm, tm), 1)
    before = jnp.where(row > col, 1.0, 0.0).astype(BF16)
    rank = _dot(before, sel.astype(BF16)) + carry_sc[...]
    info = info_ref[...]
    lane = lax.broadcasted_iota(jnp.int32, sel.shape, 1)
    e1 = info[:, INFO_E1:INFO_E1 + 1].astype(jnp.int32)
    e2 = info[:, INFO_E2:INFO_E2 + 1].astype(jnp.int32)
    r1 = jnp.sum(jnp.where(lane == e1, rank, 0.0), axis=-1, keepdims=True)
    r2 = jnp.sum(jnp.where(lane == e2, rank, 0.0), axis=-1, keepdims=True)
    rank_ref[...] = _lane_pack(lane, ((0, r1), (1, r2)))
    carry_sc[...] += jnp.sum(sel, axis=0, keepdims=True)

    @pl.when(i == pl.num_programs(0) - 1)
    def _():
        counts_ref[...] = carry_sc[...]


def _moe_plan_kernel(rank_ref, info_ref, counts_ref, dest_ref, tiles_ref, *, tile, n_experts):
    counts = counts_ref[...]
    ntile_e = jnp.floor((counts + (tile - 1)) * (1.0 / tile))
    padded = jnp.broadcast_to(ntile_e * tile, (8, LANES))
    r = lax.broadcasted_iota(jnp.int32, (LANES, LANES), 0)
    c = lax.broadcasted_iota(jnp.int32, (LANES, LANES), 1)
    lower = jnp.where(r < c, 1.0, 0.0).astype(BF16)
    p_hi, p_mid, p_lo = _split3(padded)
    offs = (_dot(p_hi, lower) + (_dot(p_mid, lower) + _dot(p_lo, lower)))[0:1, :]
    ends = offs + padded[0:1, :]

    info = info_ref[...]
    rank = rank_ref[...]
    lane = lax.broadcasted_iota(jnp.int32, info.shape, 1)
    e1 = info[:, INFO_E1:INFO_E1 + 1].astype(jnp.int32)
    e2 = info[:, INFO_E2:INFO_E2 + 1].astype(jnp.int32)
    d1 = jnp.sum(jnp.where(lane == e1, offs, 0.0), axis=-1, keepdims=True) + rank[:, 0:1]
    d2 = jnp.sum(jnp.where(lane == e2, offs, 0.0), axis=-1, keepdims=True) + rank[:, 1:2]
    dest_ref[...] = _lane_pack(lane, ((0, d1), (1, d2))).astype(jnp.int32)

    ends_col = jnp.broadcast_to(ends, (LANES, LANES)).T
    start = (c * tile).astype(F32)
    n_before = jnp.sum(jnp.where((ends_col <= start) & (r < n_experts), 1.0, 0.0), axis=0, keepdims=True)
    lane1 = lax.broadcasted_iota(jnp.int32, (1, LANES), 1)
    total = jnp.sum(jnp.where(lane1 == n_experts - 1, ends, 0.0), axis=-1, keepdims=True)
    pad_start = offs + counts
    clear_from = jnp.floor(pad_start * (1.0 / MOE_CLEAR_ROWS)) * MOE_CLEAR_ROWS
    tail = jnp.where(pad_start < ends, clear_from, -1.0)
    tiles_ref[...] = jnp.zeros_like(tiles_ref)
    tiles_ref[0:1, :] = jnp.minimum(n_before, n_experts - 1.0).astype(jnp.int32)
    tiles_ref[1:2, :] = jnp.broadcast_to(total * (1.0 / tile), (1, LANES)).astype(jnp.int32)
    tiles_ref[2:3, :] = tail.astype(jnp.int32)
    tiles_ref[3:4, :] = ends.astype(jnp.int32)


def _moe_dispatch_kernel(dest_ref, tails_ref, ends_ref, ntiles_ref, *rest, tile, n_experts, group_steps,
                         group_offsets):
    TAIL_SEM, FILL_SEM, ROW_SEM = 0, 1, 2
    n_groups = len(group_offsets)
    nw_ref = rest[0]
    x_refs = rest[1:1 + n_groups]
    xs_ref, zero_sc, xn_sc, sems = rest[1 + n_groups:]
    i = pl.program_id(0)
    n_tiles_max = xs_ref.shape[0] // tile

    def fill_copy(j):
        return pltpu.make_async_copy(zero_sc, xs_ref.at[pl.ds(pl.multiple_of(j * tile, tile), tile)],
                                     sems.at[FILL_SEM])

    def tail_pieces(e, act):
        def piece(j, _):
            row = pl.multiple_of(tails_ref[e] + j * MOE_CLEAR_ROWS, MOE_CLEAR_ROWS)
            act(pltpu.make_async_copy(zero_sc.at[pl.ds(0, MOE_CLEAR_ROWS)], xs_ref.at[pl.ds(row, MOE_CLEAR_ROWS)],
                                      sems.at[TAIL_SEM]))
            return 0

        lax.fori_loop(0, (ends_ref[e] - tails_ref[e]) // MOE_CLEAR_ROWS, piece, 0)

    @pl.when(i == 0)
    def _():
        zero_sc[...] = jnp.zeros_like(zero_sc)

        def fill(j, _):
            fill_copy(j).start()
            return 0

        def clear(e, _):
            @pl.when(tails_ref[e] >= 0)
            def _():
                tail_pieces(e, lambda c: c.start())
            return 0

        def clear_wait(e, _):
            @pl.when(tails_ref[e] >= 0)
            def _():
                tail_pieces(e, lambda c: c.wait())
            return 0

        lax.fori_loop(0, n_experts, clear, 0)
        lax.fori_loop(0, n_experts, clear_wait, 0)
        lax.fori_loop(ntiles_ref[0], n_tiles_max, fill, 0)

    slot = i % 2

    def wait_rows(which, tmw):
        for _ in range(2):
            pltpu.make_async_copy(xn_sc.at[which, pl.ds(0, tmw)], xs_ref.at[pl.ds(0, tmw)],
                                  sems.at[ROW_SEM + which]).wait()

    for g, x_ref in enumerate(x_refs):
        @pl.when((i >= group_steps[g]) & (i < group_steps[g + 1]))
        def _(g=g, x_ref=x_ref):
            tm = x_ref.shape[0]
            xn_ref = xn_sc.at[slot, pl.ds(0, tm)]
            xn_ref[...] = _rms(x_ref[...], nw_ref[...])
            base = 2 * (group_offsets[g] + (i - group_steps[g]) * tm)
            for r in range(tm):
                for k in range(2):
                    pltpu.make_async_copy(xn_ref.at[pl.ds(r, 1)], xs_ref.at[pl.ds(dest_ref[base + 2 * r + k], 1)],
                                          sems.at[ROW_SEM + slot]).start(priority=k)

            @pl.when(i > group_steps[g])
            def _():
                wait_rows(1 - slot, tm)

            if g > 0:
                @pl.when(i == group_steps[g])
                def _():
                    wait_rows(1 - slot, x_refs[g - 1].shape[0])

            @pl.when(i == pl.num_programs(0) - 1)
            def _():
                wait_rows(slot, tm)

    @pl.when(i == pl.num_programs(0) - 1)
    def _():
        def fill_wait(j, _):
            fill_copy(j).wait()
            return 0

        lax.fori_loop(ntiles_ref[0], n_tiles_max, fill_wait, 0)


def _moe_expert_kernel(texp_ref, ntiles_ref, xs_ref, wg_hbm, wu_hbm, wd_hbm, ys_ref,
                       wg_sc, wu_sc, wd_sc, slot_sm, sems):
    i = pl.program_id(0)
    n = ntiles_ref[0]

    def weight_copies(e, slot):
        return [pltpu.make_async_copy(w.at[e], sc.at[slot], sems.at[slot])
                for w, sc in ((wg_hbm, wg_sc), (wu_hbm, wu_sc), (wd_hbm, wd_sc))]

    @pl.when(i < n)
    def _():
        e = texp_ref[i]
        first = (i == 0) | (texp_ref[jnp.maximum(i - 1, 0)] != e)

        @pl.when(i == 0)
        def _():
            slot_sm[0] = 0
            for c in weight_copies(e, 0):
                c.start()

        @pl.when(first & (i > 0))
        def _():
            slot_sm[0] = 1 - slot_sm[0]

        slot = slot_sm[0]

        @pl.when(first)
        def _():
            for c in weight_copies(e, slot):
                c.wait()
            j = lax.while_loop(lambda j: (j < n) & (texp_ref[jnp.minimum(j, n - 1)] == e), lambda j: j + 1, i + 1)

            @pl.when(j < n)
            def _():
                for c in weight_copies(texp_ref[jnp.minimum(j, n - 1)], 1 - slot):
                    c.start()

        x = xs_ref[...].astype(BF16)
        hg = _dot(x, wg_sc[slot].astype(BF16))
        hu = _dot(x, wu_sc[slot].astype(BF16))
        ys_ref[...] = _dot((_silu(hg) * hu).astype(BF16), wd_sc[slot].astype(BF16))


def _moe_combine_kernel(dest_ref, ys_ref, x_ref, info_ref, fw_ref, y_ref, buf_sc, sems):
    s = pl.program_id(0)
    n_blocks = pl.num_programs(0) - 1
    tm = x_ref.shape[0]
    slot = s % 2

    @pl.when(s < n_blocks)
    def _():
        for r in range(tm):
            for k in range(2):
                pltpu.make_async_copy(ys_ref.at[pl.ds(dest_ref[2 * (s * tm + r) + k], 1)],
                                      buf_sc.at[slot, k, pl.ds(r, 1)], sems.at[slot]).start(priority=k)

    @pl.when(s > 0)
    def _():
        prev = 1 - slot
        for k in range(2):
            pltpu.make_async_copy(ys_ref.at[pl.ds(0, tm)], buf_sc.at[prev, k], sems.at[prev]).wait()
        info = info_ref[...]
        y = (x_ref[...] + info[:, INFO_G1:INFO_G1 + 1] * buf_sc[prev, 0]
             + info[:, INFO_G2:INFO_G2 + 1] * buf_sc[prev, 1])
        y_ref[...] = _rms(y, fw_ref[...])


MOE_TILE = 256
MOE_CLEAR_ROWS = 32
MOE_TOKENS_PER_STEP = 256


def _largest_tile(m, cap):
    return max(t for t in range(8, cap + 1, 8) if m % t == 0)


def moe_routed_final(x_list, p, final_w):
    d = x_list[0].shape[1]
    ne, _, f = p['moe_w_gate'].shape
    sizes = [x.shape[0] for x in x_list]
    m = sum(sizes)
    tile = min(MOE_TILE, m)
    routed = [moe_router(x, p['norm_ffn_w'], p['router_coarse_w'], p['router_coarse_b'],
                         p['router_fine_w'], p['router_fine_b'], tm=min(x.shape[0], 512)) for x in x_list]
    sel = jnp.concatenate([r[0] for r in routed], axis=0)
    info = jnp.concatenate([r[1] for r in routed], axis=0)
    tm = _largest_tile(m, 1024)
    tok = pl.BlockSpec((tm, LANES), lambda i: (i, 0))
    rank, counts = pl.pallas_call(
        _moe_rank_kernel,
        grid=(m // tm,),
        in_specs=[tok, tok],
        out_specs=[tok, pl.BlockSpec((1, LANES), lambda i: (0, 0))],
        out_shape=[jax.ShapeDtypeStruct((m, LANES), F32), jax.ShapeDtypeStruct((1, LANES), F32)],
        scratch_shapes=[pltpu.VMEM((1, LANES), F32)],
        compiler_params=_cparams(("arbitrary",)),
        name="moe_rank",
    )(sel, info)
    n_tiles_max = (2 * m) // tile + ne
    assert n_tiles_max <= LANES
    dest, tiles = pl.pallas_call(
        functools.partial(_moe_plan_kernel, tile=tile, n_experts=ne),
        out_shape=[jax.ShapeDtypeStruct((m, LANES), jnp.int32), jax.ShapeDtypeStruct((8, LANES), jnp.int32)],
        compiler_params=pltpu.CompilerParams(vmem_limit_bytes=VMEM_LIMIT),
        name="moe_plan",
    )(rank, info, counts)
    dest_flat = dest[:, :2].reshape(2 * m)
    tile_expert = tiles[0, :n_tiles_max]
    n_tiles = tiles[1, :1]
    tails = tiles[2, :ne]
    seg_ends = tiles[3, :ne]

    rows = n_tiles_max * tile

    tds = [min(mg, 2 * MOE_TOKENS_PER_STEP) for mg in sizes]
    group_steps = [0]
    for mg, td in zip(sizes, tds):
        group_steps.append(group_steps[-1] + mg // td)
    group_offsets = [sum(sizes[:g]) for g in range(len(sizes))]

    def group_spec(g):
        first, last = group_steps[g], group_steps[g + 1] - 1
        return pl.BlockSpec((tds[g], d), lambda i, *_: (jnp.clip(i, first, last) - first, 0))

    xs = pl.pallas_call(
        functools.partial(_moe_dispatch_kernel, tile=tile, n_experts=ne, group_steps=tuple(group_steps),
                          group_offsets=tuple(group_offsets)),
        grid_spec=pltpu.PrefetchScalarGridSpec(
            num_scalar_prefetch=4, grid=(group_steps[-1],),
            in_specs=[pl.BlockSpec((1, d), lambda i, *_: (0, 0))] + [group_spec(g) for g in range(len(sizes))],
            out_specs=pl.BlockSpec(memory_space=pl.ANY),
            scratch_shapes=[pltpu.VMEM((tile, d), F32), pltpu.VMEM((2, max(tds), d), F32),
                            pltpu.SemaphoreType.DMA((4,))]),
        out_shape=jax.ShapeDtypeStruct((rows, d), F32),
        compiler_params=_cparams(("arbitrary",)),
        name="moe_dispatch",
    )(dest_flat, tails, seg_ends, n_tiles, p['norm_ffn_w'].reshape(1, d), *x_list)

    def tile_idx(i, te, nt):
        return jnp.minimum(i, nt[0] - 1)

    ys = pl.pallas_call(
        _moe_expert_kernel,
        grid_spec=pltpu.PrefetchScalarGridSpec(
            num_scalar_prefetch=2, grid=(n_tiles_max,),
            in_specs=[pl.BlockSpec((tile, d), lambda i, te, nt: (tile_idx(i, te, nt), 0)),
                      pl.BlockSpec(memory_space=pl.ANY), pl.BlockSpec(memory_space=pl.ANY),
                      pl.BlockSpec(memory_space=pl.ANY)],
            out_specs=pl.BlockSpec((tile, d), lambda i, te, nt: (tile_idx(i, te, nt), 0)),
            scratch_shapes=[pltpu.VMEM((2, d, f), F32), pltpu.VMEM((2, d, f), F32), pltpu.VMEM((2, f, d), F32),
                            pltpu.SMEM((1,), jnp.int32), pltpu.SemaphoreType.DMA((2,))]),
        out_shape=jax.ShapeDtypeStruct((rows, d), F32),
        input_output_aliases={2: 0},
        compiler_params=_cparams(("arbitrary",)),
        name="moe_experts",
    )(tile_expert, n_tiles, xs, p['moe_w_gate'], p['moe_w_up'], p['moe_w_down'])

    outs = []
    off = 0
    for x, (_, info_g), mg in zip(x_list, routed, sizes):
        tc = min(mg, MOE_TOKENS_PER_STEP)
        outs.append(pl.pallas_call(
            _moe_combine_kernel,
            grid_spec=pltpu.PrefetchScalarGridSpec(
                num_scalar_prefetch=1, grid=(mg // tc + 1,),
                in_specs=[pl.BlockSpec(memory_space=pl.ANY),
                          pl.BlockSpec((tc, d), lambda s, dref: (jnp.maximum(s - 1, 0), 0)),
                          pl.BlockSpec((tc, LANES), lambda s, dref: (jnp.maximum(s - 1, 0), 0)),
                          pl.BlockSpec((1, d), lambda s, dref: (0, 0))],
                out_specs=pl.BlockSpec((tc, d), lambda s, dref: (jnp.maximum(s - 1, 0), 0)),
                scratch_shapes=[pltpu.VMEM((2, 2, tc, d), F32), pltpu.SemaphoreType.DMA((2,))]),
            out_shape=jax.ShapeDtypeStruct((mg, d), F32),
            compiler_params=_cparams(("arbitrary",)),
            name="moe_combine",
        )(dest_flat[2 * off:2 * (off + mg)], ys, x, info_g, final_w.reshape(1, d)))
        off += mg
    return outs


def s5_prompt(h, nb, seq, prep, p):
    y, hfin = s5_scan(h, prep, nb, seq)
    out = s5_head(y, h, p['s5_d'], p['s5_glu_w'], p['s5_glu_b'], p['s5_norm_w'], tm=min(1024, nb * seq))
    return out, hfin[:, 0].transpose(1, 0, 2), hfin[:, 1].transpose(1, 0, 2)


def s5_sample(h, st_re, st_im, prep, p):
    y, n_re, n_im = s5_step(h, st_re, st_im, prep, p['s5_c_re'], p['s5_c_im'])
    out = s5_head(y, h, p['s5_d'], p['s5_glu_w'], p['s5_glu_b'], p['s5_norm_w'], tm=h.shape[0])
    return out, n_re, n_im


def _row_tile(m):
    return min(m, 1024)


def _col_tile(m, n):
    if m <= 256:
        return n
    return max(t for t in range(256, 1793, 256) if n % t == 0)


def _mixer_and_attention(x, p, s5_prep, *, nb, seq, mem_kv, xa_heads, states):
    m, d = x.shape
    g, n = p['s5_a_re'].shape
    d_s5 = g * S5_CH
    nheads = p['m2_a_log'].shape[0]
    d_inner = nheads * M2_HEADDIM
    conv_dim = d_inner + 2 * M2_NGROUPS * M2_DSTATE
    n_main = d_s5 + d_inner + conv_dim
    tm = _row_tile(m)

    w_in = p['w_in']
    w_dt = jnp.pad(w_in[:, n_main:], ((0, 0), (0, LANES - nheads)))
    tm2, tn2 = tm, _col_tile(m, d)
    h, dt_raw = fused_matmul([x], w_in, n_out=n_main, gain=p['norm_mix_w'], side_w=w_dt, tm=tm,
                             tn=_col_tile(m, n_main))

    if states is None:
        s5_out, s5_re, s5_im = s5_prompt(h, nb, seq, s5_prep, p)
        m2_out, ssm, conv = ssd_prompt(h, dt_raw, nb, seq, p, d_s5=d_s5, d_inner=d_inner, nheads=nheads)
    else:
        s5_out, s5_re, s5_im = s5_sample(h, states[0], states[1], s5_prep, p)
        m2_out, ssm, conv = ssd_sample(h, dt_raw, states[2], states[3], p, d_s5=d_s5, d_inner=d_inner,
                                       nheads=nheads)
    x1 = fused_matmul([s5_out, m2_out], p['w_out'], n_out=d, res=x, tm=tm2, tn=tn2)

    q = fused_matmul([x1], p['xa_wq'], n_out=d, gain=p['norm_xa_w'], tm=tm2, tn=tn2,
                     out_dtype=BF16 if states is None else F32)
    if states is None:
        mem = mem_kv[0].shape[0] // nb
        o = attention_prompt(q, mem_kv[0], mem_kv[1], nb, seq, mem, xa_heads, tq=min(seq, 2048))
    else:
        o = attention_sample(q, mem_kv[0], mem_kv[1])
    x2 = fused_matmul([o], p['xa_wo'], n_out=d, res=x1, tm=tm2, tn=tn2)

    return x2, s5_re, s5_im, ssm, conv


def kernel(x_prompt, x_sample, mem_prompt, state_s5_re, state_s5_im, state_ssm, state_conv, cache_mem_k, cache_mem_v, norm_mix_w, w_in, s5_a_re, s5_a_im, s5_log_dt, s5_b_re, s5_b_im, s5_c_re, s5_c_im, s5_d, s5_glu_w, s5_glu_b, s5_norm_w, m2_conv_w, m2_conv_b, m2_dt_bias, m2_a_log, m2_d, m2_norm_w, w_out, norm_xa_w, norm_mem_w, xa_wq, xa_wk, xa_wv, xa_wo, norm_ffn_w, router_coarse_w, router_coarse_b, router_fine_w, router_fine_b, moe_w_gate, moe_w_up, moe_w_down, norm_final_w):
    depth = w_in.shape[0]
    assert depth == 1, "the final norm is fused into the (only) layer"
    per_layer = dict(
        norm_mix_w=norm_mix_w, w_in=w_in, s5_a_re=s5_a_re, s5_a_im=s5_a_im, s5_log_dt=s5_log_dt,
        s5_b_re=s5_b_re, s5_b_im=s5_b_im, s5_c_re=s5_c_re, s5_c_im=s5_c_im, s5_d=s5_d, s5_glu_w=s5_glu_w,
        s5_glu_b=s5_glu_b, s5_norm_w=s5_norm_w, m2_conv_w=m2_conv_w, m2_conv_b=m2_conv_b, m2_dt_bias=m2_dt_bias,
        m2_a_log=m2_a_log, m2_d=m2_d, m2_norm_w=m2_norm_w, w_out=w_out, norm_xa_w=norm_xa_w,
        norm_mem_w=norm_mem_w, xa_wq=xa_wq, xa_wk=xa_wk, xa_wv=xa_wv, xa_wo=xa_wo, norm_ffn_w=norm_ffn_w,
        router_coarse_w=router_coarse_w, router_coarse_b=router_coarse_b, router_fine_w=router_fine_w,
        router_fine_b=router_fine_b, moe_w_gate=moe_w_gate, moe_w_up=moe_w_up, moe_w_down=moe_w_down)
    p = {k: v[0] for k, v in per_layer.items()}
    for name in ('w_in', 'w_out', 'xa_wq', 'xa_wo'):
        p[name] = p[name].astype(BF16)
    nb, seq, d = x_prompt.shape
    db, dseq, _ = x_sample.shape
    assert dseq == 1
    mem = mem_prompt.shape[1]
    xa_heads = cache_mem_k.shape[3]

    s5_prep = s5_prepare(p['s5_a_re'], p['s5_a_im'], p['s5_log_dt'], p['s5_b_re'], p['s5_b_im'],
                         p['s5_c_re'], p['s5_c_im'])

    memx = mem_prompt.reshape(nb * mem, d)
    mk, mv = memory_kv(memx, p['norm_mem_w'], p['xa_wk'], p['xa_wv'], tm=_row_tile(nb * mem), tn=512)
    xp, p_re, p_im, p_ssm, p_conv = _mixer_and_attention(
        x_prompt.reshape(nb * seq, d), p, s5_prep, nb=nb, seq=seq, mem_kv=(mk, mv), xa_heads=xa_heads, states=None)

    xs, s_re, s_im, s_ssm, s_conv = _mixer_and_attention(
        x_sample.reshape(db, d), p, s5_prep, nb=db, seq=1,
        mem_kv=(cache_mem_k[0], cache_mem_v[0]), xa_heads=xa_heads,
        states=(state_s5_re[0], state_s5_im[0], state_ssm[0], state_conv[0]))

    yp, ys = moe_routed_final([xp, xs], p, norm_final_w)

    kv_shape = (1, nb, mem) + cache_mem_k.shape[3:]
    return (yp.reshape(nb, seq, d), ys.reshape(db, 1, d), p_re[None], p_im[None], p_ssm[None], p_conv[None],
            mk.reshape(kv_shape), mv.reshape(kv_shape), s_re[None], s_im[None], s_ssm[None], s_conv[None])
```

```python
import functools
import math

import jax
import jax.numpy as jnp
from jax import lax
from jax.experimental import pallas as pl
from jax.experimental.pallas import tpu as pltpu

F32 = jnp.float32
BF16 = jnp.bfloat16
RMS_EPS = 1e-6

V7X_VMEM_BYTES = 64 * 1024 * 1024
VMEM_LIMIT = V7X_VMEM_BYTES - 8 * 1024 * 1024
LANES = 128

S5_CH = 16
S5_N = 64
S5_Q = 16
S5_GB = 8
S5_SEQ_PAD = 8
M2_HEADDIM = 64
M2_DSTATE = 128
M2_NGROUPS = 2
M2_CONV = 4
M2_CHUNK = 128
MOE_GROUPS = 4
MOE_PER_GROUP = 8


def _cparams(sem):
    return pltpu.CompilerParams(dimension_semantics=sem, vmem_limit_bytes=VMEM_LIMIT)


def _rms(x, w):
    return x * lax.rsqrt(jnp.mean(x * x, axis=-1, keepdims=True) + RMS_EPS) * w


def _sigmoid(x):
    return 1.0 / (1.0 + jnp.exp(-x))


def _silu(x):
    return x * _sigmoid(x)


def _softplus(x):
    return jnp.maximum(x, 0.0) + jnp.log1p(jnp.exp(-jnp.abs(x)))


def _gelu_tanh(x):
    return 0.5 * x * (1.0 + jnp.tanh(math.sqrt(2.0 / math.pi) * (x + 0.044715 * (x * x * x))))


def _dot(a, b):
    return jnp.dot(a, b, preferred_element_type=F32)


def _dot_nt(a, b):
    return lax.dot_general(a, b, (((1,), (1,)), ((), ())), preferred_element_type=F32)


def _split3(x):
    hi = x.astype(BF16)
    r = x - hi.astype(F32)
    mid = r.astype(BF16)
    lo = (r - mid.astype(F32)).astype(BF16)
    return hi, mid, lo


def _split2(x):
    hi = x.astype(BF16)
    lo = (x - hi.astype(F32)).astype(BF16)
    return hi, lo


def _mm_kernel(*refs, n_lhs, has_gain, has_res, has_side, staged):
    it = iter(refs)
    lhs = [next(it) for _ in range(n_lhs)]
    gain = next(it) if has_gain else None
    ws = [next(it) for _ in range(n_lhs)]
    side_w = next(it) if has_side else None
    res = next(it) if has_res else None
    out = next(it)
    side_out = next(it) if has_side else None
    lhs_bf = next(it) if staged else lhs

    if staged:
        @pl.when(pl.program_id(1) == 0)
        def _():
            for i in range(n_lhs):
                x = lhs[i][...]
                if has_gain:
                    x = _rms(x, gain[...])
                lhs_bf[i] = x.astype(BF16)
            if has_side:
                side_out[...] = _dot(lhs_bf[0], side_w[...].astype(BF16))

    acc = None
    for i in range(n_lhs):
        p = _dot(lhs_bf[i][...], ws[i][...].astype(BF16))
        acc = p if acc is None else acc + p
    if has_res:
        acc = acc + res[...]
    out[...] = acc.astype(out.dtype)


def fused_matmul(lhs_list, w, *, n_out, gain=None, res=None, side_w=None, out_dtype=F32, tm, tn):
    n_lhs = len(lhs_list)
    m, kp = lhs_list[0].shape
    assert all(a.shape == (m, kp) for a in lhs_list)
    assert w.shape[0] == n_lhs * kp and m % tm == 0 and n_out % tn == 0
    assert gain is None or n_lhs == 1
    staged = gain is not None or any(a.dtype != BF16 for a in lhs_list)
    assert staged or side_w is None
    grid = (m // tm, n_out // tn)
    in_specs = [pl.BlockSpec((tm, kp), lambda i, j: (i, 0)) for _ in range(n_lhs)]
    args = list(lhs_list)
    if gain is not None:
        in_specs.append(pl.BlockSpec((1, kp), lambda i, j: (0, 0)))
        args.append(gain.reshape(1, kp))
    for p in range(n_lhs):
        in_specs.append(pl.BlockSpec((kp, tn), lambda i, j, p=p: (p, j)))
        args.append(w)
    if side_w is not None:
        in_specs.append(pl.BlockSpec((kp, LANES), lambda i, j: (0, 0)))
        args.append(side_w)
    if res is not None:
        in_specs.append(pl.BlockSpec((tm, tn), lambda i, j: (i, j)))
        args.append(res)
    out_shape = [jax.ShapeDtypeStruct((m, n_out), out_dtype)]
    out_specs = [pl.BlockSpec((tm, tn), lambda i, j: (i, j))]
    if side_w is not None:
        out_shape.append(jax.ShapeDtypeStruct((m, LANES), F32))
        out_specs.append(pl.BlockSpec((tm, LANES), lambda i, j: (i, 0)))
    outs = pl.pallas_call(
        functools.partial(_mm_kernel, n_lhs=n_lhs, has_gain=gain is not None,
                          has_res=res is not None, has_side=side_w is not None, staged=staged),
        grid=grid, in_specs=in_specs, out_specs=out_specs, out_shape=out_shape,
        scratch_shapes=[pltpu.VMEM((n_lhs, tm, kp), BF16)] if staged else [],
        compiler_params=_cparams(("parallel", "arbitrary")),
        name="fused_matmul",
    )(*args)
    return outs if side_w is not None else outs[0]


def _memory_kv_kernel(x_ref, g_ref, wk_ref, wv_ref, k_ref, v_ref, lhs_sc, *, ntn):
    j = pl.program_id(1)

    @pl.when(j == 0)
    def _():
        lhs_sc[...] = _rms(x_ref[...], g_ref[...]).astype(BF16)

    @pl.when(j < ntn)
    def _():
        k_ref[...] = _dot(lhs_sc[...], wk_ref[...].astype(BF16))

    @pl.when(j >= ntn)
    def _():
        v_ref[...] = _dot(lhs_sc[...], wv_ref[...].astype(BF16))


def memory_kv(x, gain, wk, wv, *, tm, tn):
    m, d = x.shape
    n = wk.shape[1]
    assert wk.shape == wv.shape == (d, n) and m % tm == 0 and n % tn == 0
    ntn = n // tn
    first = lambda j: jnp.minimum(j, ntn - 1)
    second = lambda j: jnp.maximum(j - ntn, 0)
    return pl.pallas_call(
        functools.partial(_memory_kv_kernel, ntn=ntn),
        grid=(m // tm, 2 * ntn),
        in_specs=[pl.BlockSpec((tm, d), lambda i, j: (i, 0)), pl.BlockSpec((1, d), lambda i, j: (0, 0)),
                  pl.BlockSpec((d, tn), lambda i, j: (0, first(j))), pl.BlockSpec((d, tn), lambda i, j: (0, second(j)))],
        out_specs=[pl.BlockSpec((tm, tn), lambda i, j: (i, first(j))),
                   pl.BlockSpec((tm, tn), lambda i, j: (i, second(j)))],
        out_shape=[jax.ShapeDtypeStruct((m, n), F32), jax.ShapeDtypeStruct((m, n), F32)],
        scratch_shapes=[pltpu.VMEM((tm, d), BF16)],
        compiler_params=_cparams(("parallel", "arbitrary")),
        name="memory_kv",
    )(x, gain.reshape(1, d), wk, wv)


def _s5_prep_kernel(*refs):
    for g in range(refs[0].shape[0]):
        _s5_prep_group(*[r.at[g] for r in refs])


def _s5_prep_group(lre_ref, lim_ref, ldt_ref, btre_ref, btim_ref, cre_ref, cim_ref,
                   tz_ref, wsre_ref, wsim_ref, wcre_ref, wcim_ref, aq_ref, ab_ref, bbt_ref):
    q, ch = S5_Q, S5_CH
    lr = lre_ref[...]
    li = lim_ref[...]
    step = jnp.exp(ldt_ref[...])
    mag = jnp.exp(lr * step)
    ab_re = mag * jnp.cos(li * step)
    ab_im = mag * jnp.sin(li * step)
    den = lr * lr + li * li
    num_re = ab_re - 1.0
    coef_re = (num_re * lr + ab_im * li) / den
    coef_im = (ab_im * lr - num_re * li) / den
    bt_re = btre_ref[...]
    bt_im = btim_ref[...]
    bb_re = coef_re * bt_re - coef_im * bt_im
    bb_im = coef_re * bt_im + coef_im * bt_re
    c_re = cre_ref[...]
    c_im = cim_ref[...]

    pw = [(jnp.ones_like(ab_re), jnp.zeros_like(ab_re))]
    for _ in range(q):
        pr, pi = pw[-1]
        pw.append((pr * ab_re - pi * ab_im, pr * ab_im + pi * ab_re))

    ca_re = [c_re * pr - c_im * pi for pr, pi in pw]
    ca_im = [c_re * pi + c_im * pr for pr, pi in pw]
    wcre_ref[...] = jnp.concatenate(ca_re[1:], axis=0).astype(BF16)
    wcim_ref[...] = jnp.concatenate([-x for x in ca_im[1:]], axis=0).astype(BF16)

    pr_stack = jnp.concatenate(ca_re[:q], axis=0)
    pi_stack = jnp.concatenate(ca_im[:q], axis=0)
    krow = None
    for a, b, sign in ((bb_re, pr_stack, 1.0), (bb_im, pi_stack, -1.0)):
        a_hi, a_lo = _split2(a)
        b_hi, b_lo = _split2(b)
        t = _dot_nt(a_hi, b_hi) + (_dot_nt(a_hi, b_lo) + _dot_nt(a_lo, b_hi))
        krow = sign * t if krow is None else krow + sign * t
    lane = lax.broadcasted_iota(jnp.int32, krow.shape, 1)
    blocks = [krow]
    for s in range(1, q):
        blocks.append(jnp.where(lane >= s * ch, pltpu.roll(krow, s * ch, 1), 0.0))
    tz_ref[...] = jnp.concatenate(blocks, axis=0).astype(BF16)

    ws_re, ws_im = [], []
    for s in range(q):
        pr, pi = pw[q - 1 - s]
        ws_re.append(bb_re * pr - bb_im * pi)
        ws_im.append(bb_re * pi + bb_im * pr)
    wsre_ref[...] = jnp.concatenate(ws_re, axis=0).astype(BF16)
    wsim_ref[...] = jnp.concatenate(ws_im, axis=0).astype(BF16)

    aq_ref[0:1, :] = pw[q][0]
    aq_ref[1:2, :] = pw[q][1]
    ab_ref[0:1, :] = ab_re
    ab_ref[1:2, :] = ab_im
    bbt_ref[0:ch, :] = bb_re
    bbt_ref[ch:2 * ch, :] = bb_im


def s5_prepare(a_re, a_im, log_dt, b_re, b_im, c_re, c_im):
    g, n = a_re.shape
    ch, q = S5_CH, S5_Q
    qc = q * ch
    bt_re = jnp.swapaxes(b_re, 1, 2)
    bt_im = jnp.swapaxes(b_im, 1, 2)

    def per_g(*dims):
        return pl.BlockSpec((S5_GB,) + dims, lambda i: (i,) + (0,) * len(dims))

    return pl.pallas_call(
        _s5_prep_kernel,
        grid=(g // S5_GB,),
        in_specs=[per_g(1, n), per_g(1, n), per_g(1, 1), per_g(ch, n), per_g(ch, n), per_g(ch, n), per_g(ch, n)],
        out_specs=[per_g(qc, qc), per_g(qc, n), per_g(qc, n), per_g(qc, n), per_g(qc, n),
                   per_g(2, n), per_g(2, n), per_g(2 * ch, n)],
        out_shape=[jax.ShapeDtypeStruct((g, qc, qc), BF16),
                   jax.ShapeDtypeStruct((g, qc, n), BF16), jax.ShapeDtypeStruct((g, qc, n), BF16),
                   jax.ShapeDtypeStruct((g, qc, n), BF16), jax.ShapeDtypeStruct((g, qc, n), BF16),
                   jax.ShapeDtypeStruct((g, 2, n), F32), jax.ShapeDtypeStruct((g, 2, n), F32),
                   jax.ShapeDtypeStruct((g, 2 * ch, n), F32)],
        compiler_params=_cparams(("parallel",)),
        name="s5_prepare",
    )(a_re.reshape(g, 1, n), a_im.reshape(g, 1, n), log_dt.reshape(g, 1, 1), bt_re, bt_im, c_re, c_im)


def _s5_scan_kernel(h_ref, tz_ref, wsre_ref, wsim_ref, wcre_ref, wcim_ref, aq_ref,
                    y_ref, hfin_ref, xs_sc, u_sc, yg_sc, sre_sc, sim_sc, *, nb, nchunk):
    gb, q, ch = S5_GB, S5_Q, S5_CH
    rows = nb * nchunk
    seq_stride = nchunk + S5_SEQ_PAD
    per_vreg = LANES // ch
    assert gb == per_vreg and q % per_vreg == 0
    slot = lax.broadcasted_iota(jnp.int32, (rows, LANES), 1) // ch

    halves = q // per_vreg

    def rot_rows(w, g):
        if g == 0:
            return w
        cut = (per_vreg - g) * ch
        parts = []
        for hf in range(halves):
            blk = w[hf * LANES:(hf + 1) * LANES]
            parts += [blk[cut:], blk[:cut]]
        return jnp.concatenate(parts, axis=0)

    for s in range(q):
        x = h_ref[pl.ds(s, rows, stride=q), :].astype(BF16)
        k = s % per_vreg
        xs_sc[s] = pltpu.roll(x, k * ch, 1) if k else x
    keep = [jnp.where(slot == j, 1.0, 0.0).astype(BF16) for j in range(per_vreg)]
    for g in range(gb):
        for hf in range(halves):
            acc = None
            for k in range(per_vreg):
                piece = xs_sc[hf * per_vreg + k] * keep[(g + k) % per_vreg]
                acc = piece if acc is None else acc + piece
            u_sc[g, :, hf * LANES:(hf + 1) * LANES] = acc

    for g in range(gb):
        u = u_sc[g]
        tz = rot_rows(tz_ref[g], g)
        tz = jnp.concatenate([pltpu.roll(tz[:, hf * LANES:(hf + 1) * LANES], g * ch, 1) if g
                              else tz[:, hf * LANES:(hf + 1) * LANES] for hf in range(halves)], axis=1)
        yg_sc[g] = _dot(u, tz)
        for sc, w_ref in ((sre_sc, wsre_ref), (sim_sc, wsim_ref)):
            s_all = _dot(u, rot_rows(w_ref[g], g))
            for b in range(nb):
                sc[g, b * seq_stride:b * seq_stride + nchunk, :] = s_all[b * nchunk:(b + 1) * nchunk]

    ar = [jnp.broadcast_to(aq_ref[g, 0:1, :], (nb, S5_N)) for g in range(gb)]
    ai = [jnp.broadcast_to(aq_ref[g, 1:2, :], (nb, S5_N)) for g in range(gb)]

    def step(c, carry):
        at = pl.ds(c, nb, stride=seq_stride)
        new = []
        for g in range(gb):
            hr, hi = carry[g]
            sr = sre_sc[g, at, :]
            si = sim_sc[g, at, :]
            sre_sc[g, at, :] = hr
            sim_sc[g, at, :] = hi
            new.append((ar[g] * hr - ai[g] * hi + sr, ar[g] * hi + ai[g] * hr + si))
        return tuple(new)

    zero = jnp.zeros((nb, S5_N), F32)
    fin = lax.fori_loop(0, nchunk, step, tuple((zero, zero) for _ in range(gb)), unroll=4)
    for g in range(gb):
        hfin_ref[g, 0] = fin[g][0]
        hfin_ref[g, 1] = fin[g][1]
        h_in = [jnp.concatenate([sc[g, b * seq_stride:b * seq_stride + nchunk, :] for b in range(nb)], axis=0)
                for sc in (sre_sc, sim_sc)]
        yg_sc[g] += (_dot_nt(h_in[0].astype(BF16), rot_rows(wcre_ref[g], g))
                     + _dot_nt(h_in[1].astype(BF16), rot_rows(wcim_ref[g], g)))

    for t in range(q):
        hf, k = divmod(t, per_vreg)
        acc = jnp.zeros((rows, LANES), F32)
        for g in range(gb):
            acc = jnp.where(slot == (k + g) % per_vreg, yg_sc[g, :, hf * LANES:(hf + 1) * LANES], acc)
        y_ref[pl.ds(t, rows, stride=q), :] = pltpu.roll(acc, (per_vreg - k) * ch, 1) if k else acc


def s5_scan(h, prep, nb, seq):
    tz, ws_re, ws_im, wc_re, wc_im, aq = prep[:6]
    g, qc, _ = tz.shape
    n = S5_N
    gb = S5_GB
    q = S5_Q
    assert gb * S5_CH == LANES and qc == q * S5_CH and seq % q == 0
    nchunk = seq // q
    rows = nb * nchunk
    m = nb * seq

    def blk(*dims):
        return pl.BlockSpec((gb,) + dims, lambda i: (i,) + (0,) * len(dims))

    return pl.pallas_call(
        functools.partial(_s5_scan_kernel, nb=nb, nchunk=nchunk),
        grid=(g // gb,),
        in_specs=[pl.BlockSpec((m, LANES), lambda i: (0, i)),
                  blk(qc, qc), blk(qc, n), blk(qc, n), blk(qc, n), blk(qc, n), blk(2, n)],
        out_specs=[pl.BlockSpec((m, LANES), lambda i: (0, i)), blk(2, nb, n)],
        out_shape=[jax.ShapeDtypeStruct((m, g * S5_CH), F32), jax.ShapeDtypeStruct((g, 2, nb, n), F32)],
        scratch_shapes=[pltpu.VMEM((q, rows, LANES), BF16), pltpu.VMEM((gb, rows, qc), BF16),
                        pltpu.VMEM((gb, rows, qc), F32),
                        pltpu.VMEM((gb, nb * (nchunk + S5_SEQ_PAD), n), F32),
                        pltpu.VMEM((gb, nb * (nchunk + S5_SEQ_PAD), n), F32)],
        compiler_params=_cparams(("parallel",)),
        name="s5_scan",
    )(h, tz, ws_re, ws_im, wc_re, wc_im, aq)


def _s5_step_kernel(u_ref, hre_ref, him_ref, ab_ref, bbt_ref, cre_ref, cim_ref, y_ref, ore_ref, oim_ref):
    gb = hre_ref.shape[1]
    ch = S5_CH
    for g in range(gb):
        u = u_ref[:, g * ch:(g + 1) * ch].astype(BF16)
        bb_re = bbt_ref[g, 0:ch, :].astype(BF16)
        bb_im = bbt_ref[g, ch:2 * ch, :].astype(BF16)
        ar = ab_ref[g, 0:1, :]
        ai = ab_ref[g, 1:2, :]
        hr0 = hre_ref[:, g, :]
        hi0 = him_ref[:, g, :]
        hr = _dot(u, bb_re) + (ar * hr0 - ai * hi0)
        hi = _dot(u, bb_im) + (ar * hi0 + ai * hr0)
        ore_ref[:, g, :] = hr
        oim_ref[:, g, :] = hi
        y_ref[:, g * ch:(g + 1) * ch] = (_dot_nt(hr.astype(BF16), cre_ref[g].astype(BF16))
                                         - _dot_nt(hi.astype(BF16), cim_ref[g].astype(BF16)))


def s5_step(h, h_re, h_im, prep, c_re, c_im):
    ab, bbt = prep[6], prep[7]
    b, g, n = h_re.shape
    ch = S5_CH
    gb = S5_GB
    assert gb * ch == LANES

    def blk(*dims):
        return pl.BlockSpec((gb,) + dims, lambda i: (i,) + (0,) * len(dims))

    tok = pl.BlockSpec((b, LANES), lambda i: (0, i))
    st = pl.BlockSpec((b, gb, n), lambda i: (0, i, 0))
    return pl.pallas_call(
        _s5_step_kernel,
        grid=(g // gb,),
        in_specs=[tok, st, st, blk(2, n), blk(2 * ch, n), blk(ch, n), blk(ch, n)],
        out_specs=[tok, st, st],
        out_shape=[jax.ShapeDtypeStruct((b, g * ch), F32), jax.ShapeDtypeStruct((b, g, n), F32),
                   jax.ShapeDtypeStruct((b, g, n), F32)],
        compiler_params=_cparams(("parallel",)),
        name="s5_step",
    )(h, h_re, h_im, ab, bbt, c_re, c_im)


def _s5_head_kernel(y_ref, u_ref, d_ref, w_ref, b_ref, nw_ref, o_ref):
    y = y_ref[...] + d_ref[...] * u_ref[...]
    g = _gelu_tanh(y)
    gate = _sigmoid(_dot(g.astype(BF16), w_ref[...].astype(BF16)) + b_ref[...])
    o_ref[...] = _rms(g * gate, nw_ref[...]).astype(o_ref.dtype)


def s5_head(y, h, d, glu_w, glu_b, norm_w, *, tm):
    m, ds = y.shape
    row = lambda a: a.reshape(1, ds)
    vec = pl.BlockSpec((1, ds), lambda i: (0, 0))
    return pl.pallas_call(
        _s5_head_kernel,
        grid=(m // tm,),
        in_specs=[pl.BlockSpec((tm, ds), lambda i: (i, 0)), pl.BlockSpec((tm, ds), lambda i: (i, 0)), vec,
                  pl.BlockSpec((ds, ds), lambda i: (0, 0)), vec, vec],
        out_specs=pl.BlockSpec((tm, ds), lambda i: (i, 0)),
        out_shape=jax.ShapeDtypeStruct((m, ds), BF16),
        compiler_params=_cparams(("parallel",)),
        name="s5_head",
    )(y, h, row(d), glu_w, row(glu_b), row(norm_w))


SSD_CHUNKS_PER_STEP = 4


def _pair_select(first, col0, col1, shape):
    return jnp.where(first, jnp.broadcast_to(col0, shape), jnp.broadcast_to(col1, shape))


def _ssd_chunk_kernel(*refs, d_inner, nheads, n_xparts):
    xparts = refs[:n_xparts]
    (z_ref, dt_ref, cw_ref, cb_ref, dtb_ref, alog_ref, dvec_ref, nw_ref,
     out_ref, ssm_ref, conv_ref, state_sc, xpad_sc, y_sc) = refs[n_xparts:]
    c = pl.program_id(1)
    q = M2_CHUNK
    hp = M2_HEADDIM
    ns = M2_DSTATE
    heads_per_group = nheads // M2_NGROUPS
    halo = 8

    @pl.when(c == 0)
    def _():
        state_sc[...] = jnp.zeros_like(state_sc)
        xpad_sc[0:halo, :] = jnp.zeros((halo, xpad_sc.shape[1]), F32)

    nrows = z_ref.shape[0]
    wpart = xparts[0].shape[1]
    for i, xr in enumerate(xparts):
        xpad_sc[halo:halo + nrows, i * wpart:(i + 1) * wpart] = xr[...]
    cw = cw_ref[...]
    a = -jnp.exp(alog_ref[...])
    row = lax.broadcasted_iota(jnp.int32, (q, q), 0)
    col = lax.broadcasted_iota(jnp.int32, (q, q), 1)
    causal = row >= col
    tri = jnp.where(causal, 1.0, 0.0).astype(BF16)
    first = col < hp
    first_rows = row < hp

    for r0 in range(0, nrows, q):
        conv = cb_ref[...] + cw[M2_CONV - 1:M2_CONV, :] * xpad_sc[halo + r0:halo + r0 + q, :]
        for k in range(1, M2_CONV):
            conv = conv + cw[M2_CONV - 1 - k:M2_CONV - k, :] * xpad_sc[halo + r0 - k:halo + r0 - k + q, :]
        xc = _silu(conv)

        dt = _softplus(dt_ref[r0:r0 + q, :] + dtb_ref[...])
        da = dt * a
        d_hi, d_mid, d_lo = _split3(da)
        acum = _dot(tri, d_hi) + (_dot(tri, d_mid) + _dot(tri, d_lo))
        acum_t = acum.T
        alast = acum[q - 1:q, :]

        for pr in range(nheads // 2):
            grp = (2 * pr) // heads_per_group
            b_bf = xc[:, d_inner + grp * ns:d_inner + (grp + 1) * ns].astype(BF16)
            c_bf = xc[:, d_inner + (M2_NGROUPS + grp) * ns:d_inner + (M2_NGROUPS + grp + 1) * ns].astype(BF16)
            cb = _dot_nt(c_bf, b_bf)
            xpair = xc[:, pr * 2 * hp:(pr + 1) * 2 * hp]
            h0, h1 = 2 * pr, 2 * pr + 1
            acol = [acum[:, h:h + 1] for h in (h0, h1)]
            m = []
            for k, h in enumerate((h0, h1)):
                seg = jnp.broadcast_to(acol[k], (q, q)) - jnp.broadcast_to(acum_t[h:h + 1, :], (q, q))
                lmat = jnp.exp(jnp.where(causal, seg, -1e30))
                m.append((cb * lmat).astype(BF16))
            dtp = _pair_select(first, dt[:, h0:h0 + 1], dt[:, h1:h1 + 1], (q, q))
            xdt = xpair * dtp
            xdt_bf = xdt.astype(BF16)
            y_diag = jnp.where(first, _dot(m[0], xdt_bf), _dot(m[1], xdt_bf))
            dec_end = _pair_select(first, jnp.exp(alast[:, h0:h0 + 1] - acol[0]),
                                   jnp.exp(alast[:, h1:h1 + 1] - acol[1]), (q, q))
            xw_t = (xdt * dec_end).T.astype(BF16)
            chunk_state = _dot(xw_t, b_bf)
            rows = pl.ds(pr * 2 * hp, 2 * hp)
            prev = state_sc[rows, :]
            y_off = _dot_nt(c_bf, prev.astype(BF16)) * _pair_select(first, jnp.exp(acol[0]), jnp.exp(acol[1]),
                                                                     (q, q))
            sdec = jnp.where(first_rows, jnp.broadcast_to(jnp.exp(alast[:, h0:h0 + 1]), (q, q)),
                             jnp.broadcast_to(jnp.exp(alast[:, h1:h1 + 1]), (q, q)))
            state_sc[rows, :] = prev * sdec + chunk_state
            y_sc[:, pr * 2 * hp:(pr + 1) * 2 * hp] = (y_diag + y_off
                                                       + dvec_ref[:, pr * 2 * hp:(pr + 1) * 2 * hp] * xpair)

        out_ref[r0:r0 + q, :] = _rms(y_sc[...] * _silu(z_ref[r0:r0 + q, :]), nw_ref[...]).astype(out_ref.dtype)

    xpad_sc[0:halo, :] = xpad_sc[nrows:nrows + halo, :]

    @pl.when(c == pl.num_programs(1) - 1)
    def _():
        ssm_ref[...] = state_sc[...]
        conv_ref[...] = xpad_sc[halo + nrows - (M2_CONV - 1):halo + nrows, :]


def ssd_prompt(h, dt_raw, nb, seq, p, *, d_s5, d_inner, nheads):
    q = M2_CHUNK
    rows_step = SSD_CHUNKS_PER_STEP * q if seq % (SSD_CHUNKS_PER_STEP * q) == 0 else q
    nc = seq // rows_step
    conv_dim = d_inner + 2 * M2_NGROUPS * M2_DSTATE
    xw = 512
    xoff = d_s5 + d_inner
    assert d_s5 % d_inner == 0 and xoff % xw == 0 and conv_dim % xw == 0
    assert M2_CHUNK == 2 * M2_HEADDIM == M2_DSTATE == LANES
    zblk = d_s5 // d_inner
    n_xparts = conv_dim // xw
    m = nb * seq
    pad = lambda v: jnp.pad(v, (0, LANES - v.shape[0])).reshape(1, LANES)
    dvec = jnp.repeat(p['m2_d'], M2_HEADDIM).reshape(1, d_inner)
    vec = lambda n: pl.BlockSpec((1, n), lambda b, c: (0, 0))
    tok = lambda w, j: pl.BlockSpec((rows_step, w), lambda b, c, j=j: (b * nc + c, j))
    out, ssm, conv = pl.pallas_call(
        functools.partial(_ssd_chunk_kernel, d_inner=d_inner, nheads=nheads, n_xparts=n_xparts),
        grid=(nb, nc),
        in_specs=[tok(xw, xoff // xw + i) for i in range(n_xparts)] + [tok(d_inner, zblk), tok(LANES, 0),
                  pl.BlockSpec((M2_CONV, conv_dim), lambda b, c: (0, 0)), vec(conv_dim), vec(LANES), vec(LANES),
                  vec(d_inner), vec(d_inner)],
        out_specs=[tok(d_inner, 0),
                   pl.BlockSpec((None, nheads * M2_HEADDIM, M2_DSTATE), lambda b, c: (b, 0, 0)),
                   pl.BlockSpec((None, M2_CONV - 1, conv_dim), lambda b, c: (b, 0, 0))],
        out_shape=[jax.ShapeDtypeStruct((m, d_inner), BF16),
                   jax.ShapeDtypeStruct((nb, nheads * M2_HEADDIM, M2_DSTATE), F32),
                   jax.ShapeDtypeStruct((nb, M2_CONV - 1, conv_dim), F32)],
        scratch_shapes=[pltpu.VMEM((nheads * M2_HEADDIM, M2_DSTATE), F32),
                        pltpu.VMEM((rows_step + 8, conv_dim), F32),
                        pltpu.VMEM((q, d_inner), F32)],
        compiler_params=_cparams(("parallel", "arbitrary")),
        name="ssd_chunk",
    )(*([h] * n_xparts), h, dt_raw, p['m2_conv_w'], p['m2_conv_b'].reshape(1, conv_dim), pad(p['m2_dt_bias']),
      pad(p['m2_a_log']), dvec, p['m2_norm_w'].reshape(1, d_inner))
    return out, ssm.reshape(nb, nheads, M2_HEADDIM, M2_DSTATE), conv


SSD_STEP_SEQS = 8


def _ssd_step_kernel(*refs, d_inner, nheads, n_xparts):
    xparts = refs[:n_xparts]
    (z_ref, dt_ref, cs0_ref, cs1_ref, cs2_ref, cw_ref, cb_ref, dtb_ref, alog_ref, dvec_ref, nw_ref, st_ref,
     out_ref, so_ref, lhs_sc, bfull_sc, ct_sc, yt_sc, xs_sc) = refs[n_xparts:]
    i = pl.program_id(0)
    nb = z_ref.shape[0]
    ns = M2_DSTATE
    rows_g = (nheads // M2_NGROUPS) * M2_HEADDIM

    @pl.when(i == 0)
    def _():
        cw = cw_ref[...]
        xbc = jnp.concatenate([xr[...] for xr in xparts], axis=1)
        conv = (cb_ref[...] + cw[3:4, :] * xbc + cw[2:3, :] * cs2_ref[...]
                + cw[1:2, :] * cs1_ref[...] + cw[0:1, :] * cs0_ref[...])
        xc = _silu(conv)
        dt = _softplus(dt_ref[...] + dtb_ref[...])
        dec = jnp.exp(dt * (-jnp.exp(alog_ref[...])))
        hrow = lax.broadcasted_iota(jnp.int32, (LANES, d_inner), 0)
        hcol = lax.broadcasted_iota(jnp.int32, (LANES, d_inner), 1)
        expand = jnp.where(hcol // M2_HEADDIM == hrow, 1.0, 0.0).astype(BF16)

        def expand_heads(v):
            a, b_, c = _split3(v)
            return _dot(a, expand) + (_dot(b_, expand) + _dot(c, expand))

        xs = xc[:, :d_inner]
        xs_sc[...] = xs
        xdt_t = (xs * expand_heads(dt)).T
        d_hi, d_mid, d_lo = _split3(expand_heads(dec).T)
        for g in range(M2_NGROUPS):
            r = slice(g * rows_g, (g + 1) * rows_g)
            lhs_sc[g] = jnp.concatenate([xdt_t[r].astype(BF16), d_hi[r], d_mid[r], d_lo[r]], axis=1)
            b_g = xc[:, d_inner + g * ns:d_inner + (g + 1) * ns]
            bfull_sc[g] = jnp.concatenate([b_g, jnp.zeros_like(b_g)], axis=1)
            c_g = xc[:, d_inner + (M2_NGROUPS + g) * ns:d_inner + (M2_NGROUPS + g + 1) * ns]
            ct_sc[g] = c_g.T
        yt_sc[...] = jnp.zeros_like(yt_sc)

    row_id = lax.broadcasted_iota(jnp.int32, (nb, 2 * ns), 0)
    lane_id = lax.broadcasted_iota(jnp.int32, (nb, 2 * ns), 1)
    col_id = lax.broadcasted_iota(jnp.int32, (ns, nb), 1)
    for j in range(st_ref.shape[0]):
        b = i * st_ref.shape[0] + j
        r_bot = jnp.where((row_id == b) & (lane_id >= ns), 1.0, 0.0).astype(BF16)
        for g in range(M2_NGROUPS):
            r = pl.ds(g * rows_g, rows_g)
            r_top = jnp.where(row_id == b, bfull_sc[g], 0.0).astype(BF16)
            rhs = jnp.concatenate([r_top, r_bot, r_bot, r_bot], axis=0)
            o = _dot(lhs_sc[g], rhs)
            hnew = st_ref[j, r, :] * o[:, ns:] + o[:, :ns]
            so_ref[j, r, :] = hnew
            cm = jnp.where(col_id == b, ct_sc[g], 0.0).astype(BF16)
            yt_sc[r, :] += _dot(hnew.astype(BF16), cm)

    @pl.when(i == pl.num_programs(0) - 1)
    def _():
        y = yt_sc[...].T + dvec_ref[...] * xs_sc[...]
        out_ref[...] = _rms(y * _silu(z_ref[...]), nw_ref[...]).astype(out_ref.dtype)


def ssd_sample(h, dt_raw, state, conv_state, p, *, d_s5, d_inner, nheads):
    nb = h.shape[0]
    conv_dim = d_inner + 2 * M2_NGROUPS * M2_DSTATE
    xw = 512
    xoff = d_s5 + d_inner
    assert nb == LANES and M2_DSTATE == LANES and M2_CONV == 4
    assert xoff % xw == 0 and conv_dim % xw == 0 and d_s5 % d_inner == 0 and nb % SSD_STEP_SEQS == 0
    n_xparts = conv_dim // xw
    rows = nheads * M2_HEADDIM
    rows_g = rows // M2_NGROUPS
    pad = lambda v: jnp.pad(v, (0, LANES - v.shape[0])).reshape(1, LANES)
    dvec = jnp.repeat(p['m2_d'], M2_HEADDIM).reshape(1, d_inner)
    full = lambda a, b, j=0: pl.BlockSpec((a, b), lambda i, j=j: (0, j))
    st_spec = pl.BlockSpec((SSD_STEP_SEQS, rows, M2_DSTATE), lambda i: (i, 0, 0))
    out, new_state = pl.pallas_call(
        functools.partial(_ssd_step_kernel, d_inner=d_inner, nheads=nheads, n_xparts=n_xparts),
        grid=(nb // SSD_STEP_SEQS,),
        in_specs=[full(nb, xw, xoff // xw + k) for k in range(n_xparts)]
        + [full(nb, d_inner, d_s5 // d_inner), full(nb, LANES)]
        + [full(nb, conv_dim)] * 3
        + [full(M2_CONV, conv_dim), full(1, conv_dim), full(1, LANES), full(1, LANES), full(1, d_inner),
           full(1, d_inner), st_spec],
        out_specs=[full(nb, d_inner), st_spec],
        out_shape=[jax.ShapeDtypeStruct((nb, d_inner), BF16), jax.ShapeDtypeStruct((nb, rows, M2_DSTATE), F32)],
        scratch_shapes=[pltpu.VMEM((M2_NGROUPS, rows_g, 4 * nb), BF16),
                        pltpu.VMEM((M2_NGROUPS, nb, 2 * M2_DSTATE), F32),
                        pltpu.VMEM((M2_NGROUPS, M2_DSTATE, nb), F32),
                        pltpu.VMEM((rows, nb), F32),
                        pltpu.VMEM((nb, d_inner), F32)],
        compiler_params=_cparams(("arbitrary",)),
        name="ssd_step",
    )(*([h] * n_xparts), h, dt_raw, conv_state[:, 0], conv_state[:, 1], conv_state[:, 2],
      p['m2_conv_w'], p['m2_conv_b'].reshape(1, conv_dim), pad(p['m2_dt_bias']), pad(p['m2_a_log']),
      dvec, p['m2_norm_w'].reshape(1, d_inner), state.reshape(nb, rows, M2_DSTATE))
    xbc = lax.slice_in_dim(h, xoff, xoff + conv_dim, axis=1)
    new_conv = jnp.concatenate([conv_state[:, 1:], xbc[:, None, :]], axis=1)
    return out, new_state.reshape(state.shape), new_conv


def _softmax_rows(s):
    e = jnp.exp(s - jnp.max(s, axis=-1, keepdims=True))
    return e / jnp.sum(e, axis=-1, keepdims=True)


def _attn_kernel(q_ref, k_ref, v_ref, o_ref, *, scale):
    s = _dot_nt(q_ref[...].astype(BF16), k_ref[...].astype(BF16)) * scale
    o_ref[...] = _dot(_softmax_rows(s).astype(BF16), v_ref[...].astype(BF16)).astype(o_ref.dtype)


def attention_prompt(q, k, v, nb, seq, mem, heads, *, tq):
    d = q.shape[1]
    hd = d // heads
    nq = seq // tq
    kv_spec = pl.BlockSpec((mem, hd), lambda b, h, i: (b, h))
    q_spec = pl.BlockSpec((tq, hd), lambda b, h, i: (b * nq + i, h))
    return pl.pallas_call(
        functools.partial(_attn_kernel, scale=hd ** -0.5),
        grid=(nb, heads, nq),
        in_specs=[q_spec, kv_spec, kv_spec],
        out_specs=q_spec,
        out_shape=jax.ShapeDtypeStruct(q.shape, BF16),
        compiler_params=_cparams(("parallel", "parallel", "parallel")),
        name="attention_prompt",
    )(q, k, v)


ATTN_STEP_SEQS = 4


ATTN_STEP_ROWS = 64


def _attn_step_kernel(q_ref, k_ref, v_ref, o_ref, s_sc, *, scale):
    nseq, mem, heads, hd = k_ref.shape
    ch = ATTN_STEP_ROWS
    pack = 8 // heads
    for j in range(nseq):
        q = jnp.concatenate([q_ref[j]] * pack, axis=0)

        def score(c, mx):
            rows = pl.ds(pl.multiple_of(c * ch, ch), ch)
            k = k_ref[j, rows].reshape(ch // pack, pack * heads, hd)
            s = jnp.sum(k * q, axis=-1, keepdims=True) * scale
            s_sc[pl.ds(pl.multiple_of(c * (ch // pack), ch // pack), ch // pack)] = s
            return jnp.maximum(mx, jnp.max(s, axis=0))

        mx = lax.fori_loop(0, mem // ch, score, jnp.full((pack * heads, 1), NEG, F32))
        mx1 = mx[0:heads]
        for i in range(1, pack):
            mx1 = jnp.maximum(mx1, mx[i * heads:(i + 1) * heads])
        mx = jnp.concatenate([mx1] * pack, axis=0)

        def accum(c, carry):
            den, acc = carry
            rows = pl.ds(pl.multiple_of(c * ch, ch), ch)
            v = v_ref[j, rows].reshape(ch // pack, pack * heads, hd)
            e = jnp.exp(s_sc[pl.ds(pl.multiple_of(c * (ch // pack), ch // pack), ch // pack)] - mx)
            return den + jnp.sum(e, axis=0), acc + jnp.sum(e * v, axis=0)

        den, acc = lax.fori_loop(0, mem // ch, accum,
                                 (jnp.zeros((pack * heads, 1), F32), jnp.zeros((pack * heads, hd), F32)))
        den1, acc1 = den[0:heads], acc[0:heads]
        for i in range(1, pack):
            den1 = den1 + den[i * heads:(i + 1) * heads]
            acc1 = acc1 + acc[i * heads:(i + 1) * heads]
        o_ref[j] = acc1 / den1


def attention_sample(q, k_cache, v_cache):
    b, mem, heads, hd = k_cache.shape
    nseq = ATTN_STEP_SEQS
    q_spec = pl.BlockSpec((nseq, heads, hd), lambda i: (i, 0, 0))
    kv_spec = pl.BlockSpec((nseq, mem, heads, hd), lambda i: (i, 0, 0, 0))
    out = pl.pallas_call(
        functools.partial(_attn_step_kernel, scale=hd ** -0.5),
        grid=(b // nseq,),
        in_specs=[q_spec, kv_spec, kv_spec],
        out_specs=q_spec,
        out_shape=jax.ShapeDtypeStruct((b, heads, hd), F32),
        scratch_shapes=[pltpu.VMEM((mem * heads // 8, 8, 1), F32)],
        compiler_params=_cparams(("parallel",)),
        name="attention_step",
    )(q.reshape(b, heads, hd), k_cache, v_cache)
    return out.reshape(b, heads * hd)


NEG = -1e30


INFO_G1, INFO_G2, INFO_E1, INFO_E2 = 0, 1, 2, 3


def _lane_pack(lane, values):
    out = 0.0
    for k, v in values:
        out = jnp.where(lane == k, v, out)
    return out


def _router_kernel(x_ref, nw_ref, wr_ref, br_ref, sel_ref, info_ref, *, n_experts):
    xn = _rms(x_ref[...], nw_ref[...])
    x_hi, x_lo = _split2(xn)
    w_hi, w_lo = _split2(wr_ref[...])
    hi_terms = _dot(x_hi, jnp.concatenate([w_hi, w_lo], axis=1))
    logits = hi_terms[:, :LANES] + (hi_terms[:, LANES:] + _dot(x_lo, w_hi)) + br_ref[...]
    lane = lax.broadcasted_iota(jnp.int32, logits.shape, 1)
    big = jnp.int32(2 ** 30)
    is_c = (lane >= n_experts) & (lane < n_experts + MOE_GROUPS)
    lc = jnp.where(is_c, logits, NEG)
    cmax = jnp.max(lc, axis=-1, keepdims=True)
    gsel = jnp.min(jnp.where(lc == cmax, lane, big), axis=-1, keepdims=True) - n_experts
    gate_c = 1.0 / jnp.sum(jnp.where(is_c, jnp.exp(lc - cmax), 0.0), axis=-1, keepdims=True)
    in_group = (lane < n_experts) & (lane // MOE_PER_GROUP == gsel)
    lf = jnp.where(in_group, logits, NEG)
    t1 = jnp.max(lf, axis=-1, keepdims=True)
    i1 = jnp.min(jnp.where(lf == t1, lane, big), axis=-1, keepdims=True)
    lf2 = jnp.where(lane == i1, NEG, lf)
    t2 = jnp.max(lf2, axis=-1, keepdims=True)
    i2 = jnp.min(jnp.where(lf2 == t2, lane, big), axis=-1, keepdims=True)
    r = jnp.exp(t2 - t1)
    g1 = gate_c / (1.0 + r)
    g2 = gate_c * r / (1.0 + r)
    sel_ref[...] = jnp.where((lane == i1) | (lane == i2), 1.0, 0.0)
    info_ref[...] = _lane_pack(lane, ((INFO_G1, g1), (INFO_G2, g2), (INFO_E1, i1.astype(F32)),
                                      (INFO_E2, i2.astype(F32))))


def moe_router(x, norm_w, w_coarse, b_coarse, w_fine, b_fine, *, tm):
    m, d = x.shape
    e = w_fine.shape[1]
    padw = LANES - e - MOE_GROUPS
    wr = jnp.concatenate([w_fine, w_coarse, jnp.zeros((d, padw), F32)], axis=1)
    br = jnp.concatenate([b_fine, b_coarse, jnp.zeros((padw,), F32)]).reshape(1, LANES)
    tok = pl.BlockSpec((tm, LANES), lambda i: (i, 0))
    return pl.pallas_call(
        functools.partial(_router_kernel, n_experts=e),
        grid=(m // tm,),
        in_specs=[pl.BlockSpec((tm, d), lambda i: (i, 0)), pl.BlockSpec((1, d), lambda i: (0, 0)),
                  pl.BlockSpec((d, LANES), lambda i: (0, 0)), pl.BlockSpec((1, LANES), lambda i: (0, 0))],
        out_specs=[tok, tok],
        out_shape=[jax.ShapeDtypeStruct((m, LANES), F32), jax.ShapeDtypeStruct((m, LANES), F32)],
        compiler_params=_cparams(("parallel",)),
        name="moe_router",
    )(x, norm_w.reshape(1, d), wr, br)


def _moe_rank_kernel(sel_ref, info_ref, rank_ref, counts_ref, carry_sc):
    i = pl.program_id(0)

    @pl.when(i == 0)
    def _():
        carry_sc[...] = jnp.zeros_like(carry_sc)

    sel = sel_ref[...]
    tm = sel.shape[0]
    row = lax.broadcasted_iota(jnp.int32, (tm, tm), 0)
    col = lax.broadcasted_iota(jnp.int32, (tm, tm), 1)
    before = jnp.where(row > col, 1.0, 0.0).astype(BF16)
    rank = _dot(before, sel.astype(BF16)) + carry_sc[...]
    info = info_ref[...]
    lane = lax.broadcasted_iota(jnp.int32, sel.shape, 1)
    e1 = info[:, INFO_E1:INFO_E1 + 1].astype(jnp.int32)
    e2 = info[:, INFO_E2:INFO_E2 + 1].astype(jnp.int32)
    r1 = jnp.sum(jnp.where(lane == e1, rank, 0.0), axis=-1, keepdims=True)
    r2 = jnp.sum(jnp.where(lane == e2, rank, 0.0), axis=-1, keepdims=True)
    rank_ref[...] = _lane_pack(lane, ((0, r1), (1, r2)))
    carry_sc[...] += jnp.sum(sel, axis=0, keepdims=True)

    @pl.when(i == pl.num_programs(0) - 1)
    def _():
        counts_ref[...] = carry_sc[...]


def _moe_plan_kernel(rank_ref, info_ref, counts_ref, dest_ref, tiles_ref, *, tile, n_experts):
    counts = counts_ref[...]
    ntile_e = jnp.floor((counts + (tile - 1)) * (1.0 / tile))
    padded = jnp.broadcast_to(ntile_e * tile, (8, LANES))
    r = lax.broadcasted_iota(jnp.int32, (LANES, LANES), 0)
    c = lax.broadcasted_iota(jnp.int32, (LANES, LANES), 1)
    lower = jnp.where(r < c, 1.0, 0.0).astype(BF16)
    p_hi, p_mid, p_lo = _split3(padded)
    offs = (_dot(p_hi, lower) + (_dot(p_mid, lower) + _dot(p_lo, lower)))[0:1, :]
    ends = offs + padded[0:1, :]

    info = info_ref[...]
    rank = rank_ref[...]
    lane = lax.broadcasted_iota(jnp.int32, info.shape, 1)
    e1 = info[:, INFO_E1:INFO_E1 + 1].astype(jnp.int32)
    e2 = info[:, INFO_E2:INFO_E2 + 1].astype(jnp.int32)
    d1 = jnp.sum(jnp.where(lane == e1, offs, 0.0), axis=-1, keepdims=True) + rank[:, 0:1]
    d2 = jnp.sum(jnp.where(lane == e2, offs, 0.0), axis=-1, keepdims=True) + rank[:, 1:2]
    dest_ref[...] = _lane_pack(lane, ((0, d1), (1, d2))).astype(jnp.int32)

    ends_col = jnp.broadcast_to(ends, (LANES, LANES)).T
    start = (c * tile).astype(F32)
    n_before = jnp.sum(jnp.where((ends_col <= start) & (r < n_experts), 1.0, 0.0), axis=0, keepdims=True)
    lane1 = lax.broadcasted_iota(jnp.int32, (1, LANES), 1)
    total = jnp.sum(jnp.where(lane1 == n_experts - 1, ends, 0.0), axis=-1, keepdims=True)
    pad_start = offs + counts
    clear_from = jnp.floor(pad_start * (1.0 / MOE_CLEAR_ROWS)) * MOE_CLEAR_ROWS
    tail = jnp.where(pad_start < ends, clear_from, -1.0)
    tiles_ref[...] = jnp.zeros_like(tiles_ref)
    tiles_ref[0:1, :] = jnp.minimum(n_before, n_experts - 1.0).astype(jnp.int32)
    tiles_ref[1:2, :] = jnp.broadcast_to(total * (1.0 / tile), (1, LANES)).astype(jnp.int32)
    tiles_ref[2:3, :] = tail.astype(jnp.int32)
    tiles_ref[3:4, :] = ends.astype(jnp.int32)


def _moe_dispatch_kernel(dest_ref, tails_ref, ends_ref, ntiles_ref, *rest, tile, n_experts, group_steps,
                         group_offsets):
    TAIL_SEM, FILL_SEM, ROW_SEM = 0, 1, 2
    n_groups = len(group_offsets)
    nw_ref = rest[0]
    x_refs = rest[1:1 + n_groups]
    xs_ref, zero_sc, xn_sc, sems = rest[1 + n_groups:]
    i = pl.program_id(0)
    n_tiles_max = xs_ref.shape[0] // tile

    def fill_copy(j):
        return pltpu.make_async_copy(zero_sc, xs_ref.at[pl.ds(pl.multiple_of(j * tile, tile), tile)],
                                     sems.at[FILL_SEM])

    def tail_pieces(e, act):
        def piece(j, _):
            row = pl.multiple_of(tails_ref[e] + j * MOE_CLEAR_ROWS, MOE_CLEAR_ROWS)
            act(pltpu.make_async_copy(zero_sc.at[pl.ds(0, MOE_CLEAR_ROWS)], xs_ref.at[pl.ds(row, MOE_CLEAR_ROWS)],
                                      sems.at[TAIL_SEM]))
            return 0

        lax.fori_loop(0, (ends_ref[e] - tails_ref[e]) // MOE_CLEAR_ROWS, piece, 0)

    @pl.when(i == 0)
    def _():
        zero_sc[...] = jnp.zeros_like(zero_sc)

        def fill(j, _):
            fill_copy(j).start()
            return 0

        def clear(e, _):
            @pl.when(tails_ref[e] >= 0)
            def _():
                tail_pieces(e, lambda c: c.start())
            return 0

        def clear_wait(e, _):
            @pl.when(tails_ref[e] >= 0)
            def _():
                tail_pieces(e, lambda c: c.wait())
            return 0

        lax.fori_loop(0, n_experts, clear, 0)
        lax.fori_loop(0, n_experts, clear_wait, 0)
        lax.fori_loop(ntiles_ref[0], n_tiles_max, fill, 0)

    slot = i % 2

    def wait_rows(which, tmw):
        for _ in range(2):
            pltpu.make_async_copy(xn_sc.at[which, pl.ds(0, tmw)], xs_ref.at[pl.ds(0, tmw)],
                                  sems.at[ROW_SEM + which]).wait()

    for g, x_ref in enumerate(x_refs):
        @pl.when((i >= group_steps[g]) & (i < group_steps[g + 1]))
        def _(g=g, x_ref=x_ref):
            tm = x_ref.shape[0]
            xn_ref = xn_sc.at[slot, pl.ds(0, tm)]
            xn_ref[...] = _rms(x_ref[...], nw_ref[...])
            base = 2 * (group_offsets[g] + (i - group_steps[g]) * tm)
            for r in range(tm):
                for k in range(2):
                    pltpu.make_async_copy(xn_ref.at[pl.ds(r, 1)], xs_ref.at[pl.ds(dest_ref[base + 2 * r + k], 1)],
                                          sems.at[ROW_SEM + slot]).start(priority=k)

            @pl.when(i > group_steps[g])
            def _():
                wait_rows(1 - slot, tm)

            if g > 0:
                @pl.when(i == group_steps[g])
                def _():
                    wait_rows(1 - slot, x_refs[g - 1].shape[0])

            @pl.when(i == pl.num_programs(0) - 1)
            def _():
                wait_rows(slot, tm)

    @pl.when(i == pl.num_programs(0) - 1)
    def _():
        def fill_wait(j, _):
            fill_copy(j).wait()
            return 0

        lax.fori_loop(ntiles_ref[0], n_tiles_max, fill_wait, 0)


def _moe_expert_kernel(texp_ref, ntiles_ref, xs_ref, wg_hbm, wu_hbm, wd_hbm, ys_ref,
                       wg_sc, wu_sc, wd_sc, slot_sm, sems):
    i = pl.program_id(0)
    n = ntiles_ref[0]

    def weight_copies(e, slot):
        return [pltpu.make_async_copy(w.at[e], sc.at[slot], sems.at[slot])
                for w, sc in ((wg_hbm, wg_sc), (wu_hbm, wu_sc), (wd_hbm, wd_sc))]

    @pl.when(i < n)
    def _():
        e = texp_ref[i]
        first = (i == 0) | (texp_ref[jnp.maximum(i - 1, 0)] != e)

        @pl.when(i == 0)
        def _():
            slot_sm[0] = 0
            for c in weight_copies(e, 0):
                c.start()

        @pl.when(first & (i > 0))
        def _():
            slot_sm[0] = 1 - slot_sm[0]

        slot = slot_sm[0]

        @pl.when(first)
        def _():
            for c in weight_copies(e, slot):
                c.wait()
            j = lax.while_loop(lambda j: (j < n) & (texp_ref[jnp.minimum(j, n - 1)] == e), lambda j: j + 1, i + 1)

            @pl.when(j < n)
            def _():
                for c in weight_copies(texp_ref[jnp.minimum(j, n - 1)], 1 - slot):
                    c.start()

        x = xs_ref[...].astype(BF16)
        hg = _dot(x, wg_sc[slot].astype(BF16))
        hu = _dot(x, wu_sc[slot].astype(BF16))
        ys_ref[...] = _dot((_silu(hg) * hu).astype(BF16), wd_sc[slot].astype(BF16))


def _moe_combine_kernel(dest_ref, ys_ref, x_ref, info_ref, fw_ref, y_ref, buf_sc, sems):
    s = pl.program_id(0)
    n_blocks = pl.num_programs(0) - 1
    tm = x_ref.shape[0]
    slot = s % 2

    @pl.when(s < n_blocks)
    def _():
        for r in range(tm):
            for k in range(2):
                pltpu.make_async_copy(ys_ref.at[pl.ds(dest_ref[2 * (s * tm + r) + k], 1)],
                                      buf_sc.at[slot, k, pl.ds(r, 1)], sems.at[slot]).start(priority=k)

    @pl.when(s > 0)
    def _():
        prev = 1 - slot
        for k in range(2):
            pltpu.make_async_copy(ys_ref.at[pl.ds(0, tm)], buf_sc.at[prev, k], sems.at[prev]).wait()
        info = info_ref[...]
        y = (x_ref[...] + info[:, INFO_G1:INFO_G1 + 1] * buf_sc[prev, 0]
             + info[:, INFO_G2:INFO_G2 + 1] * buf_sc[prev, 1])
        y_ref[...] = _rms(y, fw_ref[...])


MOE_TILE = 256
MOE_CLEAR_ROWS = 32
MOE_TOKENS_PER_STEP = 256


def _largest_tile(m, cap):
    return max(t for t in range(8, cap + 1, 8) if m % t == 0)


def moe_routed_final(x_list, p, final_w):
    d = x_list[0].shape[1]
    ne, _, f = p['moe_w_gate'].shape
    sizes = [x.shape[0] for x in x_list]
    m = sum(sizes)
    tile = min(MOE_TILE, m)
    routed = [moe_router(x, p['norm_ffn_w'], p['router_coarse_w'], p['router_coarse_b'],
                         p['router_fine_w'], p['router_fine_b'], tm=min(x.shape[0], 512)) for x in x_list]
    sel = jnp.concatenate([r[0] for r in routed], axis=0)
    info = jnp.concatenate([r[1] for r in routed], axis=0)
    tm = _largest_tile(m, 1024)
    tok = pl.BlockSpec((tm, LANES), lambda i: (i, 0))
    rank, counts = pl.pallas_call(
        _moe_rank_kernel,
        grid=(m // tm,),
        in_specs=[tok, tok],
        out_specs=[tok, pl.BlockSpec((1, LANES), lambda i: (0, 0))],
        out_shape=[jax.ShapeDtypeStruct((m, LANES), F32), jax.ShapeDtypeStruct((1, LANES), F32)],
        scratch_shapes=[pltpu.VMEM((1, LANES), F32)],
        compiler_params=_cparams(("arbitrary",)),
        name="moe_rank",
    )(sel, info)
    n_tiles_max = (2 * m) // tile + ne
    assert n_tiles_max <= LANES
    dest, tiles = pl.pallas_call(
        functools.partial(_moe_plan_kernel, tile=tile, n_experts=ne),
        out_shape=[jax.ShapeDtypeStruct((m, LANES), jnp.int32), jax.ShapeDtypeStruct((8, LANES), jnp.int32)],
        compiler_params=pltpu.CompilerParams(vmem_limit_bytes=VMEM_LIMIT),
        name="moe_plan",
    )(rank, info, counts)
    dest_flat = dest[:, :2].reshape(2 * m)
    tile_expert = tiles[0, :n_tiles_max]
    n_tiles = tiles[1, :1]
    tails = tiles[2, :ne]
    seg_ends = tiles[3, :ne]

    rows = n_tiles_max * tile

    tds = [min(mg, 4 * MOE_TOKENS_PER_STEP) for mg in sizes]
    group_steps = [0]
    for mg, td in zip(sizes, tds):
        group_steps.append(group_steps[-1] + mg // td)
    group_offsets = [sum(sizes[:g]) for g in range(len(sizes))]

    def group_spec(g):
        first, last = group_steps[g], group_steps[g + 1] - 1
        return pl.BlockSpec((tds[g], d), lambda i, *_: (jnp.clip(i, first, last) - first, 0))

    xs = pl.pallas_call(
        functools.partial(_moe_dispatch_kernel, tile=tile, n_experts=ne, group_steps=tuple(group_steps),
                          group_offsets=tuple(group_offsets)),
        grid_spec=pltpu.PrefetchScalarGridSpec(
            num_scalar_prefetch=4, grid=(group_steps[-1],),
            in_specs=[pl.BlockSpec((1, d), lambda i, *_: (0, 0))] + [group_spec(g) for g in range(len(sizes))],
            out_specs=pl.BlockSpec(memory_space=pl.ANY),
            scratch_shapes=[pltpu.VMEM((tile, d), F32), pltpu.VMEM((2, max(tds), d), F32),
                            pltpu.SemaphoreType.DMA((4,))]),
        out_shape=jax.ShapeDtypeStruct((rows, d), F32),
        compiler_params=_cparams(("arbitrary",)),
        name="moe_dispatch",
    )(dest_flat, tails, seg_ends, n_tiles, p['norm_ffn_w'].reshape(1, d), *x_list)

    def tile_idx(i, te, nt):
        return jnp.minimum(i, nt[0] - 1)

    ys = pl.pallas_call(
        _moe_expert_kernel,
        grid_spec=pltpu.PrefetchScalarGridSpec(
            num_scalar_prefetch=2, grid=(n_tiles_max,),
            in_specs=[pl.BlockSpec((tile, d), lambda i, te, nt: (tile_idx(i, te, nt), 0)),
                      pl.BlockSpec(memory_space=pl.ANY), pl.BlockSpec(memory_space=pl.ANY),
                      pl.BlockSpec(memory_space=pl.ANY)],
            out_specs=pl.BlockSpec((tile, d), lambda i, te, nt: (tile_idx(i, te, nt), 0)),
            scratch_shapes=[pltpu.VMEM((2, d, f), F32), pltpu.VMEM((2, d, f), F32), pltpu.VMEM((2, f, d), F32),
                            pltpu.SMEM((1,), jnp.int32), pltpu.SemaphoreType.DMA((2,))]),
        out_shape=jax.ShapeDtypeStruct((rows, d), F32),
        input_output_aliases={2: 0},
        compiler_params=_cparams(("arbitrary",)),
        name="moe_experts",
    )(tile_expert, n_tiles, xs, p['moe_w_gate'], p['moe_w_up'], p['moe_w_down'])

    outs = []
    off = 0
    for x, (_, info_g), mg in zip(x_list, routed, sizes):
        tc = min(mg, MOE_TOKENS_PER_STEP)
        outs.append(pl.pallas_call(
            _moe_combine_kernel,
            grid_spec=pltpu.PrefetchScalarGridSpec(
                num_scalar_prefetch=1, grid=(mg // tc + 1,),
                in_specs=[pl.BlockSpec(memory_space=pl.ANY),
                          pl.BlockSpec((tc, d), lambda s, dref: (jnp.maximum(s - 1, 0), 0)),
                          pl.BlockSpec((tc, LANES), lambda s, dref: (jnp.maximum(s - 1, 0), 0)),
                          pl.BlockSpec((1, d), lambda s, dref: (0, 0))],
                out_specs=pl.BlockSpec((tc, d), lambda s, dref: (jnp.maximum(s - 1, 0), 0)),
                scratch_shapes=[pltpu.VMEM((2, 2, tc, d), F32), pltpu.SemaphoreType.DMA((2,))]),
            out_shape=jax.ShapeDtypeStruct((mg, d), F32),
            compiler_params=_cparams(("arbitrary",)),
            name="moe_combine",
        )(dest_flat[2 * off:2 * (off + mg)], ys, x, info_g, final_w.reshape(1, d)))
        off += mg
    return outs


def s5_prompt(h, nb, seq, prep, p):
    y, hfin = s5_scan(h, prep, nb, seq)
    out = s5_head(y, h, p['s5_d'], p['s5_glu_w'], p['s5_glu_b'], p['s5_norm_w'], tm=min(1024, nb * seq))
    return out, hfin[:, 0].transpose(1, 0, 2), hfin[:, 1].transpose(1, 0, 2)


def s5_sample(h, st_re, st_im, prep, p):
    y, n_re, n_im = s5_step(h, st_re, st_im, prep, p['s5_c_re'], p['s5_c_im'])
    out = s5_head(y, h, p['s5_d'], p['s5_glu_w'], p['s5_glu_b'], p['s5_norm_w'], tm=h.shape[0])
    return out, n_re, n_im


def _row_tile(m):
    return min(m, 1024)


def _col_tile(m, n):
    if m <= 256:
        return n
    return max(t for t in range(256, 1793, 256) if n % t == 0)


def _mixer_and_attention(x, p, s5_prep, *, nb, seq, mem_kv, xa_heads, states):
    m, d = x.shape
    g, n = p['s5_a_re'].shape
    d_s5 = g * S5_CH
    nheads = p['m2_a_log'].shape[0]
    d_inner = nheads * M2_HEADDIM
    conv_dim = d_inner + 2 * M2_NGROUPS * M2_DSTATE
    n_main = d_s5 + d_inner + conv_dim
    tm = _row_tile(m)

    w_in = p['w_in']
    w_dt = jnp.pad(w_in[:, n_main:], ((0, 0), (0, LANES - nheads)))
    tm2, tn2 = tm, _col_tile(m, d)
    h, dt_raw = fused_matmul([x], w_in, n_out=n_main, gain=p['norm_mix_w'], side_w=w_dt, tm=tm,
                             tn=_col_tile(m, n_main))

    if states is None:
        s5_out, s5_re, s5_im = s5_prompt(h, nb, seq, s5_prep, p)
        m2_out, ssm, conv = ssd_prompt(h, dt_raw, nb, seq, p, d_s5=d_s5, d_inner=d_inner, nheads=nheads)
    else:
        s5_out, s5_re, s5_im = s5_sample(h, states[0], states[1], s5_prep, p)
        m2_out, ssm, conv = ssd_sample(h, dt_raw, states[2], states[3], p, d_s5=d_s5, d_inner=d_inner,
                                       nheads=nheads)
    x1 = fused_matmul([s5_out, m2_out], p['w_out'], n_out=d, res=x, tm=tm2, tn=tn2)

    q = fused_matmul([x1], p['xa_wq'], n_out=d, gain=p['norm_xa_w'], tm=tm2, tn=tn2,
                     out_dtype=BF16 if states is None else F32)
    if states is None:
        mem = mem_kv[0].shape[0] // nb
        o = attention_prompt(q, mem_kv[0], mem_kv[1], nb, seq, mem, xa_heads, tq=min(seq, 2048))
    else:
        o = attention_sample(q, mem_kv[0], mem_kv[1])
    x2 = fused_matmul([o], p['xa_wo'], n_out=d, res=x1, tm=tm2, tn=tn2)

    return x2, s5_re, s5_im, ssm, conv


def kernel(x_prompt, x_sample, mem_prompt, state_s5_re, state_s5_im, state_ssm, state_conv, cache_mem_k, cache_mem_v, norm_mix_w, w_in, s5_a_re, s5_a_im, s5_log_dt, s5_b_re, s5_b_im, s5_c_re, s5_c_im, s5_d, s5_glu_w, s5_glu_b, s5_norm_w, m2_conv_w, m2_conv_b, m2_dt_bias, m2_a_log, m2_d, m2_norm_w, w_out, norm_xa_w, norm_mem_w, xa_wq, xa_wk, xa_wv, xa_wo, norm_ffn_w, router_coarse_w, router_coarse_b, router_fine_w, router_fine_b, moe_w_gate, moe_w_up, moe_w_down, norm_final_w):
    depth = w_in.shape[0]
    assert depth == 1, "the final norm is fused into the (only) layer"
    per_layer = dict(
        norm_mix_w=norm_mix_w, w_in=w_in, s5_a_re=s5_a_re, s5_a_im=s5_a_im, s5_log_dt=s5_log_dt,
        s5_b_re=s5_b_re, s5_b_im=s5_b_im, s5_c_re=s5_c_re, s5_c_im=s5_c_im, s5_d=s5_d, s5_glu_w=s5_glu_w,
        s5_glu_b=s5_glu_b, s5_norm_w=s5_norm_w, m2_conv_w=m2_conv_w, m2_conv_b=m2_conv_b, m2_dt_bias=m2_dt_bias,
        m2_a_log=m2_a_log, m2_d=m2_d, m2_norm_w=m2_norm_w, w_out=w_out, norm_xa_w=norm_xa_w,
        norm_mem_w=norm_mem_w, xa_wq=xa_wq, xa_wk=xa_wk, xa_wv=xa_wv, xa_wo=xa_wo, norm_ffn_w=norm_ffn_w,
        router_coarse_w=router_coarse_w, router_coarse_b=router_coarse_b, router_fine_w=router_fine_w,
        router_fine_b=router_fine_b, moe_w_gate=moe_w_gate, moe_w_up=moe_w_up, moe_w_down=moe_w_down)
    p = {k: v[0] for k, v in per_layer.items()}
    for name in ('w_in', 'w_out', 'xa_wq', 'xa_wo'):
        p[name] = p[name].astype(BF16)
    nb, seq, d = x_prompt.shape
    db, dseq, _ = x_sample.shape
    assert dseq == 1
    mem = mem_prompt.shape[1]
    xa_heads = cache_mem_k.shape[3]

    s5_prep = s5_prepare(p['s5_a_re'], p['s5_a_im'], p['s5_log_dt'], p['s5_b_re'], p['s5_b_im'],
                         p['s5_c_re'], p['s5_c_im'])

    memx = mem_prompt.reshape(nb * mem, d)
    mk, mv = memory_kv(memx, p['norm_mem_w'], p['xa_wk'], p['xa_wv'], tm=_row_tile(nb * mem), tn=512)
    xp, p_re, p_im, p_ssm, p_conv = _mixer_and_attention(
        x_prompt.reshape(nb * seq, d), p, s5_prep, nb=nb, seq=seq, mem_kv=(mk, mv), xa_heads=xa_heads, states=None)

    xs, s_re, s_im, s_ssm, s_conv = _mixer_and_attention(
        x_sample.reshape(db, d), p, s5_prep, nb=db, seq=1,
        mem_kv=(cache_mem_k[0], cache_mem_v[0]), xa_heads=xa_heads,
        states=(state_s5_re[0], state_s5_im[0], state_ssm[0], state_conv[0]))

    yp, ys = moe_routed_final([xp, xs], p, norm_final_w)

    kv_shape = (1, nb, mem) + cache_mem_k.shape[3:]
    return (yp.reshape(nb, seq, d), ys.reshape(db, 1, d), p_re[None], p_im[None], p_ssm[None], p_conv[None],
            mk.reshape(kv_shape), mv.reshape(kv_shape), s_re[None], s_im[None], s_ssm[None], s_conv[None])
```
